```python
import math
import jax, jax.numpy as jnp
from jax import lax
import numpy as np

D_MODEL = 2048
BATCH = 2
SEQ = 4096
DEPTH = 1

HEAD_DIM = 64
N_HEADS_DIL = 12
DIL_PATTERNS = ((128, 1), (512, 4), (2048, 16))
DIL_BLOCK = 128
N_HEADS_GLA = 4
GLA_DK = 128
GLA_DV = 256
GLA_GATE_RANK = 16
GLA_TAU = 16.0
GLA_CHUNK = 64
N_HEADS_MEM = 4
MEM_LEN = 256
REL_BUCKETS = 32
REL_MAX_DIST = 2048
EPS = 1e-6
NEG_INF = -1e30

D_DIL = N_HEADS_DIL * HEAD_DIM
D_GLA_K = N_HEADS_GLA * GLA_DK
D_GLA_V = N_HEADS_GLA * GLA_DV
D_MEM = N_HEADS_MEM * HEAD_DIM
D_MIX = D_DIL + D_GLA_V + D_MEM
IN_SIZES = (D_DIL, D_DIL, D_DIL,
            D_GLA_K, D_GLA_K, D_GLA_V,
            GLA_GATE_RANK,
            D_MEM,
            D_MIX)
D_IN_PROJ = sum(IN_SIZES)
IN_SPLIT_IDX = tuple(int(i) for i in np.cumsum(IN_SIZES)[:-1])

kernel_name = "hybrid_dilated_gla_memory_block"


def rmsnorm(x, w):
    xf = x.astype(jnp.float32)
    y = xf * lax.rsqrt(jnp.mean(xf * xf, axis=-1, keepdims=True) + EPS)
    return (y * w.astype(jnp.float32)).astype(x.dtype)


def t5_bucket(dist):
    max_exact = REL_BUCKETS // 2
    n = jnp.maximum(dist, 1).astype(jnp.float32)
    large = max_exact + (jnp.log(n / max_exact) / math.log(REL_MAX_DIST / max_exact)
                         * (REL_BUCKETS - max_exact)).astype(jnp.int32)
    large = jnp.minimum(large, REL_BUCKETS - 1)
    return jnp.where(dist < max_exact, dist, large)


def dilated_attention(q, k, v, rel_bias):
    B, S, H, Dh = q.shape
    BLK = DIL_BLOCK
    i = jnp.arange(BLK)[:, None]
    m = jnp.arange(2 * BLK)[None, :]
    rel = i - m + BLK
    outs, lses = [], []
    for window, dil in DIL_PATTERNS:
        back = window // dil
        assert back <= BLK
        L = S // dil
        nb = -(-L // BLK)
        Lp = nb * BLK

        def to_sub(t):
            return t.reshape(B, L, dil, H, Dh).transpose(0, 2, 3, 1, 4)

        qs, ks, vs = to_sub(q), to_sub(k), to_sub(v)
        qb = jnp.pad(qs, ((0, 0),) * 3 + ((0, Lp - L), (0, 0))).reshape(B, dil, H, nb, BLK, Dh)

        def band(t):
            tp = jnp.pad(t, ((0, 0),) * 3 + ((BLK, Lp - L), (0, 0)))
            prev = tp[..., :Lp, :].reshape(B, dil, H, nb, BLK, Dh)
            cur = tp[..., BLK:, :].reshape(B, dil, H, nb, BLK, Dh)
            return jnp.concatenate([prev, cur], axis=-2)

        kb, vb = band(ks), band(vs)
        key_idx = jnp.arange(nb)[:, None, None] * BLK - BLK + m[None]
        mask = (rel >= 0)[None] & (rel <= back)[None] & (key_idx >= 0)
        bias = rel_bias.astype(jnp.float32)[t5_bucket(jnp.clip(rel, 0, back) * dil)]
        bias = bias.transpose(2, 0, 1)
        logits = jnp.einsum('bghnqc,bghnkc->bghnqk', qb.astype(jnp.float32), kb.astype(jnp.float32))
        logits = logits + bias[None, None, :, None]
        logits = jnp.where(mask[None, None, None], logits, NEG_INF)
        mx = jnp.max(logits, axis=-1, keepdims=True)
        p = jnp.exp(logits - mx)
        den = jnp.sum(p, axis=-1, keepdims=True)
        o = jnp.einsum('bghnqk,bghnkc->bghnqc', p, vb.astype(jnp.float32)) / den
        lse = (mx + jnp.log(den))[..., 0]
        o = o.reshape(B, dil, H, Lp, Dh)[..., :L, :].transpose(0, 3, 1, 2, 4).reshape(B, S, H, Dh)
        lse = lse.reshape(B, dil, H, Lp)[..., :L].transpose(0, 3, 1, 2).reshape(B, S, H)
        outs.append(o)
        lses.append(lse)
    w = jax.nn.softmax(jnp.stack(lses, axis=0), axis=0)
    out = jnp.sum(w[..., None] * jnp.stack(outs, axis=0), axis=0)
    return out.astype(q.dtype)


def gla_attention(q, k, v, log_alpha, norm_w):
    B, S, H, dk = q.shape
    dv = v.shape[-1]
    C = GLA_CHUNK
    N = S // C
    f32 = lambda t: t.astype(jnp.float32).reshape(B, N, C, H, t.shape[-1])
    qc, kc, vc, la = f32(q), f32(k), f32(v), f32(log_alpha)
    b = jnp.cumsum(la, axis=2)
    b_last = b[:, :, -1:]
    q_t = qc * jnp.exp(b)
    k_t = kc * jnp.exp(-b)
    k_s = kc * jnp.exp(b_last - b)
    causal = jnp.tril(jnp.ones((C, C), dtype=bool))
    A = jnp.einsum('bnqhc,bnkhc->bnhqk', q_t, k_t)
    A = jnp.where(causal, A, 0.0)
    o_intra = jnp.einsum('bnhqk,bnkhv->bnqhv', A, vc)
    kv = jnp.einsum('bnkhc,bnkhv->bnhcv', k_s, vc)
    decay = jnp.exp(b_last[:, :, 0])

    def step(state, inp):
        dec, kv_n = inp
        return dec[..., None] * state + kv_n, state

    _, states = lax.scan(step, jnp.zeros((B, H, dk, dv), jnp.float32),
                         (jnp.moveaxis(decay, 1, 0), jnp.moveaxis(kv, 1, 0)))
    states = jnp.moveaxis(states, 0, 1)
    o_inter = jnp.einsum('bnqhc,bnhcv->bnqhv', q_t, states)
    o = (o_intra + o_inter).reshape(B, S, H, dv)
    return rmsnorm(o, norm_w).astype(q.dtype)


def memory_attention(q, mem_n, w_mem_kv):
    B, M, _ = mem_n.shape
    kv = mem_n @ w_mem_kv
    mk, mv = jnp.split(kv, 2, axis=-1)
    mk = mk.reshape(B, M, N_HEADS_MEM, HEAD_DIM)
    mv = mv.reshape(B, M, N_HEADS_MEM, HEAD_DIM)
    logits = jnp.einsum('bshc,bmhc->bhsm', q.astype(jnp.float32), mk.astype(jnp.float32))
    p = jax.nn.softmax(logits, axis=-1)
    return jnp.einsum('bhsm,bmhc->bshc', p, mv.astype(jnp.float32)).astype(q.dtype)


def setup_inputs(seed: int = 0) -> dict:
    key = jax.random.key(seed)
    ks = jax.random.split(key, 12)
    nrm = lambda k, shape, s: jax.random.normal(k, shape, jnp.float32) * s
    return {
        "x": nrm(ks[0], (BATCH, SEQ, D_MODEL), 1.0),
        "mem": nrm(ks[1], (BATCH, MEM_LEN, D_MODEL), 1.0),
        "norm_pre_w": 1.0 + nrm(ks[2], (DEPTH, D_MODEL), 0.02),
        "w_in": nrm(ks[3], (DEPTH, D_MODEL, D_IN_PROJ), D_MODEL ** -0.5),
        "rel_bias": nrm(ks[4], (REL_BUCKETS, N_HEADS_DIL), 0.1),
        "w_gla_gate2": nrm(ks[5], (DEPTH, GLA_GATE_RANK, D_GLA_K), GLA_GATE_RANK ** -0.5),
        "b_gla_gate": nrm(ks[6], (DEPTH, D_GLA_K), 0.1),
        "gla_norm_w": 1.0 + nrm(ks[7], (DEPTH, GLA_DV), 0.02),
        "mem_norm_w": 1.0 + nrm(ks[8], (DEPTH, D_MODEL), 0.02),
        "w_mem_kv": nrm(ks[9], (DEPTH, D_MODEL, 2 * D_MEM), D_MODEL ** -0.5),
        "w_out": nrm(ks[10], (DEPTH, D_MIX, D_MODEL), D_MIX ** -0.5),
        "norm_post_w": 1.0 + nrm(ks[11], (DEPTH, D_MODEL), 0.02),
    }


def reference(x, mem, norm_pre_w, w_in, rel_bias, w_gla_gate2, b_gla_gate, gla_norm_w,
              mem_norm_w, w_mem_kv, w_out, norm_post_w):
    B, S, _ = x.shape
    for l in range(DEPTH):
        h = rmsnorm(x, norm_pre_w[l])
        proj = h @ w_in[l]
        dq, dk, dv, gq, gk, gv, g_lr, mq, gate = jnp.split(proj, IN_SPLIT_IDX, axis=-1)
        a = dilated_attention(dq.reshape(B, S, N_HEADS_DIL, HEAD_DIM) * (HEAD_DIM ** -0.5),
                              dk.reshape(B, S, N_HEADS_DIL, HEAD_DIM),
                              dv.reshape(B, S, N_HEADS_DIL, HEAD_DIM), rel_bias)
        log_alpha = jax.nn.log_sigmoid((g_lr @ w_gla_gate2[l] + b_gla_gate[l]).astype(jnp.float32)) / GLA_TAU
        o_gla = gla_attention(gq.reshape(B, S, N_HEADS_GLA, GLA_DK) * (GLA_DK ** -0.5),
                              gk.reshape(B, S, N_HEADS_GLA, GLA_DK),
                              gv.reshape(B, S, N_HEADS_GLA, GLA_DV),
                              log_alpha.reshape(B, S, N_HEADS_GLA, GLA_DK), gla_norm_w[l])
        o_mem = memory_attention(mq.reshape(B, S, N_HEADS_MEM, HEAD_DIM) * (HEAD_DIM ** -0.5),
                                 rmsnorm(mem, mem_norm_w[l]), w_mem_kv[l])
        mix = jnp.concatenate([a.reshape(B, S, D_DIL), o_gla.reshape(B, S, D_GLA_V),
                               o_mem.reshape(B, S, D_MEM)], axis=-1)
        y = (mix * jax.nn.silu(gate)) @ w_out[l]
        x = x + rmsnorm(y, norm_post_w[l])
    return x
```

```python
import functools
import math

import numpy as np
import jax
import jax.numpy as jnp
from jax import lax
from jax.experimental import pallas as pl
from jax.experimental.pallas import tpu as pltpu

F32 = jnp.float32
BF16 = jnp.bfloat16

D_MODEL = 2048
HEAD_DIM = 64
N_HEADS_DIL = 12
DIL_PATTERNS = ((128, 1), (512, 4), (2048, 16))
DIL_BLOCK = 128
N_HEADS_GLA = 4
GLA_DK = 128
GLA_DV = 256
GLA_GATE_RANK = 16
GLA_TAU = 16.0
GLA_CHUNK = 64
N_HEADS_MEM = 4
MEM_LEN = 256
REL_BUCKETS = 32
REL_MAX_DIST = 2048
EPS = 1e-6
NEG_INF = -1e30

D_DIL = N_HEADS_DIL * HEAD_DIM
D_GLA_K = N_HEADS_GLA * GLA_DK
D_GLA_V = N_HEADS_GLA * GLA_DV
D_MEM = N_HEADS_MEM * HEAD_DIM
D_MIX = D_DIL + D_GLA_V + D_MEM

LANES = 128
GLR_PAD = 256

COL_DQ = 0
COL_DK = COL_DQ + D_DIL
COL_DV = COL_DK + D_DIL
COL_GATE_A = COL_DV + D_DIL
COL_GQ = COL_GATE_A + D_DIL
COL_GK = COL_GQ + D_GLA_K
COL_GV = COL_GK + D_GLA_K
COL_GATE_G = COL_GV + D_GLA_V
COL_MQ = COL_GATE_G + D_GLA_V
COL_GATE_M = COL_MQ + D_MEM
COL_GLR = COL_GATE_M + D_MEM
ROW_W = COL_GLR + GLR_PAD

VMEM_LIMIT = 56 * 1024 * 1024


def _cparams(sem):
    return pltpu.CompilerParams(dimension_semantics=sem, vmem_limit_bytes=VMEM_LIMIT)


IN_TM = 1024
IN_TN = 1152


def _in_proj_kernel(x_ref, nw_ref, w_ref, sc_ref, o_ref, h_ref):
    @pl.when(pl.program_id(1) == 0)
    def _():
        x = x_ref[...]
        ms = jnp.mean(x * x, axis=-1, keepdims=True)
        h_ref[...] = (x * lax.rsqrt(ms + EPS) * nw_ref[...]).astype(BF16)

    acc = jnp.dot(h_ref[...], w_ref[...], preferred_element_type=F32)
    o_ref[...] = (acc * sc_ref[...]).astype(o_ref.dtype)


def _in_proj(x2d, norm_w, w_p, col_scale):
    m = x2d.shape[0]
    return pl.pallas_call(
        _in_proj_kernel,
        grid=(m // IN_TM, ROW_W // IN_TN),
        in_specs=[
            pl.BlockSpec((IN_TM, D_MODEL), lambda i, j: (i, 0)),
            pl.BlockSpec((1, D_MODEL), lambda i, j: (0, 0)),
            pl.BlockSpec((D_MODEL, IN_TN), lambda i, j: (0, j)),
            pl.BlockSpec((1, IN_TN), lambda i, j: (0, j)),
        ],
        out_specs=pl.BlockSpec((IN_TM, IN_TN), lambda i, j: (i, j)),
        out_shape=jax.ShapeDtypeStruct((m, ROW_W), BF16),
        scratch_shapes=[pltpu.VMEM((IN_TM, D_MODEL), BF16)],
        compiler_params=_cparams(("parallel", "arbitrary")),
        name="in_proj",
    )(x2d, norm_w, w_p, col_scale)


def _pair_attention(q_pair, k_pair, v_pair, bias_e, bias_o):
    rows = q_pair.shape[0]
    lo_q = lax.broadcasted_iota(jnp.int32, q_pair.shape, 1) < HEAD_DIM
    zero = jnp.zeros_like(q_pair)
    q_both = jnp.concatenate([jnp.where(lo_q, q_pair, zero), jnp.where(lo_q, zero, q_pair)], axis=0)
    s = lax.dot_general(q_both, k_pair, (((1,), (1,)), ((), ())), preferred_element_type=F32)
    s_e, s_o = s[:rows], s[rows:]
    if bias_e is not None:
        s_e = s_e + bias_e
        s_o = s_o + bias_o
    mx_e = jnp.max(s_e, axis=-1, keepdims=True)
    mx_o = jnp.max(s_o, axis=-1, keepdims=True)
    p_e = jnp.exp(s_e - mx_e).astype(BF16)
    p_o = jnp.exp(s_o - mx_o).astype(BF16)
    lo_v = lax.broadcasted_iota(jnp.int32, v_pair.shape, 1) < HEAD_DIM
    one = jnp.ones_like(v_pair)
    r_e = jnp.dot(p_e, jnp.where(lo_v, v_pair, one), preferred_element_type=F32)
    r_o = jnp.dot(p_o, jnp.where(lo_v, one, v_pair), preferred_element_type=F32)
    sw_e = pltpu.roll(r_e, HEAD_DIM, 1)
    sw_o = pltpu.roll(r_o, HEAD_DIM, 1)
    lo = lax.broadcasted_iota(jnp.int32, r_e.shape, 1) < HEAD_DIM
    o = jnp.where(lo, r_e, r_o) / jnp.where(lo, sw_e, sw_o)
    lse_e = mx_e + jnp.log(sw_e)
    lse_o = mx_o + jnp.log(r_o)
    return o, lse_e, lse_o


def _dil_kernel(q_ref, kp_ref, kc_ref, vp_ref, vc_ref, bias_ref, o_ref, lse_ref):
    first = (pl.program_id(2) == 0).astype(jnp.int32)
    lane = lax.broadcasted_iota(jnp.int32, (DIL_BLOCK, LANES), 1)
    lse_c = jnp.zeros((DIL_BLOCK, LANES), F32)
    for j in range(N_HEADS_DIL // 2):
        sl = slice(j * LANES, (j + 1) * LANES)
        k_pair = jnp.concatenate([kp_ref[0, :, sl], kc_ref[0, :, sl]], axis=0)
        v_pair = jnp.concatenate([vp_ref[0, :, sl], vc_ref[0, :, sl]], axis=0)
        o, lse_e, lse_o = _pair_attention(q_ref[0, :, sl], k_pair, v_pair,
                                          bias_ref[first, 2 * j], bias_ref[first, 2 * j + 1])
        o_ref[0, :, sl] = o.astype(o_ref.dtype)
        lse_c = jnp.where(lane == 2 * j, lse_e, lse_c)
        lse_c = jnp.where(lane == 2 * j + 1, lse_o, lse_c)
    lse_ref[0] = lse_c


def _dilated(proj, bias_tab, batch, seq, dil):
    sub_len = seq // dil
    nb = sub_len // DIL_BLOCK
    blocks_per_row = ROW_W // D_DIL
    view = proj.reshape(batch, sub_len, dil * ROW_W)

    def spec(col, prev):
        if prev:
            return pl.BlockSpec((1, DIL_BLOCK, D_DIL),
                                lambda b, r, l: (b, jnp.maximum(l - 1, 0), r * blocks_per_row + col))
        return pl.BlockSpec((1, DIL_BLOCK, D_DIL), lambda b, r, l: (b, l, r * blocks_per_row + col))

    o, lse = pl.pallas_call(
        _dil_kernel,
        grid=(batch, dil, nb),
        in_specs=[
            spec(COL_DQ // D_DIL, False),
            spec(COL_DK // D_DIL, True), spec(COL_DK // D_DIL, False),
            spec(COL_DV // D_DIL, True), spec(COL_DV // D_DIL, False),
            pl.BlockSpec((2, N_HEADS_DIL, DIL_BLOCK, 2 * DIL_BLOCK), lambda b, r, l: (0, 0, 0, 0)),
        ],
        out_specs=[
            pl.BlockSpec((1, DIL_BLOCK, D_DIL), lambda b, r, l: (b, l, r)),
            pl.BlockSpec((1, DIL_BLOCK, LANES), lambda b, r, l: (b, l, r)),
        ],
        out_shape=[
            jax.ShapeDtypeStruct((batch, sub_len, dil * D_DIL), BF16),
            jax.ShapeDtypeStruct((batch, sub_len, dil * LANES), F32),
        ],
        compiler_params=_cparams(("parallel", "parallel", "arbitrary")),
        name=f"dilated_d{dil}",
    )(view, view, view, view, view, bias_tab)
    return o.reshape(batch * seq, D_DIL), lse.reshape(batch * seq, LANES)


def _t5_bucket(dist):
    max_exact = REL_BUCKETS // 2
    n = jnp.maximum(dist, 1).astype(F32)
    large = max_exact + (jnp.log(n / max_exact) / math.log(REL_MAX_DIST / max_exact)
                         * (REL_BUCKETS - max_exact)).astype(jnp.int32)
    large = jnp.minimum(large, REL_BUCKETS - 1)
    return jnp.where(dist < max_exact, dist, large)


def _bias_table(rel_bias, window, dil):
    back = window // dil
    i = jnp.arange(DIL_BLOCK)[:, None]
    m = jnp.arange(2 * DIL_BLOCK)[None, :]
    rel = i - m + DIL_BLOCK
    bias = rel_bias.astype(F32)[_t5_bucket(jnp.clip(rel, 0, back) * dil)].transpose(2, 0, 1)
    band = (rel >= 0) & (rel <= back)
    has_prev = jnp.where(band[None], bias, NEG_INF)
    no_prev = jnp.where((band & (m >= DIL_BLOCK))[None], bias, NEG_INF)
    return jnp.stack([has_prev, no_prev], axis=0)


GLA_TM = 256


def _gla_kernel(q_ref, k_ref, v_ref, g_ref, w2h_ref, w2l_ref, bg_ref, nw_ref, o_ref, st_ref):
    @pl.when(pl.program_id(1) == 0)
    def _():
        st_ref[...] = jnp.zeros_like(st_ref)

    g = g_ref[:, :LANES]
    z = (jnp.dot(g, w2h_ref[...], preferred_element_type=F32)
         + jnp.dot(g, w2l_ref[...], preferred_element_type=F32) + bg_ref[...])
    la = (jnp.minimum(z, 0.0) - jnp.log1p(jnp.exp(-jnp.abs(z)))) * (1.0 / GLA_TAU)
    la_h = la.astype(BF16)
    la_l = (la - la_h.astype(F32)).astype(BF16)

    r = lax.broadcasted_iota(jnp.int32, (GLA_TM, GLA_TM), 0)
    c = lax.broadcasted_iota(jnp.int32, (GLA_TM, GLA_TM), 1)
    same = (r // GLA_CHUNK) == (c // GLA_CHUNK)
    tri = jnp.where(same & (c <= r), 1.0, 0.0).astype(BF16)
    blk = jnp.where(same, 1.0, 0.0).astype(BF16)
    b = jnp.dot(tri, la_h, preferred_element_type=F32) + jnp.dot(tri, la_l, preferred_element_type=F32)
    b_last = jnp.dot(blk, la_h, preferred_element_type=F32) + jnp.dot(blk, la_l, preferred_element_type=F32)

    q = q_ref[...].astype(F32)
    k = k_ref[...].astype(F32)
    q_t = (q * jnp.exp(b)).astype(BF16)
    k_t = (k * jnp.exp(-b)).astype(BF16)
    k_s = k * jnp.exp(b_last - b)

    causal = (lax.broadcasted_iota(jnp.int32, (GLA_CHUNK, GLA_CHUNK), 1)
              <= lax.broadcasted_iota(jnp.int32, (GLA_CHUNK, GLA_CHUNK), 0))
    nw = nw_ref[...]
    for h in range(N_HEADS_GLA):
        cs = slice(h * GLA_DK, (h + 1) * GLA_DK)
        vs = slice(h * GLA_DV, (h + 1) * GLA_DV)
        k_s_t = k_s[:, cs].T.astype(BF16)
        b_last_t = b_last[:, cs].T
        for ci in range(GLA_TM // GLA_CHUNK):
            rs = slice(ci * GLA_CHUNK, (ci + 1) * GLA_CHUNK)
            v_c = v_ref[rs, vs]
            q_c = q_t[rs, cs]
            a = lax.dot_general(q_c, k_t[rs, cs], (((1,), (1,)), ((), ())), preferred_element_type=F32)
            a = jnp.where(causal, a, 0.0).astype(BF16)
            st = st_ref[h]
            o = (jnp.dot(a, v_c, preferred_element_type=F32)
                 + jnp.dot(q_c, st.astype(BF16), preferred_element_type=F32))
            kv = jnp.dot(k_s_t[:, rs], v_c, preferred_element_type=F32)
            decay = jnp.exp(jnp.broadcast_to(b_last_t[:, ci * GLA_CHUNK:ci * GLA_CHUNK + 1], (GLA_DK, GLA_DV)))
            st_ref[h] = decay * st + kv
            ms = jnp.mean(o * o, axis=-1, keepdims=True)
            o_ref[rs, vs] = (o * lax.rsqrt(ms + EPS) * nw).astype(o_ref.dtype)


def _gla(proj, w2_hi, w2_lo, b_gate, norm_w, batch, seq):
    steps = seq // GLA_TM
    row = lambda b, i: b * steps + i
    return pl.pallas_call(
        _gla_kernel,
        grid=(batch, steps),
        in_specs=[
            pl.BlockSpec((GLA_TM, D_GLA_K), lambda b, i: (row(b, i), COL_GQ // D_GLA_K)),
            pl.BlockSpec((GLA_TM, D_GLA_K), lambda b, i: (row(b, i), COL_GK // D_GLA_K)),
            pl.BlockSpec((GLA_TM, D_GLA_V), lambda b, i: (row(b, i), COL_GV // D_GLA_V)),
            pl.BlockSpec((GLA_TM, GLR_PAD), lambda b, i: (row(b, i), COL_GLR // GLR_PAD)),
            pl.BlockSpec((LANES, D_GLA_K), lambda b, i: (0, 0)),
            pl.BlockSpec((LANES, D_GLA_K), lambda b, i: (0, 0)),
            pl.BlockSpec((1, D_GLA_K), lambda b, i: (0, 0)),
            pl.BlockSpec((1, GLA_DV), lambda b, i: (0, 0)),
        ],
        out_specs=pl.BlockSpec((GLA_TM, D_GLA_V), lambda b, i: (row(b, i), 0)),
        out_shape=jax.ShapeDtypeStruct((batch * seq, D_GLA_V), BF16),
        scratch_shapes=[pltpu.VMEM((N_HEADS_GLA, GLA_DK, GLA_DV), F32)],
        compiler_params=_cparams(("parallel", "arbitrary")),
        name="gla",
    )(proj, proj, proj, proj, w2_hi, w2_lo, b_gate, norm_w)


MEM_TM = 512


def _mem_kernel(q_ref, mem_ref, mnw_ref, wkv_ref, o_ref, kv_ref):
    @pl.when(pl.program_id(1) == 0)
    def _():
        m = mem_ref[0]
        ms = jnp.mean(m * m, axis=-1, keepdims=True)
        mn = (m * lax.rsqrt(ms + EPS) * mnw_ref[...]).astype(BF16)
        kv_ref[...] = jnp.dot(mn, wkv_ref[...], preferred_element_type=F32).astype(BF16)

    for j in range(N_HEADS_MEM // 2):
        sl = slice(j * LANES, (j + 1) * LANES)
        o, _, _ = _pair_attention(q_ref[:, sl], kv_ref[:, sl],
                                  kv_ref[:, D_MEM + j * LANES:D_MEM + (j + 1) * LANES], None, None)
        o_ref[:, sl] = o.astype(o_ref.dtype)


def _mem_attn(proj, mem, mem_norm_w, w_kv, batch, seq):
    steps = seq // MEM_TM
    return pl.pallas_call(
        _mem_kernel,
        grid=(batch, steps),
        in_specs=[
            pl.BlockSpec((MEM_TM, D_MEM), lambda b, i: (b * steps + i, COL_MQ // D_MEM)),
            pl.BlockSpec((1, MEM_LEN, D_MODEL), lambda b, i: (b, 0, 0)),
            pl.BlockSpec((1, D_MODEL), lambda b, i: (0, 0)),
            pl.BlockSpec((D_MODEL, 2 * D_MEM), lambda b, i: (0, 0)),
        ],
        out_specs=pl.BlockSpec((MEM_TM, D_MEM), lambda b, i: (b * steps + i, 0)),
        out_shape=jax.ShapeDtypeStruct((batch * seq, D_MEM), BF16),
        scratch_shapes=[pltpu.VMEM((MEM_LEN, 2 * D_MEM), BF16)],
        compiler_params=_cparams(("parallel", "arbitrary")),
        name="mem_attn",
    )(proj, mem, mem_norm_w, w_kv)


OUT_TM = 256


def _silu(g):
    return g / (1.0 + jnp.exp(-g))


def _out_kernel(o1_ref, o2_ref, o3_ref, l1_ref, l2_ref, l3_ref, og_ref, om_ref,
                ga_ref, gg_ref, gm_ref, ex_ref, w_ref, x_ref, nw_ref, out_ref):
    l1, l2, l3 = l1_ref[...], l2_ref[...], l3_ref[...]
    mx = jnp.maximum(jnp.maximum(l1, l2), l3)
    e1, e2, e3 = jnp.exp(l1 - mx), jnp.exp(l2 - mx), jnp.exp(l3 - mx)
    inv = 1.0 / (e1 + e2 + e3)
    ex = ex_ref[...]
    a = jnp.zeros((OUT_TM, D_DIL), F32)
    for e, o_ref in ((e1, o1_ref), (e2, o2_ref), (e3, o3_ref)):
        w = jnp.dot((e * inv).astype(BF16), ex, preferred_element_type=F32)
        a = a + w * o_ref[...].astype(F32)
    mix_a = (a * _silu(ga_ref[...].astype(F32))).astype(BF16)
    mix_g = (og_ref[...].astype(F32) * _silu(gg_ref[...].astype(F32))).astype(BF16)
    mix_m = (om_ref[...].astype(F32) * _silu(gm_ref[...].astype(F32))).astype(BF16)
    y = (jnp.dot(mix_a, w_ref[0:D_DIL, :], preferred_element_type=F32)
         + jnp.dot(mix_g, w_ref[D_DIL:D_DIL + D_GLA_V, :], preferred_element_type=F32)
         + jnp.dot(mix_m, w_ref[D_DIL + D_GLA_V:D_MIX, :], preferred_element_type=F32))
    ms = jnp.mean(y * y, axis=-1, keepdims=True)
    out_ref[...] = x_ref[...] + y * lax.rsqrt(ms + EPS) * nw_ref[...]


def _out_proj(o_pats, lse_pats, o_gla, o_mem, proj, expand, w_out, x2d, norm_w):
    m = x2d.shape[0]
    rows = lambda width: pl.BlockSpec((OUT_TM, width), lambda i: (i, 0))
    const = lambda shape: pl.BlockSpec(shape, lambda i: (0, 0))
    return pl.pallas_call(
        _out_kernel,
        grid=(m // OUT_TM,),
        in_specs=[
            rows(D_DIL), rows(D_DIL), rows(D_DIL), rows(LANES), rows(LANES), rows(LANES),
            rows(D_GLA_V), rows(D_MEM),
            pl.BlockSpec((OUT_TM, D_DIL), lambda i: (i, COL_GATE_A // D_DIL)),
            pl.BlockSpec((OUT_TM, D_GLA_V), lambda i: (i, COL_GATE_G // D_GLA_V)),
            pl.BlockSpec((OUT_TM, D_MEM), lambda i: (i, COL_GATE_M // D_MEM)),
            const((LANES, D_DIL)), const((D_MIX, D_MODEL)),
            rows(D_MODEL), const((1, D_MODEL)),
        ],
        out_specs=rows(D_MODEL),
        out_shape=jax.ShapeDtypeStruct((m, D_MODEL), x2d.dtype),
        compiler_params=_cparams(("parallel",)),
        name="out_proj",
    )(*o_pats, *lse_pats, o_gla, o_mem, proj, proj, proj, expand, w_out, x2d, norm_w)


def _regroup_in_weights(w_in):
    c = np.cumsum([0, D_DIL, D_DIL, D_DIL, D_GLA_K, D_GLA_K, D_GLA_V, GLA_GATE_RANK, D_MEM, D_MIX])
    dq, dk, dv, gq, gk, gv, glr, mq, gate = (w_in[:, c[n]:c[n + 1]] for n in range(9))
    pad = jnp.zeros((D_MODEL, GLR_PAD - GLA_GATE_RANK), w_in.dtype)
    w_p = jnp.concatenate([dq, dk, dv, gate[:, :D_DIL], gq, gk, gv, gate[:, D_DIL:D_DIL + D_GLA_V],
                           mq, gate[:, D_DIL + D_GLA_V:], glr, pad], axis=1).astype(BF16)
    scale = np.ones((1, ROW_W), np.float32)
    scale[:, COL_DQ:COL_DQ + D_DIL] = HEAD_DIM ** -0.5
    scale[:, COL_GQ:COL_GQ + D_GLA_K] = GLA_DK ** -0.5
    scale[:, COL_MQ:COL_MQ + D_MEM] = HEAD_DIM ** -0.5
    return w_p, jnp.asarray(scale)


def _split_hi_lo(w):
    hi = w.astype(BF16)
    return hi, (w - hi.astype(F32)).astype(BF16)


def kernel(x, mem, norm_pre_w, w_in, rel_bias, w_gla_gate2, b_gla_gate, gla_norm_w, mem_norm_w, w_mem_kv,
           w_out, norm_post_w):
    batch, seq, _ = x.shape
    depth = w_in.shape[0]
    expand = jnp.asarray(np.kron(np.eye(LANES, N_HEADS_DIL), np.ones((1, HEAD_DIM))), BF16)
    bias_tabs = [_bias_table(rel_bias, window, dil) for window, dil in DIL_PATTERNS]
    for l in range(depth):
        x2d = x.reshape(batch * seq, D_MODEL)
        w_p, col_scale = _regroup_in_weights(w_in[l])
        proj = _in_proj(x2d, norm_pre_w[l][None], w_p, col_scale)

        o_pats, lse_pats = [], []
        for (window, dil), tab in zip(DIL_PATTERNS, bias_tabs):
            o_p, lse_p = _dilated(proj, tab, batch, seq, dil)
            o_pats.append(o_p)
            lse_pats.append(lse_p)

        w2 = jnp.zeros((LANES, D_GLA_K), F32).at[:GLA_GATE_RANK].set(w_gla_gate2[l])
        w2_hi, w2_lo = _split_hi_lo(w2)
        o_gla = _gla(proj, w2_hi, w2_lo, b_gla_gate[l][None], gla_norm_w[l][None], batch, seq)

        o_mem = _mem_attn(proj, mem, mem_norm_w[l][None], w_mem_kv[l].astype(BF16), batch, seq)

        out = _out_proj(o_pats, lse_pats, o_gla, o_mem, proj, expand, w_out[l].astype(BF16), x2d,
                        norm_post_w[l][None])
        x = out.reshape(batch, seq, D_MODEL)
    return x
```

```python
import functools
import math

import numpy as np
import jax
import jax.numpy as jnp
from jax import lax
from jax.experimental import pallas as pl
from jax.experimental.pallas import tpu as pltpu

F32 = jnp.float32
BF16 = jnp.bfloat16

D_MODEL = 2048
HEAD_DIM = 64
N_HEADS_DIL = 12
DIL_PATTERNS = ((128, 1), (512, 4), (2048, 16))
DIL_BLOCK = 128
N_HEADS_GLA = 4
GLA_DK = 128
GLA_DV = 256
GLA_GATE_RANK = 16
GLA_TAU = 16.0
GLA_CHUNK = 64
N_HEADS_MEM = 4
MEM_LEN = 256
REL_BUCKETS = 32
REL_MAX_DIST = 2048
EPS = 1e-6
NEG_INF = -1e30

D_DIL = N_HEADS_DIL * HEAD_DIM
D_GLA_K = N_HEADS_GLA * GLA_DK
D_GLA_V = N_HEADS_GLA * GLA_DV
D_MEM = N_HEADS_MEM * HEAD_DIM
D_MIX = D_DIL + D_GLA_V + D_MEM

LANES = 128
GLR_PAD = 256

COL_DQ = 0
COL_DK = COL_DQ + D_DIL
COL_DV = COL_DK + D_DIL
COL_GATE_A = COL_DV + D_DIL
COL_GQ = COL_GATE_A + D_DIL
COL_GK = COL_GQ + D_GLA_K
COL_GV = COL_GK + D_GLA_K
COL_GATE_G = COL_GV + D_GLA_V
COL_MQ = COL_GATE_G + D_GLA_V
COL_GATE_M = COL_MQ + D_MEM
COL_GLR = COL_GATE_M + D_MEM
ROW_W = COL_GLR + GLR_PAD

VMEM_LIMIT = 56 * 1024 * 1024


def _cparams(sem):
    return pltpu.CompilerParams(dimension_semantics=sem, vmem_limit_bytes=VMEM_LIMIT)


IN_TM = 1024
IN_TN = D_DIL
N_DIL_TILES = 3
N_SLABS = IN_TN // LANES


def _in_proj_kernel(x_ref, nw_ref, w_ref, sc_ref, o_ref, x4_ref, x16_ref, h_ref, slab_ref):
    j = pl.program_id(1)

    @pl.when(j == 0)
    def _():
        x = x_ref[...]
        ms = jnp.mean(x * x, axis=-1, keepdims=True)
        h_ref[...] = (x * lax.rsqrt(ms + EPS) * nw_ref[...]).astype(BF16)

    res = jnp.dot(h_ref[...], w_ref[...], preferred_element_type=F32) * sc_ref[...]
    o_ref[...] = res.astype(o_ref.dtype)

    @pl.when(j < N_DIL_TILES)
    def _():
        for s in range(N_SLABS):
            slab_ref[s] = res[:, s * LANES:(s + 1) * LANES]
        for out_ref, dil in ((x4_ref, 4), (x16_ref, 16)):
            for r in range(dil):
                for s in range(N_SLABS):
                    out_ref[0, r, :, s * LANES:(s + 1) * LANES] = (
                        slab_ref[s, pl.ds(r, IN_TM // dil, stride=dil), :].astype(out_ref.dtype))


def _in_proj(x2d, norm_w, w_p, col_scale, batch, seq):
    m = x2d.shape[0]
    tiles_per_seq = seq // IN_TM
    last = N_DIL_TILES - 1
    grouped = lambda dil: pl.BlockSpec(
        (1, dil, IN_TM // dil, IN_TN),
        lambda i, j: (i // tiles_per_seq, 0, i % tiles_per_seq, jnp.minimum(j, last)))
    return pl.pallas_call(
        _in_proj_kernel,
        grid=(m // IN_TM, ROW_W // IN_TN),
        in_specs=[
            pl.BlockSpec((IN_TM, D_MODEL), lambda i, j: (i, 0)),
            pl.BlockSpec((1, D_MODEL), lambda i, j: (0, 0)),
            pl.BlockSpec((D_MODEL, IN_TN), lambda i, j: (0, j)),
            pl.BlockSpec((1, IN_TN), lambda i, j: (0, j)),
        ],
        out_specs=[pl.BlockSpec((IN_TM, IN_TN), lambda i, j: (i, j)), grouped(4), grouped(16)],
        out_shape=[
            jax.ShapeDtypeStruct((m, ROW_W), BF16),
            jax.ShapeDtypeStruct((batch, 4, seq // 4, N_DIL_TILES * D_DIL), BF16),
            jax.ShapeDtypeStruct((batch, 16, seq // 16, N_DIL_TILES * D_DIL), BF16),
        ],
        scratch_shapes=[pltpu.VMEM((IN_TM, D_MODEL), BF16), pltpu.VMEM((N_SLABS, IN_TM, LANES), F32)],
        compiler_params=_cparams(("arbitrary", "arbitrary")),
        name="in_proj",
    )(x2d, norm_w, w_p, col_scale)


def _pair_attention(q_pair, k_pair, v_pair, bias_e, bias_o):
    rows = q_pair.shape[0]
    lo_q = lax.broadcasted_iota(jnp.int32, q_pair.shape, 1) < HEAD_DIM
    zero = jnp.zeros_like(q_pair)
    q_both = jnp.concatenate([jnp.where(lo_q, q_pair, zero), jnp.where(lo_q, zero, q_pair)], axis=0)
    s = lax.dot_general(q_both, k_pair, (((1,), (1,)), ((), ())), preferred_element_type=F32)
    s_e, s_o = s[:rows], s[rows:]
    if bias_e is not None:
        s_e = s_e + bias_e
        s_o = s_o + bias_o
    mx_e = jnp.max(s_e, axis=-1, keepdims=True)
    mx_o = jnp.max(s_o, axis=-1, keepdims=True)
    p_e = jnp.exp(s_e - mx_e).astype(BF16)
    p_o = jnp.exp(s_o - mx_o).astype(BF16)
    lo_v = lax.broadcasted_iota(jnp.int32, v_pair.shape, 1) < HEAD_DIM
    one = jnp.ones_like(v_pair)
    r_e = jnp.dot(p_e, jnp.where(lo_v, v_pair, one), preferred_element_type=F32)
    r_o = jnp.dot(p_o, jnp.where(lo_v, one, v_pair), preferred_element_type=F32)
    sw_e = pltpu.roll(r_e, HEAD_DIM, 1)
    sw_o = pltpu.roll(r_o, HEAD_DIM, 1)
    lo = lax.broadcasted_iota(jnp.int32, r_e.shape, 1) < HEAD_DIM
    o = jnp.where(lo, r_e, r_o) / jnp.where(lo, sw_e, sw_o)
    lse_e = mx_e + jnp.log(sw_e)
    lse_o = mx_o + jnp.log(r_o)
    return o, lse_e, lse_o


def _dil_kernel(q_ref, kp_ref, kc_ref, vp_ref, vc_ref, rbh_ref, rbl_ref, sel_ref, mrow_ref,
                o_ref, lse_ref, bias_ref, acc_ref, *, dil):
    b, lb, r = pl.program_id(0), pl.program_id(1), pl.program_id(2)

    @pl.when((b == 0) & (lb == 0) & (r == 0))
    def _():
        f = (jnp.dot(rbh_ref[...], sel_ref[...], preferred_element_type=F32)
             + jnp.dot(rbl_ref[...], sel_ref[...], preferred_element_type=F32) + mrow_ref[...])
        col = lax.broadcasted_iota(jnp.int32, (DIL_BLOCK, 2 * DIL_BLOCK), 1)
        for h in range(N_HEADS_DIL):
            row = jnp.broadcast_to(f[h:h + 1, :], (DIL_BLOCK, 2 * DIL_BLOCK))
            tab = pltpu.roll(row, 0, 1, stride=1, stride_axis=0)
            bias_ref[0, h] = tab
            bias_ref[1, h] = jnp.where(col >= DIL_BLOCK, tab, NEG_INF)

    first = (lb == 0).astype(jnp.int32)
    lane = lax.broadcasted_iota(jnp.int32, (DIL_BLOCK, LANES), 1)
    lse_c = jnp.zeros((DIL_BLOCK, LANES), F32)
    for j in range(N_HEADS_DIL // 2):
        sl = slice(j * LANES, (j + 1) * LANES)
        k_pair = jnp.concatenate([kp_ref[:, sl], kc_ref[:, sl]], axis=0)
        v_pair = jnp.concatenate([vp_ref[:, sl], vc_ref[:, sl]], axis=0)
        o, lse_e, lse_o = _pair_attention(q_ref[:, sl], k_pair, v_pair,
                                          bias_ref[first, 2 * j], bias_ref[first, 2 * j + 1])
        if dil == 1:
            o_ref[:, sl] = o.astype(o_ref.dtype)
        else:
            acc_ref[j, pl.ds(r, DIL_BLOCK, stride=dil), :] = o
        lse_c = jnp.where(lane == 2 * j, lse_e, lse_c)
        lse_c = jnp.where(lane == 2 * j + 1, lse_o, lse_c)
    if dil == 1:
        lse_ref[...] = lse_c
    else:
        lse_ref[pl.ds(r, DIL_BLOCK, stride=dil), :] = lse_c

        @pl.when(r == dil - 1)
        def _():
            for j in range(N_HEADS_DIL // 2):
                o_ref[:, j * LANES:(j + 1) * LANES] = acc_ref[j].astype(o_ref.dtype)


def _t5_bucket_np(dist):
    max_exact = REL_BUCKETS // 2
    n = np.maximum(dist, 1).astype(np.float32)
    large = max_exact + (np.log(n / np.float32(max_exact)) / np.float32(math.log(REL_MAX_DIST / max_exact))
                         * np.float32(REL_BUCKETS - max_exact)).astype(np.int32)
    large = np.minimum(large, REL_BUCKETS - 1)
    return np.where(dist < max_exact, dist, large)


def _bias_selector(window, dil):
    back = window // dil
    x = np.arange(2 * DIL_BLOCK)
    rel = DIL_BLOCK - x
    valid = (rel >= 0) & (rel <= back)
    bucket = _t5_bucket_np(np.clip(rel, 0, back) * dil)
    sel = np.zeros((LANES, 2 * DIL_BLOCK), np.float32)
    sel[bucket[valid], x[valid]] = 1.0
    mrow = np.where(valid, 0.0, NEG_INF).astype(np.float32)[None]
    return jnp.asarray(sel, BF16), jnp.asarray(mrow)


def _dilated(src, col0, rb_hi, rb_lo, batch, seq, window, dil):
    sub_len = seq // dil
    nb = sub_len // DIL_BLOCK
    sel, mrow = _bias_selector(window, dil)
    rows = DIL_BLOCK * dil

    def spec(col, prev):
        if prev:
            return pl.BlockSpec((None, None, DIL_BLOCK, D_DIL),
                                lambda b, l, r: (b, r, jnp.maximum(l - 1, 0), col0 + col))
        return pl.BlockSpec((None, None, DIL_BLOCK, D_DIL), lambda b, l, r: (b, r, l, col0 + col))

    const = lambda shape: pl.BlockSpec(shape, lambda b, l, r: (0, 0))
    o, lse = pl.pallas_call(
        functools.partial(_dil_kernel, dil=dil),
        grid=(batch, nb, dil),
        in_specs=[
            spec(0, False), spec(1, True), spec(1, False), spec(2, True), spec(2, False),
            const(rb_hi.shape), const(rb_lo.shape), const(sel.shape), const(mrow.shape),
        ],
        out_specs=[
            pl.BlockSpec((rows, D_DIL), lambda b, l, r: (b * nb + l, 0)),
            pl.BlockSpec((rows, LANES), lambda b, l, r: (b * nb + l, 0)),
        ],
        out_shape=[
            jax.ShapeDtypeStruct((batch * seq, D_DIL), BF16),
            jax.ShapeDtypeStruct((batch * seq, LANES), F32),
        ],
        scratch_shapes=[
            pltpu.VMEM((2, N_HEADS_DIL, DIL_BLOCK, 2 * DIL_BLOCK), F32),
            pltpu.VMEM((N_HEADS_DIL // 2, rows, LANES), F32),
        ],
        compiler_params=_cparams(("arbitrary", "arbitrary", "arbitrary")),
        name=f"dilated_d{dil}",
    )(src, src, src, src, src, rb_hi, rb_lo, sel, mrow)
    return o, lse


GLA_TM = 256


def _gla_kernel(q_ref, k_ref, v_ref, g_ref, w2h_ref, w2l_ref, bg_ref, nw_ref, o_ref, st_ref):
    @pl.when(pl.program_id(1) == 0)
    def _():
        st_ref[...] = jnp.zeros_like(st_ref)

    g = g_ref[:, :LANES]
    z = (jnp.dot(g, w2h_ref[...], preferred_element_type=F32)
         + jnp.dot(g, w2l_ref[...], preferred_element_type=F32) + bg_ref[...])
    la = (jnp.minimum(z, 0.0) - jnp.log1p(jnp.exp(-jnp.abs(z)))) * (1.0 / GLA_TAU)
    la_h = la.astype(BF16)
    la_l = (la - la_h.astype(F32)).astype(BF16)

    r = lax.broadcasted_iota(jnp.int32, (GLA_TM, GLA_TM), 0)
    c = lax.broadcasted_iota(jnp.int32, (GLA_TM, GLA_TM), 1)
    same = (r // GLA_CHUNK) == (c // GLA_CHUNK)
    tri = jnp.where(same & (c <= r), 1.0, 0.0).astype(BF16)
    blk = jnp.where(same, 1.0, 0.0).astype(BF16)
    b = jnp.dot(tri, la_h, preferred_element_type=F32) + jnp.dot(tri, la_l, preferred_element_type=F32)
    b_last = jnp.dot(blk, la_h, preferred_element_type=F32) + jnp.dot(blk, la_l, preferred_element_type=F32)

    q = q_ref[...].astype(F32)
    k = k_ref[...].astype(F32)
    q_t = (q * jnp.exp(b)).astype(BF16)
    k_t = (k * jnp.exp(-b)).astype(BF16)
    k_s = k * jnp.exp(b_last - b)

    causal = (lax.broadcasted_iota(jnp.int32, (GLA_CHUNK, GLA_CHUNK), 1)
              <= lax.broadcasted_iota(jnp.int32, (GLA_CHUNK, GLA_CHUNK), 0))
    nw = nw_ref[...]
    for h in range(N_HEADS_GLA):
        cs = slice(h * GLA_DK, (h + 1) * GLA_DK)
        vs = slice(h * GLA_DV, (h + 1) * GLA_DV)
        k_s_t = k_s[:, cs].T.astype(BF16)
        b_last_t = b_last[:, cs].T
        for ci in range(GLA_TM // GLA_CHUNK):
            rs = slice(ci * GLA_CHUNK, (ci + 1) * GLA_CHUNK)
            v_c = v_ref[rs, vs]
            q_c = q_t[rs, cs]
            a = lax.dot_general(q_c, k_t[rs, cs], (((1,), (1,)), ((), ())), preferred_element_type=F32)
            a = jnp.where(causal, a, 0.0).astype(BF16)
            st = st_ref[h]
            o = (jnp.dot(a, v_c, preferred_element_type=F32)
                 + jnp.dot(q_c, st.astype(BF16), preferred_element_type=F32))
            kv = jnp.dot(k_s_t[:, rs], v_c, preferred_element_type=F32)
            decay = jnp.exp(jnp.broadcast_to(b_last_t[:, ci * GLA_CHUNK:ci * GLA_CHUNK + 1], (GLA_DK, GLA_DV)))
            st_ref[h] = decay * st + kv
            ms = jnp.mean(o * o, axis=-1, keepdims=True)
            o_ref[rs, vs] = (o * lax.rsqrt(ms + EPS) * nw).astype(o_ref.dtype)


def _gla(proj, w2_hi, w2_lo, b_gate, norm_w, batch, seq):
    steps = seq // GLA_TM
    row = lambda b, i: b * steps + i
    return pl.pallas_call(
        _gla_kernel,
        grid=(batch, steps),
        in_specs=[
            pl.BlockSpec((GLA_TM, D_GLA_K), lambda b, i: (row(b, i), COL_GQ // D_GLA_K)),
            pl.BlockSpec((GLA_TM, D_GLA_K), lambda b, i: (row(b, i), COL_GK // D_GLA_K)),
            pl.BlockSpec((GLA_TM, D_GLA_V), lambda b, i: (row(b, i), COL_GV // D_GLA_V)),
            pl.BlockSpec((GLA_TM, GLR_PAD), lambda b, i: (row(b, i), COL_GLR // GLR_PAD)),
            pl.BlockSpec((LANES, D_GLA_K), lambda b, i: (0, 0)),
            pl.BlockSpec((LANES, D_GLA_K), lambda b, i: (0, 0)),
            pl.BlockSpec((1, D_GLA_K), lambda b, i: (0, 0)),
            pl.BlockSpec((1, GLA_DV), lambda b, i: (0, 0)),
        ],
        out_specs=pl.BlockSpec((GLA_TM, D_GLA_V), lambda b, i: (row(b, i), 0)),
        out_shape=jax.ShapeDtypeStruct((batch * seq, D_GLA_V), BF16),
        scratch_shapes=[pltpu.VMEM((N_HEADS_GLA, GLA_DK, GLA_DV), F32)],
        compiler_params=_cparams(("parallel", "arbitrary")),
        name="gla",
    )(proj, proj, proj, proj, w2_hi, w2_lo, b_gate, norm_w)


MEM_TM = 512


def _mem_kernel(q_ref, mem_ref, mnw_ref, wkv_ref, o_ref, kv_ref):
    @pl.when(pl.program_id(1) == 0)
    def _():
        m = mem_ref[0]
        ms = jnp.mean(m * m, axis=-1, keepdims=True)
        mn = (m * lax.rsqrt(ms + EPS) * mnw_ref[...]).astype(BF16)
        kv_ref[...] = jnp.dot(mn, wkv_ref[...], preferred_element_type=F32).astype(BF16)

    for j in range(N_HEADS_MEM // 2):
        sl = slice(j * LANES, (j + 1) * LANES)
        o, _, _ = _pair_attention(q_ref[:, sl], kv_ref[:, sl],
                                  kv_ref[:, D_MEM + j * LANES:D_MEM + (j + 1) * LANES], None, None)
        o_ref[:, sl] = o.astype(o_ref.dtype)


def _mem_attn(proj, mem, mem_norm_w, w_kv, batch, seq):
    steps = seq // MEM_TM
    return pl.pallas_call(
        _mem_kernel,
        grid=(batch, steps),
        in_specs=[
            pl.BlockSpec((MEM_TM, D_MEM), lambda b, i: (b * steps + i, COL_MQ // D_MEM)),
            pl.BlockSpec((1, MEM_LEN, D_MODEL), lambda b, i: (b, 0, 0)),
            pl.BlockSpec((1, D_MODEL), lambda b, i: (0, 0)),
            pl.BlockSpec((D_MODEL, 2 * D_MEM), lambda b, i: (0, 0)),
        ],
        out_specs=pl.BlockSpec((MEM_TM, D_MEM), lambda b, i: (b * steps + i, 0)),
        out_shape=jax.ShapeDtypeStruct((batch * seq, D_MEM), BF16),
        scratch_shapes=[pltpu.VMEM((MEM_LEN, 2 * D_MEM), BF16)],
        compiler_params=_cparams(("parallel", "arbitrary")),
        name="mem_attn",
    )(proj, mem, mem_norm_w, w_kv)


OUT_TM = 256


def _silu(g):
    return g / (1.0 + jnp.exp(-g))


def _out_kernel(o1_ref, o2_ref, o3_ref, l1_ref, l2_ref, l3_ref, og_ref, om_ref,
                ga_ref, gg_ref, gm_ref, ex_ref, w_ref, x_ref, nw_ref, out_ref):
    l1, l2, l3 = l1_ref[...], l2_ref[...], l3_ref[...]
    mx = jnp.maximum(jnp.maximum(l1, l2), l3)
    e1, e2, e3 = jnp.exp(l1 - mx), jnp.exp(l2 - mx), jnp.exp(l3 - mx)
    inv = 1.0 / (e1 + e2 + e3)
    ex = ex_ref[...]
    a = jnp.zeros((OUT_TM, D_DIL), F32)
    for e, o_ref in ((e1, o1_ref), (e2, o2_ref), (e3, o3_ref)):
        w = jnp.dot((e * inv).astype(BF16), ex, preferred_element_type=F32)
        a = a + w * o_ref[...].astype(F32)
    mix_a = (a * _silu(ga_ref[...].astype(F32))).astype(BF16)
    mix_g = (og_ref[...].astype(F32) * _silu(gg_ref[...].astype(F32))).astype(BF16)
    mix_m = (om_ref[...].astype(F32) * _silu(gm_ref[...].astype(F32))).astype(BF16)
    y = (jnp.dot(mix_a, w_ref[0:D_DIL, :], preferred_element_type=F32)
         + jnp.dot(mix_g, w_ref[D_DIL:D_DIL + D_GLA_V, :], preferred_element_type=F32)
         + jnp.dot(mix_m, w_ref[D_DIL + D_GLA_V:D_MIX, :], preferred_element_type=F32))
    ms = jnp.mean(y * y, axis=-1, keepdims=True)
    out_ref[...] = x_ref[...] + y * lax.rsqrt(ms + EPS) * nw_ref[...]


def _out_proj(o_pats, lse_pats, o_gla, o_mem, proj, expand, w_out, x2d, norm_w):
    m = x2d.shape[0]
    rows = lambda width: pl.BlockSpec((OUT_TM, width), lambda i: (i, 0))
    const = lambda shape: pl.BlockSpec(shape, lambda i: (0, 0))
    return pl.pallas_call(
        _out_kernel,
        grid=(m // OUT_TM,),
        in_specs=[
            rows(D_DIL), rows(D_DIL), rows(D_DIL), rows(LANES), rows(LANES), rows(LANES),
            rows(D_GLA_V), rows(D_MEM),
            pl.BlockSpec((OUT_TM, D_DIL), lambda i: (i, COL_GATE_A // D_DIL)),
            pl.BlockSpec((OUT_TM, D_GLA_V), lambda i: (i, COL_GATE_G // D_GLA_V)),
            pl.BlockSpec((OUT_TM, D_MEM), lambda i: (i, COL_GATE_M // D_MEM)),
            const((LANES, D_DIL)), const((D_MIX, D_MODEL)),
            rows(D_MODEL), const((1, D_MODEL)),
        ],
        out_specs=rows(D_MODEL),
        out_shape=jax.ShapeDtypeStruct((m, D_MODEL), x2d.dtype),
        compiler_params=_cparams(("parallel",)),
        name="out_proj",
    )(*o_pats, *lse_pats, o_gla, o_mem, proj, proj, proj, expand, w_out, x2d, norm_w)


def _regroup_in_weights(w_in):
    c = np.cumsum([0, D_DIL, D_DIL, D_DIL, D_GLA_K, D_GLA_K, D_GLA_V, GLA_GATE_RANK, D_MEM, D_MIX])
    dq, dk, dv, gq, gk, gv, glr, mq, gate = (w_in[:, c[n]:c[n + 1]] for n in range(9))
    pad = jnp.zeros((D_MODEL, GLR_PAD - GLA_GATE_RANK), w_in.dtype)
    w_p = jnp.concatenate([dq, dk, dv, gate[:, :D_DIL], gq, gk, gv, gate[:, D_DIL:D_DIL + D_GLA_V],
                           mq, gate[:, D_DIL + D_GLA_V:], glr, pad], axis=1).astype(BF16)
    scale = np.ones((1, ROW_W), np.float32)
    scale[:, COL_DQ:COL_DQ + D_DIL] = HEAD_DIM ** -0.5
    scale[:, COL_GQ:COL_GQ + D_GLA_K] = GLA_DK ** -0.5
    scale[:, COL_MQ:COL_MQ + D_MEM] = HEAD_DIM ** -0.5
    return w_p, jnp.asarray(scale)


def _split_hi_lo(w):
    hi = w.astype(BF16)
    return hi, (w - hi.astype(F32)).astype(BF16)


def kernel(x, mem, norm_pre_w, w_in, rel_bias, w_gla_gate2, b_gla_gate, gla_norm_w, mem_norm_w, w_mem_kv,
           w_out, norm_post_w):
    batch, seq, _ = x.shape
    depth = w_in.shape[0]
    expand = jnp.asarray(np.kron(np.eye(LANES, N_HEADS_DIL), np.ones((1, HEAD_DIM))), BF16)
    rb_t = jnp.zeros((16, LANES), F32).at[:N_HEADS_DIL, :REL_BUCKETS].set(rel_bias.astype(F32).T)
    rb_hi, rb_lo = _split_hi_lo(rb_t)
    for l in range(depth):
        x2d = x.reshape(batch * seq, D_MODEL)
        w_p, col_scale = _regroup_in_weights(w_in[l])
        proj, qkv4, qkv16 = _in_proj(x2d, norm_pre_w[l][None], w_p, col_scale, batch, seq)

        srcs = (proj.reshape(batch, 1, seq, ROW_W), qkv4, qkv16)
        o_pats, lse_pats = [], []
        for (window, dil), src in zip(DIL_PATTERNS, srcs):
            o_p, lse_p = _dilated(src, COL_DQ // D_DIL, rb_hi, rb_lo, batch, seq, window, dil)
            o_pats.append(o_p)
            lse_pats.append(lse_p)

        w2 = jnp.zeros((LANES, D_GLA_K), F32).at[:GLA_GATE_RANK].set(w_gla_gate2[l])
        w2_hi, w2_lo = _split_hi_lo(w2)
        o_gla = _gla(proj, w2_hi, w2_lo, b_gla_gate[l][None], gla_norm_w[l][None], batch, seq)

        o_mem = _mem_attn(proj, mem, mem_norm_w[l][None], w_mem_kv[l].astype(BF16), batch, seq)

        out = _out_proj(o_pats, lse_pats, o_gla, o_mem, proj, expand, w_out[l].astype(BF16), x2d,
                        norm_post_w[l][None])
        x = out.reshape(batch, seq, D_MODEL)
    return x
```

```python
import functools
import math

import numpy as np
import jax
import jax.numpy as jnp
from jax import lax
from jax.experimental import pallas as pl
from jax.experimental.pallas import tpu as pltpu

F32 = jnp.float32
BF16 = jnp.bfloat16

D_MODEL = 2048
HEAD_DIM = 64
N_HEADS_DIL = 12
DIL_PATTERNS = ((128, 1), (512, 4), (2048, 16))
DIL_BLOCK = 128
N_HEADS_GLA = 4
GLA_DK = 128
GLA_DV = 256
GLA_GATE_RANK = 16
GLA_TAU = 16.0
GLA_CHUNK = 64
N_HEADS_MEM = 4
MEM_LEN = 256
REL_BUCKETS = 32
REL_MAX_DIST = 2048
EPS = 1e-6
NEG_INF = -1e30

D_DIL = N_HEADS_DIL * HEAD_DIM
D_GLA_K = N_HEADS_GLA * GLA_DK
D_GLA_V = N_HEADS_GLA * GLA_DV
D_MEM = N_HEADS_MEM * HEAD_DIM
D_MIX = D_DIL + D_GLA_V + D_MEM

LANES = 128
GLR_PAD = 256

COL_DQ = 0
COL_DK = COL_DQ + D_DIL
COL_DV = COL_DK + D_DIL
COL_GATE_A = COL_DV + D_DIL
COL_GQ = COL_GATE_A + D_DIL
COL_GK = COL_GQ + D_GLA_K
COL_GV = COL_GK + D_GLA_K
COL_GATE_G = COL_GV + D_GLA_V
COL_MQ = COL_GATE_G + D_GLA_V
COL_GATE_M = COL_MQ + D_MEM
COL_GLR = COL_GATE_M + D_MEM
ROW_W = COL_GLR + GLR_PAD

VMEM_LIMIT = 56 * 1024 * 1024


def _cparams(sem):
    return pltpu.CompilerParams(dimension_semantics=sem, vmem_limit_bytes=VMEM_LIMIT)


IN_TM = 1024
IN_TN = D_DIL
N_DIL_TILES = 3
N_SLABS = IN_TN // LANES


def _in_proj_kernel(x_ref, nw_ref, w_ref, sc_ref, o_ref, x4_ref, x16_ref, h_ref, slab_ref):
    j = pl.program_id(1)

    @pl.when(j == 0)
    def _():
        x = x_ref[...]
        ms = jnp.mean(x * x, axis=-1, keepdims=True)
        h_ref[...] = (x * lax.rsqrt(ms + EPS) * nw_ref[...]).astype(BF16)

    res = jnp.dot(h_ref[...], w_ref[...], preferred_element_type=F32) * sc_ref[...]
    o_ref[...] = res.astype(o_ref.dtype)

    @pl.when(j < N_DIL_TILES)
    def _():
        for s in range(N_SLABS):
            slab_ref[s] = res[:, s * LANES:(s + 1) * LANES]
        for out_ref, dil in ((x4_ref, 4), (x16_ref, 16)):
            for r in range(dil):
                for s in range(N_SLABS):
                    out_ref[0, r, :, s * LANES:(s + 1) * LANES] = (
                        slab_ref[s, pl.ds(r, IN_TM // dil, stride=dil), :].astype(out_ref.dtype))


def _in_proj(x2d, norm_w, w_p, col_scale, batch, seq):
    m = x2d.shape[0]
    tiles_per_seq = seq // IN_TM
    last = N_DIL_TILES - 1
    grouped = lambda dil: pl.BlockSpec(
        (1, dil, IN_TM // dil, IN_TN),
        lambda i, j: (i // tiles_per_seq, 0, i % tiles_per_seq, jnp.minimum(j, last)))
    return pl.pallas_call(
        _in_proj_kernel,
        grid=(m // IN_TM, ROW_W // IN_TN),
        in_specs=[
            pl.BlockSpec((IN_TM, D_MODEL), lambda i, j: (i, 0)),
            pl.BlockSpec((1, D_MODEL), lambda i, j: (0, 0)),
            pl.BlockSpec((D_MODEL, IN_TN), lambda i, j: (0, j)),
            pl.BlockSpec((1, IN_TN), lambda i, j: (0, j)),
        ],
        out_specs=[pl.BlockSpec((IN_TM, IN_TN), lambda i, j: (i, j)), grouped(4), grouped(16)],
        out_shape=[
            jax.ShapeDtypeStruct((m, ROW_W), BF16),
            jax.ShapeDtypeStruct((batch, 4, seq // 4, N_DIL_TILES * D_DIL), BF16),
            jax.ShapeDtypeStruct((batch, 16, seq // 16, N_DIL_TILES * D_DIL), BF16),
        ],
        scratch_shapes=[pltpu.VMEM((IN_TM, D_MODEL), BF16), pltpu.VMEM((N_SLABS, IN_TM, LANES), F32)],
        compiler_params=_cparams(("arbitrary", "arbitrary")),
        name="in_proj",
    )(x2d, norm_w, w_p, col_scale)


def _pair_attention(q_pair, k_pair, v_pair, bias_e, bias_o):
    rows = q_pair.shape[0]
    lo_q = lax.broadcasted_iota(jnp.int32, q_pair.shape, 1) < HEAD_DIM
    zero = jnp.zeros_like(q_pair)
    q_both = jnp.concatenate([jnp.where(lo_q, q_pair, zero), jnp.where(lo_q, zero, q_pair)], axis=0)
    s = lax.dot_general(q_both, k_pair, (((1,), (1,)), ((), ())), preferred_element_type=F32)
    s_e, s_o = s[:rows], s[rows:]
    if bias_e is not None:
        s_e = s_e + bias_e
        s_o = s_o + bias_o
    mx_e = jnp.max(s_e, axis=-1, keepdims=True)
    mx_o = jnp.max(s_o, axis=-1, keepdims=True)
    p_e = jnp.exp(s_e - mx_e).astype(BF16)
    p_o = jnp.exp(s_o - mx_o).astype(BF16)
    lo_v = lax.broadcasted_iota(jnp.int32, v_pair.shape, 1) < HEAD_DIM
    one = jnp.ones_like(v_pair)
    r_e = jnp.dot(p_e, jnp.where(lo_v, v_pair, one), preferred_element_type=F32)
    r_o = jnp.dot(p_o, jnp.where(lo_v, one, v_pair), preferred_element_type=F32)
    sw_e = pltpu.roll(r_e, HEAD_DIM, 1)
    sw_o = pltpu.roll(r_o, HEAD_DIM, 1)
    lo = lax.broadcasted_iota(jnp.int32, r_e.shape, 1) < HEAD_DIM
    o = jnp.where(lo, r_e, r_o) / jnp.where(lo, sw_e, sw_o)
    lse_e = mx_e + jnp.log(sw_e)
    lse_o = mx_o + jnp.log(r_o)
    return o, lse_e, lse_o


def _dil_kernel(q_ref, kp_ref, kc_ref, vp_ref, vc_ref, rbh_ref, rbl_ref, sel_ref, mrow_ref,
                o_ref, lse_ref, bias_ref, acc_ref, *, dil):
    b, lb, r = pl.program_id(0), pl.program_id(1), pl.program_id(2)

    @pl.when((b == 0) & (lb == 0) & (r == 0))
    def _():
        f = (jnp.dot(rbh_ref[...], sel_ref[...], preferred_element_type=F32)
             + jnp.dot(rbl_ref[...], sel_ref[...], preferred_element_type=F32) + mrow_ref[...])
        col = lax.broadcasted_iota(jnp.int32, (DIL_BLOCK, 2 * DIL_BLOCK), 1)
        for h in range(N_HEADS_DIL):
            row = jnp.broadcast_to(f[h:h + 1, :], (DIL_BLOCK, 2 * DIL_BLOCK))
            tab = pltpu.roll(row, 0, 1, stride=1, stride_axis=0)
            bias_ref[0, h] = tab
            bias_ref[1, h] = jnp.where(col >= DIL_BLOCK, tab, NEG_INF)

    first = (lb == 0).astype(jnp.int32)
    lane = lax.broadcasted_iota(jnp.int32, (DIL_BLOCK, LANES), 1)
    lse_c = jnp.zeros((DIL_BLOCK, LANES), F32)
    for j in range(N_HEADS_DIL // 2):
        sl = slice(j * LANES, (j + 1) * LANES)
        k_pair = jnp.concatenate([kp_ref[:, sl], kc_ref[:, sl]], axis=0)
        v_pair = jnp.concatenate([vp_ref[:, sl], vc_ref[:, sl]], axis=0)
        o, lse_e, lse_o = _pair_attention(q_ref[:, sl], k_pair, v_pair,
                                          bias_ref[first, 2 * j], bias_ref[first, 2 * j + 1])
        if dil == 1:
            o_ref[:, sl] = o.astype(o_ref.dtype)
        else:
            acc_ref[j, pl.ds(r, DIL_BLOCK, stride=dil), :] = o
        lse_c = jnp.where(lane == 2 * j, lse_e, lse_c)
        lse_c = jnp.where(lane == 2 * j + 1, lse_o, lse_c)
    if dil == 1:
        lse_ref[...] = lse_c
    else:
        lse_ref[pl.ds(r, DIL_BLOCK, stride=dil), :] = lse_c

        @pl.when(r == dil - 1)
        def _():
            for j in range(N_HEADS_DIL // 2):
                o_ref[:, j * LANES:(j + 1) * LANES] = acc_ref[j].astype(o_ref.dtype)


def _t5_bucket_np(dist):
    max_exact = REL_BUCKETS // 2
    n = np.maximum(dist, 1).astype(np.float32)
    large = max_exact + (np.log(n / np.float32(max_exact)) / np.float32(math.log(REL_MAX_DIST / max_exact))
                         * np.float32(REL_BUCKETS - max_exact)).astype(np.int32)
    large = np.minimum(large, REL_BUCKETS - 1)
    return np.where(dist < max_exact, dist, large)


def _bias_selector(window, dil):
    back = window // dil
    x = np.arange(2 * DIL_BLOCK)
    rel = DIL_BLOCK - x
    valid = (rel >= 0) & (rel <= back)
    bucket = _t5_bucket_np(np.clip(rel, 0, back) * dil)
    sel = np.zeros((LANES, 2 * DIL_BLOCK), np.float32)
    sel[bucket[valid], x[valid]] = 1.0
    mrow = np.where(valid, 0.0, NEG_INF).astype(np.float32)[None]
    return jnp.asarray(sel, BF16), jnp.asarray(mrow)


def _dilated(src, col0, rb_hi, rb_lo, batch, seq, window, dil):
    sub_len = seq // dil
    nb = sub_len // DIL_BLOCK
    sel, mrow = _bias_selector(window, dil)
    rows = DIL_BLOCK * dil

    def spec(col, prev):
        if prev:
            return pl.BlockSpec((None, None, DIL_BLOCK, D_DIL),
                                lambda b, l, r: (b, r, jnp.maximum(l - 1, 0), col0 + col))
        return pl.BlockSpec((None, None, DIL_BLOCK, D_DIL), lambda b, l, r: (b, r, l, col0 + col))

    const = lambda shape: pl.BlockSpec(shape, lambda b, l, r: (0, 0))
    o, lse = pl.pallas_call(
        functools.partial(_dil_kernel, dil=dil),
        grid=(batch, nb, dil),
        in_specs=[
            spec(0, False), spec(1, True), spec(1, False), spec(2, True), spec(2, False),
            const(rb_hi.shape), const(rb_lo.shape), const(sel.shape), const(mrow.shape),
        ],
        out_specs=[
            pl.BlockSpec((rows, D_DIL), lambda b, l, r: (b * nb + l, 0)),
            pl.BlockSpec((rows, LANES), lambda b, l, r: (b * nb + l, 0)),
        ],
        out_shape=[
            jax.ShapeDtypeStruct((batch * seq, D_DIL), BF16),
            jax.ShapeDtypeStruct((batch * seq, LANES), F32),
        ],
        scratch_shapes=[
            pltpu.VMEM((2, N_HEADS_DIL, DIL_BLOCK, 2 * DIL_BLOCK), F32),
            pltpu.VMEM((N_HEADS_DIL // 2, rows, LANES), F32),
        ],
        compiler_params=_cparams(("arbitrary", "arbitrary", "arbitrary")),
        name=f"dilated_d{dil}",
    )(src, src, src, src, src, rb_hi, rb_lo, sel, mrow)
    return o, lse


GLA_TM = 256


def _gla_kernel(q_ref, k_ref, v_ref, g_ref, w2h_ref, w2l_ref, bg_ref, nw_ref, o_ref, st_ref):
    @pl.when(pl.program_id(1) == 0)
    def _():
        st_ref[...] = jnp.zeros_like(st_ref)

    g = g_ref[:, :LANES]
    z = (jnp.dot(g, w2h_ref[...], preferred_element_type=F32)
         + jnp.dot(g, w2l_ref[...], preferred_element_type=F32) + bg_ref[...])
    la = (jnp.minimum(z, 0.0) - jnp.log1p(jnp.exp(-jnp.abs(z)))) * (1.0 / GLA_TAU)
    la_h = la.astype(BF16)
    la_l = (la - la_h.astype(F32)).astype(BF16)

    r = lax.broadcasted_iota(jnp.int32, (GLA_TM, GLA_TM), 0)
    c = lax.broadcasted_iota(jnp.int32, (GLA_TM, GLA_TM), 1)
    same = (r // GLA_CHUNK) == (c // GLA_CHUNK)
    tri = jnp.where(same & (c <= r), 1.0, 0.0).astype(BF16)
    blk = jnp.where(same, 1.0, 0.0).astype(BF16)
    b = jnp.dot(tri, la_h, preferred_element_type=F32) + jnp.dot(tri, la_l, preferred_element_type=F32)
    b_last = jnp.dot(blk, la_h, preferred_element_type=F32) + jnp.dot(blk, la_l, preferred_element_type=F32)

    q = q_ref[...].astype(F32)
    k = k_ref[...].astype(F32)
    q_t = (q * jnp.exp(b)).astype(BF16)
    k_t = (k * jnp.exp(-b)).astype(BF16)
    k_s = k * jnp.exp(b_last - b)

    causal = (lax.broadcasted_iota(jnp.int32, (GLA_CHUNK, GLA_CHUNK), 1)
              <= lax.broadcasted_iota(jnp.int32, (GLA_CHUNK, GLA_CHUNK), 0))
    nw = nw_ref[...]
    for h in range(N_HEADS_GLA):
        cs = slice(h * GLA_DK, (h + 1) * GLA_DK)
        vs = slice(h * GLA_DV, (h + 1) * GLA_DV)
        k_s_t = k_s[:, cs].T.astype(BF16)
        b_last_t = b_last[:, cs].T
        for ci in range(GLA_TM // GLA_CHUNK):
            rs = slice(ci * GLA_CHUNK, (ci + 1) * GLA_CHUNK)
            v_c = v_ref[rs, vs]
            q_c = q_t[rs, cs]
            a = lax.dot_general(q_c, k_t[rs, cs], (((1,), (1,)), ((), ())), preferred_element_type=F32)
            a = jnp.where(causal, a, 0.0).astype(BF16)
            st = st_ref[h]
            o = (jnp.dot(a, v_c, preferred_element_type=F32)
                 + jnp.dot(q_c, st.astype(BF16), preferred_element_type=F32))
            kv = jnp.dot(k_s_t[:, rs], v_c, preferred_element_type=F32)
            decay = jnp.exp(jnp.broadcast_to(b_last_t[:, ci * GLA_CHUNK:ci * GLA_CHUNK + 1], (GLA_DK, GLA_DV)))
            st_ref[h] = decay * st + kv
            ms = jnp.mean(o * o, axis=-1, keepdims=True)
            o_ref[rs, vs] = (o * lax.rsqrt(ms + EPS) * nw).astype(o_ref.dtype)


def _gla(proj, w2_hi, w2_lo, b_gate, norm_w, batch, seq):
    steps = seq // GLA_TM
    row = lambda b, i: b * steps + i
    return pl.pallas_call(
        _gla_kernel,
        grid=(batch, steps),
        in_specs=[
            pl.BlockSpec((GLA_TM, D_GLA_K), lambda b, i: (row(b, i), COL_GQ // D_GLA_K)),
            pl.BlockSpec((GLA_TM, D_GLA_K), lambda b, i: (row(b, i), COL_GK // D_GLA_K)),
            pl.BlockSpec((GLA_TM, D_GLA_V), lambda b, i: (row(b, i), COL_GV // D_GLA_V)),
            pl.BlockSpec((GLA_TM, GLR_PAD), lambda b, i: (row(b, i), COL_GLR // GLR_PAD)),
            pl.BlockSpec((LANES, D_GLA_K), lambda b, i: (0, 0)),
            pl.BlockSpec((LANES, D_GLA_K), lambda b, i: (0, 0)),
            pl.BlockSpec((1, D_GLA_K), lambda b, i: (0, 0)),
            pl.BlockSpec((1, GLA_DV), lambda b, i: (0, 0)),
        ],
        out_specs=pl.BlockSpec((GLA_TM, D_GLA_V), lambda b, i: (row(b, i), 0)),
        out_shape=jax.ShapeDtypeStruct((batch * seq, D_GLA_V), BF16),
        scratch_shapes=[pltpu.VMEM((N_HEADS_GLA, GLA_DK, GLA_DV), F32)],
        compiler_params=_cparams(("parallel", "arbitrary")),
        name="gla",
    )(proj, proj, proj, proj, w2_hi, w2_lo, b_gate, norm_w)


MEM_TM = 512


def _mem_kernel(q_ref, mem_ref, mnw_ref, wkv_ref, o_ref, kv_ref):
    @pl.when(pl.program_id(1) == 0)
    def _():
        m = mem_ref[0]
        ms = jnp.mean(m * m, axis=-1, keepdims=True)
        mn = (m * lax.rsqrt(ms + EPS) * mnw_ref[...]).astype(BF16)
        kv_ref[...] = jnp.dot(mn, wkv_ref[...], preferred_element_type=F32).astype(BF16)

    for j in range(N_HEADS_MEM // 2):
        sl = slice(j * LANES, (j + 1) * LANES)
        o, _, _ = _pair_attention(q_ref[:, sl], kv_ref[:, sl],
                                  kv_ref[:, D_MEM + j * LANES:D_MEM + (j + 1) * LANES], None, None)
        o_ref[:, sl] = o.astype(o_ref.dtype)


def _mem_attn(proj, mem, mem_norm_w, w_kv, batch, seq):
    steps = seq // MEM_TM
    return pl.pallas_call(
        _mem_kernel,
        grid=(batch, steps),
        in_specs=[
            pl.BlockSpec((MEM_TM, D_MEM), lambda b, i: (b * steps + i, COL_MQ // D_MEM)),
            pl.BlockSpec((1, MEM_LEN, D_MODEL), lambda b, i: (b, 0, 0)),
            pl.BlockSpec((1, D_MODEL), lambda b, i: (0, 0)),
            pl.BlockSpec((D_MODEL, 2 * D_MEM), lambda b, i: (0, 0)),
        ],
        out_specs=pl.BlockSpec((MEM_TM, D_MEM), lambda b, i: (b * steps + i, 0)),
        out_shape=jax.ShapeDtypeStruct((batch * seq, D_MEM), BF16),
        scratch_shapes=[pltpu.VMEM((MEM_LEN, 2 * D_MEM), BF16)],
        compiler_params=_cparams(("parallel", "arbitrary")),
        name="mem_attn",
    )(proj, mem, mem_norm_w, w_kv)


OUT_TM = 256


def _silu(g):
    return g / (1.0 + jnp.exp(-g))


def _out_kernel(o1_ref, o2_ref, o3_ref, l1_ref, l2_ref, l3_ref, og_ref, om_ref,
                ga_ref, gg_ref, gm_ref, ex_ref, w_ref, x_ref, nw_ref, out_ref):
    l1, l2, l3 = l1_ref[...], l2_ref[...], l3_ref[...]
    mx = jnp.maximum(jnp.maximum(l1, l2), l3)
    e1, e2, e3 = jnp.exp(l1 - mx), jnp.exp(l2 - mx), jnp.exp(l3 - mx)
    inv = 1.0 / (e1 + e2 + e3)
    ex = ex_ref[...]
    a = jnp.zeros((OUT_TM, D_DIL), F32)
    for e, o_ref in ((e1, o1_ref), (e2, o2_ref), (e3, o3_ref)):
        w = jnp.dot((e * inv).astype(BF16), ex, preferred_element_type=F32)
        a = a + w * o_ref[...].astype(F32)
    mix_a = (a * _silu(ga_ref[...].astype(F32))).astype(BF16)
    mix_g = (og_ref[...].astype(F32) * _silu(gg_ref[...].astype(F32))).astype(BF16)
    mix_m = (om_ref[...].astype(F32) * _silu(gm_ref[...].astype(F32))).astype(BF16)
    y = (jnp.dot(mix_a, w_ref[0:D_DIL, :], preferred_element_type=F32)
         + jnp.dot(mix_g, w_ref[D_DIL:D_DIL + D_GLA_V, :], preferred_element_type=F32)
         + jnp.dot(mix_m, w_ref[D_DIL + D_GLA_V:D_MIX, :], preferred_element_type=F32))
    ms = jnp.mean(y * y, axis=-1, keepdims=True)
    out_ref[...] = x_ref[...] + y * lax.rsqrt(ms + EPS) * nw_ref[...]


def _out_proj(o_pats, lse_pats, o_gla, o_mem, proj, expand, w_out, x2d, norm_w):
    m = x2d.shape[0]
    rows = lambda width: pl.BlockSpec((OUT_TM, width), lambda i: (i, 0))
    const = lambda shape: pl.BlockSpec(shape, lambda i: (0, 0))
    return pl.pallas_call(
        _out_kernel,
        grid=(m // OUT_TM,),
        in_specs=[
            rows(D_DIL), rows(D_DIL), rows(D_DIL), rows(LANES), rows(LANES), rows(LANES),
            rows(D_GLA_V), rows(D_MEM),
            pl.BlockSpec((OUT_TM, D_DIL), lambda i: (i, COL_GATE_A // D_DIL)),
            pl.BlockSpec((OUT_TM, D_GLA_V), lambda i: (i, COL_GATE_G // D_GLA_V)),
            pl.BlockSpec((OUT_TM, D_MEM), lambda i: (i, COL_GATE_M // D_MEM)),
            const((LANES, D_DIL)), const((D_MIX, D_MODEL)),
            rows(D_MODEL), const((1, D_MODEL)),
        ],
        out_specs=rows(D_MODEL),
        out_shape=jax.ShapeDtypeStruct((m, D_MODEL), x2d.dtype),
        compiler_params=_cparams(("parallel",)),
        name="out_proj",
    )(*o_pats, *lse_pats, o_gla, o_mem, proj, proj, proj, expand, w_out, x2d, norm_w)


D_IN_PROJ = 3 * D_DIL + 2 * D_GLA_K + D_GLA_V + GLA_GATE_RANK + D_MEM + D_MIX
SRC_GLR = 3 * D_DIL + 2 * D_GLA_K + D_GLA_V
SRC_MQ = SRC_GLR + GLA_GATE_RANK
SRC_GATE = SRC_MQ + D_MEM
W_SEGMENTS = (
    (0, 3 * D_DIL, COL_DQ),
    (SRC_GATE, D_DIL, COL_GATE_A),
    (3 * D_DIL, 2 * D_GLA_K + D_GLA_V, COL_GQ),
    (SRC_GATE + D_DIL, D_GLA_V, COL_GATE_G),
    (SRC_MQ, D_MEM, COL_MQ),
    (SRC_GATE + D_DIL + D_GLA_V, D_MEM, COL_GATE_M),
)
WPREP_TM = 256


def _wprep_kernel(w_ref, o_ref):
    for src, width, dst in W_SEGMENTS:
        o_ref[:, dst:dst + width] = w_ref[:, src:src + width].astype(o_ref.dtype)
    glr = w_ref[:, SRC_GLR:SRC_GLR + LANES]
    keep = lax.broadcasted_iota(jnp.int32, glr.shape, 1) < GLA_GATE_RANK
    o_ref[:, COL_GLR:COL_GLR + LANES] = jnp.where(keep, glr, 0.0).astype(o_ref.dtype)
    o_ref[:, COL_GLR + LANES:ROW_W] = jnp.zeros((WPREP_TM, ROW_W - COL_GLR - LANES), o_ref.dtype)


def _regroup_in_weights(w_in):
    w_p = pl.pallas_call(
        _wprep_kernel,
        grid=(D_MODEL // WPREP_TM,),
        in_specs=[pl.BlockSpec((WPREP_TM, D_IN_PROJ), lambda i: (i, 0))],
        out_specs=pl.BlockSpec((WPREP_TM, ROW_W), lambda i: (i, 0)),
        out_shape=jax.ShapeDtypeStruct((D_MODEL, ROW_W), BF16),
        compiler_params=_cparams(("parallel",)),
        name="weight_regroup",
    )(w_in)
    scale = np.ones((1, ROW_W), np.float32)
    scale[:, COL_DQ:COL_DQ + D_DIL] = HEAD_DIM ** -0.5
    scale[:, COL_GQ:COL_GQ + D_GLA_K] = GLA_DK ** -0.5
    scale[:, COL_MQ:COL_MQ + D_MEM] = HEAD_DIM ** -0.5
    return w_p, jnp.asarray(scale)


def _split_hi_lo(w):
    hi = w.astype(BF16)
    return hi, (w - hi.astype(F32)).astype(BF16)


def kernel(x, mem, norm_pre_w, w_in, rel_bias, w_gla_gate2, b_gla_gate, gla_norm_w, mem_norm_w, w_mem_kv,
           w_out, norm_post_w):
    batch, seq, _ = x.shape
    depth = w_in.shape[0]
    expand = jnp.asarray(np.kron(np.eye(LANES, N_HEADS_DIL), np.ones((1, HEAD_DIM))), BF16)
    rb_t = jnp.zeros((16, LANES), F32).at[:N_HEADS_DIL, :REL_BUCKETS].set(rel_bias.astype(F32).T)
    rb_hi, rb_lo = _split_hi_lo(rb_t)
    for l in range(depth):
        x2d = x.reshape(batch * seq, D_MODEL)
        w_p, col_scale = _regroup_in_weights(w_in[l])
        proj, qkv4, qkv16 = _in_proj(x2d, norm_pre_w[l][None], w_p, col_scale, batch, seq)

        srcs = (proj.reshape(batch, 1, seq, ROW_W), qkv4, qkv16)
        o_pats, lse_pats = [], []
        for (window, dil), src in zip(DIL_PATTERNS, srcs):
            o_p, lse_p = _dilated(src, COL_DQ // D_DIL, rb_hi, rb_lo, batch, seq, window, dil)
            o_pats.append(o_p)
            lse_pats.append(lse_p)

        w2 = jnp.zeros((LANES, D_GLA_K), F32).at[:GLA_GATE_RANK].set(w_gla_gate2[l])
        w2_hi, w2_lo = _split_hi_lo(w2)
        o_gla = _gla(proj, w2_hi, w2_lo, b_gla_gate[l][None], gla_norm_w[l][None], batch, seq)

        o_mem = _mem_attn(proj, mem, mem_norm_w[l][None], w_mem_kv[l].astype(BF16), batch, seq)

        out = _out_proj(o_pats, lse_pats, o_gla, o_mem, proj, expand, w_out[l].astype(BF16), x2d,
                        norm_post_w[l][None])
        x = out.reshape(batch, seq, D_MODEL)
    return x
```

```python
import functools
import math

import numpy as np
import jax
import jax.numpy as jnp
from jax import lax
from jax.experimental import pallas as pl
from jax.experimental.pallas import tpu as pltpu

F32 = jnp.float32
BF16 = jnp.bfloat16

D_MODEL = 2048
HEAD_DIM = 64
N_HEADS_DIL = 12
DIL_PATTERNS = ((128, 1), (512, 4), (2048, 16))
DIL_BLOCK = 128
N_HEADS_GLA = 4
GLA_DK = 128
GLA_DV = 256
GLA_GATE_RANK = 16
GLA_TAU = 16.0
GLA_CHUNK = 64
N_HEADS_MEM = 4
MEM_LEN = 256
REL_BUCKETS = 32
REL_MAX_DIST = 2048
EPS = 1e-6
NEG_INF = -1e30

D_DIL = N_HEADS_DIL * HEAD_DIM
D_GLA_K = N_HEADS_GLA * GLA_DK
D_GLA_V = N_HEADS_GLA * GLA_DV
D_MEM = N_HEADS_MEM * HEAD_DIM
D_MIX = D_DIL + D_GLA_V + D_MEM

LANES = 128
GLR_PAD = 256

COL_DQ = 0
COL_DK = COL_DQ + D_DIL
COL_DV = COL_DK + D_DIL
COL_GATE_A = COL_DV + D_DIL
COL_GQ = COL_GATE_A + D_DIL
COL_GK = COL_GQ + D_GLA_K
COL_GV = COL_GK + D_GLA_K
COL_GATE_G = COL_GV + D_GLA_V
COL_MQ = COL_GATE_G + D_GLA_V
COL_GATE_M = COL_MQ + D_MEM
COL_GLR = COL_GATE_M + D_MEM
ROW_W = COL_GLR + GLR_PAD

VMEM_LIMIT = 56 * 1024 * 1024


def _cparams(sem):
    return pltpu.CompilerParams(dimension_semantics=sem, vmem_limit_bytes=VMEM_LIMIT)


IN_TM = 1024
IN_TN = D_DIL
N_DIL_TILES = 3
N_SLABS = IN_TN // LANES


def _in_proj_kernel(x_ref, nw_ref, w_ref, sc_ref, o_ref, x4_ref, x16_ref, h_ref, slab_ref):
    j = pl.program_id(1)

    @pl.when(j == 0)
    def _():
        x = x_ref[...]
        ms = jnp.mean(x * x, axis=-1, keepdims=True)
        h_ref[...] = (x * lax.rsqrt(ms + EPS) * nw_ref[...]).astype(BF16)

    res = jnp.dot(h_ref[...], w_ref[...], preferred_element_type=F32) * sc_ref[...]
    o_ref[...] = res.astype(o_ref.dtype)

    @pl.when(j < N_DIL_TILES)
    def _():
        for s in range(N_SLABS):
            slab_ref[s] = res[:, s * LANES:(s + 1) * LANES]
        for out_ref, dil in ((x4_ref, 4), (x16_ref, 16)):
            for r in range(dil):
                for s in range(N_SLABS):
                    out_ref[0, r, :, s * LANES:(s + 1) * LANES] = (
                        slab_ref[s, pl.ds(r, IN_TM // dil, stride=dil), :].astype(out_ref.dtype))


def _in_proj(x2d, norm_w, w_p, col_scale, batch, seq):
    m = x2d.shape[0]
    tiles_per_seq = seq // IN_TM
    last = N_DIL_TILES - 1
    grouped = lambda dil: pl.BlockSpec(
        (1, dil, IN_TM // dil, IN_TN),
        lambda i, j: (i // tiles_per_seq, 0, i % tiles_per_seq, jnp.minimum(j, last)))
    return pl.pallas_call(
        _in_proj_kernel,
        grid=(m // IN_TM, ROW_W // IN_TN),
        in_specs=[
            pl.BlockSpec((IN_TM, D_MODEL), lambda i, j: (i, 0)),
            pl.BlockSpec((1, D_MODEL), lambda i, j: (0, 0)),
            pl.BlockSpec((D_MODEL, IN_TN), lambda i, j: (0, j)),
            pl.BlockSpec((1, IN_TN), lambda i, j: (0, j)),
        ],
        out_specs=[pl.BlockSpec((IN_TM, IN_TN), lambda i, j: (i, j)), grouped(4), grouped(16)],
        out_shape=[
            jax.ShapeDtypeStruct((m, ROW_W), BF16),
            jax.ShapeDtypeStruct((batch, 4, seq // 4, N_DIL_TILES * D_DIL), BF16),
            jax.ShapeDtypeStruct((batch, 16, seq // 16, N_DIL_TILES * D_DIL), BF16),
        ],
        scratch_shapes=[pltpu.VMEM((IN_TM, D_MODEL), BF16), pltpu.VMEM((N_SLABS, IN_TM, LANES), F32)],
        compiler_params=_cparams(("arbitrary", "arbitrary")),
        name="in_proj",
    )(x2d, norm_w, w_p, col_scale)


def _pair_attention(q_pair, k_pair, v_pair, bias_e, bias_o):
    rows = q_pair.shape[0]
    lo_q = lax.broadcasted_iota(jnp.int32, q_pair.shape, 1) < HEAD_DIM
    zero = jnp.zeros_like(q_pair)
    q_both = jnp.concatenate([jnp.where(lo_q, q_pair, zero), jnp.where(lo_q, zero, q_pair)], axis=0)
    s = lax.dot_general(q_both, k_pair, (((1,), (1,)), ((), ())), preferred_element_type=F32)
    s_e, s_o = s[:rows], s[rows:]
    if bias_e is not None:
        s_e = s_e + bias_e
        s_o = s_o + bias_o
    mx_e = jnp.max(s_e, axis=-1, keepdims=True)
    mx_o = jnp.max(s_o, axis=-1, keepdims=True)
    p_e = jnp.exp(s_e - mx_e).astype(BF16)
    p_o = jnp.exp(s_o - mx_o).astype(BF16)
    lo_v = lax.broadcasted_iota(jnp.int32, v_pair.shape, 1) < HEAD_DIM
    one = jnp.ones_like(v_pair)
    r_e = jnp.dot(p_e, jnp.where(lo_v, v_pair, one), preferred_element_type=F32)
    r_o = jnp.dot(p_o, jnp.where(lo_v, one, v_pair), preferred_element_type=F32)
    sw_e = pltpu.roll(r_e, HEAD_DIM, 1)
    sw_o = pltpu.roll(r_o, HEAD_DIM, 1)
    lo = lax.broadcasted_iota(jnp.int32, r_e.shape, 1) < HEAD_DIM
    o = jnp.where(lo, r_e, r_o) / jnp.where(lo, sw_e, sw_o)
    lse_e = mx_e + jnp.log(sw_e)
    lse_o = mx_o + jnp.log(r_o)
    return o, lse_e, lse_o


def _dil_kernel(q_ref, kp_ref, kc_ref, vp_ref, vc_ref, rbh_ref, rbl_ref, sel_ref, mrow_ref,
                o_ref, lse_ref, bias_ref, acc_ref, *, dil):
    b, lb, r = pl.program_id(0), pl.program_id(1), pl.program_id(2)

    @pl.when((b == 0) & (lb == 0) & (r == 0))
    def _():
        f = (jnp.dot(rbh_ref[...], sel_ref[...], preferred_element_type=F32)
             + jnp.dot(rbl_ref[...], sel_ref[...], preferred_element_type=F32) + mrow_ref[...])
        col = lax.broadcasted_iota(jnp.int32, (DIL_BLOCK, 2 * DIL_BLOCK), 1)
        for h in range(N_HEADS_DIL):
            row = jnp.broadcast_to(f[h:h + 1, :], (DIL_BLOCK, 2 * DIL_BLOCK))
            tab = pltpu.roll(row, 0, 1, stride=1, stride_axis=0)
            bias_ref[0, h] = tab
            bias_ref[1, h] = jnp.where(col >= DIL_BLOCK, tab, NEG_INF)

    first = (lb == 0).astype(jnp.int32)
    lane = lax.broadcasted_iota(jnp.int32, (DIL_BLOCK, LANES), 1)
    lse_c = jnp.zeros((DIL_BLOCK, LANES), F32)
    for j in range(N_HEADS_DIL // 2):
        sl = slice(j * LANES, (j + 1) * LANES)
        k_pair = jnp.concatenate([kp_ref[:, sl], kc_ref[:, sl]], axis=0)
        v_pair = jnp.concatenate([vp_ref[:, sl], vc_ref[:, sl]], axis=0)
        o, lse_e, lse_o = _pair_attention(q_ref[:, sl], k_pair, v_pair,
                                          bias_ref[first, 2 * j], bias_ref[first, 2 * j + 1])
        if dil == 1:
            o_ref[:, sl] = o.astype(o_ref.dtype)
        else:
            acc_ref[j, pl.ds(r, DIL_BLOCK, stride=dil), :] = o
        lse_c = jnp.where(lane == 2 * j, lse_e, lse_c)
        lse_c = jnp.where(lane == 2 * j + 1, lse_o, lse_c)
    if dil == 1:
        lse_ref[...] = lse_c
    else:
        lse_ref[pl.ds(r, DIL_BLOCK, stride=dil), :] = lse_c

        @pl.when(r == dil - 1)
        def _():
            for j in range(N_HEADS_DIL // 2):
                o_ref[:, j * LANES:(j + 1) * LANES] = acc_ref[j].astype(o_ref.dtype)


def _t5_bucket_np(dist):
    max_exact = REL_BUCKETS // 2
    n = np.maximum(dist, 1).astype(np.float32)
    large = max_exact + (np.log(n / np.float32(max_exact)) / np.float32(math.log(REL_MAX_DIST / max_exact))
                         * np.float32(REL_BUCKETS - max_exact)).astype(np.int32)
    large = np.minimum(large, REL_BUCKETS - 1)
    return np.where(dist < max_exact, dist, large)


def _bias_selector(window, dil):
    back = window // dil
    x = np.arange(2 * DIL_BLOCK)
    rel = DIL_BLOCK - x
    valid = (rel >= 0) & (rel <= back)
    bucket = _t5_bucket_np(np.clip(rel, 0, back) * dil)
    sel = np.zeros((LANES, 2 * DIL_BLOCK), np.float32)
    sel[bucket[valid], x[valid]] = 1.0
    mrow = np.where(valid, 0.0, NEG_INF).astype(np.float32)[None]
    return jnp.asarray(sel, BF16), jnp.asarray(mrow)


def _dilated(src, col0, rb_hi, rb_lo, batch, seq, window, dil):
    sub_len = seq // dil
    nb = sub_len // DIL_BLOCK
    sel, mrow = _bias_selector(window, dil)
    rows = DIL_BLOCK * dil

    def spec(col, prev):
        if prev:
            return pl.BlockSpec((None, None, DIL_BLOCK, D_DIL),
                                lambda b, l, r: (b, r, jnp.maximum(l - 1, 0), col0 + col))
        return pl.BlockSpec((None, None, DIL_BLOCK, D_DIL), lambda b, l, r: (b, r, l, col0 + col))

    const = lambda shape: pl.BlockSpec(shape, lambda b, l, r: (0, 0))
    o, lse = pl.pallas_call(
        functools.partial(_dil_kernel, dil=dil),
        grid=(batch, nb, dil),
        in_specs=[
            spec(0, False), spec(1, True), spec(1, False), spec(2, True), spec(2, False),
            const(rb_hi.shape), const(rb_lo.shape), const(sel.shape), const(mrow.shape),
        ],
        out_specs=[
            pl.BlockSpec((rows, D_DIL), lambda b, l, r: (b * nb + l, 0)),
            pl.BlockSpec((rows, LANES), lambda b, l, r: (b * nb + l, 0)),
        ],
        out_shape=[
            jax.ShapeDtypeStruct((batch * seq, D_DIL), BF16),
            jax.ShapeDtypeStruct((batch * seq, LANES), F32),
        ],
        scratch_shapes=[
            pltpu.VMEM((2, N_HEADS_DIL, DIL_BLOCK, 2 * DIL_BLOCK), F32),
            pltpu.VMEM((N_HEADS_DIL // 2, rows, LANES), F32),
        ],
        compiler_params=_cparams(("arbitrary", "arbitrary", "arbitrary")),
        name=f"dilated_d{dil}",
    )(src, src, src, src, src, rb_hi, rb_lo, sel, mrow)
    return o, lse


GLA_TM = 256


def _gla_kernel(q_ref, k_ref, v_ref, g_ref, w2h_ref, w2l_ref, bg_ref, nw_ref, o_ref, st_ref):
    @pl.when(pl.program_id(1) == 0)
    def _():
        st_ref[...] = jnp.zeros_like(st_ref)

    g = g_ref[:, :LANES]
    z = (jnp.dot(g, w2h_ref[...], preferred_element_type=F32)
         + jnp.dot(g, w2l_ref[...], preferred_element_type=F32) + bg_ref[...])
    la = (jnp.minimum(z, 0.0) - jnp.log1p(jnp.exp(-jnp.abs(z)))) * (1.0 / GLA_TAU)
    la_h = la.astype(BF16)
    la_l = (la - la_h.astype(F32)).astype(BF16)

    r = lax.broadcasted_iota(jnp.int32, (GLA_TM, GLA_TM), 0)
    c = lax.broadcasted_iota(jnp.int32, (GLA_TM, GLA_TM), 1)
    same = (r // GLA_CHUNK) == (c // GLA_CHUNK)
    tri = jnp.where(same & (c <= r), 1.0, 0.0).astype(BF16)
    blk = jnp.where(same, 1.0, 0.0).astype(BF16)
    b = jnp.dot(tri, la_h, preferred_element_type=F32) + jnp.dot(tri, la_l, preferred_element_type=F32)
    b_last = jnp.dot(blk, la_h, preferred_element_type=F32) + jnp.dot(blk, la_l, preferred_element_type=F32)

    q = q_ref[...].astype(F32)
    k = k_ref[...].astype(F32)
    q_t = (q * jnp.exp(b)).astype(BF16)
    k_t = (k * jnp.exp(-b)).astype(BF16)
    k_s = k * jnp.exp(b_last - b)

    causal = (lax.broadcasted_iota(jnp.int32, (GLA_CHUNK, GLA_CHUNK), 1)
              <= lax.broadcasted_iota(jnp.int32, (GLA_CHUNK, GLA_CHUNK), 0))
    nw = nw_ref[...]
    for h in range(N_HEADS_GLA):
        cs = slice(h * GLA_DK, (h + 1) * GLA_DK)
        vs = slice(h * GLA_DV, (h + 1) * GLA_DV)
        k_s_t = k_s[:, cs].T.astype(BF16)
        b_last_t = b_last[:, cs].T
        for ci in range(GLA_TM // GLA_CHUNK):
            rs = slice(ci * GLA_CHUNK, (ci + 1) * GLA_CHUNK)
            v_c = v_ref[rs, vs]
            q_c = q_t[rs, cs]
            a = lax.dot_general(q_c, k_t[rs, cs], (((1,), (1,)), ((), ())), preferred_element_type=F32)
            a = jnp.where(causal, a, 0.0).astype(BF16)
            st = st_ref[h]
            o = (jnp.dot(a, v_c, preferred_element_type=F32)
                 + jnp.dot(q_c, st.astype(BF16), preferred_element_type=F32))
            kv = jnp.dot(k_s_t[:, rs], v_c, preferred_element_type=F32)
            decay = jnp.exp(jnp.broadcast_to(b_last_t[:, ci * GLA_CHUNK:ci * GLA_CHUNK + 1], (GLA_DK, GLA_DV)))
            st_ref[h] = decay * st + kv
            ms = jnp.mean(o * o, axis=-1, keepdims=True)
            o_ref[rs, vs] = (o * lax.rsqrt(ms + EPS) * nw).astype(o_ref.dtype)


def _gla(proj, w2_hi, w2_lo, b_gate, norm_w, batch, seq):
    steps = seq // GLA_TM
    row = lambda b, i: b * steps + i
    return pl.pallas_call(
        _gla_kernel,
        grid=(batch, steps),
        in_specs=[
            pl.BlockSpec((GLA_TM, D_GLA_K), lambda b, i: (row(b, i), COL_GQ // D_GLA_K)),
            pl.BlockSpec((GLA_TM, D_GLA_K), lambda b, i: (row(b, i), COL_GK // D_GLA_K)),
            pl.BlockSpec((GLA_TM, D_GLA_V), lambda b, i: (row(b, i), COL_GV // D_GLA_V)),
            pl.BlockSpec((GLA_TM, GLR_PAD), lambda b, i: (row(b, i), COL_GLR // GLR_PAD)),
            pl.BlockSpec((LANES, D_GLA_K), lambda b, i: (0, 0)),
            pl.BlockSpec((LANES, D_GLA_K), lambda b, i: (0, 0)),
            pl.BlockSpec((1, D_GLA_K), lambda b, i: (0, 0)),
            pl.BlockSpec((1, GLA_DV), lambda b, i: (0, 0)),
        ],
        out_specs=pl.BlockSpec((GLA_TM, D_GLA_V), lambda b, i: (row(b, i), 0)),
        out_shape=jax.ShapeDtypeStruct((batch * seq, D_GLA_V), BF16),
        scratch_shapes=[pltpu.VMEM((N_HEADS_GLA, GLA_DK, GLA_DV), F32)],
        compiler_params=_cparams(("parallel", "arbitrary")),
        name="gla",
    )(proj, proj, proj, proj, w2_hi, w2_lo, b_gate, norm_w)


MEM_TM = 512


def _mem_kernel(q_ref, mem_ref, mnw_ref, wkv_ref, o_ref, kv_ref):
    @pl.when(pl.program_id(1) == 0)
    def _():
        m = mem_ref[0]
        ms = jnp.mean(m * m, axis=-1, keepdims=True)
        mn = (m * lax.rsqrt(ms + EPS) * mnw_ref[...]).astype(BF16)
        kv_ref[...] = jnp.dot(mn, wkv_ref[...], preferred_element_type=F32).astype(BF16)

    for j in range(N_HEADS_MEM // 2):
        sl = slice(j * LANES, (j + 1) * LANES)
        o, _, _ = _pair_attention(q_ref[:, sl], kv_ref[:, sl],
                                  kv_ref[:, D_MEM + j * LANES:D_MEM + (j + 1) * LANES], None, None)
        o_ref[:, sl] = o.astype(o_ref.dtype)


def _mem_attn(proj, mem, mem_norm_w, w_kv, batch, seq):
    steps = seq // MEM_TM
    return pl.pallas_call(
        _mem_kernel,
        grid=(batch, steps),
        in_specs=[
            pl.BlockSpec((MEM_TM, D_MEM), lambda b, i: (b * steps + i, COL_MQ // D_MEM)),
            pl.BlockSpec((1, MEM_LEN, D_MODEL), lambda b, i: (b, 0, 0)),
            pl.BlockSpec((1, D_MODEL), lambda b, i: (0, 0)),
            pl.BlockSpec((D_MODEL, 2 * D_MEM), lambda b, i: (0, 0)),
        ],
        out_specs=pl.BlockSpec((MEM_TM, D_MEM), lambda b, i: (b * steps + i, 0)),
        out_shape=jax.ShapeDtypeStruct((batch * seq, D_MEM), BF16),
        scratch_shapes=[pltpu.VMEM((MEM_LEN, 2 * D_MEM), BF16)],
        compiler_params=_cparams(("parallel", "arbitrary")),
        name="mem_attn",
    )(proj, mem, mem_norm_w, w_kv)


OUT_TM = 256


def _silu(g):
    return g / (1.0 + jnp.exp(-g))


def _out_kernel(o1_ref, o2_ref, o3_ref, l1_ref, l2_ref, l3_ref, og_ref, om_ref,
                ga_ref, gg_ref, gm_ref, ex_ref, w_ref, x_ref, nw_ref, out_ref):
    l1, l2, l3 = l1_ref[...], l2_ref[...], l3_ref[...]
    mx = jnp.maximum(jnp.maximum(l1, l2), l3)
    e1, e2, e3 = jnp.exp(l1 - mx), jnp.exp(l2 - mx), jnp.exp(l3 - mx)
    inv = 1.0 / (e1 + e2 + e3)
    ex = ex_ref[...]
    a = jnp.zeros((OUT_TM, D_DIL), F32)
    for e, o_ref in ((e1, o1_ref), (e2, o2_ref), (e3, o3_ref)):
        w = jnp.dot((e * inv).astype(BF16), ex, preferred_element_type=F32)
        a = a + w * o_ref[...].astype(F32)
    mix_a = (a * _silu(ga_ref[...].astype(F32))).astype(BF16)
    mix_g = (og_ref[...].astype(F32) * _silu(gg_ref[...].astype(F32))).astype(BF16)
    mix_m = (om_ref[...].astype(F32) * _silu(gm_ref[...].astype(F32))).astype(BF16)
    y = (jnp.dot(mix_a, w_ref[0:D_DIL, :], preferred_element_type=F32)
         + jnp.dot(mix_g, w_ref[D_DIL:D_DIL + D_GLA_V, :], preferred_element_type=F32)
         + jnp.dot(mix_m, w_ref[D_DIL + D_GLA_V:D_MIX, :], preferred_element_type=F32))
    ms = jnp.mean(y * y, axis=-1, keepdims=True)
    out_ref[...] = x_ref[...] + y * lax.rsqrt(ms + EPS) * nw_ref[...]


def _out_proj(o_pats, lse_pats, o_gla, o_mem, proj, expand, w_out, x2d, norm_w):
    m = x2d.shape[0]
    rows = lambda width: pl.BlockSpec((OUT_TM, width), lambda i: (i, 0))
    const = lambda shape: pl.BlockSpec(shape, lambda i: (0, 0))
    return pl.pallas_call(
        _out_kernel,
        grid=(m // OUT_TM,),
        in_specs=[
            rows(D_DIL), rows(D_DIL), rows(D_DIL), rows(LANES), rows(LANES), rows(LANES),
            rows(D_GLA_V), rows(D_MEM),
            pl.BlockSpec((OUT_TM, D_DIL), lambda i: (i, COL_GATE_A // D_DIL)),
            pl.BlockSpec((OUT_TM, D_GLA_V), lambda i: (i, COL_GATE_G // D_GLA_V)),
            pl.BlockSpec((OUT_TM, D_MEM), lambda i: (i, COL_GATE_M // D_MEM)),
            const((LANES, D_DIL)), const((D_MIX, D_MODEL)),
            rows(D_MODEL), const((1, D_MODEL)),
        ],
        out_specs=rows(D_MODEL),
        out_shape=jax.ShapeDtypeStruct((m, D_MODEL), x2d.dtype),
        compiler_params=_cparams(("parallel",)),
        name="out_proj",
    )(*o_pats, *lse_pats, o_gla, o_mem, proj, proj, proj, expand, w_out, x2d, norm_w)


D_IN_PROJ = 3 * D_DIL + 2 * D_GLA_K + D_GLA_V + GLA_GATE_RANK + D_MEM + D_MIX
SRC_GLR = 3 * D_DIL + 2 * D_GLA_K + D_GLA_V
SRC_MQ = SRC_GLR + GLA_GATE_RANK
SRC_GATE = SRC_MQ + D_MEM
W_SEGMENTS = (
    (0, 3 * D_DIL, COL_DQ),
    (SRC_GATE, D_DIL, COL_GATE_A),
    (3 * D_DIL, 2 * D_GLA_K + D_GLA_V, COL_GQ),
    (SRC_GATE + D_DIL, D_GLA_V, COL_GATE_G),
    (SRC_MQ, D_MEM, COL_MQ),
    (SRC_GATE + D_DIL + D_GLA_V, D_MEM, COL_GATE_M),
)
WPREP_TK = 512


def _wprep_kernel(wt_ref, o_ref):
    for src, width, dst in W_SEGMENTS:
        for c in range(width // LANES):
            lo = c * LANES
            o_ref[:, dst + lo:dst + lo + LANES] = wt_ref[src + lo:src + lo + LANES, :].T.astype(o_ref.dtype)
    glr = wt_ref[SRC_GLR:SRC_GLR + LANES, :].T
    keep = lax.broadcasted_iota(jnp.int32, glr.shape, 1) < GLA_GATE_RANK
    o_ref[:, COL_GLR:COL_GLR + LANES] = jnp.where(keep, glr, 0.0).astype(o_ref.dtype)
    o_ref[:, COL_GLR + LANES:ROW_W] = jnp.zeros((WPREP_TK, ROW_W - COL_GLR - LANES), o_ref.dtype)


def _regroup_in_weights(w_in):
    w_p = pl.pallas_call(
        _wprep_kernel,
        grid=(D_MODEL // WPREP_TK,),
        in_specs=[pl.BlockSpec((D_IN_PROJ, WPREP_TK), lambda i: (0, i))],
        out_specs=pl.BlockSpec((WPREP_TK, ROW_W), lambda i: (i, 0)),
        out_shape=jax.ShapeDtypeStruct((D_MODEL, ROW_W), BF16),
        compiler_params=_cparams(("parallel",)),
        name="weight_regroup",
    )(w_in.T)
    scale = np.ones((1, ROW_W), np.float32)
    scale[:, COL_DQ:COL_DQ + D_DIL] = HEAD_DIM ** -0.5
    scale[:, COL_GQ:COL_GQ + D_GLA_K] = GLA_DK ** -0.5
    scale[:, COL_MQ:COL_MQ + D_MEM] = HEAD_DIM ** -0.5
    return w_p, jnp.asarray(scale)


def _split_hi_lo(w):
    hi = w.astype(BF16)
    return hi, (w - hi.astype(F32)).astype(BF16)


def kernel(x, mem, norm_pre_w, w_in, rel_bias, w_gla_gate2, b_gla_gate, gla_norm_w, mem_norm_w, w_mem_kv,
           w_out, norm_post_w):
    batch, seq, _ = x.shape
    depth = w_in.shape[0]
    expand = jnp.asarray(np.kron(np.eye(LANES, N_HEADS_DIL), np.ones((1, HEAD_DIM))), BF16)
    rb_t = jnp.zeros((16, LANES), F32).at[:N_HEADS_DIL, :REL_BUCKETS].set(rel_bias.astype(F32).T)
    rb_hi, rb_lo = _split_hi_lo(rb_t)
    for l in range(depth):
        x2d = x.reshape(batch * seq, D_MODEL)
        w_p, col_scale = _regroup_in_weights(w_in[l])
        proj, qkv4, qkv16 = _in_proj(x2d, norm_pre_w[l][None], w_p, col_scale, batch, seq)

        srcs = (proj.reshape(batch, 1, seq, ROW_W), qkv4, qkv16)
        o_pats, lse_pats = [], []
        for (window, dil), src in zip(DIL_PATTERNS, srcs):
            o_p, lse_p = _dilated(src, COL_DQ // D_DIL, rb_hi, rb_lo, batch, seq, window, dil)
            o_pats.append(o_p)
            lse_pats.append(lse_p)

        w2 = jnp.zeros((LANES, D_GLA_K), F32).at[:GLA_GATE_RANK].set(w_gla_gate2[l])
        w2_hi, w2_lo = _split_hi_lo(w2)
        o_gla = _gla(proj, w2_hi, w2_lo, b_gla_gate[l][None], gla_norm_w[l][None], batch, seq)

        o_mem = _mem_attn(proj, mem, mem_norm_w[l][None], w_mem_kv[l].astype(BF16), batch, seq)

        out = _out_proj(o_pats, lse_pats, o_gla, o_mem, proj, expand, w_out[l].astype(BF16), x2d,
                        norm_post_w[l][None])
        x = out.reshape(batch, seq, D_MODEL)
    return x
```

```python
import functools
import math

import numpy as np
import jax
import jax.numpy as jnp
from jax import lax
from jax.experimental import pallas as pl
from jax.experimental.pallas import tpu as pltpu

F32 = jnp.float32
BF16 = jnp.bfloat16

D_MODEL = 2048
HEAD_DIM = 64
N_HEADS_DIL = 12
DIL_PATTERNS = ((128, 1), (512, 4), (2048, 16))
DIL_BLOCK = 128
N_HEADS_GLA = 4
GLA_DK = 128
GLA_DV = 256
GLA_GATE_RANK = 16
GLA_TAU = 16.0
GLA_CHUNK = 64
N_HEADS_MEM = 4
MEM_LEN = 256
REL_BUCKETS = 32
REL_MAX_DIST = 2048
EPS = 1e-6
NEG_INF = -1e30

D_DIL = N_HEADS_DIL * HEAD_DIM
D_GLA_K = N_HEADS_GLA * GLA_DK
D_GLA_V = N_HEADS_GLA * GLA_DV
D_MEM = N_HEADS_MEM * HEAD_DIM
D_MIX = D_DIL + D_GLA_V + D_MEM

LANES = 128
GLR_PAD = 256

COL_DQ = 0
COL_DK = COL_DQ + D_DIL
COL_DV = COL_DK + D_DIL
COL_GATE_A = COL_DV + D_DIL
COL_GQ = COL_GATE_A + D_DIL
COL_GK = COL_GQ + D_GLA_K
COL_GV = COL_GK + D_GLA_K
COL_GATE_G = COL_GV + D_GLA_V
COL_MQ = COL_GATE_G + D_GLA_V
COL_GATE_M = COL_MQ + D_MEM
COL_GLR = COL_GATE_M + D_MEM
ROW_W = COL_GLR + GLR_PAD

VMEM_LIMIT = 56 * 1024 * 1024


def _cparams(sem):
    return pltpu.CompilerParams(dimension_semantics=sem, vmem_limit_bytes=VMEM_LIMIT)


IN_TM = 1024
IN_TN = D_DIL
N_DIL_TILES = 3
N_SLABS = IN_TN // LANES


def _in_proj_kernel(x_ref, nw_ref, w_ref, sc_ref, o_ref, x4_ref, x16_ref, h_ref, slab_ref):
    j = pl.program_id(1)

    @pl.when(j == 0)
    def _():
        x = x_ref[...]
        ms = jnp.mean(x * x, axis=-1, keepdims=True)
        h_ref[...] = (x * lax.rsqrt(ms + EPS) * nw_ref[...]).astype(BF16)

    res = jnp.dot(h_ref[...], w_ref[...], preferred_element_type=F32) * sc_ref[...]
    o_ref[...] = res.astype(o_ref.dtype)

    @pl.when(j < N_DIL_TILES)
    def _():
        for s in range(N_SLABS):
            slab_ref[s] = res[:, s * LANES:(s + 1) * LANES]
        for out_ref, dil in ((x4_ref, 4), (x16_ref, 16)):
            for r in range(dil):
                for s in range(N_SLABS):
                    out_ref[0, r, :, s * LANES:(s + 1) * LANES] = (
                        slab_ref[s, pl.ds(r, IN_TM // dil, stride=dil), :].astype(out_ref.dtype))


def _in_proj(x2d, norm_w, w_p, col_scale, batch, seq):
    m = x2d.shape[0]
    tiles_per_seq = seq // IN_TM
    last = N_DIL_TILES - 1
    grouped = lambda dil: pl.BlockSpec(
        (1, dil, IN_TM // dil, IN_TN),
        lambda i, j: (i // tiles_per_seq, 0, i % tiles_per_seq, jnp.minimum(j, last)))
    return pl.pallas_call(
        _in_proj_kernel,
        grid=(m // IN_TM, ROW_W // IN_TN),
        in_specs=[
            pl.BlockSpec((IN_TM, D_MODEL), lambda i, j: (i, 0)),
            pl.BlockSpec((1, D_MODEL), lambda i, j: (0, 0)),
            pl.BlockSpec((D_MODEL, IN_TN), lambda i, j: (0, j)),
            pl.BlockSpec((1, IN_TN), lambda i, j: (0, j)),
        ],
        out_specs=[pl.BlockSpec((IN_TM, IN_TN), lambda i, j: (i, j)), grouped(4), grouped(16)],
        out_shape=[
            jax.ShapeDtypeStruct((m, ROW_W), BF16),
            jax.ShapeDtypeStruct((batch, 4, seq // 4, N_DIL_TILES * D_DIL), BF16),
            jax.ShapeDtypeStruct((batch, 16, seq // 16, N_DIL_TILES * D_DIL), BF16),
        ],
        scratch_shapes=[pltpu.VMEM((IN_TM, D_MODEL), BF16), pltpu.VMEM((N_SLABS, IN_TM, LANES), F32)],
        compiler_params=_cparams(("arbitrary", "arbitrary")),
        name="in_proj",
    )(x2d, norm_w, w_p, col_scale)


LOG2E = math.log2(math.e)


def _pair_attention(q_pair, k_pair, v_pair, bias_e, bias_o):
    rows = q_pair.shape[0]
    lo_q = lax.broadcasted_iota(jnp.int32, q_pair.shape, 1) < HEAD_DIM
    zero = jnp.zeros_like(q_pair)
    q_both = jnp.concatenate([jnp.where(lo_q, q_pair, zero), jnp.where(lo_q, zero, q_pair)], axis=0)
    s = lax.dot_general(q_both, k_pair, (((1,), (1,)), ((), ())), preferred_element_type=F32)
    s_e, s_o = s[:rows], s[rows:]
    if bias_e is not None:
        s_e = s_e + bias_e
        s_o = s_o + bias_o
    mx_e = jnp.max(s_e, axis=-1, keepdims=True)
    mx_o = jnp.max(s_o, axis=-1, keepdims=True)
    p_e = jnp.exp2(s_e - mx_e).astype(BF16)
    p_o = jnp.exp2(s_o - mx_o).astype(BF16)
    lo_v = lax.broadcasted_iota(jnp.int32, v_pair.shape, 1) < HEAD_DIM
    one = jnp.ones_like(v_pair)
    r_e = jnp.dot(p_e, jnp.where(lo_v, v_pair, one), preferred_element_type=F32)
    r_o = jnp.dot(p_o, jnp.where(lo_v, one, v_pair), preferred_element_type=F32)
    return r_e, r_o, mx_e, mx_o


def _stat_lane(head):
    return head + HEAD_DIM if head % 2 == 0 else head


def _dil_kernel(q_ref, kp_ref, kc_ref, vp_ref, vc_ref, rbh_ref, rbl_ref, sel_ref, mrow_ref,
                o_ref, mx_ref, den_ref, bias_ref, acc_ref, *, dil):
    b, lb, r = pl.program_id(0), pl.program_id(1), pl.program_id(2)

    @pl.when((b == 0) & (lb == 0) & (r == 0))
    def _():
        f = (jnp.dot(rbh_ref[...], sel_ref[...], preferred_element_type=F32)
             + jnp.dot(rbl_ref[...], sel_ref[...], preferred_element_type=F32)) * LOG2E + mrow_ref[...]
        col = lax.broadcasted_iota(jnp.int32, (DIL_BLOCK, 2 * DIL_BLOCK), 1)
        for h in range(N_HEADS_DIL):
            row = jnp.broadcast_to(f[h:h + 1, :], (DIL_BLOCK, 2 * DIL_BLOCK))
            tab = pltpu.roll(row, 0, 1, stride=1, stride_axis=0)
            bias_ref[0, h] = tab
            bias_ref[1, h] = jnp.where(col >= DIL_BLOCK, tab, NEG_INF)

    first = (lb == 0).astype(jnp.int32)
    lane = lax.broadcasted_iota(jnp.int32, (DIL_BLOCK, LANES), 1)
    lo = lane < HEAD_DIM
    mx_c = jnp.zeros((DIL_BLOCK, LANES), F32)
    den_c = jnp.ones((DIL_BLOCK, LANES), F32)
    for j in range(N_HEADS_DIL // 2):
        sl = slice(j * LANES, (j + 1) * LANES)
        k_pair = jnp.concatenate([kp_ref[:, sl], kc_ref[:, sl]], axis=0)
        v_pair = jnp.concatenate([vp_ref[:, sl], vc_ref[:, sl]], axis=0)
        r_e, r_o, mx_e, mx_o = _pair_attention(q_ref[:, sl], k_pair, v_pair,
                                               bias_ref[first, 2 * j], bias_ref[first, 2 * j + 1])
        num = jnp.where(lo, r_e, r_o)
        if dil == 1:
            o_ref[:, sl] = num.astype(o_ref.dtype)
        else:
            acc_ref[j, pl.ds(r, DIL_BLOCK, stride=dil), :] = num
        is_e, is_o = lane == _stat_lane(2 * j), lane == _stat_lane(2 * j + 1)
        mx_c = jnp.where(is_e, mx_e, jnp.where(is_o, mx_o, mx_c))
        den_c = jnp.where(is_e, r_e, jnp.where(is_o, r_o, den_c))
    if dil == 1:
        mx_ref[...] = mx_c
        den_ref[...] = den_c
    else:
        mx_ref[pl.ds(r, DIL_BLOCK, stride=dil), :] = mx_c
        den_ref[pl.ds(r, DIL_BLOCK, stride=dil), :] = den_c

        @pl.when(r == dil - 1)
        def _():
            for j in range(N_HEADS_DIL // 2):
                o_ref[:, j * LANES:(j + 1) * LANES] = acc_ref[j].astype(o_ref.dtype)


def _t5_bucket_np(dist):
    max_exact = REL_BUCKETS // 2
    n = np.maximum(dist, 1).astype(np.float32)
    large = max_exact + (np.log(n / np.float32(max_exact)) / np.float32(math.log(REL_MAX_DIST / max_exact))
                         * np.float32(REL_BUCKETS - max_exact)).astype(np.int32)
    large = np.minimum(large, REL_BUCKETS - 1)
    return np.where(dist < max_exact, dist, large)


def _bias_selector(window, dil):
    back = window // dil
    x = np.arange(2 * DIL_BLOCK)
    rel = DIL_BLOCK - x
    valid = (rel >= 0) & (rel <= back)
    bucket = _t5_bucket_np(np.clip(rel, 0, back) * dil)
    sel = np.zeros((LANES, 2 * DIL_BLOCK), np.float32)
    sel[bucket[valid], x[valid]] = 1.0
    mrow = np.where(valid, 0.0, NEG_INF).astype(np.float32)[None]
    return jnp.asarray(sel, BF16), jnp.asarray(mrow)


def _dilated(src, col0, rb_hi, rb_lo, batch, seq, window, dil):
    sub_len = seq // dil
    nb = sub_len // DIL_BLOCK
    sel, mrow = _bias_selector(window, dil)
    rows = DIL_BLOCK * dil

    def spec(col, prev):
        if prev:
            return pl.BlockSpec((None, None, DIL_BLOCK, D_DIL),
                                lambda b, l, r: (b, r, jnp.maximum(l - 1, 0), col0 + col))
        return pl.BlockSpec((None, None, DIL_BLOCK, D_DIL), lambda b, l, r: (b, r, l, col0 + col))

    const = lambda shape: pl.BlockSpec(shape, lambda b, l, r: (0, 0))
    stat_spec = pl.BlockSpec((rows, LANES), lambda b, l, r: (b * nb + l, 0))
    stat_shape = jax.ShapeDtypeStruct((batch * seq, LANES), F32)
    return pl.pallas_call(
        functools.partial(_dil_kernel, dil=dil),
        grid=(batch, nb, dil),
        in_specs=[
            spec(0, False), spec(1, True), spec(1, False), spec(2, True), spec(2, False),
            const(rb_hi.shape), const(rb_lo.shape), const(sel.shape), const(mrow.shape),
        ],
        out_specs=[pl.BlockSpec((rows, D_DIL), lambda b, l, r: (b * nb + l, 0)), stat_spec, stat_spec],
        out_shape=[jax.ShapeDtypeStruct((batch * seq, D_DIL), BF16), stat_shape, stat_shape],
        scratch_shapes=[
            pltpu.VMEM((2, N_HEADS_DIL, DIL_BLOCK, 2 * DIL_BLOCK), F32),
            pltpu.VMEM((N_HEADS_DIL // 2, rows, LANES), F32),
        ],
        compiler_params=_cparams(("arbitrary", "arbitrary", "arbitrary")),
        name=f"dilated_d{dil}",
    )(src, src, src, src, src, rb_hi, rb_lo, sel, mrow)


GLA_TM = 256


def _gla_kernel(q_ref, k_ref, v_ref, g_ref, w2h_ref, w2l_ref, bg_ref, nw_ref, o_ref, st_ref):
    @pl.when(pl.program_id(1) == 0)
    def _():
        st_ref[...] = jnp.zeros_like(st_ref)

    g = g_ref[:, :LANES]
    z = (jnp.dot(g, w2h_ref[...], preferred_element_type=F32)
         + jnp.dot(g, w2l_ref[...], preferred_element_type=F32) + bg_ref[...])
    la = (jnp.minimum(z, 0.0) - jnp.log1p(jnp.exp(-jnp.abs(z)))) * (1.0 / GLA_TAU)
    la_h = la.astype(BF16)
    la_l = (la - la_h.astype(F32)).astype(BF16)

    r = lax.broadcasted_iota(jnp.int32, (GLA_TM, GLA_TM), 0)
    c = lax.broadcasted_iota(jnp.int32, (GLA_TM, GLA_TM), 1)
    same = (r // GLA_CHUNK) == (c // GLA_CHUNK)
    tri = jnp.where(same & (c <= r), 1.0, 0.0).astype(BF16)
    blk = jnp.where(same, 1.0, 0.0).astype(BF16)
    b = jnp.dot(tri, la_h, preferred_element_type=F32) + jnp.dot(tri, la_l, preferred_element_type=F32)
    b_last = jnp.dot(blk, la_h, preferred_element_type=F32) + jnp.dot(blk, la_l, preferred_element_type=F32)

    q = q_ref[...].astype(F32)
    k = k_ref[...].astype(F32)
    q_t = (q * jnp.exp(b)).astype(BF16)
    k_t = (k * jnp.exp(-b)).astype(BF16)
    k_s = k * jnp.exp(b_last - b)

    causal = (lax.broadcasted_iota(jnp.int32, (GLA_CHUNK, GLA_CHUNK), 1)
              <= lax.broadcasted_iota(jnp.int32, (GLA_CHUNK, GLA_CHUNK), 0))
    nw = nw_ref[...]
    for h in range(N_HEADS_GLA):
        cs = slice(h * GLA_DK, (h + 1) * GLA_DK)
        vs = slice(h * GLA_DV, (h + 1) * GLA_DV)
        k_s_t = k_s[:, cs].T.astype(BF16)
        b_last_t = b_last[:, cs].T
        for ci in range(GLA_TM // GLA_CHUNK):
            rs = slice(ci * GLA_CHUNK, (ci + 1) * GLA_CHUNK)
            v_c = v_ref[rs, vs]
            q_c = q_t[rs, cs]
            a = lax.dot_general(q_c, k_t[rs, cs], (((1,), (1,)), ((), ())), preferred_element_type=F32)
            a = jnp.where(causal, a, 0.0).astype(BF16)
            st = st_ref[h]
            o = (jnp.dot(a, v_c, preferred_element_type=F32)
                 + jnp.dot(q_c, st.astype(BF16), preferred_element_type=F32))
            kv = jnp.dot(k_s_t[:, rs], v_c, preferred_element_type=F32)
            decay = jnp.exp(jnp.broadcast_to(b_last_t[:, ci * GLA_CHUNK:ci * GLA_CHUNK + 1], (GLA_DK, GLA_DV)))
            st_ref[h] = decay * st + kv
            ms = jnp.mean(o * o, axis=-1, keepdims=True)
            o_ref[rs, vs] = (o * lax.rsqrt(ms + EPS) * nw).astype(o_ref.dtype)


def _gla(proj, w2_hi, w2_lo, b_gate, norm_w, batch, seq):
    steps = seq // GLA_TM
    row = lambda b, i: b * steps + i
    return pl.pallas_call(
        _gla_kernel,
        grid=(batch, steps),
        in_specs=[
            pl.BlockSpec((GLA_TM, D_GLA_K), lambda b, i: (row(b, i), COL_GQ // D_GLA_K)),
            pl.BlockSpec((GLA_TM, D_GLA_K), lambda b, i: (row(b, i), COL_GK // D_GLA_K)),
            pl.BlockSpec((GLA_TM, D_GLA_V), lambda b, i: (row(b, i), COL_GV // D_GLA_V)),
            pl.BlockSpec((GLA_TM, GLR_PAD), lambda b, i: (row(b, i), COL_GLR // GLR_PAD)),
            pl.BlockSpec((LANES, D_GLA_K), lambda b, i: (0, 0)),
            pl.BlockSpec((LANES, D_GLA_K), lambda b, i: (0, 0)),
            pl.BlockSpec((1, D_GLA_K), lambda b, i: (0, 0)),
            pl.BlockSpec((1, GLA_DV), lambda b, i: (0, 0)),
        ],
        out_specs=pl.BlockSpec((GLA_TM, D_GLA_V), lambda b, i: (row(b, i), 0)),
        out_shape=jax.ShapeDtypeStruct((batch * seq, D_GLA_V), BF16),
        scratch_shapes=[pltpu.VMEM((N_HEADS_GLA, GLA_DK, GLA_DV), F32)],
        compiler_params=_cparams(("parallel", "arbitrary")),
        name="gla",
    )(proj, proj, proj, proj, w2_hi, w2_lo, b_gate, norm_w)


MEM_TM = 512


def _mem_kernel(q_ref, mem_ref, mnw_ref, wkv_ref, o_ref, kv_ref):
    @pl.when(pl.program_id(1) == 0)
    def _():
        m = mem_ref[0]
        ms = jnp.mean(m * m, axis=-1, keepdims=True)
        mn = (m * lax.rsqrt(ms + EPS) * mnw_ref[...]).astype(BF16)
        kv_ref[...] = jnp.dot(mn, wkv_ref[...], preferred_element_type=F32).astype(BF16)

    for j in range(N_HEADS_MEM // 2):
        sl = slice(j * LANES, (j + 1) * LANES)
        r_e, r_o, _, _ = _pair_attention(q_ref[:, sl], kv_ref[:, sl],
                                         kv_ref[:, D_MEM + j * LANES:D_MEM + (j + 1) * LANES], None, None)
        lo = lax.broadcasted_iota(jnp.int32, r_e.shape, 1) < HEAD_DIM
        den = jnp.where(lo, pltpu.roll(r_e, HEAD_DIM, 1), pltpu.roll(r_o, HEAD_DIM, 1))
        o_ref[:, sl] = (jnp.where(lo, r_e, r_o) / den).astype(o_ref.dtype)


def _mem_attn(proj, mem, mem_norm_w, w_kv, batch, seq):
    steps = seq // MEM_TM
    return pl.pallas_call(
        _mem_kernel,
        grid=(batch, steps),
        in_specs=[
            pl.BlockSpec((MEM_TM, D_MEM), lambda b, i: (b * steps + i, COL_MQ // D_MEM)),
            pl.BlockSpec((1, MEM_LEN, D_MODEL), lambda b, i: (b, 0, 0)),
            pl.BlockSpec((1, D_MODEL), lambda b, i: (0, 0)),
            pl.BlockSpec((D_MODEL, 2 * D_MEM), lambda b, i: (0, 0)),
        ],
        out_specs=pl.BlockSpec((MEM_TM, D_MEM), lambda b, i: (b * steps + i, 0)),
        out_shape=jax.ShapeDtypeStruct((batch * seq, D_MEM), BF16),
        scratch_shapes=[pltpu.VMEM((MEM_LEN, 2 * D_MEM), BF16)],
        compiler_params=_cparams(("parallel", "arbitrary")),
        name="mem_attn",
    )(proj, mem, mem_norm_w, w_kv)


OUT_TM = 512


def _silu(g):
    return g / (1.0 + jnp.exp(-g))


def _out_kernel(n1_ref, n2_ref, n3_ref, m1_ref, m2_ref, m3_ref, d1_ref, d2_ref, d3_ref, og_ref, om_ref,
                ga_ref, gg_ref, gm_ref, ex_ref, w_ref, x_ref, nw_ref, out_ref):
    m1, m2, m3 = m1_ref[...], m2_ref[...], m3_ref[...]
    mx = jnp.maximum(jnp.maximum(m1, m2), m3)
    t1, t2, t3 = jnp.exp2(m1 - mx), jnp.exp2(m2 - mx), jnp.exp2(m3 - mx)
    inv = 1.0 / (t1 * d1_ref[...] + t2 * d2_ref[...] + t3 * d3_ref[...])
    ex = ex_ref[...]
    a = jnp.zeros((OUT_TM, D_DIL), F32)
    for t, n_ref in ((t1, n1_ref), (t2, n2_ref), (t3, n3_ref)):
        w = jnp.dot((t * inv).astype(BF16), ex, preferred_element_type=F32)
        a = a + w * n_ref[...].astype(F32)
    mix_a = (a * _silu(ga_ref[...].astype(F32))).astype(BF16)
    mix_g = (og_ref[...].astype(F32) * _silu(gg_ref[...].astype(F32))).astype(BF16)
    mix_m = (om_ref[...].astype(F32) * _silu(gm_ref[...].astype(F32))).astype(BF16)
    y = (jnp.dot(mix_a, w_ref[0:D_DIL, :], preferred_element_type=F32)
         + jnp.dot(mix_g, w_ref[D_DIL:D_DIL + D_GLA_V, :], preferred_element_type=F32)
         + jnp.dot(mix_m, w_ref[D_DIL + D_GLA_V:D_MIX, :], preferred_element_type=F32))
    ms = jnp.mean(y * y, axis=-1, keepdims=True)
    out_ref[...] = x_ref[...] + y * lax.rsqrt(ms + EPS) * nw_ref[...]


def _out_proj(pats, o_gla, o_mem, proj, expand, w_out, x2d, norm_w):
    m = x2d.shape[0]
    rows = lambda width: pl.BlockSpec((OUT_TM, width), lambda i: (i, 0))
    const = lambda shape: pl.BlockSpec(shape, lambda i: (0, 0), pipeline_mode=pl.Buffered(1))
    nums, mxs, dens = zip(*pats)
    return pl.pallas_call(
        _out_kernel,
        grid=(m // OUT_TM,),
        in_specs=[
            *([rows(D_DIL)] * 3), *([rows(LANES)] * 6),
            rows(D_GLA_V), rows(D_MEM),
            pl.BlockSpec((OUT_TM, D_DIL), lambda i: (i, COL_GATE_A // D_DIL)),
            pl.BlockSpec((OUT_TM, D_GLA_V), lambda i: (i, COL_GATE_G // D_GLA_V)),
            pl.BlockSpec((OUT_TM, D_MEM), lambda i: (i, COL_GATE_M // D_MEM)),
            const((LANES, D_DIL)), const((D_MIX, D_MODEL)),
            rows(D_MODEL), const((1, D_MODEL)),
        ],
        out_specs=rows(D_MODEL),
        out_shape=jax.ShapeDtypeStruct((m, D_MODEL), x2d.dtype),
        compiler_params=_cparams(("parallel",)),
        name="out_proj",
    )(*nums, *mxs, *dens, o_gla, o_mem, proj, proj, proj, expand, w_out, x2d, norm_w)


D_IN_PROJ = 3 * D_DIL + 2 * D_GLA_K + D_GLA_V + GLA_GATE_RANK + D_MEM + D_MIX
SRC_GLR = 3 * D_DIL + 2 * D_GLA_K + D_GLA_V
SRC_MQ = SRC_GLR + GLA_GATE_RANK
SRC_GATE = SRC_MQ + D_MEM
W_SEGMENTS = (
    (0, 3 * D_DIL, COL_DQ),
    (SRC_GATE, D_DIL, COL_GATE_A),
    (3 * D_DIL, 2 * D_GLA_K + D_GLA_V, COL_GQ),
    (SRC_GATE + D_DIL, D_GLA_V, COL_GATE_G),
    (SRC_MQ, D_MEM, COL_MQ),
    (SRC_GATE + D_DIL + D_GLA_V, D_MEM, COL_GATE_M),
)
WPREP_TK = 512


def _wprep_kernel(wt_ref, o_ref):
    for src, width, dst in W_SEGMENTS:
        for c in range(width // LANES):
            lo = c * LANES
            o_ref[:, dst + lo:dst + lo + LANES] = wt_ref[src + lo:src + lo + LANES, :].T.astype(o_ref.dtype)
    glr = wt_ref[SRC_GLR:SRC_GLR + LANES, :].T
    keep = lax.broadcasted_iota(jnp.int32, glr.shape, 1) < GLA_GATE_RANK
    o_ref[:, COL_GLR:COL_GLR + LANES] = jnp.where(keep, glr, 0.0).astype(o_ref.dtype)
    o_ref[:, COL_GLR + LANES:ROW_W] = jnp.zeros((WPREP_TK, ROW_W - COL_GLR - LANES), o_ref.dtype)


def _regroup_in_weights(w_in):
    w_p = pl.pallas_call(
        _wprep_kernel,
        grid=(D_MODEL // WPREP_TK,),
        in_specs=[pl.BlockSpec((D_IN_PROJ, WPREP_TK), lambda i: (0, i))],
        out_specs=pl.BlockSpec((WPREP_TK, ROW_W), lambda i: (i, 0)),
        out_shape=jax.ShapeDtypeStruct((D_MODEL, ROW_W), BF16),
        compiler_params=_cparams(("parallel",)),
        name="weight_regroup",
    )(w_in.T)
    scale = np.ones((1, ROW_W), np.float32)
    scale[:, COL_DQ:COL_DQ + D_DIL] = HEAD_DIM ** -0.5 * LOG2E
    scale[:, COL_GQ:COL_GQ + D_GLA_K] = GLA_DK ** -0.5
    scale[:, COL_MQ:COL_MQ + D_MEM] = HEAD_DIM ** -0.5 * LOG2E
    return w_p, jnp.asarray(scale)


def _split_hi_lo(w):
    hi = w.astype(BF16)
    return hi, (w - hi.astype(F32)).astype(BF16)


def kernel(x, mem, norm_pre_w, w_in, rel_bias, w_gla_gate2, b_gla_gate, gla_norm_w, mem_norm_w, w_mem_kv,
           w_out, norm_post_w):
    batch, seq, _ = x.shape
    depth = w_in.shape[0]
    expand_np = np.zeros((LANES, D_DIL), np.float32)
    for h in range(N_HEADS_DIL):
        expand_np[_stat_lane(h), h * HEAD_DIM:(h + 1) * HEAD_DIM] = 1.0
    expand = jnp.asarray(expand_np, BF16)
    rb_t = jnp.zeros((16, LANES), F32).at[:N_HEADS_DIL, :REL_BUCKETS].set(rel_bias.astype(F32).T)
    rb_hi, rb_lo = _split_hi_lo(rb_t)
    for l in range(depth):
        x2d = x.reshape(batch * seq, D_MODEL)
        w_p, col_scale = _regroup_in_weights(w_in[l])
        proj, qkv4, qkv16 = _in_proj(x2d, norm_pre_w[l][None], w_p, col_scale, batch, seq)

        srcs = (proj.reshape(batch, 1, seq, ROW_W), qkv4, qkv16)
        pats = [_dilated(src, COL_DQ // D_DIL, rb_hi, rb_lo, batch, seq, window, dil)
                for (window, dil), src in zip(DIL_PATTERNS, srcs)]

        w2 = jnp.zeros((LANES, D_GLA_K), F32).at[:GLA_GATE_RANK].set(w_gla_gate2[l])
        w2_hi, w2_lo = _split_hi_lo(w2)
        o_gla = _gla(proj, w2_hi, w2_lo, b_gla_gate[l][None], gla_norm_w[l][None], batch, seq)

        o_mem = _mem_attn(proj, mem, mem_norm_w[l][None], w_mem_kv[l].astype(BF16), batch, seq)

        out = _out_proj(pats, o_gla, o_mem, proj, expand, w_out[l].astype(BF16), x2d, norm_post_w[l][None])
        x = out.reshape(batch, seq, D_MODEL)
    return x
```

```python
import functools
import math

import numpy as np
import jax
import jax.numpy as jnp
from jax import lax
from jax.experimental import pallas as pl
from jax.experimental.pallas import tpu as pltpu

F32 = jnp.float32
BF16 = jnp.bfloat16

D_MODEL = 2048
HEAD_DIM = 64
N_HEADS_DIL = 12
DIL_PATTERNS = ((128, 1), (512, 4), (2048, 16))
DIL_BLOCK = 128
N_HEADS_GLA = 4
GLA_DK = 128
GLA_DV = 256
GLA_GATE_RANK = 16
GLA_TAU = 16.0
GLA_CHUNK = 64
N_HEADS_MEM = 4
MEM_LEN = 256
REL_BUCKETS = 32
REL_MAX_DIST = 2048
EPS = 1e-6
NEG_INF = -1e30

D_DIL = N_HEADS_DIL * HEAD_DIM
D_GLA_K = N_HEADS_GLA * GLA_DK
D_GLA_V = N_HEADS_GLA * GLA_DV
D_MEM = N_HEADS_MEM * HEAD_DIM
D_MIX = D_DIL + D_GLA_V + D_MEM

LANES = 128
GLR_PAD = 256

COL_DQ = 0
COL_DK = COL_DQ + D_DIL
COL_DV = COL_DK + D_DIL
COL_GATE_A = COL_DV + D_DIL
COL_GQ = COL_GATE_A + D_DIL
COL_GK = COL_GQ + D_GLA_K
COL_GV = COL_GK + D_GLA_K
COL_GATE_G = COL_GV + D_GLA_V
COL_MQ = COL_GATE_G + D_GLA_V
COL_GATE_M = COL_MQ + D_MEM
COL_GLR = COL_GATE_M + D_MEM
ROW_W = COL_GLR + GLR_PAD

VMEM_LIMIT = 56 * 1024 * 1024


def _cparams(sem):
    return pltpu.CompilerParams(dimension_semantics=sem, vmem_limit_bytes=VMEM_LIMIT)


IN_TM = 1024
IN_TN = D_DIL
N_DIL_TILES = 3
N_SLABS = IN_TN // LANES


def _in_proj_kernel(x_ref, nw_ref, w_ref, sc_ref, o_ref, x4_ref, x16_ref, h_ref, slab_ref):
    j = pl.program_id(1)

    @pl.when(j == 0)
    def _():
        x = x_ref[...]
        ms = jnp.mean(x * x, axis=-1, keepdims=True)
        h_ref[...] = (x * lax.rsqrt(ms + EPS) * nw_ref[...]).astype(BF16)

    res = jnp.dot(h_ref[...], w_ref[...], preferred_element_type=F32) * sc_ref[...]
    o_ref[...] = res.astype(o_ref.dtype)

    @pl.when(j < N_DIL_TILES)
    def _():
        for s in range(N_SLABS):
            slab_ref[s] = res[:, s * LANES:(s + 1) * LANES]
        for out_ref, dil in ((x4_ref, 4), (x16_ref, 16)):
            for r in range(dil):
                for s in range(N_SLABS):
                    out_ref[0, r, :, s * LANES:(s + 1) * LANES] = (
                        slab_ref[s, pl.ds(r, IN_TM // dil, stride=dil), :].astype(out_ref.dtype))


def _in_proj(x2d, norm_w, w_p, col_scale, batch, seq):
    m = x2d.shape[0]
    tiles_per_seq = seq // IN_TM
    last = N_DIL_TILES - 1
    grouped = lambda dil: pl.BlockSpec(
        (1, dil, IN_TM // dil, IN_TN),
        lambda i, j: (i // tiles_per_seq, 0, i % tiles_per_seq, jnp.minimum(j, last)))
    return pl.pallas_call(
        _in_proj_kernel,
        grid=(m // IN_TM, ROW_W // IN_TN),
        in_specs=[
            pl.BlockSpec((IN_TM, D_MODEL), lambda i, j: (i, 0)),
            pl.BlockSpec((1, D_MODEL), lambda i, j: (0, 0)),
            pl.BlockSpec((D_MODEL, IN_TN), lambda i, j: (0, j)),
            pl.BlockSpec((1, IN_TN), lambda i, j: (0, j)),
        ],
        out_specs=[pl.BlockSpec((IN_TM, IN_TN), lambda i, j: (i, j)), grouped(4), grouped(16)],
        out_shape=[
            jax.ShapeDtypeStruct((m, ROW_W), BF16),
            jax.ShapeDtypeStruct((batch, 4, seq // 4, N_DIL_TILES * D_DIL), BF16),
            jax.ShapeDtypeStruct((batch, 16, seq // 16, N_DIL_TILES * D_DIL), BF16),
        ],
        scratch_shapes=[pltpu.VMEM((IN_TM, D_MODEL), BF16), pltpu.VMEM((N_SLABS, IN_TM, LANES), F32)],
        compiler_params=_cparams(("arbitrary", "arbitrary")),
        name="in_proj",
    )(x2d, norm_w, w_p, col_scale)


LOG2E = math.log2(math.e)


def _pair_attention(q_pair, k_pair, v_pair, bias_e, bias_o):
    rows = q_pair.shape[0]
    lo_q = lax.broadcasted_iota(jnp.int32, q_pair.shape, 1) < HEAD_DIM
    zero = jnp.zeros_like(q_pair)
    q_both = jnp.concatenate([jnp.where(lo_q, q_pair, zero), jnp.where(lo_q, zero, q_pair)], axis=0)
    s = lax.dot_general(q_both, k_pair, (((1,), (1,)), ((), ())), preferred_element_type=F32)
    s_e, s_o = s[:rows], s[rows:]
    if bias_e is not None:
        s_e = s_e + bias_e
        s_o = s_o + bias_o
    mx_e = jnp.max(s_e, axis=-1, keepdims=True)
    mx_o = jnp.max(s_o, axis=-1, keepdims=True)
    p_e = jnp.exp2(s_e - mx_e).astype(BF16)
    p_o = jnp.exp2(s_o - mx_o).astype(BF16)
    lo_v = lax.broadcasted_iota(jnp.int32, v_pair.shape, 1) < HEAD_DIM
    one = jnp.ones_like(v_pair)
    r_e = jnp.dot(p_e, jnp.where(lo_v, v_pair, one), preferred_element_type=F32)
    r_o = jnp.dot(p_o, jnp.where(lo_v, one, v_pair), preferred_element_type=F32)
    return r_e, r_o, mx_e, mx_o


def _stat_lane(head):
    return head + HEAD_DIM if head % 2 == 0 else head


def _dil_kernel(q_ref, kp_ref, kc_ref, vp_ref, vc_ref, rbh_ref, rbl_ref, sel_ref, mrow_ref,
                o_ref, mx_ref, den_ref, bias_ref, acc_ref, *, dil):
    b, lb, r = pl.program_id(0), pl.program_id(1), pl.program_id(2)

    @pl.when((b == 0) & (lb == 0) & (r == 0))
    def _():
        f = (jnp.dot(rbh_ref[...], sel_ref[...], preferred_element_type=F32)
             + jnp.dot(rbl_ref[...], sel_ref[...], preferred_element_type=F32)) * LOG2E + mrow_ref[...]
        col = lax.broadcasted_iota(jnp.int32, (DIL_BLOCK, 2 * DIL_BLOCK), 1)
        for h in range(N_HEADS_DIL):
            row = jnp.broadcast_to(f[h:h + 1, :], (DIL_BLOCK, 2 * DIL_BLOCK))
            tab = pltpu.roll(row, 0, 1, stride=1, stride_axis=0)
            bias_ref[0, h] = tab
            bias_ref[1, h] = jnp.where(col >= DIL_BLOCK, tab, NEG_INF)

    first = (lb == 0).astype(jnp.int32)
    lane = lax.broadcasted_iota(jnp.int32, (DIL_BLOCK, LANES), 1)
    lo = lane < HEAD_DIM
    mx_c = jnp.zeros((DIL_BLOCK, LANES), F32)
    den_c = jnp.ones((DIL_BLOCK, LANES), F32)
    for j in range(N_HEADS_DIL // 2):
        sl = slice(j * LANES, (j + 1) * LANES)
        k_pair = jnp.concatenate([kp_ref[:, sl], kc_ref[:, sl]], axis=0)
        v_pair = jnp.concatenate([vp_ref[:, sl], vc_ref[:, sl]], axis=0)
        r_e, r_o, mx_e, mx_o = _pair_attention(q_ref[:, sl], k_pair, v_pair,
                                               bias_ref[first, 2 * j], bias_ref[first, 2 * j + 1])
        num = jnp.where(lo, r_e, r_o)
        if dil == 1:
            o_ref[:, sl] = num.astype(o_ref.dtype)
        else:
            acc_ref[j, pl.ds(r, DIL_BLOCK, stride=dil), :] = num
        is_e, is_o = lane == _stat_lane(2 * j), lane == _stat_lane(2 * j + 1)
        mx_c = jnp.where(is_e, mx_e, jnp.where(is_o, mx_o, mx_c))
        den_c = jnp.where(is_e, r_e, jnp.where(is_o, r_o, den_c))
    if dil == 1:
        mx_ref[...] = mx_c
        den_ref[...] = den_c
    else:
        mx_ref[pl.ds(r, DIL_BLOCK, stride=dil), :] = mx_c
        den_ref[pl.ds(r, DIL_BLOCK, stride=dil), :] = den_c

        @pl.when(r == dil - 1)
        def _():
            for j in range(N_HEADS_DIL // 2):
                o_ref[:, j * LANES:(j + 1) * LANES] = acc_ref[j].astype(o_ref.dtype)


def _t5_bucket_np(dist):
    max_exact = REL_BUCKETS // 2
    n = np.maximum(dist, 1).astype(np.float32)
    large = max_exact + (np.log(n / np.float32(max_exact)) / np.float32(math.log(REL_MAX_DIST / max_exact))
                         * np.float32(REL_BUCKETS - max_exact)).astype(np.int32)
    large = np.minimum(large, REL_BUCKETS - 1)
    return np.where(dist < max_exact, dist, large)


def _bias_selector(window, dil):
    back = window // dil
    x = np.arange(2 * DIL_BLOCK)
    rel = DIL_BLOCK - x
    valid = (rel >= 0) & (rel <= back)
    bucket = _t5_bucket_np(np.clip(rel, 0, back) * dil)
    sel = np.zeros((LANES, 2 * DIL_BLOCK), np.float32)
    sel[bucket[valid], x[valid]] = 1.0
    mrow = np.where(valid, 0.0, NEG_INF).astype(np.float32)[None]
    return jnp.asarray(sel, BF16), jnp.asarray(mrow)


def _dilated(src, col0, rb_hi, rb_lo, batch, seq, window, dil):
    sub_len = seq // dil
    nb = sub_len // DIL_BLOCK
    sel, mrow = _bias_selector(window, dil)
    rows = DIL_BLOCK * dil

    def spec(col, prev):
        if prev:
            return pl.BlockSpec((None, None, DIL_BLOCK, D_DIL),
                                lambda b, l, r: (b, r, jnp.maximum(l - 1, 0), col0 + col))
        return pl.BlockSpec((None, None, DIL_BLOCK, D_DIL), lambda b, l, r: (b, r, l, col0 + col))

    const = lambda shape: pl.BlockSpec(shape, lambda b, l, r: (0, 0))
    stat_spec = pl.BlockSpec((rows, LANES), lambda b, l, r: (b * nb + l, 0))
    stat_shape = jax.ShapeDtypeStruct((batch * seq, LANES), F32)
    return pl.pallas_call(
        functools.partial(_dil_kernel, dil=dil),
        grid=(batch, nb, dil),
        in_specs=[
            spec(0, False), spec(1, True), spec(1, False), spec(2, True), spec(2, False),
            const(rb_hi.shape), const(rb_lo.shape), const(sel.shape), const(mrow.shape),
        ],
        out_specs=[pl.BlockSpec((rows, D_DIL), lambda b, l, r: (b * nb + l, 0)), stat_spec, stat_spec],
        out_shape=[jax.ShapeDtypeStruct((batch * seq, D_DIL), BF16), stat_shape, stat_shape],
        scratch_shapes=[
            pltpu.VMEM((2, N_HEADS_DIL, DIL_BLOCK, 2 * DIL_BLOCK), F32),
            pltpu.VMEM((N_HEADS_DIL // 2, rows, LANES), F32),
        ],
        compiler_params=_cparams(("arbitrary", "arbitrary", "arbitrary")),
        name=f"dilated_d{dil}",
    )(src, src, src, src, src, rb_hi, rb_lo, sel, mrow)


GLA_TM = 256


GLA_NC = GLA_TM // GLA_CHUNK


def _gla_tile(bb, q_ref, k_ref, v_ref, g_ref, w2h_ref, w2l_ref, bg_ref, nw_ref, tri_ref, o_ref, st_ref):
    g = g_ref[bb, :, :LANES]
    z = (jnp.dot(g, w2h_ref[...], preferred_element_type=F32)
         + jnp.dot(g, w2l_ref[...], preferred_element_type=F32) + bg_ref[...])
    la = (jnp.minimum(z, 0.0) - jnp.log(1.0 + jnp.exp(-jnp.abs(z)))) * (1.0 / GLA_TAU)
    la_h = la.astype(BF16)
    la_l = (la - la_h.astype(F32)).astype(BF16)

    tri = tri_ref[...]
    b = jnp.dot(tri, la_h, preferred_element_type=F32) + jnp.dot(tri, la_l, preferred_element_type=F32)
    b_last = jnp.concatenate(
        [jnp.broadcast_to(b[(ci + 1) * GLA_CHUNK - 1:(ci + 1) * GLA_CHUNK, :], (GLA_CHUNK, D_GLA_K))
         for ci in range(GLA_NC)], axis=0)

    q = q_ref[bb].astype(F32)
    k = k_ref[bb].astype(F32)
    q_t = (q * jnp.exp(b)).astype(BF16)
    k_t = (k * jnp.exp(-b)).astype(BF16)
    k_s = k * jnp.exp(b_last - b)

    causal = tri.astype(F32) > 0.0
    t_chunk = lax.broadcasted_iota(jnp.int32, (GLA_DK, GLA_TM), 1) // GLA_CHUNK
    nw = nw_ref[...]
    for h in range(N_HEADS_GLA):
        cs = slice(h * GLA_DK, (h + 1) * GLA_DK)
        vs = slice(h * GLA_DV, (h + 1) * GLA_DV)
        v_h = v_ref[bb, :, vs]
        a = lax.dot_general(q_t[:, cs], k_t[:, cs], (((1,), (1,)), ((), ())), preferred_element_type=F32)
        o = jnp.dot(jnp.where(causal, a, 0.0).astype(BF16), v_h, preferred_element_type=F32)
        k_s_t = k_s[:, cs].T.astype(BF16)
        zero = jnp.zeros_like(k_s_t)
        kv_all = jnp.dot(jnp.concatenate([jnp.where(t_chunk == ci, k_s_t, zero) for ci in range(GLA_NC)], axis=0),
                         v_h, preferred_element_type=F32)
        decay_t = jnp.exp(b_last[:, cs].T)
        st = st_ref[bb, h]
        outs = []
        for ci in range(GLA_NC):
            rs = slice(ci * GLA_CHUNK, (ci + 1) * GLA_CHUNK)
            o_c = o[rs] + jnp.dot(q_t[rs, cs], st.astype(BF16), preferred_element_type=F32)
            decay = jnp.broadcast_to(decay_t[:, ci * GLA_CHUNK:ci * GLA_CHUNK + 1], (GLA_DK, GLA_DV))
            st = decay * st + kv_all[ci * GLA_DK:(ci + 1) * GLA_DK]
            ms = jnp.mean(o_c * o_c, axis=-1, keepdims=True)
            outs.append((o_c * lax.rsqrt(ms + EPS) * nw).astype(o_ref.dtype))
        st_ref[bb, h] = st
        o_ref[bb, :, vs] = jnp.concatenate(outs, axis=0)


def _gla_kernel(*refs):
    st_ref = refs[-1]

    @pl.when(pl.program_id(0) == 0)
    def _():
        st_ref[...] = jnp.zeros_like(st_ref)

    for bb in range(st_ref.shape[0]):
        _gla_tile(bb, *refs)


def _gla(proj, w2_hi, w2_lo, b_gate, norm_w, batch, seq):
    pos = np.arange(GLA_TM)
    tri = jnp.asarray((pos[:, None] // GLA_CHUNK == pos[None, :] // GLA_CHUNK) & (pos[None, :] <= pos[:, None]),
                      BF16)
    proj3 = proj.reshape(batch, seq, ROW_W)
    rows = lambda width, col: pl.BlockSpec((batch, GLA_TM, width), lambda i: (0, i, col // width))
    const = lambda shape: pl.BlockSpec(shape, lambda i: (0, 0))
    out = pl.pallas_call(
        _gla_kernel,
        grid=(seq // GLA_TM,),
        in_specs=[
            rows(D_GLA_K, COL_GQ), rows(D_GLA_K, COL_GK), rows(D_GLA_V, COL_GV), rows(GLR_PAD, COL_GLR),
            const((LANES, D_GLA_K)), const((LANES, D_GLA_K)), const((1, D_GLA_K)), const((1, GLA_DV)),
            const((GLA_TM, GLA_TM)),
        ],
        out_specs=rows(D_GLA_V, 0),
        out_shape=jax.ShapeDtypeStruct((batch, seq, D_GLA_V), BF16),
        scratch_shapes=[pltpu.VMEM((batch, N_HEADS_GLA, GLA_DK, GLA_DV), F32)],
        compiler_params=_cparams(("arbitrary",)),
        name="gla",
    )(proj3, proj3, proj3, proj3, w2_hi, w2_lo, b_gate, norm_w, tri)
    return out.reshape(batch * seq, D_GLA_V)


MEM_TM = 512


def _mem_kernel(q_ref, mem_ref, mnw_ref, wkv_ref, o_ref, kv_ref):
    @pl.when(pl.program_id(1) == 0)
    def _():
        m = mem_ref[0]
        ms = jnp.mean(m * m, axis=-1, keepdims=True)
        mn = (m * lax.rsqrt(ms + EPS) * mnw_ref[...]).astype(BF16)
        kv_ref[...] = jnp.dot(mn, wkv_ref[...], preferred_element_type=F32).astype(BF16)

    for j in range(N_HEADS_MEM // 2):
        sl = slice(j * LANES, (j + 1) * LANES)
        r_e, r_o, _, _ = _pair_attention(q_ref[:, sl], kv_ref[:, sl],
                                         kv_ref[:, D_MEM + j * LANES:D_MEM + (j + 1) * LANES], None, None)
        lo = lax.broadcasted_iota(jnp.int32, r_e.shape, 1) < HEAD_DIM
        den = jnp.where(lo, pltpu.roll(r_e, HEAD_DIM, 1), pltpu.roll(r_o, HEAD_DIM, 1))
        o_ref[:, sl] = (jnp.where(lo, r_e, r_o) / den).astype(o_ref.dtype)


def _mem_attn(proj, mem, mem_norm_w, w_kv, batch, seq):
    steps = seq // MEM_TM
    return pl.pallas_call(
        _mem_kernel,
        grid=(batch, steps),
        in_specs=[
            pl.BlockSpec((MEM_TM, D_MEM), lambda b, i: (b * steps + i, COL_MQ // D_MEM)),
            pl.BlockSpec((1, MEM_LEN, D_MODEL), lambda b, i: (b, 0, 0)),
            pl.BlockSpec((1, D_MODEL), lambda b, i: (0, 0)),
            pl.BlockSpec((D_MODEL, 2 * D_MEM), lambda b, i: (0, 0)),
        ],
        out_specs=pl.BlockSpec((MEM_TM, D_MEM), lambda b, i: (b * steps + i, 0)),
        out_shape=jax.ShapeDtypeStruct((batch * seq, D_MEM), BF16),
        scratch_shapes=[pltpu.VMEM((MEM_LEN, 2 * D_MEM), BF16)],
        compiler_params=_cparams(("parallel", "arbitrary")),
        name="mem_attn",
    )(proj, mem, mem_norm_w, w_kv)


OUT_TM = 512


def _silu(g):
    return g / (1.0 + jnp.exp(-g))


def _out_kernel(n1_ref, n2_ref, n3_ref, m1_ref, m2_ref, m3_ref, d1_ref, d2_ref, d3_ref, og_ref, om_ref,
                ga_ref, gg_ref, gm_ref, ex_ref, w_ref, x_ref, nw_ref, out_ref):
    m1, m2, m3 = m1_ref[...], m2_ref[...], m3_ref[...]
    mx = jnp.maximum(jnp.maximum(m1, m2), m3)
    t1, t2, t3 = jnp.exp2(m1 - mx), jnp.exp2(m2 - mx), jnp.exp2(m3 - mx)
    inv = 1.0 / (t1 * d1_ref[...] + t2 * d2_ref[...] + t3 * d3_ref[...])
    ex = ex_ref[...]
    a = jnp.zeros((OUT_TM, D_DIL), F32)
    for t, n_ref in ((t1, n1_ref), (t2, n2_ref), (t3, n3_ref)):
        w = jnp.dot((t * inv).astype(BF16), ex, preferred_element_type=F32)
        a = a + w * n_ref[...].astype(F32)
    mix_a = (a * _silu(ga_ref[...].astype(F32))).astype(BF16)
    mix_g = (og_ref[...].astype(F32) * _silu(gg_ref[...].astype(F32))).astype(BF16)
    mix_m = (om_ref[...].astype(F32) * _silu(gm_ref[...].astype(F32))).astype(BF16)
    y = (jnp.dot(mix_a, w_ref[0:D_DIL, :], preferred_element_type=F32)
         + jnp.dot(mix_g, w_ref[D_DIL:D_DIL + D_GLA_V, :], preferred_element_type=F32)
         + jnp.dot(mix_m, w_ref[D_DIL + D_GLA_V:D_MIX, :], preferred_element_type=F32))
    ms = jnp.mean(y * y, axis=-1, keepdims=True)
    out_ref[...] = x_ref[...] + y * lax.rsqrt(ms + EPS) * nw_ref[...]


def _out_proj(pats, o_gla, o_mem, proj, expand, w_out, x2d, norm_w):
    m = x2d.shape[0]
    rows = lambda width: pl.BlockSpec((OUT_TM, width), lambda i: (i, 0))
    const = lambda shape: pl.BlockSpec(shape, lambda i: (0, 0), pipeline_mode=pl.Buffered(1))
    nums, mxs, dens = zip(*pats)
    return pl.pallas_call(
        _out_kernel,
        grid=(m // OUT_TM,),
        in_specs=[
            *([rows(D_DIL)] * 3), *([rows(LANES)] * 6),
            rows(D_GLA_V), rows(D_MEM),
            pl.BlockSpec((OUT_TM, D_DIL), lambda i: (i, COL_GATE_A // D_DIL)),
            pl.BlockSpec((OUT_TM, D_GLA_V), lambda i: (i, COL_GATE_G // D_GLA_V)),
            pl.BlockSpec((OUT_TM, D_MEM), lambda i: (i, COL_GATE_M // D_MEM)),
            const((LANES, D_DIL)), const((D_MIX, D_MODEL)),
            rows(D_MODEL), const((1, D_MODEL)),
        ],
        out_specs=rows(D_MODEL),
        out_shape=jax.ShapeDtypeStruct((m, D_MODEL), x2d.dtype),
        compiler_params=_cparams(("parallel",)),
        name="out_proj",
    )(*nums, *mxs, *dens, o_gla, o_mem, proj, proj, proj, expand, w_out, x2d, norm_w)


D_IN_PROJ = 3 * D_DIL + 2 * D_GLA_K + D_GLA_V + GLA_GATE_RANK + D_MEM + D_MIX
SRC_GLR = 3 * D_DIL + 2 * D_GLA_K + D_GLA_V
SRC_MQ = SRC_GLR + GLA_GATE_RANK
SRC_GATE = SRC_MQ + D_MEM
W_SEGMENTS = (
    (0, 3 * D_DIL, COL_DQ),
    (SRC_GATE, D_DIL, COL_GATE_A),
    (3 * D_DIL, 2 * D_GLA_K + D_GLA_V, COL_GQ),
    (SRC_GATE + D_DIL, D_GLA_V, COL_GATE_G),
    (SRC_MQ, D_MEM, COL_MQ),
    (SRC_GATE + D_DIL + D_GLA_V, D_MEM, COL_GATE_M),
)
WPREP_TK = 512


def _wprep_kernel(wt_ref, o_ref):
    for src, width, dst in W_SEGMENTS:
        for c in range(width // LANES):
            lo = c * LANES
            o_ref[:, dst + lo:dst + lo + LANES] = wt_ref[src + lo:src + lo + LANES, :].T.astype(o_ref.dtype)
    glr = wt_ref[SRC_GLR:SRC_GLR + LANES, :].T
    keep = lax.broadcasted_iota(jnp.int32, glr.shape, 1) < GLA_GATE_RANK
    o_ref[:, COL_GLR:COL_GLR + LANES] = jnp.where(keep, glr, 0.0).astype(o_ref.dtype)
    o_ref[:, COL_GLR + LANES:ROW_W] = jnp.zeros((WPREP_TK, ROW_W - COL_GLR - LANES), o_ref.dtype)


def _regroup_in_weights(w_in):
    w_p = pl.pallas_call(
        _wprep_kernel,
        grid=(D_MODEL // WPREP_TK,),
        in_specs=[pl.BlockSpec((D_IN_PROJ, WPREP_TK), lambda i: (0, i))],
        out_specs=pl.BlockSpec((WPREP_TK, ROW_W), lambda i: (i, 0)),
        out_shape=jax.ShapeDtypeStruct((D_MODEL, ROW_W), BF16),
        compiler_params=_cparams(("parallel",)),
        name="weight_regroup",
    )(w_in.T)
    scale = np.ones((1, ROW_W), np.float32)
    scale[:, COL_DQ:COL_DQ + D_DIL] = HEAD_DIM ** -0.5 * LOG2E
    scale[:, COL_GQ:COL_GQ + D_GLA_K] = GLA_DK ** -0.5
    scale[:, COL_MQ:COL_MQ + D_MEM] = HEAD_DIM ** -0.5 * LOG2E
    return w_p, jnp.asarray(scale)


def _split_hi_lo(w):
    hi = w.astype(BF16)
    return hi, (w - hi.astype(F32)).astype(BF16)


def kernel(x, mem, norm_pre_w, w_in, rel_bias, w_gla_gate2, b_gla_gate, gla_norm_w, mem_norm_w, w_mem_kv,
           w_out, norm_post_w):
    batch, seq, _ = x.shape
    depth = w_in.shape[0]
    expand_np = np.zeros((LANES, D_DIL), np.float32)
    for h in range(N_HEADS_DIL):
        expand_np[_stat_lane(h), h * HEAD_DIM:(h + 1) * HEAD_DIM] = 1.0
    expand = jnp.asarray(expand_np, BF16)
    rb_t = jnp.zeros((16, LANES), F32).at[:N_HEADS_DIL, :REL_BUCKETS].set(rel_bias.astype(F32).T)
    rb_hi, rb_lo = _split_hi_lo(rb_t)
    for l in range(depth):
        x2d = x.reshape(batch * seq, D_MODEL)
        w_p, col_scale = _regroup_in_weights(w_in[l])
        proj, qkv4, qkv16 = _in_proj(x2d, norm_pre_w[l][None], w_p, col_scale, batch, seq)

        srcs = (proj.reshape(batch, 1, seq, ROW_W), qkv4, qkv16)
        pats = [_dilated(src, COL_DQ // D_DIL, rb_hi, rb_lo, batch, seq, window, dil)
                for (window, dil), src in zip(DIL_PATTERNS, srcs)]

        w2 = jnp.zeros((LANES, D_GLA_K), F32).at[:GLA_GATE_RANK].set(w_gla_gate2[l])
        w2_hi, w2_lo = _split_hi_lo(w2)
        o_gla = _gla(proj, w2_hi, w2_lo, b_gla_gate[l][None], gla_norm_w[l][None], batch, seq)

        o_mem = _mem_attn(proj, mem, mem_norm_w[l][None], w_mem_kv[l].astype(BF16), batch, seq)

        out = _out_proj(pats, o_gla, o_mem, proj, expand, w_out[l].astype(BF16), x2d, norm_post_w[l][None])
        x = out.reshape(batch, seq, D_MODEL)
    return x
```

```python
import functools
import math

import numpy as np
import jax
import jax.numpy as jnp
from jax import lax
from jax.experimental import pallas as pl
from jax.experimental.pallas import tpu as pltpu

F32 = jnp.float32
BF16 = jnp.bfloat16

D_MODEL = 2048
HEAD_DIM = 64
N_HEADS_DIL = 12
DIL_PATTERNS = ((128, 1), (512, 4), (2048, 16))
DIL_BLOCK = 128
N_HEADS_GLA = 4
GLA_DK = 128
GLA_DV = 256
GLA_GATE_RANK = 16
GLA_TAU = 16.0
GLA_CHUNK = 64
N_HEADS_MEM = 4
MEM_LEN = 256
REL_BUCKETS = 32
REL_MAX_DIST = 2048
EPS = 1e-6
NEG_INF = -1e30

D_DIL = N_HEADS_DIL * HEAD_DIM
D_GLA_K = N_HEADS_GLA * GLA_DK
D_GLA_V = N_HEADS_GLA * GLA_DV
D_MEM = N_HEADS_MEM * HEAD_DIM
D_MIX = D_DIL + D_GLA_V + D_MEM

LANES = 128
GLR_PAD = 256

QKV_W = 3 * D_DIL
COL_GQ = 0
COL_GK = COL_GQ + D_GLA_K
COL_GV = COL_GK + D_GLA_K
COL_GATE_G = COL_GV + D_GLA_V
COL_MQ = COL_GATE_G + D_GLA_V
COL_GATE_M = COL_MQ + D_MEM
COL_GLR = COL_GATE_M + D_MEM
COL_GATE_A = COL_GLR + GLR_PAD
REST_W = COL_GATE_A + D_DIL

VMEM_LIMIT = 56 * 1024 * 1024


def _cparams(sem):
    return pltpu.CompilerParams(dimension_semantics=sem, vmem_limit_bytes=VMEM_LIMIT)


QKV_TM = 512
N_SLABS = D_DIL // LANES
REGROUP_DILS = (4, 16)


def _qkv_proj_kernel(x_ref, nw_ref, w_ref, sc_ref, h_ref, o_ref, x4_ref, x16_ref, slab_ref):
    x = x_ref[...]
    ms = jnp.mean(x * x, axis=-1, keepdims=True)
    h = (x * lax.rsqrt(ms + EPS) * nw_ref[...]).astype(BF16)
    h_ref[...] = h
    for t in range(QKV_W // D_DIL):
        cols = slice(t * D_DIL, (t + 1) * D_DIL)
        res = jnp.dot(h, w_ref[:, cols], preferred_element_type=F32) * sc_ref[:, cols]
        o_ref[:, cols] = res.astype(o_ref.dtype)
        for s in range(N_SLABS):
            slab_ref[t, s] = res[:, s * LANES:(s + 1) * LANES]
        for out_ref, dil in zip((x4_ref, x16_ref), REGROUP_DILS):
            for r in range(dil):
                for s in range(N_SLABS):
                    lo = t * D_DIL + s * LANES
                    out_ref[0, r, :, lo:lo + LANES] = (
                        slab_ref[t, s, pl.ds(r, QKV_TM // dil, stride=dil), :].astype(out_ref.dtype))


def _qkv_proj(x2d, norm_w, w_qkv, scale, batch, seq):
    m = x2d.shape[0]
    tiles_per_seq = seq // QKV_TM
    const = lambda shape: pl.BlockSpec(shape, lambda i: (0, 0), pipeline_mode=pl.Buffered(1))
    grouped = lambda dil: pl.BlockSpec((1, dil, QKV_TM // dil, QKV_W),
                                       lambda i: (i // tiles_per_seq, 0, i % tiles_per_seq, 0))
    return pl.pallas_call(
        _qkv_proj_kernel,
        grid=(m // QKV_TM,),
        in_specs=[
            pl.BlockSpec((QKV_TM, D_MODEL), lambda i: (i, 0)),
            const((1, D_MODEL)), const((D_MODEL, QKV_W)), const((1, QKV_W)),
        ],
        out_specs=[pl.BlockSpec((QKV_TM, D_MODEL), lambda i: (i, 0)),
                   pl.BlockSpec((QKV_TM, QKV_W), lambda i: (i, 0))] + [grouped(d) for d in REGROUP_DILS],
        out_shape=[jax.ShapeDtypeStruct((m, D_MODEL), BF16), jax.ShapeDtypeStruct((m, QKV_W), BF16)]
        + [jax.ShapeDtypeStruct((batch, d, seq // d, QKV_W), BF16) for d in REGROUP_DILS],
        scratch_shapes=[pltpu.VMEM((QKV_W // D_DIL, N_SLABS, QKV_TM, LANES), F32)],
        compiler_params=_cparams(("parallel",)),
        name="qkv_proj",
    )(x2d, norm_w, w_qkv, scale)


REST_TM = 1024
REST_TN = 1536


def _rest_proj_kernel(h_ref, w_ref, sc_ref, o_ref):
    acc = jnp.dot(h_ref[...], w_ref[...], preferred_element_type=F32)
    o_ref[...] = (acc * sc_ref[...]).astype(o_ref.dtype)


def _rest_proj(h, w_rest, scale):
    m = h.shape[0]
    return pl.pallas_call(
        _rest_proj_kernel,
        grid=(m // REST_TM, REST_W // REST_TN),
        in_specs=[
            pl.BlockSpec((REST_TM, D_MODEL), lambda i, j: (i, 0)),
            pl.BlockSpec((D_MODEL, REST_TN), lambda i, j: (0, j)),
            pl.BlockSpec((1, REST_TN), lambda i, j: (0, j)),
        ],
        out_specs=pl.BlockSpec((REST_TM, REST_TN), lambda i, j: (i, j)),
        out_shape=jax.ShapeDtypeStruct((m, REST_W), BF16),
        compiler_params=_cparams(("parallel", "arbitrary")),
        name="rest_proj",
    )(h, w_rest, scale)


LOG2E = math.log2(math.e)


def _pair_attention(q_pair, k_pair, v_pair, bias_e, bias_o):
    rows = q_pair.shape[0]
    lo_q = lax.broadcasted_iota(jnp.int32, q_pair.shape, 1) < HEAD_DIM
    zero = jnp.zeros_like(q_pair)
    q_both = jnp.concatenate([jnp.where(lo_q, q_pair, zero), jnp.where(lo_q, zero, q_pair)], axis=0)
    s = lax.dot_general(q_both, k_pair, (((1,), (1,)), ((), ())), preferred_element_type=F32)
    s_e, s_o = s[:rows], s[rows:]
    if bias_e is not None:
        s_e = s_e + bias_e
        s_o = s_o + bias_o
    mx_e = jnp.max(s_e, axis=-1, keepdims=True)
    mx_o = jnp.max(s_o, axis=-1, keepdims=True)
    p_e = jnp.exp2(s_e - mx_e).astype(BF16)
    p_o = jnp.exp2(s_o - mx_o).astype(BF16)
    lo_v = lax.broadcasted_iota(jnp.int32, v_pair.shape, 1) < HEAD_DIM
    one = jnp.ones_like(v_pair)
    r_e = jnp.dot(p_e, jnp.where(lo_v, v_pair, one), preferred_element_type=F32)
    r_o = jnp.dot(p_o, jnp.where(lo_v, one, v_pair), preferred_element_type=F32)
    return r_e, r_o, mx_e, mx_o


def _stat_lane(head):
    return head + HEAD_DIM if head % 2 == 0 else head


def _dil_kernel(q_ref, kp_ref, kc_ref, vp_ref, vc_ref, rbh_ref, rbl_ref, sel_ref, mrow_ref,
                o_ref, mx_ref, den_ref, bias_ref, acc_ref, *, dil):
    b, lb, r = pl.program_id(0), pl.program_id(1), pl.program_id(2)

    @pl.when((b == 0) & (lb == 0) & (r == 0))
    def _():
        f = (jnp.dot(rbh_ref[...], sel_ref[...], preferred_element_type=F32)
             + jnp.dot(rbl_ref[...], sel_ref[...], preferred_element_type=F32)) * LOG2E + mrow_ref[...]
        col = lax.broadcasted_iota(jnp.int32, (DIL_BLOCK, 2 * DIL_BLOCK), 1)
        for h in range(N_HEADS_DIL):
            row = jnp.broadcast_to(f[h:h + 1, :], (DIL_BLOCK, 2 * DIL_BLOCK))
            tab = pltpu.roll(row, 0, 1, stride=1, stride_axis=0)
            bias_ref[0, h] = tab
            bias_ref[1, h] = jnp.where(col >= DIL_BLOCK, tab, NEG_INF)

    first = (lb == 0).astype(jnp.int32)
    lane = lax.broadcasted_iota(jnp.int32, (DIL_BLOCK, LANES), 1)
    lo = lane < HEAD_DIM
    mx_c = jnp.zeros((DIL_BLOCK, LANES), F32)
    den_c = jnp.ones((DIL_BLOCK, LANES), F32)
    for j in range(N_HEADS_DIL // 2):
        sl = slice(j * LANES, (j + 1) * LANES)
        k_pair = jnp.concatenate([kp_ref[:, sl], kc_ref[:, sl]], axis=0)
        v_pair = jnp.concatenate([vp_ref[:, sl], vc_ref[:, sl]], axis=0)
        r_e, r_o, mx_e, mx_o = _pair_attention(q_ref[:, sl], k_pair, v_pair,
                                               bias_ref[first, 2 * j], bias_ref[first, 2 * j + 1])
        num = jnp.where(lo, r_e, r_o)
        if dil == 1:
            o_ref[:, sl] = num.astype(o_ref.dtype)
        else:
            acc_ref[j, pl.ds(r, DIL_BLOCK, stride=dil), :] = num
        is_e, is_o = lane == _stat_lane(2 * j), lane == _stat_lane(2 * j + 1)
        mx_c = jnp.where(is_e, mx_e, jnp.where(is_o, mx_o, mx_c))
        den_c = jnp.where(is_e, r_e, jnp.where(is_o, r_o, den_c))
    if dil == 1:
        mx_ref[...] = mx_c
        den_ref[...] = den_c
    else:
        mx_ref[pl.ds(r, DIL_BLOCK, stride=dil), :] = mx_c
        den_ref[pl.ds(r, DIL_BLOCK, stride=dil), :] = den_c

        @pl.when(r == dil - 1)
        def _():
            for j in range(N_HEADS_DIL // 2):
                o_ref[:, j * LANES:(j + 1) * LANES] = acc_ref[j].astype(o_ref.dtype)


def _t5_bucket_np(dist):
    max_exact = REL_BUCKETS // 2
    n = np.maximum(dist, 1).astype(np.float32)
    large = max_exact + (np.log(n / np.float32(max_exact)) / np.float32(math.log(REL_MAX_DIST / max_exact))
                         * np.float32(REL_BUCKETS - max_exact)).astype(np.int32)
    large = np.minimum(large, REL_BUCKETS - 1)
    return np.where(dist < max_exact, dist, large)


def _bias_selector(window, dil):
    back = window // dil
    x = np.arange(2 * DIL_BLOCK)
    rel = DIL_BLOCK - x
    valid = (rel >= 0) & (rel <= back)
    bucket = _t5_bucket_np(np.clip(rel, 0, back) * dil)
    sel = np.zeros((LANES, 2 * DIL_BLOCK), np.float32)
    sel[bucket[valid], x[valid]] = 1.0
    mrow = np.where(valid, 0.0, NEG_INF).astype(np.float32)[None]
    return jnp.asarray(sel, BF16), jnp.asarray(mrow)


def _dilated(src, rb_hi, rb_lo, batch, seq, window, dil):
    sub_len = seq // dil
    nb = sub_len // DIL_BLOCK
    sel, mrow = _bias_selector(window, dil)
    rows = DIL_BLOCK * dil

    def spec(col, prev):
        if prev:
            return pl.BlockSpec((None, None, DIL_BLOCK, D_DIL),
                                lambda b, l, r: (b, r, jnp.maximum(l - 1, 0), col))
        return pl.BlockSpec((None, None, DIL_BLOCK, D_DIL), lambda b, l, r: (b, r, l, col))

    const = lambda shape: pl.BlockSpec(shape, lambda b, l, r: (0, 0))
    stat_spec = pl.BlockSpec((rows, LANES), lambda b, l, r: (b * nb + l, 0))
    stat_shape = jax.ShapeDtypeStruct((batch * seq, LANES), F32)
    return pl.pallas_call(
        functools.partial(_dil_kernel, dil=dil),
        grid=(batch, nb, dil),
        in_specs=[
            spec(0, False), spec(1, True), spec(1, False), spec(2, True), spec(2, False),
            const(rb_hi.shape), const(rb_lo.shape), const(sel.shape), const(mrow.shape),
        ],
        out_specs=[pl.BlockSpec((rows, D_DIL), lambda b, l, r: (b * nb + l, 0)), stat_spec, stat_spec],
        out_shape=[jax.ShapeDtypeStruct((batch * seq, D_DIL), BF16), stat_shape, stat_shape],
        scratch_shapes=[
            pltpu.VMEM((2, N_HEADS_DIL, DIL_BLOCK, 2 * DIL_BLOCK), F32),
            pltpu.VMEM((N_HEADS_DIL // 2, rows, LANES), F32),
        ],
        compiler_params=_cparams(("arbitrary", "arbitrary", "arbitrary")),
        name=f"dilated_d{dil}",
    )(src, src, src, src, src, rb_hi, rb_lo, sel, mrow)


GLA_TM = 256


GLA_NC = GLA_TM // GLA_CHUNK


def _gla_tile(bb, q_ref, k_ref, v_ref, g_ref, w2h_ref, w2l_ref, bg_ref, nw_ref, tri_ref, o_ref, st_ref):
    g = g_ref[bb, :, :LANES]
    z = (jnp.dot(g, w2h_ref[...], preferred_element_type=F32)
         + jnp.dot(g, w2l_ref[...], preferred_element_type=F32) + bg_ref[...])
    la = (jnp.minimum(z, 0.0) - jnp.log(1.0 + jnp.exp(-jnp.abs(z)))) * (1.0 / GLA_TAU)
    la_h = la.astype(BF16)
    la_l = (la - la_h.astype(F32)).astype(BF16)

    tri = tri_ref[...]
    b = jnp.dot(tri, la_h, preferred_element_type=F32) + jnp.dot(tri, la_l, preferred_element_type=F32)
    b_last = jnp.concatenate(
        [jnp.broadcast_to(b[(ci + 1) * GLA_CHUNK - 1:(ci + 1) * GLA_CHUNK, :], (GLA_CHUNK, D_GLA_K))
         for ci in range(GLA_NC)], axis=0)

    q = q_ref[bb].astype(F32)
    k = k_ref[bb].astype(F32)
    q_t = (q * jnp.exp(b)).astype(BF16)
    k_t = (k * jnp.exp(-b)).astype(BF16)
    k_s = k * jnp.exp(b_last - b)

    causal = tri.astype(F32) > 0.0
    t_chunk = lax.broadcasted_iota(jnp.int32, (GLA_DK, GLA_TM), 1) // GLA_CHUNK
    nw = nw_ref[...]
    for h in range(N_HEADS_GLA):
        cs = slice(h * GLA_DK, (h + 1) * GLA_DK)
        vs = slice(h * GLA_DV, (h + 1) * GLA_DV)
        v_h = v_ref[bb, :, vs]
        a = lax.dot_general(q_t[:, cs], k_t[:, cs], (((1,), (1,)), ((), ())), preferred_element_type=F32)
        o = jnp.dot(jnp.where(causal, a, 0.0).astype(BF16), v_h, preferred_element_type=F32)
        k_s_t = k_s[:, cs].T.astype(BF16)
        zero = jnp.zeros_like(k_s_t)
        kv_all = jnp.dot(jnp.concatenate([jnp.where(t_chunk == ci, k_s_t, zero) for ci in range(GLA_NC)], axis=0),
                         v_h, preferred_element_type=F32)
        decay_t = jnp.exp(b_last[:, cs].T)
        st = st_ref[bb, h]
        outs = []
        for ci in range(GLA_NC):
            rs = slice(ci * GLA_CHUNK, (ci + 1) * GLA_CHUNK)
            o_c = o[rs] + jnp.dot(q_t[rs, cs], st.astype(BF16), preferred_element_type=F32)
            decay = jnp.broadcast_to(decay_t[:, ci * GLA_CHUNK:ci * GLA_CHUNK + 1], (GLA_DK, GLA_DV))
            st = decay * st + kv_all[ci * GLA_DK:(ci + 1) * GLA_DK]
            ms = jnp.mean(o_c * o_c, axis=-1, keepdims=True)
            outs.append((o_c * lax.rsqrt(ms + EPS) * nw).astype(o_ref.dtype))
        st_ref[bb, h] = st
        o_ref[bb, :, vs] = jnp.concatenate(outs, axis=0)


def _gla_kernel(*refs):
    st_ref = refs[-1]

    @pl.when(pl.program_id(0) == 0)
    def _():
        st_ref[...] = jnp.zeros_like(st_ref)

    for bb in range(st_ref.shape[0]):
        _gla_tile(bb, *refs)


def _gla(proj, w2_hi, w2_lo, b_gate, norm_w, batch, seq):
    pos = np.arange(GLA_TM)
    tri = jnp.asarray((pos[:, None] // GLA_CHUNK == pos[None, :] // GLA_CHUNK) & (pos[None, :] <= pos[:, None]),
                      BF16)
    proj3 = proj.reshape(batch, seq, REST_W)
    rows = lambda width, col: pl.BlockSpec((batch, GLA_TM, width), lambda i: (0, i, col // width))
    const = lambda shape: pl.BlockSpec(shape, lambda i: (0, 0))
    out = pl.pallas_call(
        _gla_kernel,
        grid=(seq // GLA_TM,),
        in_specs=[
            rows(D_GLA_K, COL_GQ), rows(D_GLA_K, COL_GK), rows(D_GLA_V, COL_GV), rows(GLR_PAD, COL_GLR),
            const((LANES, D_GLA_K)), const((LANES, D_GLA_K)), const((1, D_GLA_K)), const((1, GLA_DV)),
            const((GLA_TM, GLA_TM)),
        ],
        out_specs=rows(D_GLA_V, 0),
        out_shape=jax.ShapeDtypeStruct((batch, seq, D_GLA_V), BF16),
        scratch_shapes=[pltpu.VMEM((batch, N_HEADS_GLA, GLA_DK, GLA_DV), F32)],
        compiler_params=_cparams(("arbitrary",)),
        name="gla",
    )(proj3, proj3, proj3, proj3, w2_hi, w2_lo, b_gate, norm_w, tri)
    return out.reshape(batch * seq, D_GLA_V)


MEM_TM = 512


def _mem_kernel(q_ref, mem_ref, mnw_ref, wkv_ref, o_ref, kv_ref):
    @pl.when(pl.program_id(1) == 0)
    def _():
        m = mem_ref[0]
        ms = jnp.mean(m * m, axis=-1, keepdims=True)
        mn = (m * lax.rsqrt(ms + EPS) * mnw_ref[...]).astype(BF16)
        kv_ref[...] = jnp.dot(mn, wkv_ref[...], preferred_element_type=F32).astype(BF16)

    for j in range(N_HEADS_MEM // 2):
        sl = slice(j * LANES, (j + 1) * LANES)
        r_e, r_o, _, _ = _pair_attention(q_ref[:, sl], kv_ref[:, sl],
                                         kv_ref[:, D_MEM + j * LANES:D_MEM + (j + 1) * LANES], None, None)
        lo = lax.broadcasted_iota(jnp.int32, r_e.shape, 1) < HEAD_DIM
        den = jnp.where(lo, pltpu.roll(r_e, HEAD_DIM, 1), pltpu.roll(r_o, HEAD_DIM, 1))
        o_ref[:, sl] = (jnp.where(lo, r_e, r_o) / den).astype(o_ref.dtype)


def _mem_attn(proj, mem, mem_norm_w, w_kv, batch, seq):
    steps = seq // MEM_TM
    return pl.pallas_call(
        _mem_kernel,
        grid=(batch, steps),
        in_specs=[
            pl.BlockSpec((MEM_TM, D_MEM), lambda b, i: (b * steps + i, COL_MQ // D_MEM)),
            pl.BlockSpec((1, MEM_LEN, D_MODEL), lambda b, i: (b, 0, 0)),
            pl.BlockSpec((1, D_MODEL), lambda b, i: (0, 0)),
            pl.BlockSpec((D_MODEL, 2 * D_MEM), lambda b, i: (0, 0)),
        ],
        out_specs=pl.BlockSpec((MEM_TM, D_MEM), lambda b, i: (b * steps + i, 0)),
        out_shape=jax.ShapeDtypeStruct((batch * seq, D_MEM), BF16),
        scratch_shapes=[pltpu.VMEM((MEM_LEN, 2 * D_MEM), BF16)],
        compiler_params=_cparams(("parallel", "arbitrary")),
        name="mem_attn",
    )(proj, mem, mem_norm_w, w_kv)


OUT_TM = 512


def _silu(g):
    return g / (1.0 + jnp.exp(-g))


def _out_kernel(n1_ref, n2_ref, n3_ref, m1_ref, m2_ref, m3_ref, d1_ref, d2_ref, d3_ref, og_ref, om_ref,
                ga_ref, gg_ref, gm_ref, ex_ref, w_ref, x_ref, nw_ref, out_ref):
    m1, m2, m3 = m1_ref[...], m2_ref[...], m3_ref[...]
    mx = jnp.maximum(jnp.maximum(m1, m2), m3)
    t1, t2, t3 = jnp.exp2(m1 - mx), jnp.exp2(m2 - mx), jnp.exp2(m3 - mx)
    inv = 1.0 / (t1 * d1_ref[...] + t2 * d2_ref[...] + t3 * d3_ref[...])
    ex = ex_ref[...]
    a = jnp.zeros((OUT_TM, D_DIL), F32)
    for t, n_ref in ((t1, n1_ref), (t2, n2_ref), (t3, n3_ref)):
        w = jnp.dot((t * inv).astype(BF16), ex, preferred_element_type=F32)
        a = a + w * n_ref[...].astype(F32)
    mix_a = (a * _silu(ga_ref[...].astype(F32))).astype(BF16)
    mix_g = (og_ref[...].astype(F32) * _silu(gg_ref[...].astype(F32))).astype(BF16)
    mix_m = (om_ref[...].astype(F32) * _silu(gm_ref[...].astype(F32))).astype(BF16)
    y = (jnp.dot(mix_a, w_ref[0:D_DIL, :], preferred_element_type=F32)
         + jnp.dot(mix_g, w_ref[D_DIL:D_DIL + D_GLA_V, :], preferred_element_type=F32)
         + jnp.dot(mix_m, w_ref[D_DIL + D_GLA_V:D_MIX, :], preferred_element_type=F32))
    ms = jnp.mean(y * y, axis=-1, keepdims=True)
    out_ref[...] = x_ref[...] + y * lax.rsqrt(ms + EPS) * nw_ref[...]


def _out_proj(pats, o_gla, o_mem, proj, expand, w_out, x2d, norm_w):
    m = x2d.shape[0]
    rows = lambda width: pl.BlockSpec((OUT_TM, width), lambda i: (i, 0))
    const = lambda shape: pl.BlockSpec(shape, lambda i: (0, 0), pipeline_mode=pl.Buffered(1))
    nums, mxs, dens = zip(*pats)
    return pl.pallas_call(
        _out_kernel,
        grid=(m // OUT_TM,),
        in_specs=[
            *([rows(D_DIL)] * 3), *([rows(LANES)] * 6),
            rows(D_GLA_V), rows(D_MEM),
            pl.BlockSpec((OUT_TM, D_DIL), lambda i: (i, COL_GATE_A // D_DIL)),
            pl.BlockSpec((OUT_TM, D_GLA_V), lambda i: (i, COL_GATE_G // D_GLA_V)),
            pl.BlockSpec((OUT_TM, D_MEM), lambda i: (i, COL_GATE_M // D_MEM)),
            const((LANES, D_DIL)), const((D_MIX, D_MODEL)),
            rows(D_MODEL), const((1, D_MODEL)),
        ],
        out_specs=rows(D_MODEL),
        out_shape=jax.ShapeDtypeStruct((m, D_MODEL), x2d.dtype),
        compiler_params=_cparams(("parallel",)),
        name="out_proj",
    )(*nums, *mxs, *dens, o_gla, o_mem, proj, proj, proj, expand, w_out, x2d, norm_w)


D_IN_PROJ = 3 * D_DIL + 2 * D_GLA_K + D_GLA_V + GLA_GATE_RANK + D_MEM + D_MIX
SRC_GLR = 3 * D_DIL + 2 * D_GLA_K + D_GLA_V
SRC_MQ = SRC_GLR + GLA_GATE_RANK
SRC_GATE = SRC_MQ + D_MEM
REST_SEGMENTS = (
    (QKV_W, 2 * D_GLA_K + D_GLA_V, COL_GQ),
    (SRC_GATE + D_DIL, D_GLA_V, COL_GATE_G),
    (SRC_MQ, D_MEM, COL_MQ),
    (SRC_GATE + D_DIL + D_GLA_V, D_MEM, COL_GATE_M),
    (SRC_GATE, D_DIL, COL_GATE_A),
)
WPREP_TK = 512


def _wprep_kernel(wt_ref, qkv_ref, rest_ref):
    def put(o_ref, src, width, dst):
        for c in range(width // LANES):
            lo = c * LANES
            o_ref[:, dst + lo:dst + lo + LANES] = wt_ref[src + lo:src + lo + LANES, :].T.astype(o_ref.dtype)

    put(qkv_ref, 0, QKV_W, 0)
    for src, width, dst in REST_SEGMENTS:
        put(rest_ref, src, width, dst)
    glr = wt_ref[SRC_GLR:SRC_GLR + LANES, :].T
    keep = lax.broadcasted_iota(jnp.int32, glr.shape, 1) < GLA_GATE_RANK
    rest_ref[:, COL_GLR:COL_GLR + LANES] = jnp.where(keep, glr, 0.0).astype(rest_ref.dtype)
    rest_ref[:, COL_GLR + LANES:COL_GLR + GLR_PAD] = jnp.zeros((WPREP_TK, GLR_PAD - LANES), rest_ref.dtype)


def _regroup_in_weights(w_in):
    w_qkv, w_rest = pl.pallas_call(
        _wprep_kernel,
        grid=(D_MODEL // WPREP_TK,),
        in_specs=[pl.BlockSpec((D_IN_PROJ, WPREP_TK), lambda i: (0, i))],
        out_specs=[pl.BlockSpec((WPREP_TK, QKV_W), lambda i: (i, 0)),
                   pl.BlockSpec((WPREP_TK, REST_W), lambda i: (i, 0))],
        out_shape=[jax.ShapeDtypeStruct((D_MODEL, QKV_W), BF16), jax.ShapeDtypeStruct((D_MODEL, REST_W), BF16)],
        compiler_params=_cparams(("parallel",)),
        name="weight_regroup",
    )(w_in.T)
    scale_qkv = np.ones((1, QKV_W), np.float32)
    scale_qkv[:, :D_DIL] = HEAD_DIM ** -0.5 * LOG2E
    scale_rest = np.ones((1, REST_W), np.float32)
    scale_rest[:, COL_GQ:COL_GQ + D_GLA_K] = GLA_DK ** -0.5
    scale_rest[:, COL_MQ:COL_MQ + D_MEM] = HEAD_DIM ** -0.5 * LOG2E
    return w_qkv, w_rest, jnp.asarray(scale_qkv), jnp.asarray(scale_rest)


def _split_hi_lo(w):
    hi = w.astype(BF16)
    return hi, (w - hi.astype(F32)).astype(BF16)


def kernel(x, mem, norm_pre_w, w_in, rel_bias, w_gla_gate2, b_gla_gate, gla_norm_w, mem_norm_w, w_mem_kv,
           w_out, norm_post_w):
    batch, seq, _ = x.shape
    depth = w_in.shape[0]
    expand_np = np.zeros((LANES, D_DIL), np.float32)
    for h in range(N_HEADS_DIL):
        expand_np[_stat_lane(h), h * HEAD_DIM:(h + 1) * HEAD_DIM] = 1.0
    expand = jnp.asarray(expand_np, BF16)
    rb_t = jnp.zeros((16, LANES), F32).at[:N_HEADS_DIL, :REL_BUCKETS].set(rel_bias.astype(F32).T)
    rb_hi, rb_lo = _split_hi_lo(rb_t)
    for l in range(depth):
        x2d = x.reshape(batch * seq, D_MODEL)
        w_qkv, w_rest, scale_qkv, scale_rest = _regroup_in_weights(w_in[l])
        h, qkv1, qkv4, qkv16 = _qkv_proj(x2d, norm_pre_w[l][None], w_qkv, scale_qkv, batch, seq)
        rest = _rest_proj(h, w_rest, scale_rest)

        srcs = (qkv1.reshape(batch, 1, seq, QKV_W), qkv4, qkv16)
        pats = [_dilated(src, rb_hi, rb_lo, batch, seq, window, dil)
                for (window, dil), src in zip(DIL_PATTERNS, srcs)]

        w2 = jnp.zeros((LANES, D_GLA_K), F32).at[:GLA_GATE_RANK].set(w_gla_gate2[l])
        w2_hi, w2_lo = _split_hi_lo(w2)
        o_gla = _gla(rest, w2_hi, w2_lo, b_gla_gate[l][None], gla_norm_w[l][None], batch, seq)

        o_mem = _mem_attn(rest, mem, mem_norm_w[l][None], w_mem_kv[l].astype(BF16), batch, seq)

        out = _out_proj(pats, o_gla, o_mem, rest, expand, w_out[l].astype(BF16), x2d, norm_post_w[l][None])
        x = out.reshape(batch, seq, D_MODEL)
    return x
```

```python
import functools
import math

import numpy as np
import jax
import jax.numpy as jnp
from jax import lax
from jax.experimental import pallas as pl
from jax.experimental.pallas import tpu as pltpu

F32 = jnp.float32
BF16 = jnp.bfloat16

D_MODEL = 2048
HEAD_DIM = 64
N_HEADS_DIL = 12
DIL_PATTERNS = ((128, 1), (512, 4), (2048, 16))
DIL_BLOCK = 128
N_HEADS_GLA = 4
GLA_DK = 128
GLA_DV = 256
GLA_GATE_RANK = 16
GLA_TAU = 16.0
GLA_CHUNK = 64
N_HEADS_MEM = 4
MEM_LEN = 256
REL_BUCKETS = 32
REL_MAX_DIST = 2048
EPS = 1e-6
NEG_INF = -1e30

D_DIL = N_HEADS_DIL * HEAD_DIM
D_GLA_K = N_HEADS_GLA * GLA_DK
D_GLA_V = N_HEADS_GLA * GLA_DV
D_MEM = N_HEADS_MEM * HEAD_DIM
D_MIX = D_DIL + D_GLA_V + D_MEM

LANES = 128
GLR_PAD = 256

QKV_W = 3 * D_DIL
COL_GQ = 0
COL_GK = COL_GQ + D_GLA_K
COL_GV = COL_GK + D_GLA_K
COL_GATE_G = COL_GV + D_GLA_V
COL_MQ = COL_GATE_G + D_GLA_V
COL_GATE_M = COL_MQ + D_MEM
COL_GLR = COL_GATE_M + D_MEM
COL_GATE_A = COL_GLR + GLR_PAD
REST_W = COL_GATE_A + D_DIL

VMEM_LIMIT = 56 * 1024 * 1024


def _cparams(sem):
    return pltpu.CompilerParams(dimension_semantics=sem, vmem_limit_bytes=VMEM_LIMIT)


QKV_TM = 512
N_SLABS = D_DIL // LANES
REGROUP_DILS = (4, 16)


def _qkv_proj_kernel(x_ref, nw_ref, w_ref, sc_ref, h_ref, o_ref, x4_ref, x16_ref, slab_ref):
    x = x_ref[...]
    ms = jnp.mean(x * x, axis=-1, keepdims=True)
    h = (x * lax.rsqrt(ms + EPS) * nw_ref[...]).astype(BF16)
    h_ref[...] = h
    for t in range(QKV_W // D_DIL):
        cols = slice(t * D_DIL, (t + 1) * D_DIL)
        res = jnp.dot(h, w_ref[:, cols], preferred_element_type=F32) * sc_ref[:, cols]
        o_ref[:, cols] = res.astype(o_ref.dtype)
        for s in range(N_SLABS):
            slab_ref[t, s] = res[:, s * LANES:(s + 1) * LANES]
        for out_ref, dil in zip((x4_ref, x16_ref), REGROUP_DILS):
            for r in range(dil):
                for s in range(N_SLABS):
                    lo = t * D_DIL + s * LANES
                    out_ref[0, r, :, lo:lo + LANES] = (
                        slab_ref[t, s, pl.ds(r, QKV_TM // dil, stride=dil), :].astype(out_ref.dtype))


def _qkv_proj(x2d, norm_w, w_qkv, scale, batch, seq):
    m = x2d.shape[0]
    tiles_per_seq = seq // QKV_TM
    const = lambda shape: pl.BlockSpec(shape, lambda i: (0, 0), pipeline_mode=pl.Buffered(1))
    grouped = lambda dil: pl.BlockSpec((1, dil, QKV_TM // dil, QKV_W),
                                       lambda i: (i // tiles_per_seq, 0, i % tiles_per_seq, 0))
    return pl.pallas_call(
        _qkv_proj_kernel,
        grid=(m // QKV_TM,),
        in_specs=[
            pl.BlockSpec((QKV_TM, D_MODEL), lambda i: (i, 0)),
            const((1, D_MODEL)), const((D_MODEL, QKV_W)), const((1, QKV_W)),
        ],
        out_specs=[pl.BlockSpec((QKV_TM, D_MODEL), lambda i: (i, 0)),
                   pl.BlockSpec((QKV_TM, QKV_W), lambda i: (i, 0))] + [grouped(d) for d in REGROUP_DILS],
        out_shape=[jax.ShapeDtypeStruct((m, D_MODEL), BF16), jax.ShapeDtypeStruct((m, QKV_W), BF16)]
        + [jax.ShapeDtypeStruct((batch, d, seq // d, QKV_W), BF16) for d in REGROUP_DILS],
        scratch_shapes=[pltpu.VMEM((QKV_W // D_DIL, N_SLABS, QKV_TM, LANES), F32)],
        compiler_params=_cparams(("parallel",)),
        name="qkv_proj",
    )(x2d, norm_w, w_qkv, scale)


REST_TM = 1024
REST_TN = 1536


def _rest_proj_kernel(h_ref, w_ref, sc_ref, o_ref):
    acc = jnp.dot(h_ref[...], w_ref[...], preferred_element_type=F32)
    o_ref[...] = (acc * sc_ref[...]).astype(o_ref.dtype)


def _rest_proj(h, w_rest, scale):
    m = h.shape[0]
    return pl.pallas_call(
        _rest_proj_kernel,
        grid=(m // REST_TM, REST_W // REST_TN),
        in_specs=[
            pl.BlockSpec((REST_TM, D_MODEL), lambda i, j: (i, 0)),
            pl.BlockSpec((D_MODEL, REST_TN), lambda i, j: (0, j)),
            pl.BlockSpec((1, REST_TN), lambda i, j: (0, j)),
        ],
        out_specs=pl.BlockSpec((REST_TM, REST_TN), lambda i, j: (i, j)),
        out_shape=jax.ShapeDtypeStruct((m, REST_W), BF16),
        compiler_params=_cparams(("parallel", "arbitrary")),
        name="rest_proj",
    )(h, w_rest, scale)


LOG2E = math.log2(math.e)


def _pair_attention(q_pair, k_pair, v_pair, bias_e, bias_o):
    rows = q_pair.shape[0]
    lo_q = lax.broadcasted_iota(jnp.int32, q_pair.shape, 1) < HEAD_DIM
    zero = jnp.zeros_like(q_pair)
    q_both = jnp.concatenate([jnp.where(lo_q, q_pair, zero), jnp.where(lo_q, zero, q_pair)], axis=0)
    s = lax.dot_general(q_both, k_pair, (((1,), (1,)), ((), ())), preferred_element_type=F32)
    s_e, s_o = s[:rows], s[rows:]
    if bias_e is not None:
        s_e = s_e + bias_e
        s_o = s_o + bias_o
    mx_e = jnp.max(s_e, axis=-1, keepdims=True)
    mx_o = jnp.max(s_o, axis=-1, keepdims=True)
    p_e = jnp.exp2(s_e - mx_e).astype(BF16)
    p_o = jnp.exp2(s_o - mx_o).astype(BF16)
    lo_v = lax.broadcasted_iota(jnp.int32, v_pair.shape, 1) < HEAD_DIM
    one = jnp.ones_like(v_pair)
    r_e = jnp.dot(p_e, jnp.where(lo_v, v_pair, one), preferred_element_type=F32)
    r_o = jnp.dot(p_o, jnp.where(lo_v, one, v_pair), preferred_element_type=F32)
    return r_e, r_o, mx_e, mx_o


def _stat_lane(head):
    return head + HEAD_DIM if head % 2 == 0 else head


DIL_UNITS = {1: 4, 4: 4, 16: 4}


def _dil_init_bias(rbh_ref, rbl_ref, sel_ref, mrow_ref, bias_ref):
    f = (jnp.dot(rbh_ref[...], sel_ref[...], preferred_element_type=F32)
         + jnp.dot(rbl_ref[...], sel_ref[...], preferred_element_type=F32)) * LOG2E + mrow_ref[...]
    col = lax.broadcasted_iota(jnp.int32, (DIL_BLOCK, 2 * DIL_BLOCK), 1)
    for h in range(N_HEADS_DIL):
        row = jnp.broadcast_to(f[h:h + 1, :], (DIL_BLOCK, 2 * DIL_BLOCK))
        tab = pltpu.roll(row, 0, 1, stride=1, stride_axis=0)
        bias_ref[0, h] = tab
        bias_ref[1, h] = jnp.where(col >= DIL_BLOCK, tab, NEG_INF)


def _dil_unit(q, kp, kc, vp, vc, first, bias_ref, put_num):
    lane = lax.broadcasted_iota(jnp.int32, (DIL_BLOCK, LANES), 1)
    lo = lane < HEAD_DIM
    mx_c = jnp.zeros((DIL_BLOCK, LANES), F32)
    den_c = jnp.ones((DIL_BLOCK, LANES), F32)
    for j in range(N_HEADS_DIL // 2):
        sl = slice(j * LANES, (j + 1) * LANES)
        k_pair = jnp.concatenate([kp(sl), kc(sl)], axis=0)
        v_pair = jnp.concatenate([vp(sl), vc(sl)], axis=0)
        r_e, r_o, mx_e, mx_o = _pair_attention(q(sl), k_pair, v_pair,
                                               bias_ref[first, 2 * j], bias_ref[first, 2 * j + 1])
        put_num(j, jnp.where(lo, r_e, r_o))
        is_e, is_o = lane == _stat_lane(2 * j), lane == _stat_lane(2 * j + 1)
        mx_c = jnp.where(is_e, mx_e, jnp.where(is_o, mx_o, mx_c))
        den_c = jnp.where(is_e, r_e, jnp.where(is_o, r_o, den_c))
    return mx_c, den_c


def _dil1_kernel(q_ref, kp_ref, kc_ref, vp_ref, vc_ref, rbh_ref, rbl_ref, sel_ref, mrow_ref,
                 o_ref, mx_ref, den_ref, bias_ref):
    @pl.when((pl.program_id(0) == 0) & (pl.program_id(1) == 0))
    def _():
        _dil_init_bias(rbh_ref, rbl_ref, sel_ref, mrow_ref, bias_ref)

    for u in range(DIL_UNITS[1]):
        rows = slice(u * DIL_BLOCK, (u + 1) * DIL_BLOCK)
        prev = slice((u - 1) * DIL_BLOCK, u * DIL_BLOCK)
        if u == 0:
            first = (pl.program_id(1) == 0).astype(jnp.int32)
            kp, vp = (lambda sl: kp_ref[:, sl]), (lambda sl: vp_ref[:, sl])
        else:
            first = 0
            kp, vp = (lambda sl, p=prev: kc_ref[p, sl]), (lambda sl, p=prev: vc_ref[p, sl])

        def put_num(j, num, rows=rows):
            o_ref[rows, j * LANES:(j + 1) * LANES] = num.astype(o_ref.dtype)

        mx_c, den_c = _dil_unit(lambda sl, r=rows: q_ref[r, sl], kp, lambda sl, r=rows: kc_ref[r, sl],
                                vp, lambda sl, r=rows: vc_ref[r, sl], first, bias_ref, put_num)
        mx_ref[rows, :] = mx_c
        den_ref[rows, :] = den_c


def _dil_kernel(q_ref, kp_ref, kc_ref, vp_ref, vc_ref, rbh_ref, rbl_ref, sel_ref, mrow_ref,
                o_ref, mx_ref, den_ref, bias_ref, acc_ref, *, dil):
    b, lb, g = pl.program_id(0), pl.program_id(1), pl.program_id(2)
    nr = DIL_UNITS[dil]

    @pl.when((b == 0) & (lb == 0) & (g == 0))
    def _():
        _dil_init_bias(rbh_ref, rbl_ref, sel_ref, mrow_ref, bias_ref)

    first = (lb == 0).astype(jnp.int32)
    for u in range(nr):
        rows = pl.ds(g * nr + u, DIL_BLOCK, stride=dil)

        def put_num(j, num, rows=rows):
            acc_ref[j, rows, :] = num

        mx_c, den_c = _dil_unit(lambda sl, u=u: q_ref[u, :, sl], lambda sl, u=u: kp_ref[u, :, sl],
                                lambda sl, u=u: kc_ref[u, :, sl], lambda sl, u=u: vp_ref[u, :, sl],
                                lambda sl, u=u: vc_ref[u, :, sl], first, bias_ref, put_num)
        mx_ref[rows, :] = mx_c
        den_ref[rows, :] = den_c

    @pl.when(g == dil // nr - 1)
    def _():
        for j in range(N_HEADS_DIL // 2):
            o_ref[:, j * LANES:(j + 1) * LANES] = acc_ref[j].astype(o_ref.dtype)


def _t5_bucket_np(dist):
    max_exact = REL_BUCKETS // 2
    n = np.maximum(dist, 1).astype(np.float32)
    large = max_exact + (np.log(n / np.float32(max_exact)) / np.float32(math.log(REL_MAX_DIST / max_exact))
                         * np.float32(REL_BUCKETS - max_exact)).astype(np.int32)
    large = np.minimum(large, REL_BUCKETS - 1)
    return np.where(dist < max_exact, dist, large)


def _bias_selector(window, dil):
    back = window // dil
    x = np.arange(2 * DIL_BLOCK)
    rel = DIL_BLOCK - x
    valid = (rel >= 0) & (rel <= back)
    bucket = _t5_bucket_np(np.clip(rel, 0, back) * dil)
    sel = np.zeros((LANES, 2 * DIL_BLOCK), np.float32)
    sel[bucket[valid], x[valid]] = 1.0
    mrow = np.where(valid, 0.0, NEG_INF).astype(np.float32)[None]
    return jnp.asarray(sel, BF16), jnp.asarray(mrow)


def _dilated(src, rb_hi, rb_lo, batch, seq, window, dil):
    sub_len = seq // dil
    nb = sub_len // DIL_BLOCK
    sel, mrow = _bias_selector(window, dil)
    nu = DIL_UNITS[dil]
    consts = (rb_hi, rb_lo, sel, mrow)
    const = lambda a: pl.BlockSpec(a.shape, lambda *_: (0, 0))
    stat_shape = jax.ShapeDtypeStruct((batch * seq, LANES), F32)
    out_shape = [jax.ShapeDtypeStruct((batch * seq, D_DIL), BF16), stat_shape, stat_shape]
    bias_scratch = pltpu.VMEM((2, N_HEADS_DIL, DIL_BLOCK, 2 * DIL_BLOCK), F32)

    if dil == 1:
        rows = nu * DIL_BLOCK
        steps = nb // nu
        cur = lambda col: pl.BlockSpec((None, None, rows, D_DIL), lambda b, l: (b, 0, l, col))
        prev = lambda col: pl.BlockSpec((None, None, DIL_BLOCK, D_DIL),
                                        lambda b, l: (b, 0, jnp.maximum(nu * l - 1, 0), col))
        out = lambda width: pl.BlockSpec((rows, width), lambda b, l: (b * steps + l, 0))
        return pl.pallas_call(
            _dil1_kernel,
            grid=(batch, steps),
            in_specs=[cur(0), prev(1), cur(1), prev(2), cur(2)] + [const(a) for a in consts],
            out_specs=[out(D_DIL), out(LANES), out(LANES)],
            out_shape=out_shape,
            scratch_shapes=[bias_scratch],
            compiler_params=_cparams(("arbitrary", "arbitrary")),
            name="dilated_d1",
        )(src, src, src, src, src, *consts)

    rows = DIL_BLOCK * dil
    cur = lambda col: pl.BlockSpec((None, nu, DIL_BLOCK, D_DIL), lambda b, l, g: (b, g, l, col))
    prev = lambda col: pl.BlockSpec((None, nu, DIL_BLOCK, D_DIL),
                                    lambda b, l, g: (b, g, jnp.maximum(l - 1, 0), col))
    out = lambda width: pl.BlockSpec((rows, width), lambda b, l, g: (b * nb + l, 0))
    return pl.pallas_call(
        functools.partial(_dil_kernel, dil=dil),
        grid=(batch, nb, dil // nu),
        in_specs=[cur(0), prev(1), cur(1), prev(2), cur(2)] + [const(a) for a in consts],
        out_specs=[out(D_DIL), out(LANES), out(LANES)],
        out_shape=out_shape,
        scratch_shapes=[bias_scratch, pltpu.VMEM((N_HEADS_DIL // 2, rows, LANES), F32)],
        compiler_params=_cparams(("arbitrary", "arbitrary", "arbitrary")),
        name=f"dilated_d{dil}",
    )(src, src, src, src, src, *consts)


GLA_TM = 256


GLA_NC = GLA_TM // GLA_CHUNK


def _gla_tile(bb, q_ref, k_ref, v_ref, g_ref, w2h_ref, w2l_ref, bg_ref, nw_ref, tri_ref, o_ref, st_ref):
    g = g_ref[bb, :, :LANES]
    z = (jnp.dot(g, w2h_ref[...], preferred_element_type=F32)
         + jnp.dot(g, w2l_ref[...], preferred_element_type=F32) + bg_ref[...])
    la = (jnp.minimum(z, 0.0) - jnp.log(1.0 + jnp.exp(-jnp.abs(z)))) * (1.0 / GLA_TAU)
    la_h = la.astype(BF16)
    la_l = (la - la_h.astype(F32)).astype(BF16)

    tri = tri_ref[...]
    b = jnp.dot(tri, la_h, preferred_element_type=F32) + jnp.dot(tri, la_l, preferred_element_type=F32)
    b_last = jnp.concatenate(
        [jnp.broadcast_to(b[(ci + 1) * GLA_CHUNK - 1:(ci + 1) * GLA_CHUNK, :], (GLA_CHUNK, D_GLA_K))
         for ci in range(GLA_NC)], axis=0)

    q = q_ref[bb].astype(F32)
    k = k_ref[bb].astype(F32)
    q_t = (q * jnp.exp(b)).astype(BF16)
    k_t = (k * jnp.exp(-b)).astype(BF16)
    k_s = k * jnp.exp(b_last - b)

    causal = tri.astype(F32) > 0.0
    t_chunk = lax.broadcasted_iota(jnp.int32, (GLA_DK, GLA_TM), 1) // GLA_CHUNK
    nw = nw_ref[...]
    for h in range(N_HEADS_GLA):
        cs = slice(h * GLA_DK, (h + 1) * GLA_DK)
        vs = slice(h * GLA_DV, (h + 1) * GLA_DV)
        v_h = v_ref[bb, :, vs]
        a = lax.dot_general(q_t[:, cs], k_t[:, cs], (((1,), (1,)), ((), ())), preferred_element_type=F32)
        o = jnp.dot(jnp.where(causal, a, 0.0).astype(BF16), v_h, preferred_element_type=F32)
        k_s_t = k_s[:, cs].T.astype(BF16)
        zero = jnp.zeros_like(k_s_t)
        kv_all = jnp.dot(jnp.concatenate([jnp.where(t_chunk == ci, k_s_t, zero) for ci in range(GLA_NC)], axis=0),
                         v_h, preferred_element_type=F32)
        decay_t = jnp.exp(b_last[:, cs].T)
        st = st_ref[bb, h]
        outs = []
        for ci in range(GLA_NC):
            rs = slice(ci * GLA_CHUNK, (ci + 1) * GLA_CHUNK)
            o_c = o[rs] + jnp.dot(q_t[rs, cs], st.astype(BF16), preferred_element_type=F32)
            decay = jnp.broadcast_to(decay_t[:, ci * GLA_CHUNK:ci * GLA_CHUNK + 1], (GLA_DK, GLA_DV))
            st = decay * st + kv_all[ci * GLA_DK:(ci + 1) * GLA_DK]
            ms = jnp.mean(o_c * o_c, axis=-1, keepdims=True)
            outs.append((o_c * lax.rsqrt(ms + EPS) * nw).astype(o_ref.dtype))
        st_ref[bb, h] = st
        o_ref[bb, :, vs] = jnp.concatenate(outs, axis=0)


def _gla_kernel(*refs):
    st_ref = refs[-1]

    @pl.when(pl.program_id(0) == 0)
    def _():
        st_ref[...] = jnp.zeros_like(st_ref)

    for bb in range(st_ref.shape[0]):
        _gla_tile(bb, *refs)


def _gla(proj, w2_hi, w2_lo, b_gate, norm_w, batch, seq):
    pos = np.arange(GLA_TM)
    tri = jnp.asarray((pos[:, None] // GLA_CHUNK == pos[None, :] // GLA_CHUNK) & (pos[None, :] <= pos[:, None]),
                      BF16)
    proj3 = proj.reshape(batch, seq, REST_W)
    rows = lambda width, col: pl.BlockSpec((batch, GLA_TM, width), lambda i: (0, i, col // width))
    const = lambda shape: pl.BlockSpec(shape, lambda i: (0, 0))
    out = pl.pallas_call(
        _gla_kernel,
        grid=(seq // GLA_TM,),
        in_specs=[
            rows(D_GLA_K, COL_GQ), rows(D_GLA_K, COL_GK), rows(D_GLA_V, COL_GV), rows(GLR_PAD, COL_GLR),
            const((LANES, D_GLA_K)), const((LANES, D_GLA_K)), const((1, D_GLA_K)), const((1, GLA_DV)),
            const((GLA_TM, GLA_TM)),
        ],
        out_specs=rows(D_GLA_V, 0),
        out_shape=jax.ShapeDtypeStruct((batch, seq, D_GLA_V), BF16),
        scratch_shapes=[pltpu.VMEM((batch, N_HEADS_GLA, GLA_DK, GLA_DV), F32)],
        compiler_params=_cparams(("arbitrary",)),
        name="gla",
    )(proj3, proj3, proj3, proj3, w2_hi, w2_lo, b_gate, norm_w, tri)
    return out.reshape(batch * seq, D_GLA_V)


MEM_TM = 512


def _mem_kernel(q_ref, mem_ref, mnw_ref, wkv_ref, o_ref, kv_ref):
    @pl.when(pl.program_id(1) == 0)
    def _():
        m = mem_ref[0]
        ms = jnp.mean(m * m, axis=-1, keepdims=True)
        mn = (m * lax.rsqrt(ms + EPS) * mnw_ref[...]).astype(BF16)
        kv_ref[...] = jnp.dot(mn, wkv_ref[...], preferred_element_type=F32).astype(BF16)

    for j in range(N_HEADS_MEM // 2):
        sl = slice(j * LANES, (j + 1) * LANES)
        r_e, r_o, _, _ = _pair_attention(q_ref[:, sl], kv_ref[:, sl],
                                         kv_ref[:, D_MEM + j * LANES:D_MEM + (j + 1) * LANES], None, None)
        lo = lax.broadcasted_iota(jnp.int32, r_e.shape, 1) < HEAD_DIM
        den = jnp.where(lo, pltpu.roll(r_e, HEAD_DIM, 1), pltpu.roll(r_o, HEAD_DIM, 1))
        o_ref[:, sl] = (jnp.where(lo, r_e, r_o) / den).astype(o_ref.dtype)


def _mem_attn(proj, mem, mem_norm_w, w_kv, batch, seq):
    steps = seq // MEM_TM
    return pl.pallas_call(
        _mem_kernel,
        grid=(batch, steps),
        in_specs=[
            pl.BlockSpec((MEM_TM, D_MEM), lambda b, i: (b * steps + i, COL_MQ // D_MEM)),
            pl.BlockSpec((1, MEM_LEN, D_MODEL), lambda b, i: (b, 0, 0)),
            pl.BlockSpec((1, D_MODEL), lambda b, i: (0, 0)),
            pl.BlockSpec((D_MODEL, 2 * D_MEM), lambda b, i: (0, 0)),
        ],
        out_specs=pl.BlockSpec((MEM_TM, D_MEM), lambda b, i: (b * steps + i, 0)),
        out_shape=jax.ShapeDtypeStruct((batch * seq, D_MEM), BF16),
        scratch_shapes=[pltpu.VMEM((MEM_LEN, 2 * D_MEM), BF16)],
        compiler_params=_cparams(("parallel", "arbitrary")),
        name="mem_attn",
    )(proj, mem, mem_norm_w, w_kv)


OUT_TM = 512


def _silu(g):
    return g / (1.0 + jnp.exp(-g))


def _out_kernel(n1_ref, n2_ref, n3_ref, m1_ref, m2_ref, m3_ref, d1_ref, d2_ref, d3_ref, og_ref, om_ref,
                ga_ref, gg_ref, gm_ref, ex_ref, w_ref, x_ref, nw_ref, out_ref):
    m1, m2, m3 = m1_ref[...], m2_ref[...], m3_ref[...]
    mx = jnp.maximum(jnp.maximum(m1, m2), m3)
    t1, t2, t3 = jnp.exp2(m1 - mx), jnp.exp2(m2 - mx), jnp.exp2(m3 - mx)
    inv = 1.0 / (t1 * d1_ref[...] + t2 * d2_ref[...] + t3 * d3_ref[...])
    ex = ex_ref[...]
    a = jnp.zeros((OUT_TM, D_DIL), F32)
    for t, n_ref in ((t1, n1_ref), (t2, n2_ref), (t3, n3_ref)):
        w = jnp.dot((t * inv).astype(BF16), ex, preferred_element_type=F32)
        a = a + w * n_ref[...].astype(F32)
    mix_a = (a * _silu(ga_ref[...].astype(F32))).astype(BF16)
    mix_g = (og_ref[...].astype(F32) * _silu(gg_ref[...].astype(F32))).astype(BF16)
    mix_m = (om_ref[...].astype(F32) * _silu(gm_ref[...].astype(F32))).astype(BF16)
    y = (jnp.dot(mix_a, w_ref[0:D_DIL, :], preferred_element_type=F32)
         + jnp.dot(mix_g, w_ref[D_DIL:D_DIL + D_GLA_V, :], preferred_element_type=F32)
         + jnp.dot(mix_m, w_ref[D_DIL + D_GLA_V:D_MIX, :], preferred_element_type=F32))
    ms = jnp.mean(y * y, axis=-1, keepdims=True)
    out_ref[...] = x_ref[...] + y * lax.rsqrt(ms + EPS) * nw_ref[...]


def _out_proj(pats, o_gla, o_mem, proj, expand, w_out, x2d, norm_w):
    m = x2d.shape[0]
    rows = lambda width: pl.BlockSpec((OUT_TM, width), lambda i: (i, 0))
    const = lambda shape: pl.BlockSpec(shape, lambda i: (0, 0), pipeline_mode=pl.Buffered(1))
    nums, mxs, dens = zip(*pats)
    return pl.pallas_call(
        _out_kernel,
        grid=(m // OUT_TM,),
        in_specs=[
            *([rows(D_DIL)] * 3), *([rows(LANES)] * 6),
            rows(D_GLA_V), rows(D_MEM),
            pl.BlockSpec((OUT_TM, D_DIL), lambda i: (i, COL_GATE_A // D_DIL)),
            pl.BlockSpec((OUT_TM, D_GLA_V), lambda i: (i, COL_GATE_G // D_GLA_V)),
            pl.BlockSpec((OUT_TM, D_MEM), lambda i: (i, COL_GATE_M // D_MEM)),
            const((LANES, D_DIL)), const((D_MIX, D_MODEL)),
            rows(D_MODEL), const((1, D_MODEL)),
        ],
        out_specs=rows(D_MODEL),
        out_shape=jax.ShapeDtypeStruct((m, D_MODEL), x2d.dtype),
        compiler_params=_cparams(("parallel",)),
        name="out_proj",
    )(*nums, *mxs, *dens, o_gla, o_mem, proj, proj, proj, expand, w_out, x2d, norm_w)


D_IN_PROJ = 3 * D_DIL + 2 * D_GLA_K + D_GLA_V + GLA_GATE_RANK + D_MEM + D_MIX
SRC_GLR = 3 * D_DIL + 2 * D_GLA_K + D_GLA_V
SRC_MQ = SRC_GLR + GLA_GATE_RANK
SRC_GATE = SRC_MQ + D_MEM
REST_SEGMENTS = (
    (QKV_W, 2 * D_GLA_K + D_GLA_V, COL_GQ),
    (SRC_GATE + D_DIL, D_GLA_V, COL_GATE_G),
    (SRC_MQ, D_MEM, COL_MQ),
    (SRC_GATE + D_DIL + D_GLA_V, D_MEM, COL_GATE_M),
    (SRC_GATE, D_DIL, COL_GATE_A),
)
WPREP_TK = 512


def _wprep_kernel(wt_ref, qkv_ref, rest_ref):
    def put(o_ref, src, width, dst):
        for c in range(width // LANES):
            lo = c * LANES
            o_ref[:, dst + lo:dst + lo + LANES] = wt_ref[src + lo:src + lo + LANES, :].T.astype(o_ref.dtype)

    put(qkv_ref, 0, QKV_W, 0)
    for src, width, dst in REST_SEGMENTS:
        put(rest_ref, src, width, dst)
    glr = wt_ref[SRC_GLR:SRC_GLR + LANES, :].T
    keep = lax.broadcasted_iota(jnp.int32, glr.shape, 1) < GLA_GATE_RANK
    rest_ref[:, COL_GLR:COL_GLR + LANES] = jnp.where(keep, glr, 0.0).astype(rest_ref.dtype)
    rest_ref[:, COL_GLR + LANES:COL_GLR + GLR_PAD] = jnp.zeros((WPREP_TK, GLR_PAD - LANES), rest_ref.dtype)


def _regroup_in_weights(w_in):
    w_qkv, w_rest = pl.pallas_call(
        _wprep_kernel,
        grid=(D_MODEL // WPREP_TK,),
        in_specs=[pl.BlockSpec((D_IN_PROJ, WPREP_TK), lambda i: (0, i))],
        out_specs=[pl.BlockSpec((WPREP_TK, QKV_W), lambda i: (i, 0)),
                   pl.BlockSpec((WPREP_TK, REST_W), lambda i: (i, 0))],
        out_shape=[jax.ShapeDtypeStruct((D_MODEL, QKV_W), BF16), jax.ShapeDtypeStruct((D_MODEL, REST_W), BF16)],
        compiler_params=_cparams(("parallel",)),
        name="weight_regroup",
    )(w_in.T)
    scale_qkv = np.ones((1, QKV_W), np.float32)
    scale_qkv[:, :D_DIL] = HEAD_DIM ** -0.5 * LOG2E
    scale_rest = np.ones((1, REST_W), np.float32)
    scale_rest[:, COL_GQ:COL_GQ + D_GLA_K] = GLA_DK ** -0.5
    scale_rest[:, COL_MQ:COL_MQ + D_MEM] = HEAD_DIM ** -0.5 * LOG2E
    return w_qkv, w_rest, jnp.asarray(scale_qkv), jnp.asarray(scale_rest)


def _split_hi_lo(w):
    hi = w.astype(BF16)
    return hi, (w - hi.astype(F32)).astype(BF16)


def kernel(x, mem, norm_pre_w, w_in, rel_bias, w_gla_gate2, b_gla_gate, gla_norm_w, mem_norm_w, w_mem_kv,
           w_out, norm_post_w):
    batch, seq, _ = x.shape
    depth = w_in.shape[0]
    expand_np = np.zeros((LANES, D_DIL), np.float32)
    for h in range(N_HEADS_DIL):
        expand_np[_stat_lane(h), h * HEAD_DIM:(h + 1) * HEAD_DIM] = 1.0
    expand = jnp.asarray(expand_np, BF16)
    rb_t = jnp.zeros((16, LANES), F32).at[:N_HEADS_DIL, :REL_BUCKETS].set(rel_bias.astype(F32).T)
    rb_hi, rb_lo = _split_hi_lo(rb_t)
    for l in range(depth):
        x2d = x.reshape(batch * seq, D_MODEL)
        w_qkv, w_rest, scale_qkv, scale_rest = _regroup_in_weights(w_in[l])
        h, qkv1, qkv4, qkv16 = _qkv_proj(x2d, norm_pre_w[l][None], w_qkv, scale_qkv, batch, seq)
        rest = _rest_proj(h, w_rest, scale_rest)

        srcs = (qkv1.reshape(batch, 1, seq, QKV_W), qkv4, qkv16)
        pats = [_dilated(src, rb_hi, rb_lo, batch, seq, window, dil)
                for (window, dil), src in zip(DIL_PATTERNS, srcs)]

        w2 = jnp.zeros((LANES, D_GLA_K), F32).at[:GLA_GATE_RANK].set(w_gla_gate2[l])
        w2_hi, w2_lo = _split_hi_lo(w2)
        o_gla = _gla(rest, w2_hi, w2_lo, b_gla_gate[l][None], gla_norm_w[l][None], batch, seq)

        o_mem = _mem_attn(rest, mem, mem_norm_w[l][None], w_mem_kv[l].astype(BF16), batch, seq)

        out = _out_proj(pats, o_gla, o_mem, rest, expand, w_out[l].astype(BF16), x2d, norm_post_w[l][None])
        x = out.reshape(batch, seq, D_MODEL)
    return x
```

```python
import functools
import math

import numpy as np
import jax
import jax.numpy as jnp
from jax import lax
from jax.experimental import pallas as pl
from jax.experimental.pallas import tpu as pltpu

F32 = jnp.float32
BF16 = jnp.bfloat16

D_MODEL = 2048
HEAD_DIM = 64
N_HEADS_DIL = 12
DIL_PATTERNS = ((128, 1), (512, 4), (2048, 16))
DIL_BLOCK = 128
N_HEADS_GLA = 4
GLA_DK = 128
GLA_DV = 256
GLA_GATE_RANK = 16
GLA_TAU = 16.0
GLA_CHUNK = 64
N_HEADS_MEM = 4
MEM_LEN = 256
REL_BUCKETS = 32
REL_MAX_DIST = 2048
EPS = 1e-6
NEG_INF = -1e30

D_DIL = N_HEADS_DIL * HEAD_DIM
D_GLA_K = N_HEADS_GLA * GLA_DK
D_GLA_V = N_HEADS_GLA * GLA_DV
D_MEM = N_HEADS_MEM * HEAD_DIM
D_MIX = D_DIL + D_GLA_V + D_MEM

LANES = 128
GLR_PAD = 256

QKV_W = 3 * D_DIL
COL_GQ = 0
COL_GK = COL_GQ + D_GLA_K
COL_GV = COL_GK + D_GLA_K
COL_GATE_G = COL_GV + D_GLA_V
COL_MQ = COL_GATE_G + D_GLA_V
COL_GATE_M = COL_MQ + D_MEM
COL_GLR = COL_GATE_M + D_MEM
COL_GATE_A = COL_GLR + GLR_PAD
REST_W = COL_GATE_A + D_DIL

VMEM_LIMIT = 56 * 1024 * 1024


def _cparams(sem):
    return pltpu.CompilerParams(dimension_semantics=sem, vmem_limit_bytes=VMEM_LIMIT)


QKV_TM = 512
N_SLABS = D_DIL // LANES
REGROUP_DILS = (4, 16)


def _qkv_proj_kernel(x_ref, nw_ref, w_ref, sc_ref, h_ref, o_ref, x4_ref, x16_ref, slab_ref):
    x = x_ref[...]
    ms = jnp.mean(x * x, axis=-1, keepdims=True)
    h = (x * lax.rsqrt(ms + EPS) * nw_ref[...]).astype(BF16)
    h_ref[...] = h
    for t in range(QKV_W // D_DIL):
        cols = slice(t * D_DIL, (t + 1) * D_DIL)
        res = jnp.dot(h, w_ref[:, cols], preferred_element_type=F32) * sc_ref[:, cols]
        o_ref[:, cols] = res.astype(o_ref.dtype)
        for s in range(N_SLABS):
            slab_ref[t, s] = res[:, s * LANES:(s + 1) * LANES]
        for out_ref, dil in zip((x4_ref, x16_ref), REGROUP_DILS):
            for r in range(dil):
                for s in range(N_SLABS):
                    lo = t * D_DIL + s * LANES
                    out_ref[0, r, :, lo:lo + LANES] = (
                        slab_ref[t, s, pl.ds(r, QKV_TM // dil, stride=dil), :].astype(out_ref.dtype))


def _qkv_proj(x2d, norm_w, w_qkv, scale, batch, seq):
    m = x2d.shape[0]
    tiles_per_seq = seq // QKV_TM
    const = lambda shape: pl.BlockSpec(shape, lambda i: (0, 0), pipeline_mode=pl.Buffered(1))
    grouped = lambda dil: pl.BlockSpec((1, dil, QKV_TM // dil, QKV_W),
                                       lambda i: (i // tiles_per_seq, 0, i % tiles_per_seq, 0))
    return pl.pallas_call(
        _qkv_proj_kernel,
        grid=(m // QKV_TM,),
        in_specs=[
            pl.BlockSpec((QKV_TM, D_MODEL), lambda i: (i, 0)),
            const((1, D_MODEL)), const((D_MODEL, QKV_W)), const((1, QKV_W)),
        ],
        out_specs=[pl.BlockSpec((QKV_TM, D_MODEL), lambda i: (i, 0)),
                   pl.BlockSpec((QKV_TM, QKV_W), lambda i: (i, 0))] + [grouped(d) for d in REGROUP_DILS],
        out_shape=[jax.ShapeDtypeStruct((m, D_MODEL), BF16), jax.ShapeDtypeStruct((m, QKV_W), BF16)]
        + [jax.ShapeDtypeStruct((batch, d, seq // d, QKV_W), BF16) for d in REGROUP_DILS],
        scratch_shapes=[pltpu.VMEM((QKV_W // D_DIL, N_SLABS, QKV_TM, LANES), F32)],
        compiler_params=_cparams(("parallel",)),
        name="qkv_proj",
    )(x2d, norm_w, w_qkv, scale)


REST_TM = 1024
REST_TN = 1536


def _rest_proj_kernel(h_ref, w_ref, sc_ref, o_ref):
    acc = jnp.dot(h_ref[...], w_ref[...], preferred_element_type=F32)
    o_ref[...] = (acc * sc_ref[...]).astype(o_ref.dtype)


def _rest_proj(h, w_rest, scale):
    m = h.shape[0]
    return pl.pallas_call(
        _rest_proj_kernel,
        grid=(m // REST_TM, REST_W // REST_TN),
        in_specs=[
            pl.BlockSpec((REST_TM, D_MODEL), lambda i, j: (i, 0)),
            pl.BlockSpec((D_MODEL, REST_TN), lambda i, j: (0, j)),
            pl.BlockSpec((1, REST_TN), lambda i, j: (0, j)),
        ],
        out_specs=pl.BlockSpec((REST_TM, REST_TN), lambda i, j: (i, j)),
        out_shape=jax.ShapeDtypeStruct((m, REST_W), BF16),
        compiler_params=_cparams(("parallel", "arbitrary")),
        name="rest_proj",
    )(h, w_rest, scale)


LOG2E = math.log2(math.e)


def _pair_attention(q_pair, k_pair, v_pair, bias_e, bias_o):
    rows = q_pair.shape[0]
    lo_q = lax.broadcasted_iota(jnp.int32, q_pair.shape, 1) < HEAD_DIM
    zero = jnp.zeros_like(q_pair)
    q_both = jnp.concatenate([jnp.where(lo_q, q_pair, zero), jnp.where(lo_q, zero, q_pair)], axis=0)
    s = lax.dot_general(q_both, k_pair, (((1,), (1,)), ((), ())), preferred_element_type=F32)
    s_e, s_o = s[:rows], s[rows:]
    if bias_e is not None:
        s_e = s_e + bias_e
        s_o = s_o + bias_o
    mx_e = jnp.max(s_e, axis=-1, keepdims=True)
    mx_o = jnp.max(s_o, axis=-1, keepdims=True)
    p_e = jnp.exp2(s_e - mx_e).astype(BF16)
    p_o = jnp.exp2(s_o - mx_o).astype(BF16)
    lo_v = lax.broadcasted_iota(jnp.int32, v_pair.shape, 1) < HEAD_DIM
    one = jnp.ones_like(v_pair)
    r_e = jnp.dot(p_e, jnp.where(lo_v, v_pair, one), preferred_element_type=F32)
    r_o = jnp.dot(p_o, jnp.where(lo_v, one, v_pair), preferred_element_type=F32)
    return r_e, r_o, mx_e, mx_o


def _stat_lane(head):
    return head + HEAD_DIM if head % 2 == 0 else head


DIL_UNITS = {1: 4, 4: 4, 16: 4}


def _dil_init_bias(rbh_ref, rbl_ref, sel_ref, mrow_ref, bias_ref):
    f = (jnp.dot(rbh_ref[...], sel_ref[...], preferred_element_type=F32)
         + jnp.dot(rbl_ref[...], sel_ref[...], preferred_element_type=F32)) * LOG2E + mrow_ref[...]
    col = lax.broadcasted_iota(jnp.int32, (DIL_BLOCK, 2 * DIL_BLOCK), 1)
    for h in range(N_HEADS_DIL):
        row = jnp.broadcast_to(f[h:h + 1, :], (DIL_BLOCK, 2 * DIL_BLOCK))
        tab = pltpu.roll(row, 0, 1, stride=1, stride_axis=0)
        bias_ref[0, h] = tab
        bias_ref[1, h] = jnp.where(col >= DIL_BLOCK, tab, NEG_INF)


def _dil_unit(q, kp, kc, vp, vc, first, bias_ref, put_num):
    lane = lax.broadcasted_iota(jnp.int32, (DIL_BLOCK, LANES), 1)
    lo = lane < HEAD_DIM
    mx_c = jnp.zeros((DIL_BLOCK, LANES), F32)
    den_c = jnp.ones((DIL_BLOCK, LANES), F32)
    for j in range(N_HEADS_DIL // 2):
        sl = slice(j * LANES, (j + 1) * LANES)
        k_pair = jnp.concatenate([kp(sl), kc(sl)], axis=0)
        v_pair = jnp.concatenate([vp(sl), vc(sl)], axis=0)
        r_e, r_o, mx_e, mx_o = _pair_attention(q(sl), k_pair, v_pair,
                                               bias_ref[first, 2 * j], bias_ref[first, 2 * j + 1])
        put_num(j, jnp.where(lo, r_e, r_o))
        is_e, is_o = lane == _stat_lane(2 * j), lane == _stat_lane(2 * j + 1)
        mx_c = jnp.where(is_e, mx_e, jnp.where(is_o, mx_o, mx_c))
        den_c = jnp.where(is_e, r_e, jnp.where(is_o, r_o, den_c))
    return mx_c, den_c


def _dil1_kernel(q_ref, kp_ref, kc_ref, vp_ref, vc_ref, rbh_ref, rbl_ref, sel_ref, mrow_ref,
                 o_ref, mx_ref, den_ref, bias_ref):
    @pl.when((pl.program_id(0) == 0) & (pl.program_id(1) == 0))
    def _():
        _dil_init_bias(rbh_ref, rbl_ref, sel_ref, mrow_ref, bias_ref)

    for u in range(DIL_UNITS[1]):
        rows = slice(u * DIL_BLOCK, (u + 1) * DIL_BLOCK)
        prev = slice((u - 1) * DIL_BLOCK, u * DIL_BLOCK)
        if u == 0:
            first = (pl.program_id(1) == 0).astype(jnp.int32)
            kp, vp = (lambda sl: kp_ref[:, sl]), (lambda sl: vp_ref[:, sl])
        else:
            first = 0
            kp, vp = (lambda sl, p=prev: kc_ref[p, sl]), (lambda sl, p=prev: vc_ref[p, sl])

        def put_num(j, num, rows=rows):
            o_ref[rows, j * LANES:(j + 1) * LANES] = num.astype(o_ref.dtype)

        mx_c, den_c = _dil_unit(lambda sl, r=rows: q_ref[r, sl], kp, lambda sl, r=rows: kc_ref[r, sl],
                                vp, lambda sl, r=rows: vc_ref[r, sl], first, bias_ref, put_num)
        mx_ref[rows, :] = mx_c
        den_ref[rows, :] = den_c


def _dil_kernel(q_ref, kp_ref, kc_ref, vp_ref, vc_ref, rbh_ref, rbl_ref, sel_ref, mrow_ref,
                o_ref, mx_ref, den_ref, bias_ref, acc_ref, *, dil):
    b, lb, g = pl.program_id(0), pl.program_id(1), pl.program_id(2)
    nr = DIL_UNITS[dil]

    @pl.when((b == 0) & (lb == 0) & (g == 0))
    def _():
        _dil_init_bias(rbh_ref, rbl_ref, sel_ref, mrow_ref, bias_ref)

    first = (lb == 0).astype(jnp.int32)
    for u in range(nr):
        rows = pl.ds(g * nr + u, DIL_BLOCK, stride=dil)

        def put_num(j, num, rows=rows):
            acc_ref[j, rows, :] = num

        mx_c, den_c = _dil_unit(lambda sl, u=u: q_ref[u, :, sl], lambda sl, u=u: kp_ref[u, :, sl],
                                lambda sl, u=u: kc_ref[u, :, sl], lambda sl, u=u: vp_ref[u, :, sl],
                                lambda sl, u=u: vc_ref[u, :, sl], first, bias_ref, put_num)
        mx_ref[rows, :] = mx_c
        den_ref[rows, :] = den_c

    @pl.when(g == dil // nr - 1)
    def _():
        for j in range(N_HEADS_DIL // 2):
            o_ref[:, j * LANES:(j + 1) * LANES] = acc_ref[j].astype(o_ref.dtype)


def _t5_bucket_np(dist):
    max_exact = REL_BUCKETS // 2
    n = np.maximum(dist, 1).astype(np.float32)
    large = max_exact + (np.log(n / np.float32(max_exact)) / np.float32(math.log(REL_MAX_DIST / max_exact))
                         * np.float32(REL_BUCKETS - max_exact)).astype(np.int32)
    large = np.minimum(large, REL_BUCKETS - 1)
    return np.where(dist < max_exact, dist, large)


def _bias_selector(window, dil):
    back = window // dil
    x = np.arange(2 * DIL_BLOCK)
    rel = DIL_BLOCK - x
    valid = (rel >= 0) & (rel <= back)
    bucket = _t5_bucket_np(np.clip(rel, 0, back) * dil)
    sel = np.zeros((LANES, 2 * DIL_BLOCK), np.float32)
    sel[bucket[valid], x[valid]] = 1.0
    mrow = np.where(valid, 0.0, NEG_INF).astype(np.float32)[None]
    return jnp.asarray(sel, BF16), jnp.asarray(mrow)


def _dilated(src, rb_hi, rb_lo, batch, seq, window, dil):
    sub_len = seq // dil
    nb = sub_len // DIL_BLOCK
    sel, mrow = _bias_selector(window, dil)
    nu = DIL_UNITS[dil]
    consts = (rb_hi, rb_lo, sel, mrow)
    const = lambda a: pl.BlockSpec(a.shape, lambda *_: (0, 0))
    stat_shape = jax.ShapeDtypeStruct((batch * seq, LANES), F32)
    out_shape = [jax.ShapeDtypeStruct((batch * seq, D_DIL), BF16), stat_shape, stat_shape]
    bias_scratch = pltpu.VMEM((2, N_HEADS_DIL, DIL_BLOCK, 2 * DIL_BLOCK), F32)

    if dil == 1:
        rows = nu * DIL_BLOCK
        steps = nb // nu
        cur = lambda col: pl.BlockSpec((None, None, rows, D_DIL), lambda b, l: (b, 0, l, col))
        prev = lambda col: pl.BlockSpec((None, None, DIL_BLOCK, D_DIL),
                                        lambda b, l: (b, 0, jnp.maximum(nu * l - 1, 0), col))
        out = lambda width: pl.BlockSpec((rows, width), lambda b, l: (b * steps + l, 0))
        return pl.pallas_call(
            _dil1_kernel,
            grid=(batch, steps),
            in_specs=[cur(0), prev(1), cur(1), prev(2), cur(2)] + [const(a) for a in consts],
            out_specs=[out(D_DIL), out(LANES), out(LANES)],
            out_shape=out_shape,
            scratch_shapes=[bias_scratch],
            compiler_params=_cparams(("arbitrary", "arbitrary")),
            name="dilated_d1",
        )(src, src, src, src, src, *consts)

    rows = DIL_BLOCK * dil
    cur = lambda col: pl.BlockSpec((None, nu, DIL_BLOCK, D_DIL), lambda b, l, g: (b, g, l, col))
    prev = lambda col: pl.BlockSpec((None, nu, DIL_BLOCK, D_DIL),
                                    lambda b, l, g: (b, g, jnp.maximum(l - 1, 0), col))
    out = lambda width: pl.BlockSpec((rows, width), lambda b, l, g: (b * nb + l, 0))
    return pl.pallas_call(
        functools.partial(_dil_kernel, dil=dil),
        grid=(batch, nb, dil // nu),
        in_specs=[cur(0), prev(1), cur(1), prev(2), cur(2)] + [const(a) for a in consts],
        out_specs=[out(D_DIL), out(LANES), out(LANES)],
        out_shape=out_shape,
        scratch_shapes=[bias_scratch, pltpu.VMEM((N_HEADS_DIL // 2, rows, LANES), F32)],
        compiler_params=_cparams(("arbitrary", "arbitrary", "arbitrary")),
        name=f"dilated_d{dil}",
    )(src, src, src, src, src, *consts)


GLA_TM = 256


GLA_NC = GLA_TM // GLA_CHUNK


def _gla_tile(bb, q_ref, k_ref, v_ref, g_ref, w2h_ref, w2l_ref, bg_ref, nw_ref, tri_ref, o_ref, st_ref):
    g = g_ref[bb, :, :LANES]
    z = (jnp.dot(g, w2h_ref[...], preferred_element_type=F32)
         + jnp.dot(g, w2l_ref[...], preferred_element_type=F32) + bg_ref[...])
    la = (jnp.minimum(z, 0.0) - jnp.log(1.0 + jnp.exp(-jnp.abs(z)))) * (1.0 / GLA_TAU)
    la_h = la.astype(BF16)
    la_l = (la - la_h.astype(F32)).astype(BF16)

    tri = tri_ref[...]
    b = jnp.dot(tri, la_h, preferred_element_type=F32) + jnp.dot(tri, la_l, preferred_element_type=F32)
    b_last = jnp.concatenate(
        [jnp.broadcast_to(b[(ci + 1) * GLA_CHUNK - 1:(ci + 1) * GLA_CHUNK, :], (GLA_CHUNK, D_GLA_K))
         for ci in range(GLA_NC)], axis=0)

    q = q_ref[bb].astype(F32)
    k = k_ref[bb].astype(F32)
    q_t = (q * jnp.exp(b)).astype(BF16)
    k_t = (k * jnp.exp(-b)).astype(BF16)
    k_s = k * jnp.exp(b_last - b)

    causal = tri.astype(F32) > 0.0
    t_chunk = lax.broadcasted_iota(jnp.int32, (GLA_DK, GLA_TM), 1) // GLA_CHUNK
    nw = nw_ref[...]
    for h in range(N_HEADS_GLA):
        cs = slice(h * GLA_DK, (h + 1) * GLA_DK)
        vs = slice(h * GLA_DV, (h + 1) * GLA_DV)
        v_h = v_ref[bb, :, vs]
        a = lax.dot_general(q_t[:, cs], k_t[:, cs], (((1,), (1,)), ((), ())), preferred_element_type=F32)
        o = jnp.dot(jnp.where(causal, a, 0.0).astype(BF16), v_h, preferred_element_type=F32)
        k_s_t = k_s[:, cs].T.astype(BF16)
        zero = jnp.zeros_like(k_s_t)
        kv_all = jnp.dot(jnp.concatenate([jnp.where(t_chunk == ci, k_s_t, zero) for ci in range(GLA_NC)], axis=0),
                         v_h, preferred_element_type=F32)
        decay_t = jnp.exp(b_last[:, cs].T)
        st = st_ref[bb, h]
        outs = []
        for ci in range(GLA_NC):
            rs = slice(ci * GLA_CHUNK, (ci + 1) * GLA_CHUNK)
            o_c = o[rs] + jnp.dot(q_t[rs, cs], st.astype(BF16), preferred_element_type=F32)
            decay = jnp.broadcast_to(decay_t[:, ci * GLA_CHUNK:ci * GLA_CHUNK + 1], (GLA_DK, GLA_DV))
            st = decay * st + kv_all[ci * GLA_DK:(ci + 1) * GLA_DK]
            ms = jnp.mean(o_c * o_c, axis=-1, keepdims=True)
            outs.append((o_c * lax.rsqrt(ms + EPS) * nw).astype(o_ref.dtype))
        st_ref[bb, h] = st
        o_ref[bb, :, vs] = jnp.concatenate(outs, axis=0)


def _gla_kernel(*refs):
    st_ref = refs[-1]

    @pl.when(pl.program_id(0) == 0)
    def _():
        st_ref[...] = jnp.zeros_like(st_ref)

    for bb in range(st_ref.shape[0]):
        _gla_tile(bb, *refs)


def _gla(proj, w2_hi, w2_lo, b_gate, norm_w, batch, seq):
    pos = np.arange(GLA_TM)
    tri = jnp.asarray((pos[:, None] // GLA_CHUNK == pos[None, :] // GLA_CHUNK) & (pos[None, :] <= pos[:, None]),
                      BF16)
    proj3 = proj.reshape(batch, seq, REST_W)
    rows = lambda width, col: pl.BlockSpec((batch, GLA_TM, width), lambda i: (0, i, col // width))
    const = lambda shape: pl.BlockSpec(shape, lambda i: (0, 0))
    out = pl.pallas_call(
        _gla_kernel,
        grid=(seq // GLA_TM,),
        in_specs=[
            rows(D_GLA_K, COL_GQ), rows(D_GLA_K, COL_GK), rows(D_GLA_V, COL_GV), rows(GLR_PAD, COL_GLR),
            const((LANES, D_GLA_K)), const((LANES, D_GLA_K)), const((1, D_GLA_K)), const((1, GLA_DV)),
            const((GLA_TM, GLA_TM)),
        ],
        out_specs=rows(D_GLA_V, 0),
        out_shape=jax.ShapeDtypeStruct((batch, seq, D_GLA_V), BF16),
        scratch_shapes=[pltpu.VMEM((batch, N_HEADS_GLA, GLA_DK, GLA_DV), F32)],
        compiler_params=_cparams(("arbitrary",)),
        name="gla",
    )(proj3, proj3, proj3, proj3, w2_hi, w2_lo, b_gate, norm_w, tri)
    return out.reshape(batch * seq, D_GLA_V)


MEM_TM = 1024
MEM_UNIT = 128


def _mem_kernel(q_ref, mem_ref, mnw_ref, wkv_ref, o_ref, den_ref, kv_ref):
    @pl.when(pl.program_id(1) == 0)
    def _():
        m = mem_ref[0]
        ms = jnp.mean(m * m, axis=-1, keepdims=True)
        mn = (m * lax.rsqrt(ms + EPS) * mnw_ref[...]).astype(BF16)
        kv_ref[...] = jnp.dot(mn, wkv_ref[...], preferred_element_type=F32).astype(BF16)

    lane = lax.broadcasted_iota(jnp.int32, (MEM_UNIT, LANES), 1)
    lo = lane < HEAD_DIM
    for u in range(MEM_TM // MEM_UNIT):
        rows = slice(u * MEM_UNIT, (u + 1) * MEM_UNIT)
        den_c = jnp.ones((MEM_UNIT, LANES), F32)
        for j in range(N_HEADS_MEM // 2):
            sl = slice(j * LANES, (j + 1) * LANES)
            r_e, r_o, _, _ = _pair_attention(q_ref[rows, sl], kv_ref[:, sl],
                                             kv_ref[:, D_MEM + j * LANES:D_MEM + (j + 1) * LANES], None, None)
            o_ref[rows, sl] = jnp.where(lo, r_e, r_o).astype(o_ref.dtype)
            den_c = jnp.where(lane == _stat_lane(2 * j), r_e, jnp.where(lane == _stat_lane(2 * j + 1), r_o, den_c))
        den_ref[rows, :] = den_c


def _mem_attn(proj, mem, mem_norm_w, w_kv, batch, seq):
    steps = seq // MEM_TM
    rows = lambda width, col=0: pl.BlockSpec((MEM_TM, width), lambda b, i: (b * steps + i, col // width))
    return pl.pallas_call(
        _mem_kernel,
        grid=(batch, steps),
        in_specs=[
            rows(D_MEM, COL_MQ),
            pl.BlockSpec((1, MEM_LEN, D_MODEL), lambda b, i: (b, 0, 0)),
            pl.BlockSpec((1, D_MODEL), lambda b, i: (0, 0)),
            pl.BlockSpec((D_MODEL, 2 * D_MEM), lambda b, i: (0, 0)),
        ],
        out_specs=[rows(D_MEM), rows(LANES)],
        out_shape=[jax.ShapeDtypeStruct((batch * seq, D_MEM), BF16),
                   jax.ShapeDtypeStruct((batch * seq, LANES), F32)],
        scratch_shapes=[pltpu.VMEM((MEM_LEN, 2 * D_MEM), BF16)],
        compiler_params=_cparams(("parallel", "arbitrary")),
        name="mem_attn",
    )(proj, mem, mem_norm_w, w_kv)


OUT_TM = 512


def _silu(g):
    h = g * 0.5
    return h * jnp.tanh(h) + h


OUT_GW = 256
OUT_CW = 512


def _out_tile(read_ref, write_ref, n1_ref, n2_ref, n3_ref, m1_ref, m2_ref, m3_ref, d1_ref, d2_ref, d3_ref,
              og_ref, nm_ref, dm_ref, ga_ref, gg_ref, gm_ref, ex_ref, exm_ref, w_ref, x_ref, nw_ref, out_ref):
    m1, m2, m3 = m1_ref[...], m2_ref[...], m3_ref[...]
    mx = jnp.maximum(jnp.maximum(m1, m2), m3)
    t1, t2, t3 = jnp.exp2(m1 - mx), jnp.exp2(m2 - mx), jnp.exp2(m3 - mx)
    inv = 1.0 / (t1 * d1_ref[...] + t2 * d2_ref[...] + t3 * d3_ref[...])
    wts = [(t * inv).astype(BF16) for t in (t1, t2, t3)]
    inv_m = (1.0 / dm_ref[...]).astype(BF16)

    def gate_chunk(lo):
        cols = slice(lo, lo + OUT_GW)
        if lo < D_DIL:
            a = jnp.zeros((OUT_TM, OUT_GW), F32)
            for wt, n_ref in zip(wts, (n1_ref, n2_ref, n3_ref)):
                a = a + jnp.dot(wt, ex_ref[:, cols], preferred_element_type=F32) * n_ref[:, cols].astype(F32)
            g = ga_ref[:, cols]
        elif lo < D_DIL + D_GLA_V:
            lc = slice(lo - D_DIL, lo - D_DIL + OUT_GW)
            a, g = og_ref[:, lc], gg_ref[:, lc]
        else:
            a = jnp.dot(inv_m, exm_ref[...], preferred_element_type=F32) * nm_ref[...].astype(F32)
            g = gm_ref[...]
        write_ref[:, cols] = a.astype(BF16) * _silu(g)

    ssq = jnp.zeros((OUT_TM, 1), F32)
    for c in range(D_MODEL // OUT_CW):
        for lo in range(c * OUT_CW, (c + 1) * OUT_CW, OUT_GW):
            gate_chunk(lo)
        cols = slice(c * OUT_CW, (c + 1) * OUT_CW)
        y = jnp.dot(read_ref[...], w_ref[:, cols], preferred_element_type=F32)
        ssq = ssq + jnp.sum(y * y, axis=-1, keepdims=True)
        out_ref[:, cols] = y
    scale = lax.rsqrt(ssq * (1.0 / D_MODEL) + EPS)
    out_ref[...] = x_ref[...] + out_ref[...] * scale * nw_ref[...]


def _out_kernel(*refs):
    mix_even, mix_odd = refs[-2:]
    s = pl.program_id(0)

    @pl.when(s == 0)
    def _():
        mix_odd[...] = jnp.zeros_like(mix_odd)

    @pl.when(s % 2 == 0)
    def _():
        _out_tile(mix_odd, mix_even, *refs[:-2])

    @pl.when(s % 2 == 1)
    def _():
        _out_tile(mix_even, mix_odd, *refs[:-2])


def _out_proj(pats, o_gla, mem_pair, proj, expand, expand_mem, w_out, x2d, norm_w):
    m = x2d.shape[0]
    n_tiles = m // OUT_TM
    gate_rows = lambda width, col=0: pl.BlockSpec(
        (OUT_TM, width), lambda s: (jnp.minimum(s, n_tiles - 1), col // width))
    proj_rows = lambda width: pl.BlockSpec((OUT_TM, width), lambda s: (jnp.maximum(s - 1, 0), 0))
    const = lambda shape: pl.BlockSpec(shape, lambda s: (0, 0), pipeline_mode=pl.Buffered(1))
    nums, mxs, dens = zip(*pats)
    return pl.pallas_call(
        _out_kernel,
        grid=(n_tiles + 1,),
        in_specs=[
            *([gate_rows(D_DIL)] * 3), *([gate_rows(LANES)] * 6),
            gate_rows(D_GLA_V), gate_rows(D_MEM), gate_rows(LANES),
            gate_rows(D_DIL, COL_GATE_A), gate_rows(D_GLA_V, COL_GATE_G), gate_rows(D_MEM, COL_GATE_M),
            const((LANES, D_DIL)), const((LANES, D_MEM)), const((D_MIX, D_MODEL)),
            proj_rows(D_MODEL), const((1, D_MODEL)),
        ],
        out_specs=proj_rows(D_MODEL),
        out_shape=jax.ShapeDtypeStruct((m, D_MODEL), x2d.dtype),
        scratch_shapes=[pltpu.VMEM((OUT_TM, D_MIX), BF16), pltpu.VMEM((OUT_TM, D_MIX), BF16)],
        compiler_params=_cparams(("arbitrary",)),
        name="out_proj",
    )(*nums, *mxs, *dens, o_gla, *mem_pair, proj, proj, proj, expand, expand_mem, w_out, x2d, norm_w)


D_IN_PROJ = 3 * D_DIL + 2 * D_GLA_K + D_GLA_V + GLA_GATE_RANK + D_MEM + D_MIX
SRC_GLR = 3 * D_DIL + 2 * D_GLA_K + D_GLA_V
SRC_MQ = SRC_GLR + GLA_GATE_RANK
SRC_GATE = SRC_MQ + D_MEM
REST_SEGMENTS = (
    (QKV_W, 2 * D_GLA_K + D_GLA_V, COL_GQ),
    (SRC_GATE + D_DIL, D_GLA_V, COL_GATE_G),
    (SRC_MQ, D_MEM, COL_MQ),
    (SRC_GATE + D_DIL + D_GLA_V, D_MEM, COL_GATE_M),
    (SRC_GATE, D_DIL, COL_GATE_A),
)
WPREP_TK = 512


def _wprep_kernel(wt_ref, qkv_ref, rest_ref):
    def put(o_ref, src, width, dst):
        for c in range(width // LANES):
            lo = c * LANES
            o_ref[:, dst + lo:dst + lo + LANES] = wt_ref[src + lo:src + lo + LANES, :].T.astype(o_ref.dtype)

    put(qkv_ref, 0, QKV_W, 0)
    for src, width, dst in REST_SEGMENTS:
        put(rest_ref, src, width, dst)
    glr = wt_ref[SRC_GLR:SRC_GLR + LANES, :].T
    keep = lax.broadcasted_iota(jnp.int32, glr.shape, 1) < GLA_GATE_RANK
    rest_ref[:, COL_GLR:COL_GLR + LANES] = jnp.where(keep, glr, 0.0).astype(rest_ref.dtype)
    rest_ref[:, COL_GLR + LANES:COL_GLR + GLR_PAD] = jnp.zeros((WPREP_TK, GLR_PAD - LANES), rest_ref.dtype)


def _regroup_in_weights(w_in):
    w_qkv, w_rest = pl.pallas_call(
        _wprep_kernel,
        grid=(D_MODEL // WPREP_TK,),
        in_specs=[pl.BlockSpec((D_IN_PROJ, WPREP_TK), lambda i: (0, i))],
        out_specs=[pl.BlockSpec((WPREP_TK, QKV_W), lambda i: (i, 0)),
                   pl.BlockSpec((WPREP_TK, REST_W), lambda i: (i, 0))],
        out_shape=[jax.ShapeDtypeStruct((D_MODEL, QKV_W), BF16), jax.ShapeDtypeStruct((D_MODEL, REST_W), BF16)],
        compiler_params=_cparams(("parallel",)),
        name="weight_regroup",
    )(w_in.T)
    scale_qkv = np.ones((1, QKV_W), np.float32)
    scale_qkv[:, :D_DIL] = HEAD_DIM ** -0.5 * LOG2E
    scale_rest = np.ones((1, REST_W), np.float32)
    scale_rest[:, COL_GQ:COL_GQ + D_GLA_K] = GLA_DK ** -0.5
    scale_rest[:, COL_MQ:COL_MQ + D_MEM] = HEAD_DIM ** -0.5 * LOG2E
    return w_qkv, w_rest, jnp.asarray(scale_qkv), jnp.asarray(scale_rest)


def _split_hi_lo(w):
    hi = w.astype(BF16)
    return hi, (w - hi.astype(F32)).astype(BF16)


def kernel(x, mem, norm_pre_w, w_in, rel_bias, w_gla_gate2, b_gla_gate, gla_norm_w, mem_norm_w, w_mem_kv,
           w_out, norm_post_w):
    batch, seq, _ = x.shape
    depth = w_in.shape[0]
    expand_np = np.zeros((LANES, D_DIL), np.float32)
    for h in range(N_HEADS_DIL):
        expand_np[_stat_lane(h), h * HEAD_DIM:(h + 1) * HEAD_DIM] = 1.0
    expand = jnp.asarray(expand_np, BF16)
    expand_mem = jnp.asarray(expand_np[:, :D_MEM], BF16)
    rb_t = jnp.zeros((16, LANES), F32).at[:N_HEADS_DIL, :REL_BUCKETS].set(rel_bias.astype(F32).T)
    rb_hi, rb_lo = _split_hi_lo(rb_t)
    for l in range(depth):
        x2d = x.reshape(batch * seq, D_MODEL)
        w_qkv, w_rest, scale_qkv, scale_rest = _regroup_in_weights(w_in[l])
        h, qkv1, qkv4, qkv16 = _qkv_proj(x2d, norm_pre_w[l][None], w_qkv, scale_qkv, batch, seq)
        rest = _rest_proj(h, w_rest, scale_rest)

        srcs = (qkv1.reshape(batch, 1, seq, QKV_W), qkv4, qkv16)
        pats = [_dilated(src, rb_hi, rb_lo, batch, seq, window, dil)
                for (window, dil), src in zip(DIL_PATTERNS, srcs)]

        w2 = jnp.zeros((LANES, D_GLA_K), F32).at[:GLA_GATE_RANK].set(w_gla_gate2[l])
        w2_hi, w2_lo = _split_hi_lo(w2)
        o_gla = _gla(rest, w2_hi, w2_lo, b_gla_gate[l][None], gla_norm_w[l][None], batch, seq)

        mem_pair = _mem_attn(rest, mem, mem_norm_w[l][None], w_mem_kv[l].astype(BF16), batch, seq)

        out = _out_proj(pats, o_gla, mem_pair, rest, expand, expand_mem, w_out[l].astype(BF16), x2d,
                        norm_post_w[l][None])
        x = out.reshape(batch, seq, D_MODEL)
    return x
```

```python
import functools
import math

import numpy as np
import jax
import jax.numpy as jnp
from jax import lax
from jax.experimental import pallas as pl
from jax.experimental.pallas import tpu as pltpu

F32 = jnp.float32
BF16 = jnp.bfloat16

D_MODEL = 2048
HEAD_DIM = 64
N_HEADS_DIL = 12
DIL_PATTERNS = ((128, 1), (512, 4), (2048, 16))
DIL_BLOCK = 128
N_HEADS_GLA = 4
GLA_DK = 128
GLA_DV = 256
GLA_GATE_RANK = 16
GLA_TAU = 16.0
GLA_CHUNK = 64
N_HEADS_MEM = 4
MEM_LEN = 256
REL_BUCKETS = 32
REL_MAX_DIST = 2048
EPS = 1e-6
NEG_INF = -1e30

D_DIL = N_HEADS_DIL * HEAD_DIM
D_GLA_K = N_HEADS_GLA * GLA_DK
D_GLA_V = N_HEADS_GLA * GLA_DV
D_MEM = N_HEADS_MEM * HEAD_DIM
D_MIX = D_DIL + D_GLA_V + D_MEM

LANES = 128
GLR_PAD = 256

QKV_W = 3 * D_DIL
COL_GQ = 0
COL_GK = COL_GQ + D_GLA_K
COL_GV = COL_GK + D_GLA_K
COL_GATE_G = COL_GV + D_GLA_V
COL_MQ = COL_GATE_G + D_GLA_V
COL_GATE_M = COL_MQ + D_MEM
COL_GLR = COL_GATE_M + D_MEM
COL_GATE_A = COL_GLR + GLR_PAD
REST_W = COL_GATE_A + D_DIL

VMEM_LIMIT = 56 * 1024 * 1024


def _cparams(sem):
    return pltpu.CompilerParams(dimension_semantics=sem, vmem_limit_bytes=VMEM_LIMIT)


QKV_TM = 512
N_SLABS = D_DIL // LANES
REGROUP_DILS = (4, 16)


def _qkv_proj_kernel(x_ref, nw_ref, w_ref, sc_ref, h_ref, o_ref, x4_ref, x16_ref, slab_ref):
    x = x_ref[...]
    ms = jnp.mean(x * x, axis=-1, keepdims=True)
    h_ref[...] = (x * lax.rsqrt(ms + EPS) * nw_ref[...]).astype(BF16)
    for t in range(QKV_W // D_DIL):
        cols = slice(t * D_DIL, (t + 1) * D_DIL)
        res = jnp.dot(h_ref[...], w_ref[:, cols], preferred_element_type=F32) * sc_ref[:, cols]
        o_ref[:, cols] = res.astype(o_ref.dtype)
        for s in range(N_SLABS):
            slab_ref[t, s] = res[:, s * LANES:(s + 1) * LANES]
        for out_ref, dil in zip((x4_ref, x16_ref), REGROUP_DILS):
            for r in range(dil):
                for s in range(N_SLABS):
                    lo = t * D_DIL + s * LANES
                    out_ref[0, r, :, lo:lo + LANES] = (
                        slab_ref[t, s, pl.ds(r, QKV_TM // dil, stride=dil), :].astype(out_ref.dtype))


def _qkv_proj(x2d, norm_w, w_qkv, scale, batch, seq):
    m = x2d.shape[0]
    tiles_per_seq = seq // QKV_TM
    const = lambda shape: pl.BlockSpec(shape, lambda i: (0, 0), pipeline_mode=pl.Buffered(1))
    grouped = lambda dil: pl.BlockSpec((1, dil, QKV_TM // dil, QKV_W),
                                       lambda i: (i // tiles_per_seq, 0, i % tiles_per_seq, 0))
    return pl.pallas_call(
        _qkv_proj_kernel,
        grid=(m // QKV_TM,),
        in_specs=[
            pl.BlockSpec((QKV_TM, D_MODEL), lambda i: (i, 0)),
            const((1, D_MODEL)), const((D_MODEL, QKV_W)), const((1, QKV_W)),
        ],
        out_specs=[pl.BlockSpec((QKV_TM, D_MODEL), lambda i: (i, 0)),
                   pl.BlockSpec((QKV_TM, QKV_W), lambda i: (i, 0))] + [grouped(d) for d in REGROUP_DILS],
        out_shape=[jax.ShapeDtypeStruct((m, D_MODEL), BF16), jax.ShapeDtypeStruct((m, QKV_W), BF16)]
        + [jax.ShapeDtypeStruct((batch, d, seq // d, QKV_W), BF16) for d in REGROUP_DILS],
        scratch_shapes=[pltpu.VMEM((QKV_W // D_DIL, N_SLABS, QKV_TM, LANES), F32)],
        compiler_params=_cparams(("parallel",)),
        name="qkv_proj",
    )(x2d, norm_w, w_qkv, scale)


REST_TM = 1024
REST_TN = 1536


def _rest_proj_kernel(h_ref, w_ref, sc_ref, o_ref):
    acc = jnp.dot(h_ref[...], w_ref[...], preferred_element_type=F32)
    o_ref[...] = (acc * sc_ref[...]).astype(o_ref.dtype)


def _rest_proj(h, w_rest, scale):
    m = h.shape[0]
    return pl.pallas_call(
        _rest_proj_kernel,
        grid=(m // REST_TM, REST_W // REST_TN),
        in_specs=[
            pl.BlockSpec((REST_TM, D_MODEL), lambda i, j: (i, 0)),
            pl.BlockSpec((D_MODEL, REST_TN), lambda i, j: (0, j)),
            pl.BlockSpec((1, REST_TN), lambda i, j: (0, j)),
        ],
        out_specs=pl.BlockSpec((REST_TM, REST_TN), lambda i, j: (i, j)),
        out_shape=jax.ShapeDtypeStruct((m, REST_W), BF16),
        compiler_params=_cparams(("parallel", "arbitrary")),
        name="rest_proj",
    )(h, w_rest, scale)


LOG2E = math.log2(math.e)


def _pair_attention(q_pair, k_pair, v_pair, bias_e, bias_o):
    rows = q_pair.shape[0]
    lo_q = lax.broadcasted_iota(jnp.int32, q_pair.shape, 1) < HEAD_DIM
    zero = jnp.zeros_like(q_pair)
    q_both = jnp.concatenate([jnp.where(lo_q, q_pair, zero), jnp.where(lo_q, zero, q_pair)], axis=0)
    s = lax.dot_general(q_both, k_pair, (((1,), (1,)), ((), ())), preferred_element_type=F32)
    s_e, s_o = s[:rows], s[rows:]
    if bias_e is not None:
        s_e = s_e + bias_e
        s_o = s_o + bias_o
    mx_e = jnp.max(s_e, axis=-1, keepdims=True)
    mx_o = jnp.max(s_o, axis=-1, keepdims=True)
    p_e = jnp.exp2(s_e - mx_e).astype(BF16)
    p_o = jnp.exp2(s_o - mx_o).astype(BF16)
    lo_v = lax.broadcasted_iota(jnp.int32, v_pair.shape, 1) < HEAD_DIM
    one = jnp.ones_like(v_pair)
    r_e = jnp.dot(p_e, jnp.where(lo_v, v_pair, one), preferred_element_type=F32)
    r_o = jnp.dot(p_o, jnp.where(lo_v, one, v_pair), preferred_element_type=F32)
    return r_e, r_o, mx_e, mx_o


def _stat_lane(head):
    return head + HEAD_DIM if head % 2 == 0 else head


DIL_UNITS = {1: 4, 4: 4, 16: 4}


def _dil_init_bias(rbh_ref, rbl_ref, sel_ref, mrow_ref, bias_ref):
    f = (jnp.dot(rbh_ref[...], sel_ref[...], preferred_element_type=F32)
         + jnp.dot(rbl_ref[...], sel_ref[...], preferred_element_type=F32)) * LOG2E + mrow_ref[...]
    col = lax.broadcasted_iota(jnp.int32, (DIL_BLOCK, 2 * DIL_BLOCK), 1)
    for h in range(N_HEADS_DIL):
        row = jnp.broadcast_to(f[h:h + 1, :], (DIL_BLOCK, 2 * DIL_BLOCK))
        tab = pltpu.roll(row, 0, 1, stride=1, stride_axis=0)
        bias_ref[0, h] = tab
        bias_ref[1, h] = jnp.where(col >= DIL_BLOCK, tab, NEG_INF)


def _dil_unit(q, kp, kc, vp, vc, first, bias_ref, put_num):
    lane = lax.broadcasted_iota(jnp.int32, (DIL_BLOCK, LANES), 1)
    lo = lane < HEAD_DIM
    mx_c = jnp.zeros((DIL_BLOCK, LANES), F32)
    den_c = jnp.ones((DIL_BLOCK, LANES), F32)
    for j in range(N_HEADS_DIL // 2):
        sl = slice(j * LANES, (j + 1) * LANES)
        k_pair = jnp.concatenate([kp(sl), kc(sl)], axis=0)
        v_pair = jnp.concatenate([vp(sl), vc(sl)], axis=0)
        r_e, r_o, mx_e, mx_o = _pair_attention(q(sl), k_pair, v_pair,
                                               bias_ref[first, 2 * j], bias_ref[first, 2 * j + 1])
        put_num(j, jnp.where(lo, r_e, r_o))
        is_e, is_o = lane == _stat_lane(2 * j), lane == _stat_lane(2 * j + 1)
        mx_c = jnp.where(is_e, mx_e, jnp.where(is_o, mx_o, mx_c))
        den_c = jnp.where(is_e, r_e, jnp.where(is_o, r_o, den_c))
    return mx_c, den_c


def _dil1_kernel(q_ref, kp_ref, kc_ref, vp_ref, vc_ref, rbh_ref, rbl_ref, sel_ref, mrow_ref,
                 o_ref, mx_ref, den_ref, bias_ref):
    @pl.when((pl.program_id(0) == 0) & (pl.program_id(1) == 0))
    def _():
        _dil_init_bias(rbh_ref, rbl_ref, sel_ref, mrow_ref, bias_ref)

    for u in range(DIL_UNITS[1]):
        rows = slice(u * DIL_BLOCK, (u + 1) * DIL_BLOCK)
        prev = slice((u - 1) * DIL_BLOCK, u * DIL_BLOCK)
        if u == 0:
            first = (pl.program_id(1) == 0).astype(jnp.int32)
            kp, vp = (lambda sl: kp_ref[:, sl]), (lambda sl: vp_ref[:, sl])
        else:
            first = 0
            kp, vp = (lambda sl, p=prev: kc_ref[p, sl]), (lambda sl, p=prev: vc_ref[p, sl])

        def put_num(j, num, rows=rows):
            o_ref[rows, j * LANES:(j + 1) * LANES] = num.astype(o_ref.dtype)

        mx_c, den_c = _dil_unit(lambda sl, r=rows: q_ref[r, sl], kp, lambda sl, r=rows: kc_ref[r, sl],
                                vp, lambda sl, r=rows: vc_ref[r, sl], first, bias_ref, put_num)
        mx_ref[rows, :] = mx_c
        den_ref[rows, :] = den_c


def _dil_kernel(q_ref, kp_ref, kc_ref, vp_ref, vc_ref, rbh_ref, rbl_ref, sel_ref, mrow_ref,
                o_ref, mx_ref, den_ref, bias_ref, acc_ref, *, dil):
    b, lb, g = pl.program_id(0), pl.program_id(1), pl.program_id(2)
    nr = DIL_UNITS[dil]

    @pl.when((b == 0) & (lb == 0) & (g == 0))
    def _():
        _dil_init_bias(rbh_ref, rbl_ref, sel_ref, mrow_ref, bias_ref)

    first = (lb == 0).astype(jnp.int32)
    for u in range(nr):
        rows = pl.ds(g * nr + u, DIL_BLOCK, stride=dil)

        def put_num(j, num, rows=rows):
            acc_ref[j, rows, :] = num

        mx_c, den_c = _dil_unit(lambda sl, u=u: q_ref[u, :, sl], lambda sl, u=u: kp_ref[u, :, sl],
                                lambda sl, u=u: kc_ref[u, :, sl], lambda sl, u=u: vp_ref[u, :, sl],
                                lambda sl, u=u: vc_ref[u, :, sl], first, bias_ref, put_num)
        mx_ref[rows, :] = mx_c
        den_ref[rows, :] = den_c

    @pl.when(g == dil // nr - 1)
    def _():
        for j in range(N_HEADS_DIL // 2):
            o_ref[:, j * LANES:(j + 1) * LANES] = acc_ref[j].astype(o_ref.dtype)


def _t5_bucket_np(dist):
    max_exact = REL_BUCKETS // 2
    n = np.maximum(dist, 1).astype(np.float32)
    large = max_exact + (np.log(n / np.float32(max_exact)) / np.float32(math.log(REL_MAX_DIST / max_exact))
                         * np.float32(REL_BUCKETS - max_exact)).astype(np.int32)
    large = np.minimum(large, REL_BUCKETS - 1)
    return np.where(dist < max_exact, dist, large)


def _bias_selector(window, dil):
    back = window // dil
    x = np.arange(2 * DIL_BLOCK)
    rel = DIL_BLOCK - x
    valid = (rel >= 0) & (rel <= back)
    bucket = _t5_bucket_np(np.clip(rel, 0, back) * dil)
    sel = np.zeros((LANES, 2 * DIL_BLOCK), np.float32)
    sel[bucket[valid], x[valid]] = 1.0
    mrow = np.where(valid, 0.0, NEG_INF).astype(np.float32)[None]
    return jnp.asarray(sel, BF16), jnp.asarray(mrow)


def _dilated(src, rb_hi, rb_lo, batch, seq, window, dil):
    sub_len = seq // dil
    nb = sub_len // DIL_BLOCK
    sel, mrow = _bias_selector(window, dil)
    nu = DIL_UNITS[dil]
    consts = (rb_hi, rb_lo, sel, mrow)
    const = lambda a: pl.BlockSpec(a.shape, lambda *_: (0, 0))
    stat_shape = jax.ShapeDtypeStruct((batch * seq, LANES), F32)
    out_shape = [jax.ShapeDtypeStruct((batch * seq, D_DIL), BF16), stat_shape, stat_shape]
    bias_scratch = pltpu.VMEM((2, N_HEADS_DIL, DIL_BLOCK, 2 * DIL_BLOCK), F32)

    if dil == 1:
        rows = nu * DIL_BLOCK
        steps = nb // nu
        cur = lambda col: pl.BlockSpec((None, None, rows, D_DIL), lambda b, l: (b, 0, l, col))
        prev = lambda col: pl.BlockSpec((None, None, DIL_BLOCK, D_DIL),
                                        lambda b, l: (b, 0, jnp.maximum(nu * l - 1, 0), col))
        out = lambda width: pl.BlockSpec((rows, width), lambda b, l: (b * steps + l, 0))
        return pl.pallas_call(
            _dil1_kernel,
            grid=(batch, steps),
            in_specs=[cur(0), prev(1), cur(1), prev(2), cur(2)] + [const(a) for a in consts],
            out_specs=[out(D_DIL), out(LANES), out(LANES)],
            out_shape=out_shape,
            scratch_shapes=[bias_scratch],
            compiler_params=_cparams(("arbitrary", "arbitrary")),
            name="dilated_d1",
        )(src, src, src, src, src, *consts)

    rows = DIL_BLOCK * dil
    cur = lambda col: pl.BlockSpec((None, nu, DIL_BLOCK, D_DIL), lambda b, l, g: (b, g, l, col))
    prev = lambda col: pl.BlockSpec((None, nu, DIL_BLOCK, D_DIL),
                                    lambda b, l, g: (b, g, jnp.maximum(l - 1, 0), col))
    out = lambda width: pl.BlockSpec((rows, width), lambda b, l, g: (b * nb + l, 0))
    return pl.pallas_call(
        functools.partial(_dil_kernel, dil=dil),
        grid=(batch, nb, dil // nu),
        in_specs=[cur(0), prev(1), cur(1), prev(2), cur(2)] + [const(a) for a in consts],
        out_specs=[out(D_DIL), out(LANES), out(LANES)],
        out_shape=out_shape,
        scratch_shapes=[bias_scratch, pltpu.VMEM((N_HEADS_DIL // 2, rows, LANES), F32)],
        compiler_params=_cparams(("arbitrary", "arbitrary", "arbitrary")),
        name=f"dilated_d{dil}",
    )(src, src, src, src, src, *consts)


GLA_TM = 256


GLA_NC = GLA_TM // GLA_CHUNK


def _gla_tile(bb, q_ref, k_ref, v_ref, g_ref, w2h_ref, w2l_ref, bg_ref, nw_ref, tri_ref, o_ref, st_ref):
    g = g_ref[bb, :, :LANES]
    z = (jnp.dot(g, w2h_ref[...], preferred_element_type=F32)
         + jnp.dot(g, w2l_ref[...], preferred_element_type=F32) + bg_ref[...])
    la = (jnp.minimum(z, 0.0) * (1.0 / GLA_TAU)
          - jnp.log2(1.0 + jnp.exp2(jnp.abs(z) * -LOG2E)) * (math.log(2.0) / GLA_TAU))
    la_h = la.astype(BF16)
    la_l = (la - la_h.astype(F32)).astype(BF16)

    tri = tri_ref[...]
    b = jnp.dot(tri, la_h, preferred_element_type=F32) + jnp.dot(tri, la_l, preferred_element_type=F32)
    b_last = jnp.concatenate(
        [jnp.broadcast_to(b[(ci + 1) * GLA_CHUNK - 1:(ci + 1) * GLA_CHUNK, :], (GLA_CHUNK, D_GLA_K))
         for ci in range(GLA_NC)], axis=0)

    q = q_ref[bb].astype(F32)
    k = k_ref[bb].astype(F32)
    q_t = (q * jnp.exp(b)).astype(BF16)
    k_t = (k * jnp.exp(-b)).astype(BF16)
    k_s = k * jnp.exp(b_last - b)

    causal = tri.astype(F32) > 0.0
    t_chunk = lax.broadcasted_iota(jnp.int32, (GLA_DK, GLA_TM), 1) // GLA_CHUNK
    nw = nw_ref[...]
    for h in range(N_HEADS_GLA):
        cs = slice(h * GLA_DK, (h + 1) * GLA_DK)
        vs = slice(h * GLA_DV, (h + 1) * GLA_DV)
        v_h = v_ref[bb, :, vs]
        a = lax.dot_general(q_t[:, cs], k_t[:, cs], (((1,), (1,)), ((), ())), preferred_element_type=F32)
        o = jnp.dot(jnp.where(causal, a, 0.0).astype(BF16), v_h, preferred_element_type=F32)
        k_s_t = k_s[:, cs].T.astype(BF16)
        zero = jnp.zeros_like(k_s_t)
        kv_all = jnp.dot(jnp.concatenate([jnp.where(t_chunk == ci, k_s_t, zero) for ci in range(GLA_NC)], axis=0),
                         v_h, preferred_element_type=F32)
        decay_t = jnp.exp(b_last[:, cs].T)
        st = st_ref[bb, h]
        outs = []
        for ci in range(GLA_NC):
            rs = slice(ci * GLA_CHUNK, (ci + 1) * GLA_CHUNK)
            o_c = o[rs] + jnp.dot(q_t[rs, cs], st.astype(BF16), preferred_element_type=F32)
            decay = jnp.broadcast_to(decay_t[:, ci * GLA_CHUNK:ci * GLA_CHUNK + 1], (GLA_DK, GLA_DV))
            st = decay * st + kv_all[ci * GLA_DK:(ci + 1) * GLA_DK]
            ms = jnp.mean(o_c * o_c, axis=-1, keepdims=True)
            outs.append((o_c * lax.rsqrt(ms + EPS) * nw).astype(o_ref.dtype))
        st_ref[bb, h] = st
        o_ref[bb, :, vs] = jnp.concatenate(outs, axis=0)


def _gla_kernel(*refs):
    *tile_refs, wout_ref, o_ref, wout_bf_ref, st_ref = refs

    @pl.when(pl.program_id(0) == 0)
    def _():
        st_ref[...] = jnp.zeros_like(st_ref)

    wout_bf_ref[...] = wout_ref[...].astype(wout_bf_ref.dtype)

    for bb in range(st_ref.shape[0]):
        _gla_tile(bb, *tile_refs, o_ref, st_ref)


def _gla(proj, w2_hi, w2_lo, b_gate, norm_w, w_out, batch, seq):
    steps = seq // GLA_TM
    wout_rows = w_out.shape[0] // steps
    pos = np.arange(GLA_TM)
    tri = jnp.asarray((pos[:, None] // GLA_CHUNK == pos[None, :] // GLA_CHUNK) & (pos[None, :] <= pos[:, None]),
                      BF16)
    proj3 = proj.reshape(batch, seq, REST_W)
    rows = lambda width, col: pl.BlockSpec((batch, GLA_TM, width), lambda i: (0, i, col // width))
    const = lambda shape: pl.BlockSpec(shape, lambda i: (0, 0))
    wout_spec = pl.BlockSpec((wout_rows, w_out.shape[1]), lambda i: (i, 0))
    out, w_out_bf = pl.pallas_call(
        _gla_kernel,
        grid=(steps,),
        in_specs=[
            rows(D_GLA_K, COL_GQ), rows(D_GLA_K, COL_GK), rows(D_GLA_V, COL_GV), rows(GLR_PAD, COL_GLR),
            const((LANES, D_GLA_K)), const((LANES, D_GLA_K)), const((1, D_GLA_K)), const((1, GLA_DV)),
            const((GLA_TM, GLA_TM)), wout_spec,
        ],
        out_specs=[rows(D_GLA_V, 0), wout_spec],
        out_shape=[jax.ShapeDtypeStruct((batch, seq, D_GLA_V), BF16), jax.ShapeDtypeStruct(w_out.shape, BF16)],
        scratch_shapes=[pltpu.VMEM((batch, N_HEADS_GLA, GLA_DK, GLA_DV), F32)],
        compiler_params=_cparams(("arbitrary",)),
        name="gla",
    )(proj3, proj3, proj3, proj3, w2_hi, w2_lo, b_gate, norm_w, tri, w_out)
    return out.reshape(batch * seq, D_GLA_V), w_out_bf


MEM_TM = 1024
MEM_UNIT = 128


def _mem_kernel(q_ref, mem_ref, mnw_ref, wkv_ref, o_ref, den_ref, kv_ref):
    @pl.when(pl.program_id(1) == 0)
    def _():
        m = mem_ref[0]
        ms = jnp.mean(m * m, axis=-1, keepdims=True)
        mn = (m * lax.rsqrt(ms + EPS) * mnw_ref[...]).astype(BF16)
        kv_ref[...] = jnp.dot(mn, wkv_ref[...].astype(BF16), preferred_element_type=F32).astype(BF16)

    lane = lax.broadcasted_iota(jnp.int32, (MEM_UNIT, LANES), 1)
    lo = lane < HEAD_DIM
    for u in range(MEM_TM // MEM_UNIT):
        rows = slice(u * MEM_UNIT, (u + 1) * MEM_UNIT)
        den_c = jnp.ones((MEM_UNIT, LANES), F32)
        for j in range(N_HEADS_MEM // 2):
            sl = slice(j * LANES, (j + 1) * LANES)
            r_e, r_o, _, _ = _pair_attention(q_ref[rows, sl], kv_ref[:, sl],
                                             kv_ref[:, D_MEM + j * LANES:D_MEM + (j + 1) * LANES], None, None)
            o_ref[rows, sl] = jnp.where(lo, r_e, r_o).astype(o_ref.dtype)
            den_c = jnp.where(lane == _stat_lane(2 * j), r_e, jnp.where(lane == _stat_lane(2 * j + 1), r_o, den_c))
        den_ref[rows, :] = den_c


def _mem_attn(proj, mem, mem_norm_w, w_kv, batch, seq):
    steps = seq // MEM_TM
    rows = lambda width, col=0: pl.BlockSpec((MEM_TM, width), lambda b, i: (b * steps + i, col // width))
    return pl.pallas_call(
        _mem_kernel,
        grid=(batch, steps),
        in_specs=[
            rows(D_MEM, COL_MQ),
            pl.BlockSpec((1, MEM_LEN, D_MODEL), lambda b, i: (b, 0, 0)),
            pl.BlockSpec((1, D_MODEL), lambda b, i: (0, 0)),
            pl.BlockSpec((D_MODEL, 2 * D_MEM), lambda b, i: (0, 0)),
        ],
        out_specs=[rows(D_MEM), rows(LANES)],
        out_shape=[jax.ShapeDtypeStruct((batch * seq, D_MEM), BF16),
                   jax.ShapeDtypeStruct((batch * seq, LANES), F32)],
        scratch_shapes=[pltpu.VMEM((MEM_LEN, 2 * D_MEM), BF16)],
        compiler_params=_cparams(("parallel", "arbitrary")),
        name="mem_attn",
    )(proj, mem, mem_norm_w, w_kv)


OUT_TM = 512


def _silu(g):
    h = g * 0.5
    return h * jnp.tanh(h) + h


OUT_GW = 256
OUT_CW = 512


def _out_tile(read_ref, write_ref, n1_ref, n2_ref, n3_ref, m1_ref, m2_ref, m3_ref, d1_ref, d2_ref, d3_ref,
              og_ref, nm_ref, dm_ref, ga_ref, gg_ref, gm_ref, ex_ref, exm_ref, w_ref, x_ref, nw_ref, out_ref):
    m1, m2, m3 = m1_ref[...], m2_ref[...], m3_ref[...]
    mx = jnp.maximum(jnp.maximum(m1, m2), m3)
    t1, t2, t3 = jnp.exp2(m1 - mx), jnp.exp2(m2 - mx), jnp.exp2(m3 - mx)
    inv = 1.0 / (t1 * d1_ref[...] + t2 * d2_ref[...] + t3 * d3_ref[...])
    wts = [(t * inv).astype(BF16) for t in (t1, t2, t3)]
    inv_m = (1.0 / dm_ref[...]).astype(BF16)

    def gate_chunk(lo):
        cols = slice(lo, lo + OUT_GW)
        if lo < D_DIL:
            a = jnp.zeros((OUT_TM, OUT_GW), F32)
            for wt, n_ref in zip(wts, (n1_ref, n2_ref, n3_ref)):
                a = a + jnp.dot(wt, ex_ref[:, cols], preferred_element_type=F32) * n_ref[:, cols].astype(F32)
            g = ga_ref[:, cols]
        elif lo < D_DIL + D_GLA_V:
            lc = slice(lo - D_DIL, lo - D_DIL + OUT_GW)
            a, g = og_ref[:, lc], gg_ref[:, lc]
        else:
            a = jnp.dot(inv_m, exm_ref[...], preferred_element_type=F32) * nm_ref[...].astype(F32)
            g = gm_ref[...]
        write_ref[:, cols] = a.astype(BF16) * _silu(g)

    ssq = jnp.zeros((OUT_TM, 1), F32)
    for c in range(D_MODEL // OUT_CW):
        for lo in range(c * OUT_CW, (c + 1) * OUT_CW, OUT_GW):
            gate_chunk(lo)
        cols = slice(c * OUT_CW, (c + 1) * OUT_CW)
        y = jnp.dot(read_ref[...], w_ref[:, cols], preferred_element_type=F32)
        ssq = ssq + jnp.sum(y * y, axis=-1, keepdims=True)
        out_ref[:, cols] = y
    scale = lax.rsqrt(ssq * (1.0 / D_MODEL) + EPS)
    out_ref[...] = x_ref[...] + out_ref[...] * scale * nw_ref[...]


def _out_kernel(*refs):
    mix_even, mix_odd = refs[-2:]
    s = pl.program_id(0)

    @pl.when(s == 0)
    def _():
        mix_odd[...] = jnp.zeros_like(mix_odd)

    @pl.when(s % 2 == 0)
    def _():
        _out_tile(mix_odd, mix_even, *refs[:-2])

    @pl.when(s % 2 == 1)
    def _():
        _out_tile(mix_even, mix_odd, *refs[:-2])


def _out_proj(pats, o_gla, mem_pair, proj, expand, expand_mem, w_out, x2d, norm_w):
    m = x2d.shape[0]
    n_tiles = m // OUT_TM
    gate_rows = lambda width, col=0: pl.BlockSpec(
        (OUT_TM, width), lambda s: (jnp.minimum(s, n_tiles - 1), col // width))
    proj_rows = lambda width: pl.BlockSpec((OUT_TM, width), lambda s: (jnp.maximum(s - 1, 0), 0))
    const = lambda shape: pl.BlockSpec(shape, lambda s: (0, 0), pipeline_mode=pl.Buffered(1))
    nums, mxs, dens = zip(*pats)
    return pl.pallas_call(
        _out_kernel,
        grid=(n_tiles + 1,),
        in_specs=[
            *([gate_rows(D_DIL)] * 3), *([gate_rows(LANES)] * 6),
            gate_rows(D_GLA_V), gate_rows(D_MEM), gate_rows(LANES),
            gate_rows(D_DIL, COL_GATE_A), gate_rows(D_GLA_V, COL_GATE_G), gate_rows(D_MEM, COL_GATE_M),
            const((LANES, D_DIL)), const((LANES, D_MEM)), const((D_MIX, D_MODEL)),
            proj_rows(D_MODEL), const((1, D_MODEL)),
        ],
        out_specs=proj_rows(D_MODEL),
        out_shape=jax.ShapeDtypeStruct((m, D_MODEL), x2d.dtype),
        scratch_shapes=[pltpu.VMEM((OUT_TM, D_MIX), BF16), pltpu.VMEM((OUT_TM, D_MIX), BF16)],
        compiler_params=_cparams(("arbitrary",)),
        name="out_proj",
    )(*nums, *mxs, *dens, o_gla, *mem_pair, proj, proj, proj, expand, expand_mem, w_out, x2d, norm_w)


D_IN_PROJ = 3 * D_DIL + 2 * D_GLA_K + D_GLA_V + GLA_GATE_RANK + D_MEM + D_MIX
SRC_GLR = 3 * D_DIL + 2 * D_GLA_K + D_GLA_V
SRC_MQ = SRC_GLR + GLA_GATE_RANK
SRC_GATE = SRC_MQ + D_MEM
REST_SEGMENTS = (
    (QKV_W, 2 * D_GLA_K + D_GLA_V, COL_GQ),
    (SRC_GATE + D_DIL, D_GLA_V, COL_GATE_G),
    (SRC_MQ, D_MEM, COL_MQ),
    (SRC_GATE + D_DIL + D_GLA_V, D_MEM, COL_GATE_M),
    (SRC_GATE, D_DIL, COL_GATE_A),
)
WPREP_TK = 512


def _wprep_kernel(wt_ref, qkv_ref, rest_ref):
    def put(o_ref, src, width, dst):
        for c in range(width // LANES):
            lo = c * LANES
            o_ref[:, dst + lo:dst + lo + LANES] = wt_ref[src + lo:src + lo + LANES, :].T.astype(o_ref.dtype)

    put(qkv_ref, 0, QKV_W, 0)
    for src, width, dst in REST_SEGMENTS:
        put(rest_ref, src, width, dst)
    glr = wt_ref[SRC_GLR:SRC_GLR + LANES, :].T
    keep = lax.broadcasted_iota(jnp.int32, glr.shape, 1) < GLA_GATE_RANK
    rest_ref[:, COL_GLR:COL_GLR + LANES] = jnp.where(keep, glr, 0.0).astype(rest_ref.dtype)
    rest_ref[:, COL_GLR + LANES:COL_GLR + GLR_PAD] = jnp.zeros((WPREP_TK, GLR_PAD - LANES), rest_ref.dtype)


def _regroup_in_weights(w_in):
    w_qkv, w_rest = pl.pallas_call(
        _wprep_kernel,
        grid=(D_MODEL // WPREP_TK,),
        in_specs=[pl.BlockSpec((D_IN_PROJ, WPREP_TK), lambda i: (0, i))],
        out_specs=[pl.BlockSpec((WPREP_TK, QKV_W), lambda i: (i, 0)),
                   pl.BlockSpec((WPREP_TK, REST_W), lambda i: (i, 0))],
        out_shape=[jax.ShapeDtypeStruct((D_MODEL, QKV_W), BF16), jax.ShapeDtypeStruct((D_MODEL, REST_W), BF16)],
        compiler_params=_cparams(("parallel",)),
        name="weight_regroup",
    )(w_in.T)
    scale_qkv = np.ones((1, QKV_W), np.float32)
    scale_qkv[:, :D_DIL] = HEAD_DIM ** -0.5 * LOG2E
    scale_rest = np.ones((1, REST_W), np.float32)
    scale_rest[:, COL_GQ:COL_GQ + D_GLA_K] = GLA_DK ** -0.5
    scale_rest[:, COL_MQ:COL_MQ + D_MEM] = HEAD_DIM ** -0.5 * LOG2E
    return w_qkv, w_rest, jnp.asarray(scale_qkv), jnp.asarray(scale_rest)


def _split_hi_lo(w):
    hi = w.astype(BF16)
    return hi, (w - hi.astype(F32)).astype(BF16)


def kernel(x, mem, norm_pre_w, w_in, rel_bias, w_gla_gate2, b_gla_gate, gla_norm_w, mem_norm_w, w_mem_kv,
           w_out, norm_post_w):
    batch, seq, _ = x.shape
    depth = w_in.shape[0]
    expand_np = np.zeros((LANES, D_DIL), np.float32)
    for h in range(N_HEADS_DIL):
        expand_np[_stat_lane(h), h * HEAD_DIM:(h + 1) * HEAD_DIM] = 1.0
    expand = jnp.asarray(expand_np, BF16)
    expand_mem = jnp.asarray(expand_np[:, :D_MEM], BF16)
    rb_t = jnp.zeros((16, LANES), F32).at[:N_HEADS_DIL, :REL_BUCKETS].set(rel_bias.astype(F32).T)
    rb_hi, rb_lo = _split_hi_lo(rb_t)
    for l in range(depth):
        x2d = x.reshape(batch * seq, D_MODEL)
        w_qkv, w_rest, scale_qkv, scale_rest = _regroup_in_weights(w_in[l])
        h, qkv1, qkv4, qkv16 = _qkv_proj(x2d, norm_pre_w[l][None], w_qkv, scale_qkv, batch, seq)
        rest = _rest_proj(h, w_rest, scale_rest)

        srcs = (qkv1.reshape(batch, 1, seq, QKV_W), qkv4, qkv16)
        pats = [_dilated(src, rb_hi, rb_lo, batch, seq, window, dil)
                for (window, dil), src in zip(DIL_PATTERNS, srcs)]

        w2 = jnp.zeros((LANES, D_GLA_K), F32).at[:GLA_GATE_RANK].set(w_gla_gate2[l])
        w2_hi, w2_lo = _split_hi_lo(w2)
        o_gla, w_out_bf = _gla(rest, w2_hi, w2_lo, b_gla_gate[l][None], gla_norm_w[l][None], w_out[l],
                               batch, seq)

        mem_pair = _mem_attn(rest, mem, mem_norm_w[l][None], w_mem_kv[l], batch, seq)

        out = _out_proj(pats, o_gla, mem_pair, rest, expand, expand_mem, w_out_bf, x2d, norm_post_w[l][None])
        x = out.reshape(batch, seq, D_MODEL)
    return x
```

```python
import functools
import math

import numpy as np
import jax
import jax.numpy as jnp
from jax import lax
from jax.experimental import pallas as pl
from jax.experimental.pallas import tpu as pltpu

F32 = jnp.float32
BF16 = jnp.bfloat16

D_MODEL = 2048
HEAD_DIM = 64
N_HEADS_DIL = 12
DIL_PATTERNS = ((128, 1), (512, 4), (2048, 16))
DIL_BLOCK = 128
N_HEADS_GLA = 4
GLA_DK = 128
GLA_DV = 256
GLA_GATE_RANK = 16
GLA_TAU = 16.0
GLA_CHUNK = 64
N_HEADS_MEM = 4
MEM_LEN = 256
REL_BUCKETS = 32
REL_MAX_DIST = 2048
EPS = 1e-6
NEG_INF = -1e30

D_DIL = N_HEADS_DIL * HEAD_DIM
D_GLA_K = N_HEADS_GLA * GLA_DK
D_GLA_V = N_HEADS_GLA * GLA_DV
D_MEM = N_HEADS_MEM * HEAD_DIM
D_MIX = D_DIL + D_GLA_V + D_MEM

LANES = 128
GLR_PAD = 256

QKV_W = 3 * D_DIL
COL_GQ = 0
COL_GK = COL_GQ + D_GLA_K
COL_GV = COL_GK + D_GLA_K
COL_GATE_G = COL_GV + D_GLA_V
COL_MQ = COL_GATE_G + D_GLA_V
COL_GATE_M = COL_MQ + D_MEM
COL_GLR = COL_GATE_M + D_MEM
COL_GATE_A = COL_GLR + GLR_PAD
REST_W = COL_GATE_A + D_DIL

VMEM_LIMIT = 56 * 1024 * 1024


def _cparams(sem):
    return pltpu.CompilerParams(dimension_semantics=sem, vmem_limit_bytes=VMEM_LIMIT)


QKV_TM = 512
N_SLABS = D_DIL // LANES
REGROUP_DILS = (4, 16)


def _qkv_proj_kernel(x_ref, nw_ref, w_ref, sc_ref, h_ref, o_ref, x4_ref, x16_ref, slab_ref):
    x = x_ref[...]
    ms = jnp.mean(x * x, axis=-1, keepdims=True)
    h_ref[...] = (x * lax.rsqrt(ms + EPS) * nw_ref[...]).astype(BF16)
    for t in range(QKV_W // D_DIL):
        cols = slice(t * D_DIL, (t + 1) * D_DIL)
        res = jnp.dot(h_ref[...], w_ref[:, cols], preferred_element_type=F32) * sc_ref[:, cols]
        o_ref[:, cols] = res.astype(o_ref.dtype)
        for s in range(N_SLABS):
            slab_ref[t, s] = res[:, s * LANES:(s + 1) * LANES]
        for out_ref, dil in zip((x4_ref, x16_ref), REGROUP_DILS):
            for r in range(dil):
                for s in range(N_SLABS):
                    lo = t * D_DIL + s * LANES
                    out_ref[0, r, :, lo:lo + LANES] = (
                        slab_ref[t, s, pl.ds(r, QKV_TM // dil, stride=dil), :].astype(out_ref.dtype))


def _qkv_proj(x2d, norm_w, w_qkv, scale, batch, seq):
    m = x2d.shape[0]
    tiles_per_seq = seq // QKV_TM
    const = lambda shape: pl.BlockSpec(shape, lambda i: (0, 0), pipeline_mode=pl.Buffered(1))
    grouped = lambda dil: pl.BlockSpec((1, dil, QKV_TM // dil, QKV_W),
                                       lambda i: (i // tiles_per_seq, 0, i % tiles_per_seq, 0))
    return pl.pallas_call(
        _qkv_proj_kernel,
        grid=(m // QKV_TM,),
        in_specs=[
            pl.BlockSpec((QKV_TM, D_MODEL), lambda i: (i, 0)),
            const((1, D_MODEL)), const((D_MODEL, QKV_W)), const((1, QKV_W)),
        ],
        out_specs=[pl.BlockSpec((QKV_TM, D_MODEL), lambda i: (i, 0)),
                   pl.BlockSpec((QKV_TM, QKV_W), lambda i: (i, 0))] + [grouped(d) for d in REGROUP_DILS],
        out_shape=[jax.ShapeDtypeStruct((m, D_MODEL), BF16), jax.ShapeDtypeStruct((m, QKV_W), BF16)]
        + [jax.ShapeDtypeStruct((batch, d, seq // d, QKV_W), BF16) for d in REGROUP_DILS],
        scratch_shapes=[pltpu.VMEM((QKV_W // D_DIL, N_SLABS, QKV_TM, LANES), F32)],
        compiler_params=_cparams(("parallel",)),
        name="qkv_proj",
    )(x2d, norm_w, w_qkv, scale)


REST_TM = 2048
REST_TN = 768


def _rest_proj_kernel(h_ref, w_ref, sc_ref, o_ref):
    acc = jnp.dot(h_ref[...], w_ref[...], preferred_element_type=F32)
    o_ref[...] = (acc * sc_ref[...]).astype(o_ref.dtype)


def _rest_proj(h, w_rest, scale):
    m = h.shape[0]
    return pl.pallas_call(
        _rest_proj_kernel,
        grid=(m // REST_TM, REST_W // REST_TN),
        in_specs=[
            pl.BlockSpec((REST_TM, D_MODEL), lambda i, j: (i, 0)),
            pl.BlockSpec((D_MODEL, REST_TN), lambda i, j: (0, j)),
            pl.BlockSpec((1, REST_TN), lambda i, j: (0, j)),
        ],
        out_specs=pl.BlockSpec((REST_TM, REST_TN), lambda i, j: (i, j)),
        out_shape=jax.ShapeDtypeStruct((m, REST_W), BF16),
        compiler_params=_cparams(("parallel", "arbitrary")),
        name="rest_proj",
    )(h, w_rest, scale)


LOG2E = math.log2(math.e)


def _pair_attention(q_pair, k_pair, v_pair, bias_e, bias_o):
    rows = q_pair.shape[0]
    lo_q = lax.broadcasted_iota(jnp.int32, q_pair.shape, 1) < HEAD_DIM
    zero = jnp.zeros_like(q_pair)
    q_both = jnp.concatenate([jnp.where(lo_q, q_pair, zero), jnp.where(lo_q, zero, q_pair)], axis=0)
    s = lax.dot_general(q_both, k_pair, (((1,), (1,)), ((), ())), preferred_element_type=F32)
    s_e, s_o = s[:rows], s[rows:]
    if bias_e is not None:
        s_e = s_e + bias_e
        s_o = s_o + bias_o
    mx_e = jnp.max(s_e, axis=-1, keepdims=True)
    mx_o = jnp.max(s_o, axis=-1, keepdims=True)
    p_e = jnp.exp2(s_e - mx_e).astype(BF16)
    p_o = jnp.exp2(s_o - mx_o).astype(BF16)
    lo_v = lax.broadcasted_iota(jnp.int32, v_pair.shape, 1) < HEAD_DIM
    one = jnp.ones_like(v_pair)
    r_e = jnp.dot(p_e, jnp.where(lo_v, v_pair, one), preferred_element_type=F32)
    r_o = jnp.dot(p_o, jnp.where(lo_v, one, v_pair), preferred_element_type=F32)
    return r_e, r_o, mx_e, mx_o


def _stat_lane(head):
    return head + HEAD_DIM if head % 2 == 0 else head


DIL_UNITS = {1: 8, 4: 4, 16: 8}


def _dil_init_bias(rbh_ref, rbl_ref, sel_ref, mrow_ref, bias_ref):
    f = (jnp.dot(rbh_ref[...], sel_ref[...], preferred_element_type=F32)
         + jnp.dot(rbl_ref[...], sel_ref[...], preferred_element_type=F32)) * LOG2E + mrow_ref[...]
    col = lax.broadcasted_iota(jnp.int32, (DIL_BLOCK, 2 * DIL_BLOCK), 1)
    for h in range(N_HEADS_DIL):
        row = jnp.broadcast_to(f[h:h + 1, :], (DIL_BLOCK, 2 * DIL_BLOCK))
        tab = pltpu.roll(row, 0, 1, stride=1, stride_axis=0)
        bias_ref[0, h] = tab
        bias_ref[1, h] = jnp.where(col >= DIL_BLOCK, tab, NEG_INF)


def _dil_unit(q, kp, kc, vp, vc, first, bias_ref, put_num):
    lane = lax.broadcasted_iota(jnp.int32, (DIL_BLOCK, LANES), 1)
    lo = lane < HEAD_DIM
    mx_c = jnp.zeros((DIL_BLOCK, LANES), F32)
    den_c = jnp.ones((DIL_BLOCK, LANES), F32)
    for j in range(N_HEADS_DIL // 2):
        sl = slice(j * LANES, (j + 1) * LANES)
        k_pair = jnp.concatenate([kp(sl), kc(sl)], axis=0)
        v_pair = jnp.concatenate([vp(sl), vc(sl)], axis=0)
        r_e, r_o, mx_e, mx_o = _pair_attention(q(sl), k_pair, v_pair,
                                               bias_ref[first, 2 * j], bias_ref[first, 2 * j + 1])
        put_num(j, jnp.where(lo, r_e, r_o))
        is_e, is_o = lane == _stat_lane(2 * j), lane == _stat_lane(2 * j + 1)
        mx_c = jnp.where(is_e, mx_e, jnp.where(is_o, mx_o, mx_c))
        den_c = jnp.where(is_e, r_e, jnp.where(is_o, r_o, den_c))
    return mx_c, den_c


def _dil1_kernel(q_ref, kp_ref, kc_ref, vp_ref, vc_ref, rbh_ref, rbl_ref, sel_ref, mrow_ref,
                 o_ref, mx_ref, den_ref, bias_ref):
    @pl.when((pl.program_id(0) == 0) & (pl.program_id(1) == 0))
    def _():
        _dil_init_bias(rbh_ref, rbl_ref, sel_ref, mrow_ref, bias_ref)

    for u in range(DIL_UNITS[1]):
        rows = slice(u * DIL_BLOCK, (u + 1) * DIL_BLOCK)
        prev = slice((u - 1) * DIL_BLOCK, u * DIL_BLOCK)
        if u == 0:
            first = (pl.program_id(1) == 0).astype(jnp.int32)
            kp, vp = (lambda sl: kp_ref[:, sl]), (lambda sl: vp_ref[:, sl])
        else:
            first = 0
            kp, vp = (lambda sl, p=prev: kc_ref[p, sl]), (lambda sl, p=prev: vc_ref[p, sl])

        def put_num(j, num, rows=rows):
            o_ref[rows, j * LANES:(j + 1) * LANES] = num.astype(o_ref.dtype)

        mx_c, den_c = _dil_unit(lambda sl, r=rows: q_ref[r, sl], kp, lambda sl, r=rows: kc_ref[r, sl],
                                vp, lambda sl, r=rows: vc_ref[r, sl], first, bias_ref, put_num)
        mx_ref[rows, :] = mx_c
        den_ref[rows, :] = den_c


def _dil_kernel(q_ref, kp_ref, kc_ref, vp_ref, vc_ref, rbh_ref, rbl_ref, sel_ref, mrow_ref,
                o_ref, mx_ref, den_ref, bias_ref, acc_ref, *, dil):
    b, lb, g = pl.program_id(0), pl.program_id(1), pl.program_id(2)
    nr = DIL_UNITS[dil]

    @pl.when((b == 0) & (lb == 0) & (g == 0))
    def _():
        _dil_init_bias(rbh_ref, rbl_ref, sel_ref, mrow_ref, bias_ref)

    first = (lb == 0).astype(jnp.int32)
    for u in range(nr):
        rows = pl.ds(g * nr + u, DIL_BLOCK, stride=dil)

        def put_num(j, num, rows=rows):
            acc_ref[j, rows, :] = num

        mx_c, den_c = _dil_unit(lambda sl, u=u: q_ref[u, :, sl], lambda sl, u=u: kp_ref[u, :, sl],
                                lambda sl, u=u: kc_ref[u, :, sl], lambda sl, u=u: vp_ref[u, :, sl],
                                lambda sl, u=u: vc_ref[u, :, sl], first, bias_ref, put_num)
        mx_ref[rows, :] = mx_c
        den_ref[rows, :] = den_c

    @pl.when(g == dil // nr - 1)
    def _():
        for j in range(N_HEADS_DIL // 2):
            o_ref[:, j * LANES:(j + 1) * LANES] = acc_ref[j].astype(o_ref.dtype)


def _t5_bucket_np(dist):
    max_exact = REL_BUCKETS // 2
    n = np.maximum(dist, 1).astype(np.float32)
    large = max_exact + (np.log(n / np.float32(max_exact)) / np.float32(math.log(REL_MAX_DIST / max_exact))
                         * np.float32(REL_BUCKETS - max_exact)).astype(np.int32)
    large = np.minimum(large, REL_BUCKETS - 1)
    return np.where(dist < max_exact, dist, large)


def _bias_selector(window, dil):
    back = window // dil
    x = np.arange(2 * DIL_BLOCK)
    rel = DIL_BLOCK - x
    valid = (rel >= 0) & (rel <= back)
    bucket = _t5_bucket_np(np.clip(rel, 0, back) * dil)
    sel = np.zeros((LANES, 2 * DIL_BLOCK), np.float32)
    sel[bucket[valid], x[valid]] = 1.0
    mrow = np.where(valid, 0.0, NEG_INF).astype(np.float32)[None]
    return jnp.asarray(sel, BF16), jnp.asarray(mrow)


def _dilated(src, rb_hi, rb_lo, batch, seq, window, dil):
    sub_len = seq // dil
    nb = sub_len // DIL_BLOCK
    sel, mrow = _bias_selector(window, dil)
    nu = DIL_UNITS[dil]
    consts = (rb_hi, rb_lo, sel, mrow)
    const = lambda a: pl.BlockSpec(a.shape, lambda *_: (0, 0))
    stat_shape = jax.ShapeDtypeStruct((batch * seq, LANES), F32)
    out_shape = [jax.ShapeDtypeStruct((batch * seq, D_DIL), BF16), stat_shape, stat_shape]
    bias_scratch = pltpu.VMEM((2, N_HEADS_DIL, DIL_BLOCK, 2 * DIL_BLOCK), F32)

    if dil == 1:
        rows = nu * DIL_BLOCK
        steps = nb // nu
        cur = lambda col: pl.BlockSpec((None, None, rows, D_DIL), lambda b, l: (b, 0, l, col))
        prev = lambda col: pl.BlockSpec((None, None, DIL_BLOCK, D_DIL),
                                        lambda b, l: (b, 0, jnp.maximum(nu * l - 1, 0), col))
        out = lambda width: pl.BlockSpec((rows, width), lambda b, l: (b * steps + l, 0))
        return pl.pallas_call(
            _dil1_kernel,
            grid=(batch, steps),
            in_specs=[cur(0), prev(1), cur(1), prev(2), cur(2)] + [const(a) for a in consts],
            out_specs=[out(D_DIL), out(LANES), out(LANES)],
            out_shape=out_shape,
            scratch_shapes=[bias_scratch],
            compiler_params=_cparams(("arbitrary", "arbitrary")),
            name="dilated_d1",
        )(src, src, src, src, src, *consts)

    rows = DIL_BLOCK * dil
    cur = lambda col: pl.BlockSpec((None, nu, DIL_BLOCK, D_DIL), lambda b, l, g: (b, g, l, col))
    prev = lambda col: pl.BlockSpec((None, nu, DIL_BLOCK, D_DIL),
                                    lambda b, l, g: (b, g, jnp.maximum(l - 1, 0), col))
    out = lambda width: pl.BlockSpec((rows, width), lambda b, l, g: (b * nb + l, 0))
    return pl.pallas_call(
        functools.partial(_dil_kernel, dil=dil),
        grid=(batch, nb, dil // nu),
        in_specs=[cur(0), prev(1), cur(1), prev(2), cur(2)] + [const(a) for a in consts],
        out_specs=[out(D_DIL), out(LANES), out(LANES)],
        out_shape=out_shape,
        scratch_shapes=[bias_scratch, pltpu.VMEM((N_HEADS_DIL // 2, rows, LANES), F32)],
        compiler_params=_cparams(("arbitrary", "arbitrary", "arbitrary")),
        name=f"dilated_d{dil}",
    )(src, src, src, src, src, *consts)


GLA_TM = 256


GLA_NC = GLA_TM // GLA_CHUNK


def _gla_tile(bb, q_ref, k_ref, v_ref, g_ref, w2h_ref, w2l_ref, bg_ref, nw_ref, tri_ref, o_ref, st_ref):
    g = g_ref[bb, :, :LANES]
    z = (jnp.dot(g, w2h_ref[...], preferred_element_type=F32)
         + jnp.dot(g, w2l_ref[...], preferred_element_type=F32) + bg_ref[...])
    la = (jnp.minimum(z, 0.0) * (1.0 / GLA_TAU)
          - jnp.log2(1.0 + jnp.exp2(jnp.abs(z) * -LOG2E)) * (math.log(2.0) / GLA_TAU))
    la_h = la.astype(BF16)
    la_l = (la - la_h.astype(F32)).astype(BF16)

    tri = tri_ref[...]
    b = jnp.dot(tri, la_h, preferred_element_type=F32) + jnp.dot(tri, la_l, preferred_element_type=F32)
    b_last = jnp.concatenate(
        [jnp.broadcast_to(b[(ci + 1) * GLA_CHUNK - 1:(ci + 1) * GLA_CHUNK, :], (GLA_CHUNK, D_GLA_K))
         for ci in range(GLA_NC)], axis=0)

    q = q_ref[bb].astype(F32)
    k = k_ref[bb].astype(F32)
    q_t = (q * jnp.exp(b)).astype(BF16)
    k_t = (k * jnp.exp(-b)).astype(BF16)
    k_s = k * jnp.exp(b_last - b)

    causal = tri.astype(F32) > 0.0
    t_chunk = lax.broadcasted_iota(jnp.int32, (GLA_DK, GLA_TM), 1) // GLA_CHUNK
    nw = nw_ref[...]
    for h in range(N_HEADS_GLA):
        cs = slice(h * GLA_DK, (h + 1) * GLA_DK)
        vs = slice(h * GLA_DV, (h + 1) * GLA_DV)
        v_h = v_ref[bb, :, vs]
        a = lax.dot_general(q_t[:, cs], k_t[:, cs], (((1,), (1,)), ((), ())), preferred_element_type=F32)
        o = jnp.dot(jnp.where(causal, a, 0.0).astype(BF16), v_h, preferred_element_type=F32)
        k_s_t = k_s[:, cs].T.astype(BF16)
        zero = jnp.zeros_like(k_s_t)
        kv_all = jnp.dot(jnp.concatenate([jnp.where(t_chunk == ci, k_s_t, zero) for ci in range(GLA_NC)], axis=0),
                         v_h, preferred_element_type=F32)
        decay_t = jnp.exp(b_last[:, cs].T)
        st = st_ref[bb, h]
        outs = []
        for ci in range(GLA_NC):
            rs = slice(ci * GLA_CHUNK, (ci + 1) * GLA_CHUNK)
            o_c = o[rs] + jnp.dot(q_t[rs, cs], st.astype(BF16), preferred_element_type=F32)
            decay = jnp.broadcast_to(decay_t[:, ci * GLA_CHUNK:ci * GLA_CHUNK + 1], (GLA_DK, GLA_DV))
            st = decay * st + kv_all[ci * GLA_DK:(ci + 1) * GLA_DK]
            ms = jnp.mean(o_c * o_c, axis=-1, keepdims=True)
            outs.append((o_c * lax.rsqrt(ms + EPS) * nw).astype(o_ref.dtype))
        st_ref[bb, h] = st
        o_ref[bb, :, vs] = jnp.concatenate(outs, axis=0)


def _gla_kernel(*refs):
    *tile_refs, wout_ref, o_ref, wout_bf_ref, st_ref = refs

    @pl.when(pl.program_id(0) == 0)
    def _():
        st_ref[...] = jnp.zeros_like(st_ref)

    wout_bf_ref[...] = wout_ref[...].astype(wout_bf_ref.dtype)

    for bb in range(st_ref.shape[0]):
        _gla_tile(bb, *tile_refs, o_ref, st_ref)


def _gla(proj, w2_hi, w2_lo, b_gate, norm_w, w_out, batch, seq):
    steps = seq // GLA_TM
    wout_rows = w_out.shape[0] // steps
    pos = np.arange(GLA_TM)
    tri = jnp.asarray((pos[:, None] // GLA_CHUNK == pos[None, :] // GLA_CHUNK) & (pos[None, :] <= pos[:, None]),
                      BF16)
    proj3 = proj.reshape(batch, seq, REST_W)
    rows = lambda width, col: pl.BlockSpec((batch, GLA_TM, width), lambda i: (0, i, col // width))
    const = lambda shape: pl.BlockSpec(shape, lambda i: (0, 0))
    wout_spec = pl.BlockSpec((wout_rows, w_out.shape[1]), lambda i: (i, 0))
    out, w_out_bf = pl.pallas_call(
        _gla_kernel,
        grid=(steps,),
        in_specs=[
            rows(D_GLA_K, COL_GQ), rows(D_GLA_K, COL_GK), rows(D_GLA_V, COL_GV), rows(GLR_PAD, COL_GLR),
            const((LANES, D_GLA_K)), const((LANES, D_GLA_K)), const((1, D_GLA_K)), const((1, GLA_DV)),
            const((GLA_TM, GLA_TM)), wout_spec,
        ],
        out_specs=[rows(D_GLA_V, 0), wout_spec],
        out_shape=[jax.ShapeDtypeStruct((batch, seq, D_GLA_V), BF16), jax.ShapeDtypeStruct(w_out.shape, BF16)],
        scratch_shapes=[pltpu.VMEM((batch, N_HEADS_GLA, GLA_DK, GLA_DV), F32)],
        compiler_params=_cparams(("arbitrary",)),
        name="gla",
    )(proj3, proj3, proj3, proj3, w2_hi, w2_lo, b_gate, norm_w, tri, w_out)
    return out.reshape(batch * seq, D_GLA_V), w_out_bf


MEM_TM = 1024
MEM_UNIT = 128


def _mem_kernel(q_ref, mem_ref, mnw_ref, wkv_ref, o_ref, den_ref, kv_ref):
    @pl.when(pl.program_id(1) == 0)
    def _():
        m = mem_ref[0]
        ms = jnp.mean(m * m, axis=-1, keepdims=True)
        mn = (m * lax.rsqrt(ms + EPS) * mnw_ref[...]).astype(BF16)
        kv_ref[...] = jnp.dot(mn, wkv_ref[...].astype(BF16), preferred_element_type=F32).astype(BF16)

    lane = lax.broadcasted_iota(jnp.int32, (MEM_UNIT, LANES), 1)
    lo = lane < HEAD_DIM
    for u in range(MEM_TM // MEM_UNIT):
        rows = slice(u * MEM_UNIT, (u + 1) * MEM_UNIT)
        den_c = jnp.ones((MEM_UNIT, LANES), F32)
        for j in range(N_HEADS_MEM // 2):
            sl = slice(j * LANES, (j + 1) * LANES)
            r_e, r_o, _, _ = _pair_attention(q_ref[rows, sl], kv_ref[:, sl],
                                             kv_ref[:, D_MEM + j * LANES:D_MEM + (j + 1) * LANES], None, None)
            o_ref[rows, sl] = jnp.where(lo, r_e, r_o).astype(o_ref.dtype)
            den_c = jnp.where(lane == _stat_lane(2 * j), r_e, jnp.where(lane == _stat_lane(2 * j + 1), r_o, den_c))
        den_ref[rows, :] = den_c


def _mem_attn(proj, mem, mem_norm_w, w_kv, batch, seq):
    steps = seq // MEM_TM
    rows = lambda width, col=0: pl.BlockSpec((MEM_TM, width), lambda b, i: (b * steps + i, col // width))
    return pl.pallas_call(
        _mem_kernel,
        grid=(batch, steps),
        in_specs=[
            rows(D_MEM, COL_MQ),
            pl.BlockSpec((1, MEM_LEN, D_MODEL), lambda b, i: (b, 0, 0)),
            pl.BlockSpec((1, D_MODEL), lambda b, i: (0, 0)),
            pl.BlockSpec((D_MODEL, 2 * D_MEM), lambda b, i: (0, 0)),
        ],
        out_specs=[rows(D_MEM), rows(LANES)],
        out_shape=[jax.ShapeDtypeStruct((batch * seq, D_MEM), BF16),
                   jax.ShapeDtypeStruct((batch * seq, LANES), F32)],
        scratch_shapes=[pltpu.VMEM((MEM_LEN, 2 * D_MEM), BF16)],
        compiler_params=_cparams(("parallel", "arbitrary")),
        name="mem_attn",
    )(proj, mem, mem_norm_w, w_kv)


OUT_TM = 512


def _silu(g):
    h = g * 0.5
    return h * jnp.tanh(h) + h


OUT_GW = 256
OUT_CW = 512


def _out_tile(read_ref, write_ref, n1_ref, n2_ref, n3_ref, m1_ref, m2_ref, m3_ref, d1_ref, d2_ref, d3_ref,
              og_ref, nm_ref, dm_ref, ga_ref, gg_ref, gm_ref, ex_ref, exm_ref, w_ref, x_ref, nw_ref, out_ref):
    m1, m2, m3 = m1_ref[...], m2_ref[...], m3_ref[...]
    mx = jnp.maximum(jnp.maximum(m1, m2), m3)
    t1, t2, t3 = jnp.exp2(m1 - mx), jnp.exp2(m2 - mx), jnp.exp2(m3 - mx)
    inv = 1.0 / (t1 * d1_ref[...] + t2 * d2_ref[...] + t3 * d3_ref[...])
    wts = [(t * inv).astype(BF16) for t in (t1, t2, t3)]
    inv_m = (1.0 / dm_ref[...]).astype(BF16)

    def gate_chunk(lo):
        cols = slice(lo, lo + OUT_GW)
        if lo < D_DIL:
            a = jnp.zeros((OUT_TM, OUT_GW), F32)
            for wt, n_ref in zip(wts, (n1_ref, n2_ref, n3_ref)):
                a = a + jnp.dot(wt, ex_ref[:, cols], preferred_element_type=F32) * n_ref[:, cols].astype(F32)
            g = ga_ref[:, cols]
        elif lo < D_DIL + D_GLA_V:
            lc = slice(lo - D_DIL, lo - D_DIL + OUT_GW)
            a, g = og_ref[:, lc], gg_ref[:, lc]
        else:
            a = jnp.dot(inv_m, exm_ref[...], preferred_element_type=F32) * nm_ref[...].astype(F32)
            g = gm_ref[...]
        write_ref[:, cols] = a.astype(BF16) * _silu(g)

    ssq = jnp.zeros((OUT_TM, 1), F32)
    for c in range(D_MODEL // OUT_CW):
        for lo in range(c * OUT_CW, (c + 1) * OUT_CW, OUT_GW):
            gate_chunk(lo)
        cols = slice(c * OUT_CW, (c + 1) * OUT_CW)
        y = jnp.dot(read_ref[...], w_ref[:, cols], preferred_element_type=F32)
        ssq = ssq + jnp.sum(y * y, axis=-1, keepdims=True)
        out_ref[:, cols] = y
    scale = lax.rsqrt(ssq * (1.0 / D_MODEL) + EPS)
    out_ref[...] = x_ref[...] + out_ref[...] * scale * nw_ref[...]


def _out_kernel(*refs):
    mix_even, mix_odd = refs[-2:]
    s = pl.program_id(0)

    @pl.when(s == 0)
    def _():
        mix_odd[...] = jnp.zeros_like(mix_odd)

    @pl.when(s % 2 == 0)
    def _():
        _out_tile(mix_odd, mix_even, *refs[:-2])

    @pl.when(s % 2 == 1)
    def _():
        _out_tile(mix_even, mix_odd, *refs[:-2])


def _out_proj(pats, o_gla, mem_pair, proj, expand, expand_mem, w_out, x2d, norm_w):
    m = x2d.shape[0]
    n_tiles = m // OUT_TM
    gate_rows = lambda width, col=0: pl.BlockSpec(
        (OUT_TM, width), lambda s: (jnp.minimum(s, n_tiles - 1), col // width))
    proj_rows = lambda width: pl.BlockSpec((OUT_TM, width), lambda s: (jnp.maximum(s - 1, 0), 0))
    const = lambda shape: pl.BlockSpec(shape, lambda s: (0, 0), pipeline_mode=pl.Buffered(1))
    nums, mxs, dens = zip(*pats)
    return pl.pallas_call(
        _out_kernel,
        grid=(n_tiles + 1,),
        in_specs=[
            *([gate_rows(D_DIL)] * 3), *([gate_rows(LANES)] * 6),
            gate_rows(D_GLA_V), gate_rows(D_MEM), gate_rows(LANES),
            gate_rows(D_DIL, COL_GATE_A), gate_rows(D_GLA_V, COL_GATE_G), gate_rows(D_MEM, COL_GATE_M),
            const((LANES, D_DIL)), const((LANES, D_MEM)), const((D_MIX, D_MODEL)),
            proj_rows(D_MODEL), const((1, D_MODEL)),
        ],
        out_specs=proj_rows(D_MODEL),
        out_shape=jax.ShapeDtypeStruct((m, D_MODEL), x2d.dtype),
        scratch_shapes=[pltpu.VMEM((OUT_TM, D_MIX), BF16), pltpu.VMEM((OUT_TM, D_MIX), BF16)],
        compiler_params=_cparams(("arbitrary",)),
        name="out_proj",
    )(*nums, *mxs, *dens, o_gla, *mem_pair, proj, proj, proj, expand, expand_mem, w_out, x2d, norm_w)


D_IN_PROJ = 3 * D_DIL + 2 * D_GLA_K + D_GLA_V + GLA_GATE_RANK + D_MEM + D_MIX
SRC_GLR = 3 * D_DIL + 2 * D_GLA_K + D_GLA_V
SRC_MQ = SRC_GLR + GLA_GATE_RANK
SRC_GATE = SRC_MQ + D_MEM
REST_SEGMENTS = (
    (QKV_W, 2 * D_GLA_K + D_GLA_V, COL_GQ),
    (SRC_GATE + D_DIL, D_GLA_V, COL_GATE_G),
    (SRC_MQ, D_MEM, COL_MQ),
    (SRC_GATE + D_DIL + D_GLA_V, D_MEM, COL_GATE_M),
    (SRC_GATE, D_DIL, COL_GATE_A),
)
WPREP_TK = 512


def _wprep_kernel(wt_ref, qkv_ref, rest_ref):
    def put(o_ref, src, width, dst):
        for c in range(width // LANES):
            lo = c * LANES
            o_ref[:, dst + lo:dst + lo + LANES] = wt_ref[src + lo:src + lo + LANES, :].T.astype(o_ref.dtype)

    put(qkv_ref, 0, QKV_W, 0)
    for src, width, dst in REST_SEGMENTS:
        put(rest_ref, src, width, dst)
    glr = wt_ref[SRC_GLR:SRC_GLR + LANES, :].T
    keep = lax.broadcasted_iota(jnp.int32, glr.shape, 1) < GLA_GATE_RANK
    rest_ref[:, COL_GLR:COL_GLR + LANES] = jnp.where(keep, glr, 0.0).astype(rest_ref.dtype)
    rest_ref[:, COL_GLR + LANES:COL_GLR + GLR_PAD] = jnp.zeros((WPREP_TK, GLR_PAD - LANES), rest_ref.dtype)


def _regroup_in_weights(w_in):
    w_qkv, w_rest = pl.pallas_call(
        _wprep_kernel,
        grid=(D_MODEL // WPREP_TK,),
        in_specs=[pl.BlockSpec((D_IN_PROJ, WPREP_TK), lambda i: (0, i))],
        out_specs=[pl.BlockSpec((WPREP_TK, QKV_W), lambda i: (i, 0)),
                   pl.BlockSpec((WPREP_TK, REST_W), lambda i: (i, 0))],
        out_shape=[jax.ShapeDtypeStruct((D_MODEL, QKV_W), BF16), jax.ShapeDtypeStruct((D_MODEL, REST_W), BF16)],
        compiler_params=_cparams(("parallel",)),
        name="weight_regroup",
    )(w_in.T)
    scale_qkv = np.ones((1, QKV_W), np.float32)
    scale_qkv[:, :D_DIL] = HEAD_DIM ** -0.5 * LOG2E
    scale_rest = np.ones((1, REST_W), np.float32)
    scale_rest[:, COL_GQ:COL_GQ + D_GLA_K] = GLA_DK ** -0.5
    scale_rest[:, COL_MQ:COL_MQ + D_MEM] = HEAD_DIM ** -0.5 * LOG2E
    return w_qkv, w_rest, jnp.asarray(scale_qkv), jnp.asarray(scale_rest)


def _split_hi_lo(w):
    hi = w.astype(BF16)
    return hi, (w - hi.astype(F32)).astype(BF16)


def kernel(x, mem, norm_pre_w, w_in, rel_bias, w_gla_gate2, b_gla_gate, gla_norm_w, mem_norm_w, w_mem_kv,
           w_out, norm_post_w):
    batch, seq, _ = x.shape
    depth = w_in.shape[0]
    expand_np = np.zeros((LANES, D_DIL), np.float32)
    for h in range(N_HEADS_DIL):
        expand_np[_stat_lane(h), h * HEAD_DIM:(h + 1) * HEAD_DIM] = 1.0
    expand = jnp.asarray(expand_np, BF16)
    expand_mem = jnp.asarray(expand_np[:, :D_MEM], BF16)
    rb_t = jnp.zeros((16, LANES), F32).at[:N_HEADS_DIL, :REL_BUCKETS].set(rel_bias.astype(F32).T)
    rb_hi, rb_lo = _split_hi_lo(rb_t)
    for l in range(depth):
        x2d = x.reshape(batch * seq, D_MODEL)
        w_qkv, w_rest, scale_qkv, scale_rest = _regroup_in_weights(w_in[l])
        h, qkv1, qkv4, qkv16 = _qkv_proj(x2d, norm_pre_w[l][None], w_qkv, scale_qkv, batch, seq)
        rest = _rest_proj(h, w_rest, scale_rest)

        srcs = (qkv1.reshape(batch, 1, seq, QKV_W), qkv4, qkv16)
        pats = [_dilated(src, rb_hi, rb_lo, batch, seq, window, dil)
                for (window, dil), src in zip(DIL_PATTERNS, srcs)]

        w2 = jnp.zeros((LANES, D_GLA_K), F32).at[:GLA_GATE_RANK].set(w_gla_gate2[l])
        w2_hi, w2_lo = _split_hi_lo(w2)
        o_gla, w_out_bf = _gla(rest, w2_hi, w2_lo, b_gla_gate[l][None], gla_norm_w[l][None], w_out[l],
                               batch, seq)

        mem_pair = _mem_attn(rest, mem, mem_norm_w[l][None], w_mem_kv[l], batch, seq)

        out = _out_proj(pats, o_gla, mem_pair, rest, expand, expand_mem, w_out_bf, x2d, norm_post_w[l][None])
        x = out.reshape(batch, seq, D_MODEL)
    return x
```

```python
import functools
import math

import numpy as np
import jax
import jax.numpy as jnp
from jax import lax
from jax.experimental import pallas as pl
from jax.experimental.pallas import tpu as pltpu

F32 = jnp.float32
BF16 = jnp.bfloat16

D_MODEL = 2048
HEAD_DIM = 64
N_HEADS_DIL = 12
DIL_PATTERNS = ((128, 1), (512, 4), (2048, 16))
DIL_BLOCK = 128
N_HEADS_GLA = 4
GLA_DK = 128
GLA_DV = 256
GLA_GATE_RANK = 16
GLA_TAU = 16.0
GLA_CHUNK = 64
N_HEADS_MEM = 4
MEM_LEN = 256
REL_BUCKETS = 32
REL_MAX_DIST = 2048
EPS = 1e-6
NEG_INF = -1e30

D_DIL = N_HEADS_DIL * HEAD_DIM
D_GLA_K = N_HEADS_GLA * GLA_DK
D_GLA_V = N_HEADS_GLA * GLA_DV
D_MEM = N_HEADS_MEM * HEAD_DIM
D_MIX = D_DIL + D_GLA_V + D_MEM

LANES = 128
GLR_PAD = 256

QKV_W = 3 * D_DIL
COL_GQ = 0
COL_GK = COL_GQ + D_GLA_K
COL_GV = COL_GK + D_GLA_K
COL_GATE_G = COL_GV + D_GLA_V
COL_MQ = COL_GATE_G + D_GLA_V
COL_GATE_M = COL_MQ + D_MEM
COL_GLR = COL_GATE_M + D_MEM
COL_GATE_A = COL_GLR + GLR_PAD
REST_W = COL_GATE_A + D_DIL

VMEM_LIMIT = 56 * 1024 * 1024


def _cparams(sem):
    return pltpu.CompilerParams(dimension_semantics=sem, vmem_limit_bytes=VMEM_LIMIT)


QKV_TM = 512
N_SLABS = D_DIL // LANES
REGROUP_DILS = (4, 16)


def _qkv_proj_kernel(x_ref, nw_ref, w_ref, sc_ref, h_ref, o_ref, x4_ref, x16_ref, slab_ref):
    x = x_ref[...]
    ms = jnp.mean(x * x, axis=-1, keepdims=True)
    h_ref[...] = (x * lax.rsqrt(ms + EPS) * nw_ref[...]).astype(BF16)
    for t in range(QKV_W // D_DIL):
        cols = slice(t * D_DIL, (t + 1) * D_DIL)
        res = jnp.dot(h_ref[...], w_ref[:, cols], preferred_element_type=F32) * sc_ref[:, cols]
        o_ref[:, cols] = res.astype(o_ref.dtype)
        for s in range(N_SLABS):
            slab_ref[t, s] = res[:, s * LANES:(s + 1) * LANES]
        for out_ref, dil in zip((x4_ref, x16_ref), REGROUP_DILS):
            for r in range(dil):
                for s in range(N_SLABS):
                    lo = t * D_DIL + s * LANES
                    out_ref[0, r, :, lo:lo + LANES] = (
                        slab_ref[t, s, pl.ds(r, QKV_TM // dil, stride=dil), :].astype(out_ref.dtype))


def _qkv_proj(x2d, norm_w, w_qkv, scale, batch, seq):
    m = x2d.shape[0]
    tiles_per_seq = seq // QKV_TM
    const = lambda shape: pl.BlockSpec(shape, lambda i: (0, 0), pipeline_mode=pl.Buffered(1))
    grouped = lambda dil: pl.BlockSpec((1, dil, QKV_TM // dil, QKV_W),
                                       lambda i: (i // tiles_per_seq, 0, i % tiles_per_seq, 0))
    return pl.pallas_call(
        _qkv_proj_kernel,
        grid=(m // QKV_TM,),
        in_specs=[
            pl.BlockSpec((QKV_TM, D_MODEL), lambda i: (i, 0)),
            const((1, D_MODEL)), const((D_MODEL, QKV_W)), const((1, QKV_W)),
        ],
        out_specs=[pl.BlockSpec((QKV_TM, D_MODEL), lambda i: (i, 0)),
                   pl.BlockSpec((QKV_TM, QKV_W), lambda i: (i, 0))] + [grouped(d) for d in REGROUP_DILS],
        out_shape=[jax.ShapeDtypeStruct((m, D_MODEL), BF16), jax.ShapeDtypeStruct((m, QKV_W), BF16)]
        + [jax.ShapeDtypeStruct((batch, d, seq // d, QKV_W), BF16) for d in REGROUP_DILS],
        scratch_shapes=[pltpu.VMEM((QKV_W // D_DIL, N_SLABS, QKV_TM, LANES), F32)],
        compiler_params=_cparams(("parallel",)),
        name="qkv_proj",
    )(x2d, norm_w, w_qkv, scale)


REST_TM = 1024
REST_TN = 1536


def _rest_proj_kernel(h_ref, w_ref, sc_ref, o_ref):
    acc = jnp.dot(h_ref[...], w_ref[...], preferred_element_type=F32)
    o_ref[...] = (acc * sc_ref[...]).astype(o_ref.dtype)


def _rest_proj(h, w_rest, scale):
    m = h.shape[0]
    return pl.pallas_call(
        _rest_proj_kernel,
        grid=(m // REST_TM, REST_W // REST_TN),
        in_specs=[
            pl.BlockSpec((REST_TM, D_MODEL), lambda i, j: (i, 0)),
            pl.BlockSpec((D_MODEL, REST_TN), lambda i, j: (0, j)),
            pl.BlockSpec((1, REST_TN), lambda i, j: (0, j)),
        ],
        out_specs=pl.BlockSpec((REST_TM, REST_TN), lambda i, j: (i, j)),
        out_shape=jax.ShapeDtypeStruct((m, REST_W), BF16),
        compiler_params=_cparams(("parallel", "arbitrary")),
        name="rest_proj",
    )(h, w_rest, scale)


LOG2E = math.log2(math.e)


def _pair_attention(q_pair, k_pair, v_pair, bias_e, bias_o):
    rows = q_pair.shape[0]
    lo_q = lax.broadcasted_iota(jnp.int32, q_pair.shape, 1) < HEAD_DIM
    zero = jnp.zeros_like(q_pair)
    q_both = jnp.concatenate([jnp.where(lo_q, q_pair, zero), jnp.where(lo_q, zero, q_pair)], axis=0)
    s = lax.dot_general(q_both, k_pair, (((1,), (1,)), ((), ())), preferred_element_type=F32)
    s_e, s_o = s[:rows], s[rows:]
    if bias_e is not None:
        s_e = s_e + bias_e
        s_o = s_o + bias_o
    mx_e = jnp.max(s_e, axis=-1, keepdims=True)
    mx_o = jnp.max(s_o, axis=-1, keepdims=True)
    p_e = jnp.exp2(s_e - mx_e).astype(BF16)
    p_o = jnp.exp2(s_o - mx_o).astype(BF16)
    lo_v = lax.broadcasted_iota(jnp.int32, v_pair.shape, 1) < HEAD_DIM
    one = jnp.ones_like(v_pair)
    r_e = jnp.dot(p_e, jnp.where(lo_v, v_pair, one), preferred_element_type=F32)
    r_o = jnp.dot(p_o, jnp.where(lo_v, one, v_pair), preferred_element_type=F32)
    return r_e, r_o, mx_e, mx_o


def _stat_lane(head):
    return head + HEAD_DIM if head % 2 == 0 else head


DIL_UNITS = {1: 8, 4: 4, 16: 8}


def _dil_init_bias(rbh_ref, rbl_ref, sel_ref, mrow_ref, bias_ref):
    f = (jnp.dot(rbh_ref[...], sel_ref[...], preferred_element_type=F32)
         + jnp.dot(rbl_ref[...], sel_ref[...], preferred_element_type=F32)) * LOG2E + mrow_ref[...]
    col = lax.broadcasted_iota(jnp.int32, (DIL_BLOCK, 2 * DIL_BLOCK), 1)
    for h in range(N_HEADS_DIL):
        row = jnp.broadcast_to(f[h:h + 1, :], (DIL_BLOCK, 2 * DIL_BLOCK))
        tab = pltpu.roll(row, 0, 1, stride=1, stride_axis=0)
        bias_ref[0, h] = tab
        bias_ref[1, h] = jnp.where(col >= DIL_BLOCK, tab, NEG_INF)


def _dil_unit(q, kp, kc, vp, vc, first, bias_ref, put_num):
    lane = lax.broadcasted_iota(jnp.int32, (DIL_BLOCK, LANES), 1)
    lo = lane < HEAD_DIM
    mx_c = jnp.zeros((DIL_BLOCK, LANES), F32)
    den_c = jnp.ones((DIL_BLOCK, LANES), F32)
    for j in range(N_HEADS_DIL // 2):
        sl = slice(j * LANES, (j + 1) * LANES)
        k_pair = jnp.concatenate([kp(sl), kc(sl)], axis=0)
        v_pair = jnp.concatenate([vp(sl), vc(sl)], axis=0)
        r_e, r_o, mx_e, mx_o = _pair_attention(q(sl), k_pair, v_pair,
                                               bias_ref[first, 2 * j], bias_ref[first, 2 * j + 1])
        put_num(j, jnp.where(lo, r_e, r_o))
        is_e, is_o = lane == _stat_lane(2 * j), lane == _stat_lane(2 * j + 1)
        mx_c = jnp.where(is_e, mx_e, jnp.where(is_o, mx_o, mx_c))
        den_c = jnp.where(is_e, r_e, jnp.where(is_o, r_o, den_c))
    return mx_c, den_c


def _dil1_kernel(q_ref, kp_ref, kc_ref, vp_ref, vc_ref, rbh_ref, rbl_ref, sel_ref, mrow_ref,
                 o_ref, mx_ref, den_ref, bias_ref):
    @pl.when((pl.program_id(0) == 0) & (pl.program_id(1) == 0))
    def _():
        _dil_init_bias(rbh_ref, rbl_ref, sel_ref, mrow_ref, bias_ref)

    for u in range(DIL_UNITS[1]):
        rows = slice(u * DIL_BLOCK, (u + 1) * DIL_BLOCK)
        prev = slice((u - 1) * DIL_BLOCK, u * DIL_BLOCK)
        if u == 0:
            first = (pl.program_id(1) == 0).astype(jnp.int32)
            kp, vp = (lambda sl: kp_ref[:, sl]), (lambda sl: vp_ref[:, sl])
        else:
            first = 0
            kp, vp = (lambda sl, p=prev: kc_ref[p, sl]), (lambda sl, p=prev: vc_ref[p, sl])

        def put_num(j, num, rows=rows):
            o_ref[rows, j * LANES:(j + 1) * LANES] = num.astype(o_ref.dtype)

        mx_c, den_c = _dil_unit(lambda sl, r=rows: q_ref[r, sl], kp, lambda sl, r=rows: kc_ref[r, sl],
                                vp, lambda sl, r=rows: vc_ref[r, sl], first, bias_ref, put_num)
        mx_ref[rows, :] = mx_c
        den_ref[rows, :] = den_c


def _dil_kernel(q_ref, kp_ref, kc_ref, vp_ref, vc_ref, rbh_ref, rbl_ref, sel_ref, mrow_ref,
                o_ref, mx_ref, den_ref, bias_ref, acc_ref, *, dil):
    b, lb, g = pl.program_id(0), pl.program_id(1), pl.program_id(2)
    nr = DIL_UNITS[dil]

    @pl.when((b == 0) & (lb == 0) & (g == 0))
    def _():
        _dil_init_bias(rbh_ref, rbl_ref, sel_ref, mrow_ref, bias_ref)

    first = (lb == 0).astype(jnp.int32)
    for u in range(nr):
        rows = pl.ds(g * nr + u, DIL_BLOCK, stride=dil)

        def put_num(j, num, rows=rows):
            acc_ref[j, rows, :] = num

        mx_c, den_c = _dil_unit(lambda sl, u=u: q_ref[u, :, sl], lambda sl, u=u: kp_ref[u, :, sl],
                                lambda sl, u=u: kc_ref[u, :, sl], lambda sl, u=u: vp_ref[u, :, sl],
                                lambda sl, u=u: vc_ref[u, :, sl], first, bias_ref, put_num)
        mx_ref[rows, :] = mx_c
        den_ref[rows, :] = den_c

    @pl.when(g == dil // nr - 1)
    def _():
        for j in range(N_HEADS_DIL // 2):
            o_ref[:, j * LANES:(j + 1) * LANES] = acc_ref[j].astype(o_ref.dtype)


def _t5_bucket_np(dist):
    max_exact = REL_BUCKETS // 2
    n = np.maximum(dist, 1).astype(np.float32)
    large = max_exact + (np.log(n / np.float32(max_exact)) / np.float32(math.log(REL_MAX_DIST / max_exact))
                         * np.float32(REL_BUCKETS - max_exact)).astype(np.int32)
    large = np.minimum(large, REL_BUCKETS - 1)
    return np.where(dist < max_exact, dist, large)


def _bias_selector(window, dil):
    back = window // dil
    x = np.arange(2 * DIL_BLOCK)
    rel = DIL_BLOCK - x
    valid = (rel >= 0) & (rel <= back)
    bucket = _t5_bucket_np(np.clip(rel, 0, back) * dil)
    sel = np.zeros((LANES, 2 * DIL_BLOCK), np.float32)
    sel[bucket[valid], x[valid]] = 1.0
    mrow = np.where(valid, 0.0, NEG_INF).astype(np.float32)[None]
    return jnp.asarray(sel, BF16), jnp.asarray(mrow)


def _dilated(src, rb_hi, rb_lo, batch, seq, window, dil):
    sub_len = seq // dil
    nb = sub_len // DIL_BLOCK
    sel, mrow = _bias_selector(window, dil)
    nu = DIL_UNITS[dil]
    consts = (rb_hi, rb_lo, sel, mrow)
    const = lambda a: pl.BlockSpec(a.shape, lambda *_: (0, 0))
    stat_shape = jax.ShapeDtypeStruct((batch * seq, LANES), F32)
    out_shape = [jax.ShapeDtypeStruct((batch * seq, D_DIL), BF16), stat_shape, stat_shape]
    bias_scratch = pltpu.VMEM((2, N_HEADS_DIL, DIL_BLOCK, 2 * DIL_BLOCK), F32)

    if dil == 1:
        rows = nu * DIL_BLOCK
        steps = nb // nu
        cur = lambda col: pl.BlockSpec((None, None, rows, D_DIL), lambda b, l: (b, 0, l, col))
        prev = lambda col: pl.BlockSpec((None, None, DIL_BLOCK, D_DIL),
                                        lambda b, l: (b, 0, jnp.maximum(nu * l - 1, 0), col))
        out = lambda width: pl.BlockSpec((rows, width), lambda b, l: (b * steps + l, 0))
        return pl.pallas_call(
            _dil1_kernel,
            grid=(batch, steps),
            in_specs=[cur(0), prev(1), cur(1), prev(2), cur(2)] + [const(a) for a in consts],
            out_specs=[out(D_DIL), out(LANES), out(LANES)],
            out_shape=out_shape,
            scratch_shapes=[bias_scratch],
            compiler_params=_cparams(("arbitrary", "arbitrary")),
            name="dilated_d1",
        )(src, src, src, src, src, *consts)

    rows = DIL_BLOCK * dil
    cur = lambda col: pl.BlockSpec((None, nu, DIL_BLOCK, D_DIL), lambda b, l, g: (b, g, l, col))
    prev = lambda col: pl.BlockSpec((None, nu, DIL_BLOCK, D_DIL),
                                    lambda b, l, g: (b, g, jnp.maximum(l - 1, 0), col))
    out = lambda width: pl.BlockSpec((rows, width), lambda b, l, g: (b * nb + l, 0))
    return pl.pallas_call(
        functools.partial(_dil_kernel, dil=dil),
        grid=(batch, nb, dil // nu),
        in_specs=[cur(0), prev(1), cur(1), prev(2), cur(2)] + [const(a) for a in consts],
        out_specs=[out(D_DIL), out(LANES), out(LANES)],
        out_shape=out_shape,
        scratch_shapes=[bias_scratch, pltpu.VMEM((N_HEADS_DIL // 2, rows, LANES), F32)],
        compiler_params=_cparams(("arbitrary", "arbitrary", "arbitrary")),
        name=f"dilated_d{dil}",
    )(src, src, src, src, src, *consts)


GLA_TM = 256
GLA_STEP_TILES = 2


GLA_NC = GLA_TM // GLA_CHUNK


def _gla_tile(bb, rows, q_ref, k_ref, v_ref, g_ref, w2h_ref, w2l_ref, bg_ref, nw_ref, tri_ref, o_ref, st_ref):
    g = g_ref[bb, rows, :LANES]
    z = (jnp.dot(g, w2h_ref[...], preferred_element_type=F32)
         + jnp.dot(g, w2l_ref[...], preferred_element_type=F32) + bg_ref[...])
    la = (jnp.minimum(z, 0.0) * (1.0 / GLA_TAU)
          - jnp.log2(1.0 + jnp.exp2(jnp.abs(z) * -LOG2E)) * (math.log(2.0) / GLA_TAU))
    la_h = la.astype(BF16)
    la_l = (la - la_h.astype(F32)).astype(BF16)

    tri = tri_ref[...]
    b = jnp.dot(tri, la_h, preferred_element_type=F32) + jnp.dot(tri, la_l, preferred_element_type=F32)
    b_last = jnp.concatenate(
        [jnp.broadcast_to(b[(ci + 1) * GLA_CHUNK - 1:(ci + 1) * GLA_CHUNK, :], (GLA_CHUNK, D_GLA_K))
         for ci in range(GLA_NC)], axis=0)

    q = q_ref[bb, rows].astype(F32)
    k = k_ref[bb, rows].astype(F32)
    q_t = (q * jnp.exp(b)).astype(BF16)
    k_t = (k * jnp.exp(-b)).astype(BF16)
    k_s = k * jnp.exp(b_last - b)

    causal = tri.astype(F32) > 0.0
    t_chunk = lax.broadcasted_iota(jnp.int32, (GLA_DK, GLA_TM), 1) // GLA_CHUNK
    nw = nw_ref[...]
    for h in range(N_HEADS_GLA):
        cs = slice(h * GLA_DK, (h + 1) * GLA_DK)
        vs = slice(h * GLA_DV, (h + 1) * GLA_DV)
        v_h = v_ref[bb, rows, vs]
        a = lax.dot_general(q_t[:, cs], k_t[:, cs], (((1,), (1,)), ((), ())), preferred_element_type=F32)
        o = jnp.dot(jnp.where(causal, a, 0.0).astype(BF16), v_h, preferred_element_type=F32)
        k_s_t = k_s[:, cs].T.astype(BF16)
        zero = jnp.zeros_like(k_s_t)
        kv_all = jnp.dot(jnp.concatenate([jnp.where(t_chunk == ci, k_s_t, zero) for ci in range(GLA_NC)], axis=0),
                         v_h, preferred_element_type=F32)
        decay_t = jnp.exp(b_last[:, cs].T)
        st = st_ref[bb, h]
        outs = []
        for ci in range(GLA_NC):
            rs = slice(ci * GLA_CHUNK, (ci + 1) * GLA_CHUNK)
            o_c = o[rs] + jnp.dot(q_t[rs, cs], st.astype(BF16), preferred_element_type=F32)
            decay = jnp.broadcast_to(decay_t[:, ci * GLA_CHUNK:ci * GLA_CHUNK + 1], (GLA_DK, GLA_DV))
            st = decay * st + kv_all[ci * GLA_DK:(ci + 1) * GLA_DK]
            ms = jnp.mean(o_c * o_c, axis=-1, keepdims=True)
            outs.append((o_c * lax.rsqrt(ms + EPS) * nw).astype(o_ref.dtype))
        st_ref[bb, h] = st
        o_ref[bb, rows, vs] = jnp.concatenate(outs, axis=0)


def _gla_kernel(*refs):
    *tile_refs, wout_ref, o_ref, wout_bf_ref, st_ref = refs

    @pl.when(pl.program_id(0) == 0)
    def _():
        st_ref[...] = jnp.zeros_like(st_ref)

    wout_bf_ref[...] = wout_ref[...].astype(wout_bf_ref.dtype)

    for tt in range(GLA_STEP_TILES):
        rows = slice(tt * GLA_TM, (tt + 1) * GLA_TM)
        for bb in range(st_ref.shape[0]):
            _gla_tile(bb, rows, *tile_refs, o_ref, st_ref)


def _gla(proj, w2_hi, w2_lo, b_gate, norm_w, w_out, batch, seq):
    step_rows = GLA_TM * GLA_STEP_TILES
    steps = seq // step_rows
    wout_rows = w_out.shape[0] // steps
    pos = np.arange(GLA_TM)
    tri = jnp.asarray((pos[:, None] // GLA_CHUNK == pos[None, :] // GLA_CHUNK) & (pos[None, :] <= pos[:, None]),
                      BF16)
    proj3 = proj.reshape(batch, seq, REST_W)
    rows = lambda width, col: pl.BlockSpec((batch, step_rows, width), lambda i: (0, i, col // width))
    const = lambda shape: pl.BlockSpec(shape, lambda i: (0, 0))
    wout_spec = pl.BlockSpec((wout_rows, w_out.shape[1]), lambda i: (i, 0))
    out, w_out_bf = pl.pallas_call(
        _gla_kernel,
        grid=(steps,),
        in_specs=[
            rows(D_GLA_K, COL_GQ), rows(D_GLA_K, COL_GK), rows(D_GLA_V, COL_GV), rows(GLR_PAD, COL_GLR),
            const((LANES, D_GLA_K)), const((LANES, D_GLA_K)), const((1, D_GLA_K)), const((1, GLA_DV)),
            const((GLA_TM, GLA_TM)), wout_spec,
        ],
        out_specs=[rows(D_GLA_V, 0), wout_spec],
        out_shape=[jax.ShapeDtypeStruct((batch, seq, D_GLA_V), BF16), jax.ShapeDtypeStruct(w_out.shape, BF16)],
        scratch_shapes=[pltpu.VMEM((batch, N_HEADS_GLA, GLA_DK, GLA_DV), F32)],
        compiler_params=_cparams(("arbitrary",)),
        name="gla",
    )(proj3, proj3, proj3, proj3, w2_hi, w2_lo, b_gate, norm_w, tri, w_out)
    return out.reshape(batch * seq, D_GLA_V), w_out_bf


MEM_TM = 1024
MEM_UNIT = 128


def _mem_kernel(q_ref, mem_ref, mnw_ref, wkv_ref, o_ref, den_ref, kv_ref):
    @pl.when(pl.program_id(1) == 0)
    def _():
        m = mem_ref[0]
        ms = jnp.mean(m * m, axis=-1, keepdims=True)
        mn = (m * lax.rsqrt(ms + EPS) * mnw_ref[...]).astype(BF16)
        kv_ref[...] = jnp.dot(mn, wkv_ref[...].astype(BF16), preferred_element_type=F32).astype(BF16)

    lane = lax.broadcasted_iota(jnp.int32, (MEM_UNIT, LANES), 1)
    lo = lane < HEAD_DIM
    for u in range(MEM_TM // MEM_UNIT):
        rows = slice(u * MEM_UNIT, (u + 1) * MEM_UNIT)
        den_c = jnp.ones((MEM_UNIT, LANES), F32)
        for j in range(N_HEADS_MEM // 2):
            sl = slice(j * LANES, (j + 1) * LANES)
            r_e, r_o, _, _ = _pair_attention(q_ref[rows, sl], kv_ref[:, sl],
                                             kv_ref[:, D_MEM + j * LANES:D_MEM + (j + 1) * LANES], None, None)
            o_ref[rows, sl] = jnp.where(lo, r_e, r_o).astype(o_ref.dtype)
            den_c = jnp.where(lane == _stat_lane(2 * j), r_e, jnp.where(lane == _stat_lane(2 * j + 1), r_o, den_c))
        den_ref[rows, :] = den_c


def _mem_attn(proj, mem, mem_norm_w, w_kv, batch, seq):
    steps = seq // MEM_TM
    rows = lambda width, col=0: pl.BlockSpec((MEM_TM, width), lambda b, i: (b * steps + i, col // width))
    return pl.pallas_call(
        _mem_kernel,
        grid=(batch, steps),
        in_specs=[
            rows(D_MEM, COL_MQ),
            pl.BlockSpec((1, MEM_LEN, D_MODEL), lambda b, i: (b, 0, 0)),
            pl.BlockSpec((1, D_MODEL), lambda b, i: (0, 0)),
            pl.BlockSpec((D_MODEL, 2 * D_MEM), lambda b, i: (0, 0)),
        ],
        out_specs=[rows(D_MEM), rows(LANES)],
        out_shape=[jax.ShapeDtypeStruct((batch * seq, D_MEM), BF16),
                   jax.ShapeDtypeStruct((batch * seq, LANES), F32)],
        scratch_shapes=[pltpu.VMEM((MEM_LEN, 2 * D_MEM), BF16)],
        compiler_params=_cparams(("parallel", "arbitrary")),
        name="mem_attn",
    )(proj, mem, mem_norm_w, w_kv)


OUT_TM = 512


def _silu(g):
    h = g * 0.5
    return h * jnp.tanh(h) + h


OUT_GW = 256
OUT_CW = 512


def _out_tile(read_ref, write_ref, n1_ref, n2_ref, n3_ref, m1_ref, m2_ref, m3_ref, d1_ref, d2_ref, d3_ref,
              og_ref, nm_ref, dm_ref, ga_ref, gg_ref, gm_ref, ex_ref, exm_ref, w_ref, x_ref, nw_ref, out_ref):
    m1, m2, m3 = m1_ref[...], m2_ref[...], m3_ref[...]
    mx = jnp.maximum(jnp.maximum(m1, m2), m3)
    t1, t2, t3 = jnp.exp2(m1 - mx), jnp.exp2(m2 - mx), jnp.exp2(m3 - mx)
    inv = 1.0 / (t1 * d1_ref[...] + t2 * d2_ref[...] + t3 * d3_ref[...])
    wts = [(t * inv).astype(BF16) for t in (t1, t2, t3)]
    inv_m = (1.0 / dm_ref[...]).astype(BF16)

    def gate_chunk(lo):
        cols = slice(lo, lo + OUT_GW)
        if lo < D_DIL:
            a = jnp.zeros((OUT_TM, OUT_GW), F32)
            for wt, n_ref in zip(wts, (n1_ref, n2_ref, n3_ref)):
                a = a + jnp.dot(wt, ex_ref[:, cols], preferred_element_type=F32) * n_ref[:, cols].astype(F32)
            g = ga_ref[:, cols]
        elif lo < D_DIL + D_GLA_V:
            lc = slice(lo - D_DIL, lo - D_DIL + OUT_GW)
            a, g = og_ref[:, lc], gg_ref[:, lc]
        else:
            a = jnp.dot(inv_m, exm_ref[...], preferred_element_type=F32) * nm_ref[...].astype(F32)
            g = gm_ref[...]
        write_ref[:, cols] = a.astype(BF16) * _silu(g)

    ssq = jnp.zeros((OUT_TM, 1), F32)
    for c in range(D_MODEL // OUT_CW):
        for lo in range(c * OUT_CW, (c + 1) * OUT_CW, OUT_GW):
            gate_chunk(lo)
        cols = slice(c * OUT_CW, (c + 1) * OUT_CW)
        y = jnp.dot(read_ref[...], w_ref[:, cols], preferred_element_type=F32)
        ssq = ssq + jnp.sum(y * y, axis=-1, keepdims=True)
        out_ref[:, cols] = y
    scale = lax.rsqrt(ssq * (1.0 / D_MODEL) + EPS)
    out_ref[...] = x_ref[...] + out_ref[...] * scale * nw_ref[...]


def _out_kernel(*refs):
    mix_even, mix_odd = refs[-2:]
    s = pl.program_id(0)

    @pl.when(s == 0)
    def _():
        mix_odd[...] = jnp.zeros_like(mix_odd)

    @pl.when(s % 2 == 0)
    def _():
        _out_tile(mix_odd, mix_even, *refs[:-2])

    @pl.when(s % 2 == 1)
    def _():
        _out_tile(mix_even, mix_odd, *refs[:-2])


def _out_proj(pats, o_gla, mem_pair, proj, expand, expand_mem, w_out, x2d, norm_w):
    m = x2d.shape[0]
    n_tiles = m // OUT_TM
    gate_rows = lambda width, col=0: pl.BlockSpec(
        (OUT_TM, width), lambda s: (jnp.minimum(s, n_tiles - 1), col // width))
    proj_rows = lambda width: pl.BlockSpec((OUT_TM, width), lambda s: (jnp.maximum(s - 1, 0), 0))
    const = lambda shape: pl.BlockSpec(shape, lambda s: (0, 0), pipeline_mode=pl.Buffered(1))
    nums, mxs, dens = zip(*pats)
    return pl.pallas_call(
        _out_kernel,
        grid=(n_tiles + 1,),
        in_specs=[
            *([gate_rows(D_DIL)] * 3), *([gate_rows(LANES)] * 6),
            gate_rows(D_GLA_V), gate_rows(D_MEM), gate_rows(LANES),
            gate_rows(D_DIL, COL_GATE_A), gate_rows(D_GLA_V, COL_GATE_G), gate_rows(D_MEM, COL_GATE_M),
            const((LANES, D_DIL)), const((LANES, D_MEM)), const((D_MIX, D_MODEL)),
            proj_rows(D_MODEL), const((1, D_MODEL)),
        ],
        out_specs=proj_rows(D_MODEL),
        out_shape=jax.ShapeDtypeStruct((m, D_MODEL), x2d.dtype),
        scratch_shapes=[pltpu.VMEM((OUT_TM, D_MIX), BF16), pltpu.VMEM((OUT_TM, D_MIX), BF16)],
        compiler_params=_cparams(("arbitrary",)),
        name="out_proj",
    )(*nums, *mxs, *dens, o_gla, *mem_pair, proj, proj, proj, expand, expand_mem, w_out, x2d, norm_w)


D_IN_PROJ = 3 * D_DIL + 2 * D_GLA_K + D_GLA_V + GLA_GATE_RANK + D_MEM + D_MIX
SRC_GLR = 3 * D_DIL + 2 * D_GLA_K + D_GLA_V
SRC_MQ = SRC_GLR + GLA_GATE_RANK
SRC_GATE = SRC_MQ + D_MEM
REST_SEGMENTS = (
    (QKV_W, 2 * D_GLA_K + D_GLA_V, COL_GQ),
    (SRC_GATE + D_DIL, D_GLA_V, COL_GATE_G),
    (SRC_MQ, D_MEM, COL_MQ),
    (SRC_GATE + D_DIL + D_GLA_V, D_MEM, COL_GATE_M),
    (SRC_GATE, D_DIL, COL_GATE_A),
)
WPREP_TK = 512


def _wprep_kernel(wt_ref, qkv_ref, rest_ref):
    def put(o_ref, src, width, dst):
        for c in range(width // LANES):
            lo = c * LANES
            o_ref[:, dst + lo:dst + lo + LANES] = wt_ref[src + lo:src + lo + LANES, :].T.astype(o_ref.dtype)

    put(qkv_ref, 0, QKV_W, 0)
    for src, width, dst in REST_SEGMENTS:
        put(rest_ref, src, width, dst)
    glr = wt_ref[SRC_GLR:SRC_GLR + LANES, :].T
    keep = lax.broadcasted_iota(jnp.int32, glr.shape, 1) < GLA_GATE_RANK
    rest_ref[:, COL_GLR:COL_GLR + LANES] = jnp.where(keep, glr, 0.0).astype(rest_ref.dtype)
    rest_ref[:, COL_GLR + LANES:COL_GLR + GLR_PAD] = jnp.zeros((WPREP_TK, GLR_PAD - LANES), rest_ref.dtype)


def _regroup_in_weights(w_in):
    w_qkv, w_rest = pl.pallas_call(
        _wprep_kernel,
        grid=(D_MODEL // WPREP_TK,),
        in_specs=[pl.BlockSpec((D_IN_PROJ, WPREP_TK), lambda i: (0, i))],
        out_specs=[pl.BlockSpec((WPREP_TK, QKV_W), lambda i: (i, 0)),
                   pl.BlockSpec((WPREP_TK, REST_W), lambda i: (i, 0))],
        out_shape=[jax.ShapeDtypeStruct((D_MODEL, QKV_W), BF16), jax.ShapeDtypeStruct((D_MODEL, REST_W), BF16)],
        compiler_params=_cparams(("parallel",)),
        name="weight_regroup",
    )(w_in.T)
    scale_qkv = np.ones((1, QKV_W), np.float32)
    scale_qkv[:, :D_DIL] = HEAD_DIM ** -0.5 * LOG2E
    scale_rest = np.ones((1, REST_W), np.float32)
    scale_rest[:, COL_GQ:COL_GQ + D_GLA_K] = GLA_DK ** -0.5
    scale_rest[:, COL_MQ:COL_MQ + D_MEM] = HEAD_DIM ** -0.5 * LOG2E
    return w_qkv, w_rest, jnp.asarray(scale_qkv), jnp.asarray(scale_rest)


def _split_hi_lo(w):
    hi = w.astype(BF16)
    return hi, (w - hi.astype(F32)).astype(BF16)


def kernel(x, mem, norm_pre_w, w_in, rel_bias, w_gla_gate2, b_gla_gate, gla_norm_w, mem_norm_w, w_mem_kv,
           w_out, norm_post_w):
    batch, seq, _ = x.shape
    depth = w_in.shape[0]
    expand_np = np.zeros((LANES, D_DIL), np.float32)
    for h in range(N_HEADS_DIL):
        expand_np[_stat_lane(h), h * HEAD_DIM:(h + 1) * HEAD_DIM] = 1.0
    expand = jnp.asarray(expand_np, BF16)
    expand_mem = jnp.asarray(expand_np[:, :D_MEM], BF16)
    rb_t = jnp.zeros((16, LANES), F32).at[:N_HEADS_DIL, :REL_BUCKETS].set(rel_bias.astype(F32).T)
    rb_hi, rb_lo = _split_hi_lo(rb_t)
    for l in range(depth):
        x2d = x.reshape(batch * seq, D_MODEL)
        w_qkv, w_rest, scale_qkv, scale_rest = _regroup_in_weights(w_in[l])
        h, qkv1, qkv4, qkv16 = _qkv_proj(x2d, norm_pre_w[l][None], w_qkv, scale_qkv, batch, seq)
        rest = _rest_proj(h, w_rest, scale_rest)

        srcs = (qkv1.reshape(batch, 1, seq, QKV_W), qkv4, qkv16)
        pats = [_dilated(src, rb_hi, rb_lo, batch, seq, window, dil)
                for (window, dil), src in zip(DIL_PATTERNS, srcs)]

        w2 = jnp.zeros((LANES, D_GLA_K), F32).at[:GLA_GATE_RANK].set(w_gla_gate2[l])
        w2_hi, w2_lo = _split_hi_lo(w2)
        o_gla, w_out_bf = _gla(rest, w2_hi, w2_lo, b_gla_gate[l][None], gla_norm_w[l][None], w_out[l],
                               batch, seq)

        mem_pair = _mem_attn(rest, mem, mem_norm_w[l][None], w_mem_kv[l], batch, seq)

        out = _out_proj(pats, o_gla, mem_pair, rest, expand, expand_mem, w_out_bf, x2d, norm_post_w[l][None])
        x = out.reshape(batch, seq, D_MODEL)
    return x
```

```python
import functools
import math

import numpy as np
import jax
import jax.numpy as jnp
from jax import lax
from jax.experimental import pallas as pl
from jax.experimental.pallas import tpu as pltpu

F32 = jnp.float32
BF16 = jnp.bfloat16

D_MODEL = 2048
HEAD_DIM = 64
N_HEADS_DIL = 12
DIL_PATTERNS = ((128, 1), (512, 4), (2048, 16))
DIL_BLOCK = 128
N_HEADS_GLA = 4
GLA_DK = 128
GLA_DV = 256
GLA_GATE_RANK = 16
GLA_TAU = 16.0
GLA_CHUNK = 64
N_HEADS_MEM = 4
MEM_LEN = 256
REL_BUCKETS = 32
REL_MAX_DIST = 2048
EPS = 1e-6
NEG_INF = -1e30

D_DIL = N_HEADS_DIL * HEAD_DIM
D_GLA_K = N_HEADS_GLA * GLA_DK
D_GLA_V = N_HEADS_GLA * GLA_DV
D_MEM = N_HEADS_MEM * HEAD_DIM
D_MIX = D_DIL + D_GLA_V + D_MEM

LANES = 128
GLR_PAD = 256

QKV_W = 3 * D_DIL
COL_GQ = 0
COL_GK = COL_GQ + D_GLA_K
COL_GV = COL_GK + D_GLA_K
COL_GATE_G = COL_GV + D_GLA_V
COL_MQ = COL_GATE_G + D_GLA_V
COL_GATE_M = COL_MQ + D_MEM
COL_GLR = COL_GATE_M + D_MEM
COL_GATE_A = COL_GLR + GLR_PAD
REST_W = COL_GATE_A + D_DIL

VMEM_LIMIT = 56 * 1024 * 1024


def _cparams(sem):
    return pltpu.CompilerParams(dimension_semantics=sem, vmem_limit_bytes=VMEM_LIMIT)


QKV_TM = 512
N_SLABS = D_DIL // LANES
REGROUP_DILS = (4, 16)


def _qkv_proj_kernel(x_ref, nw_ref, w_ref, sc_ref, h_ref, o_ref, x4_ref, x16_ref, slab_ref):
    x = x_ref[...]
    ms = jnp.mean(x * x, axis=-1, keepdims=True)
    h_ref[...] = (x * lax.rsqrt(ms + EPS) * nw_ref[...]).astype(BF16)
    for t in range(QKV_W // D_DIL):
        cols = slice(t * D_DIL, (t + 1) * D_DIL)
        res = jnp.dot(h_ref[...], w_ref[:, cols], preferred_element_type=F32) * sc_ref[:, cols]
        o_ref[:, cols] = res.astype(o_ref.dtype)
        for s in range(N_SLABS):
            slab_ref[t, s] = res[:, s * LANES:(s + 1) * LANES]
        for out_ref, dil in zip((x4_ref, x16_ref), REGROUP_DILS):
            for r in range(dil):
                for s in range(N_SLABS):
                    lo = t * D_DIL + s * LANES
                    out_ref[0, r, :, lo:lo + LANES] = (
                        slab_ref[t, s, pl.ds(r, QKV_TM // dil, stride=dil), :].astype(out_ref.dtype))


def _qkv_proj(x2d, norm_w, w_qkv, scale, batch, seq):
    m = x2d.shape[0]
    tiles_per_seq = seq // QKV_TM
    const = lambda shape: pl.BlockSpec(shape, lambda i: (0, 0), pipeline_mode=pl.Buffered(1))
    grouped = lambda dil: pl.BlockSpec((1, dil, QKV_TM // dil, QKV_W),
                                       lambda i: (i // tiles_per_seq, 0, i % tiles_per_seq, 0))
    return pl.pallas_call(
        _qkv_proj_kernel,
        grid=(m // QKV_TM,),
        in_specs=[
            pl.BlockSpec((QKV_TM, D_MODEL), lambda i: (i, 0)),
            const((1, D_MODEL)), const((D_MODEL, QKV_W)), const((1, QKV_W)),
        ],
        out_specs=[pl.BlockSpec((QKV_TM, D_MODEL), lambda i: (i, 0)),
                   pl.BlockSpec((QKV_TM, QKV_W), lambda i: (i, 0))] + [grouped(d) for d in REGROUP_DILS],
        out_shape=[jax.ShapeDtypeStruct((m, D_MODEL), BF16), jax.ShapeDtypeStruct((m, QKV_W), BF16)]
        + [jax.ShapeDtypeStruct((batch, d, seq // d, QKV_W), BF16) for d in REGROUP_DILS],
        scratch_shapes=[pltpu.VMEM((QKV_W // D_DIL, N_SLABS, QKV_TM, LANES), F32)],
        compiler_params=_cparams(("parallel",)),
        name="qkv_proj",
    )(x2d, norm_w, w_qkv, scale)


REST_TM = 1024
REST_TN = 2304


def _rest_proj_kernel(h_ref, w_ref, sc_ref, o_ref):
    acc = jnp.dot(h_ref[...], w_ref[...], preferred_element_type=F32)
    o_ref[...] = (acc * sc_ref[...]).astype(o_ref.dtype)


def _rest_proj(h, w_rest, scale):
    m = h.shape[0]
    return pl.pallas_call(
        _rest_proj_kernel,
        grid=(m // REST_TM, REST_W // REST_TN),
        in_specs=[
            pl.BlockSpec((REST_TM, D_MODEL), lambda i, j: (i, 0)),
            pl.BlockSpec((D_MODEL, REST_TN), lambda i, j: (0, j)),
            pl.BlockSpec((1, REST_TN), lambda i, j: (0, j)),
        ],
        out_specs=pl.BlockSpec((REST_TM, REST_TN), lambda i, j: (i, j)),
        out_shape=jax.ShapeDtypeStruct((m, REST_W), BF16),
        compiler_params=_cparams(("parallel", "arbitrary")),
        name="rest_proj",
    )(h, w_rest, scale)


LOG2E = math.log2(math.e)


def _pair_attention(q_pair, k_pair, v_pair, bias_e, bias_o):
    rows = q_pair.shape[0]
    lo_q = lax.broadcasted_iota(jnp.int32, q_pair.shape, 1) < HEAD_DIM
    zero = jnp.zeros_like(q_pair)
    q_both = jnp.concatenate([jnp.where(lo_q, q_pair, zero), jnp.where(lo_q, zero, q_pair)], axis=0)
    s = lax.dot_general(q_both, k_pair, (((1,), (1,)), ((), ())), preferred_element_type=F32)
    s_e, s_o = s[:rows], s[rows:]
    if bias_e is not None:
        s_e = s_e + bias_e
        s_o = s_o + bias_o
    mx_e = jnp.max(s_e, axis=-1, keepdims=True)
    mx_o = jnp.max(s_o, axis=-1, keepdims=True)
    p_e = jnp.exp2(s_e - mx_e).astype(BF16)
    p_o = jnp.exp2(s_o - mx_o).astype(BF16)
    lo_v = lax.broadcasted_iota(jnp.int32, v_pair.shape, 1) < HEAD_DIM
    one = jnp.ones_like(v_pair)
    r_e = jnp.dot(p_e, jnp.where(lo_v, v_pair, one), preferred_element_type=F32)
    r_o = jnp.dot(p_o, jnp.where(lo_v, one, v_pair), preferred_element_type=F32)
    return r_e, r_o, mx_e, mx_o


def _stat_lane(head):
    return head + HEAD_DIM if head % 2 == 0 else head


STAT_DEN_SHIFT = 16


DIL_UNITS = {1: 8, 4: 4, 16: 8}


def _dil_init_bias(rbh_ref, rbl_ref, sel_ref, mrow_ref, bias_ref):
    f = (jnp.dot(rbh_ref[...], sel_ref[...], preferred_element_type=F32)
         + jnp.dot(rbl_ref[...], sel_ref[...], preferred_element_type=F32)) * LOG2E + mrow_ref[...]
    col = lax.broadcasted_iota(jnp.int32, (DIL_BLOCK, 2 * DIL_BLOCK), 1)
    for h in range(N_HEADS_DIL):
        row = jnp.broadcast_to(f[h:h + 1, :], (DIL_BLOCK, 2 * DIL_BLOCK))
        tab = pltpu.roll(row, 0, 1, stride=1, stride_axis=0)
        bias_ref[0, h] = tab
        bias_ref[1, h] = jnp.where(col >= DIL_BLOCK, tab, NEG_INF)


def _dil_unit(q, kp, kc, vp, vc, first, bias_ref, put_num):
    lane = lax.broadcasted_iota(jnp.int32, (DIL_BLOCK, LANES), 1)
    lo = lane < HEAD_DIM
    st_c = jnp.zeros((DIL_BLOCK, LANES), F32)
    for j in range(N_HEADS_DIL // 2):
        sl = slice(j * LANES, (j + 1) * LANES)
        k_pair = jnp.concatenate([kp(sl), kc(sl)], axis=0)
        v_pair = jnp.concatenate([vp(sl), vc(sl)], axis=0)
        r_e, r_o, mx_e, mx_o = _pair_attention(q(sl), k_pair, v_pair,
                                               bias_ref[first, 2 * j], bias_ref[first, 2 * j + 1])
        put_num(j, jnp.where(lo, r_e, r_o))
        l_e, l_o = _stat_lane(2 * j), _stat_lane(2 * j + 1)
        st_c = jnp.where(lane == l_e, mx_e, jnp.where(lane == l_e + STAT_DEN_SHIFT, r_e, st_c))
        st_c = jnp.where(lane == l_o, mx_o, jnp.where(lane == l_o + STAT_DEN_SHIFT, r_o, st_c))
    return st_c


def _dil1_kernel(q_ref, kp_ref, kc_ref, vp_ref, vc_ref, rbh_ref, rbl_ref, sel_ref, mrow_ref,
                 o_ref, st_ref, bias_ref):
    @pl.when((pl.program_id(0) == 0) & (pl.program_id(1) == 0))
    def _():
        _dil_init_bias(rbh_ref, rbl_ref, sel_ref, mrow_ref, bias_ref)

    for u in range(DIL_UNITS[1]):
        rows = slice(u * DIL_BLOCK, (u + 1) * DIL_BLOCK)
        prev = slice((u - 1) * DIL_BLOCK, u * DIL_BLOCK)
        if u == 0:
            first = (pl.program_id(1) == 0).astype(jnp.int32)
            kp, vp = (lambda sl: kp_ref[:, sl]), (lambda sl: vp_ref[:, sl])
        else:
            first = 0
            kp, vp = (lambda sl, p=prev: kc_ref[p, sl]), (lambda sl, p=prev: vc_ref[p, sl])

        def put_num(j, num, rows=rows):
            o_ref[rows, j * LANES:(j + 1) * LANES] = num.astype(o_ref.dtype)

        st_ref[rows, :] = _dil_unit(lambda sl, r=rows: q_ref[r, sl], kp, lambda sl, r=rows: kc_ref[r, sl],
                                    vp, lambda sl, r=rows: vc_ref[r, sl], first, bias_ref, put_num)


def _dil_kernel(q_ref, kp_ref, kc_ref, vp_ref, vc_ref, rbh_ref, rbl_ref, sel_ref, mrow_ref,
                o_ref, st_ref, bias_ref, acc_ref, *, dil):
    b, lb, g = pl.program_id(0), pl.program_id(1), pl.program_id(2)
    nr = DIL_UNITS[dil]

    @pl.when((b == 0) & (lb == 0) & (g == 0))
    def _():
        _dil_init_bias(rbh_ref, rbl_ref, sel_ref, mrow_ref, bias_ref)

    first = (lb == 0).astype(jnp.int32)
    for u in range(nr):
        rows = pl.ds(g * nr + u, DIL_BLOCK, stride=dil)

        def put_num(j, num, rows=rows):
            acc_ref[j, rows, :] = num

        st_ref[rows, :] = _dil_unit(lambda sl, u=u: q_ref[u, :, sl], lambda sl, u=u: kp_ref[u, :, sl],
                                    lambda sl, u=u: kc_ref[u, :, sl], lambda sl, u=u: vp_ref[u, :, sl],
                                    lambda sl, u=u: vc_ref[u, :, sl], first, bias_ref, put_num)

    @pl.when(g == dil // nr - 1)
    def _():
        for j in range(N_HEADS_DIL // 2):
            o_ref[:, j * LANES:(j + 1) * LANES] = acc_ref[j].astype(o_ref.dtype)


def _t5_bucket_np(dist):
    max_exact = REL_BUCKETS // 2
    n = np.maximum(dist, 1).astype(np.float32)
    large = max_exact + (np.log(n / np.float32(max_exact)) / np.float32(math.log(REL_MAX_DIST / max_exact))
                         * np.float32(REL_BUCKETS - max_exact)).astype(np.int32)
    large = np.minimum(large, REL_BUCKETS - 1)
    return np.where(dist < max_exact, dist, large)


def _bias_selector(window, dil):
    back = window // dil
    x = np.arange(2 * DIL_BLOCK)
    rel = DIL_BLOCK - x
    valid = (rel >= 0) & (rel <= back)
    bucket = _t5_bucket_np(np.clip(rel, 0, back) * dil)
    sel = np.zeros((LANES, 2 * DIL_BLOCK), np.float32)
    sel[bucket[valid], x[valid]] = 1.0
    mrow = np.where(valid, 0.0, NEG_INF).astype(np.float32)[None]
    return jnp.asarray(sel, BF16), jnp.asarray(mrow)


def _dilated(src, rb_hi, rb_lo, batch, seq, window, dil):
    sub_len = seq // dil
    nb = sub_len // DIL_BLOCK
    sel, mrow = _bias_selector(window, dil)
    nu = DIL_UNITS[dil]
    consts = (rb_hi, rb_lo, sel, mrow)
    const = lambda a: pl.BlockSpec(a.shape, lambda *_: (0, 0))
    stat_shape = jax.ShapeDtypeStruct((batch * seq, LANES), F32)
    out_shape = [jax.ShapeDtypeStruct((batch * seq, D_DIL), BF16), stat_shape]
    bias_scratch = pltpu.VMEM((2, N_HEADS_DIL, DIL_BLOCK, 2 * DIL_BLOCK), F32)

    if dil == 1:
        rows = nu * DIL_BLOCK
        steps = nb // nu
        cur = lambda col: pl.BlockSpec((None, None, rows, D_DIL), lambda b, l: (b, 0, l, col))
        prev = lambda col: pl.BlockSpec((None, None, DIL_BLOCK, D_DIL),
                                        lambda b, l: (b, 0, jnp.maximum(nu * l - 1, 0), col))
        out = lambda width: pl.BlockSpec((rows, width), lambda b, l: (b * steps + l, 0))
        return pl.pallas_call(
            _dil1_kernel,
            grid=(batch, steps),
            in_specs=[cur(0), prev(1), cur(1), prev(2), cur(2)] + [const(a) for a in consts],
            out_specs=[out(D_DIL), out(LANES)],
            out_shape=out_shape,
            scratch_shapes=[bias_scratch],
            compiler_params=_cparams(("arbitrary", "arbitrary")),
            name="dilated_d1",
        )(src, src, src, src, src, *consts)

    rows = DIL_BLOCK * dil
    cur = lambda col: pl.BlockSpec((None, nu, DIL_BLOCK, D_DIL), lambda b, l, g: (b, g, l, col))
    prev = lambda col: pl.BlockSpec((None, nu, DIL_BLOCK, D_DIL),
                                    lambda b, l, g: (b, g, jnp.maximum(l - 1, 0), col))
    out = lambda width: pl.BlockSpec((rows, width), lambda b, l, g: (b * nb + l, 0))
    return pl.pallas_call(
        functools.partial(_dil_kernel, dil=dil),
        grid=(batch, nb, dil // nu),
        in_specs=[cur(0), prev(1), cur(1), prev(2), cur(2)] + [const(a) for a in consts],
        out_specs=[out(D_DIL), out(LANES)],
        out_shape=out_shape,
        scratch_shapes=[bias_scratch, pltpu.VMEM((N_HEADS_DIL // 2, rows, LANES), F32)],
        compiler_params=_cparams(("arbitrary", "arbitrary", "arbitrary")),
        name=f"dilated_d{dil}",
    )(src, src, src, src, src, *consts)


GLA_TM = 256
GLA_STEP_TILES = 2


GLA_NC = GLA_TM // GLA_CHUNK


def _gla_tile(bb, rows, q_ref, k_ref, v_ref, g_ref, w2h_ref, w2l_ref, bg_ref, nw_ref, tri_ref, o_ref, st_ref):
    g = g_ref[bb, rows, :LANES]
    z = (jnp.dot(g, w2h_ref[...], preferred_element_type=F32)
         + jnp.dot(g, w2l_ref[...], preferred_element_type=F32) + bg_ref[...])
    la = (jnp.minimum(z, 0.0) * (LOG2E / GLA_TAU)
          - jnp.log2(1.0 + jnp.exp2(jnp.abs(z) * -LOG2E)) * (1.0 / GLA_TAU))
    la_h = la.astype(BF16)
    la_l = (la - la_h.astype(F32)).astype(BF16)

    tri = tri_ref[...]
    b = jnp.dot(tri, la_h, preferred_element_type=F32) + jnp.dot(tri, la_l, preferred_element_type=F32)
    b_last = jnp.concatenate(
        [jnp.broadcast_to(b[(ci + 1) * GLA_CHUNK - 1:(ci + 1) * GLA_CHUNK, :], (GLA_CHUNK, D_GLA_K))
         for ci in range(GLA_NC)], axis=0)

    q = q_ref[bb, rows].astype(F32)
    k = k_ref[bb, rows].astype(F32)
    q_t = (q * jnp.exp2(b)).astype(BF16)
    k_t = (k * jnp.exp2(-b)).astype(BF16)
    k_s = k * jnp.exp2(b_last - b)

    causal = tri.astype(F32) > 0.0
    t_chunk = lax.broadcasted_iota(jnp.int32, (GLA_DK, GLA_TM), 1) // GLA_CHUNK
    nw = nw_ref[...]
    for h in range(N_HEADS_GLA):
        cs = slice(h * GLA_DK, (h + 1) * GLA_DK)
        vs = slice(h * GLA_DV, (h + 1) * GLA_DV)
        v_h = v_ref[bb, rows, vs]
        a = lax.dot_general(q_t[:, cs], k_t[:, cs], (((1,), (1,)), ((), ())), preferred_element_type=F32)
        o = jnp.dot(jnp.where(causal, a, 0.0).astype(BF16), v_h, preferred_element_type=F32)
        k_s_t = k_s[:, cs].T.astype(BF16)
        zero = jnp.zeros_like(k_s_t)
        kv_all = jnp.dot(jnp.concatenate([jnp.where(t_chunk == ci, k_s_t, zero) for ci in range(GLA_NC)], axis=0),
                         v_h, preferred_element_type=F32)
        decay_t = jnp.exp2(b_last[:, cs].T)
        st = st_ref[bb, h]
        outs = []
        for ci in range(GLA_NC):
            rs = slice(ci * GLA_CHUNK, (ci + 1) * GLA_CHUNK)
            o_c = o[rs] + jnp.dot(q_t[rs, cs], st.astype(BF16), preferred_element_type=F32)
            decay = jnp.broadcast_to(decay_t[:, ci * GLA_CHUNK:ci * GLA_CHUNK + 1], (GLA_DK, GLA_DV))
            st = decay * st + kv_all[ci * GLA_DK:(ci + 1) * GLA_DK]
            ms = jnp.mean(o_c * o_c, axis=-1, keepdims=True)
            outs.append((o_c * lax.rsqrt(ms + EPS) * nw).astype(o_ref.dtype))
        st_ref[bb, h] = st
        o_ref[bb, rows, vs] = jnp.concatenate(outs, axis=0)


def _gla_kernel(*refs):
    *tile_refs, wout_ref, o_ref, wout_bf_ref, st_ref = refs

    @pl.when(pl.program_id(0) == 0)
    def _():
        st_ref[...] = jnp.zeros_like(st_ref)

    wout_bf_ref[...] = wout_ref[...].astype(wout_bf_ref.dtype)

    for tt in range(GLA_STEP_TILES):
        rows = slice(tt * GLA_TM, (tt + 1) * GLA_TM)
        for bb in range(st_ref.shape[0]):
            _gla_tile(bb, rows, *tile_refs, o_ref, st_ref)


def _gla(proj, w2_hi, w2_lo, b_gate, norm_w, w_out, batch, seq):
    step_rows = GLA_TM * GLA_STEP_TILES
    steps = seq // step_rows
    wout_rows = w_out.shape[0] // steps
    pos = np.arange(GLA_TM)
    tri = jnp.asarray((pos[:, None] // GLA_CHUNK == pos[None, :] // GLA_CHUNK) & (pos[None, :] <= pos[:, None]),
                      BF16)
    proj3 = proj.reshape(batch, seq, REST_W)
    rows = lambda width, col: pl.BlockSpec((batch, step_rows, width), lambda i: (0, i, col // width))
    const = lambda shape: pl.BlockSpec(shape, lambda i: (0, 0))
    wout_spec = pl.BlockSpec((wout_rows, w_out.shape[1]), lambda i: (i, 0))
    out, w_out_bf = pl.pallas_call(
        _gla_kernel,
        grid=(steps,),
        in_specs=[
            rows(D_GLA_K, COL_GQ), rows(D_GLA_K, COL_GK), rows(D_GLA_V, COL_GV), rows(GLR_PAD, COL_GLR),
            const((LANES, D_GLA_K)), const((LANES, D_GLA_K)), const((1, D_GLA_K)), const((1, GLA_DV)),
            const((GLA_TM, GLA_TM)), wout_spec,
        ],
        out_specs=[rows(D_GLA_V, 0), wout_spec],
        out_shape=[jax.ShapeDtypeStruct((batch, seq, D_GLA_V), BF16), jax.ShapeDtypeStruct(w_out.shape, BF16)],
        scratch_shapes=[pltpu.VMEM((batch, N_HEADS_GLA, GLA_DK, GLA_DV), F32)],
        compiler_params=_cparams(("arbitrary",)),
        name="gla",
    )(proj3, proj3, proj3, proj3, w2_hi, w2_lo, b_gate, norm_w, tri, w_out)
    return out.reshape(batch * seq, D_GLA_V), w_out_bf


MEM_TM = 1024
MEM_UNIT = 128


def _mem_kernel(q_ref, mem_ref, mnw_ref, wkv_ref, o_ref, den_ref, kv_ref):
    @pl.when(pl.program_id(1) == 0)
    def _():
        m = mem_ref[0]
        ms = jnp.mean(m * m, axis=-1, keepdims=True)
        mn = (m * lax.rsqrt(ms + EPS) * mnw_ref[...]).astype(BF16)
        kv_ref[...] = jnp.dot(mn, wkv_ref[...].astype(BF16), preferred_element_type=F32).astype(BF16)

    lane = lax.broadcasted_iota(jnp.int32, (MEM_UNIT, LANES), 1)
    lo = lane < HEAD_DIM
    for u in range(MEM_TM // MEM_UNIT):
        rows = slice(u * MEM_UNIT, (u + 1) * MEM_UNIT)
        den_c = jnp.ones((MEM_UNIT, LANES), F32)
        for j in range(N_HEADS_MEM // 2):
            sl = slice(j * LANES, (j + 1) * LANES)
            r_e, r_o, _, _ = _pair_attention(q_ref[rows, sl], kv_ref[:, sl],
                                             kv_ref[:, D_MEM + j * LANES:D_MEM + (j + 1) * LANES], None, None)
            o_ref[rows, sl] = jnp.where(lo, r_e, r_o).astype(o_ref.dtype)
            den_c = jnp.where(lane == _stat_lane(2 * j), r_e, jnp.where(lane == _stat_lane(2 * j + 1), r_o, den_c))
        den_ref[rows, :] = den_c


def _mem_attn(proj, mem, mem_norm_w, w_kv, batch, seq):
    steps = seq // MEM_TM
    rows = lambda width, col=0: pl.BlockSpec((MEM_TM, width), lambda b, i: (b * steps + i, col // width))
    return pl.pallas_call(
        _mem_kernel,
        grid=(batch, steps),
        in_specs=[
            rows(D_MEM, COL_MQ),
            pl.BlockSpec((1, MEM_LEN, D_MODEL), lambda b, i: (b, 0, 0)),
            pl.BlockSpec((1, D_MODEL), lambda b, i: (0, 0)),
            pl.BlockSpec((D_MODEL, 2 * D_MEM), lambda b, i: (0, 0)),
        ],
        out_specs=[rows(D_MEM), rows(LANES)],
        out_shape=[jax.ShapeDtypeStruct((batch * seq, D_MEM), BF16),
                   jax.ShapeDtypeStruct((batch * seq, LANES), F32)],
        scratch_shapes=[pltpu.VMEM((MEM_LEN, 2 * D_MEM), BF16)],
        compiler_params=_cparams(("parallel", "arbitrary")),
        name="mem_attn",
    )(proj, mem, mem_norm_w, w_kv)


OUT_TM = 512


def _silu(g):
    h = g * 0.5
    return h * jnp.tanh(h) + h


OUT_GW = 256
OUT_CW = 512


def _out_tile(read_ref, write_ref, n1_ref, n2_ref, n3_ref, s1_ref, s2_ref, s3_ref,
              og_ref, nm_ref, dm_ref, ga_ref, gg_ref, gm_ref, sm_ref, ex_ref, exm_ref, w_ref, x_ref, nw_ref,
              out_ref):
    keep = sm_ref[0:1, :]
    fill = sm_ref[1:2, :]
    stats = [s_ref[...] for s_ref in (s1_ref, s2_ref, s3_ref)]
    m1, m2, m3 = [st * keep for st in stats]
    d1, d2, d3 = [pltpu.roll(st, LANES - STAT_DEN_SHIFT, 1) * keep + fill for st in stats]
    mx = jnp.maximum(jnp.maximum(m1, m2), m3)
    t1, t2, t3 = jnp.exp2(m1 - mx), jnp.exp2(m2 - mx), jnp.exp2(m3 - mx)
    inv = 1.0 / (t1 * d1 + t2 * d2 + t3 * d3)
    wts = [(t * inv).astype(BF16) for t in (t1, t2, t3)]
    inv_m = (1.0 / dm_ref[...]).astype(BF16)

    def gate_chunk(lo):
        cols = slice(lo, lo + OUT_GW)
        if lo < D_DIL:
            a = jnp.zeros((OUT_TM, OUT_GW), F32)
            for wt, n_ref in zip(wts, (n1_ref, n2_ref, n3_ref)):
                a = a + jnp.dot(wt, ex_ref[:, cols], preferred_element_type=F32) * n_ref[:, cols].astype(F32)
            g = ga_ref[:, cols]
        elif lo < D_DIL + D_GLA_V:
            lc = slice(lo - D_DIL, lo - D_DIL + OUT_GW)
            a, g = og_ref[:, lc], gg_ref[:, lc]
        else:
            a = jnp.dot(inv_m, exm_ref[...], preferred_element_type=F32) * nm_ref[...].astype(F32)
            g = gm_ref[...]
        write_ref[:, cols] = a.astype(BF16) * _silu(g)

    ssq = jnp.zeros((OUT_TM, 1), F32)
    for c in range(D_MODEL // OUT_CW):
        for lo in range(c * OUT_CW, (c + 1) * OUT_CW, OUT_GW):
            gate_chunk(lo)
        cols = slice(c * OUT_CW, (c + 1) * OUT_CW)
        y = jnp.dot(read_ref[...], w_ref[:, cols], preferred_element_type=F32)
        ssq = ssq + jnp.sum(y * y, axis=-1, keepdims=True)
        out_ref[:, cols] = y
    scale = lax.rsqrt(ssq * (1.0 / D_MODEL) + EPS)
    out_ref[...] = x_ref[...] + out_ref[...] * scale * nw_ref[...]


def _out_kernel(*refs):
    mix_even, mix_odd = refs[-2:]
    s = pl.program_id(0)

    @pl.when(s == 0)
    def _():
        mix_odd[...] = jnp.zeros_like(mix_odd)

    @pl.when(s % 2 == 0)
    def _():
        _out_tile(mix_odd, mix_even, *refs[:-2])

    @pl.when(s % 2 == 1)
    def _():
        _out_tile(mix_even, mix_odd, *refs[:-2])


def _out_proj(pats, o_gla, mem_pair, proj, stat_mask, expand, expand_mem, w_out, x2d, norm_w):
    m = x2d.shape[0]
    n_tiles = m // OUT_TM
    gate_rows = lambda width, col=0: pl.BlockSpec(
        (OUT_TM, width), lambda s: (jnp.minimum(s, n_tiles - 1), col // width))
    proj_rows = lambda width: pl.BlockSpec((OUT_TM, width), lambda s: (jnp.maximum(s - 1, 0), 0))
    const = lambda shape: pl.BlockSpec(shape, lambda s: (0, 0), pipeline_mode=pl.Buffered(1))
    nums, stats = zip(*pats)
    return pl.pallas_call(
        _out_kernel,
        grid=(n_tiles + 1,),
        in_specs=[
            *([gate_rows(D_DIL)] * 3), *([gate_rows(LANES)] * 3),
            gate_rows(D_GLA_V), gate_rows(D_MEM), gate_rows(LANES),
            gate_rows(D_DIL, COL_GATE_A), gate_rows(D_GLA_V, COL_GATE_G), gate_rows(D_MEM, COL_GATE_M),
            const((8, LANES)), const((LANES, D_DIL)), const((LANES, D_MEM)), const((D_MIX, D_MODEL)),
            proj_rows(D_MODEL), const((1, D_MODEL)),
        ],
        out_specs=proj_rows(D_MODEL),
        out_shape=jax.ShapeDtypeStruct((m, D_MODEL), x2d.dtype),
        scratch_shapes=[pltpu.VMEM((OUT_TM, D_MIX), BF16), pltpu.VMEM((OUT_TM, D_MIX), BF16)],
        compiler_params=_cparams(("arbitrary",)),
        name="out_proj",
    )(*nums, *stats, o_gla, *mem_pair, proj, proj, proj, stat_mask, expand, expand_mem, w_out, x2d, norm_w)


D_IN_PROJ = 3 * D_DIL + 2 * D_GLA_K + D_GLA_V + GLA_GATE_RANK + D_MEM + D_MIX
SRC_GLR = 3 * D_DIL + 2 * D_GLA_K + D_GLA_V
SRC_MQ = SRC_GLR + GLA_GATE_RANK
SRC_GATE = SRC_MQ + D_MEM
REST_SEGMENTS = (
    (QKV_W, 2 * D_GLA_K + D_GLA_V, COL_GQ),
    (SRC_GATE + D_DIL, D_GLA_V, COL_GATE_G),
    (SRC_MQ, D_MEM, COL_MQ),
    (SRC_GATE + D_DIL + D_GLA_V, D_MEM, COL_GATE_M),
    (SRC_GATE, D_DIL, COL_GATE_A),
)
WPREP_TK = 512


def _wprep_kernel(wt_ref, qkv_ref, rest_ref):
    def put(o_ref, src, width, dst):
        for c in range(width // LANES):
            lo = c * LANES
            o_ref[:, dst + lo:dst + lo + LANES] = wt_ref[src + lo:src + lo + LANES, :].T.astype(o_ref.dtype)

    put(qkv_ref, 0, QKV_W, 0)
    for src, width, dst in REST_SEGMENTS:
        put(rest_ref, src, width, dst)
    glr = wt_ref[SRC_GLR:SRC_GLR + LANES, :].T
    keep = lax.broadcasted_iota(jnp.int32, glr.shape, 1) < GLA_GATE_RANK
    rest_ref[:, COL_GLR:COL_GLR + LANES] = jnp.where(keep, glr, 0.0).astype(rest_ref.dtype)
    rest_ref[:, COL_GLR + LANES:COL_GLR + GLR_PAD] = jnp.zeros((WPREP_TK, GLR_PAD - LANES), rest_ref.dtype)


def _regroup_in_weights(w_in):
    w_qkv, w_rest = pl.pallas_call(
        _wprep_kernel,
        grid=(D_MODEL // WPREP_TK,),
        in_specs=[pl.BlockSpec((D_IN_PROJ, WPREP_TK), lambda i: (0, i))],
        out_specs=[pl.BlockSpec((WPREP_TK, QKV_W), lambda i: (i, 0)),
                   pl.BlockSpec((WPREP_TK, REST_W), lambda i: (i, 0))],
        out_shape=[jax.ShapeDtypeStruct((D_MODEL, QKV_W), BF16), jax.ShapeDtypeStruct((D_MODEL, REST_W), BF16)],
        compiler_params=_cparams(("parallel",)),
        name="weight_regroup",
    )(w_in.T)
    scale_qkv = np.ones((1, QKV_W), np.float32)
    scale_qkv[:, :D_DIL] = HEAD_DIM ** -0.5 * LOG2E
    scale_rest = np.ones((1, REST_W), np.float32)
    scale_rest[:, COL_GQ:COL_GQ + D_GLA_K] = GLA_DK ** -0.5
    scale_rest[:, COL_MQ:COL_MQ + D_MEM] = HEAD_DIM ** -0.5 * LOG2E
    return w_qkv, w_rest, jnp.asarray(scale_qkv), jnp.asarray(scale_rest)


def _split_hi_lo(w):
    hi = w.astype(BF16)
    return hi, (w - hi.astype(F32)).astype(BF16)


def kernel(x, mem, norm_pre_w, w_in, rel_bias, w_gla_gate2, b_gla_gate, gla_norm_w, mem_norm_w, w_mem_kv,
           w_out, norm_post_w):
    batch, seq, _ = x.shape
    depth = w_in.shape[0]
    expand_np = np.zeros((LANES, D_DIL), np.float32)
    for h in range(N_HEADS_DIL):
        expand_np[_stat_lane(h), h * HEAD_DIM:(h + 1) * HEAD_DIM] = 1.0
    expand = jnp.asarray(expand_np, BF16)
    expand_mem = jnp.asarray(expand_np[:, :D_MEM], BF16)
    stat_mask_np = np.zeros((8, LANES), np.float32)
    stat_mask_np[0, [_stat_lane(h) for h in range(N_HEADS_DIL)]] = 1.0
    stat_mask_np[1] = 1.0 - stat_mask_np[0]
    stat_mask = jnp.asarray(stat_mask_np)
    rb_t = jnp.zeros((16, LANES), F32).at[:N_HEADS_DIL, :REL_BUCKETS].set(rel_bias.astype(F32).T)
    rb_hi, rb_lo = _split_hi_lo(rb_t)
    for l in range(depth):
        x2d = x.reshape(batch * seq, D_MODEL)
        w_qkv, w_rest, scale_qkv, scale_rest = _regroup_in_weights(w_in[l])
        h, qkv1, qkv4, qkv16 = _qkv_proj(x2d, norm_pre_w[l][None], w_qkv, scale_qkv, batch, seq)
        rest = _rest_proj(h, w_rest, scale_rest)

        srcs = (qkv1.reshape(batch, 1, seq, QKV_W), qkv4, qkv16)
        pats = [_dilated(src, rb_hi, rb_lo, batch, seq, window, dil)
                for (window, dil), src in zip(DIL_PATTERNS, srcs)]

        w2 = jnp.zeros((LANES, D_GLA_K), F32).at[:GLA_GATE_RANK].set(w_gla_gate2[l])
        w2_hi, w2_lo = _split_hi_lo(w2)
        o_gla, w_out_bf = _gla(rest, w2_hi, w2_lo, b_gla_gate[l][None], gla_norm_w[l][None], w_out[l],
                               batch, seq)

        mem_pair = _mem_attn(rest, mem, mem_norm_w[l][None], w_mem_kv[l], batch, seq)

        out = _out_proj(pats, o_gla, mem_pair, rest, stat_mask, expand, expand_mem, w_out_bf, x2d,
                        norm_post_w[l][None])
        x = out.reshape(batch, seq, D_MODEL)
    return x
```

```python
import functools
import math

import numpy as np
import jax
import jax.numpy as jnp
from jax import lax
from jax.experimental import pallas as pl
from jax.experimental.pallas import tpu as pltpu

F32 = jnp.float32
BF16 = jnp.bfloat16

D_MODEL = 2048
HEAD_DIM = 64
N_HEADS_DIL = 12
DIL_PATTERNS = ((128, 1), (512, 4), (2048, 16))
DIL_BLOCK = 128
N_HEADS_GLA = 4
GLA_DK = 128
GLA_DV = 256
GLA_GATE_RANK = 16
GLA_TAU = 16.0
GLA_CHUNK = 64
N_HEADS_MEM = 4
MEM_LEN = 256
REL_BUCKETS = 32
REL_MAX_DIST = 2048
EPS = 1e-6
NEG_INF = -1e30

D_DIL = N_HEADS_DIL * HEAD_DIM
D_GLA_K = N_HEADS_GLA * GLA_DK
D_GLA_V = N_HEADS_GLA * GLA_DV
D_MEM = N_HEADS_MEM * HEAD_DIM
D_MIX = D_DIL + D_GLA_V + D_MEM

LANES = 128
GLR_PAD = 256

QKV_W = 3 * D_DIL
COL_GQ = 0
COL_GK = COL_GQ + D_GLA_K
COL_GV = COL_GK + D_GLA_K
COL_GATE_G = COL_GV + D_GLA_V
COL_MQ = COL_GATE_G + D_GLA_V
COL_GATE_M = COL_MQ + D_MEM
COL_GLR = COL_GATE_M + D_MEM
COL_GATE_A = COL_GLR + GLR_PAD
REST_W = COL_GATE_A + D_DIL

VMEM_LIMIT = 56 * 1024 * 1024


def _cparams(sem):
    return pltpu.CompilerParams(dimension_semantics=sem, vmem_limit_bytes=VMEM_LIMIT)


QKV_TM = 512
N_SLABS = D_DIL // LANES
REGROUP_DILS = (4, 16)


def _qkv_proj_kernel(x_ref, nw_ref, w_ref, sc_ref, h_ref, o_ref, x4_ref, x16_ref, slab_ref):
    x = x_ref[...]
    ms = jnp.mean(x * x, axis=-1, keepdims=True)
    h_ref[...] = (x * lax.rsqrt(ms + EPS) * nw_ref[...]).astype(BF16)
    for t in range(QKV_W // D_DIL):
        cols = slice(t * D_DIL, (t + 1) * D_DIL)
        res = jnp.dot(h_ref[...], w_ref[:, cols], preferred_element_type=F32) * sc_ref[:, cols]
        o_ref[:, cols] = res.astype(o_ref.dtype)
        for s in range(N_SLABS):
            slab_ref[t, s] = res[:, s * LANES:(s + 1) * LANES]
        for out_ref, dil in zip((x4_ref, x16_ref), REGROUP_DILS):
            for r in range(dil):
                for s in range(N_SLABS):
                    lo = t * D_DIL + s * LANES
                    out_ref[0, r, :, lo:lo + LANES] = (
                        slab_ref[t, s, pl.ds(r, QKV_TM // dil, stride=dil), :].astype(out_ref.dtype))


def _qkv_proj(x2d, norm_w, w_qkv, scale, batch, seq):
    m = x2d.shape[0]
    tiles_per_seq = seq // QKV_TM
    const = lambda shape: pl.BlockSpec(shape, lambda i: (0, 0), pipeline_mode=pl.Buffered(1))
    grouped = lambda dil: pl.BlockSpec((1, dil, QKV_TM // dil, QKV_W),
                                       lambda i: (i // tiles_per_seq, 0, i % tiles_per_seq, 0))
    return pl.pallas_call(
        _qkv_proj_kernel,
        grid=(m // QKV_TM,),
        in_specs=[
            pl.BlockSpec((QKV_TM, D_MODEL), lambda i: (i, 0)),
            const((1, D_MODEL)), const((D_MODEL, QKV_W)), const((1, QKV_W)),
        ],
        out_specs=[pl.BlockSpec((QKV_TM, D_MODEL), lambda i: (i, 0)),
                   pl.BlockSpec((QKV_TM, QKV_W), lambda i: (i, 0))] + [grouped(d) for d in REGROUP_DILS],
        out_shape=[jax.ShapeDtypeStruct((m, D_MODEL), BF16), jax.ShapeDtypeStruct((m, QKV_W), BF16)]
        + [jax.ShapeDtypeStruct((batch, d, seq // d, QKV_W), BF16) for d in REGROUP_DILS],
        scratch_shapes=[pltpu.VMEM((QKV_W // D_DIL, N_SLABS, QKV_TM, LANES), F32)],
        compiler_params=_cparams(("parallel",)),
        name="qkv_proj",
    )(x2d, norm_w, w_qkv, scale)


REST_TM = 1024
REST_TN = 2304


def _rest_proj_kernel(h_ref, w_ref, sc_ref, o_ref):
    acc = jnp.dot(h_ref[...], w_ref[...], preferred_element_type=F32)
    o_ref[...] = (acc * sc_ref[...]).astype(o_ref.dtype)


def _rest_proj(h, w_rest, scale):
    m = h.shape[0]
    return pl.pallas_call(
        _rest_proj_kernel,
        grid=(m // REST_TM, REST_W // REST_TN),
        in_specs=[
            pl.BlockSpec((REST_TM, D_MODEL), lambda i, j: (i, 0)),
            pl.BlockSpec((D_MODEL, REST_TN), lambda i, j: (0, j)),
            pl.BlockSpec((1, REST_TN), lambda i, j: (0, j)),
        ],
        out_specs=pl.BlockSpec((REST_TM, REST_TN), lambda i, j: (i, j)),
        out_shape=jax.ShapeDtypeStruct((m, REST_W), BF16),
        compiler_params=_cparams(("parallel", "arbitrary")),
        name="rest_proj",
    )(h, w_rest, scale)


LOG2E = math.log2(math.e)


def _pair_attention(q_pair, k_pair, v_pair, bias_e, bias_o):
    rows = q_pair.shape[0]
    lo_q = lax.broadcasted_iota(jnp.int32, q_pair.shape, 1) < HEAD_DIM
    zero = jnp.zeros_like(q_pair)
    q_both = jnp.concatenate([jnp.where(lo_q, q_pair, zero), jnp.where(lo_q, zero, q_pair)], axis=0)
    s = lax.dot_general(q_both, k_pair, (((1,), (1,)), ((), ())), preferred_element_type=F32)
    s_e, s_o = s[:rows], s[rows:]
    if bias_e is not None:
        s_e = s_e + bias_e
        s_o = s_o + bias_o
    mx_e = jnp.max(s_e, axis=-1, keepdims=True)
    mx_o = jnp.max(s_o, axis=-1, keepdims=True)
    p_e = jnp.exp2(s_e - mx_e).astype(BF16)
    p_o = jnp.exp2(s_o - mx_o).astype(BF16)
    lo_v = lax.broadcasted_iota(jnp.int32, v_pair.shape, 1) < HEAD_DIM
    one = jnp.ones_like(v_pair)
    r_e = jnp.dot(p_e, jnp.where(lo_v, v_pair, one), preferred_element_type=F32)
    r_o = jnp.dot(p_o, jnp.where(lo_v, one, v_pair), preferred_element_type=F32)
    return r_e, r_o, mx_e, mx_o


def _stat_lane(head):
    return head + HEAD_DIM if head % 2 == 0 else head


STAT_DEN_SHIFT = 16


DIL_UNITS = {1: 8, 4: 4, 16: 8}


def _dil_init_bias(rbh_ref, rbl_ref, sel_ref, mrow_ref, bias_ref):
    f = (jnp.dot(rbh_ref[...], sel_ref[...], preferred_element_type=F32)
         + jnp.dot(rbl_ref[...], sel_ref[...], preferred_element_type=F32)) * LOG2E + mrow_ref[...]
    col = lax.broadcasted_iota(jnp.int32, (DIL_BLOCK, 2 * DIL_BLOCK), 1)
    for h in range(N_HEADS_DIL):
        row = jnp.broadcast_to(f[h:h + 1, :], (DIL_BLOCK, 2 * DIL_BLOCK))
        tab = pltpu.roll(row, 0, 1, stride=1, stride_axis=0)
        bias_ref[0, h] = tab
        bias_ref[1, h] = jnp.where(col >= DIL_BLOCK, tab, NEG_INF)


def _dil_unit(q, kp, kc, vp, vc, first, bias_ref, put_num):
    lane = lax.broadcasted_iota(jnp.int32, (DIL_BLOCK, LANES), 1)
    lo = lane < HEAD_DIM
    st_c = jnp.zeros((DIL_BLOCK, LANES), F32)
    for j in range(N_HEADS_DIL // 2):
        sl = slice(j * LANES, (j + 1) * LANES)
        k_pair = jnp.concatenate([kp(sl), kc(sl)], axis=0)
        v_pair = jnp.concatenate([vp(sl), vc(sl)], axis=0)
        r_e, r_o, mx_e, mx_o = _pair_attention(q(sl), k_pair, v_pair,
                                               bias_ref[first, 2 * j], bias_ref[first, 2 * j + 1])
        put_num(j, jnp.where(lo, r_e, r_o))
        l_e, l_o = _stat_lane(2 * j), _stat_lane(2 * j + 1)
        st_c = jnp.where(lane == l_e, mx_e, jnp.where(lane == l_e + STAT_DEN_SHIFT, r_e, st_c))
        st_c = jnp.where(lane == l_o, mx_o, jnp.where(lane == l_o + STAT_DEN_SHIFT, r_o, st_c))
    return st_c


def _dil1_kernel(q_ref, kp_ref, kc_ref, vp_ref, vc_ref, rbh_ref, rbl_ref, sel_ref, mrow_ref,
                 o_ref, st_ref, bias_ref):
    @pl.when((pl.program_id(0) == 0) & (pl.program_id(1) == 0))
    def _():
        _dil_init_bias(rbh_ref, rbl_ref, sel_ref, mrow_ref, bias_ref)

    for u in range(DIL_UNITS[1]):
        rows = slice(u * DIL_BLOCK, (u + 1) * DIL_BLOCK)
        prev = slice((u - 1) * DIL_BLOCK, u * DIL_BLOCK)
        if u == 0:
            first = (pl.program_id(1) == 0).astype(jnp.int32)
            kp, vp = (lambda sl: kp_ref[:, sl]), (lambda sl: vp_ref[:, sl])
        else:
            first = 0
            kp, vp = (lambda sl, p=prev: kc_ref[p, sl]), (lambda sl, p=prev: vc_ref[p, sl])

        def put_num(j, num, rows=rows):
            o_ref[rows, j * LANES:(j + 1) * LANES] = num.astype(o_ref.dtype)

        st_ref[rows, :] = _dil_unit(lambda sl, r=rows: q_ref[r, sl], kp, lambda sl, r=rows: kc_ref[r, sl],
                                    vp, lambda sl, r=rows: vc_ref[r, sl], first, bias_ref, put_num)


DIL_LBLOCKS = {4: 2, 16: 1}


def _dil_kernel(q_ref, kp_ref, kc_ref, vp_ref, vc_ref, rbh_ref, rbl_ref, sel_ref, mrow_ref,
                o_ref, st_ref, bias_ref, acc_ref, *, dil):
    b, lb, g = pl.program_id(0), pl.program_id(1), pl.program_id(2)
    nr, nl = DIL_UNITS[dil], DIL_LBLOCKS[dil]

    @pl.when((b == 0) & (lb == 0) & (g == 0))
    def _():
        _dil_init_bias(rbh_ref, rbl_ref, sel_ref, mrow_ref, bias_ref)

    for ll in range(nl):
        sub = slice(ll * DIL_BLOCK, (ll + 1) * DIL_BLOCK)
        prev = slice((ll - 1) * DIL_BLOCK, ll * DIL_BLOCK)
        first = (lb == 0).astype(jnp.int32) if ll == 0 else 0
        for u in range(nr):
            rows = pl.ds(ll * DIL_BLOCK * dil + g * nr + u, DIL_BLOCK, stride=dil)
            if ll == 0:
                kp, vp = (lambda sl, u=u: kp_ref[u, :, sl]), (lambda sl, u=u: vp_ref[u, :, sl])
            else:
                kp = lambda sl, u=u, p=prev: kc_ref[u, p, sl]
                vp = lambda sl, u=u, p=prev: vc_ref[u, p, sl]

            def put_num(j, num, rows=rows):
                acc_ref[j, rows, :] = num

            st_ref[rows, :] = _dil_unit(lambda sl, u=u, r=sub: q_ref[u, r, sl], kp,
                                        lambda sl, u=u, r=sub: kc_ref[u, r, sl], vp,
                                        lambda sl, u=u, r=sub: vc_ref[u, r, sl], first, bias_ref, put_num)

    @pl.when(g == dil // nr - 1)
    def _():
        for j in range(N_HEADS_DIL // 2):
            o_ref[:, j * LANES:(j + 1) * LANES] = acc_ref[j].astype(o_ref.dtype)


def _t5_bucket_np(dist):
    max_exact = REL_BUCKETS // 2
    n = np.maximum(dist, 1).astype(np.float32)
    large = max_exact + (np.log(n / np.float32(max_exact)) / np.float32(math.log(REL_MAX_DIST / max_exact))
                         * np.float32(REL_BUCKETS - max_exact)).astype(np.int32)
    large = np.minimum(large, REL_BUCKETS - 1)
    return np.where(dist < max_exact, dist, large)


def _bias_selector(window, dil):
    back = window // dil
    x = np.arange(2 * DIL_BLOCK)
    rel = DIL_BLOCK - x
    valid = (rel >= 0) & (rel <= back)
    bucket = _t5_bucket_np(np.clip(rel, 0, back) * dil)
    sel = np.zeros((LANES, 2 * DIL_BLOCK), np.float32)
    sel[bucket[valid], x[valid]] = 1.0
    mrow = np.where(valid, 0.0, NEG_INF).astype(np.float32)[None]
    return jnp.asarray(sel, BF16), jnp.asarray(mrow)


def _dilated(src, rb_hi, rb_lo, batch, seq, window, dil):
    sub_len = seq // dil
    nb = sub_len // DIL_BLOCK
    sel, mrow = _bias_selector(window, dil)
    nu = DIL_UNITS[dil]
    consts = (rb_hi, rb_lo, sel, mrow)
    const = lambda a: pl.BlockSpec(a.shape, lambda *_: (0, 0))
    stat_shape = jax.ShapeDtypeStruct((batch * seq, LANES), F32)
    out_shape = [jax.ShapeDtypeStruct((batch * seq, D_DIL), BF16), stat_shape]
    bias_scratch = pltpu.VMEM((2, N_HEADS_DIL, DIL_BLOCK, 2 * DIL_BLOCK), F32)

    if dil == 1:
        rows = nu * DIL_BLOCK
        steps = nb // nu
        cur = lambda col: pl.BlockSpec((None, None, rows, D_DIL), lambda b, l: (b, 0, l, col))
        prev = lambda col: pl.BlockSpec((None, None, DIL_BLOCK, D_DIL),
                                        lambda b, l: (b, 0, jnp.maximum(nu * l - 1, 0), col))
        out = lambda width: pl.BlockSpec((rows, width), lambda b, l: (b * steps + l, 0))
        return pl.pallas_call(
            _dil1_kernel,
            grid=(batch, steps),
            in_specs=[cur(0), prev(1), cur(1), prev(2), cur(2)] + [const(a) for a in consts],
            out_specs=[out(D_DIL), out(LANES)],
            out_shape=out_shape,
            scratch_shapes=[bias_scratch],
            compiler_params=_cparams(("arbitrary", "arbitrary")),
            name="dilated_d1",
        )(src, src, src, src, src, *consts)

    nl = DIL_LBLOCKS[dil]
    steps = nb // nl
    rows = nl * DIL_BLOCK * dil
    cur = lambda col: pl.BlockSpec((None, nu, nl * DIL_BLOCK, D_DIL), lambda b, l, g: (b, g, l, col))
    prev = lambda col: pl.BlockSpec((None, nu, DIL_BLOCK, D_DIL),
                                    lambda b, l, g: (b, g, jnp.maximum(nl * l - 1, 0), col))
    out = lambda width: pl.BlockSpec((rows, width), lambda b, l, g: (b * steps + l, 0))
    return pl.pallas_call(
        functools.partial(_dil_kernel, dil=dil),
        grid=(batch, steps, dil // nu),
        in_specs=[cur(0), prev(1), cur(1), prev(2), cur(2)] + [const(a) for a in consts],
        out_specs=[out(D_DIL), out(LANES)],
        out_shape=out_shape,
        scratch_shapes=[bias_scratch, pltpu.VMEM((N_HEADS_DIL // 2, rows, LANES), F32)],
        compiler_params=_cparams(("arbitrary", "arbitrary", "arbitrary")),
        name=f"dilated_d{dil}",
    )(src, src, src, src, src, *consts)


GLA_TM = 256
GLA_STEP_TILES = 2


GLA_NC = GLA_TM // GLA_CHUNK


def _gla_tile(bb, rows, q_ref, k_ref, v_ref, g_ref, w2h_ref, w2l_ref, bg_ref, nw_ref, tri_ref, o_ref, st_ref):
    g = g_ref[bb, rows, :LANES]
    z = (jnp.dot(g, w2h_ref[...], preferred_element_type=F32)
         + jnp.dot(g, w2l_ref[...], preferred_element_type=F32) + bg_ref[...])
    la = (jnp.minimum(z, 0.0) * (LOG2E / GLA_TAU)
          - jnp.log2(1.0 + jnp.exp2(jnp.abs(z) * -LOG2E)) * (1.0 / GLA_TAU))
    la_h = la.astype(BF16)
    la_l = (la - la_h.astype(F32)).astype(BF16)

    tri = tri_ref[...]
    b = jnp.dot(tri, la_h, preferred_element_type=F32) + jnp.dot(tri, la_l, preferred_element_type=F32)
    b_last = jnp.concatenate(
        [jnp.broadcast_to(b[(ci + 1) * GLA_CHUNK - 1:(ci + 1) * GLA_CHUNK, :], (GLA_CHUNK, D_GLA_K))
         for ci in range(GLA_NC)], axis=0)

    q = q_ref[bb, rows].astype(F32)
    k = k_ref[bb, rows].astype(F32)
    q_t = (q * jnp.exp2(b)).astype(BF16)
    k_t = (k * jnp.exp2(-b)).astype(BF16)
    k_s = k * jnp.exp2(b_last - b)

    causal = tri.astype(F32) > 0.0
    t_chunk = lax.broadcasted_iota(jnp.int32, (GLA_DK, GLA_TM), 1) // GLA_CHUNK
    nw = nw_ref[...]
    for h in range(N_HEADS_GLA):
        cs = slice(h * GLA_DK, (h + 1) * GLA_DK)
        vs = slice(h * GLA_DV, (h + 1) * GLA_DV)
        v_h = v_ref[bb, rows, vs]
        a = lax.dot_general(q_t[:, cs], k_t[:, cs], (((1,), (1,)), ((), ())), preferred_element_type=F32)
        o = jnp.dot(jnp.where(causal, a, 0.0).astype(BF16), v_h, preferred_element_type=F32)
        k_s_t = k_s[:, cs].T.astype(BF16)
        zero = jnp.zeros_like(k_s_t)
        kv_all = jnp.dot(jnp.concatenate([jnp.where(t_chunk == ci, k_s_t, zero) for ci in range(GLA_NC)], axis=0),
                         v_h, preferred_element_type=F32)
        decay_t = jnp.exp2(b_last[:, cs].T)
        st = st_ref[bb, h]
        outs = []
        for ci in range(GLA_NC):
            rs = slice(ci * GLA_CHUNK, (ci + 1) * GLA_CHUNK)
            o_c = o[rs] + jnp.dot(q_t[rs, cs], st.astype(BF16), preferred_element_type=F32)
            decay = jnp.broadcast_to(decay_t[:, ci * GLA_CHUNK:ci * GLA_CHUNK + 1], (GLA_DK, GLA_DV))
            st = decay * st + kv_all[ci * GLA_DK:(ci + 1) * GLA_DK]
            ms = jnp.mean(o_c * o_c, axis=-1, keepdims=True)
            outs.append((o_c * lax.rsqrt(ms + EPS) * nw).astype(o_ref.dtype))
        st_ref[bb, h] = st
        o_ref[bb, rows, vs] = jnp.concatenate(outs, axis=0)


def _gla_kernel(*refs):
    *tile_refs, wout_ref, o_ref, wout_bf_ref, st_ref = refs

    @pl.when(pl.program_id(0) == 0)
    def _():
        st_ref[...] = jnp.zeros_like(st_ref)

    wout_bf_ref[...] = wout_ref[...].astype(wout_bf_ref.dtype)

    for tt in range(GLA_STEP_TILES):
        rows = slice(tt * GLA_TM, (tt + 1) * GLA_TM)
        for bb in range(st_ref.shape[0]):
            _gla_tile(bb, rows, *tile_refs, o_ref, st_ref)


def _gla(proj, w2_hi, w2_lo, b_gate, norm_w, w_out, batch, seq):
    step_rows = GLA_TM * GLA_STEP_TILES
    steps = seq // step_rows
    wout_rows = w_out.shape[0] // steps
    pos = np.arange(GLA_TM)
    tri = jnp.asarray((pos[:, None] // GLA_CHUNK == pos[None, :] // GLA_CHUNK) & (pos[None, :] <= pos[:, None]),
                      BF16)
    proj3 = proj.reshape(batch, seq, REST_W)
    rows = lambda width, col: pl.BlockSpec((batch, step_rows, width), lambda i: (0, i, col // width))
    const = lambda shape: pl.BlockSpec(shape, lambda i: (0, 0))
    wout_spec = pl.BlockSpec((wout_rows, w_out.shape[1]), lambda i: (i, 0))
    out, w_out_bf = pl.pallas_call(
        _gla_kernel,
        grid=(steps,),
        in_specs=[
            rows(D_GLA_K, COL_GQ), rows(D_GLA_K, COL_GK), rows(D_GLA_V, COL_GV), rows(GLR_PAD, COL_GLR),
            const((LANES, D_GLA_K)), const((LANES, D_GLA_K)), const((1, D_GLA_K)), const((1, GLA_DV)),
            const((GLA_TM, GLA_TM)), wout_spec,
        ],
        out_specs=[rows(D_GLA_V, 0), wout_spec],
        out_shape=[jax.ShapeDtypeStruct((batch, seq, D_GLA_V), BF16), jax.ShapeDtypeStruct(w_out.shape, BF16)],
        scratch_shapes=[pltpu.VMEM((batch, N_HEADS_GLA, GLA_DK, GLA_DV), F32)],
        compiler_params=_cparams(("arbitrary",)),
        name="gla",
    )(proj3, proj3, proj3, proj3, w2_hi, w2_lo, b_gate, norm_w, tri, w_out)
    return out.reshape(batch * seq, D_GLA_V), w_out_bf


MEM_TM = 1024
MEM_UNIT = 128


def _mem_kernel(q_ref, mem_ref, mnw_ref, wkv_ref, o_ref, den_ref, kv_ref):
    @pl.when(pl.program_id(1) == 0)
    def _():
        m = mem_ref[0]
        ms = jnp.mean(m * m, axis=-1, keepdims=True)
        mn = (m * lax.rsqrt(ms + EPS) * mnw_ref[...]).astype(BF16)
        kv_ref[...] = jnp.dot(mn, wkv_ref[...].astype(BF16), preferred_element_type=F32).astype(BF16)

    lane = lax.broadcasted_iota(jnp.int32, (MEM_UNIT, LANES), 1)
    lo = lane < HEAD_DIM
    for u in range(MEM_TM // MEM_UNIT):
        rows = slice(u * MEM_UNIT, (u + 1) * MEM_UNIT)
        den_c = jnp.ones((MEM_UNIT, LANES), F32)
        for j in range(N_HEADS_MEM // 2):
            sl = slice(j * LANES, (j + 1) * LANES)
            r_e, r_o, _, _ = _pair_attention(q_ref[rows, sl], kv_ref[:, sl],
                                             kv_ref[:, D_MEM + j * LANES:D_MEM + (j + 1) * LANES], None, None)
            o_ref[rows, sl] = jnp.where(lo, r_e, r_o).astype(o_ref.dtype)
            den_c = jnp.where(lane == _stat_lane(2 * j), r_e, jnp.where(lane == _stat_lane(2 * j + 1), r_o, den_c))
        den_ref[rows, :] = den_c


def _mem_attn(proj, mem, mem_norm_w, w_kv, batch, seq):
    steps = seq // MEM_TM
    rows = lambda width, col=0: pl.BlockSpec((MEM_TM, width), lambda b, i: (b * steps + i, col // width))
    return pl.pallas_call(
        _mem_kernel,
        grid=(batch, steps),
        in_specs=[
            rows(D_MEM, COL_MQ),
            pl.BlockSpec((1, MEM_LEN, D_MODEL), lambda b, i: (b, 0, 0)),
            pl.BlockSpec((1, D_MODEL), lambda b, i: (0, 0)),
            pl.BlockSpec((D_MODEL, 2 * D_MEM), lambda b, i: (0, 0)),
        ],
        out_specs=[rows(D_MEM), rows(LANES)],
        out_shape=[jax.ShapeDtypeStruct((batch * seq, D_MEM), BF16),
                   jax.ShapeDtypeStruct((batch * seq, LANES), F32)],
        scratch_shapes=[pltpu.VMEM((MEM_LEN, 2 * D_MEM), BF16)],
        compiler_params=_cparams(("parallel", "arbitrary")),
        name="mem_attn",
    )(proj, mem, mem_norm_w, w_kv)


OUT_TM = 512


def _silu(g):
    h = g * 0.5
    return h * jnp.tanh(h) + h


OUT_GW = 256
OUT_CW = 512


def _out_tile(read_ref, write_ref, n1_ref, n2_ref, n3_ref, s1_ref, s2_ref, s3_ref,
              og_ref, nm_ref, dm_ref, ga_ref, gg_ref, gm_ref, sm_ref, ex_ref, exm_ref, w_ref, x_ref, nw_ref,
              out_ref):
    keep = sm_ref[0:1, :]
    fill = sm_ref[1:2, :]
    stats = [s_ref[...] for s_ref in (s1_ref, s2_ref, s3_ref)]
    m1, m2, m3 = [st * keep for st in stats]
    d1, d2, d3 = [pltpu.roll(st, LANES - STAT_DEN_SHIFT, 1) * keep + fill for st in stats]
    mx = jnp.maximum(jnp.maximum(m1, m2), m3)
    t1, t2, t3 = jnp.exp2(m1 - mx), jnp.exp2(m2 - mx), jnp.exp2(m3 - mx)
    inv = 1.0 / (t1 * d1 + t2 * d2 + t3 * d3)
    wts = [(t * inv).astype(BF16) for t in (t1, t2, t3)]
    inv_m = (1.0 / dm_ref[...]).astype(BF16)

    def gate_chunk(lo):
        cols = slice(lo, lo + OUT_GW)
        if lo < D_DIL:
            a = jnp.zeros((OUT_TM, OUT_GW), F32)
            for wt, n_ref in zip(wts, (n1_ref, n2_ref, n3_ref)):
                a = a + jnp.dot(wt, ex_ref[:, cols], preferred_element_type=F32) * n_ref[:, cols].astype(F32)
            g = ga_ref[:, cols]
        elif lo < D_DIL + D_GLA_V:
            lc = slice(lo - D_DIL, lo - D_DIL + OUT_GW)
            a, g = og_ref[:, lc], gg_ref[:, lc]
        else:
            a = jnp.dot(inv_m, exm_ref[...], preferred_element_type=F32) * nm_ref[...].astype(F32)
            g = gm_ref[...]
        write_ref[:, cols] = a.astype(BF16) * _silu(g)

    ssq = jnp.zeros((OUT_TM, 1), F32)
    for c in range(D_MODEL // OUT_CW):
        for lo in range(c * OUT_CW, (c + 1) * OUT_CW, OUT_GW):
            gate_chunk(lo)
        cols = slice(c * OUT_CW, (c + 1) * OUT_CW)
        y = jnp.dot(read_ref[...], w_ref[:, cols], preferred_element_type=F32)
        ssq = ssq + jnp.sum(y * y, axis=-1, keepdims=True)
        out_ref[:, cols] = y
    scale = lax.rsqrt(ssq * (1.0 / D_MODEL) + EPS)
    out_ref[...] = x_ref[...] + out_ref[...] * scale * nw_ref[...]


def _out_kernel(*refs):
    mix_even, mix_odd = refs[-2:]
    s = pl.program_id(0)

    @pl.when(s == 0)
    def _():
        mix_odd[...] = jnp.zeros_like(mix_odd)

    @pl.when(s % 2 == 0)
    def _():
        _out_tile(mix_odd, mix_even, *refs[:-2])

    @pl.when(s % 2 == 1)
    def _():
        _out_tile(mix_even, mix_odd, *refs[:-2])


def _out_proj(pats, o_gla, mem_pair, proj, stat_mask, expand, expand_mem, w_out, x2d, norm_w):
    m = x2d.shape[0]
    n_tiles = m // OUT_TM
    gate_rows = lambda width, col=0: pl.BlockSpec(
        (OUT_TM, width), lambda s: (jnp.minimum(s, n_tiles - 1), col // width))
    proj_rows = lambda width: pl.BlockSpec((OUT_TM, width), lambda s: (jnp.maximum(s - 1, 0), 0))
    const = lambda shape: pl.BlockSpec(shape, lambda s: (0, 0), pipeline_mode=pl.Buffered(1))
    nums, stats = zip(*pats)
    return pl.pallas_call(
        _out_kernel,
        grid=(n_tiles + 1,),
        in_specs=[
            *([gate_rows(D_DIL)] * 3), *([gate_rows(LANES)] * 3),
            gate_rows(D_GLA_V), gate_rows(D_MEM), gate_rows(LANES),
            gate_rows(D_DIL, COL_GATE_A), gate_rows(D_GLA_V, COL_GATE_G), gate_rows(D_MEM, COL_GATE_M),
            const((8, LANES)), const((LANES, D_DIL)), const((LANES, D_MEM)), const((D_MIX, D_MODEL)),
            proj_rows(D_MODEL), const((1, D_MODEL)),
        ],
        out_specs=proj_rows(D_MODEL),
        out_shape=jax.ShapeDtypeStruct((m, D_MODEL), x2d.dtype),
        scratch_shapes=[pltpu.VMEM((OUT_TM, D_MIX), BF16), pltpu.VMEM((OUT_TM, D_MIX), BF16)],
        compiler_params=_cparams(("arbitrary",)),
        name="out_proj",
    )(*nums, *stats, o_gla, *mem_pair, proj, proj, proj, stat_mask, expand, expand_mem, w_out, x2d, norm_w)


D_IN_PROJ = 3 * D_DIL + 2 * D_GLA_K + D_GLA_V + GLA_GATE_RANK + D_MEM + D_MIX
SRC_GLR = 3 * D_DIL + 2 * D_GLA_K + D_GLA_V
SRC_MQ = SRC_GLR + GLA_GATE_RANK
SRC_GATE = SRC_MQ + D_MEM
REST_SEGMENTS = (
    (QKV_W, 2 * D_GLA_K + D_GLA_V, COL_GQ),
    (SRC_GATE + D_DIL, D_GLA_V, COL_GATE_G),
    (SRC_MQ, D_MEM, COL_MQ),
    (SRC_GATE + D_DIL + D_GLA_V, D_MEM, COL_GATE_M),
    (SRC_GATE, D_DIL, COL_GATE_A),
)
WPREP_TK = 512


def _wprep_kernel(wt_ref, qkv_ref, rest_ref):
    def put(o_ref, src, width, dst):
        for c in range(width // LANES):
            lo = c * LANES
            o_ref[:, dst + lo:dst + lo + LANES] = wt_ref[src + lo:src + lo + LANES, :].T.astype(o_ref.dtype)

    put(qkv_ref, 0, QKV_W, 0)
    for src, width, dst in REST_SEGMENTS:
        put(rest_ref, src, width, dst)
    glr = wt_ref[SRC_GLR:SRC_GLR + LANES, :].T
    keep = lax.broadcasted_iota(jnp.int32, glr.shape, 1) < GLA_GATE_RANK
    rest_ref[:, COL_GLR:COL_GLR + LANES] = jnp.where(keep, glr, 0.0).astype(rest_ref.dtype)
    rest_ref[:, COL_GLR + LANES:COL_GLR + GLR_PAD] = jnp.zeros((WPREP_TK, GLR_PAD - LANES), rest_ref.dtype)


def _regroup_in_weights(w_in):
    w_qkv, w_rest = pl.pallas_call(
        _wprep_kernel,
        grid=(D_MODEL // WPREP_TK,),
        in_specs=[pl.BlockSpec((D_IN_PROJ, WPREP_TK), lambda i: (0, i))],
        out_specs=[pl.BlockSpec((WPREP_TK, QKV_W), lambda i: (i, 0)),
                   pl.BlockSpec((WPREP_TK, REST_W), lambda i: (i, 0))],
        out_shape=[jax.ShapeDtypeStruct((D_MODEL, QKV_W), BF16), jax.ShapeDtypeStruct((D_MODEL, REST_W), BF16)],
        compiler_params=_cparams(("parallel",)),
        name="weight_regroup",
    )(w_in.T)
    scale_qkv = np.ones((1, QKV_W), np.float32)
    scale_qkv[:, :D_DIL] = HEAD_DIM ** -0.5 * LOG2E
    scale_rest = np.ones((1, REST_W), np.float32)
    scale_rest[:, COL_GQ:COL_GQ + D_GLA_K] = GLA_DK ** -0.5
    scale_rest[:, COL_MQ:COL_MQ + D_MEM] = HEAD_DIM ** -0.5 * LOG2E
    return w_qkv, w_rest, jnp.asarray(scale_qkv), jnp.asarray(scale_rest)


def _split_hi_lo(w):
    hi = w.astype(BF16)
    return hi, (w - hi.astype(F32)).astype(BF16)


def kernel(x, mem, norm_pre_w, w_in, rel_bias, w_gla_gate2, b_gla_gate, gla_norm_w, mem_norm_w, w_mem_kv,
           w_out, norm_post_w):
    batch, seq, _ = x.shape
    depth = w_in.shape[0]
    expand_np = np.zeros((LANES, D_DIL), np.float32)
    for h in range(N_HEADS_DIL):
        expand_np[_stat_lane(h), h * HEAD_DIM:(h + 1) * HEAD_DIM] = 1.0
    expand = jnp.asarray(expand_np, BF16)
    expand_mem = jnp.asarray(expand_np[:, :D_MEM], BF16)
    stat_mask_np = np.zeros((8, LANES), np.float32)
    stat_mask_np[0, [_stat_lane(h) for h in range(N_HEADS_DIL)]] = 1.0
    stat_mask_np[1] = 1.0 - stat_mask_np[0]
    stat_mask = jnp.asarray(stat_mask_np)
    rb_t = jnp.zeros((16, LANES), F32).at[:N_HEADS_DIL, :REL_BUCKETS].set(rel_bias.astype(F32).T)
    rb_hi, rb_lo = _split_hi_lo(rb_t)
    for l in range(depth):
        x2d = x.reshape(batch * seq, D_MODEL)
        w_qkv, w_rest, scale_qkv, scale_rest = _regroup_in_weights(w_in[l])
        h, qkv1, qkv4, qkv16 = _qkv_proj(x2d, norm_pre_w[l][None], w_qkv, scale_qkv, batch, seq)
        rest = _rest_proj(h, w_rest, scale_rest)

        srcs = (qkv1.reshape(batch, 1, seq, QKV_W), qkv4, qkv16)
        pats = [_dilated(src, rb_hi, rb_lo, batch, seq, window, dil)
                for (window, dil), src in zip(DIL_PATTERNS, srcs)]

        w2 = jnp.zeros((LANES, D_GLA_K), F32).at[:GLA_GATE_RANK].set(w_gla_gate2[l])
        w2_hi, w2_lo = _split_hi_lo(w2)
        o_gla, w_out_bf = _gla(rest, w2_hi, w2_lo, b_gla_gate[l][None], gla_norm_w[l][None], w_out[l],
                               batch, seq)

        mem_pair = _mem_attn(rest, mem, mem_norm_w[l][None], w_mem_kv[l], batch, seq)

        out = _out_proj(pats, o_gla, mem_pair, rest, stat_mask, expand, expand_mem, w_out_bf, x2d,
                        norm_post_w[l][None])
        x = out.reshape(batch, seq, D_MODEL)
    return x
```

```python
import functools
import math

import numpy as np
import jax
import jax.numpy as jnp
from jax import lax
from jax.experimental import pallas as pl
from jax.experimental.pallas import tpu as pltpu

F32 = jnp.float32
BF16 = jnp.bfloat16

D_MODEL = 2048
HEAD_DIM = 64
N_HEADS_DIL = 12
DIL_PATTERNS = ((128, 1), (512, 4), (2048, 16))
DIL_BLOCK = 128
N_HEADS_GLA = 4
GLA_DK = 128
GLA_DV = 256
GLA_GATE_RANK = 16
GLA_TAU = 16.0
GLA_CHUNK = 64
N_HEADS_MEM = 4
MEM_LEN = 256
REL_BUCKETS = 32
REL_MAX_DIST = 2048
EPS = 1e-6
NEG_INF = -1e30

D_DIL = N_HEADS_DIL * HEAD_DIM
D_GLA_K = N_HEADS_GLA * GLA_DK
D_GLA_V = N_HEADS_GLA * GLA_DV
D_MEM = N_HEADS_MEM * HEAD_DIM
D_MIX = D_DIL + D_GLA_V + D_MEM

LANES = 128
GLR_PAD = 256

QKV_W = 3 * D_DIL
COL_GQ = 0
COL_GK = COL_GQ + D_GLA_K
COL_GV = COL_GK + D_GLA_K
COL_GATE_G = COL_GV + D_GLA_V
COL_MQ = COL_GATE_G + D_GLA_V
COL_GATE_M = COL_MQ + D_MEM
COL_GLR = COL_GATE_M + D_MEM
COL_GATE_A = COL_GLR + GLR_PAD
REST_W = COL_GATE_A + D_DIL

VMEM_LIMIT = 56 * 1024 * 1024


def _cparams(sem):
    return pltpu.CompilerParams(dimension_semantics=sem, vmem_limit_bytes=VMEM_LIMIT)


QKV_TM = 512
N_SLABS = D_DIL // LANES
REGROUP_DILS = (4, 16)


def _qkv_proj_kernel(x_ref, nw_ref, w_ref, sc_ref, h_ref, o_ref, x4_ref, x16_ref, slab_ref):
    x = x_ref[...]
    ms = jnp.mean(x * x, axis=-1, keepdims=True)
    h_ref[...] = (x * lax.rsqrt(ms + EPS) * nw_ref[...]).astype(BF16)
    for t in range(QKV_W // D_DIL):
        cols = slice(t * D_DIL, (t + 1) * D_DIL)
        res = jnp.dot(h_ref[...], w_ref[:, cols], preferred_element_type=F32) * sc_ref[:, cols]
        o_ref[:, cols] = res.astype(o_ref.dtype)
        for s in range(N_SLABS):
            slab_ref[t, s] = res[:, s * LANES:(s + 1) * LANES]
        for out_ref, dil in zip((x4_ref, x16_ref), REGROUP_DILS):
            for r in range(dil):
                for s in range(N_SLABS):
                    lo = t * D_DIL + s * LANES
                    out_ref[0, r, :, lo:lo + LANES] = (
                        slab_ref[t, s, pl.ds(r, QKV_TM // dil, stride=dil), :].astype(out_ref.dtype))


def _qkv_proj(x2d, norm_w, w_qkv, scale, batch, seq):
    m = x2d.shape[0]
    tiles_per_seq = seq // QKV_TM
    const = lambda shape: pl.BlockSpec(shape, lambda i: (0, 0), pipeline_mode=pl.Buffered(1))
    grouped = lambda dil: pl.BlockSpec((1, dil, QKV_TM // dil, QKV_W),
                                       lambda i: (i // tiles_per_seq, 0, i % tiles_per_seq, 0))
    return pl.pallas_call(
        _qkv_proj_kernel,
        grid=(m // QKV_TM,),
        in_specs=[
            pl.BlockSpec((QKV_TM, D_MODEL), lambda i: (i, 0)),
            const((1, D_MODEL)), const((D_MODEL, QKV_W)), const((1, QKV_W)),
        ],
        out_specs=[pl.BlockSpec((QKV_TM, D_MODEL), lambda i: (i, 0)),
                   pl.BlockSpec((QKV_TM, QKV_W), lambda i: (i, 0))] + [grouped(d) for d in REGROUP_DILS],
        out_shape=[jax.ShapeDtypeStruct((m, D_MODEL), BF16), jax.ShapeDtypeStruct((m, QKV_W), BF16)]
        + [jax.ShapeDtypeStruct((batch, d, seq // d, QKV_W), BF16) for d in REGROUP_DILS],
        scratch_shapes=[pltpu.VMEM((QKV_W // D_DIL, N_SLABS, QKV_TM, LANES), F32)],
        compiler_params=_cparams(("parallel",)),
        name="qkv_proj",
    )(x2d, norm_w, w_qkv, scale)


REST_TM = 1024
REST_TN = 2304


def _rest_proj_kernel(h_ref, w_ref, sc_ref, o_ref):
    acc = jnp.dot(h_ref[...], w_ref[...], preferred_element_type=F32)
    o_ref[...] = (acc * sc_ref[...]).astype(o_ref.dtype)


def _rest_proj(h, w_rest, scale):
    m = h.shape[0]
    return pl.pallas_call(
        _rest_proj_kernel,
        grid=(m // REST_TM, REST_W // REST_TN),
        in_specs=[
            pl.BlockSpec((REST_TM, D_MODEL), lambda i, j: (i, 0)),
            pl.BlockSpec((D_MODEL, REST_TN), lambda i, j: (0, j)),
            pl.BlockSpec((1, REST_TN), lambda i, j: (0, j)),
        ],
        out_specs=pl.BlockSpec((REST_TM, REST_TN), lambda i, j: (i, j)),
        out_shape=jax.ShapeDtypeStruct((m, REST_W), BF16),
        compiler_params=_cparams(("parallel", "arbitrary")),
        name="rest_proj",
    )(h, w_rest, scale)


LOG2E = math.log2(math.e)


def _pair_attention(q_pair, k_pair, v_pair, bias_e, bias_o):
    rows = q_pair.shape[0]
    lo_q = lax.broadcasted_iota(jnp.int32, q_pair.shape, 1) < HEAD_DIM
    zero = jnp.zeros_like(q_pair)
    q_both = jnp.concatenate([jnp.where(lo_q, q_pair, zero), jnp.where(lo_q, zero, q_pair)], axis=0)
    s = lax.dot_general(q_both, k_pair, (((1,), (1,)), ((), ())), preferred_element_type=F32)
    s_e, s_o = s[:rows], s[rows:]
    if bias_e is not None:
        s_e = s_e + bias_e
        s_o = s_o + bias_o
    mx_e = jnp.max(s_e, axis=-1, keepdims=True)
    mx_o = jnp.max(s_o, axis=-1, keepdims=True)
    p_e = jnp.exp2(s_e - mx_e).astype(BF16)
    p_o = jnp.exp2(s_o - mx_o).astype(BF16)
    lo_v = lax.broadcasted_iota(jnp.int32, v_pair.shape, 1) < HEAD_DIM
    one = jnp.ones_like(v_pair)
    r_e = jnp.dot(p_e, jnp.where(lo_v, v_pair, one), preferred_element_type=F32)
    r_o = jnp.dot(p_o, jnp.where(lo_v, one, v_pair), preferred_element_type=F32)
    return r_e, r_o, mx_e, mx_o


def _stat_lane(head):
    return head + HEAD_DIM if head % 2 == 0 else head


STAT_DEN_SHIFT = 16


DIL_UNITS = {1: 8, 4: 4, 16: 8}


def _dil_init_bias(rbh_ref, rbl_ref, sel_ref, mrow_ref, bias_ref):
    f = (jnp.dot(rbh_ref[...], sel_ref[...], preferred_element_type=F32)
         + jnp.dot(rbl_ref[...], sel_ref[...], preferred_element_type=F32)) * LOG2E + mrow_ref[...]
    col = lax.broadcasted_iota(jnp.int32, (DIL_BLOCK, 2 * DIL_BLOCK), 1)
    for h in range(N_HEADS_DIL):
        row = jnp.broadcast_to(f[h:h + 1, :], (DIL_BLOCK, 2 * DIL_BLOCK))
        tab = pltpu.roll(row, 0, 1, stride=1, stride_axis=0)
        bias_ref[0, h] = tab
        bias_ref[1, h] = jnp.where(col >= DIL_BLOCK, tab, NEG_INF)


def _dil_unit(q, kp, kc, vp, vc, first, bias_ref, put_num):
    lane = lax.broadcasted_iota(jnp.int32, (DIL_BLOCK, LANES), 1)
    lo = lane < HEAD_DIM
    st_c = jnp.zeros((DIL_BLOCK, LANES), F32)
    for j in range(N_HEADS_DIL // 2):
        sl = slice(j * LANES, (j + 1) * LANES)
        k_pair = jnp.concatenate([kp(sl), kc(sl)], axis=0)
        v_pair = jnp.concatenate([vp(sl), vc(sl)], axis=0)
        r_e, r_o, mx_e, mx_o = _pair_attention(q(sl), k_pair, v_pair,
                                               bias_ref[first, 2 * j], bias_ref[first, 2 * j + 1])
        put_num(j, jnp.where(lo, r_e, r_o))
        l_e, l_o = _stat_lane(2 * j), _stat_lane(2 * j + 1)
        st_c = jnp.where(lane == l_e, mx_e, jnp.where(lane == l_e + STAT_DEN_SHIFT, r_e, st_c))
        st_c = jnp.where(lane == l_o, mx_o, jnp.where(lane == l_o + STAT_DEN_SHIFT, r_o, st_c))
    return st_c


def _dil1_kernel(q_ref, kp_ref, kc_ref, vp_ref, vc_ref, rbh_ref, rbl_ref, sel_ref, mrow_ref,
                 o_ref, st_ref, bias_ref):
    @pl.when((pl.program_id(0) == 0) & (pl.program_id(1) == 0))
    def _():
        _dil_init_bias(rbh_ref, rbl_ref, sel_ref, mrow_ref, bias_ref)

    for u in range(DIL_UNITS[1]):
        rows = slice(u * DIL_BLOCK, (u + 1) * DIL_BLOCK)
        prev = slice((u - 1) * DIL_BLOCK, u * DIL_BLOCK)
        if u == 0:
            first = (pl.program_id(1) == 0).astype(jnp.int32)
            kp, vp = (lambda sl: kp_ref[:, sl]), (lambda sl: vp_ref[:, sl])
        else:
            first = 0
            kp, vp = (lambda sl, p=prev: kc_ref[p, sl]), (lambda sl, p=prev: vc_ref[p, sl])

        def put_num(j, num, rows=rows):
            o_ref[rows, j * LANES:(j + 1) * LANES] = num.astype(o_ref.dtype)

        st_ref[rows, :] = _dil_unit(lambda sl, r=rows: q_ref[r, sl], kp, lambda sl, r=rows: kc_ref[r, sl],
                                    vp, lambda sl, r=rows: vc_ref[r, sl], first, bias_ref, put_num)


DIL_LBLOCKS = {4: 2, 16: 1}


def _dil_kernel(q_ref, kp_ref, kc_ref, vp_ref, vc_ref, rbh_ref, rbl_ref, sel_ref, mrow_ref,
                o_ref, st_ref, bias_ref, acc_ref, *, dil):
    b, lb, g = pl.program_id(0), pl.program_id(1), pl.program_id(2)
    nr, nl = DIL_UNITS[dil], DIL_LBLOCKS[dil]

    @pl.when((b == 0) & (lb == 0) & (g == 0))
    def _():
        _dil_init_bias(rbh_ref, rbl_ref, sel_ref, mrow_ref, bias_ref)

    for ll in range(nl):
        sub = slice(ll * DIL_BLOCK, (ll + 1) * DIL_BLOCK)
        prev = slice((ll - 1) * DIL_BLOCK, ll * DIL_BLOCK)
        first = (lb == 0).astype(jnp.int32) if ll == 0 else 0
        for u in range(nr):
            rows = pl.ds(ll * DIL_BLOCK * dil + g * nr + u, DIL_BLOCK, stride=dil)
            if ll == 0:
                kp, vp = (lambda sl, u=u: kp_ref[u, :, sl]), (lambda sl, u=u: vp_ref[u, :, sl])
            else:
                kp = lambda sl, u=u, p=prev: kc_ref[u, p, sl]
                vp = lambda sl, u=u, p=prev: vc_ref[u, p, sl]

            def put_num(j, num, rows=rows):
                acc_ref[j, rows, :] = num

            st_ref[rows, :] = _dil_unit(lambda sl, u=u, r=sub: q_ref[u, r, sl], kp,
                                        lambda sl, u=u, r=sub: kc_ref[u, r, sl], vp,
                                        lambda sl, u=u, r=sub: vc_ref[u, r, sl], first, bias_ref, put_num)

    @pl.when(g == dil // nr - 1)
    def _():
        for j in range(N_HEADS_DIL // 2):
            o_ref[:, j * LANES:(j + 1) * LANES] = acc_ref[j].astype(o_ref.dtype)


def _t5_bucket_np(dist):
    max_exact = REL_BUCKETS // 2
    n = np.maximum(dist, 1).astype(np.float32)
    large = max_exact + (np.log(n / np.float32(max_exact)) / np.float32(math.log(REL_MAX_DIST / max_exact))
                         * np.float32(REL_BUCKETS - max_exact)).astype(np.int32)
    large = np.minimum(large, REL_BUCKETS - 1)
    return np.where(dist < max_exact, dist, large)


def _bias_selector(window, dil):
    back = window // dil
    x = np.arange(2 * DIL_BLOCK)
    rel = DIL_BLOCK - x
    valid = (rel >= 0) & (rel <= back)
    bucket = _t5_bucket_np(np.clip(rel, 0, back) * dil)
    sel = np.zeros((LANES, 2 * DIL_BLOCK), np.float32)
    sel[bucket[valid], x[valid]] = 1.0
    mrow = np.where(valid, 0.0, NEG_INF).astype(np.float32)[None]
    return jnp.asarray(sel, BF16), jnp.asarray(mrow)


def _dilated(src, rb_hi, rb_lo, batch, seq, window, dil):
    sub_len = seq // dil
    nb = sub_len // DIL_BLOCK
    sel, mrow = _bias_selector(window, dil)
    nu = DIL_UNITS[dil]
    consts = (rb_hi, rb_lo, sel, mrow)
    const = lambda a: pl.BlockSpec(a.shape, lambda *_: (0, 0))
    stat_shape = jax.ShapeDtypeStruct((batch * seq, LANES), F32)
    out_shape = [jax.ShapeDtypeStruct((batch * seq, D_DIL), BF16), stat_shape]
    bias_scratch = pltpu.VMEM((2, N_HEADS_DIL, DIL_BLOCK, 2 * DIL_BLOCK), F32)

    if dil == 1:
        rows = nu * DIL_BLOCK
        steps = nb // nu
        cur = lambda col: pl.BlockSpec((None, None, rows, D_DIL), lambda b, l: (b, 0, l, col))
        prev = lambda col: pl.BlockSpec((None, None, DIL_BLOCK, D_DIL),
                                        lambda b, l: (b, 0, jnp.maximum(nu * l - 1, 0), col))
        out = lambda width: pl.BlockSpec((rows, width), lambda b, l: (b * steps + l, 0))
        return pl.pallas_call(
            _dil1_kernel,
            grid=(batch, steps),
            in_specs=[cur(0), prev(1), cur(1), prev(2), cur(2)] + [const(a) for a in consts],
            out_specs=[out(D_DIL), out(LANES)],
            out_shape=out_shape,
            scratch_shapes=[bias_scratch],
            compiler_params=_cparams(("arbitrary", "arbitrary")),
            name="dilated_d1",
        )(src, src, src, src, src, *consts)

    nl = DIL_LBLOCKS[dil]
    steps = nb // nl
    rows = nl * DIL_BLOCK * dil
    cur = lambda col: pl.BlockSpec((None, nu, nl * DIL_BLOCK, D_DIL), lambda b, l, g: (b, g, l, col))
    prev = lambda col: pl.BlockSpec((None, nu, DIL_BLOCK, D_DIL),
                                    lambda b, l, g: (b, g, jnp.maximum(nl * l - 1, 0), col))
    out = lambda width: pl.BlockSpec((rows, width), lambda b, l, g: (b * steps + l, 0))
    return pl.pallas_call(
        functools.partial(_dil_kernel, dil=dil),
        grid=(batch, steps, dil // nu),
        in_specs=[cur(0), prev(1), cur(1), prev(2), cur(2)] + [const(a) for a in consts],
        out_specs=[out(D_DIL), out(LANES)],
        out_shape=out_shape,
        scratch_shapes=[bias_scratch, pltpu.VMEM((N_HEADS_DIL // 2, rows, LANES), F32)],
        compiler_params=_cparams(("arbitrary", "arbitrary", "arbitrary")),
        name=f"dilated_d{dil}",
    )(src, src, src, src, src, *consts)


GLA_TM = 256
GLA_STEP_TILES = 2


GLA_NC = GLA_TM // GLA_CHUNK


def _gla_tile(bb, rows, q_ref, k_ref, v_ref, g_ref, w2h_ref, w2l_ref, bg_ref, nw_ref, tri_ref, o_ref, st_ref):
    g = g_ref[bb, rows, :LANES]
    z = (jnp.dot(g, w2h_ref[...], preferred_element_type=F32)
         + jnp.dot(g, w2l_ref[...], preferred_element_type=F32) + bg_ref[...])
    la = (jnp.minimum(z, 0.0) * (LOG2E / GLA_TAU)
          - jnp.log2(1.0 + jnp.exp2(jnp.abs(z) * -LOG2E)) * (1.0 / GLA_TAU))
    la_h = la.astype(BF16)
    la_l = (la - la_h.astype(F32)).astype(BF16)

    tri = tri_ref[...]
    b = jnp.dot(tri, la_h, preferred_element_type=F32) + jnp.dot(tri, la_l, preferred_element_type=F32)
    b_last = jnp.concatenate(
        [jnp.broadcast_to(b[(ci + 1) * GLA_CHUNK - 1:(ci + 1) * GLA_CHUNK, :], (GLA_CHUNK, D_GLA_K))
         for ci in range(GLA_NC)], axis=0)

    q = q_ref[bb, rows].astype(F32)
    k = k_ref[bb, rows].astype(F32)
    q_t = (q * jnp.exp2(b)).astype(BF16)
    k_t = (k * jnp.exp2(-b)).astype(BF16)
    k_s = k * jnp.exp2(b_last - b)

    causal = tri.astype(F32) > 0.0
    t_chunk = lax.broadcasted_iota(jnp.int32, (GLA_DK, GLA_TM), 1) // GLA_CHUNK
    nw = nw_ref[...]
    for h in range(N_HEADS_GLA):
        cs = slice(h * GLA_DK, (h + 1) * GLA_DK)
        vs = slice(h * GLA_DV, (h + 1) * GLA_DV)
        v_h = v_ref[bb, rows, vs]
        a = lax.dot_general(q_t[:, cs], k_t[:, cs], (((1,), (1,)), ((), ())), preferred_element_type=F32)
        o = jnp.dot(jnp.where(causal, a, 0.0).astype(BF16), v_h, preferred_element_type=F32)
        k_s_t = k_s[:, cs].T.astype(BF16)
        zero = jnp.zeros_like(k_s_t)
        kv_all = jnp.dot(jnp.concatenate([jnp.where(t_chunk == ci, k_s_t, zero) for ci in range(GLA_NC)], axis=0),
                         v_h, preferred_element_type=F32)
        decay_t = jnp.exp2(b_last[:, cs].T)
        st = st_ref[bb, h]
        outs = []
        for ci in range(GLA_NC):
            rs = slice(ci * GLA_CHUNK, (ci + 1) * GLA_CHUNK)
            o_c = o[rs] + jnp.dot(q_t[rs, cs], st.astype(BF16), preferred_element_type=F32)
            decay = jnp.broadcast_to(decay_t[:, ci * GLA_CHUNK:ci * GLA_CHUNK + 1], (GLA_DK, GLA_DV))
            st = decay * st + kv_all[ci * GLA_DK:(ci + 1) * GLA_DK]
            ms = jnp.mean(o_c * o_c, axis=-1, keepdims=True)
            outs.append((o_c * lax.rsqrt(ms + EPS) * nw).astype(o_ref.dtype))
        st_ref[bb, h] = st
        o_ref[bb, rows, vs] = jnp.concatenate(outs, axis=0)


def _gla_kernel(*refs):
    *tile_refs, wout_ref, o_ref, wout_bf_ref, st_ref = refs

    @pl.when(pl.program_id(0) == 0)
    def _():
        st_ref[...] = jnp.zeros_like(st_ref)

    wout_bf_ref[...] = wout_ref[...].astype(wout_bf_ref.dtype)

    for tt in range(GLA_STEP_TILES):
        rows = slice(tt * GLA_TM, (tt + 1) * GLA_TM)
        for bb in range(st_ref.shape[0]):
            _gla_tile(bb, rows, *tile_refs, o_ref, st_ref)


def _gla(proj, w2_hi, w2_lo, b_gate, norm_w, w_out, batch, seq):
    step_rows = GLA_TM * GLA_STEP_TILES
    steps = seq // step_rows
    wout_rows = w_out.shape[0] // steps
    pos = np.arange(GLA_TM)
    tri = jnp.asarray((pos[:, None] // GLA_CHUNK == pos[None, :] // GLA_CHUNK) & (pos[None, :] <= pos[:, None]),
                      BF16)
    proj3 = proj.reshape(batch, seq, REST_W)
    rows = lambda width, col: pl.BlockSpec((batch, step_rows, width), lambda i: (0, i, col // width))
    const = lambda shape: pl.BlockSpec(shape, lambda i: (0, 0))
    wout_spec = pl.BlockSpec((wout_rows, w_out.shape[1]), lambda i: (i, 0))
    out, w_out_bf = pl.pallas_call(
        _gla_kernel,
        grid=(steps,),
        in_specs=[
            rows(D_GLA_K, COL_GQ), rows(D_GLA_K, COL_GK), rows(D_GLA_V, COL_GV), rows(GLR_PAD, COL_GLR),
            const((LANES, D_GLA_K)), const((LANES, D_GLA_K)), const((1, D_GLA_K)), const((1, GLA_DV)),
            const((GLA_TM, GLA_TM)), wout_spec,
        ],
        out_specs=[rows(D_GLA_V, 0), wout_spec],
        out_shape=[jax.ShapeDtypeStruct((batch, seq, D_GLA_V), BF16), jax.ShapeDtypeStruct(w_out.shape, BF16)],
        scratch_shapes=[pltpu.VMEM((batch, N_HEADS_GLA, GLA_DK, GLA_DV), F32)],
        compiler_params=_cparams(("arbitrary",)),
        name="gla",
    )(proj3, proj3, proj3, proj3, w2_hi, w2_lo, b_gate, norm_w, tri, w_out)
    return out.reshape(batch * seq, D_GLA_V), w_out_bf


MEM_TM = 1024
MEM_UNIT = 128


def _mem_kernel(q_ref, mem_ref, mnw_ref, wkv_ref, o_ref, den_ref, kv_ref):
    @pl.when(pl.program_id(1) == 0)
    def _():
        m = mem_ref[0]
        ms = jnp.mean(m * m, axis=-1, keepdims=True)
        mn = (m * lax.rsqrt(ms + EPS) * mnw_ref[...]).astype(BF16)
        kv_ref[...] = jnp.dot(mn, wkv_ref[...].astype(BF16), preferred_element_type=F32).astype(BF16)

    lane = lax.broadcasted_iota(jnp.int32, (MEM_UNIT, LANES), 1)
    lo = lane < HEAD_DIM
    for u in range(MEM_TM // MEM_UNIT):
        rows = slice(u * MEM_UNIT, (u + 1) * MEM_UNIT)
        den_c = jnp.ones((MEM_UNIT, LANES), F32)
        for j in range(N_HEADS_MEM // 2):
            sl = slice(j * LANES, (j + 1) * LANES)
            r_e, r_o, _, _ = _pair_attention(q_ref[rows, sl], kv_ref[:, sl],
                                             kv_ref[:, D_MEM + j * LANES:D_MEM + (j + 1) * LANES], None, None)
            o_ref[rows, sl] = jnp.where(lo, r_e, r_o).astype(o_ref.dtype)
            den_c = jnp.where(lane == _stat_lane(2 * j), r_e, jnp.where(lane == _stat_lane(2 * j + 1), r_o, den_c))
        den_ref[rows, :] = den_c


def _mem_attn(proj, mem, mem_norm_w, w_kv, batch, seq):
    steps = seq // MEM_TM
    rows = lambda width, col=0: pl.BlockSpec((MEM_TM, width), lambda b, i: (b * steps + i, col // width))
    return pl.pallas_call(
        _mem_kernel,
        grid=(batch, steps),
        in_specs=[
            rows(D_MEM, COL_MQ),
            pl.BlockSpec((1, MEM_LEN, D_MODEL), lambda b, i: (b, 0, 0)),
            pl.BlockSpec((1, D_MODEL), lambda b, i: (0, 0)),
            pl.BlockSpec((D_MODEL, 2 * D_MEM), lambda b, i: (0, 0)),
        ],
        out_specs=[rows(D_MEM), rows(LANES)],
        out_shape=[jax.ShapeDtypeStruct((batch * seq, D_MEM), BF16),
                   jax.ShapeDtypeStruct((batch * seq, LANES), F32)],
        scratch_shapes=[pltpu.VMEM((MEM_LEN, 2 * D_MEM), BF16)],
        compiler_params=_cparams(("parallel", "arbitrary")),
        name="mem_attn",
    )(proj, mem, mem_norm_w, w_kv)


OUT_TM = 512


def _silu(g):
    h = g * 0.5
    return h * jnp.tanh(h) + h


OUT_GW = 256


def _out_kernel(n1_ref, n2_ref, n3_ref, s1_ref, s2_ref, s3_ref, og_ref, nm_ref, dm_ref, ga_ref, gg_ref, gm_ref,
                sm_ref, ex_ref, exm_ref, w_ref, x_ref, nw_ref, out_ref, mix_ref):
    keep = sm_ref[0:1, :]
    fill = sm_ref[1:2, :]
    stats = [s_ref[...] for s_ref in (s1_ref, s2_ref, s3_ref)]
    m1, m2, m3 = [st * keep for st in stats]
    d1, d2, d3 = [pltpu.roll(st, LANES - STAT_DEN_SHIFT, 1) * keep + fill for st in stats]
    mx = jnp.maximum(jnp.maximum(m1, m2), m3)
    t1, t2, t3 = jnp.exp2(m1 - mx), jnp.exp2(m2 - mx), jnp.exp2(m3 - mx)
    inv = 1.0 / (t1 * d1 + t2 * d2 + t3 * d3)
    wts = [(t * inv).astype(BF16) for t in (t1, t2, t3)]
    inv_m = (1.0 / dm_ref[...]).astype(BF16)

    def gate_chunk(lo):
        cols = slice(lo, lo + OUT_GW)
        if lo < D_DIL:
            a = jnp.zeros((OUT_TM, OUT_GW), F32)
            for wt, n_ref in zip(wts, (n1_ref, n2_ref, n3_ref)):
                a = a + jnp.dot(wt, ex_ref[:, cols], preferred_element_type=F32) * n_ref[:, cols].astype(F32)
            g = ga_ref[:, cols]
        elif lo < D_DIL + D_GLA_V:
            lc = slice(lo - D_DIL, lo - D_DIL + OUT_GW)
            a, g = og_ref[:, lc], gg_ref[:, lc]
        else:
            a = jnp.dot(inv_m, exm_ref[...], preferred_element_type=F32) * nm_ref[...].astype(F32)
            g = gm_ref[...]
        mix_ref[:, cols] = a.astype(BF16) * _silu(g)

    for lo in range(0, D_MIX, OUT_GW):
        gate_chunk(lo)
    y = jnp.dot(mix_ref[...], w_ref[...], preferred_element_type=F32)
    ms = jnp.mean(y * y, axis=-1, keepdims=True)
    out_ref[...] = x_ref[...] + y * lax.rsqrt(ms + EPS) * nw_ref[...]


def _out_proj(pats, o_gla, mem_pair, proj, stat_mask, expand, expand_mem, w_out, x2d, norm_w):
    m = x2d.shape[0]
    rows = lambda width, col=0: pl.BlockSpec((OUT_TM, width), lambda s: (s, col // width))
    const = lambda shape: pl.BlockSpec(shape, lambda s: (0, 0), pipeline_mode=pl.Buffered(1))
    nums, stats = zip(*pats)
    return pl.pallas_call(
        _out_kernel,
        grid=(m // OUT_TM,),
        in_specs=[
            *([rows(D_DIL)] * 3), *([rows(LANES)] * 3),
            rows(D_GLA_V), rows(D_MEM), rows(LANES),
            rows(D_DIL, COL_GATE_A), rows(D_GLA_V, COL_GATE_G), rows(D_MEM, COL_GATE_M),
            const((8, LANES)), const((LANES, D_DIL)), const((LANES, D_MEM)), const((D_MIX, D_MODEL)),
            rows(D_MODEL), const((1, D_MODEL)),
        ],
        out_specs=rows(D_MODEL),
        out_shape=jax.ShapeDtypeStruct((m, D_MODEL), x2d.dtype),
        scratch_shapes=[pltpu.VMEM((OUT_TM, D_MIX), BF16)],
        compiler_params=_cparams(("parallel",)),
        name="out_proj",
    )(*nums, *stats, o_gla, *mem_pair, proj, proj, proj, stat_mask, expand, expand_mem, w_out, x2d, norm_w)


D_IN_PROJ = 3 * D_DIL + 2 * D_GLA_K + D_GLA_V + GLA_GATE_RANK + D_MEM + D_MIX
SRC_GLR = 3 * D_DIL + 2 * D_GLA_K + D_GLA_V
SRC_MQ = SRC_GLR + GLA_GATE_RANK
SRC_GATE = SRC_MQ + D_MEM
REST_SEGMENTS = (
    (QKV_W, 2 * D_GLA_K + D_GLA_V, COL_GQ),
    (SRC_GATE + D_DIL, D_GLA_V, COL_GATE_G),
    (SRC_MQ, D_MEM, COL_MQ),
    (SRC_GATE + D_DIL + D_GLA_V, D_MEM, COL_GATE_M),
    (SRC_GATE, D_DIL, COL_GATE_A),
)
WPREP_TK = 512


def _wprep_kernel(wt_ref, qkv_ref, rest_ref):
    def put(o_ref, src, width, dst):
        for c in range(width // LANES):
            lo = c * LANES
            o_ref[:, dst + lo:dst + lo + LANES] = wt_ref[src + lo:src + lo + LANES, :].T.astype(o_ref.dtype)

    put(qkv_ref, 0, QKV_W, 0)
    for src, width, dst in REST_SEGMENTS:
        put(rest_ref, src, width, dst)
    glr = wt_ref[SRC_GLR:SRC_GLR + LANES, :].T
    keep = lax.broadcasted_iota(jnp.int32, glr.shape, 1) < GLA_GATE_RANK
    rest_ref[:, COL_GLR:COL_GLR + LANES] = jnp.where(keep, glr, 0.0).astype(rest_ref.dtype)
    rest_ref[:, COL_GLR + LANES:COL_GLR + GLR_PAD] = jnp.zeros((WPREP_TK, GLR_PAD - LANES), rest_ref.dtype)


def _regroup_in_weights(w_in):
    w_qkv, w_rest = pl.pallas_call(
        _wprep_kernel,
        grid=(D_MODEL // WPREP_TK,),
        in_specs=[pl.BlockSpec((D_IN_PROJ, WPREP_TK), lambda i: (0, i))],
        out_specs=[pl.BlockSpec((WPREP_TK, QKV_W), lambda i: (i, 0)),
                   pl.BlockSpec((WPREP_TK, REST_W), lambda i: (i, 0))],
        out_shape=[jax.ShapeDtypeStruct((D_MODEL, QKV_W), BF16), jax.ShapeDtypeStruct((D_MODEL, REST_W), BF16)],
        compiler_params=_cparams(("parallel",)),
        name="weight_regroup",
    )(w_in.T)
    scale_qkv = np.ones((1, QKV_W), np.float32)
    scale_qkv[:, :D_DIL] = HEAD_DIM ** -0.5 * LOG2E
    scale_rest = np.ones((1, REST_W), np.float32)
    scale_rest[:, COL_GQ:COL_GQ + D_GLA_K] = GLA_DK ** -0.5
    scale_rest[:, COL_MQ:COL_MQ + D_MEM] = HEAD_DIM ** -0.5 * LOG2E
    return w_qkv, w_rest, jnp.asarray(scale_qkv), jnp.asarray(scale_rest)


def _split_hi_lo(w):
    hi = w.astype(BF16)
    return hi, (w - hi.astype(F32)).astype(BF16)


def kernel(x, mem, norm_pre_w, w_in, rel_bias, w_gla_gate2, b_gla_gate, gla_norm_w, mem_norm_w, w_mem_kv,
           w_out, norm_post_w):
    batch, seq, _ = x.shape
    depth = w_in.shape[0]
    expand_np = np.zeros((LANES, D_DIL), np.float32)
    for h in range(N_HEADS_DIL):
        expand_np[_stat_lane(h), h * HEAD_DIM:(h + 1) * HEAD_DIM] = 1.0
    expand = jnp.asarray(expand_np, BF16)
    expand_mem = jnp.asarray(expand_np[:, :D_MEM], BF16)
    stat_mask_np = np.zeros((8, LANES), np.float32)
    stat_mask_np[0, [_stat_lane(h) for h in range(N_HEADS_DIL)]] = 1.0
    stat_mask_np[1] = 1.0 - stat_mask_np[0]
    stat_mask = jnp.asarray(stat_mask_np)
    rb_t = jnp.zeros((16, LANES), F32).at[:N_HEADS_DIL, :REL_BUCKETS].set(rel_bias.astype(F32).T)
    rb_hi, rb_lo = _split_hi_lo(rb_t)
    for l in range(depth):
        x2d = x.reshape(batch * seq, D_MODEL)
        w_qkv, w_rest, scale_qkv, scale_rest = _regroup_in_weights(w_in[l])
        h, qkv1, qkv4, qkv16 = _qkv_proj(x2d, norm_pre_w[l][None], w_qkv, scale_qkv, batch, seq)
        rest = _rest_proj(h, w_rest, scale_rest)

        srcs = (qkv1.reshape(batch, 1, seq, QKV_W), qkv4, qkv16)
        pats = [_dilated(src, rb_hi, rb_lo, batch, seq, window, dil)
                for (window, dil), src in zip(DIL_PATTERNS, srcs)]

        w2 = jnp.zeros((LANES, D_GLA_K), F32).at[:GLA_GATE_RANK].set(w_gla_gate2[l])
        w2_hi, w2_lo = _split_hi_lo(w2)
        o_gla, w_out_bf = _gla(rest, w2_hi, w2_lo, b_gla_gate[l][None], gla_norm_w[l][None], w_out[l],
                               batch, seq)

        mem_pair = _mem_attn(rest, mem, mem_norm_w[l][None], w_mem_kv[l], batch, seq)

        out = _out_proj(pats, o_gla, mem_pair, rest, stat_mask, expand, expand_mem, w_out_bf, x2d,
                        norm_post_w[l][None])
        x = out.reshape(batch, seq, D_MODEL)
    return x
```

```python
import functools
import math

import numpy as np
import jax
import jax.numpy as jnp
from jax import lax
from jax.experimental import pallas as pl
from jax.experimental.pallas import tpu as pltpu

F32 = jnp.float32
BF16 = jnp.bfloat16

D_MODEL = 2048
HEAD_DIM = 64
N_HEADS_DIL = 12
DIL_PATTERNS = ((128, 1), (512, 4), (2048, 16))
DIL_BLOCK = 128
N_HEADS_GLA = 4
GLA_DK = 128
GLA_DV = 256
GLA_GATE_RANK = 16
GLA_TAU = 16.0
GLA_CHUNK = 64
N_HEADS_MEM = 4
MEM_LEN = 256
REL_BUCKETS = 32
REL_MAX_DIST = 2048
EPS = 1e-6
NEG_INF = -1e30

D_DIL = N_HEADS_DIL * HEAD_DIM
D_GLA_K = N_HEADS_GLA * GLA_DK
D_GLA_V = N_HEADS_GLA * GLA_DV
D_MEM = N_HEADS_MEM * HEAD_DIM
D_MIX = D_DIL + D_GLA_V + D_MEM

LANES = 128
GLR_PAD = 256

QKV_W = 3 * D_DIL
COL_GQ = 0
COL_GK = COL_GQ + D_GLA_K
COL_GV = COL_GK + D_GLA_K
COL_GATE_G = COL_GV + D_GLA_V
COL_MQ = COL_GATE_G + D_GLA_V
COL_GATE_M = COL_MQ + D_MEM
COL_GLR = COL_GATE_M + D_MEM
COL_GATE_A = COL_GLR + GLR_PAD
REST_W = COL_GATE_A + D_DIL

VMEM_LIMIT = 56 * 1024 * 1024


def _cparams(sem):
    return pltpu.CompilerParams(dimension_semantics=sem, vmem_limit_bytes=VMEM_LIMIT)


QKV_TM = 512
N_SLABS = D_DIL // LANES
REGROUP_DILS = (4, 16)


def _qkv_proj_kernel(x_ref, nw_ref, w_ref, sc_ref, h_ref, o_ref, x4_ref, x16_ref, slab_ref):
    x = x_ref[...]
    ms = jnp.mean(x * x, axis=-1, keepdims=True)
    h_ref[...] = (x * lax.rsqrt(ms + EPS) * nw_ref[...]).astype(BF16)
    for t in range(QKV_W // D_DIL):
        cols = slice(t * D_DIL, (t + 1) * D_DIL)
        res = jnp.dot(h_ref[...], w_ref[:, cols], preferred_element_type=F32) * sc_ref[:, cols]
        o_ref[:, cols] = res.astype(o_ref.dtype)
        for s in range(N_SLABS):
            slab_ref[t, s] = res[:, s * LANES:(s + 1) * LANES]
        for out_ref, dil in zip((x4_ref, x16_ref), REGROUP_DILS):
            for r in range(dil):
                for s in range(N_SLABS):
                    lo = t * D_DIL + s * LANES
                    out_ref[0, r, :, lo:lo + LANES] = (
                        slab_ref[t, s, pl.ds(r, QKV_TM // dil, stride=dil), :].astype(out_ref.dtype))


def _qkv_proj(x2d, norm_w, w_qkv, scale, batch, seq):
    m = x2d.shape[0]
    tiles_per_seq = seq // QKV_TM
    const = lambda shape: pl.BlockSpec(shape, lambda i: (0, 0), pipeline_mode=pl.Buffered(1))
    grouped = lambda dil: pl.BlockSpec((1, dil, QKV_TM // dil, QKV_W),
                                       lambda i: (i // tiles_per_seq, 0, i % tiles_per_seq, 0))
    return pl.pallas_call(
        _qkv_proj_kernel,
        grid=(m // QKV_TM,),
        in_specs=[
            pl.BlockSpec((QKV_TM, D_MODEL), lambda i: (i, 0)),
            const((1, D_MODEL)), const((D_MODEL, QKV_W)), const((1, QKV_W)),
        ],
        out_specs=[pl.BlockSpec((QKV_TM, D_MODEL), lambda i: (i, 0)),
                   pl.BlockSpec((QKV_TM, QKV_W), lambda i: (i, 0))] + [grouped(d) for d in REGROUP_DILS],
        out_shape=[jax.ShapeDtypeStruct((m, D_MODEL), BF16), jax.ShapeDtypeStruct((m, QKV_W), BF16)]
        + [jax.ShapeDtypeStruct((batch, d, seq // d, QKV_W), BF16) for d in REGROUP_DILS],
        scratch_shapes=[pltpu.VMEM((QKV_W // D_DIL, N_SLABS, QKV_TM, LANES), F32)],
        compiler_params=_cparams(("parallel",)),
        name="qkv_proj",
    )(x2d, norm_w, w_qkv, scale)


REST_TM = 1024
REST_TN = 2304


def _rest_proj_kernel(h_ref, w_ref, sc_ref, o_ref):
    acc = jnp.dot(h_ref[...], w_ref[...], preferred_element_type=F32)
    o_ref[...] = (acc * sc_ref[...]).astype(o_ref.dtype)


def _rest_proj(h, w_rest, scale):
    m = h.shape[0]
    return pl.pallas_call(
        _rest_proj_kernel,
        grid=(m // REST_TM, REST_W // REST_TN),
        in_specs=[
            pl.BlockSpec((REST_TM, D_MODEL), lambda i, j: (i, 0)),
            pl.BlockSpec((D_MODEL, REST_TN), lambda i, j: (0, j)),
            pl.BlockSpec((1, REST_TN), lambda i, j: (0, j)),
        ],
        out_specs=pl.BlockSpec((REST_TM, REST_TN), lambda i, j: (i, j)),
        out_shape=jax.ShapeDtypeStruct((m, REST_W), BF16),
        compiler_params=_cparams(("parallel", "arbitrary")),
        name="rest_proj",
    )(h, w_rest, scale)


LOG2E = math.log2(math.e)


def _pair_attention(q_pair, k_pair, v_pair, bias_e, bias_o):
    rows = q_pair.shape[0]
    lo_q = lax.broadcasted_iota(jnp.int32, q_pair.shape, 1) < HEAD_DIM
    zero = jnp.zeros_like(q_pair)
    q_both = jnp.concatenate([jnp.where(lo_q, q_pair, zero), jnp.where(lo_q, zero, q_pair)], axis=0)
    s = lax.dot_general(q_both, k_pair, (((1,), (1,)), ((), ())), preferred_element_type=F32)
    s_e, s_o = s[:rows], s[rows:]
    if bias_e is not None:
        s_e = s_e + bias_e
        s_o = s_o + bias_o
    mx_e = jnp.max(s_e, axis=-1, keepdims=True)
    mx_o = jnp.max(s_o, axis=-1, keepdims=True)
    p_e = jnp.exp2(s_e - mx_e).astype(BF16)
    p_o = jnp.exp2(s_o - mx_o).astype(BF16)
    lo_v = lax.broadcasted_iota(jnp.int32, v_pair.shape, 1) < HEAD_DIM
    one = jnp.ones_like(v_pair)
    r_e = jnp.dot(p_e, jnp.where(lo_v, v_pair, one), preferred_element_type=F32)
    r_o = jnp.dot(p_o, jnp.where(lo_v, one, v_pair), preferred_element_type=F32)
    return r_e, r_o, mx_e, mx_o


def _stat_lane(head):
    return head + HEAD_DIM if head % 2 == 0 else head


STAT_DEN_SHIFT = 16


DIL_UNITS = {1: 8, 4: 4, 16: 8}


def _dil_init_bias(rbh_ref, rbl_ref, sel_ref, mrow_ref, bias_ref):
    f = (jnp.dot(rbh_ref[...], sel_ref[...], preferred_element_type=F32)
         + jnp.dot(rbl_ref[...], sel_ref[...], preferred_element_type=F32)) * LOG2E + mrow_ref[...]
    col = lax.broadcasted_iota(jnp.int32, (DIL_BLOCK, 2 * DIL_BLOCK), 1)
    for h in range(N_HEADS_DIL):
        row = jnp.broadcast_to(f[h:h + 1, :], (DIL_BLOCK, 2 * DIL_BLOCK))
        tab = pltpu.roll(row, 0, 1, stride=1, stride_axis=0)
        bias_ref[0, h] = tab
        bias_ref[1, h] = jnp.where(col >= DIL_BLOCK, tab, NEG_INF)


def _dil_unit(q, kp, kc, vp, vc, first, bias_ref, put_num):
    lane = lax.broadcasted_iota(jnp.int32, (DIL_BLOCK, LANES), 1)
    lo = lane < HEAD_DIM
    st_c = jnp.zeros((DIL_BLOCK, LANES), F32)
    for j in range(N_HEADS_DIL // 2):
        sl = slice(j * LANES, (j + 1) * LANES)
        k_pair = jnp.concatenate([kp(sl), kc(sl)], axis=0)
        v_pair = jnp.concatenate([vp(sl), vc(sl)], axis=0)
        r_e, r_o, mx_e, mx_o = _pair_attention(q(sl), k_pair, v_pair,
                                               bias_ref[first, 2 * j], bias_ref[first, 2 * j + 1])
        put_num(j, jnp.where(lo, r_e, r_o))
        l_e, l_o = _stat_lane(2 * j), _stat_lane(2 * j + 1)
        st_c = jnp.where(lane == l_e, mx_e, jnp.where(lane == l_e + STAT_DEN_SHIFT, r_e, st_c))
        st_c = jnp.where(lane == l_o, mx_o, jnp.where(lane == l_o + STAT_DEN_SHIFT, r_o, st_c))
    return st_c


def _dil1_kernel(q_ref, kp_ref, kc_ref, vp_ref, vc_ref, rbh_ref, rbl_ref, sel_ref, mrow_ref,
                 o_ref, st_ref, bias_ref):
    @pl.when((pl.program_id(0) == 0) & (pl.program_id(1) == 0))
    def _():
        _dil_init_bias(rbh_ref, rbl_ref, sel_ref, mrow_ref, bias_ref)

    for u in range(DIL_UNITS[1]):
        rows = slice(u * DIL_BLOCK, (u + 1) * DIL_BLOCK)
        prev = slice((u - 1) * DIL_BLOCK, u * DIL_BLOCK)
        if u == 0:
            first = (pl.program_id(1) == 0).astype(jnp.int32)
            kp, vp = (lambda sl: kp_ref[:, sl]), (lambda sl: vp_ref[:, sl])
        else:
            first = 0
            kp, vp = (lambda sl, p=prev: kc_ref[p, sl]), (lambda sl, p=prev: vc_ref[p, sl])

        def put_num(j, num, rows=rows):
            o_ref[rows, j * LANES:(j + 1) * LANES] = num.astype(o_ref.dtype)

        st_ref[rows, :] = _dil_unit(lambda sl, r=rows: q_ref[r, sl], kp, lambda sl, r=rows: kc_ref[r, sl],
                                    vp, lambda sl, r=rows: vc_ref[r, sl], first, bias_ref, put_num)


DIL_LBLOCKS = {4: 2, 16: 2}


def _dil_kernel(q_ref, kp_ref, kc_ref, vp_ref, vc_ref, rbh_ref, rbl_ref, sel_ref, mrow_ref,
                o_ref, st_ref, bias_ref, acc_ref, *, dil):
    b, lb, g = pl.program_id(0), pl.program_id(1), pl.program_id(2)
    nr, nl = DIL_UNITS[dil], DIL_LBLOCKS[dil]

    @pl.when((b == 0) & (lb == 0) & (g == 0))
    def _():
        _dil_init_bias(rbh_ref, rbl_ref, sel_ref, mrow_ref, bias_ref)

    for ll in range(nl):
        sub = slice(ll * DIL_BLOCK, (ll + 1) * DIL_BLOCK)
        prev = slice((ll - 1) * DIL_BLOCK, ll * DIL_BLOCK)
        first = (lb == 0).astype(jnp.int32) if ll == 0 else 0
        for u in range(nr):
            rows = pl.ds(ll * DIL_BLOCK * dil + g * nr + u, DIL_BLOCK, stride=dil)
            if ll == 0:
                kp, vp = (lambda sl, u=u: kp_ref[u, :, sl]), (lambda sl, u=u: vp_ref[u, :, sl])
            else:
                kp = lambda sl, u=u, p=prev: kc_ref[u, p, sl]
                vp = lambda sl, u=u, p=prev: vc_ref[u, p, sl]

            def put_num(j, num, rows=rows):
                acc_ref[j, rows, :] = num

            st_ref[rows, :] = _dil_unit(lambda sl, u=u, r=sub: q_ref[u, r, sl], kp,
                                        lambda sl, u=u, r=sub: kc_ref[u, r, sl], vp,
                                        lambda sl, u=u, r=sub: vc_ref[u, r, sl], first, bias_ref, put_num)

    @pl.when(g == dil // nr - 1)
    def _():
        for j in range(N_HEADS_DIL // 2):
            o_ref[:, j * LANES:(j + 1) * LANES] = acc_ref[j].astype(o_ref.dtype)


def _t5_bucket_np(dist):
    max_exact = REL_BUCKETS // 2
    n = np.maximum(dist, 1).astype(np.float32)
    large = max_exact + (np.log(n / np.float32(max_exact)) / np.float32(math.log(REL_MAX_DIST / max_exact))
                         * np.float32(REL_BUCKETS - max_exact)).astype(np.int32)
    large = np.minimum(large, REL_BUCKETS - 1)
    return np.where(dist < max_exact, dist, large)


def _bias_selector(window, dil):
    back = window // dil
    x = np.arange(2 * DIL_BLOCK)
    rel = DIL_BLOCK - x
    valid = (rel >= 0) & (rel <= back)
    bucket = _t5_bucket_np(np.clip(rel, 0, back) * dil)
    sel = np.zeros((LANES, 2 * DIL_BLOCK), np.float32)
    sel[bucket[valid], x[valid]] = 1.0
    mrow = np.where(valid, 0.0, NEG_INF).astype(np.float32)[None]
    return jnp.asarray(sel, BF16), jnp.asarray(mrow)


def _dilated(src, rb_hi, rb_lo, batch, seq, window, dil):
    sub_len = seq // dil
    nb = sub_len // DIL_BLOCK
    sel, mrow = _bias_selector(window, dil)
    nu = DIL_UNITS[dil]
    consts = (rb_hi, rb_lo, sel, mrow)
    const = lambda a: pl.BlockSpec(a.shape, lambda *_: (0, 0))
    stat_shape = jax.ShapeDtypeStruct((batch * seq, LANES), F32)
    out_shape = [jax.ShapeDtypeStruct((batch * seq, D_DIL), BF16), stat_shape]
    bias_scratch = pltpu.VMEM((2, N_HEADS_DIL, DIL_BLOCK, 2 * DIL_BLOCK), F32)

    if dil == 1:
        rows = nu * DIL_BLOCK
        steps = nb // nu
        cur = lambda col: pl.BlockSpec((None, None, rows, D_DIL), lambda b, l: (b, 0, l, col))
        prev = lambda col: pl.BlockSpec((None, None, DIL_BLOCK, D_DIL),
                                        lambda b, l: (b, 0, jnp.maximum(nu * l - 1, 0), col))
        out = lambda width: pl.BlockSpec((rows, width), lambda b, l: (b * steps + l, 0))
        return pl.pallas_call(
            _dil1_kernel,
            grid=(batch, steps),
            in_specs=[cur(0), prev(1), cur(1), prev(2), cur(2)] + [const(a) for a in consts],
            out_specs=[out(D_DIL), out(LANES)],
            out_shape=out_shape,
            scratch_shapes=[bias_scratch],
            compiler_params=_cparams(("arbitrary", "arbitrary")),
            name="dilated_d1",
        )(src, src, src, src, src, *consts)

    nl = DIL_LBLOCKS[dil]
    steps = nb // nl
    rows = nl * DIL_BLOCK * dil
    cur = lambda col: pl.BlockSpec((None, nu, nl * DIL_BLOCK, D_DIL), lambda b, l, g: (b, g, l, col))
    prev = lambda col: pl.BlockSpec((None, nu, DIL_BLOCK, D_DIL),
                                    lambda b, l, g: (b, g, jnp.maximum(nl * l - 1, 0), col))
    out = lambda width: pl.BlockSpec((rows, width), lambda b, l, g: (b * steps + l, 0))
    return pl.pallas_call(
        functools.partial(_dil_kernel, dil=dil),
        grid=(batch, steps, dil // nu),
        in_specs=[cur(0), prev(1), cur(1), prev(2), cur(2)] + [const(a) for a in consts],
        out_specs=[out(D_DIL), out(LANES)],
        out_shape=out_shape,
        scratch_shapes=[bias_scratch, pltpu.VMEM((N_HEADS_DIL // 2, rows, LANES), F32)],
        compiler_params=_cparams(("arbitrary", "arbitrary", "arbitrary")),
        name=f"dilated_d{dil}",
    )(src, src, src, src, src, *consts)


GLA_TM = 256
GLA_STEP_TILES = 2


GLA_NC = GLA_TM // GLA_CHUNK


def _gla_tile(bb, rows, q_ref, k_ref, v_ref, g_ref, w2h_ref, w2l_ref, bg_ref, nw_ref, tri_ref, o_ref, st_ref):
    g = g_ref[bb, rows, :LANES]
    z = (jnp.dot(g, w2h_ref[...], preferred_element_type=F32)
         + jnp.dot(g, w2l_ref[...], preferred_element_type=F32) + bg_ref[...])
    la = (jnp.minimum(z, 0.0) * (LOG2E / GLA_TAU)
          - jnp.log2(1.0 + jnp.exp2(jnp.abs(z) * -LOG2E)) * (1.0 / GLA_TAU))
    la_h = la.astype(BF16)
    la_l = (la - la_h.astype(F32)).astype(BF16)

    tri = tri_ref[...]
    b = jnp.dot(tri, la_h, preferred_element_type=F32) + jnp.dot(tri, la_l, preferred_element_type=F32)
    b_last = jnp.concatenate(
        [jnp.broadcast_to(b[(ci + 1) * GLA_CHUNK - 1:(ci + 1) * GLA_CHUNK, :], (GLA_CHUNK, D_GLA_K))
         for ci in range(GLA_NC)], axis=0)

    q = q_ref[bb, rows].astype(F32)
    k = k_ref[bb, rows].astype(F32)
    q_t = (q * jnp.exp2(b)).astype(BF16)
    k_t = (k * jnp.exp2(-b)).astype(BF16)
    k_s = k * jnp.exp2(b_last - b)

    causal = tri.astype(F32) > 0.0
    t_chunk = lax.broadcasted_iota(jnp.int32, (GLA_DK, GLA_TM), 1) // GLA_CHUNK
    nw = nw_ref[...]
    for h in range(N_HEADS_GLA):
        cs = slice(h * GLA_DK, (h + 1) * GLA_DK)
        vs = slice(h * GLA_DV, (h + 1) * GLA_DV)
        v_h = v_ref[bb, rows, vs]
        a = lax.dot_general(q_t[:, cs], k_t[:, cs], (((1,), (1,)), ((), ())), preferred_element_type=F32)
        o = jnp.dot(jnp.where(causal, a, 0.0).astype(BF16), v_h, preferred_element_type=F32)
        k_s_t = k_s[:, cs].T.astype(BF16)
        zero = jnp.zeros_like(k_s_t)
        kv_all = jnp.dot(jnp.concatenate([jnp.where(t_chunk == ci, k_s_t, zero) for ci in range(GLA_NC)], axis=0),
                         v_h, preferred_element_type=F32)
        decay_t = jnp.exp2(b_last[:, cs].T)
        st = st_ref[bb, h]
        outs = []
        for ci in range(GLA_NC):
            rs = slice(ci * GLA_CHUNK, (ci + 1) * GLA_CHUNK)
            o_c = o[rs] + jnp.dot(q_t[rs, cs], st.astype(BF16), preferred_element_type=F32)
            decay = jnp.broadcast_to(decay_t[:, ci * GLA_CHUNK:ci * GLA_CHUNK + 1], (GLA_DK, GLA_DV))
            st = decay * st + kv_all[ci * GLA_DK:(ci + 1) * GLA_DK]
            ms = jnp.mean(o_c * o_c, axis=-1, keepdims=True)
            outs.append((o_c * lax.rsqrt(ms + EPS) * nw).astype(o_ref.dtype))
        st_ref[bb, h] = st
        o_ref[bb, rows, vs] = jnp.concatenate(outs, axis=0)


def _gla_kernel(*refs):
    *tile_refs, wout_ref, o_ref, wout_bf_ref, st_ref = refs

    @pl.when(pl.program_id(0) == 0)
    def _():
        st_ref[...] = jnp.zeros_like(st_ref)

    wout_bf_ref[...] = wout_ref[...].astype(wout_bf_ref.dtype)

    for tt in range(GLA_STEP_TILES):
        rows = slice(tt * GLA_TM, (tt + 1) * GLA_TM)
        for bb in range(st_ref.shape[0]):
            _gla_tile(bb, rows, *tile_refs, o_ref, st_ref)


def _gla(proj, w2_hi, w2_lo, b_gate, norm_w, w_out, batch, seq):
    step_rows = GLA_TM * GLA_STEP_TILES
    steps = seq // step_rows
    wout_rows = w_out.shape[0] // steps
    pos = np.arange(GLA_TM)
    tri = jnp.asarray((pos[:, None] // GLA_CHUNK == pos[None, :] // GLA_CHUNK) & (pos[None, :] <= pos[:, None]),
                      BF16)
    proj3 = proj.reshape(batch, seq, REST_W)
    rows = lambda width, col: pl.BlockSpec((batch, step_rows, width), lambda i: (0, i, col // width))
    const = lambda shape: pl.BlockSpec(shape, lambda i: (0, 0))
    wout_spec = pl.BlockSpec((wout_rows, w_out.shape[1]), lambda i: (i, 0))
    out, w_out_bf = pl.pallas_call(
        _gla_kernel,
        grid=(steps,),
        in_specs=[
            rows(D_GLA_K, COL_GQ), rows(D_GLA_K, COL_GK), rows(D_GLA_V, COL_GV), rows(GLR_PAD, COL_GLR),
            const((LANES, D_GLA_K)), const((LANES, D_GLA_K)), const((1, D_GLA_K)), const((1, GLA_DV)),
            const((GLA_TM, GLA_TM)), wout_spec,
        ],
        out_specs=[rows(D_GLA_V, 0), wout_spec],
        out_shape=[jax.ShapeDtypeStruct((batch, seq, D_GLA_V), BF16), jax.ShapeDtypeStruct(w_out.shape, BF16)],
        scratch_shapes=[pltpu.VMEM((batch, N_HEADS_GLA, GLA_DK, GLA_DV), F32)],
        compiler_params=_cparams(("arbitrary",)),
        name="gla",
    )(proj3, proj3, proj3, proj3, w2_hi, w2_lo, b_gate, norm_w, tri, w_out)
    return out.reshape(batch * seq, D_GLA_V), w_out_bf


MEM_TM = 2048
MEM_UNIT = 128


def _mem_kernel(q_ref, mem_ref, mnw_ref, wkv_ref, o_ref, den_ref, kv_ref):
    @pl.when(pl.program_id(1) == 0)
    def _():
        m = mem_ref[0]
        ms = jnp.mean(m * m, axis=-1, keepdims=True)
        mn = (m * lax.rsqrt(ms + EPS) * mnw_ref[...]).astype(BF16)
        kv_ref[...] = jnp.dot(mn, wkv_ref[...].astype(BF16), preferred_element_type=F32).astype(BF16)

    lane = lax.broadcasted_iota(jnp.int32, (MEM_UNIT, LANES), 1)
    lo = lane < HEAD_DIM
    for u in range(MEM_TM // MEM_UNIT):
        rows = slice(u * MEM_UNIT, (u + 1) * MEM_UNIT)
        den_c = jnp.ones((MEM_UNIT, LANES), F32)
        for j in range(N_HEADS_MEM // 2):
            sl = slice(j * LANES, (j + 1) * LANES)
            r_e, r_o, _, _ = _pair_attention(q_ref[rows, sl], kv_ref[:, sl],
                                             kv_ref[:, D_MEM + j * LANES:D_MEM + (j + 1) * LANES], None, None)
            o_ref[rows, sl] = jnp.where(lo, r_e, r_o).astype(o_ref.dtype)
            den_c = jnp.where(lane == _stat_lane(2 * j), r_e, jnp.where(lane == _stat_lane(2 * j + 1), r_o, den_c))
        den_ref[rows, :] = den_c


def _mem_attn(proj, mem, mem_norm_w, w_kv, batch, seq):
    steps = seq // MEM_TM
    rows = lambda width, col=0: pl.BlockSpec((MEM_TM, width), lambda b, i: (b * steps + i, col // width))
    return pl.pallas_call(
        _mem_kernel,
        grid=(batch, steps),
        in_specs=[
            rows(D_MEM, COL_MQ),
            pl.BlockSpec((1, MEM_LEN, D_MODEL), lambda b, i: (b, 0, 0)),
            pl.BlockSpec((1, D_MODEL), lambda b, i: (0, 0)),
            pl.BlockSpec((D_MODEL, 2 * D_MEM), lambda b, i: (0, 0)),
        ],
        out_specs=[rows(D_MEM), rows(LANES)],
        out_shape=[jax.ShapeDtypeStruct((batch * seq, D_MEM), BF16),
                   jax.ShapeDtypeStruct((batch * seq, LANES), F32)],
        scratch_shapes=[pltpu.VMEM((MEM_LEN, 2 * D_MEM), BF16)],
        compiler_params=_cparams(("parallel", "arbitrary")),
        name="mem_attn",
    )(proj, mem, mem_norm_w, w_kv)


OUT_TM = 512


def _silu(g):
    h = g * 0.5
    return h * jnp.tanh(h) + h


OUT_GW = 256


def _out_kernel(n1_ref, n2_ref, n3_ref, s1_ref, s2_ref, s3_ref, og_ref, nm_ref, dm_ref, ga_ref, gg_ref, gm_ref,
                sm_ref, ex_ref, exm_ref, w_ref, x_ref, nw_ref, out_ref, mix_ref):
    keep = sm_ref[0:1, :]
    fill = sm_ref[1:2, :]
    stats = [s_ref[...] for s_ref in (s1_ref, s2_ref, s3_ref)]
    m1, m2, m3 = [st * keep for st in stats]
    d1, d2, d3 = [pltpu.roll(st, LANES - STAT_DEN_SHIFT, 1) * keep + fill for st in stats]
    mx = jnp.maximum(jnp.maximum(m1, m2), m3)
    t1, t2, t3 = jnp.exp2(m1 - mx), jnp.exp2(m2 - mx), jnp.exp2(m3 - mx)
    inv = 1.0 / (t1 * d1 + t2 * d2 + t3 * d3)
    wts = [(t * inv).astype(BF16) for t in (t1, t2, t3)]
    inv_m = (1.0 / dm_ref[...]).astype(BF16)

    def gate_chunk(lo):
        cols = slice(lo, lo + OUT_GW)
        if lo < D_DIL:
            a = jnp.zeros((OUT_TM, OUT_GW), F32)
            for wt, n_ref in zip(wts, (n1_ref, n2_ref, n3_ref)):
                a = a + jnp.dot(wt, ex_ref[:, cols], preferred_element_type=F32) * n_ref[:, cols].astype(F32)
            g = ga_ref[:, cols]
        elif lo < D_DIL + D_GLA_V:
            lc = slice(lo - D_DIL, lo - D_DIL + OUT_GW)
            a, g = og_ref[:, lc], gg_ref[:, lc]
        else:
            a = jnp.dot(inv_m, exm_ref[...], preferred_element_type=F32) * nm_ref[...].astype(F32)
            g = gm_ref[...]
        mix_ref[:, cols] = a.astype(BF16) * _silu(g)

    for lo in range(0, D_MIX, OUT_GW):
        gate_chunk(lo)
    y = jnp.dot(mix_ref[...], w_ref[...], preferred_element_type=F32)
    ms = jnp.mean(y * y, axis=-1, keepdims=True)
    out_ref[...] = x_ref[...] + y * lax.rsqrt(ms + EPS) * nw_ref[...]


def _out_proj(pats, o_gla, mem_pair, proj, stat_mask, expand, expand_mem, w_out, x2d, norm_w):
    m = x2d.shape[0]
    rows = lambda width, col=0: pl.BlockSpec((OUT_TM, width), lambda s: (s, col // width))
    const = lambda shape: pl.BlockSpec(shape, lambda s: (0, 0), pipeline_mode=pl.Buffered(1))
    nums, stats = zip(*pats)
    return pl.pallas_call(
        _out_kernel,
        grid=(m // OUT_TM,),
        in_specs=[
            *([rows(D_DIL)] * 3), *([rows(LANES)] * 3),
            rows(D_GLA_V), rows(D_MEM), rows(LANES),
            rows(D_DIL, COL_GATE_A), rows(D_GLA_V, COL_GATE_G), rows(D_MEM, COL_GATE_M),
            const((8, LANES)), const((LANES, D_DIL)), const((LANES, D_MEM)), const((D_MIX, D_MODEL)),
            rows(D_MODEL), const((1, D_MODEL)),
        ],
        out_specs=rows(D_MODEL),
        out_shape=jax.ShapeDtypeStruct((m, D_MODEL), x2d.dtype),
        scratch_shapes=[pltpu.VMEM((OUT_TM, D_MIX), BF16)],
        compiler_params=_cparams(("parallel",)),
        name="out_proj",
    )(*nums, *stats, o_gla, *mem_pair, proj, proj, proj, stat_mask, expand, expand_mem, w_out, x2d, norm_w)


D_IN_PROJ = 3 * D_DIL + 2 * D_GLA_K + D_GLA_V + GLA_GATE_RANK + D_MEM + D_MIX
SRC_GLR = 3 * D_DIL + 2 * D_GLA_K + D_GLA_V
SRC_MQ = SRC_GLR + GLA_GATE_RANK
SRC_GATE = SRC_MQ + D_MEM
REST_SEGMENTS = (
    (QKV_W, 2 * D_GLA_K + D_GLA_V, COL_GQ),
    (SRC_GATE + D_DIL, D_GLA_V, COL_GATE_G),
    (SRC_MQ, D_MEM, COL_MQ),
    (SRC_GATE + D_DIL + D_GLA_V, D_MEM, COL_GATE_M),
    (SRC_GATE, D_DIL, COL_GATE_A),
)
WPREP_TK = 512


def _wprep_kernel(wt_ref, qkv_ref, rest_ref):
    def put(o_ref, src, width, dst):
        for c in range(width // LANES):
            lo = c * LANES
            o_ref[:, dst + lo:dst + lo + LANES] = wt_ref[src + lo:src + lo + LANES, :].T.astype(o_ref.dtype)

    put(qkv_ref, 0, QKV_W, 0)
    for src, width, dst in REST_SEGMENTS:
        put(rest_ref, src, width, dst)
    glr = wt_ref[SRC_GLR:SRC_GLR + LANES, :].T
    keep = lax.broadcasted_iota(jnp.int32, glr.shape, 1) < GLA_GATE_RANK
    rest_ref[:, COL_GLR:COL_GLR + LANES] = jnp.where(keep, glr, 0.0).astype(rest_ref.dtype)
    rest_ref[:, COL_GLR + LANES:COL_GLR + GLR_PAD] = jnp.zeros((WPREP_TK, GLR_PAD - LANES), rest_ref.dtype)


def _regroup_in_weights(w_in):
    w_qkv, w_rest = pl.pallas_call(
        _wprep_kernel,
        grid=(D_MODEL // WPREP_TK,),
        in_specs=[pl.BlockSpec((D_IN_PROJ, WPREP_TK), lambda i: (0, i))],
        out_specs=[pl.BlockSpec((WPREP_TK, QKV_W), lambda i: (i, 0)),
                   pl.BlockSpec((WPREP_TK, REST_W), lambda i: (i, 0))],
        out_shape=[jax.ShapeDtypeStruct((D_MODEL, QKV_W), BF16), jax.ShapeDtypeStruct((D_MODEL, REST_W), BF16)],
        compiler_params=_cparams(("parallel",)),
        name="weight_regroup",
    )(w_in.T)
    scale_qkv = np.ones((1, QKV_W), np.float32)
    scale_qkv[:, :D_DIL] = HEAD_DIM ** -0.5 * LOG2E
    scale_rest = np.ones((1, REST_W), np.float32)
    scale_rest[:, COL_GQ:COL_GQ + D_GLA_K] = GLA_DK ** -0.5
    scale_rest[:, COL_MQ:COL_MQ + D_MEM] = HEAD_DIM ** -0.5 * LOG2E
    return w_qkv, w_rest, jnp.asarray(scale_qkv), jnp.asarray(scale_rest)


def _split_hi_lo(w):
    hi = w.astype(BF16)
    return hi, (w - hi.astype(F32)).astype(BF16)


def kernel(x, mem, norm_pre_w, w_in, rel_bias, w_gla_gate2, b_gla_gate, gla_norm_w, mem_norm_w, w_mem_kv,
           w_out, norm_post_w):
    batch, seq, _ = x.shape
    depth = w_in.shape[0]
    expand_np = np.zeros((LANES, D_DIL), np.float32)
    for h in range(N_HEADS_DIL):
        expand_np[_stat_lane(h), h * HEAD_DIM:(h + 1) * HEAD_DIM] = 1.0
    expand = jnp.asarray(expand_np, BF16)
    expand_mem = jnp.asarray(expand_np[:, :D_MEM], BF16)
    stat_mask_np = np.zeros((8, LANES), np.float32)
    stat_mask_np[0, [_stat_lane(h) for h in range(N_HEADS_DIL)]] = 1.0
    stat_mask_np[1] = 1.0 - stat_mask_np[0]
    stat_mask = jnp.asarray(stat_mask_np)
    rb_t = jnp.zeros((16, LANES), F32).at[:N_HEADS_DIL, :REL_BUCKETS].set(rel_bias.astype(F32).T)
    rb_hi, rb_lo = _split_hi_lo(rb_t)
    for l in range(depth):
        x2d = x.reshape(batch * seq, D_MODEL)
        w_qkv, w_rest, scale_qkv, scale_rest = _regroup_in_weights(w_in[l])
        h, qkv1, qkv4, qkv16 = _qkv_proj(x2d, norm_pre_w[l][None], w_qkv, scale_qkv, batch, seq)
        rest = _rest_proj(h, w_rest, scale_rest)

        srcs = (qkv1.reshape(batch, 1, seq, QKV_W), qkv4, qkv16)
        pats = [_dilated(src, rb_hi, rb_lo, batch, seq, window, dil)
                for (window, dil), src in zip(DIL_PATTERNS, srcs)]

        w2 = jnp.zeros((LANES, D_GLA_K), F32).at[:GLA_GATE_RANK].set(w_gla_gate2[l])
        w2_hi, w2_lo = _split_hi_lo(w2)
        o_gla, w_out_bf = _gla(rest, w2_hi, w2_lo, b_gla_gate[l][None], gla_norm_w[l][None], w_out[l],
                               batch, seq)

        mem_pair = _mem_attn(rest, mem, mem_norm_w[l][None], w_mem_kv[l], batch, seq)

        out = _out_proj(pats, o_gla, mem_pair, rest, stat_mask, expand, expand_mem, w_out_bf, x2d,
                        norm_post_w[l][None])
        x = out.reshape(batch, seq, D_MODEL)
    return x
```

```python
import functools
import math

import numpy as np
import jax
import jax.numpy as jnp
from jax import lax
from jax.experimental import pallas as pl
from jax.experimental.pallas import tpu as pltpu

F32 = jnp.float32
BF16 = jnp.bfloat16

D_MODEL = 2048
HEAD_DIM = 64
N_HEADS_DIL = 12
DIL_PATTERNS = ((128, 1), (512, 4), (2048, 16))
DIL_BLOCK = 128
N_HEADS_GLA = 4
GLA_DK = 128
GLA_DV = 256
GLA_GATE_RANK = 16
GLA_TAU = 16.0
GLA_CHUNK = 64
N_HEADS_MEM = 4
MEM_LEN = 256
REL_BUCKETS = 32
REL_MAX_DIST = 2048
EPS = 1e-6
NEG_INF = -1e30

D_DIL = N_HEADS_DIL * HEAD_DIM
D_GLA_K = N_HEADS_GLA * GLA_DK
D_GLA_V = N_HEADS_GLA * GLA_DV
D_MEM = N_HEADS_MEM * HEAD_DIM
D_MIX = D_DIL + D_GLA_V + D_MEM

LANES = 128
GLR_PAD = 256

QKV_W = 3 * D_DIL
COL_GQ = 0
COL_GK = COL_GQ + D_GLA_K
COL_GV = COL_GK + D_GLA_K
COL_GATE_G = COL_GV + D_GLA_V
COL_MQ = COL_GATE_G + D_GLA_V
COL_GATE_M = COL_MQ + D_MEM
COL_GLR = COL_GATE_M + D_MEM
COL_GATE_A = COL_GLR + GLR_PAD
REST_W = COL_GATE_A + D_DIL

VMEM_LIMIT = 56 * 1024 * 1024


def _cparams(sem):
    return pltpu.CompilerParams(dimension_semantics=sem, vmem_limit_bytes=VMEM_LIMIT)


QKV_TM = 512
N_SLABS = D_DIL // LANES
REGROUP_DILS = (4, 16)


def _qkv_proj_kernel(x_ref, nw_ref, w_ref, sc_ref, h_ref, o_ref, x4_ref, x16_ref, slab_ref):
    x = x_ref[...]
    ms = jnp.mean(x * x, axis=-1, keepdims=True)
    h_ref[...] = (x * lax.rsqrt(ms + EPS) * nw_ref[...]).astype(BF16)
    for t in range(QKV_W // D_DIL):
        cols = slice(t * D_DIL, (t + 1) * D_DIL)
        res = jnp.dot(h_ref[...], w_ref[:, cols], preferred_element_type=F32) * sc_ref[:, cols]
        o_ref[:, cols] = res.astype(o_ref.dtype)
        for s in range(N_SLABS):
            slab_ref[t, s] = res[:, s * LANES:(s + 1) * LANES]
        for out_ref, dil in zip((x4_ref, x16_ref), REGROUP_DILS):
            for r in range(dil):
                for s in range(N_SLABS):
                    lo = t * D_DIL + s * LANES
                    out_ref[0, r, :, lo:lo + LANES] = (
                        slab_ref[t, s, pl.ds(r, QKV_TM // dil, stride=dil), :].astype(out_ref.dtype))


def _qkv_proj(x2d, norm_w, w_qkv, scale, batch, seq):
    m = x2d.shape[0]
    tiles_per_seq = seq // QKV_TM
    const = lambda shape: pl.BlockSpec(shape, lambda i: (0, 0), pipeline_mode=pl.Buffered(1))
    grouped = lambda dil: pl.BlockSpec((1, dil, QKV_TM // dil, QKV_W),
                                       lambda i: (i // tiles_per_seq, 0, i % tiles_per_seq, 0))
    return pl.pallas_call(
        _qkv_proj_kernel,
        grid=(m // QKV_TM,),
        in_specs=[
            pl.BlockSpec((QKV_TM, D_MODEL), lambda i: (i, 0)),
            const((1, D_MODEL)), const((D_MODEL, QKV_W)), const((1, QKV_W)),
        ],
        out_specs=[pl.BlockSpec((QKV_TM, D_MODEL), lambda i: (i, 0)),
                   pl.BlockSpec((QKV_TM, QKV_W), lambda i: (i, 0))] + [grouped(d) for d in REGROUP_DILS],
        out_shape=[jax.ShapeDtypeStruct((m, D_MODEL), BF16), jax.ShapeDtypeStruct((m, QKV_W), BF16)]
        + [jax.ShapeDtypeStruct((batch, d, seq // d, QKV_W), BF16) for d in REGROUP_DILS],
        scratch_shapes=[pltpu.VMEM((QKV_W // D_DIL, N_SLABS, QKV_TM, LANES), F32)],
        compiler_params=_cparams(("parallel",)),
        name="qkv_proj",
    )(x2d, norm_w, w_qkv, scale)


REST_TM = 512


def _rest_proj_kernel(h_ref, w_ref, sc_ref, o_ref):
    acc = jnp.dot(h_ref[...], w_ref[...], preferred_element_type=F32)
    o_ref[...] = (acc * sc_ref[...]).astype(o_ref.dtype)


def _rest_proj(h, w_rest, scale):
    m = h.shape[0]
    const = lambda shape: pl.BlockSpec(shape, lambda i: (0, 0), pipeline_mode=pl.Buffered(1))
    return pl.pallas_call(
        _rest_proj_kernel,
        grid=(m // REST_TM,),
        in_specs=[pl.BlockSpec((REST_TM, D_MODEL), lambda i: (i, 0)), const((D_MODEL, REST_W)), const((1, REST_W))],
        out_specs=pl.BlockSpec((REST_TM, REST_W), lambda i: (i, 0)),
        out_shape=jax.ShapeDtypeStruct((m, REST_W), BF16),
        compiler_params=_cparams(("parallel",)),
        name="rest_proj",
    )(h, w_rest, scale)


LOG2E = math.log2(math.e)


def _pair_attention(q_pair, k_pair, v_pair, bias_e, bias_o):
    rows = q_pair.shape[0]
    lo_q = lax.broadcasted_iota(jnp.int32, q_pair.shape, 1) < HEAD_DIM
    zero = jnp.zeros_like(q_pair)
    q_both = jnp.concatenate([jnp.where(lo_q, q_pair, zero), jnp.where(lo_q, zero, q_pair)], axis=0)
    s = lax.dot_general(q_both, k_pair, (((1,), (1,)), ((), ())), preferred_element_type=F32)
    s_e, s_o = s[:rows], s[rows:]
    if bias_e is not None:
        s_e = s_e + bias_e
        s_o = s_o + bias_o
    mx_e = jnp.max(s_e, axis=-1, keepdims=True)
    mx_o = jnp.max(s_o, axis=-1, keepdims=True)
    p_e = jnp.exp2(s_e - mx_e).astype(BF16)
    p_o = jnp.exp2(s_o - mx_o).astype(BF16)
    lo_v = lax.broadcasted_iota(jnp.int32, v_pair.shape, 1) < HEAD_DIM
    one = jnp.ones_like(v_pair)
    r_e = jnp.dot(p_e, jnp.where(lo_v, v_pair, one), preferred_element_type=F32)
    r_o = jnp.dot(p_o, jnp.where(lo_v, one, v_pair), preferred_element_type=F32)
    return r_e, r_o, mx_e, mx_o


def _stat_lane(head):
    return head + HEAD_DIM if head % 2 == 0 else head


STAT_DEN_SHIFT = 16


DIL_UNITS = {1: 8, 4: 4, 16: 8}


def _dil_init_bias(rbh_ref, rbl_ref, sel_ref, mrow_ref, bias_ref):
    f = (jnp.dot(rbh_ref[...], sel_ref[...], preferred_element_type=F32)
         + jnp.dot(rbl_ref[...], sel_ref[...], preferred_element_type=F32)) * LOG2E + mrow_ref[...]
    col = lax.broadcasted_iota(jnp.int32, (DIL_BLOCK, 2 * DIL_BLOCK), 1)
    for h in range(N_HEADS_DIL):
        row = jnp.broadcast_to(f[h:h + 1, :], (DIL_BLOCK, 2 * DIL_BLOCK))
        tab = pltpu.roll(row, 0, 1, stride=1, stride_axis=0)
        bias_ref[0, h] = tab
        bias_ref[1, h] = jnp.where(col >= DIL_BLOCK, tab, NEG_INF)


def _dil_unit(q, kp, kc, vp, vc, first, bias_ref, put_num):
    lane = lax.broadcasted_iota(jnp.int32, (DIL_BLOCK, LANES), 1)
    lo = lane < HEAD_DIM
    st_c = jnp.zeros((DIL_BLOCK, LANES), F32)
    for j in range(N_HEADS_DIL // 2):
        sl = slice(j * LANES, (j + 1) * LANES)
        k_pair = jnp.concatenate([kp(sl), kc(sl)], axis=0)
        v_pair = jnp.concatenate([vp(sl), vc(sl)], axis=0)
        r_e, r_o, mx_e, mx_o = _pair_attention(q(sl), k_pair, v_pair,
                                               bias_ref[first, 2 * j], bias_ref[first, 2 * j + 1])
        put_num(j, jnp.where(lo, r_e, r_o))
        l_e, l_o = _stat_lane(2 * j), _stat_lane(2 * j + 1)
        st_c = jnp.where(lane == l_e, mx_e, jnp.where(lane == l_e + STAT_DEN_SHIFT, r_e, st_c))
        st_c = jnp.where(lane == l_o, mx_o, jnp.where(lane == l_o + STAT_DEN_SHIFT, r_o, st_c))
    return st_c


def _dil1_kernel(q_ref, kp_ref, kc_ref, vp_ref, vc_ref, rbh_ref, rbl_ref, sel_ref, mrow_ref,
                 o_ref, st_ref, bias_ref):
    @pl.when((pl.program_id(0) == 0) & (pl.program_id(1) == 0))
    def _():
        _dil_init_bias(rbh_ref, rbl_ref, sel_ref, mrow_ref, bias_ref)

    for u in range(DIL_UNITS[1]):
        rows = slice(u * DIL_BLOCK, (u + 1) * DIL_BLOCK)
        prev = slice((u - 1) * DIL_BLOCK, u * DIL_BLOCK)
        if u == 0:
            first = (pl.program_id(1) == 0).astype(jnp.int32)
            kp, vp = (lambda sl: kp_ref[:, sl]), (lambda sl: vp_ref[:, sl])
        else:
            first = 0
            kp, vp = (lambda sl, p=prev: kc_ref[p, sl]), (lambda sl, p=prev: vc_ref[p, sl])

        def put_num(j, num, rows=rows):
            o_ref[rows, j * LANES:(j + 1) * LANES] = num.astype(o_ref.dtype)

        st_ref[rows, :] = _dil_unit(lambda sl, r=rows: q_ref[r, sl], kp, lambda sl, r=rows: kc_ref[r, sl],
                                    vp, lambda sl, r=rows: vc_ref[r, sl], first, bias_ref, put_num)


DIL_LBLOCKS = {4: 2, 16: 1}


def _dil_kernel(q_ref, kp_ref, kc_ref, vp_ref, vc_ref, rbh_ref, rbl_ref, sel_ref, mrow_ref,
                o_ref, st_ref, bias_ref, acc_ref, *, dil):
    b, lb, g = pl.program_id(0), pl.program_id(1), pl.program_id(2)
    nr, nl = DIL_UNITS[dil], DIL_LBLOCKS[dil]

    @pl.when((b == 0) & (lb == 0) & (g == 0))
    def _():
        _dil_init_bias(rbh_ref, rbl_ref, sel_ref, mrow_ref, bias_ref)

    for ll in range(nl):
        sub = slice(ll * DIL_BLOCK, (ll + 1) * DIL_BLOCK)
        prev = slice((ll - 1) * DIL_BLOCK, ll * DIL_BLOCK)
        first = (lb == 0).astype(jnp.int32) if ll == 0 else 0
        for u in range(nr):
            rows = pl.ds(ll * DIL_BLOCK * dil + g * nr + u, DIL_BLOCK, stride=dil)
            if ll == 0:
                kp, vp = (lambda sl, u=u: kp_ref[u, :, sl]), (lambda sl, u=u: vp_ref[u, :, sl])
            else:
                kp = lambda sl, u=u, p=prev: kc_ref[u, p, sl]
                vp = lambda sl, u=u, p=prev: vc_ref[u, p, sl]

            def put_num(j, num, rows=rows):
                acc_ref[j, rows, :] = num

            st_ref[rows, :] = _dil_unit(lambda sl, u=u, r=sub: q_ref[u, r, sl], kp,
                                        lambda sl, u=u, r=sub: kc_ref[u, r, sl], vp,
                                        lambda sl, u=u, r=sub: vc_ref[u, r, sl], first, bias_ref, put_num)

    @pl.when(g == dil // nr - 1)
    def _():
        for j in range(N_HEADS_DIL // 2):
            o_ref[:, j * LANES:(j + 1) * LANES] = acc_ref[j].astype(o_ref.dtype)


def _t5_bucket_np(dist):
    max_exact = REL_BUCKETS // 2
    n = np.maximum(dist, 1).astype(np.float32)
    large = max_exact + (np.log(n / np.float32(max_exact)) / np.float32(math.log(REL_MAX_DIST / max_exact))
                         * np.float32(REL_BUCKETS - max_exact)).astype(np.int32)
    large = np.minimum(large, REL_BUCKETS - 1)
    return np.where(dist < max_exact, dist, large)


def _bias_selector(window, dil):
    back = window // dil
    x = np.arange(2 * DIL_BLOCK)
    rel = DIL_BLOCK - x
    valid = (rel >= 0) & (rel <= back)
    bucket = _t5_bucket_np(np.clip(rel, 0, back) * dil)
    sel = np.zeros((LANES, 2 * DIL_BLOCK), np.float32)
    sel[bucket[valid], x[valid]] = 1.0
    mrow = np.where(valid, 0.0, NEG_INF).astype(np.float32)[None]
    return jnp.asarray(sel, BF16), jnp.asarray(mrow)


def _dilated(src, rb_hi, rb_lo, batch, seq, window, dil):
    sub_len = seq // dil
    nb = sub_len // DIL_BLOCK
    sel, mrow = _bias_selector(window, dil)
    nu = DIL_UNITS[dil]
    consts = (rb_hi, rb_lo, sel, mrow)
    const = lambda a: pl.BlockSpec(a.shape, lambda *_: (0, 0))
    stat_shape = jax.ShapeDtypeStruct((batch * seq, LANES), F32)
    out_shape = [jax.ShapeDtypeStruct((batch * seq, D_DIL), BF16), stat_shape]
    bias_scratch = pltpu.VMEM((2, N_HEADS_DIL, DIL_BLOCK, 2 * DIL_BLOCK), F32)

    if dil == 1:
        rows = nu * DIL_BLOCK
        steps = nb // nu
        cur = lambda col: pl.BlockSpec((None, None, rows, D_DIL), lambda b, l: (b, 0, l, col))
        prev = lambda col: pl.BlockSpec((None, None, DIL_BLOCK, D_DIL),
                                        lambda b, l: (b, 0, jnp.maximum(nu * l - 1, 0), col))
        out = lambda width: pl.BlockSpec((rows, width), lambda b, l: (b * steps + l, 0))
        return pl.pallas_call(
            _dil1_kernel,
            grid=(batch, steps),
            in_specs=[cur(0), prev(1), cur(1), prev(2), cur(2)] + [const(a) for a in consts],
            out_specs=[out(D_DIL), out(LANES)],
            out_shape=out_shape,
            scratch_shapes=[bias_scratch],
            compiler_params=_cparams(("arbitrary", "arbitrary")),
            name="dilated_d1",
        )(src, src, src, src, src, *consts)

    nl = DIL_LBLOCKS[dil]
    steps = nb // nl
    rows = nl * DIL_BLOCK * dil
    cur = lambda col: pl.BlockSpec((None, nu, nl * DIL_BLOCK, D_DIL), lambda b, l, g: (b, g, l, col))
    prev = lambda col: pl.BlockSpec((None, nu, DIL_BLOCK, D_DIL),
                                    lambda b, l, g: (b, g, jnp.maximum(nl * l - 1, 0), col))
    out = lambda width: pl.BlockSpec((rows, width), lambda b, l, g: (b * steps + l, 0))
    return pl.pallas_call(
        functools.partial(_dil_kernel, dil=dil),
        grid=(batch, steps, dil // nu),
        in_specs=[cur(0), prev(1), cur(1), prev(2), cur(2)] + [const(a) for a in consts],
        out_specs=[out(D_DIL), out(LANES)],
        out_shape=out_shape,
        scratch_shapes=[bias_scratch, pltpu.VMEM((N_HEADS_DIL // 2, rows, LANES), F32)],
        compiler_params=_cparams(("arbitrary", "arbitrary", "arbitrary")),
        name=f"dilated_d{dil}",
    )(src, src, src, src, src, *consts)


GLA_TM = 256
GLA_STEP_TILES = 2


GLA_NC = GLA_TM // GLA_CHUNK


def _gla_tile(bb, rows, q_ref, k_ref, v_ref, g_ref, w2h_ref, w2l_ref, bg_ref, nw_ref, tri_ref, o_ref, st_ref):
    g = g_ref[bb, rows, :LANES]
    z = (jnp.dot(g, w2h_ref[...], preferred_element_type=F32)
         + jnp.dot(g, w2l_ref[...], preferred_element_type=F32) + bg_ref[...])
    la = (jnp.minimum(z, 0.0) * (LOG2E / GLA_TAU)
          - jnp.log2(1.0 + jnp.exp2(jnp.abs(z) * -LOG2E)) * (1.0 / GLA_TAU))
    la_h = la.astype(BF16)
    la_l = (la - la_h.astype(F32)).astype(BF16)

    tri = tri_ref[...]
    b = jnp.dot(tri, la_h, preferred_element_type=F32) + jnp.dot(tri, la_l, preferred_element_type=F32)
    b_last = jnp.concatenate(
        [jnp.broadcast_to(b[(ci + 1) * GLA_CHUNK - 1:(ci + 1) * GLA_CHUNK, :], (GLA_CHUNK, D_GLA_K))
         for ci in range(GLA_NC)], axis=0)

    q = q_ref[bb, rows].astype(F32)
    k = k_ref[bb, rows].astype(F32)
    q_t = (q * jnp.exp2(b)).astype(BF16)
    k_t = (k * jnp.exp2(-b)).astype(BF16)
    k_s = k * jnp.exp2(b_last - b)

    causal = tri.astype(F32) > 0.0
    t_chunk = lax.broadcasted_iota(jnp.int32, (GLA_DK, GLA_TM), 1) // GLA_CHUNK
    nw = nw_ref[...]
    for h in range(N_HEADS_GLA):
        cs = slice(h * GLA_DK, (h + 1) * GLA_DK)
        vs = slice(h * GLA_DV, (h + 1) * GLA_DV)
        v_h = v_ref[bb, rows, vs]
        a = lax.dot_general(q_t[:, cs], k_t[:, cs], (((1,), (1,)), ((), ())), preferred_element_type=F32)
        o = jnp.dot(jnp.where(causal, a, 0.0).astype(BF16), v_h, preferred_element_type=F32)
        k_s_t = k_s[:, cs].T.astype(BF16)
        zero = jnp.zeros_like(k_s_t)
        kv_all = jnp.dot(jnp.concatenate([jnp.where(t_chunk == ci, k_s_t, zero) for ci in range(GLA_NC)], axis=0),
                         v_h, preferred_element_type=F32)
        decay_t = jnp.exp2(b_last[:, cs].T)
        st = st_ref[bb, h]
        outs = []
        for ci in range(GLA_NC):
            rs = slice(ci * GLA_CHUNK, (ci + 1) * GLA_CHUNK)
            o_c = o[rs] + jnp.dot(q_t[rs, cs], st.astype(BF16), preferred_element_type=F32)
            decay = jnp.broadcast_to(decay_t[:, ci * GLA_CHUNK:ci * GLA_CHUNK + 1], (GLA_DK, GLA_DV))
            st = decay * st + kv_all[ci * GLA_DK:(ci + 1) * GLA_DK]
            ms = jnp.mean(o_c * o_c, axis=-1, keepdims=True)
            outs.append((o_c * lax.rsqrt(ms + EPS) * nw).astype(o_ref.dtype))
        st_ref[bb, h] = st
        o_ref[bb, rows, vs] = jnp.concatenate(outs, axis=0)


def _gla_kernel(*refs):
    *tile_refs, wout_ref, o_ref, wout_bf_ref, st_ref = refs

    @pl.when(pl.program_id(0) == 0)
    def _():
        st_ref[...] = jnp.zeros_like(st_ref)

    wout_bf_ref[...] = wout_ref[...].astype(wout_bf_ref.dtype)

    for tt in range(GLA_STEP_TILES):
        rows = slice(tt * GLA_TM, (tt + 1) * GLA_TM)
        for bb in range(st_ref.shape[0]):
            _gla_tile(bb, rows, *tile_refs, o_ref, st_ref)


def _gla(proj, w2_hi, w2_lo, b_gate, norm_w, w_out, batch, seq):
    step_rows = GLA_TM * GLA_STEP_TILES
    steps = seq // step_rows
    wout_rows = w_out.shape[0] // steps
    pos = np.arange(GLA_TM)
    tri = jnp.asarray((pos[:, None] // GLA_CHUNK == pos[None, :] // GLA_CHUNK) & (pos[None, :] <= pos[:, None]),
                      BF16)
    proj3 = proj.reshape(batch, seq, REST_W)
    rows = lambda width, col: pl.BlockSpec((batch, step_rows, width), lambda i: (0, i, col // width))
    const = lambda shape: pl.BlockSpec(shape, lambda i: (0, 0))
    wout_spec = pl.BlockSpec((wout_rows, w_out.shape[1]), lambda i: (i, 0))
    out, w_out_bf = pl.pallas_call(
        _gla_kernel,
        grid=(steps,),
        in_specs=[
            rows(D_GLA_K, COL_GQ), rows(D_GLA_K, COL_GK), rows(D_GLA_V, COL_GV), rows(GLR_PAD, COL_GLR),
            const((LANES, D_GLA_K)), const((LANES, D_GLA_K)), const((1, D_GLA_K)), const((1, GLA_DV)),
            const((GLA_TM, GLA_TM)), wout_spec,
        ],
        out_specs=[rows(D_GLA_V, 0), wout_spec],
        out_shape=[jax.ShapeDtypeStruct((batch, seq, D_GLA_V), BF16), jax.ShapeDtypeStruct(w_out.shape, BF16)],
        scratch_shapes=[pltpu.VMEM((batch, N_HEADS_GLA, GLA_DK, GLA_DV), F32)],
        compiler_params=_cparams(("arbitrary",)),
        name="gla",
    )(proj3, proj3, proj3, proj3, w2_hi, w2_lo, b_gate, norm_w, tri, w_out)
    return out.reshape(batch * seq, D_GLA_V), w_out_bf


MEM_TM = 2048
MEM_UNIT = 128


def _mem_kernel(q_ref, mem_ref, mnw_ref, wkv_ref, o_ref, den_ref, kv_ref):
    @pl.when(pl.program_id(1) == 0)
    def _():
        m = mem_ref[0]
        ms = jnp.mean(m * m, axis=-1, keepdims=True)
        mn = (m * lax.rsqrt(ms + EPS) * mnw_ref[...]).astype(BF16)
        kv_ref[...] = jnp.dot(mn, wkv_ref[...].astype(BF16), preferred_element_type=F32).astype(BF16)

    lane = lax.broadcasted_iota(jnp.int32, (MEM_UNIT, LANES), 1)
    lo = lane < HEAD_DIM
    for u in range(MEM_TM // MEM_UNIT):
        rows = slice(u * MEM_UNIT, (u + 1) * MEM_UNIT)
        den_c = jnp.ones((MEM_UNIT, LANES), F32)
        for j in range(N_HEADS_MEM // 2):
            sl = slice(j * LANES, (j + 1) * LANES)
            r_e, r_o, _, _ = _pair_attention(q_ref[rows, sl], kv_ref[:, sl],
                                             kv_ref[:, D_MEM + j * LANES:D_MEM + (j + 1) * LANES], None, None)
            o_ref[rows, sl] = jnp.where(lo, r_e, r_o).astype(o_ref.dtype)
            den_c = jnp.where(lane == _stat_lane(2 * j), r_e, jnp.where(lane == _stat_lane(2 * j + 1), r_o, den_c))
        den_ref[rows, :] = den_c


def _mem_attn(proj, mem, mem_norm_w, w_kv, batch, seq):
    steps = seq // MEM_TM
    rows = lambda width, col=0: pl.BlockSpec((MEM_TM, width), lambda b, i: (b * steps + i, col // width))
    return pl.pallas_call(
        _mem_kernel,
        grid=(batch, steps),
        in_specs=[
            rows(D_MEM, COL_MQ),
            pl.BlockSpec((1, MEM_LEN, D_MODEL), lambda b, i: (b, 0, 0)),
            pl.BlockSpec((1, D_MODEL), lambda b, i: (0, 0)),
            pl.BlockSpec((D_MODEL, 2 * D_MEM), lambda b, i: (0, 0)),
        ],
        out_specs=[rows(D_MEM), rows(LANES)],
        out_shape=[jax.ShapeDtypeStruct((batch * seq, D_MEM), BF16),
                   jax.ShapeDtypeStruct((batch * seq, LANES), F32)],
        scratch_shapes=[pltpu.VMEM((MEM_LEN, 2 * D_MEM), BF16)],
        compiler_params=_cparams(("parallel", "arbitrary")),
        name="mem_attn",
    )(proj, mem, mem_norm_w, w_kv)


OUT_TM = 512


def _silu(g):
    h = g * 0.5
    return h * jnp.tanh(h) + h


OUT_GW = 256


def _out_kernel(n1_ref, n2_ref, n3_ref, s1_ref, s2_ref, s3_ref, og_ref, nm_ref, dm_ref, ga_ref, gg_ref, gm_ref,
                sm_ref, ex_ref, exm_ref, w_ref, x_ref, nw_ref, out_ref, mix_ref):
    keep = sm_ref[0:1, :]
    fill = sm_ref[1:2, :]
    stats = [s_ref[...] for s_ref in (s1_ref, s2_ref, s3_ref)]
    m1, m2, m3 = [st * keep for st in stats]
    d1, d2, d3 = [pltpu.roll(st, LANES - STAT_DEN_SHIFT, 1) * keep + fill for st in stats]
    mx = jnp.maximum(jnp.maximum(m1, m2), m3)
    t1, t2, t3 = jnp.exp2(m1 - mx), jnp.exp2(m2 - mx), jnp.exp2(m3 - mx)
    inv = 1.0 / (t1 * d1 + t2 * d2 + t3 * d3)
    wts = [(t * inv).astype(BF16) for t in (t1, t2, t3)]
    inv_m = (1.0 / dm_ref[...]).astype(BF16)

    def gate_chunk(lo):
        cols = slice(lo, lo + OUT_GW)
        if lo < D_DIL:
            a = jnp.zeros((OUT_TM, OUT_GW), F32)
            for wt, n_ref in zip(wts, (n1_ref, n2_ref, n3_ref)):
                a = a + jnp.dot(wt, ex_ref[:, cols], preferred_element_type=F32) * n_ref[:, cols].astype(F32)
            g = ga_ref[:, cols]
        elif lo < D_DIL + D_GLA_V:
            lc = slice(lo - D_DIL, lo - D_DIL + OUT_GW)
            a, g = og_ref[:, lc], gg_ref[:, lc]
        else:
            a = jnp.dot(inv_m, exm_ref[...], preferred_element_type=F32) * nm_ref[...].astype(F32)
            g = gm_ref[...]
        mix_ref[:, cols] = a.astype(BF16) * _silu(g)

    for lo in range(0, D_MIX, OUT_GW):
        gate_chunk(lo)
    y = jnp.dot(mix_ref[...], w_ref[...], preferred_element_type=F32)
    ms = jnp.mean(y * y, axis=-1, keepdims=True)
    out_ref[...] = x_ref[...] + y * lax.rsqrt(ms + EPS) * nw_ref[...]


def _out_proj(pats, o_gla, mem_pair, proj, stat_mask, expand, expand_mem, w_out, x2d, norm_w):
    m = x2d.shape[0]
    rows = lambda width, col=0: pl.BlockSpec((OUT_TM, width), lambda s: (s, col // width))
    const = lambda shape: pl.BlockSpec(shape, lambda s: (0, 0), pipeline_mode=pl.Buffered(1))
    nums, stats = zip(*pats)
    return pl.pallas_call(
        _out_kernel,
        grid=(m // OUT_TM,),
        in_specs=[
            *([rows(D_DIL)] * 3), *([rows(LANES)] * 3),
            rows(D_GLA_V), rows(D_MEM), rows(LANES),
            rows(D_DIL, COL_GATE_A), rows(D_GLA_V, COL_GATE_G), rows(D_MEM, COL_GATE_M),
            const((8, LANES)), const((LANES, D_DIL)), const((LANES, D_MEM)), const((D_MIX, D_MODEL)),
            rows(D_MODEL), const((1, D_MODEL)),
        ],
        out_specs=rows(D_MODEL),
        out_shape=jax.ShapeDtypeStruct((m, D_MODEL), x2d.dtype),
        scratch_shapes=[pltpu.VMEM((OUT_TM, D_MIX), BF16)],
        compiler_params=_cparams(("parallel",)),
        name="out_proj",
    )(*nums, *stats, o_gla, *mem_pair, proj, proj, proj, stat_mask, expand, expand_mem, w_out, x2d, norm_w)


D_IN_PROJ = 3 * D_DIL + 2 * D_GLA_K + D_GLA_V + GLA_GATE_RANK + D_MEM + D_MIX
SRC_GLR = 3 * D_DIL + 2 * D_GLA_K + D_GLA_V
SRC_MQ = SRC_GLR + GLA_GATE_RANK
SRC_GATE = SRC_MQ + D_MEM
REST_SEGMENTS = (
    (QKV_W, 2 * D_GLA_K + D_GLA_V, COL_GQ),
    (SRC_GATE + D_DIL, D_GLA_V, COL_GATE_G),
    (SRC_MQ, D_MEM, COL_MQ),
    (SRC_GATE + D_DIL + D_GLA_V, D_MEM, COL_GATE_M),
    (SRC_GATE, D_DIL, COL_GATE_A),
)
WPREP_TK = 512


def _wprep_kernel(wt_ref, qkv_ref, rest_ref):
    def put(o_ref, src, width, dst):
        for c in range(width // LANES):
            lo = c * LANES
            o_ref[:, dst + lo:dst + lo + LANES] = wt_ref[src + lo:src + lo + LANES, :].T.astype(o_ref.dtype)

    put(qkv_ref, 0, QKV_W, 0)
    for src, width, dst in REST_SEGMENTS:
        put(rest_ref, src, width, dst)
    glr = wt_ref[SRC_GLR:SRC_GLR + LANES, :].T
    keep = lax.broadcasted_iota(jnp.int32, glr.shape, 1) < GLA_GATE_RANK
    rest_ref[:, COL_GLR:COL_GLR + LANES] = jnp.where(keep, glr, 0.0).astype(rest_ref.dtype)
    rest_ref[:, COL_GLR + LANES:COL_GLR + GLR_PAD] = jnp.zeros((WPREP_TK, GLR_PAD - LANES), rest_ref.dtype)


def _regroup_in_weights(w_in):
    w_qkv, w_rest = pl.pallas_call(
        _wprep_kernel,
        grid=(D_MODEL // WPREP_TK,),
        in_specs=[pl.BlockSpec((D_IN_PROJ, WPREP_TK), lambda i: (0, i))],
        out_specs=[pl.BlockSpec((WPREP_TK, QKV_W), lambda i: (i, 0)),
                   pl.BlockSpec((WPREP_TK, REST_W), lambda i: (i, 0))],
        out_shape=[jax.ShapeDtypeStruct((D_MODEL, QKV_W), BF16), jax.ShapeDtypeStruct((D_MODEL, REST_W), BF16)],
        compiler_params=_cparams(("parallel",)),
        name="weight_regroup",
    )(w_in.T)
    scale_qkv = np.ones((1, QKV_W), np.float32)
    scale_qkv[:, :D_DIL] = HEAD_DIM ** -0.5 * LOG2E
    scale_rest = np.ones((1, REST_W), np.float32)
    scale_rest[:, COL_GQ:COL_GQ + D_GLA_K] = GLA_DK ** -0.5
    scale_rest[:, COL_MQ:COL_MQ + D_MEM] = HEAD_DIM ** -0.5 * LOG2E
    return w_qkv, w_rest, jnp.asarray(scale_qkv), jnp.asarray(scale_rest)


def _split_hi_lo(w):
    hi = w.astype(BF16)
    return hi, (w - hi.astype(F32)).astype(BF16)


def kernel(x, mem, norm_pre_w, w_in, rel_bias, w_gla_gate2, b_gla_gate, gla_norm_w, mem_norm_w, w_mem_kv,
           w_out, norm_post_w):
    batch, seq, _ = x.shape
    depth = w_in.shape[0]
    expand_np = np.zeros((LANES, D_DIL), np.float32)
    for h in range(N_HEADS_DIL):
        expand_np[_stat_lane(h), h * HEAD_DIM:(h + 1) * HEAD_DIM] = 1.0
    expand = jnp.asarray(expand_np, BF16)
    expand_mem = jnp.asarray(expand_np[:, :D_MEM], BF16)
    stat_mask_np = np.zeros((8, LANES), np.float32)
    stat_mask_np[0, [_stat_lane(h) for h in range(N_HEADS_DIL)]] = 1.0
    stat_mask_np[1] = 1.0 - stat_mask_np[0]
    stat_mask = jnp.asarray(stat_mask_np)
    rb_t = jnp.zeros((16, LANES), F32).at[:N_HEADS_DIL, :REL_BUCKETS].set(rel_bias.astype(F32).T)
    rb_hi, rb_lo = _split_hi_lo(rb_t)
    for l in range(depth):
        x2d = x.reshape(batch * seq, D_MODEL)
        w_qkv, w_rest, scale_qkv, scale_rest = _regroup_in_weights(w_in[l])
        h, qkv1, qkv4, qkv16 = _qkv_proj(x2d, norm_pre_w[l][None], w_qkv, scale_qkv, batch, seq)
        rest = _rest_proj(h, w_rest, scale_rest)

        srcs = (qkv1.reshape(batch, 1, seq, QKV_W), qkv4, qkv16)
        pats = [_dilated(src, rb_hi, rb_lo, batch, seq, window, dil)
                for (window, dil), src in zip(DIL_PATTERNS, srcs)]

        w2 = jnp.zeros((LANES, D_GLA_K), F32).at[:GLA_GATE_RANK].set(w_gla_gate2[l])
        w2_hi, w2_lo = _split_hi_lo(w2)
        o_gla, w_out_bf = _gla(rest, w2_hi, w2_lo, b_gla_gate[l][None], gla_norm_w[l][None], w_out[l],
                               batch, seq)

        mem_pair = _mem_attn(rest, mem, mem_norm_w[l][None], w_mem_kv[l], batch, seq)

        out = _out_proj(pats, o_gla, mem_pair, rest, stat_mask, expand, expand_mem, w_out_bf, x2d,
                        norm_post_w[l][None])
        x = out.reshape(batch, seq, D_MODEL)
    return x
```

```python
import functools
import math

import numpy as np
import jax
import jax.numpy as jnp
from jax import lax
from jax.experimental import pallas as pl
from jax.experimental.pallas import tpu as pltpu

F32 = jnp.float32
BF16 = jnp.bfloat16

D_MODEL = 2048
HEAD_DIM = 64
N_HEADS_DIL = 12
DIL_PATTERNS = ((128, 1), (512, 4), (2048, 16))
DIL_BLOCK = 128
N_HEADS_GLA = 4
GLA_DK = 128
GLA_DV = 256
GLA_GATE_RANK = 16
GLA_TAU = 16.0
GLA_CHUNK = 64
N_HEADS_MEM = 4
MEM_LEN = 256
REL_BUCKETS = 32
REL_MAX_DIST = 2048
EPS = 1e-6
NEG_INF = -1e30

D_DIL = N_HEADS_DIL * HEAD_DIM
D_GLA_K = N_HEADS_GLA * GLA_DK
D_GLA_V = N_HEADS_GLA * GLA_DV
D_MEM = N_HEADS_MEM * HEAD_DIM
D_MIX = D_DIL + D_GLA_V + D_MEM

LANES = 128
GLR_PAD = 256

QKV_W = 3 * D_DIL
COL_GQ = 0
COL_GK = COL_GQ + D_GLA_K
COL_GV = COL_GK + D_GLA_K
COL_GATE_G = COL_GV + D_GLA_V
COL_MQ = COL_GATE_G + D_GLA_V
COL_GATE_M = COL_MQ + D_MEM
COL_GLR = COL_GATE_M + D_MEM
COL_GATE_A = COL_GLR + GLR_PAD
REST_W = COL_GATE_A + D_DIL

VMEM_LIMIT = 56 * 1024 * 1024


def _cparams(sem):
    return pltpu.CompilerParams(dimension_semantics=sem, vmem_limit_bytes=VMEM_LIMIT)


QKV_TM = 512
N_SLABS = D_DIL // LANES
REGROUP_DILS = (4, 16)


def _qkv_proj_kernel(x_ref, nw_ref, w_ref, h_ref, o_ref, x4_ref, x16_ref, slab_ref):
    x = x_ref[...]
    ms = jnp.mean(x * x, axis=-1, keepdims=True)
    h_ref[...] = (x * lax.rsqrt(ms + EPS) * nw_ref[...]).astype(BF16)
    for t in range(QKV_W // D_DIL):
        cols = slice(t * D_DIL, (t + 1) * D_DIL)
        res = jnp.dot(h_ref[...], w_ref[:, cols], preferred_element_type=F32)
        o_ref[:, cols] = res.astype(o_ref.dtype)
        for s in range(N_SLABS):
            slab_ref[t, s] = res[:, s * LANES:(s + 1) * LANES]
        for out_ref, dil in zip((x4_ref, x16_ref), REGROUP_DILS):
            for r in range(dil):
                for s in range(N_SLABS):
                    lo = t * D_DIL + s * LANES
                    out_ref[0, r, :, lo:lo + LANES] = (
                        slab_ref[t, s, pl.ds(r, QKV_TM // dil, stride=dil), :].astype(out_ref.dtype))


def _qkv_proj(x2d, norm_w, w_qkv, batch, seq):
    m = x2d.shape[0]
    tiles_per_seq = seq // QKV_TM
    const = lambda shape: pl.BlockSpec(shape, lambda i: (0, 0), pipeline_mode=pl.Buffered(1))
    grouped = lambda dil: pl.BlockSpec((1, dil, QKV_TM // dil, QKV_W),
                                       lambda i: (i // tiles_per_seq, 0, i % tiles_per_seq, 0))
    return pl.pallas_call(
        _qkv_proj_kernel,
        grid=(m // QKV_TM,),
        in_specs=[
            pl.BlockSpec((QKV_TM, D_MODEL), lambda i: (i, 0)),
            const((1, D_MODEL)), const((D_MODEL, QKV_W)),
        ],
        out_specs=[pl.BlockSpec((QKV_TM, D_MODEL), lambda i: (i, 0)),
                   pl.BlockSpec((QKV_TM, QKV_W), lambda i: (i, 0))] + [grouped(d) for d in REGROUP_DILS],
        out_shape=[jax.ShapeDtypeStruct((m, D_MODEL), BF16), jax.ShapeDtypeStruct((m, QKV_W), BF16)]
        + [jax.ShapeDtypeStruct((batch, d, seq // d, QKV_W), BF16) for d in REGROUP_DILS],
        scratch_shapes=[pltpu.VMEM((QKV_W // D_DIL, N_SLABS, QKV_TM, LANES), F32)],
        compiler_params=_cparams(("parallel",)),
        name="qkv_proj",
    )(x2d, norm_w, w_qkv)


REST_TM = 512


def _rest_proj_kernel(h_ref, w_ref, o_ref):
    o_ref[...] = jnp.dot(h_ref[...], w_ref[...], preferred_element_type=F32).astype(o_ref.dtype)


def _rest_proj(h, w_rest):
    m = h.shape[0]
    const = lambda shape: pl.BlockSpec(shape, lambda i: (0, 0), pipeline_mode=pl.Buffered(1))
    return pl.pallas_call(
        _rest_proj_kernel,
        grid=(m // REST_TM,),
        in_specs=[pl.BlockSpec((REST_TM, D_MODEL), lambda i: (i, 0)), const((D_MODEL, REST_W))],
        out_specs=pl.BlockSpec((REST_TM, REST_W), lambda i: (i, 0)),
        out_shape=jax.ShapeDtypeStruct((m, REST_W), BF16),
        compiler_params=_cparams(("parallel",)),
        name="rest_proj",
    )(h, w_rest)


LOG2E = math.log2(math.e)


def _pair_attention(q_pair, k_pair, v_pair, bias_e, bias_o):
    rows = q_pair.shape[0]
    lo_q = lax.broadcasted_iota(jnp.int32, q_pair.shape, 1) < HEAD_DIM
    zero = jnp.zeros_like(q_pair)
    q_both = jnp.concatenate([jnp.where(lo_q, q_pair, zero), jnp.where(lo_q, zero, q_pair)], axis=0)
    s = lax.dot_general(q_both, k_pair, (((1,), (1,)), ((), ())), preferred_element_type=F32)
    s_e, s_o = s[:rows], s[rows:]
    if bias_e is not None:
        s_e = s_e + bias_e
        s_o = s_o + bias_o
    mx_e = jnp.max(s_e, axis=-1, keepdims=True)
    mx_o = jnp.max(s_o, axis=-1, keepdims=True)
    p_e = jnp.exp2(s_e - mx_e).astype(BF16)
    p_o = jnp.exp2(s_o - mx_o).astype(BF16)
    lo_v = lax.broadcasted_iota(jnp.int32, v_pair.shape, 1) < HEAD_DIM
    one = jnp.ones_like(v_pair)
    r_e = jnp.dot(p_e, jnp.where(lo_v, v_pair, one), preferred_element_type=F32)
    r_o = jnp.dot(p_o, jnp.where(lo_v, one, v_pair), preferred_element_type=F32)
    return r_e, r_o, mx_e, mx_o


def _stat_lane(head):
    return head + HEAD_DIM if head % 2 == 0 else head


STAT_DEN_SHIFT = 16


DIL_UNITS = {1: 8, 4: 4, 16: 8}


def _dil_init_bias(rbh_ref, rbl_ref, sel_ref, mrow_ref, bias_ref):
    f = (jnp.dot(rbh_ref[...], sel_ref[...], preferred_element_type=F32)
         + jnp.dot(rbl_ref[...], sel_ref[...], preferred_element_type=F32)) * LOG2E + mrow_ref[...]
    col = lax.broadcasted_iota(jnp.int32, (DIL_BLOCK, 2 * DIL_BLOCK), 1)
    for h in range(N_HEADS_DIL):
        row = jnp.broadcast_to(f[h:h + 1, :], (DIL_BLOCK, 2 * DIL_BLOCK))
        tab = pltpu.roll(row, 0, 1, stride=1, stride_axis=0)
        bias_ref[0, h] = tab
        bias_ref[1, h] = jnp.where(col >= DIL_BLOCK, tab, NEG_INF)


def _dil_unit(q, kp, kc, vp, vc, first, bias_ref, put_num):
    lane = lax.broadcasted_iota(jnp.int32, (DIL_BLOCK, LANES), 1)
    lo = lane < HEAD_DIM
    st_c = jnp.zeros((DIL_BLOCK, LANES), F32)
    for j in range(N_HEADS_DIL // 2):
        sl = slice(j * LANES, (j + 1) * LANES)
        k_pair = jnp.concatenate([kp(sl), kc(sl)], axis=0)
        v_pair = jnp.concatenate([vp(sl), vc(sl)], axis=0)
        r_e, r_o, mx_e, mx_o = _pair_attention(q(sl), k_pair, v_pair,
                                               bias_ref[first, 2 * j], bias_ref[first, 2 * j + 1])
        put_num(j, jnp.where(lo, r_e, r_o))
        l_e, l_o = _stat_lane(2 * j), _stat_lane(2 * j + 1)
        st_c = jnp.where(lane == l_e, mx_e, jnp.where(lane == l_e + STAT_DEN_SHIFT, r_e, st_c))
        st_c = jnp.where(lane == l_o, mx_o, jnp.where(lane == l_o + STAT_DEN_SHIFT, r_o, st_c))
    return st_c


def _dil1_kernel(q_ref, kp_ref, kc_ref, vp_ref, vc_ref, rbh_ref, rbl_ref, sel_ref, mrow_ref,
                 o_ref, st_ref, bias_ref):
    @pl.when((pl.program_id(0) == 0) & (pl.program_id(1) == 0))
    def _():
        _dil_init_bias(rbh_ref, rbl_ref, sel_ref, mrow_ref, bias_ref)

    for u in range(DIL_UNITS[1]):
        rows = slice(u * DIL_BLOCK, (u + 1) * DIL_BLOCK)
        prev = slice((u - 1) * DIL_BLOCK, u * DIL_BLOCK)
        if u == 0:
            first = (pl.program_id(1) == 0).astype(jnp.int32)
            kp, vp = (lambda sl: kp_ref[:, sl]), (lambda sl: vp_ref[:, sl])
        else:
            first = 0
            kp, vp = (lambda sl, p=prev: kc_ref[p, sl]), (lambda sl, p=prev: vc_ref[p, sl])

        def put_num(j, num, rows=rows):
            o_ref[rows, j * LANES:(j + 1) * LANES] = num.astype(o_ref.dtype)

        st_ref[rows, :] = _dil_unit(lambda sl, r=rows: q_ref[r, sl], kp, lambda sl, r=rows: kc_ref[r, sl],
                                    vp, lambda sl, r=rows: vc_ref[r, sl], first, bias_ref, put_num)


DIL_LBLOCKS = {4: 2, 16: 1}


def _dil_kernel(q_ref, kp_ref, kc_ref, vp_ref, vc_ref, rbh_ref, rbl_ref, sel_ref, mrow_ref,
                o_ref, st_ref, bias_ref, acc_ref, *, dil):
    b, lb, g = pl.program_id(0), pl.program_id(1), pl.program_id(2)
    nr, nl = DIL_UNITS[dil], DIL_LBLOCKS[dil]

    @pl.when((b == 0) & (lb == 0) & (g == 0))
    def _():
        _dil_init_bias(rbh_ref, rbl_ref, sel_ref, mrow_ref, bias_ref)

    for ll in range(nl):
        sub = slice(ll * DIL_BLOCK, (ll + 1) * DIL_BLOCK)
        prev = slice((ll - 1) * DIL_BLOCK, ll * DIL_BLOCK)
        first = (lb == 0).astype(jnp.int32) if ll == 0 else 0
        for u in range(nr):
            rows = pl.ds(ll * DIL_BLOCK * dil + g * nr + u, DIL_BLOCK, stride=dil)
            if ll == 0:
                kp, vp = (lambda sl, u=u: kp_ref[u, :, sl]), (lambda sl, u=u: vp_ref[u, :, sl])
            else:
                kp = lambda sl, u=u, p=prev: kc_ref[u, p, sl]
                vp = lambda sl, u=u, p=prev: vc_ref[u, p, sl]

            def put_num(j, num, rows=rows):
                acc_ref[j, rows, :] = num

            st_ref[rows, :] = _dil_unit(lambda sl, u=u, r=sub: q_ref[u, r, sl], kp,
                                        lambda sl, u=u, r=sub: kc_ref[u, r, sl], vp,
                                        lambda sl, u=u, r=sub: vc_ref[u, r, sl], first, bias_ref, put_num)

    @pl.when(g == dil // nr - 1)
    def _():
        for j in range(N_HEADS_DIL // 2):
            o_ref[:, j * LANES:(j + 1) * LANES] = acc_ref[j].astype(o_ref.dtype)


def _t5_bucket_np(dist):
    max_exact = REL_BUCKETS // 2
    n = np.maximum(dist, 1).astype(np.float32)
    large = max_exact + (np.log(n / np.float32(max_exact)) / np.float32(math.log(REL_MAX_DIST / max_exact))
                         * np.float32(REL_BUCKETS - max_exact)).astype(np.int32)
    large = np.minimum(large, REL_BUCKETS - 1)
    return np.where(dist < max_exact, dist, large)


def _bias_selector(window, dil):
    back = window // dil
    x = np.arange(2 * DIL_BLOCK)
    rel = DIL_BLOCK - x
    valid = (rel >= 0) & (rel <= back)
    bucket = _t5_bucket_np(np.clip(rel, 0, back) * dil)
    sel = np.zeros((LANES, 2 * DIL_BLOCK), np.float32)
    sel[bucket[valid], x[valid]] = 1.0
    mrow = np.where(valid, 0.0, NEG_INF).astype(np.float32)[None]
    return jnp.asarray(sel, BF16), jnp.asarray(mrow)


def _dilated(src, rb_hi, rb_lo, batch, seq, window, dil):
    sub_len = seq // dil
    nb = sub_len // DIL_BLOCK
    sel, mrow = _bias_selector(window, dil)
    nu = DIL_UNITS[dil]
    consts = (rb_hi, rb_lo, sel, mrow)
    const = lambda a: pl.BlockSpec(a.shape, lambda *_: (0, 0))
    stat_shape = jax.ShapeDtypeStruct((batch * seq, LANES), F32)
    out_shape = [jax.ShapeDtypeStruct((batch * seq, D_DIL), BF16), stat_shape]
    bias_scratch = pltpu.VMEM((2, N_HEADS_DIL, DIL_BLOCK, 2 * DIL_BLOCK), F32)

    if dil == 1:
        rows = nu * DIL_BLOCK
        steps = nb // nu
        cur = lambda col: pl.BlockSpec((None, None, rows, D_DIL), lambda b, l: (b, 0, l, col))
        prev = lambda col: pl.BlockSpec((None, None, DIL_BLOCK, D_DIL),
                                        lambda b, l: (b, 0, jnp.maximum(nu * l - 1, 0), col))
        out = lambda width: pl.BlockSpec((rows, width), lambda b, l: (b * steps + l, 0))
        return pl.pallas_call(
            _dil1_kernel,
            grid=(batch, steps),
            in_specs=[cur(0), prev(1), cur(1), prev(2), cur(2)] + [const(a) for a in consts],
            out_specs=[out(D_DIL), out(LANES)],
            out_shape=out_shape,
            scratch_shapes=[bias_scratch],
            compiler_params=_cparams(("arbitrary", "arbitrary")),
            name="dilated_d1",
        )(src, src, src, src, src, *consts)

    nl = DIL_LBLOCKS[dil]
    steps = nb // nl
    rows = nl * DIL_BLOCK * dil
    cur = lambda col: pl.BlockSpec((None, nu, nl * DIL_BLOCK, D_DIL), lambda b, l, g: (b, g, l, col))
    prev = lambda col: pl.BlockSpec((None, nu, DIL_BLOCK, D_DIL),
                                    lambda b, l, g: (b, g, jnp.maximum(nl * l - 1, 0), col))
    out = lambda width: pl.BlockSpec((rows, width), lambda b, l, g: (b * steps + l, 0))
    return pl.pallas_call(
        functools.partial(_dil_kernel, dil=dil),
        grid=(batch, steps, dil // nu),
        in_specs=[cur(0), prev(1), cur(1), prev(2), cur(2)] + [const(a) for a in consts],
        out_specs=[out(D_DIL), out(LANES)],
        out_shape=out_shape,
        scratch_shapes=[bias_scratch, pltpu.VMEM((N_HEADS_DIL // 2, rows, LANES), F32)],
        compiler_params=_cparams(("arbitrary", "arbitrary", "arbitrary")),
        name=f"dilated_d{dil}",
    )(src, src, src, src, src, *consts)


GLA_TM = 256
GLA_STEP_TILES = 2


GLA_NC = GLA_TM // GLA_CHUNK


def _gla_tile(bb, rows, q_ref, k_ref, v_ref, g_ref, w2h_ref, w2l_ref, bg_ref, nw_ref, tri_ref, o_ref, st_ref):
    g = g_ref[bb, rows, :LANES]
    z = (jnp.dot(g, w2h_ref[...], preferred_element_type=F32)
         + jnp.dot(g, w2l_ref[...], preferred_element_type=F32) + bg_ref[...])
    la = (jnp.minimum(z, 0.0) * (LOG2E / GLA_TAU)
          - jnp.log2(1.0 + jnp.exp2(jnp.abs(z) * -LOG2E)) * (1.0 / GLA_TAU))
    la_h = la.astype(BF16)
    la_l = (la - la_h.astype(F32)).astype(BF16)

    tri = tri_ref[...]
    b = jnp.dot(tri, la_h, preferred_element_type=F32) + jnp.dot(tri, la_l, preferred_element_type=F32)
    b_last = jnp.concatenate(
        [jnp.broadcast_to(b[(ci + 1) * GLA_CHUNK - 1:(ci + 1) * GLA_CHUNK, :], (GLA_CHUNK, D_GLA_K))
         for ci in range(GLA_NC)], axis=0)

    q = q_ref[bb, rows].astype(F32)
    k = k_ref[bb, rows].astype(F32)
    q_t = (q * jnp.exp2(b)).astype(BF16)
    k_t = (k * jnp.exp2(-b)).astype(BF16)
    k_s = k * jnp.exp2(b_last - b)

    causal = tri.astype(F32) > 0.0
    t_chunk = lax.broadcasted_iota(jnp.int32, (GLA_DK, GLA_TM), 1) // GLA_CHUNK
    nw = nw_ref[...]
    for h in range(N_HEADS_GLA):
        cs = slice(h * GLA_DK, (h + 1) * GLA_DK)
        vs = slice(h * GLA_DV, (h + 1) * GLA_DV)
        v_h = v_ref[bb, rows, vs]
        a = lax.dot_general(q_t[:, cs], k_t[:, cs], (((1,), (1,)), ((), ())), preferred_element_type=F32)
        o = jnp.dot(jnp.where(causal, a, 0.0).astype(BF16), v_h, preferred_element_type=F32)
        k_s_t = k_s[:, cs].T.astype(BF16)
        zero = jnp.zeros_like(k_s_t)
        kv_all = jnp.dot(jnp.concatenate([jnp.where(t_chunk == ci, k_s_t, zero) for ci in range(GLA_NC)], axis=0),
                         v_h, preferred_element_type=F32)
        decay_t = jnp.exp2(b_last[:, cs].T)
        st = st_ref[bb, h]
        outs = []
        for ci in range(GLA_NC):
            rs = slice(ci * GLA_CHUNK, (ci + 1) * GLA_CHUNK)
            o_c = o[rs] + jnp.dot(q_t[rs, cs], st.astype(BF16), preferred_element_type=F32)
            decay = jnp.broadcast_to(decay_t[:, ci * GLA_CHUNK:ci * GLA_CHUNK + 1], (GLA_DK, GLA_DV))
            st = decay * st + kv_all[ci * GLA_DK:(ci + 1) * GLA_DK]
            ms = jnp.mean(o_c * o_c, axis=-1, keepdims=True)
            outs.append((o_c * lax.rsqrt(ms + EPS) * nw).astype(o_ref.dtype))
        st_ref[bb, h] = st
        o_ref[bb, rows, vs] = jnp.concatenate(outs, axis=0)


def _gla_kernel(*refs):
    *tile_refs, wout_ref, o_ref, wout_bf_ref, st_ref = refs

    @pl.when(pl.program_id(0) == 0)
    def _():
        st_ref[...] = jnp.zeros_like(st_ref)

    wout_bf_ref[...] = wout_ref[...].astype(wout_bf_ref.dtype)

    for tt in range(GLA_STEP_TILES):
        rows = slice(tt * GLA_TM, (tt + 1) * GLA_TM)
        for bb in range(st_ref.shape[0]):
            _gla_tile(bb, rows, *tile_refs, o_ref, st_ref)


def _gla(proj, w2_hi, w2_lo, b_gate, norm_w, w_out, batch, seq):
    step_rows = GLA_TM * GLA_STEP_TILES
    steps = seq // step_rows
    wout_rows = w_out.shape[0] // steps
    pos = np.arange(GLA_TM)
    tri = jnp.asarray((pos[:, None] // GLA_CHUNK == pos[None, :] // GLA_CHUNK) & (pos[None, :] <= pos[:, None]),
                      BF16)
    proj3 = proj.reshape(batch, seq, REST_W)
    rows = lambda width, col: pl.BlockSpec((batch, step_rows, width), lambda i: (0, i, col // width))
    const = lambda shape: pl.BlockSpec(shape, lambda i: (0, 0))
    wout_spec = pl.BlockSpec((wout_rows, w_out.shape[1]), lambda i: (i, 0))
    out, w_out_bf = pl.pallas_call(
        _gla_kernel,
        grid=(steps,),
        in_specs=[
            rows(D_GLA_K, COL_GQ), rows(D_GLA_K, COL_GK), rows(D_GLA_V, COL_GV), rows(GLR_PAD, COL_GLR),
            const((LANES, D_GLA_K)), const((LANES, D_GLA_K)), const((1, D_GLA_K)), const((1, GLA_DV)),
            const((GLA_TM, GLA_TM)), wout_spec,
        ],
        out_specs=[rows(D_GLA_V, 0), wout_spec],
        out_shape=[jax.ShapeDtypeStruct((batch, seq, D_GLA_V), BF16), jax.ShapeDtypeStruct(w_out.shape, BF16)],
        scratch_shapes=[pltpu.VMEM((batch, N_HEADS_GLA, GLA_DK, GLA_DV), F32)],
        compiler_params=_cparams(("arbitrary",)),
        name="gla",
    )(proj3, proj3, proj3, proj3, w2_hi, w2_lo, b_gate, norm_w, tri, w_out)
    return out.reshape(batch * seq, D_GLA_V), w_out_bf


MEM_TM = 2048
MEM_UNIT = 128


def _mem_kernel(q_ref, mem_ref, mnw_ref, wkv_ref, o_ref, den_ref, kv_ref):
    @pl.when(pl.program_id(1) == 0)
    def _():
        m = mem_ref[0]
        ms = jnp.mean(m * m, axis=-1, keepdims=True)
        mn = (m * lax.rsqrt(ms + EPS) * mnw_ref[...]).astype(BF16)
        kv_ref[...] = jnp.dot(mn, wkv_ref[...].astype(BF16), preferred_element_type=F32).astype(BF16)

    lane = lax.broadcasted_iota(jnp.int32, (MEM_UNIT, LANES), 1)
    lo = lane < HEAD_DIM
    for u in range(MEM_TM // MEM_UNIT):
        rows = slice(u * MEM_UNIT, (u + 1) * MEM_UNIT)
        den_c = jnp.ones((MEM_UNIT, LANES), F32)
        for j in range(N_HEADS_MEM // 2):
            sl = slice(j * LANES, (j + 1) * LANES)
            r_e, r_o, _, _ = _pair_attention(q_ref[rows, sl], kv_ref[:, sl],
                                             kv_ref[:, D_MEM + j * LANES:D_MEM + (j + 1) * LANES], None, None)
            o_ref[rows, sl] = jnp.where(lo, r_e, r_o).astype(o_ref.dtype)
            den_c = jnp.where(lane == _stat_lane(2 * j), r_e, jnp.where(lane == _stat_lane(2 * j + 1), r_o, den_c))
        den_ref[rows, :] = den_c


def _mem_attn(proj, mem, mem_norm_w, w_kv, batch, seq):
    steps = seq // MEM_TM
    rows = lambda width, col=0: pl.BlockSpec((MEM_TM, width), lambda b, i: (b * steps + i, col // width))
    return pl.pallas_call(
        _mem_kernel,
        grid=(batch, steps),
        in_specs=[
            rows(D_MEM, COL_MQ),
            pl.BlockSpec((1, MEM_LEN, D_MODEL), lambda b, i: (b, 0, 0)),
            pl.BlockSpec((1, D_MODEL), lambda b, i: (0, 0)),
            pl.BlockSpec((D_MODEL, 2 * D_MEM), lambda b, i: (0, 0)),
        ],
        out_specs=[rows(D_MEM), rows(LANES)],
        out_shape=[jax.ShapeDtypeStruct((batch * seq, D_MEM), BF16),
                   jax.ShapeDtypeStruct((batch * seq, LANES), F32)],
        scratch_shapes=[pltpu.VMEM((MEM_LEN, 2 * D_MEM), BF16)],
        compiler_params=_cparams(("parallel", "arbitrary")),
        name="mem_attn",
    )(proj, mem, mem_norm_w, w_kv)


OUT_TM = 512


def _silu(g):
    h = g * 0.5
    return h * jnp.tanh(h) + h


OUT_GW = 256


def _out_kernel(n1_ref, n2_ref, n3_ref, s1_ref, s2_ref, s3_ref, og_ref, nm_ref, dm_ref, ga_ref, gg_ref, gm_ref,
                sm_ref, ex_ref, exm_ref, w_ref, x_ref, nw_ref, out_ref, mix_ref):
    keep = sm_ref[0:1, :]
    fill = sm_ref[1:2, :]
    stats = [s_ref[...] for s_ref in (s1_ref, s2_ref, s3_ref)]
    m1, m2, m3 = [st * keep for st in stats]
    d1, d2, d3 = [pltpu.roll(st, LANES - STAT_DEN_SHIFT, 1) * keep + fill for st in stats]
    mx = jnp.maximum(jnp.maximum(m1, m2), m3)
    t1, t2, t3 = jnp.exp2(m1 - mx), jnp.exp2(m2 - mx), jnp.exp2(m3 - mx)
    inv = 1.0 / (t1 * d1 + t2 * d2 + t3 * d3)
    wts = [(t * inv).astype(BF16) for t in (t1, t2, t3)]
    inv_m = (1.0 / dm_ref[...]).astype(BF16)

    def gate_chunk(lo):
        cols = slice(lo, lo + OUT_GW)
        if lo < D_DIL:
            a = jnp.zeros((OUT_TM, OUT_GW), F32)
            for wt, n_ref in zip(wts, (n1_ref, n2_ref, n3_ref)):
                a = a + jnp.dot(wt, ex_ref[:, cols], preferred_element_type=F32) * n_ref[:, cols].astype(F32)
            g = ga_ref[:, cols]
        elif lo < D_DIL + D_GLA_V:
            lc = slice(lo - D_DIL, lo - D_DIL + OUT_GW)
            a, g = og_ref[:, lc], gg_ref[:, lc]
        else:
            a = jnp.dot(inv_m, exm_ref[...], preferred_element_type=F32) * nm_ref[...].astype(F32)
            g = gm_ref[...]
        mix_ref[:, cols] = a.astype(BF16) * _silu(g)

    for lo in range(0, D_MIX, OUT_GW):
        gate_chunk(lo)
    y = jnp.dot(mix_ref[...], w_ref[...], preferred_element_type=F32)
    ms = jnp.mean(y * y, axis=-1, keepdims=True)
    out_ref[...] = x_ref[...] + y * lax.rsqrt(ms + EPS) * nw_ref[...]


def _out_proj(pats, o_gla, mem_pair, proj, stat_mask, expand, expand_mem, w_out, x2d, norm_w):
    m = x2d.shape[0]
    rows = lambda width, col=0: pl.BlockSpec((OUT_TM, width), lambda s: (s, col // width))
    const = lambda shape: pl.BlockSpec(shape, lambda s: (0, 0), pipeline_mode=pl.Buffered(1))
    nums, stats = zip(*pats)
    return pl.pallas_call(
        _out_kernel,
        grid=(m // OUT_TM,),
        in_specs=[
            *([rows(D_DIL)] * 3), *([rows(LANES)] * 3),
            rows(D_GLA_V), rows(D_MEM), rows(LANES),
            rows(D_DIL, COL_GATE_A), rows(D_GLA_V, COL_GATE_G), rows(D_MEM, COL_GATE_M),
            const((8, LANES)), const((LANES, D_DIL)), const((LANES, D_MEM)), const((D_MIX, D_MODEL)),
            rows(D_MODEL), const((1, D_MODEL)),
        ],
        out_specs=rows(D_MODEL),
        out_shape=jax.ShapeDtypeStruct((m, D_MODEL), x2d.dtype),
        scratch_shapes=[pltpu.VMEM((OUT_TM, D_MIX), BF16)],
        compiler_params=_cparams(("parallel",)),
        name="out_proj",
    )(*nums, *stats, o_gla, *mem_pair, proj, proj, proj, stat_mask, expand, expand_mem, w_out, x2d, norm_w)


D_IN_PROJ = 3 * D_DIL + 2 * D_GLA_K + D_GLA_V + GLA_GATE_RANK + D_MEM + D_MIX
SRC_GLR = 3 * D_DIL + 2 * D_GLA_K + D_GLA_V
SRC_MQ = SRC_GLR + GLA_GATE_RANK
SRC_GATE = SRC_MQ + D_MEM
SCALE_ATTN_Q = HEAD_DIM ** -0.5 * LOG2E
SCALE_GLA_Q = GLA_DK ** -0.5
QKV_SEGMENTS = (
    (0, D_DIL, 0, SCALE_ATTN_Q),
    (D_DIL, 2 * D_DIL, D_DIL, 1.0),
)
REST_SEGMENTS = (
    (QKV_W, D_GLA_K, COL_GQ, SCALE_GLA_Q),
    (QKV_W + D_GLA_K, D_GLA_K + D_GLA_V, COL_GK, 1.0),
    (SRC_GATE + D_DIL, D_GLA_V, COL_GATE_G, 1.0),
    (SRC_MQ, D_MEM, COL_MQ, SCALE_ATTN_Q),
    (SRC_GATE + D_DIL + D_GLA_V, D_MEM, COL_GATE_M, 1.0),
    (SRC_GATE, D_DIL, COL_GATE_A, 1.0),
)
WPREP_TK = 512


def _wprep_kernel(wt_ref, qkv_ref, rest_ref):
    for o_ref, segments in ((qkv_ref, QKV_SEGMENTS), (rest_ref, REST_SEGMENTS)):
        for src, width, dst, scale in segments:
            for lo in range(0, width, LANES):
                w = wt_ref[src + lo:src + lo + LANES, :].T
                if scale != 1.0:
                    w = w * scale
                o_ref[:, dst + lo:dst + lo + LANES] = w.astype(o_ref.dtype)
    glr = wt_ref[SRC_GLR:SRC_GLR + LANES, :].T
    keep = lax.broadcasted_iota(jnp.int32, glr.shape, 1) < GLA_GATE_RANK
    rest_ref[:, COL_GLR:COL_GLR + LANES] = jnp.where(keep, glr, 0.0).astype(rest_ref.dtype)
    rest_ref[:, COL_GLR + LANES:COL_GLR + GLR_PAD] = jnp.zeros((WPREP_TK, GLR_PAD - LANES), rest_ref.dtype)


def _regroup_in_weights(w_in):
    return pl.pallas_call(
        _wprep_kernel,
        grid=(D_MODEL // WPREP_TK,),
        in_specs=[pl.BlockSpec((D_IN_PROJ, WPREP_TK), lambda i: (0, i))],
        out_specs=[pl.BlockSpec((WPREP_TK, QKV_W), lambda i: (i, 0)),
                   pl.BlockSpec((WPREP_TK, REST_W), lambda i: (i, 0))],
        out_shape=[jax.ShapeDtypeStruct((D_MODEL, QKV_W), BF16), jax.ShapeDtypeStruct((D_MODEL, REST_W), BF16)],
        compiler_params=_cparams(("parallel",)),
        name="weight_regroup",
    )(w_in.T)


def _split_hi_lo(w):
    hi = w.astype(BF16)
    return hi, (w - hi.astype(F32)).astype(BF16)


def kernel(x, mem, norm_pre_w, w_in, rel_bias, w_gla_gate2, b_gla_gate, gla_norm_w, mem_norm_w, w_mem_kv,
           w_out, norm_post_w):
    batch, seq, _ = x.shape
    depth = w_in.shape[0]
    expand_np = np.zeros((LANES, D_DIL), np.float32)
    for h in range(N_HEADS_DIL):
        expand_np[_stat_lane(h), h * HEAD_DIM:(h + 1) * HEAD_DIM] = 1.0
    expand = jnp.asarray(expand_np, BF16)
    expand_mem = jnp.asarray(expand_np[:, :D_MEM], BF16)
    stat_mask_np = np.zeros((8, LANES), np.float32)
    stat_mask_np[0, [_stat_lane(h) for h in range(N_HEADS_DIL)]] = 1.0
    stat_mask_np[1] = 1.0 - stat_mask_np[0]
    stat_mask = jnp.asarray(stat_mask_np)
    rb_t = jnp.zeros((16, LANES), F32).at[:N_HEADS_DIL, :REL_BUCKETS].set(rel_bias.astype(F32).T)
    rb_hi, rb_lo = _split_hi_lo(rb_t)
    for l in range(depth):
        x2d = x.reshape(batch * seq, D_MODEL)
        w_qkv, w_rest = _regroup_in_weights(w_in[l])
        h, qkv1, qkv4, qkv16 = _qkv_proj(x2d, norm_pre_w[l][None], w_qkv, batch, seq)
        rest = _rest_proj(h, w_rest)

        srcs = (qkv1.reshape(batch, 1, seq, QKV_W), qkv4, qkv16)
        pats = [_dilated(src, rb_hi, rb_lo, batch, seq, window, dil)
                for (window, dil), src in zip(DIL_PATTERNS, srcs)]

        w2 = jnp.zeros((LANES, D_GLA_K), F32).at[:GLA_GATE_RANK].set(w_gla_gate2[l])
        w2_hi, w2_lo = _split_hi_lo(w2)
        o_gla, w_out_bf = _gla(rest, w2_hi, w2_lo, b_gla_gate[l][None], gla_norm_w[l][None], w_out[l],
                               batch, seq)

        mem_pair = _mem_attn(rest, mem, mem_norm_w[l][None], w_mem_kv[l], batch, seq)

        out = _out_proj(pats, o_gla, mem_pair, rest, stat_mask, expand, expand_mem, w_out_bf, x2d,
                        norm_post_w[l][None])
        x = out.reshape(batch, seq, D_MODEL)
    return x
```

```python
import functools
import math

import numpy as np
import jax
import jax.numpy as jnp
from jax import lax
from jax.experimental import pallas as pl
from jax.experimental.pallas import tpu as pltpu

F32 = jnp.float32
BF16 = jnp.bfloat16

D_MODEL = 2048
HEAD_DIM = 64
N_HEADS_DIL = 12
DIL_PATTERNS = ((128, 1), (512, 4), (2048, 16))
DIL_BLOCK = 128
N_HEADS_GLA = 4
GLA_DK = 128
GLA_DV = 256
GLA_GATE_RANK = 16
GLA_TAU = 16.0
GLA_CHUNK = 64
N_HEADS_MEM = 4
MEM_LEN = 256
REL_BUCKETS = 32
REL_MAX_DIST = 2048
EPS = 1e-6
NEG_INF = -1e30

D_DIL = N_HEADS_DIL * HEAD_DIM
D_GLA_K = N_HEADS_GLA * GLA_DK
D_GLA_V = N_HEADS_GLA * GLA_DV
D_MEM = N_HEADS_MEM * HEAD_DIM
D_MIX = D_DIL + D_GLA_V + D_MEM

LANES = 128
GLR_PAD = 256

QKV_W = 3 * D_DIL
COL_GQ = 0
COL_GK = COL_GQ + D_GLA_K
COL_GV = COL_GK + D_GLA_K
COL_GATE_G = COL_GV + D_GLA_V
COL_MQ = COL_GATE_G + D_GLA_V
COL_GATE_M = COL_MQ + D_MEM
COL_GLR = COL_GATE_M + D_MEM
COL_GATE_A = COL_GLR + GLR_PAD
REST_W = COL_GATE_A + D_DIL

VMEM_LIMIT = 56 * 1024 * 1024


def _cparams(sem):
    return pltpu.CompilerParams(dimension_semantics=sem, vmem_limit_bytes=VMEM_LIMIT)


QKV_TM = 512
N_SLABS = D_DIL // LANES
REGROUP_DILS = (4, 16)


def _qkv_proj_kernel(x_ref, nw_ref, w_ref, h_ref, o_ref, x4_ref, x16_ref, slab_ref):
    x = x_ref[...]
    ms = jnp.mean(x * x, axis=-1, keepdims=True)
    h_ref[...] = (x * lax.rsqrt(ms + EPS) * nw_ref[...]).astype(BF16)
    for t in range(QKV_W // D_DIL):
        cols = slice(t * D_DIL, (t + 1) * D_DIL)
        res = jnp.dot(h_ref[...], w_ref[:, cols], preferred_element_type=F32)
        o_ref[:, cols] = res.astype(o_ref.dtype)
        for s in range(N_SLABS):
            slab_ref[t, s] = res[:, s * LANES:(s + 1) * LANES]
        for out_ref, dil in zip((x4_ref, x16_ref), REGROUP_DILS):
            for r in range(dil):
                for s in range(N_SLABS):
                    lo = t * D_DIL + s * LANES
                    out_ref[0, r, :, lo:lo + LANES] = (
                        slab_ref[t, s, pl.ds(r, QKV_TM // dil, stride=dil), :].astype(out_ref.dtype))


def _qkv_proj(x2d, norm_w, w_qkv, batch, seq):
    m = x2d.shape[0]
    tiles_per_seq = seq // QKV_TM
    const = lambda shape: pl.BlockSpec(shape, lambda i: (0, 0), pipeline_mode=pl.Buffered(1))
    grouped = lambda dil: pl.BlockSpec((1, dil, QKV_TM // dil, QKV_W),
                                       lambda i: (i // tiles_per_seq, 0, i % tiles_per_seq, 0))
    return pl.pallas_call(
        _qkv_proj_kernel,
        grid=(m // QKV_TM,),
        in_specs=[
            pl.BlockSpec((QKV_TM, D_MODEL), lambda i: (i, 0)),
            const((1, D_MODEL)), const((D_MODEL, QKV_W)),
        ],
        out_specs=[pl.BlockSpec((QKV_TM, D_MODEL), lambda i: (i, 0)),
                   pl.BlockSpec((QKV_TM, QKV_W), lambda i: (i, 0))] + [grouped(d) for d in REGROUP_DILS],
        out_shape=[jax.ShapeDtypeStruct((m, D_MODEL), BF16), jax.ShapeDtypeStruct((m, QKV_W), BF16)]
        + [jax.ShapeDtypeStruct((batch, d, seq // d, QKV_W), BF16) for d in REGROUP_DILS],
        scratch_shapes=[pltpu.VMEM((QKV_W // D_DIL, N_SLABS, QKV_TM, LANES), F32)],
        compiler_params=_cparams(("parallel",)),
        name="qkv_proj",
    )(x2d, norm_w, w_qkv)


REST_TM = 512


def _rest_proj_kernel(h_ref, w_ref, o_ref):
    o_ref[...] = jnp.dot(h_ref[...], w_ref[...], preferred_element_type=F32).astype(o_ref.dtype)


def _rest_proj(h, w_rest):
    m = h.shape[0]
    const = lambda shape: pl.BlockSpec(shape, lambda i: (0, 0), pipeline_mode=pl.Buffered(1))
    return pl.pallas_call(
        _rest_proj_kernel,
        grid=(m // REST_TM,),
        in_specs=[pl.BlockSpec((REST_TM, D_MODEL), lambda i: (i, 0)), const((D_MODEL, REST_W))],
        out_specs=pl.BlockSpec((REST_TM, REST_W), lambda i: (i, 0)),
        out_shape=jax.ShapeDtypeStruct((m, REST_W), BF16),
        compiler_params=_cparams(("parallel",)),
        name="rest_proj",
    )(h, w_rest)


LOG2E = math.log2(math.e)


def _pair_attention(q_pair, k_pair, v_pair, bias_e, bias_o):
    rows = q_pair.shape[0]
    lo_q = lax.broadcasted_iota(jnp.int32, q_pair.shape, 1) < HEAD_DIM
    zero = jnp.zeros_like(q_pair)
    q_both = jnp.concatenate([jnp.where(lo_q, q_pair, zero), jnp.where(lo_q, zero, q_pair)], axis=0)
    s = lax.dot_general(q_both, k_pair, (((1,), (1,)), ((), ())), preferred_element_type=F32)
    s_e, s_o = s[:rows], s[rows:]
    if bias_e is not None:
        s_e = s_e + bias_e
        s_o = s_o + bias_o
    mx_e = jnp.max(s_e, axis=-1, keepdims=True)
    mx_o = jnp.max(s_o, axis=-1, keepdims=True)
    p_e = jnp.exp2(s_e - mx_e).astype(BF16)
    p_o = jnp.exp2(s_o - mx_o).astype(BF16)
    lo_v = lax.broadcasted_iota(jnp.int32, v_pair.shape, 1) < HEAD_DIM
    one = jnp.ones_like(v_pair)
    r_e = jnp.dot(p_e, jnp.where(lo_v, v_pair, one), preferred_element_type=F32)
    r_o = jnp.dot(p_o, jnp.where(lo_v, one, v_pair), preferred_element_type=F32)
    return r_e, r_o, mx_e, mx_o


def _stat_lane(head):
    return head + HEAD_DIM if head % 2 == 0 else head


STAT_DEN_SHIFT = 16


DIL_UNITS = {1: 8, 4: 4, 16: 8}


def _dil_init_bias(rbh_ref, rbl_ref, sel_ref, mrow_ref, bias_ref):
    f = (jnp.dot(rbh_ref[...], sel_ref[...], preferred_element_type=F32)
         + jnp.dot(rbl_ref[...], sel_ref[...], preferred_element_type=F32)) * LOG2E + mrow_ref[...]
    col = lax.broadcasted_iota(jnp.int32, (DIL_BLOCK, 2 * DIL_BLOCK), 1)
    for h in range(N_HEADS_DIL):
        row = jnp.broadcast_to(f[h:h + 1, :], (DIL_BLOCK, 2 * DIL_BLOCK))
        tab = pltpu.roll(row, 0, 1, stride=1, stride_axis=0)
        bias_ref[0, h] = tab
        bias_ref[1, h] = jnp.where(col >= DIL_BLOCK, tab, NEG_INF)


def _dil_unit(q, kp, kc, vp, vc, first, bias_ref, put_num):
    lane = lax.broadcasted_iota(jnp.int32, (DIL_BLOCK, LANES), 1)
    lo = lane < HEAD_DIM
    st_c = jnp.zeros((DIL_BLOCK, LANES), F32)
    for j in range(N_HEADS_DIL // 2):
        sl = slice(j * LANES, (j + 1) * LANES)
        if kp is None:
            k_pair, v_pair = kc(sl), vc(sl)
            bias_e, bias_o = bias_ref[0, 2 * j, :, DIL_BLOCK:], bias_ref[0, 2 * j + 1, :, DIL_BLOCK:]
        else:
            k_pair = jnp.concatenate([kp(sl), kc(sl)], axis=0)
            v_pair = jnp.concatenate([vp(sl), vc(sl)], axis=0)
            bias_e, bias_o = bias_ref[first, 2 * j], bias_ref[first, 2 * j + 1]
        r_e, r_o, mx_e, mx_o = _pair_attention(q(sl), k_pair, v_pair, bias_e, bias_o)
        put_num(j, jnp.where(lo, r_e, r_o))
        l_e, l_o = _stat_lane(2 * j), _stat_lane(2 * j + 1)
        st_c = jnp.where(lane == l_e, mx_e, jnp.where(lane == l_e + STAT_DEN_SHIFT, r_e, st_c))
        st_c = jnp.where(lane == l_o, mx_o, jnp.where(lane == l_o + STAT_DEN_SHIFT, r_o, st_c))
    return st_c


def _dil1_kernel(q_ref, kp_ref, kc_ref, vp_ref, vc_ref, rbh_ref, rbl_ref, sel_ref, mrow_ref,
                 o_ref, st_ref, bias_ref):
    @pl.when((pl.program_id(0) == 0) & (pl.program_id(1) == 0))
    def _():
        _dil_init_bias(rbh_ref, rbl_ref, sel_ref, mrow_ref, bias_ref)

    for u in range(DIL_UNITS[1]):
        rows = slice(u * DIL_BLOCK, (u + 1) * DIL_BLOCK)
        prev = slice((u - 1) * DIL_BLOCK, u * DIL_BLOCK)
        if u == 0:
            first = (pl.program_id(1) == 0).astype(jnp.int32)
            kp, vp = (lambda sl: kp_ref[:, sl]), (lambda sl: vp_ref[:, sl])
        else:
            first = 0
            kp, vp = (lambda sl, p=prev: kc_ref[p, sl]), (lambda sl, p=prev: vc_ref[p, sl])

        def put_num(j, num, rows=rows):
            o_ref[rows, j * LANES:(j + 1) * LANES] = num.astype(o_ref.dtype)

        st_ref[rows, :] = _dil_unit(lambda sl, r=rows: q_ref[r, sl], kp, lambda sl, r=rows: kc_ref[r, sl],
                                    vp, lambda sl, r=rows: vc_ref[r, sl], first, bias_ref, put_num)


DIL_LBLOCKS = {4: 2, 16: 2}


def _dil_kernel(*refs, dil, whole_seq):
    if whole_seq:
        q_ref, kc_ref, vc_ref, *rest = refs
        kp_ref = vp_ref = None
    else:
        q_ref, kp_ref, kc_ref, vp_ref, vc_ref, *rest = refs
    rbh_ref, rbl_ref, sel_ref, mrow_ref, o_ref, st_ref, bias_ref, acc_ref = rest
    b, lb, g = pl.program_id(0), pl.program_id(1), pl.program_id(2)
    nr, nl = DIL_UNITS[dil], DIL_LBLOCKS[dil]

    @pl.when((b == 0) & (lb == 0) & (g == 0))
    def _():
        _dil_init_bias(rbh_ref, rbl_ref, sel_ref, mrow_ref, bias_ref)

    for ll in range(nl):
        sub = slice(ll * DIL_BLOCK, (ll + 1) * DIL_BLOCK)
        prev = slice((ll - 1) * DIL_BLOCK, ll * DIL_BLOCK)
        first = (lb == 0).astype(jnp.int32) if ll == 0 else 0
        for u in range(nr):
            rows = pl.ds(ll * DIL_BLOCK * dil + g * nr + u, DIL_BLOCK, stride=dil)
            if ll == 0 and whole_seq:
                kp = vp = None
            elif ll == 0:
                kp, vp = (lambda sl, u=u: kp_ref[u, :, sl]), (lambda sl, u=u: vp_ref[u, :, sl])
            else:
                kp = lambda sl, u=u, p=prev: kc_ref[u, p, sl]
                vp = lambda sl, u=u, p=prev: vc_ref[u, p, sl]

            def put_num(j, num, rows=rows):
                acc_ref[j, rows, :] = num

            st_ref[rows, :] = _dil_unit(lambda sl, u=u, r=sub: q_ref[u, r, sl], kp,
                                        lambda sl, u=u, r=sub: kc_ref[u, r, sl], vp,
                                        lambda sl, u=u, r=sub: vc_ref[u, r, sl], first, bias_ref, put_num)

    @pl.when(g == dil // nr - 1)
    def _():
        for j in range(N_HEADS_DIL // 2):
            o_ref[:, j * LANES:(j + 1) * LANES] = acc_ref[j].astype(o_ref.dtype)


def _t5_bucket_np(dist):
    max_exact = REL_BUCKETS // 2
    n = np.maximum(dist, 1).astype(np.float32)
    large = max_exact + (np.log(n / np.float32(max_exact)) / np.float32(math.log(REL_MAX_DIST / max_exact))
                         * np.float32(REL_BUCKETS - max_exact)).astype(np.int32)
    large = np.minimum(large, REL_BUCKETS - 1)
    return np.where(dist < max_exact, dist, large)


def _bias_selector(window, dil):
    back = window // dil
    x = np.arange(2 * DIL_BLOCK)
    rel = DIL_BLOCK - x
    valid = (rel >= 0) & (rel <= back)
    bucket = _t5_bucket_np(np.clip(rel, 0, back) * dil)
    sel = np.zeros((LANES, 2 * DIL_BLOCK), np.float32)
    sel[bucket[valid], x[valid]] = 1.0
    mrow = np.where(valid, 0.0, NEG_INF).astype(np.float32)[None]
    return jnp.asarray(sel, BF16), jnp.asarray(mrow)


def _dilated(src, rb_hi, rb_lo, batch, seq, window, dil):
    sub_len = seq // dil
    nb = sub_len // DIL_BLOCK
    sel, mrow = _bias_selector(window, dil)
    nu = DIL_UNITS[dil]
    consts = (rb_hi, rb_lo, sel, mrow)
    const = lambda a: pl.BlockSpec(a.shape, lambda *_: (0, 0))
    stat_shape = jax.ShapeDtypeStruct((batch * seq, LANES), F32)
    out_shape = [jax.ShapeDtypeStruct((batch * seq, D_DIL), BF16), stat_shape]
    bias_scratch = pltpu.VMEM((2, N_HEADS_DIL, DIL_BLOCK, 2 * DIL_BLOCK), F32)

    if dil == 1:
        rows = nu * DIL_BLOCK
        steps = nb // nu
        cur = lambda col: pl.BlockSpec((None, None, rows, D_DIL), lambda b, l: (b, 0, l, col))
        prev = lambda col: pl.BlockSpec((None, None, DIL_BLOCK, D_DIL),
                                        lambda b, l: (b, 0, jnp.maximum(nu * l - 1, 0), col))
        out = lambda width: pl.BlockSpec((rows, width), lambda b, l: (b * steps + l, 0))
        return pl.pallas_call(
            _dil1_kernel,
            grid=(batch, steps),
            in_specs=[cur(0), prev(1), cur(1), prev(2), cur(2)] + [const(a) for a in consts],
            out_specs=[out(D_DIL), out(LANES)],
            out_shape=out_shape,
            scratch_shapes=[bias_scratch],
            compiler_params=_cparams(("arbitrary", "arbitrary")),
            name="dilated_d1",
        )(src, src, src, src, src, *consts)

    nl = DIL_LBLOCKS[dil]
    steps = nb // nl
    rows = nl * DIL_BLOCK * dil
    cur = lambda col: pl.BlockSpec((None, nu, nl * DIL_BLOCK, D_DIL), lambda b, l, g: (b, g, l, col))
    prev = lambda col: pl.BlockSpec((None, nu, DIL_BLOCK, D_DIL),
                                    lambda b, l, g: (b, g, jnp.maximum(nl * l - 1, 0), col))
    out = lambda width: pl.BlockSpec((rows, width), lambda b, l, g: (b * steps + l, 0))
    whole_seq = steps == 1
    qkv_specs = [cur(0), cur(1), cur(2)] if whole_seq else [cur(0), prev(1), cur(1), prev(2), cur(2)]
    return pl.pallas_call(
        functools.partial(_dil_kernel, dil=dil, whole_seq=whole_seq),
        grid=(batch, steps, dil // nu),
        in_specs=qkv_specs + [const(a) for a in consts],
        out_specs=[out(D_DIL), out(LANES)],
        out_shape=out_shape,
        scratch_shapes=[bias_scratch, pltpu.VMEM((N_HEADS_DIL // 2, rows, LANES), F32)],
        compiler_params=_cparams(("arbitrary", "arbitrary", "arbitrary")),
        name=f"dilated_d{dil}",
    )(*([src] * len(qkv_specs)), *consts)


GLA_TM = 256
GLA_STEP_TILES = 2


GLA_NC = GLA_TM // GLA_CHUNK


def _gla_tile(bb, rows, q_ref, k_ref, v_ref, g_ref, w2h_ref, w2l_ref, bg_ref, nw_ref, tri_ref, o_ref, st_ref):
    g = g_ref[bb, rows, :LANES]
    z = (jnp.dot(g, w2h_ref[...], preferred_element_type=F32)
         + jnp.dot(g, w2l_ref[...], preferred_element_type=F32) + bg_ref[...])
    la = (jnp.minimum(z, 0.0) * (LOG2E / GLA_TAU)
          - jnp.log2(1.0 + jnp.exp2(jnp.abs(z) * -LOG2E)) * (1.0 / GLA_TAU))
    la_h = la.astype(BF16)
    la_l = (la - la_h.astype(F32)).astype(BF16)

    tri = tri_ref[...]
    b = jnp.dot(tri, la_h, preferred_element_type=F32) + jnp.dot(tri, la_l, preferred_element_type=F32)
    b_last = jnp.concatenate(
        [jnp.broadcast_to(b[(ci + 1) * GLA_CHUNK - 1:(ci + 1) * GLA_CHUNK, :], (GLA_CHUNK, D_GLA_K))
         for ci in range(GLA_NC)], axis=0)

    q = q_ref[bb, rows].astype(F32)
    k = k_ref[bb, rows].astype(F32)
    q_t = (q * jnp.exp2(b)).astype(BF16)
    k_t = (k * jnp.exp2(-b)).astype(BF16)
    k_s = k * jnp.exp2(b_last - b)

    causal = tri.astype(F32) > 0.0
    t_chunk = lax.broadcasted_iota(jnp.int32, (GLA_DK, GLA_TM), 1) // GLA_CHUNK
    nw = nw_ref[...]
    for h in range(N_HEADS_GLA):
        cs = slice(h * GLA_DK, (h + 1) * GLA_DK)
        vs = slice(h * GLA_DV, (h + 1) * GLA_DV)
        v_h = v_ref[bb, rows, vs]
        a = lax.dot_general(q_t[:, cs], k_t[:, cs], (((1,), (1,)), ((), ())), preferred_element_type=F32)
        o = jnp.dot(jnp.where(causal, a, 0.0).astype(BF16), v_h, preferred_element_type=F32)
        k_s_t = k_s[:, cs].T.astype(BF16)
        zero = jnp.zeros_like(k_s_t)
        kv_all = jnp.dot(jnp.concatenate([jnp.where(t_chunk == ci, k_s_t, zero) for ci in range(GLA_NC)], axis=0),
                         v_h, preferred_element_type=F32)
        decay_t = jnp.exp2(b_last[:, cs].T)
        st = st_ref[bb, h]
        outs = []
        for ci in range(GLA_NC):
            rs = slice(ci * GLA_CHUNK, (ci + 1) * GLA_CHUNK)
            o_c = o[rs] + jnp.dot(q_t[rs, cs], st.astype(BF16), preferred_element_type=F32)
            decay = jnp.broadcast_to(decay_t[:, ci * GLA_CHUNK:ci * GLA_CHUNK + 1], (GLA_DK, GLA_DV))
            st = decay * st + kv_all[ci * GLA_DK:(ci + 1) * GLA_DK]
            ms = jnp.mean(o_c * o_c, axis=-1, keepdims=True)
            outs.append((o_c * lax.rsqrt(ms + EPS) * nw).astype(o_ref.dtype))
        st_ref[bb, h] = st
        o_ref[bb, rows, vs] = jnp.concatenate(outs, axis=0)


def _gla_kernel(*refs):
    *tile_refs, wout_ref, o_ref, wout_bf_ref, st_ref = refs

    @pl.when(pl.program_id(0) == 0)
    def _():
        st_ref[...] = jnp.zeros_like(st_ref)

    wout_bf_ref[...] = wout_ref[...].astype(wout_bf_ref.dtype)

    for tt in range(GLA_STEP_TILES):
        rows = slice(tt * GLA_TM, (tt + 1) * GLA_TM)
        for bb in range(st_ref.shape[0]):
            _gla_tile(bb, rows, *tile_refs, o_ref, st_ref)


def _gla(proj, w2_hi, w2_lo, b_gate, norm_w, w_out, batch, seq):
    step_rows = GLA_TM * GLA_STEP_TILES
    steps = seq // step_rows
    wout_rows = w_out.shape[0] // steps
    pos = np.arange(GLA_TM)
    tri = jnp.asarray((pos[:, None] // GLA_CHUNK == pos[None, :] // GLA_CHUNK) & (pos[None, :] <= pos[:, None]),
                      BF16)
    proj3 = proj.reshape(batch, seq, REST_W)
    rows = lambda width, col: pl.BlockSpec((batch, step_rows, width), lambda i: (0, i, col // width))
    const = lambda shape: pl.BlockSpec(shape, lambda i: (0, 0))
    wout_spec = pl.BlockSpec((wout_rows, w_out.shape[1]), lambda i: (i, 0))
    out, w_out_bf = pl.pallas_call(
        _gla_kernel,
        grid=(steps,),
        in_specs=[
            rows(D_GLA_K, COL_GQ), rows(D_GLA_K, COL_GK), rows(D_GLA_V, COL_GV), rows(GLR_PAD, COL_GLR),
            const((LANES, D_GLA_K)), const((LANES, D_GLA_K)), const((1, D_GLA_K)), const((1, GLA_DV)),
            const((GLA_TM, GLA_TM)), wout_spec,
        ],
        out_specs=[rows(D_GLA_V, 0), wout_spec],
        out_shape=[jax.ShapeDtypeStruct((batch, seq, D_GLA_V), BF16), jax.ShapeDtypeStruct(w_out.shape, BF16)],
        scratch_shapes=[pltpu.VMEM((batch, N_HEADS_GLA, GLA_DK, GLA_DV), F32)],
        compiler_params=_cparams(("arbitrary",)),
        name="gla",
    )(proj3, proj3, proj3, proj3, w2_hi, w2_lo, b_gate, norm_w, tri, w_out)
    return out.reshape(batch * seq, D_GLA_V), w_out_bf


MEM_TM = 2048
MEM_UNIT = 128


def _mem_kernel(q_ref, mem_ref, mnw_ref, wkv_ref, o_ref, den_ref, kv_ref):
    @pl.when(pl.program_id(1) == 0)
    def _():
        m = mem_ref[0]
        ms = jnp.mean(m * m, axis=-1, keepdims=True)
        mn = (m * lax.rsqrt(ms + EPS) * mnw_ref[...]).astype(BF16)
        kv_ref[...] = jnp.dot(mn, wkv_ref[...].astype(BF16), preferred_element_type=F32).astype(BF16)

    lane = lax.broadcasted_iota(jnp.int32, (MEM_UNIT, LANES), 1)
    lo = lane < HEAD_DIM
    for u in range(MEM_TM // MEM_UNIT):
        rows = slice(u * MEM_UNIT, (u + 1) * MEM_UNIT)
        den_c = jnp.ones((MEM_UNIT, LANES), F32)
        for j in range(N_HEADS_MEM // 2):
            sl = slice(j * LANES, (j + 1) * LANES)
            r_e, r_o, _, _ = _pair_attention(q_ref[rows, sl], kv_ref[:, sl],
                                             kv_ref[:, D_MEM + j * LANES:D_MEM + (j + 1) * LANES], None, None)
            o_ref[rows, sl] = jnp.where(lo, r_e, r_o).astype(o_ref.dtype)
            den_c = jnp.where(lane == _stat_lane(2 * j), r_e, jnp.where(lane == _stat_lane(2 * j + 1), r_o, den_c))
        den_ref[rows, :] = den_c


def _mem_attn(proj, mem, mem_norm_w, w_kv, batch, seq):
    steps = seq // MEM_TM
    rows = lambda width, col=0: pl.BlockSpec((MEM_TM, width), lambda b, i: (b * steps + i, col // width))
    return pl.pallas_call(
        _mem_kernel,
        grid=(batch, steps),
        in_specs=[
            rows(D_MEM, COL_MQ),
            pl.BlockSpec((1, MEM_LEN, D_MODEL), lambda b, i: (b, 0, 0)),
            pl.BlockSpec((1, D_MODEL), lambda b, i: (0, 0)),
            pl.BlockSpec((D_MODEL, 2 * D_MEM), lambda b, i: (0, 0)),
        ],
        out_specs=[rows(D_MEM), rows(LANES)],
        out_shape=[jax.ShapeDtypeStruct((batch * seq, D_MEM), BF16),
                   jax.ShapeDtypeStruct((batch * seq, LANES), F32)],
        scratch_shapes=[pltpu.VMEM((MEM_LEN, 2 * D_MEM), BF16)],
        compiler_params=_cparams(("parallel", "arbitrary")),
        name="mem_attn",
    )(proj, mem, mem_norm_w, w_kv)


OUT_TM = 512


def _silu(g):
    h = g * 0.5
    return h * jnp.tanh(h) + h


OUT_GW = 256


def _out_kernel(n1_ref, n2_ref, n3_ref, s1_ref, s2_ref, s3_ref, og_ref, nm_ref, dm_ref, ga_ref, gg_ref, gm_ref,
                sm_ref, ex_ref, exm_ref, w_ref, x_ref, nw_ref, out_ref, mix_ref):
    keep = sm_ref[0:1, :]
    fill = sm_ref[1:2, :]
    stats = [s_ref[...] for s_ref in (s1_ref, s2_ref, s3_ref)]
    m1, m2, m3 = [st * keep for st in stats]
    d1, d2, d3 = [pltpu.roll(st, LANES - STAT_DEN_SHIFT, 1) * keep + fill for st in stats]
    mx = jnp.maximum(jnp.maximum(m1, m2), m3)
    t1, t2, t3 = jnp.exp2(m1 - mx), jnp.exp2(m2 - mx), jnp.exp2(m3 - mx)
    inv = 1.0 / (t1 * d1 + t2 * d2 + t3 * d3)
    wts = [(t * inv).astype(BF16) for t in (t1, t2, t3)]
    inv_m = (1.0 / dm_ref[...]).astype(BF16)

    def gate_chunk(lo):
        cols = slice(lo, lo + OUT_GW)
        if lo < D_DIL:
            a = jnp.zeros((OUT_TM, OUT_GW), F32)
            for wt, n_ref in zip(wts, (n1_ref, n2_ref, n3_ref)):
                a = a + jnp.dot(wt, ex_ref[:, cols], preferred_element_type=F32) * n_ref[:, cols].astype(F32)
            g = ga_ref[:, cols]
        elif lo < D_DIL + D_GLA_V:
            lc = slice(lo - D_DIL, lo - D_DIL + OUT_GW)
            a, g = og_ref[:, lc], gg_ref[:, lc]
        else:
            a = jnp.dot(inv_m, exm_ref[...], preferred_element_type=F32) * nm_ref[...].astype(F32)
            g = gm_ref[...]
        mix_ref[:, cols] = a.astype(BF16) * _silu(g)

    for lo in range(0, D_MIX, OUT_GW):
        gate_chunk(lo)
    y = jnp.dot(mix_ref[...], w_ref[...], preferred_element_type=F32)
    ms = jnp.mean(y * y, axis=-1, keepdims=True)
    out_ref[...] = x_ref[...] + y * lax.rsqrt(ms + EPS) * nw_ref[...]


def _out_proj(pats, o_gla, mem_pair, proj, stat_mask, expand, expand_mem, w_out, x2d, norm_w):
    m = x2d.shape[0]
    rows = lambda width, col=0: pl.BlockSpec((OUT_TM, width), lambda s: (s, col // width))
    const = lambda shape: pl.BlockSpec(shape, lambda s: (0, 0), pipeline_mode=pl.Buffered(1))
    nums, stats = zip(*pats)
    return pl.pallas_call(
        _out_kernel,
        grid=(m // OUT_TM,),
        in_specs=[
            *([rows(D_DIL)] * 3), *([rows(LANES)] * 3),
            rows(D_GLA_V), rows(D_MEM), rows(LANES),
            rows(D_DIL, COL_GATE_A), rows(D_GLA_V, COL_GATE_G), rows(D_MEM, COL_GATE_M),
            const((8, LANES)), const((LANES, D_DIL)), const((LANES, D_MEM)), const((D_MIX, D_MODEL)),
            rows(D_MODEL), const((1, D_MODEL)),
        ],
        out_specs=rows(D_MODEL),
        out_shape=jax.ShapeDtypeStruct((m, D_MODEL), x2d.dtype),
        scratch_shapes=[pltpu.VMEM((OUT_TM, D_MIX), BF16)],
        compiler_params=_cparams(("parallel",)),
        name="out_proj",
    )(*nums, *stats, o_gla, *mem_pair, proj, proj, proj, stat_mask, expand, expand_mem, w_out, x2d, norm_w)


D_IN_PROJ = 3 * D_DIL + 2 * D_GLA_K + D_GLA_V + GLA_GATE_RANK + D_MEM + D_MIX
SRC_GLR = 3 * D_DIL + 2 * D_GLA_K + D_GLA_V
SRC_MQ = SRC_GLR + GLA_GATE_RANK
SRC_GATE = SRC_MQ + D_MEM
SCALE_ATTN_Q = HEAD_DIM ** -0.5 * LOG2E
SCALE_GLA_Q = GLA_DK ** -0.5
QKV_SEGMENTS = (
    (0, D_DIL, 0, SCALE_ATTN_Q),
    (D_DIL, 2 * D_DIL, D_DIL, 1.0),
)
REST_SEGMENTS = (
    (QKV_W, D_GLA_K, COL_GQ, SCALE_GLA_Q),
    (QKV_W + D_GLA_K, D_GLA_K + D_GLA_V, COL_GK, 1.0),
    (SRC_GATE + D_DIL, D_GLA_V, COL_GATE_G, 1.0),
    (SRC_MQ, D_MEM, COL_MQ, SCALE_ATTN_Q),
    (SRC_GATE + D_DIL + D_GLA_V, D_MEM, COL_GATE_M, 1.0),
    (SRC_GATE, D_DIL, COL_GATE_A, 1.0),
)
WPREP_TK = 512


def _wprep_kernel(wt_ref, qkv_ref, rest_ref):
    for o_ref, segments in ((qkv_ref, QKV_SEGMENTS), (rest_ref, REST_SEGMENTS)):
        for src, width, dst, scale in segments:
            for lo in range(0, width, LANES):
                w = wt_ref[src + lo:src + lo + LANES, :].T
                if scale != 1.0:
                    w = w * scale
                o_ref[:, dst + lo:dst + lo + LANES] = w.astype(o_ref.dtype)
    glr = wt_ref[SRC_GLR:SRC_GLR + LANES, :].T
    keep = lax.broadcasted_iota(jnp.int32, glr.shape, 1) < GLA_GATE_RANK
    rest_ref[:, COL_GLR:COL_GLR + LANES] = jnp.where(keep, glr, 0.0).astype(rest_ref.dtype)
    rest_ref[:, COL_GLR + LANES:COL_GLR + GLR_PAD] = jnp.zeros((WPREP_TK, GLR_PAD - LANES), rest_ref.dtype)


def _regroup_in_weights(w_in):
    return pl.pallas_call(
        _wprep_kernel,
        grid=(D_MODEL // WPREP_TK,),
        in_specs=[pl.BlockSpec((D_IN_PROJ, WPREP_TK), lambda i: (0, i))],
        out_specs=[pl.BlockSpec((WPREP_TK, QKV_W), lambda i: (i, 0)),
                   pl.BlockSpec((WPREP_TK, REST_W), lambda i: (i, 0))],
        out_shape=[jax.ShapeDtypeStruct((D_MODEL, QKV_W), BF16), jax.ShapeDtypeStruct((D_MODEL, REST_W), BF16)],
        compiler_params=_cparams(("parallel",)),
        name="weight_regroup",
    )(w_in.T)


def _split_hi_lo(w):
    hi = w.astype(BF16)
    return hi, (w - hi.astype(F32)).astype(BF16)


def kernel(x, mem, norm_pre_w, w_in, rel_bias, w_gla_gate2, b_gla_gate, gla_norm_w, mem_norm_w, w_mem_kv,
           w_out, norm_post_w):
    batch, seq, _ = x.shape
    depth = w_in.shape[0]
    expand_np = np.zeros((LANES, D_DIL), np.float32)
    for h in range(N_HEADS_DIL):
        expand_np[_stat_lane(h), h * HEAD_DIM:(h + 1) * HEAD_DIM] = 1.0
    expand = jnp.asarray(expand_np, BF16)
    expand_mem = jnp.asarray(expand_np[:, :D_MEM], BF16)
    stat_mask_np = np.zeros((8, LANES), np.float32)
    stat_mask_np[0, [_stat_lane(h) for h in range(N_HEADS_DIL)]] = 1.0
    stat_mask_np[1] = 1.0 - stat_mask_np[0]
    stat_mask = jnp.asarray(stat_mask_np)
    rb_t = jnp.zeros((16, LANES), F32).at[:N_HEADS_DIL, :REL_BUCKETS].set(rel_bias.astype(F32).T)
    rb_hi, rb_lo = _split_hi_lo(rb_t)
    for l in range(depth):
        x2d = x.reshape(batch * seq, D_MODEL)
        w_qkv, w_rest = _regroup_in_weights(w_in[l])
        h, qkv1, qkv4, qkv16 = _qkv_proj(x2d, norm_pre_w[l][None], w_qkv, batch, seq)
        rest = _rest_proj(h, w_rest)

        srcs = (qkv1.reshape(batch, 1, seq, QKV_W), qkv4, qkv16)
        pats = [_dilated(src, rb_hi, rb_lo, batch, seq, window, dil)
                for (window, dil), src in zip(DIL_PATTERNS, srcs)]

        w2 = jnp.zeros((LANES, D_GLA_K), F32).at[:GLA_GATE_RANK].set(w_gla_gate2[l])
        w2_hi, w2_lo = _split_hi_lo(w2)
        o_gla, w_out_bf = _gla(rest, w2_hi, w2_lo, b_gla_gate[l][None], gla_norm_w[l][None], w_out[l],
                               batch, seq)

        mem_pair = _mem_attn(rest, mem, mem_norm_w[l][None], w_mem_kv[l], batch, seq)

        out = _out_proj(pats, o_gla, mem_pair, rest, stat_mask, expand, expand_mem, w_out_bf, x2d,
                        norm_post_w[l][None])
        x = out.reshape(batch, seq, D_MODEL)
    return x
```

```python
import functools
import math

import numpy as np
import jax
import jax.numpy as jnp
from jax import lax
from jax.experimental import pallas as pl
from jax.experimental.pallas import tpu as pltpu

F32 = jnp.float32
BF16 = jnp.bfloat16

D_MODEL = 2048
HEAD_DIM = 64
N_HEADS_DIL = 12
DIL_PATTERNS = ((128, 1), (512, 4), (2048, 16))
DIL_BLOCK = 128
N_HEADS_GLA = 4
GLA_DK = 128
GLA_DV = 256
GLA_GATE_RANK = 16
GLA_TAU = 16.0
GLA_CHUNK = 64
N_HEADS_MEM = 4
MEM_LEN = 256
REL_BUCKETS = 32
REL_MAX_DIST = 2048
EPS = 1e-6
NEG_INF = -1e30

D_DIL = N_HEADS_DIL * HEAD_DIM
D_GLA_K = N_HEADS_GLA * GLA_DK
D_GLA_V = N_HEADS_GLA * GLA_DV
D_MEM = N_HEADS_MEM * HEAD_DIM
D_MIX = D_DIL + D_GLA_V + D_MEM

LANES = 128
GLR_PAD = 256

QKV_W = 3 * D_DIL
COL_GQ = 0
COL_GK = COL_GQ + D_GLA_K
COL_GV = COL_GK + D_GLA_K
COL_GATE_G = COL_GV + D_GLA_V
COL_MQ = COL_GATE_G + D_GLA_V
COL_GATE_M = COL_MQ + D_MEM
COL_GLR = COL_GATE_M + D_MEM
COL_GATE_A = COL_GLR + GLR_PAD
REST_W = COL_GATE_A + D_DIL

VMEM_LIMIT = 56 * 1024 * 1024


def _cparams(sem):
    return pltpu.CompilerParams(dimension_semantics=sem, vmem_limit_bytes=VMEM_LIMIT)


def _split_hi_lo(w):
    hi = w.astype(BF16)
    return hi, (w - hi.astype(F32)).astype(BF16)


QKV_TM = 512
N_SLABS = D_DIL // LANES
REGROUP_DILS = (4, 16)


def _qkv_proj_kernel(x_ref, nw_ref, w_ref, h_ref, o_ref, x4_ref, x16_ref, slab_ref):
    x = x_ref[...]
    ms = jnp.mean(x * x, axis=-1, keepdims=True)
    h_ref[...] = (x * lax.rsqrt(ms + EPS) * nw_ref[...]).astype(BF16)
    for t in range(QKV_W // D_DIL):
        cols = slice(t * D_DIL, (t + 1) * D_DIL)
        res = jnp.dot(h_ref[...], w_ref[:, cols], preferred_element_type=F32)
        o_ref[:, cols] = res.astype(o_ref.dtype)
        for s in range(N_SLABS):
            slab_ref[t, s] = res[:, s * LANES:(s + 1) * LANES]
        for out_ref, dil in zip((x4_ref, x16_ref), REGROUP_DILS):
            for r in range(dil):
                for s in range(N_SLABS):
                    lo = t * D_DIL + s * LANES
                    out_ref[0, r, :, lo:lo + LANES] = (
                        slab_ref[t, s, pl.ds(r, QKV_TM // dil, stride=dil), :].astype(out_ref.dtype))


def _qkv_proj(x2d, norm_w, w_qkv, batch, seq):
    m = x2d.shape[0]
    tiles_per_seq = seq // QKV_TM
    const = lambda shape: pl.BlockSpec(shape, lambda i: (0, 0), pipeline_mode=pl.Buffered(1))
    grouped = lambda dil: pl.BlockSpec((1, dil, QKV_TM // dil, QKV_W),
                                       lambda i: (i // tiles_per_seq, 0, i % tiles_per_seq, 0))
    return pl.pallas_call(
        _qkv_proj_kernel,
        grid=(m // QKV_TM,),
        in_specs=[
            pl.BlockSpec((QKV_TM, D_MODEL), lambda i: (i, 0)),
            const((1, D_MODEL)), const((D_MODEL, QKV_W)),
        ],
        out_specs=[pl.BlockSpec((QKV_TM, D_MODEL), lambda i: (i, 0)),
                   pl.BlockSpec((QKV_TM, QKV_W), lambda i: (i, 0))] + [grouped(d) for d in REGROUP_DILS],
        out_shape=[jax.ShapeDtypeStruct((m, D_MODEL), BF16), jax.ShapeDtypeStruct((m, QKV_W), BF16)]
        + [jax.ShapeDtypeStruct((batch, d, seq // d, QKV_W), BF16) for d in REGROUP_DILS],
        scratch_shapes=[pltpu.VMEM((QKV_W // D_DIL, N_SLABS, QKV_TM, LANES), F32)],
        compiler_params=_cparams(("parallel",)),
        name="qkv_proj",
    )(x2d, norm_w, w_qkv)


REST_TM = 512


def _rest_proj_kernel(h_ref, w_ref, o_ref):
    o_ref[...] = jnp.dot(h_ref[...], w_ref[...], preferred_element_type=F32).astype(o_ref.dtype)


def _rest_proj(h, w_rest):
    m = h.shape[0]
    const = lambda shape: pl.BlockSpec(shape, lambda i: (0, 0), pipeline_mode=pl.Buffered(1))
    return pl.pallas_call(
        _rest_proj_kernel,
        grid=(m // REST_TM,),
        in_specs=[pl.BlockSpec((REST_TM, D_MODEL), lambda i: (i, 0)), const((D_MODEL, REST_W))],
        out_specs=pl.BlockSpec((REST_TM, REST_W), lambda i: (i, 0)),
        out_shape=jax.ShapeDtypeStruct((m, REST_W), BF16),
        compiler_params=_cparams(("parallel",)),
        name="rest_proj",
    )(h, w_rest)


LOG2E = math.log2(math.e)


def _pair_attention(q_pair, k_pair, v_pair, bias_e, bias_o):
    rows = q_pair.shape[0]
    lo_q = lax.broadcasted_iota(jnp.int32, q_pair.shape, 1) < HEAD_DIM
    zero = jnp.zeros_like(q_pair)
    q_both = jnp.concatenate([jnp.where(lo_q, q_pair, zero), jnp.where(lo_q, zero, q_pair)], axis=0)
    s = lax.dot_general(q_both, k_pair, (((1,), (1,)), ((), ())), preferred_element_type=F32)
    s_e, s_o = s[:rows], s[rows:]
    if bias_e is not None:
        s_e = s_e + bias_e
        s_o = s_o + bias_o
    mx_e = jnp.max(s_e, axis=-1, keepdims=True)
    mx_o = jnp.max(s_o, axis=-1, keepdims=True)
    p_e = jnp.exp2(s_e - mx_e).astype(BF16)
    p_o = jnp.exp2(s_o - mx_o).astype(BF16)
    lo_v = lax.broadcasted_iota(jnp.int32, v_pair.shape, 1) < HEAD_DIM
    one = jnp.ones_like(v_pair)
    r_e = jnp.dot(p_e, jnp.where(lo_v, v_pair, one), preferred_element_type=F32)
    r_o = jnp.dot(p_o, jnp.where(lo_v, one, v_pair), preferred_element_type=F32)
    return r_e, r_o, mx_e, mx_o


def _stat_lane(head):
    return head + HEAD_DIM if head % 2 == 0 else head


STAT_DEN_SHIFT = 16


DIL_UNITS = {1: 8, 4: 4, 16: 8}


def _dil_init_bias(rb_ref, sel_ref, mrow_ref, bias_ref):
    rb_hi, rb_lo = _split_hi_lo(rb_ref[...])
    f = (jnp.dot(rb_hi, sel_ref[...], preferred_element_type=F32)
         + jnp.dot(rb_lo, sel_ref[...], preferred_element_type=F32)) * LOG2E + mrow_ref[...]
    col = lax.broadcasted_iota(jnp.int32, (DIL_BLOCK, 2 * DIL_BLOCK), 1)
    for h in range(N_HEADS_DIL):
        row = jnp.broadcast_to(f[h:h + 1, :], (DIL_BLOCK, 2 * DIL_BLOCK))
        tab = pltpu.roll(row, 0, 1, stride=1, stride_axis=0)
        bias_ref[0, h] = tab
        bias_ref[1, h] = jnp.where(col >= DIL_BLOCK, tab, NEG_INF)


def _dil_unit(q, kp, kc, vp, vc, first, bias_ref, put_num):
    lane = lax.broadcasted_iota(jnp.int32, (DIL_BLOCK, LANES), 1)
    lo = lane < HEAD_DIM
    st_c = jnp.zeros((DIL_BLOCK, LANES), F32)
    for j in range(N_HEADS_DIL // 2):
        sl = slice(j * LANES, (j + 1) * LANES)
        if kp is None:
            k_pair, v_pair = kc(sl), vc(sl)
            bias_e, bias_o = bias_ref[0, 2 * j, :, DIL_BLOCK:], bias_ref[0, 2 * j + 1, :, DIL_BLOCK:]
        else:
            k_pair = jnp.concatenate([kp(sl), kc(sl)], axis=0)
            v_pair = jnp.concatenate([vp(sl), vc(sl)], axis=0)
            bias_e, bias_o = bias_ref[first, 2 * j], bias_ref[first, 2 * j + 1]
        r_e, r_o, mx_e, mx_o = _pair_attention(q(sl), k_pair, v_pair, bias_e, bias_o)
        put_num(j, jnp.where(lo, r_e, r_o))
        l_e, l_o = _stat_lane(2 * j), _stat_lane(2 * j + 1)
        st_c = jnp.where(lane == l_e, mx_e, jnp.where(lane == l_e + STAT_DEN_SHIFT, r_e, st_c))
        st_c = jnp.where(lane == l_o, mx_o, jnp.where(lane == l_o + STAT_DEN_SHIFT, r_o, st_c))
    return st_c


def _dil1_kernel(q_ref, kp_ref, kc_ref, vp_ref, vc_ref, rb_ref, sel_ref, mrow_ref,
                 o_ref, st_ref, bias_ref):
    @pl.when((pl.program_id(0) == 0) & (pl.program_id(1) == 0))
    def _():
        _dil_init_bias(rb_ref, sel_ref, mrow_ref, bias_ref)

    for u in range(DIL_UNITS[1]):
        rows = slice(u * DIL_BLOCK, (u + 1) * DIL_BLOCK)
        prev = slice((u - 1) * DIL_BLOCK, u * DIL_BLOCK)
        if u == 0:
            first = (pl.program_id(1) == 0).astype(jnp.int32)
            kp, vp = (lambda sl: kp_ref[:, sl]), (lambda sl: vp_ref[:, sl])
        else:
            first = 0
            kp, vp = (lambda sl, p=prev: kc_ref[p, sl]), (lambda sl, p=prev: vc_ref[p, sl])

        def put_num(j, num, rows=rows):
            o_ref[rows, j * LANES:(j + 1) * LANES] = num.astype(o_ref.dtype)

        st_ref[rows, :] = _dil_unit(lambda sl, r=rows: q_ref[r, sl], kp, lambda sl, r=rows: kc_ref[r, sl],
                                    vp, lambda sl, r=rows: vc_ref[r, sl], first, bias_ref, put_num)


DIL_LBLOCKS = {4: 2, 16: 2}


def _dil_kernel(*refs, dil, whole_seq):
    if whole_seq:
        q_ref, kc_ref, vc_ref, *rest = refs
        kp_ref = vp_ref = None
    else:
        q_ref, kp_ref, kc_ref, vp_ref, vc_ref, *rest = refs
    rb_ref, sel_ref, mrow_ref, o_ref, st_ref, bias_ref, acc_ref = rest
    b, lb, g = pl.program_id(0), pl.program_id(1), pl.program_id(2)
    nr, nl = DIL_UNITS[dil], DIL_LBLOCKS[dil]

    @pl.when((b == 0) & (lb == 0) & (g == 0))
    def _():
        _dil_init_bias(rb_ref, sel_ref, mrow_ref, bias_ref)

    for ll in range(nl):
        sub = slice(ll * DIL_BLOCK, (ll + 1) * DIL_BLOCK)
        prev = slice((ll - 1) * DIL_BLOCK, ll * DIL_BLOCK)
        first = (lb == 0).astype(jnp.int32) if ll == 0 else 0
        for u in range(nr):
            rows = pl.ds(ll * DIL_BLOCK * dil + g * nr + u, DIL_BLOCK, stride=dil)
            if ll == 0 and whole_seq:
                kp = vp = None
            elif ll == 0:
                kp, vp = (lambda sl, u=u: kp_ref[u, :, sl]), (lambda sl, u=u: vp_ref[u, :, sl])
            else:
                kp = lambda sl, u=u, p=prev: kc_ref[u, p, sl]
                vp = lambda sl, u=u, p=prev: vc_ref[u, p, sl]

            def put_num(j, num, rows=rows):
                acc_ref[j, rows, :] = num

            st_ref[rows, :] = _dil_unit(lambda sl, u=u, r=sub: q_ref[u, r, sl], kp,
                                        lambda sl, u=u, r=sub: kc_ref[u, r, sl], vp,
                                        lambda sl, u=u, r=sub: vc_ref[u, r, sl], first, bias_ref, put_num)

    @pl.when(g == dil // nr - 1)
    def _():
        for j in range(N_HEADS_DIL // 2):
            o_ref[:, j * LANES:(j + 1) * LANES] = acc_ref[j].astype(o_ref.dtype)


def _t5_bucket_np(dist):
    max_exact = REL_BUCKETS // 2
    n = np.maximum(dist, 1).astype(np.float32)
    large = max_exact + (np.log(n / np.float32(max_exact)) / np.float32(math.log(REL_MAX_DIST / max_exact))
                         * np.float32(REL_BUCKETS - max_exact)).astype(np.int32)
    large = np.minimum(large, REL_BUCKETS - 1)
    return np.where(dist < max_exact, dist, large)


def _bias_selector(window, dil):
    back = window // dil
    x = np.arange(2 * DIL_BLOCK)
    rel = DIL_BLOCK - x
    valid = (rel >= 0) & (rel <= back)
    bucket = _t5_bucket_np(np.clip(rel, 0, back) * dil)
    sel = np.zeros((LANES, 2 * DIL_BLOCK), np.float32)
    sel[bucket[valid], x[valid]] = 1.0
    mrow = np.where(valid, 0.0, NEG_INF).astype(np.float32)[None]
    return jnp.asarray(sel, BF16), jnp.asarray(mrow)


def _dilated(src, rb_t, batch, seq, window, dil):
    sub_len = seq // dil
    nb = sub_len // DIL_BLOCK
    sel, mrow = _bias_selector(window, dil)
    nu = DIL_UNITS[dil]
    consts = (rb_t, sel, mrow)
    const = lambda a: pl.BlockSpec(a.shape, lambda *_: (0, 0))
    stat_shape = jax.ShapeDtypeStruct((batch * seq, LANES), F32)
    out_shape = [jax.ShapeDtypeStruct((batch * seq, D_DIL), BF16), stat_shape]
    bias_scratch = pltpu.VMEM((2, N_HEADS_DIL, DIL_BLOCK, 2 * DIL_BLOCK), F32)

    if dil == 1:
        rows = nu * DIL_BLOCK
        steps = nb // nu
        cur = lambda col: pl.BlockSpec((None, None, rows, D_DIL), lambda b, l: (b, 0, l, col))
        prev = lambda col: pl.BlockSpec((None, None, DIL_BLOCK, D_DIL),
                                        lambda b, l: (b, 0, jnp.maximum(nu * l - 1, 0), col))
        out = lambda width: pl.BlockSpec((rows, width), lambda b, l: (b * steps + l, 0))
        return pl.pallas_call(
            _dil1_kernel,
            grid=(batch, steps),
            in_specs=[cur(0), prev(1), cur(1), prev(2), cur(2)] + [const(a) for a in consts],
            out_specs=[out(D_DIL), out(LANES)],
            out_shape=out_shape,
            scratch_shapes=[bias_scratch],
            compiler_params=_cparams(("arbitrary", "arbitrary")),
            name="dilated_d1",
        )(src, src, src, src, src, *consts)

    nl = DIL_LBLOCKS[dil]
    steps = nb // nl
    rows = nl * DIL_BLOCK * dil
    cur = lambda col: pl.BlockSpec((None, nu, nl * DIL_BLOCK, D_DIL), lambda b, l, g: (b, g, l, col))
    prev = lambda col: pl.BlockSpec((None, nu, DIL_BLOCK, D_DIL),
                                    lambda b, l, g: (b, g, jnp.maximum(nl * l - 1, 0), col))
    out = lambda width: pl.BlockSpec((rows, width), lambda b, l, g: (b * steps + l, 0))
    whole_seq = steps == 1
    qkv_specs = [cur(0), cur(1), cur(2)] if whole_seq else [cur(0), prev(1), cur(1), prev(2), cur(2)]
    return pl.pallas_call(
        functools.partial(_dil_kernel, dil=dil, whole_seq=whole_seq),
        grid=(batch, steps, dil // nu),
        in_specs=qkv_specs + [const(a) for a in consts],
        out_specs=[out(D_DIL), out(LANES)],
        out_shape=out_shape,
        scratch_shapes=[bias_scratch, pltpu.VMEM((N_HEADS_DIL // 2, rows, LANES), F32)],
        compiler_params=_cparams(("arbitrary", "arbitrary", "arbitrary")),
        name=f"dilated_d{dil}",
    )(*([src] * len(qkv_specs)), *consts)


GLA_TM = 256
GLA_STEP_TILES = 2


GLA_NC = GLA_TM // GLA_CHUNK


def _gla_tile(bb, rows, w2_hi, w2_lo, q_ref, k_ref, v_ref, g_ref, bg_ref, nw_ref, tri_ref, o_ref, st_ref):
    g = g_ref[bb, rows, :LANES]
    z = (jnp.dot(g, w2_hi, preferred_element_type=F32) + jnp.dot(g, w2_lo, preferred_element_type=F32)
         + bg_ref[...])
    la = (jnp.minimum(z, 0.0) * (LOG2E / GLA_TAU)
          - jnp.log2(1.0 + jnp.exp2(jnp.abs(z) * -LOG2E)) * (1.0 / GLA_TAU))
    la_h = la.astype(BF16)
    la_l = (la - la_h.astype(F32)).astype(BF16)

    tri = tri_ref[...]
    b = jnp.dot(tri, la_h, preferred_element_type=F32) + jnp.dot(tri, la_l, preferred_element_type=F32)
    b_last = jnp.concatenate(
        [jnp.broadcast_to(b[(ci + 1) * GLA_CHUNK - 1:(ci + 1) * GLA_CHUNK, :], (GLA_CHUNK, D_GLA_K))
         for ci in range(GLA_NC)], axis=0)

    q = q_ref[bb, rows].astype(F32)
    k = k_ref[bb, rows].astype(F32)
    q_t = (q * jnp.exp2(b)).astype(BF16)
    k_t = (k * jnp.exp2(-b)).astype(BF16)
    k_s = k * jnp.exp2(b_last - b)

    causal = tri.astype(F32) > 0.0
    t_chunk = lax.broadcasted_iota(jnp.int32, (GLA_DK, GLA_TM), 1) // GLA_CHUNK
    nw = nw_ref[...]
    for h in range(N_HEADS_GLA):
        cs = slice(h * GLA_DK, (h + 1) * GLA_DK)
        vs = slice(h * GLA_DV, (h + 1) * GLA_DV)
        v_h = v_ref[bb, rows, vs]
        a = lax.dot_general(q_t[:, cs], k_t[:, cs], (((1,), (1,)), ((), ())), preferred_element_type=F32)
        o = jnp.dot(jnp.where(causal, a, 0.0).astype(BF16), v_h, preferred_element_type=F32)
        k_s_t = k_s[:, cs].T.astype(BF16)
        zero = jnp.zeros_like(k_s_t)
        kv_all = jnp.dot(jnp.concatenate([jnp.where(t_chunk == ci, k_s_t, zero) for ci in range(GLA_NC)], axis=0),
                         v_h, preferred_element_type=F32)
        decay_t = jnp.exp2(b_last[:, cs].T)
        st = st_ref[bb, h]
        outs = []
        for ci in range(GLA_NC):
            rs = slice(ci * GLA_CHUNK, (ci + 1) * GLA_CHUNK)
            o_c = o[rs] + jnp.dot(q_t[rs, cs], st.astype(BF16), preferred_element_type=F32)
            decay = jnp.broadcast_to(decay_t[:, ci * GLA_CHUNK:ci * GLA_CHUNK + 1], (GLA_DK, GLA_DV))
            st = decay * st + kv_all[ci * GLA_DK:(ci + 1) * GLA_DK]
            ms = jnp.mean(o_c * o_c, axis=-1, keepdims=True)
            outs.append((o_c * lax.rsqrt(ms + EPS) * nw).astype(o_ref.dtype))
        st_ref[bb, h] = st
        o_ref[bb, rows, vs] = jnp.concatenate(outs, axis=0)


def _gla_kernel(q_ref, k_ref, v_ref, g_ref, w2_ref, bg_ref, nw_ref, tri_ref, wout_ref,
                o_ref, wout_bf_ref, st_ref):

    @pl.when(pl.program_id(0) == 0)
    def _():
        st_ref[...] = jnp.zeros_like(st_ref)

    wout_bf_ref[...] = wout_ref[...].astype(wout_bf_ref.dtype)

    w2_hi, w2_lo = _split_hi_lo(w2_ref[...])
    for tt in range(GLA_STEP_TILES):
        rows = slice(tt * GLA_TM, (tt + 1) * GLA_TM)
        for bb in range(st_ref.shape[0]):
            _gla_tile(bb, rows, w2_hi, w2_lo, q_ref, k_ref, v_ref, g_ref, bg_ref, nw_ref, tri_ref, o_ref, st_ref)


def _gla(proj, w2, b_gate, norm_w, w_out, batch, seq):
    step_rows = GLA_TM * GLA_STEP_TILES
    steps = seq // step_rows
    wout_rows = w_out.shape[0] // steps
    pos = np.arange(GLA_TM)
    tri = jnp.asarray((pos[:, None] // GLA_CHUNK == pos[None, :] // GLA_CHUNK) & (pos[None, :] <= pos[:, None]),
                      BF16)
    proj3 = proj.reshape(batch, seq, REST_W)
    rows = lambda width, col: pl.BlockSpec((batch, step_rows, width), lambda i: (0, i, col // width))
    const = lambda shape: pl.BlockSpec(shape, lambda i: (0, 0))
    wout_spec = pl.BlockSpec((wout_rows, w_out.shape[1]), lambda i: (i, 0))
    out, w_out_bf = pl.pallas_call(
        _gla_kernel,
        grid=(steps,),
        in_specs=[
            rows(D_GLA_K, COL_GQ), rows(D_GLA_K, COL_GK), rows(D_GLA_V, COL_GV), rows(GLR_PAD, COL_GLR),
            const((LANES, D_GLA_K)), const((1, D_GLA_K)), const((1, GLA_DV)),
            const((GLA_TM, GLA_TM)), wout_spec,
        ],
        out_specs=[rows(D_GLA_V, 0), wout_spec],
        out_shape=[jax.ShapeDtypeStruct((batch, seq, D_GLA_V), BF16), jax.ShapeDtypeStruct(w_out.shape, BF16)],
        scratch_shapes=[pltpu.VMEM((batch, N_HEADS_GLA, GLA_DK, GLA_DV), F32)],
        compiler_params=_cparams(("arbitrary",)),
        name="gla",
    )(proj3, proj3, proj3, proj3, w2, b_gate, norm_w, tri, w_out)
    return out.reshape(batch * seq, D_GLA_V), w_out_bf


MEM_TM = 2048
MEM_UNIT = 128


def _mem_kernel(q_ref, mem_ref, mnw_ref, wkv_ref, o_ref, den_ref, kv_ref):
    @pl.when(pl.program_id(1) == 0)
    def _():
        m = mem_ref[0]
        ms = jnp.mean(m * m, axis=-1, keepdims=True)
        mn = (m * lax.rsqrt(ms + EPS) * mnw_ref[...]).astype(BF16)
        kv_ref[...] = jnp.dot(mn, wkv_ref[...].astype(BF16), preferred_element_type=F32).astype(BF16)

    lane = lax.broadcasted_iota(jnp.int32, (MEM_UNIT, LANES), 1)
    lo = lane < HEAD_DIM
    for u in range(MEM_TM // MEM_UNIT):
        rows = slice(u * MEM_UNIT, (u + 1) * MEM_UNIT)
        den_c = jnp.ones((MEM_UNIT, LANES), F32)
        for j in range(N_HEADS_MEM // 2):
            sl = slice(j * LANES, (j + 1) * LANES)
            r_e, r_o, _, _ = _pair_attention(q_ref[rows, sl], kv_ref[:, sl],
                                             kv_ref[:, D_MEM + j * LANES:D_MEM + (j + 1) * LANES], None, None)
            o_ref[rows, sl] = jnp.where(lo, r_e, r_o).astype(o_ref.dtype)
            den_c = jnp.where(lane == _stat_lane(2 * j), r_e, jnp.where(lane == _stat_lane(2 * j + 1), r_o, den_c))
        den_ref[rows, :] = den_c


def _mem_attn(proj, mem, mem_norm_w, w_kv, batch, seq):
    steps = seq // MEM_TM
    rows = lambda width, col=0: pl.BlockSpec((MEM_TM, width), lambda b, i: (b * steps + i, col // width))
    return pl.pallas_call(
        _mem_kernel,
        grid=(batch, steps),
        in_specs=[
            rows(D_MEM, COL_MQ),
            pl.BlockSpec((1, MEM_LEN, D_MODEL), lambda b, i: (b, 0, 0)),
            pl.BlockSpec((1, D_MODEL), lambda b, i: (0, 0)),
            pl.BlockSpec((D_MODEL, 2 * D_MEM), lambda b, i: (0, 0)),
        ],
        out_specs=[rows(D_MEM), rows(LANES)],
        out_shape=[jax.ShapeDtypeStruct((batch * seq, D_MEM), BF16),
                   jax.ShapeDtypeStruct((batch * seq, LANES), F32)],
        scratch_shapes=[pltpu.VMEM((MEM_LEN, 2 * D_MEM), BF16)],
        compiler_params=_cparams(("parallel", "arbitrary")),
        name="mem_attn",
    )(proj, mem, mem_norm_w, w_kv)


OUT_TM = 512


def _silu(g):
    h = g * 0.5
    return h * jnp.tanh(h) + h


OUT_GW = 256


def _out_kernel(n1_ref, n2_ref, n3_ref, s1_ref, s2_ref, s3_ref, og_ref, nm_ref, dm_ref, ga_ref, gg_ref, gm_ref,
                sm_ref, ex_ref, exm_ref, w_ref, x_ref, nw_ref, out_ref, mix_ref):
    keep = sm_ref[0:1, :]
    fill = sm_ref[1:2, :]
    stats = [s_ref[...] for s_ref in (s1_ref, s2_ref, s3_ref)]
    m1, m2, m3 = [st * keep for st in stats]
    d1, d2, d3 = [pltpu.roll(st, LANES - STAT_DEN_SHIFT, 1) * keep + fill for st in stats]
    mx = jnp.maximum(jnp.maximum(m1, m2), m3)
    t1, t2, t3 = jnp.exp2(m1 - mx), jnp.exp2(m2 - mx), jnp.exp2(m3 - mx)
    inv = 1.0 / (t1 * d1 + t2 * d2 + t3 * d3)
    wts = [(t * inv).astype(BF16) for t in (t1, t2, t3)]
    inv_m = (1.0 / dm_ref[...]).astype(BF16)

    def gate_chunk(lo):
        cols = slice(lo, lo + OUT_GW)
        if lo < D_DIL:
            a = jnp.zeros((OUT_TM, OUT_GW), F32)
            for wt, n_ref in zip(wts, (n1_ref, n2_ref, n3_ref)):
                a = a + jnp.dot(wt, ex_ref[:, cols], preferred_element_type=F32) * n_ref[:, cols].astype(F32)
            g = ga_ref[:, cols]
        elif lo < D_DIL + D_GLA_V:
            lc = slice(lo - D_DIL, lo - D_DIL + OUT_GW)
            a, g = og_ref[:, lc], gg_ref[:, lc]
        else:
            a = jnp.dot(inv_m, exm_ref[...], preferred_element_type=F32) * nm_ref[...].astype(F32)
            g = gm_ref[...]
        mix_ref[:, cols] = a.astype(BF16) * _silu(g)

    for lo in range(0, D_MIX, OUT_GW):
        gate_chunk(lo)
    y = jnp.dot(mix_ref[...], w_ref[...], preferred_element_type=F32)
    ms = jnp.mean(y * y, axis=-1, keepdims=True)
    out_ref[...] = x_ref[...] + y * lax.rsqrt(ms + EPS) * nw_ref[...]


def _out_proj(pats, o_gla, mem_pair, proj, stat_mask, expand, expand_mem, w_out, x2d, norm_w):
    m = x2d.shape[0]
    rows = lambda width, col=0: pl.BlockSpec((OUT_TM, width), lambda s: (s, col // width))
    const = lambda shape: pl.BlockSpec(shape, lambda s: (0, 0), pipeline_mode=pl.Buffered(1))
    nums, stats = zip(*pats)
    return pl.pallas_call(
        _out_kernel,
        grid=(m // OUT_TM,),
        in_specs=[
            *([rows(D_DIL)] * 3), *([rows(LANES)] * 3),
            rows(D_GLA_V), rows(D_MEM), rows(LANES),
            rows(D_DIL, COL_GATE_A), rows(D_GLA_V, COL_GATE_G), rows(D_MEM, COL_GATE_M),
            const((8, LANES)), const((LANES, D_DIL)), const((LANES, D_MEM)), const((D_MIX, D_MODEL)),
            rows(D_MODEL), const((1, D_MODEL)),
        ],
        out_specs=rows(D_MODEL),
        out_shape=jax.ShapeDtypeStruct((m, D_MODEL), x2d.dtype),
        scratch_shapes=[pltpu.VMEM((OUT_TM, D_MIX), BF16)],
        compiler_params=_cparams(("parallel",)),
        name="out_proj",
    )(*nums, *stats, o_gla, *mem_pair, proj, proj, proj, stat_mask, expand, expand_mem, w_out, x2d, norm_w)


D_IN_PROJ = 3 * D_DIL + 2 * D_GLA_K + D_GLA_V + GLA_GATE_RANK + D_MEM + D_MIX
SRC_GLR = 3 * D_DIL + 2 * D_GLA_K + D_GLA_V
SRC_MQ = SRC_GLR + GLA_GATE_RANK
SRC_GATE = SRC_MQ + D_MEM
SCALE_ATTN_Q = HEAD_DIM ** -0.5 * LOG2E
SCALE_GLA_Q = GLA_DK ** -0.5
QKV_SEGMENTS = (
    (0, D_DIL, 0, SCALE_ATTN_Q),
    (D_DIL, 2 * D_DIL, D_DIL, 1.0),
)
REST_SEGMENTS = (
    (QKV_W, D_GLA_K, COL_GQ, SCALE_GLA_Q),
    (QKV_W + D_GLA_K, D_GLA_K + D_GLA_V, COL_GK, 1.0),
    (SRC_GATE + D_DIL, D_GLA_V, COL_GATE_G, 1.0),
    (SRC_MQ, D_MEM, COL_MQ, SCALE_ATTN_Q),
    (SRC_GATE + D_DIL + D_GLA_V, D_MEM, COL_GATE_M, 1.0),
    (SRC_GATE, D_DIL, COL_GATE_A, 1.0),
)
WPREP_TK = 512


def _wprep_kernel(wt_ref, qkv_ref, rest_ref):
    for o_ref, segments in ((qkv_ref, QKV_SEGMENTS), (rest_ref, REST_SEGMENTS)):
        for src, width, dst, scale in segments:
            for lo in range(0, width, LANES):
                w = wt_ref[src + lo:src + lo + LANES, :].T
                if scale != 1.0:
                    w = w * scale
                o_ref[:, dst + lo:dst + lo + LANES] = w.astype(o_ref.dtype)
    glr = wt_ref[SRC_GLR:SRC_GLR + LANES, :].T
    keep = lax.broadcasted_iota(jnp.int32, glr.shape, 1) < GLA_GATE_RANK
    rest_ref[:, COL_GLR:COL_GLR + LANES] = jnp.where(keep, glr, 0.0).astype(rest_ref.dtype)
    rest_ref[:, COL_GLR + LANES:COL_GLR + GLR_PAD] = jnp.zeros((WPREP_TK, GLR_PAD - LANES), rest_ref.dtype)


def _regroup_in_weights(w_in):
    return pl.pallas_call(
        _wprep_kernel,
        grid=(D_MODEL // WPREP_TK,),
        in_specs=[pl.BlockSpec((D_IN_PROJ, WPREP_TK), lambda i: (0, i))],
        out_specs=[pl.BlockSpec((WPREP_TK, QKV_W), lambda i: (i, 0)),
                   pl.BlockSpec((WPREP_TK, REST_W), lambda i: (i, 0))],
        out_shape=[jax.ShapeDtypeStruct((D_MODEL, QKV_W), BF16), jax.ShapeDtypeStruct((D_MODEL, REST_W), BF16)],
        compiler_params=_cparams(("parallel",)),
        name="weight_regroup",
    )(w_in.T)


def kernel(x, mem, norm_pre_w, w_in, rel_bias, w_gla_gate2, b_gla_gate, gla_norm_w, mem_norm_w, w_mem_kv,
           w_out, norm_post_w):
    batch, seq, _ = x.shape
    depth = w_in.shape[0]
    expand_np = np.zeros((LANES, D_DIL), np.float32)
    for h in range(N_HEADS_DIL):
        expand_np[_stat_lane(h), h * HEAD_DIM:(h + 1) * HEAD_DIM] = 1.0
    expand = jnp.asarray(expand_np, BF16)
    expand_mem = jnp.asarray(expand_np[:, :D_MEM], BF16)
    stat_mask_np = np.zeros((8, LANES), np.float32)
    stat_mask_np[0, [_stat_lane(h) for h in range(N_HEADS_DIL)]] = 1.0
    stat_mask_np[1] = 1.0 - stat_mask_np[0]
    stat_mask = jnp.asarray(stat_mask_np)
    rb_t = jnp.zeros((16, LANES), F32).at[:N_HEADS_DIL, :REL_BUCKETS].set(rel_bias.astype(F32).T)
    for l in range(depth):
        x2d = x.reshape(batch * seq, D_MODEL)
        w_qkv, w_rest = _regroup_in_weights(w_in[l])
        h, qkv1, qkv4, qkv16 = _qkv_proj(x2d, norm_pre_w[l][None], w_qkv, batch, seq)
        rest = _rest_proj(h, w_rest)

        srcs = (qkv1.reshape(batch, 1, seq, QKV_W), qkv4, qkv16)
        pats = [_dilated(src, rb_t, batch, seq, window, dil)
                for (window, dil), src in zip(DIL_PATTERNS, srcs)]

        w2 = jnp.zeros((LANES, D_GLA_K), F32).at[:GLA_GATE_RANK].set(w_gla_gate2[l])
        o_gla, w_out_bf = _gla(rest, w2, b_gla_gate[l][None], gla_norm_w[l][None], w_out[l],
                               batch, seq)

        mem_pair = _mem_attn(rest, mem, mem_norm_w[l][None], w_mem_kv[l], batch, seq)

        out = _out_proj(pats, o_gla, mem_pair, rest, stat_mask, expand, expand_mem, w_out_bf, x2d,
                        norm_post_w[l][None])
        x = out.reshape(batch, seq, D_MODEL)
    return x
```

```python
import functools
import math

import numpy as np
import jax
import jax.numpy as jnp
from jax import lax
from jax.experimental import pallas as pl
from jax.experimental.pallas import tpu as pltpu

F32 = jnp.float32
BF16 = jnp.bfloat16

D_MODEL = 2048
HEAD_DIM = 64
N_HEADS_DIL = 12
DIL_PATTERNS = ((128, 1), (512, 4), (2048, 16))
DIL_BLOCK = 128
N_HEADS_GLA = 4
GLA_DK = 128
GLA_DV = 256
GLA_GATE_RANK = 16
GLA_TAU = 16.0
GLA_CHUNK = 64
N_HEADS_MEM = 4
MEM_LEN = 256
REL_BUCKETS = 32
REL_MAX_DIST = 2048
EPS = 1e-6
NEG_INF = -1e30

D_DIL = N_HEADS_DIL * HEAD_DIM
D_GLA_K = N_HEADS_GLA * GLA_DK
D_GLA_V = N_HEADS_GLA * GLA_DV
D_MEM = N_HEADS_MEM * HEAD_DIM
D_MIX = D_DIL + D_GLA_V + D_MEM

LANES = 128
GLR_PAD = 256

QKV_W = 3 * D_DIL
COL_GQ = 0
COL_GK = COL_GQ + D_GLA_K
COL_GV = COL_GK + D_GLA_K
COL_GATE_G = COL_GV + D_GLA_V
COL_MQ = COL_GATE_G + D_GLA_V
COL_GATE_M = COL_MQ + D_MEM
COL_GLR = COL_GATE_M + D_MEM
COL_GATE_A = COL_GLR + GLR_PAD
REST_W = COL_GATE_A + D_DIL

VMEM_LIMIT = 56 * 1024 * 1024


def _cparams(sem):
    return pltpu.CompilerParams(dimension_semantics=sem, vmem_limit_bytes=VMEM_LIMIT)


def _split_hi_lo(w):
    hi = w.astype(BF16)
    return hi, (w - hi.astype(F32)).astype(BF16)


QKV_TM = 512
N_SLABS = D_DIL // LANES
REGROUP_DILS = (4, 16)


def _qkv_proj_kernel(x_ref, nw_ref, w_ref, h_ref, o_ref, x4_ref, x16_ref, slab_ref):
    x = x_ref[...]
    ms = jnp.mean(x * x, axis=-1, keepdims=True)
    h_ref[...] = (x * lax.rsqrt(ms + EPS) * nw_ref[...]).astype(BF16)
    for t in range(QKV_W // D_DIL):
        cols = slice(t * D_DIL, (t + 1) * D_DIL)
        res = jnp.dot(h_ref[...], w_ref[:, cols], preferred_element_type=F32)
        o_ref[:, cols] = res.astype(o_ref.dtype)
        for s in range(N_SLABS):
            slab_ref[t, s] = res[:, s * LANES:(s + 1) * LANES]
        for out_ref, dil in zip((x4_ref, x16_ref), REGROUP_DILS):
            for r in range(dil):
                for s in range(N_SLABS):
                    lo = t * D_DIL + s * LANES
                    out_ref[0, r, :, lo:lo + LANES] = (
                        slab_ref[t, s, pl.ds(r, QKV_TM // dil, stride=dil), :].astype(out_ref.dtype))


def _qkv_proj(x2d, norm_w, w_qkv, batch, seq):
    m = x2d.shape[0]
    tiles_per_seq = seq // QKV_TM
    const = lambda shape: pl.BlockSpec(shape, lambda i: (0, 0), pipeline_mode=pl.Buffered(1))
    grouped = lambda dil: pl.BlockSpec((1, dil, QKV_TM // dil, QKV_W),
                                       lambda i: (i // tiles_per_seq, 0, i % tiles_per_seq, 0))
    return pl.pallas_call(
        _qkv_proj_kernel,
        grid=(m // QKV_TM,),
        in_specs=[
            pl.BlockSpec((QKV_TM, D_MODEL), lambda i: (i, 0)),
            const((1, D_MODEL)), const((D_MODEL, QKV_W)),
        ],
        out_specs=[pl.BlockSpec((QKV_TM, D_MODEL), lambda i: (i, 0)),
                   pl.BlockSpec((QKV_TM, QKV_W), lambda i: (i, 0))] + [grouped(d) for d in REGROUP_DILS],
        out_shape=[jax.ShapeDtypeStruct((m, D_MODEL), BF16), jax.ShapeDtypeStruct((m, QKV_W), BF16)]
        + [jax.ShapeDtypeStruct((batch, d, seq // d, QKV_W), BF16) for d in REGROUP_DILS],
        scratch_shapes=[pltpu.VMEM((QKV_W // D_DIL, N_SLABS, QKV_TM, LANES), F32)],
        compiler_params=_cparams(("parallel",)),
        name="qkv_proj",
    )(x2d, norm_w, w_qkv)


REST_TM = 512


def _rest_proj_kernel(h_ref, w_ref, o_ref):
    o_ref[...] = jnp.dot(h_ref[...], w_ref[...], preferred_element_type=F32).astype(o_ref.dtype)


def _rest_proj(h, w_rest):
    m = h.shape[0]
    const = lambda shape: pl.BlockSpec(shape, lambda i: (0, 0), pipeline_mode=pl.Buffered(1))
    return pl.pallas_call(
        _rest_proj_kernel,
        grid=(m // REST_TM,),
        in_specs=[pl.BlockSpec((REST_TM, D_MODEL), lambda i: (i, 0)), const((D_MODEL, REST_W))],
        out_specs=pl.BlockSpec((REST_TM, REST_W), lambda i: (i, 0)),
        out_shape=jax.ShapeDtypeStruct((m, REST_W), BF16),
        compiler_params=_cparams(("parallel",)),
        name="rest_proj",
    )(h, w_rest)


LOG2E = math.log2(math.e)


def _pair_attention(q_pair, k_pair, v_pair, bias_e, bias_o):
    rows = q_pair.shape[0]
    lo_q = lax.broadcasted_iota(jnp.int32, q_pair.shape, 1) < HEAD_DIM
    zero = jnp.zeros_like(q_pair)
    q_both = jnp.concatenate([jnp.where(lo_q, q_pair, zero), jnp.where(lo_q, zero, q_pair)], axis=0)
    s = lax.dot_general(q_both, k_pair, (((1,), (1,)), ((), ())), preferred_element_type=F32)
    s_e, s_o = s[:rows], s[rows:]
    if bias_e is not None:
        s_e = s_e + bias_e
        s_o = s_o + bias_o
    mx_e = jnp.max(s_e, axis=-1, keepdims=True)
    mx_o = jnp.max(s_o, axis=-1, keepdims=True)
    p_e = jnp.exp2(s_e - mx_e).astype(BF16)
    p_o = jnp.exp2(s_o - mx_o).astype(BF16)
    lo_v = lax.broadcasted_iota(jnp.int32, v_pair.shape, 1) < HEAD_DIM
    one = jnp.ones_like(v_pair)
    r_e = jnp.dot(p_e, jnp.where(lo_v, v_pair, one), preferred_element_type=F32)
    r_o = jnp.dot(p_o, jnp.where(lo_v, one, v_pair), preferred_element_type=F32)
    return r_e, r_o, mx_e, mx_o


def _stat_lane(head):
    return head + HEAD_DIM if head % 2 == 0 else head


STAT_DEN_SHIFT = 16


DIL_UNITS = {1: 8, 4: 4, 16: 8}


def _dil_init_bias(rb_ref, sel_ref, mrow_ref, bias_ref):
    rb_hi, rb_lo = _split_hi_lo(rb_ref[...])
    f = (jnp.dot(rb_hi, sel_ref[...], preferred_element_type=F32)
         + jnp.dot(rb_lo, sel_ref[...], preferred_element_type=F32)) * LOG2E + mrow_ref[...]
    col = lax.broadcasted_iota(jnp.int32, (DIL_BLOCK, 2 * DIL_BLOCK), 1)
    for h in range(N_HEADS_DIL):
        row = jnp.broadcast_to(f[h:h + 1, :], (DIL_BLOCK, 2 * DIL_BLOCK))
        tab = pltpu.roll(row, 0, 1, stride=1, stride_axis=0)
        bias_ref[0, h] = tab
        bias_ref[1, h] = jnp.where(col >= DIL_BLOCK, tab, NEG_INF)


def _dil_unit(q, kp, kc, vp, vc, first, bias_ref, put_num):
    lane = lax.broadcasted_iota(jnp.int32, (DIL_BLOCK, LANES), 1)
    lo = lane < HEAD_DIM
    st_c = jnp.zeros((DIL_BLOCK, LANES), F32)
    for j in range(N_HEADS_DIL // 2):
        sl = slice(j * LANES, (j + 1) * LANES)
        if kp is None:
            k_pair, v_pair = kc(sl), vc(sl)
            bias_e, bias_o = bias_ref[0, 2 * j, :, DIL_BLOCK:], bias_ref[0, 2 * j + 1, :, DIL_BLOCK:]
        else:
            k_pair = jnp.concatenate([kp(sl), kc(sl)], axis=0)
            v_pair = jnp.concatenate([vp(sl), vc(sl)], axis=0)
            bias_e, bias_o = bias_ref[first, 2 * j], bias_ref[first, 2 * j + 1]
        r_e, r_o, mx_e, mx_o = _pair_attention(q(sl), k_pair, v_pair, bias_e, bias_o)
        put_num(j, jnp.where(lo, r_e, r_o))
        l_e, l_o = _stat_lane(2 * j), _stat_lane(2 * j + 1)
        st_c = jnp.where(lane == l_e, mx_e, jnp.where(lane == l_e + STAT_DEN_SHIFT, r_e, st_c))
        st_c = jnp.where(lane == l_o, mx_o, jnp.where(lane == l_o + STAT_DEN_SHIFT, r_o, st_c))
    return st_c


def _dil1_kernel(q_ref, kp_ref, kc_ref, vp_ref, vc_ref, rb_ref, sel_ref, mrow_ref,
                 o_ref, st_ref, bias_ref):
    @pl.when((pl.program_id(0) == 0) & (pl.program_id(1) == 0))
    def _():
        _dil_init_bias(rb_ref, sel_ref, mrow_ref, bias_ref)

    for u in range(DIL_UNITS[1]):
        rows = slice(u * DIL_BLOCK, (u + 1) * DIL_BLOCK)
        prev = slice((u - 1) * DIL_BLOCK, u * DIL_BLOCK)
        if u == 0:
            first = (pl.program_id(1) == 0).astype(jnp.int32)
            kp, vp = (lambda sl: kp_ref[:, sl]), (lambda sl: vp_ref[:, sl])
        else:
            first = 0
            kp, vp = (lambda sl, p=prev: kc_ref[p, sl]), (lambda sl, p=prev: vc_ref[p, sl])

        def put_num(j, num, rows=rows):
            o_ref[rows, j * LANES:(j + 1) * LANES] = num.astype(o_ref.dtype)

        st_ref[rows, :] = _dil_unit(lambda sl, r=rows: q_ref[r, sl], kp, lambda sl, r=rows: kc_ref[r, sl],
                                    vp, lambda sl, r=rows: vc_ref[r, sl], first, bias_ref, put_num)


DIL_LBLOCKS = {4: 2, 16: 2}


def _dil_kernel(*refs, dil, whole_seq):
    if whole_seq:
        q_ref, kc_ref, vc_ref, *rest = refs
        kp_ref = vp_ref = None
    else:
        q_ref, kp_ref, kc_ref, vp_ref, vc_ref, *rest = refs
    rb_ref, sel_ref, mrow_ref, o_ref, st_ref, bias_ref, acc_ref = rest
    b, lb, g = pl.program_id(0), pl.program_id(1), pl.program_id(2)
    nr, nl = DIL_UNITS[dil], DIL_LBLOCKS[dil]

    @pl.when((b == 0) & (lb == 0) & (g == 0))
    def _():
        _dil_init_bias(rb_ref, sel_ref, mrow_ref, bias_ref)

    for ll in range(nl):
        sub = slice(ll * DIL_BLOCK, (ll + 1) * DIL_BLOCK)
        prev = slice((ll - 1) * DIL_BLOCK, ll * DIL_BLOCK)
        first = (lb == 0).astype(jnp.int32) if ll == 0 else 0
        for u in range(nr):
            rows = pl.ds(ll * DIL_BLOCK * dil + g * nr + u, DIL_BLOCK, stride=dil)
            if ll == 0 and whole_seq:
                kp = vp = None
            elif ll == 0:
                kp, vp = (lambda sl, u=u: kp_ref[u, :, sl]), (lambda sl, u=u: vp_ref[u, :, sl])
            else:
                kp = lambda sl, u=u, p=prev: kc_ref[u, p, sl]
                vp = lambda sl, u=u, p=prev: vc_ref[u, p, sl]

            def put_num(j, num, rows=rows):
                acc_ref[j, rows, :] = num

            st_ref[rows, :] = _dil_unit(lambda sl, u=u, r=sub: q_ref[u, r, sl], kp,
                                        lambda sl, u=u, r=sub: kc_ref[u, r, sl], vp,
                                        lambda sl, u=u, r=sub: vc_ref[u, r, sl], first, bias_ref, put_num)

    @pl.when(g == dil // nr - 1)
    def _():
        for j in range(N_HEADS_DIL // 2):
            o_ref[:, j * LANES:(j + 1) * LANES] = acc_ref[j].astype(o_ref.dtype)


def _t5_bucket_np(dist):
    max_exact = REL_BUCKETS // 2
    n = np.maximum(dist, 1).astype(np.float32)
    large = max_exact + (np.log(n / np.float32(max_exact)) / np.float32(math.log(REL_MAX_DIST / max_exact))
                         * np.float32(REL_BUCKETS - max_exact)).astype(np.int32)
    large = np.minimum(large, REL_BUCKETS - 1)
    return np.where(dist < max_exact, dist, large)


def _bias_selector(window, dil):
    back = window // dil
    x = np.arange(2 * DIL_BLOCK)
    rel = DIL_BLOCK - x
    valid = (rel >= 0) & (rel <= back)
    bucket = _t5_bucket_np(np.clip(rel, 0, back) * dil)
    sel = np.zeros((LANES, 2 * DIL_BLOCK), np.float32)
    sel[bucket[valid], x[valid]] = 1.0
    mrow = np.where(valid, 0.0, NEG_INF).astype(np.float32)[None]
    return jnp.asarray(sel, BF16), jnp.asarray(mrow)


def _dilated(src, rb_t, batch, seq, window, dil):
    sub_len = seq // dil
    nb = sub_len // DIL_BLOCK
    sel, mrow = _bias_selector(window, dil)
    nu = DIL_UNITS[dil]
    consts = (rb_t, sel, mrow)
    const = lambda a: pl.BlockSpec(a.shape, lambda *_: (0, 0))
    stat_shape = jax.ShapeDtypeStruct((batch * seq, LANES), F32)
    out_shape = [jax.ShapeDtypeStruct((batch * seq, D_DIL), BF16), stat_shape]
    bias_scratch = pltpu.VMEM((2, N_HEADS_DIL, DIL_BLOCK, 2 * DIL_BLOCK), F32)

    if dil == 1:
        rows = nu * DIL_BLOCK
        steps = nb // nu
        cur = lambda col: pl.BlockSpec((None, None, rows, D_DIL), lambda b, l: (b, 0, l, col))
        prev = lambda col: pl.BlockSpec((None, None, DIL_BLOCK, D_DIL),
                                        lambda b, l: (b, 0, jnp.maximum(nu * l - 1, 0), col))
        out = lambda width: pl.BlockSpec((rows, width), lambda b, l: (b * steps + l, 0))
        return pl.pallas_call(
            _dil1_kernel,
            grid=(batch, steps),
            in_specs=[cur(0), prev(1), cur(1), prev(2), cur(2)] + [const(a) for a in consts],
            out_specs=[out(D_DIL), out(LANES)],
            out_shape=out_shape,
            scratch_shapes=[bias_scratch],
            compiler_params=_cparams(("arbitrary", "arbitrary")),
            name="dilated_d1",
        )(src, src, src, src, src, *consts)

    nl = DIL_LBLOCKS[dil]
    steps = nb // nl
    rows = nl * DIL_BLOCK * dil
    cur = lambda col: pl.BlockSpec((None, nu, nl * DIL_BLOCK, D_DIL), lambda b, l, g: (b, g, l, col))
    prev = lambda col: pl.BlockSpec((None, nu, DIL_BLOCK, D_DIL),
                                    lambda b, l, g: (b, g, jnp.maximum(nl * l - 1, 0), col))
    out = lambda width: pl.BlockSpec((rows, width), lambda b, l, g: (b * steps + l, 0))
    whole_seq = steps == 1
    qkv_specs = [cur(0), cur(1), cur(2)] if whole_seq else [cur(0), prev(1), cur(1), prev(2), cur(2)]
    return pl.pallas_call(
        functools.partial(_dil_kernel, dil=dil, whole_seq=whole_seq),
        grid=(batch, steps, dil // nu),
        in_specs=qkv_specs + [const(a) for a in consts],
        out_specs=[out(D_DIL), out(LANES)],
        out_shape=out_shape,
        scratch_shapes=[bias_scratch, pltpu.VMEM((N_HEADS_DIL // 2, rows, LANES), F32)],
        compiler_params=_cparams(("arbitrary", "arbitrary", "arbitrary")),
        name=f"dilated_d{dil}",
    )(*([src] * len(qkv_specs)), *consts)


GLA_TM = 256
GLA_STEP_TILES = 2


GLA_NC = GLA_TM // GLA_CHUNK


def _gla_tile(bb, rows, w2, q_ref, k_ref, v_ref, g_ref, bg_ref, nw_ref, tri_ref, o_ref, st_ref):
    g = g_ref[bb, rows, :LANES]
    z = jnp.dot(g, w2, preferred_element_type=F32) + bg_ref[...]
    la = (jnp.minimum(z, 0.0) * (LOG2E / GLA_TAU)
          - jnp.log2(1.0 + jnp.exp2(jnp.abs(z) * -LOG2E)) * (1.0 / GLA_TAU))
    la_h = la.astype(BF16)
    la_l = (la - la_h.astype(F32)).astype(BF16)

    tri = tri_ref[...]
    b = jnp.dot(tri, la_h, preferred_element_type=F32) + jnp.dot(tri, la_l, preferred_element_type=F32)
    b_last = jnp.concatenate(
        [jnp.broadcast_to(b[(ci + 1) * GLA_CHUNK - 1:(ci + 1) * GLA_CHUNK, :], (GLA_CHUNK, D_GLA_K))
         for ci in range(GLA_NC)], axis=0)

    q = q_ref[bb, rows].astype(F32)
    k = k_ref[bb, rows].astype(F32)
    q_t = (q * jnp.exp2(b)).astype(BF16)
    k_t = (k * jnp.exp2(-b)).astype(BF16)
    k_s = k * jnp.exp2(b_last - b)

    causal = tri > 0
    t_chunk = lax.broadcasted_iota(jnp.int32, (GLA_DK, GLA_TM), 1) // GLA_CHUNK
    nw = nw_ref[...]
    for h in range(N_HEADS_GLA):
        cs = slice(h * GLA_DK, (h + 1) * GLA_DK)
        vs = slice(h * GLA_DV, (h + 1) * GLA_DV)
        v_h = v_ref[bb, rows, vs]
        a = lax.dot_general(q_t[:, cs], k_t[:, cs], (((1,), (1,)), ((), ())), preferred_element_type=F32)
        o = jnp.dot(jnp.where(causal, a.astype(BF16), jnp.zeros_like(tri)), v_h, preferred_element_type=F32)
        k_s_t = k_s[:, cs].T.astype(BF16)
        zero = jnp.zeros_like(k_s_t)
        kv_all = jnp.dot(jnp.concatenate([jnp.where(t_chunk == ci, k_s_t, zero) for ci in range(GLA_NC)], axis=0),
                         v_h, preferred_element_type=F32)
        decay_t = jnp.exp2(b_last[:, cs].T)
        st = st_ref[bb, h]
        outs = []
        for ci in range(GLA_NC):
            rs = slice(ci * GLA_CHUNK, (ci + 1) * GLA_CHUNK)
            o_c = o[rs] + jnp.dot(q_t[rs, cs], st.astype(BF16), preferred_element_type=F32)
            decay = jnp.broadcast_to(decay_t[:, ci * GLA_CHUNK:ci * GLA_CHUNK + 1], (GLA_DK, GLA_DV))
            st = decay * st + kv_all[ci * GLA_DK:(ci + 1) * GLA_DK]
            ms = jnp.mean(o_c * o_c, axis=-1, keepdims=True)
            outs.append((o_c * lax.rsqrt(ms + EPS) * nw).astype(o_ref.dtype))
        st_ref[bb, h] = st
        o_ref[bb, rows, vs] = jnp.concatenate(outs, axis=0)


def _gla_kernel(q_ref, k_ref, v_ref, g_ref, w2_ref, bg_ref, nw_ref, tri_ref, wout_ref,
                o_ref, wout_bf_ref, st_ref):

    @pl.when(pl.program_id(0) == 0)
    def _():
        st_ref[...] = jnp.zeros_like(st_ref)

    wout_bf_ref[...] = wout_ref[...].astype(wout_bf_ref.dtype)

    w2 = w2_ref[...].astype(BF16)
    for tt in range(GLA_STEP_TILES):
        rows = slice(tt * GLA_TM, (tt + 1) * GLA_TM)
        for bb in range(st_ref.shape[0]):
            _gla_tile(bb, rows, w2, q_ref, k_ref, v_ref, g_ref, bg_ref, nw_ref, tri_ref, o_ref, st_ref)


def _gla(proj, w2, b_gate, norm_w, w_out, batch, seq):
    step_rows = GLA_TM * GLA_STEP_TILES
    steps = seq // step_rows
    wout_rows = w_out.shape[0] // steps
    pos = np.arange(GLA_TM)
    tri = jnp.asarray((pos[:, None] // GLA_CHUNK == pos[None, :] // GLA_CHUNK) & (pos[None, :] <= pos[:, None]),
                      BF16)
    proj3 = proj.reshape(batch, seq, REST_W)
    rows = lambda width, col: pl.BlockSpec((batch, step_rows, width), lambda i: (0, i, col // width))
    const = lambda shape: pl.BlockSpec(shape, lambda i: (0, 0))
    wout_spec = pl.BlockSpec((wout_rows, w_out.shape[1]), lambda i: (i, 0))
    out, w_out_bf = pl.pallas_call(
        _gla_kernel,
        grid=(steps,),
        in_specs=[
            rows(D_GLA_K, COL_GQ), rows(D_GLA_K, COL_GK), rows(D_GLA_V, COL_GV), rows(GLR_PAD, COL_GLR),
            const((LANES, D_GLA_K)), const((1, D_GLA_K)), const((1, GLA_DV)),
            const((GLA_TM, GLA_TM)), wout_spec,
        ],
        out_specs=[rows(D_GLA_V, 0), wout_spec],
        out_shape=[jax.ShapeDtypeStruct((batch, seq, D_GLA_V), BF16), jax.ShapeDtypeStruct(w_out.shape, BF16)],
        scratch_shapes=[pltpu.VMEM((batch, N_HEADS_GLA, GLA_DK, GLA_DV), F32)],
        compiler_params=_cparams(("arbitrary",)),
        name="gla",
    )(proj3, proj3, proj3, proj3, w2, b_gate, norm_w, tri, w_out)
    return out.reshape(batch * seq, D_GLA_V), w_out_bf


MEM_TM = 2048
MEM_UNIT = 128


def _mem_kernel(q_ref, mem_ref, mnw_ref, wkv_ref, o_ref, den_ref, kv_ref):
    @pl.when(pl.program_id(1) == 0)
    def _():
        m = mem_ref[0]
        ms = jnp.mean(m * m, axis=-1, keepdims=True)
        mn = (m * lax.rsqrt(ms + EPS) * mnw_ref[...]).astype(BF16)
        kv_ref[...] = jnp.dot(mn, wkv_ref[...].astype(BF16), preferred_element_type=F32).astype(BF16)

    lane = lax.broadcasted_iota(jnp.int32, (MEM_UNIT, LANES), 1)
    lo = lane < HEAD_DIM
    for u in range(MEM_TM // MEM_UNIT):
        rows = slice(u * MEM_UNIT, (u + 1) * MEM_UNIT)
        den_c = jnp.ones((MEM_UNIT, LANES), F32)
        for j in range(N_HEADS_MEM // 2):
            sl = slice(j * LANES, (j + 1) * LANES)
            r_e, r_o, _, _ = _pair_attention(q_ref[rows, sl], kv_ref[:, sl],
                                             kv_ref[:, D_MEM + j * LANES:D_MEM + (j + 1) * LANES], None, None)
            o_ref[rows, sl] = jnp.where(lo, r_e, r_o).astype(o_ref.dtype)
            den_c = jnp.where(lane == _stat_lane(2 * j), r_e, jnp.where(lane == _stat_lane(2 * j + 1), r_o, den_c))
        den_ref[rows, :] = den_c


def _mem_attn(proj, mem, mem_norm_w, w_kv, batch, seq):
    steps = seq // MEM_TM
    rows = lambda width, col=0: pl.BlockSpec((MEM_TM, width), lambda b, i: (b * steps + i, col // width))
    return pl.pallas_call(
        _mem_kernel,
        grid=(batch, steps),
        in_specs=[
            rows(D_MEM, COL_MQ),
            pl.BlockSpec((1, MEM_LEN, D_MODEL), lambda b, i: (b, 0, 0)),
            pl.BlockSpec((1, D_MODEL), lambda b, i: (0, 0)),
            pl.BlockSpec((D_MODEL, 2 * D_MEM), lambda b, i: (0, 0)),
        ],
        out_specs=[rows(D_MEM), rows(LANES)],
        out_shape=[jax.ShapeDtypeStruct((batch * seq, D_MEM), BF16),
                   jax.ShapeDtypeStruct((batch * seq, LANES), F32)],
        scratch_shapes=[pltpu.VMEM((MEM_LEN, 2 * D_MEM), BF16)],
        compiler_params=_cparams(("parallel", "arbitrary")),
        name="mem_attn",
    )(proj, mem, mem_norm_w, w_kv)


OUT_TM = 512


def _silu(g):
    h = g * 0.5
    return h * jnp.tanh(h) + h


OUT_GW = 256


def _out_kernel(n1_ref, n2_ref, n3_ref, s1_ref, s2_ref, s3_ref, og_ref, nm_ref, dm_ref, ga_ref, gg_ref, gm_ref,
                sm_ref, ex_ref, exm_ref, w_ref, x_ref, nw_ref, out_ref, mix_ref):
    keep = sm_ref[0:1, :]
    fill = sm_ref[1:2, :]
    stats = [s_ref[...] for s_ref in (s1_ref, s2_ref, s3_ref)]
    m1, m2, m3 = [st * keep for st in stats]
    d1, d2, d3 = [pltpu.roll(st, LANES - STAT_DEN_SHIFT, 1) * keep + fill for st in stats]
    mx = jnp.maximum(jnp.maximum(m1, m2), m3)
    t1, t2, t3 = jnp.exp2(m1 - mx), jnp.exp2(m2 - mx), jnp.exp2(m3 - mx)
    inv = 1.0 / (t1 * d1 + t2 * d2 + t3 * d3)
    wts = [(t * inv).astype(BF16) for t in (t1, t2, t3)]
    inv_m = (1.0 / dm_ref[...]).astype(BF16)

    def gate_chunk(lo):
        cols = slice(lo, lo + OUT_GW)
        if lo < D_DIL:
            a = jnp.zeros((OUT_TM, OUT_GW), F32)
            for wt, n_ref in zip(wts, (n1_ref, n2_ref, n3_ref)):
                a = a + jnp.dot(wt, ex_ref[:, cols], preferred_element_type=F32) * n_ref[:, cols].astype(F32)
            g = ga_ref[:, cols]
        elif lo < D_DIL + D_GLA_V:
            lc = slice(lo - D_DIL, lo - D_DIL + OUT_GW)
            a, g = og_ref[:, lc], gg_ref[:, lc]
        else:
            a = jnp.dot(inv_m, exm_ref[...], preferred_element_type=F32) * nm_ref[...].astype(F32)
            g = gm_ref[...]
        mix_ref[:, cols] = a.astype(BF16) * _silu(g)

    for lo in range(0, D_MIX, OUT_GW):
        gate_chunk(lo)
    y = jnp.dot(mix_ref[...], w_ref[...], preferred_element_type=F32)
    ms = jnp.mean(y * y, axis=-1, keepdims=True)
    out_ref[...] = x_ref[...] + y * lax.rsqrt(ms + EPS) * nw_ref[...]


def _out_proj(pats, o_gla, mem_pair, proj, stat_mask, expand, expand_mem, w_out, x2d, norm_w):
    m = x2d.shape[0]
    rows = lambda width, col=0: pl.BlockSpec((OUT_TM, width), lambda s: (s, col // width))
    const = lambda shape: pl.BlockSpec(shape, lambda s: (0, 0), pipeline_mode=pl.Buffered(1))
    nums, stats = zip(*pats)
    return pl.pallas_call(
        _out_kernel,
        grid=(m // OUT_TM,),
        in_specs=[
            *([rows(D_DIL)] * 3), *([rows(LANES)] * 3),
            rows(D_GLA_V), rows(D_MEM), rows(LANES),
            rows(D_DIL, COL_GATE_A), rows(D_GLA_V, COL_GATE_G), rows(D_MEM, COL_GATE_M),
            const((8, LANES)), const((LANES, D_DIL)), const((LANES, D_MEM)), const((D_MIX, D_MODEL)),
            rows(D_MODEL), const((1, D_MODEL)),
        ],
        out_specs=rows(D_MODEL),
        out_shape=jax.ShapeDtypeStruct((m, D_MODEL), x2d.dtype),
        scratch_shapes=[pltpu.VMEM((OUT_TM, D_MIX), BF16)],
        compiler_params=_cparams(("parallel",)),
        name="out_proj",
    )(*nums, *stats, o_gla, *mem_pair, proj, proj, proj, stat_mask, expand, expand_mem, w_out, x2d, norm_w)


D_IN_PROJ = 3 * D_DIL + 2 * D_GLA_K + D_GLA_V + GLA_GATE_RANK + D_MEM + D_MIX
SRC_GLR = 3 * D_DIL + 2 * D_GLA_K + D_GLA_V
SRC_MQ = SRC_GLR + GLA_GATE_RANK
SRC_GATE = SRC_MQ + D_MEM
SCALE_ATTN_Q = HEAD_DIM ** -0.5 * LOG2E
SCALE_GLA_Q = GLA_DK ** -0.5
QKV_SEGMENTS = (
    (0, D_DIL, 0, SCALE_ATTN_Q),
    (D_DIL, 2 * D_DIL, D_DIL, 1.0),
)
REST_SEGMENTS = (
    (QKV_W, D_GLA_K, COL_GQ, SCALE_GLA_Q),
    (QKV_W + D_GLA_K, D_GLA_K + D_GLA_V, COL_GK, 1.0),
    (SRC_GATE + D_DIL, D_GLA_V, COL_GATE_G, 1.0),
    (SRC_MQ, D_MEM, COL_MQ, SCALE_ATTN_Q),
    (SRC_GATE + D_DIL + D_GLA_V, D_MEM, COL_GATE_M, 1.0),
    (SRC_GATE, D_DIL, COL_GATE_A, 1.0),
)
WPREP_TK = 512


def _wprep_kernel(wt_ref, qkv_ref, rest_ref):
    for o_ref, segments in ((qkv_ref, QKV_SEGMENTS), (rest_ref, REST_SEGMENTS)):
        for src, width, dst, scale in segments:
            for lo in range(0, width, LANES):
                w = wt_ref[src + lo:src + lo + LANES, :].T
                if scale != 1.0:
                    w = w * scale
                o_ref[:, dst + lo:dst + lo + LANES] = w.astype(o_ref.dtype)
    glr = wt_ref[SRC_GLR:SRC_GLR + LANES, :].T
    keep = lax.broadcasted_iota(jnp.int32, glr.shape, 1) < GLA_GATE_RANK
    rest_ref[:, COL_GLR:COL_GLR + LANES] = jnp.where(keep, glr, 0.0).astype(rest_ref.dtype)
    rest_ref[:, COL_GLR + LANES:COL_GLR + GLR_PAD] = jnp.zeros((WPREP_TK, GLR_PAD - LANES), rest_ref.dtype)


def _regroup_in_weights(w_in):
    return pl.pallas_call(
        _wprep_kernel,
        grid=(D_MODEL // WPREP_TK,),
        in_specs=[pl.BlockSpec((D_IN_PROJ, WPREP_TK), lambda i: (0, i))],
        out_specs=[pl.BlockSpec((WPREP_TK, QKV_W), lambda i: (i, 0)),
                   pl.BlockSpec((WPREP_TK, REST_W), lambda i: (i, 0))],
        out_shape=[jax.ShapeDtypeStruct((D_MODEL, QKV_W), BF16), jax.ShapeDtypeStruct((D_MODEL, REST_W), BF16)],
        compiler_params=_cparams(("parallel",)),
        name="weight_regroup",
    )(w_in.T)


def kernel(x, mem, norm_pre_w, w_in, rel_bias, w_gla_gate2, b_gla_gate, gla_norm_w, mem_norm_w, w_mem_kv,
           w_out, norm_post_w):
    batch, seq, _ = x.shape
    depth = w_in.shape[0]
    expand_np = np.zeros((LANES, D_DIL), np.float32)
    for h in range(N_HEADS_DIL):
        expand_np[_stat_lane(h), h * HEAD_DIM:(h + 1) * HEAD_DIM] = 1.0
    expand = jnp.asarray(expand_np, BF16)
    expand_mem = jnp.asarray(expand_np[:, :D_MEM], BF16)
    stat_mask_np = np.zeros((8, LANES), np.float32)
    stat_mask_np[0, [_stat_lane(h) for h in range(N_HEADS_DIL)]] = 1.0
    stat_mask_np[1] = 1.0 - stat_mask_np[0]
    stat_mask = jnp.asarray(stat_mask_np)
    rb_t = jnp.zeros((16, LANES), F32).at[:N_HEADS_DIL, :REL_BUCKETS].set(rel_bias.astype(F32).T)
    for l in range(depth):
        x2d = x.reshape(batch * seq, D_MODEL)
        w_qkv, w_rest = _regroup_in_weights(w_in[l])
        h, qkv1, qkv4, qkv16 = _qkv_proj(x2d, norm_pre_w[l][None], w_qkv, batch, seq)
        rest = _rest_proj(h, w_rest)

        srcs = (qkv1.reshape(batch, 1, seq, QKV_W), qkv4, qkv16)
        pats = [_dilated(src, rb_t, batch, seq, window, dil)
                for (window, dil), src in zip(DIL_PATTERNS, srcs)]

        w2 = jnp.zeros((LANES, D_GLA_K), F32).at[:GLA_GATE_RANK].set(w_gla_gate2[l])
        o_gla, w_out_bf = _gla(rest, w2, b_gla_gate[l][None], gla_norm_w[l][None], w_out[l],
                               batch, seq)

        mem_pair = _mem_attn(rest, mem, mem_norm_w[l][None], w_mem_kv[l], batch, seq)

        out = _out_proj(pats, o_gla, mem_pair, rest, stat_mask, expand, expand_mem, w_out_bf, x2d,
                        norm_post_w[l][None])
        x = out.reshape(batch, seq, D_MODEL)
    return x
```

```python
import functools
import math

import numpy as np
import jax
import jax.numpy as jnp
from jax import lax
from jax.experimental import pallas as pl
from jax.experimental.pallas import tpu as pltpu

F32 = jnp.float32
BF16 = jnp.bfloat16

D_MODEL = 2048
HEAD_DIM = 64
N_HEADS_DIL = 12
DIL_PATTERNS = ((128, 1), (512, 4), (2048, 16))
DIL_BLOCK = 128
N_HEADS_GLA = 4
GLA_DK = 128
GLA_DV = 256
GLA_GATE_RANK = 16
GLA_TAU = 16.0
GLA_CHUNK = 64
N_HEADS_MEM = 4
MEM_LEN = 256
REL_BUCKETS = 32
REL_MAX_DIST = 2048
EPS = 1e-6
NEG_INF = -1e30

D_DIL = N_HEADS_DIL * HEAD_DIM
D_GLA_K = N_HEADS_GLA * GLA_DK
D_GLA_V = N_HEADS_GLA * GLA_DV
D_MEM = N_HEADS_MEM * HEAD_DIM
D_MIX = D_DIL + D_GLA_V + D_MEM

LANES = 128
GLR_PAD = 256

QKV_W = 3 * D_DIL
COL_GQ = 0
COL_GK = COL_GQ + D_GLA_K
COL_GV = COL_GK + D_GLA_K
COL_GATE_G = COL_GV + D_GLA_V
COL_MQ = COL_GATE_G + D_GLA_V
COL_GATE_M = COL_MQ + D_MEM
COL_GLR = COL_GATE_M + D_MEM
COL_GATE_A = COL_GLR + GLR_PAD
REST_W = COL_GATE_A + D_DIL

VMEM_LIMIT = 56 * 1024 * 1024


def _cparams(sem):
    return pltpu.CompilerParams(dimension_semantics=sem, vmem_limit_bytes=VMEM_LIMIT)


def _split_hi_lo(w):
    hi = w.astype(BF16)
    return hi, (w - hi.astype(F32)).astype(BF16)


QKV_TM = 512
N_SLABS = D_DIL // LANES
REGROUP_DILS = (4, 16)


def _qkv_proj_kernel(x_ref, nw_ref, w_ref, h_ref, o_ref, x4_ref, x16_ref, slab_ref):
    x = x_ref[...]
    ms = jnp.mean(x * x, axis=-1, keepdims=True)
    h_ref[...] = (x * lax.rsqrt(ms + EPS) * nw_ref[...]).astype(BF16)
    for t in range(QKV_W // D_DIL):
        cols = slice(t * D_DIL, (t + 1) * D_DIL)
        res = jnp.dot(h_ref[...], w_ref[:, cols], preferred_element_type=F32)
        o_ref[:, cols] = res.astype(o_ref.dtype)
        for s in range(N_SLABS):
            slab_ref[t, s] = res[:, s * LANES:(s + 1) * LANES]
        for out_ref, dil in zip((x4_ref, x16_ref), REGROUP_DILS):
            for r in range(dil):
                for s in range(N_SLABS):
                    lo = t * D_DIL + s * LANES
                    out_ref[0, r, :, lo:lo + LANES] = (
                        slab_ref[t, s, pl.ds(r, QKV_TM // dil, stride=dil), :].astype(out_ref.dtype))


def _qkv_proj(x2d, norm_w, w_qkv, batch, seq):
    m = x2d.shape[0]
    tiles_per_seq = seq // QKV_TM
    const = lambda shape: pl.BlockSpec(shape, lambda i: (0, 0), pipeline_mode=pl.Buffered(1))
    grouped = lambda dil: pl.BlockSpec((1, dil, QKV_TM // dil, QKV_W),
                                       lambda i: (i // tiles_per_seq, 0, i % tiles_per_seq, 0))
    return pl.pallas_call(
        _qkv_proj_kernel,
        grid=(m // QKV_TM,),
        in_specs=[
            pl.BlockSpec((QKV_TM, D_MODEL), lambda i: (i, 0)),
            const((1, D_MODEL)), const((D_MODEL, QKV_W)),
        ],
        out_specs=[pl.BlockSpec((QKV_TM, D_MODEL), lambda i: (i, 0)),
                   pl.BlockSpec((QKV_TM, QKV_W), lambda i: (i, 0))] + [grouped(d) for d in REGROUP_DILS],
        out_shape=[jax.ShapeDtypeStruct((m, D_MODEL), BF16), jax.ShapeDtypeStruct((m, QKV_W), BF16)]
        + [jax.ShapeDtypeStruct((batch, d, seq // d, QKV_W), BF16) for d in REGROUP_DILS],
        scratch_shapes=[pltpu.VMEM((QKV_W // D_DIL, N_SLABS, QKV_TM, LANES), F32)],
        compiler_params=_cparams(("parallel",)),
        name="qkv_proj",
    )(x2d, norm_w, w_qkv)


REST_TM = 512


def _rest_proj_kernel(h_ref, w_ref, o_ref):
    o_ref[...] = jnp.dot(h_ref[...], w_ref[...], preferred_element_type=F32).astype(o_ref.dtype)


def _rest_proj(h, w_rest):
    m = h.shape[0]
    const = lambda shape: pl.BlockSpec(shape, lambda i: (0, 0), pipeline_mode=pl.Buffered(1))
    return pl.pallas_call(
        _rest_proj_kernel,
        grid=(m // REST_TM,),
        in_specs=[pl.BlockSpec((REST_TM, D_MODEL), lambda i: (i, 0)), const((D_MODEL, REST_W))],
        out_specs=pl.BlockSpec((REST_TM, REST_W), lambda i: (i, 0)),
        out_shape=jax.ShapeDtypeStruct((m, REST_W), BF16),
        compiler_params=_cparams(("parallel",)),
        name="rest_proj",
    )(h, w_rest)


LOG2E = math.log2(math.e)


def _pair_attention(q_pair, k_pair, v_pair, bias_e, bias_o):
    rows = q_pair.shape[0]
    lo_q = lax.broadcasted_iota(jnp.int32, q_pair.shape, 1) < HEAD_DIM
    zero = jnp.zeros_like(q_pair)
    q_both = jnp.concatenate([jnp.where(lo_q, q_pair, zero), jnp.where(lo_q, zero, q_pair)], axis=0)
    s = lax.dot_general(q_both, k_pair, (((1,), (1,)), ((), ())), preferred_element_type=F32)
    s_e, s_o = s[:rows], s[rows:]
    if bias_e is not None:
        s_e = s_e + bias_e
        s_o = s_o + bias_o
    mx_e = jnp.max(s_e, axis=-1, keepdims=True)
    mx_o = jnp.max(s_o, axis=-1, keepdims=True)
    p_e = jnp.exp2(s_e - mx_e).astype(BF16)
    p_o = jnp.exp2(s_o - mx_o).astype(BF16)
    lo_v = lax.broadcasted_iota(jnp.int32, v_pair.shape, 1) < HEAD_DIM
    one = jnp.ones_like(v_pair)
    r_e = jnp.dot(p_e, jnp.where(lo_v, v_pair, one), preferred_element_type=F32)
    r_o = jnp.dot(p_o, jnp.where(lo_v, one, v_pair), preferred_element_type=F32)
    return r_e, r_o, mx_e, mx_o


def _stat_lane(head):
    return head + HEAD_DIM if head % 2 == 0 else head


STAT_DEN_SHIFT = 16


DIL_UNITS = {1: 8, 4: 4, 16: 8}


def _dil_init_bias(rb_ref, sel_ref, mrow_ref, bias_ref):
    rb_hi, rb_lo = _split_hi_lo(rb_ref[...])
    f = (jnp.dot(rb_hi, sel_ref[...], preferred_element_type=F32)
         + jnp.dot(rb_lo, sel_ref[...], preferred_element_type=F32)) * LOG2E + mrow_ref[...]
    col = lax.broadcasted_iota(jnp.int32, (DIL_BLOCK, 2 * DIL_BLOCK), 1)
    for h in range(N_HEADS_DIL):
        row = jnp.broadcast_to(f[h:h + 1, :], (DIL_BLOCK, 2 * DIL_BLOCK))
        tab = pltpu.roll(row, 0, 1, stride=1, stride_axis=0)
        bias_ref[0, h] = tab
        bias_ref[1, h] = jnp.where(col >= DIL_BLOCK, tab, NEG_INF)


def _dil_unit(q, kp, kc, vp, vc, first, bias_ref, put_num):
    lane = lax.broadcasted_iota(jnp.int32, (DIL_BLOCK, LANES), 1)
    lo = lane < HEAD_DIM
    st_c = jnp.zeros((DIL_BLOCK, LANES), F32)
    for j in range(N_HEADS_DIL // 2):
        sl = slice(j * LANES, (j + 1) * LANES)
        if kp is None:
            k_pair, v_pair = kc(sl), vc(sl)
            bias_e, bias_o = bias_ref[0, 2 * j, :, DIL_BLOCK:], bias_ref[0, 2 * j + 1, :, DIL_BLOCK:]
        else:
            k_pair = jnp.concatenate([kp(sl), kc(sl)], axis=0)
            v_pair = jnp.concatenate([vp(sl), vc(sl)], axis=0)
            bias_e, bias_o = bias_ref[first, 2 * j], bias_ref[first, 2 * j + 1]
        r_e, r_o, mx_e, mx_o = _pair_attention(q(sl), k_pair, v_pair, bias_e, bias_o)
        put_num(j, jnp.where(lo, r_e, r_o))
        l_e, l_o = _stat_lane(2 * j), _stat_lane(2 * j + 1)
        st_c = jnp.where(lane == l_e, mx_e, jnp.where(lane == l_e + STAT_DEN_SHIFT, r_e, st_c))
        st_c = jnp.where(lane == l_o, mx_o, jnp.where(lane == l_o + STAT_DEN_SHIFT, r_o, st_c))
    return st_c


def _dil1_kernel(q_ref, kp_ref, kc_ref, vp_ref, vc_ref, rb_ref, sel_ref, mrow_ref,
                 o_ref, st_ref, bias_ref):
    @pl.when((pl.program_id(0) == 0) & (pl.program_id(1) == 0))
    def _():
        _dil_init_bias(rb_ref, sel_ref, mrow_ref, bias_ref)

    for u in range(DIL_UNITS[1]):
        rows = slice(u * DIL_BLOCK, (u + 1) * DIL_BLOCK)
        prev = slice((u - 1) * DIL_BLOCK, u * DIL_BLOCK)
        if u == 0:
            first = (pl.program_id(1) == 0).astype(jnp.int32)
            kp, vp = (lambda sl: kp_ref[:, sl]), (lambda sl: vp_ref[:, sl])
        else:
            first = 0
            kp, vp = (lambda sl, p=prev: kc_ref[p, sl]), (lambda sl, p=prev: vc_ref[p, sl])

        def put_num(j, num, rows=rows):
            o_ref[rows, j * LANES:(j + 1) * LANES] = num.astype(o_ref.dtype)

        st_ref[rows, :] = _dil_unit(lambda sl, r=rows: q_ref[r, sl], kp, lambda sl, r=rows: kc_ref[r, sl],
                                    vp, lambda sl, r=rows: vc_ref[r, sl], first, bias_ref, put_num)


DIL_LBLOCKS = {4: 2, 16: 2}


def _dil_kernel(*refs, dil, whole_seq):
    if whole_seq:
        q_ref, kc_ref, vc_ref, *rest = refs
        kp_ref = vp_ref = None
    else:
        q_ref, kp_ref, kc_ref, vp_ref, vc_ref, *rest = refs
    rb_ref, sel_ref, mrow_ref, o_ref, st_ref, bias_ref, acc_ref = rest
    b, lb, g = pl.program_id(0), pl.program_id(1), pl.program_id(2)
    nr, nl = DIL_UNITS[dil], DIL_LBLOCKS[dil]

    @pl.when((b == 0) & (lb == 0) & (g == 0))
    def _():
        _dil_init_bias(rb_ref, sel_ref, mrow_ref, bias_ref)

    for ll in range(nl):
        sub = slice(ll * DIL_BLOCK, (ll + 1) * DIL_BLOCK)
        prev = slice((ll - 1) * DIL_BLOCK, ll * DIL_BLOCK)
        first = (lb == 0).astype(jnp.int32) if ll == 0 else 0
        for u in range(nr):
            rows = pl.ds(ll * DIL_BLOCK * dil + g * nr + u, DIL_BLOCK, stride=dil)
            if ll == 0 and whole_seq:
                kp = vp = None
            elif ll == 0:
                kp, vp = (lambda sl, u=u: kp_ref[u, :, sl]), (lambda sl, u=u: vp_ref[u, :, sl])
            else:
                kp = lambda sl, u=u, p=prev: kc_ref[u, p, sl]
                vp = lambda sl, u=u, p=prev: vc_ref[u, p, sl]

            def put_num(j, num, rows=rows):
                acc_ref[j, rows, :] = num

            st_ref[rows, :] = _dil_unit(lambda sl, u=u, r=sub: q_ref[u, r, sl], kp,
                                        lambda sl, u=u, r=sub: kc_ref[u, r, sl], vp,
                                        lambda sl, u=u, r=sub: vc_ref[u, r, sl], first, bias_ref, put_num)

    @pl.when(g == dil // nr - 1)
    def _():
        for j in range(N_HEADS_DIL // 2):
            o_ref[:, j * LANES:(j + 1) * LANES] = acc_ref[j].astype(o_ref.dtype)


def _t5_bucket_np(dist):
    max_exact = REL_BUCKETS // 2
    n = np.maximum(dist, 1).astype(np.float32)
    large = max_exact + (np.log(n / np.float32(max_exact)) / np.float32(math.log(REL_MAX_DIST / max_exact))
                         * np.float32(REL_BUCKETS - max_exact)).astype(np.int32)
    large = np.minimum(large, REL_BUCKETS - 1)
    return np.where(dist < max_exact, dist, large)


def _bias_selector(window, dil):
    back = window // dil
    x = np.arange(2 * DIL_BLOCK)
    rel = DIL_BLOCK - x
    valid = (rel >= 0) & (rel <= back)
    bucket = _t5_bucket_np(np.clip(rel, 0, back) * dil)
    sel = np.zeros((LANES, 2 * DIL_BLOCK), np.float32)
    sel[bucket[valid], x[valid]] = 1.0
    mrow = np.where(valid, 0.0, NEG_INF).astype(np.float32)[None]
    return jnp.asarray(sel, BF16), jnp.asarray(mrow)


def _dilated(src, rb_t, batch, seq, window, dil):
    sub_len = seq // dil
    nb = sub_len // DIL_BLOCK
    sel, mrow = _bias_selector(window, dil)
    nu = DIL_UNITS[dil]
    consts = (rb_t, sel, mrow)
    const = lambda a: pl.BlockSpec(a.shape, lambda *_: (0, 0))
    stat_shape = jax.ShapeDtypeStruct((batch * seq, LANES), F32)
    out_shape = [jax.ShapeDtypeStruct((batch * seq, D_DIL), BF16), stat_shape]
    bias_scratch = pltpu.VMEM((2, N_HEADS_DIL, DIL_BLOCK, 2 * DIL_BLOCK), F32)

    if dil == 1:
        rows = nu * DIL_BLOCK
        steps = nb // nu
        cur = lambda col: pl.BlockSpec((None, None, rows, D_DIL), lambda b, l: (b, 0, l, col))
        prev = lambda col: pl.BlockSpec((None, None, DIL_BLOCK, D_DIL),
                                        lambda b, l: (b, 0, jnp.maximum(nu * l - 1, 0), col))
        out = lambda width: pl.BlockSpec((rows, width), lambda b, l: (b * steps + l, 0))
        return pl.pallas_call(
            _dil1_kernel,
            grid=(batch, steps),
            in_specs=[cur(0), prev(1), cur(1), prev(2), cur(2)] + [const(a) for a in consts],
            out_specs=[out(D_DIL), out(LANES)],
            out_shape=out_shape,
            scratch_shapes=[bias_scratch],
            compiler_params=_cparams(("arbitrary", "arbitrary")),
            name="dilated_d1",
        )(src, src, src, src, src, *consts)

    nl = DIL_LBLOCKS[dil]
    steps = nb // nl
    rows = nl * DIL_BLOCK * dil
    cur = lambda col: pl.BlockSpec((None, nu, nl * DIL_BLOCK, D_DIL), lambda b, l, g: (b, g, l, col))
    prev = lambda col: pl.BlockSpec((None, nu, DIL_BLOCK, D_DIL),
                                    lambda b, l, g: (b, g, jnp.maximum(nl * l - 1, 0), col))
    out = lambda width: pl.BlockSpec((rows, width), lambda b, l, g: (b * steps + l, 0))
    whole_seq = steps == 1
    qkv_specs = [cur(0), cur(1), cur(2)] if whole_seq else [cur(0), prev(1), cur(1), prev(2), cur(2)]
    return pl.pallas_call(
        functools.partial(_dil_kernel, dil=dil, whole_seq=whole_seq),
        grid=(batch, steps, dil // nu),
        in_specs=qkv_specs + [const(a) for a in consts],
        out_specs=[out(D_DIL), out(LANES)],
        out_shape=out_shape,
        scratch_shapes=[bias_scratch, pltpu.VMEM((N_HEADS_DIL // 2, rows, LANES), F32)],
        compiler_params=_cparams(("arbitrary", "arbitrary", "arbitrary")),
        name=f"dilated_d{dil}",
    )(*([src] * len(qkv_specs)), *consts)


GLA_TM = 256
GLA_STEP_TILES = 2


GLA_NC = GLA_TM // GLA_CHUNK


def _gla_tile(bb, rows, w2, q_ref, k_ref, v_ref, g_ref, bg_ref, nw_ref, tri_ref, o_ref, st_ref):
    g = g_ref[bb, rows, :LANES]
    z = jnp.dot(g, w2, preferred_element_type=F32) + bg_ref[...]
    la = (jnp.minimum(z, 0.0) * (LOG2E / GLA_TAU)
          - jnp.log2(1.0 + jnp.exp2(jnp.abs(z) * -LOG2E)) * (1.0 / GLA_TAU))
    la_h = la.astype(BF16)
    la_l = (la - la_h.astype(F32)).astype(BF16)

    tri = tri_ref[...]
    b = jnp.dot(tri, la_h, preferred_element_type=F32) + jnp.dot(tri, la_l, preferred_element_type=F32)
    b_last = jnp.concatenate(
        [jnp.broadcast_to(b[(ci + 1) * GLA_CHUNK - 1:(ci + 1) * GLA_CHUNK, :], (GLA_CHUNK, D_GLA_K))
         for ci in range(GLA_NC)], axis=0)

    q = q_ref[bb, rows].astype(F32)
    k = k_ref[bb, rows].astype(F32)
    q_t = (q * jnp.exp2(b)).astype(BF16)
    k_t = (k * jnp.exp2(-b)).astype(BF16)
    k_s = k * jnp.exp2(b_last - b)

    causal = tri > 0
    t_chunk = lax.broadcasted_iota(jnp.int32, (GLA_DK, GLA_TM), 1) // GLA_CHUNK
    nw = nw_ref[...]
    for h in range(N_HEADS_GLA):
        cs = slice(h * GLA_DK, (h + 1) * GLA_DK)
        vs = slice(h * GLA_DV, (h + 1) * GLA_DV)
        v_h = v_ref[bb, rows, vs]
        a = lax.dot_general(q_t[:, cs], k_t[:, cs], (((1,), (1,)), ((), ())), preferred_element_type=F32)
        o = jnp.dot(jnp.where(causal, a.astype(BF16), jnp.zeros_like(tri)), v_h, preferred_element_type=F32)
        k_s_t = k_s[:, cs].T.astype(BF16)
        zero = jnp.zeros_like(k_s_t)
        kv_all = jnp.dot(jnp.concatenate([jnp.where(t_chunk == ci, k_s_t, zero) for ci in range(GLA_NC)], axis=0),
                         v_h, preferred_element_type=F32)
        decay_t = jnp.exp2(b_last[:, cs].T)
        st = st_ref[bb, h]
        outs = []
        for ci in range(GLA_NC):
            rs = slice(ci * GLA_CHUNK, (ci + 1) * GLA_CHUNK)
            o_c = o[rs] + jnp.dot(q_t[rs, cs], st.astype(BF16), preferred_element_type=F32)
            decay = jnp.broadcast_to(decay_t[:, ci * GLA_CHUNK:ci * GLA_CHUNK + 1], (GLA_DK, GLA_DV))
            st = decay * st + kv_all[ci * GLA_DK:(ci + 1) * GLA_DK]
            ms = jnp.mean(o_c * o_c, axis=-1, keepdims=True)
            outs.append((o_c * lax.rsqrt(ms + EPS) * nw).astype(o_ref.dtype))
        st_ref[bb, h] = st
        o_ref[bb, rows, vs] = jnp.concatenate(outs, axis=0)


def _gla_kernel(q_ref, k_ref, v_ref, g_ref, w2_ref, bg_ref, nw_ref, tri_ref, wout_ref,
                o_ref, wout_bf_ref, st_ref):

    @pl.when(pl.program_id(0) == 0)
    def _():
        st_ref[...] = jnp.zeros_like(st_ref)

    wout_bf_ref[...] = wout_ref[...].astype(wout_bf_ref.dtype)

    w2 = jnp.concatenate([w2_ref[...].astype(BF16), jnp.zeros((LANES - GLA_GATE_RANK, D_GLA_K), BF16)], axis=0)
    for tt in range(GLA_STEP_TILES):
        rows = slice(tt * GLA_TM, (tt + 1) * GLA_TM)
        for bb in range(st_ref.shape[0]):
            _gla_tile(bb, rows, w2, q_ref, k_ref, v_ref, g_ref, bg_ref, nw_ref, tri_ref, o_ref, st_ref)


def _gla(proj, w2, b_gate, norm_w, w_out, batch, seq):
    step_rows = GLA_TM * GLA_STEP_TILES
    steps = seq // step_rows
    wout_rows = w_out.shape[0] // steps
    pos = np.arange(GLA_TM)
    tri = jnp.asarray((pos[:, None] // GLA_CHUNK == pos[None, :] // GLA_CHUNK) & (pos[None, :] <= pos[:, None]),
                      BF16)
    proj3 = proj.reshape(batch, seq, REST_W)
    rows = lambda width, col: pl.BlockSpec((batch, step_rows, width), lambda i: (0, i, col // width))
    const = lambda shape: pl.BlockSpec(shape, lambda i: (0, 0))
    wout_spec = pl.BlockSpec((wout_rows, w_out.shape[1]), lambda i: (i, 0))
    out, w_out_bf = pl.pallas_call(
        _gla_kernel,
        grid=(steps,),
        in_specs=[
            rows(D_GLA_K, COL_GQ), rows(D_GLA_K, COL_GK), rows(D_GLA_V, COL_GV), rows(GLR_PAD, COL_GLR),
            const((GLA_GATE_RANK, D_GLA_K)), const((1, D_GLA_K)), const((1, GLA_DV)),
            const((GLA_TM, GLA_TM)), wout_spec,
        ],
        out_specs=[rows(D_GLA_V, 0), wout_spec],
        out_shape=[jax.ShapeDtypeStruct((batch, seq, D_GLA_V), BF16), jax.ShapeDtypeStruct(w_out.shape, BF16)],
        scratch_shapes=[pltpu.VMEM((batch, N_HEADS_GLA, GLA_DK, GLA_DV), F32)],
        compiler_params=_cparams(("arbitrary",)),
        name="gla",
    )(proj3, proj3, proj3, proj3, w2, b_gate, norm_w, tri, w_out)
    return out.reshape(batch * seq, D_GLA_V), w_out_bf


MEM_TM = 2048
MEM_UNIT = 128


def _mem_kernel(q_ref, mem_ref, mnw_ref, wkv_ref, o_ref, den_ref, kv_ref):
    @pl.when(pl.program_id(1) == 0)
    def _():
        m = mem_ref[0]
        ms = jnp.mean(m * m, axis=-1, keepdims=True)
        mn = (m * lax.rsqrt(ms + EPS) * mnw_ref[...]).astype(BF16)
        kv_ref[...] = jnp.dot(mn, wkv_ref[...].astype(BF16), preferred_element_type=F32).astype(BF16)

    lane = lax.broadcasted_iota(jnp.int32, (MEM_UNIT, LANES), 1)
    lo = lane < HEAD_DIM
    for u in range(MEM_TM // MEM_UNIT):
        rows = slice(u * MEM_UNIT, (u + 1) * MEM_UNIT)
        den_c = jnp.ones((MEM_UNIT, LANES), F32)
        for j in range(N_HEADS_MEM // 2):
            sl = slice(j * LANES, (j + 1) * LANES)
            r_e, r_o, _, _ = _pair_attention(q_ref[rows, sl], kv_ref[:, sl],
                                             kv_ref[:, D_MEM + j * LANES:D_MEM + (j + 1) * LANES], None, None)
            o_ref[rows, sl] = jnp.where(lo, r_e, r_o).astype(o_ref.dtype)
            den_c = jnp.where(lane == _stat_lane(2 * j), r_e, jnp.where(lane == _stat_lane(2 * j + 1), r_o, den_c))
        den_ref[rows, :] = den_c


def _mem_attn(proj, mem, mem_norm_w, w_kv, batch, seq):
    steps = seq // MEM_TM
    rows = lambda width, col=0: pl.BlockSpec((MEM_TM, width), lambda b, i: (b * steps + i, col // width))
    return pl.pallas_call(
        _mem_kernel,
        grid=(batch, steps),
        in_specs=[
            rows(D_MEM, COL_MQ),
            pl.BlockSpec((1, MEM_LEN, D_MODEL), lambda b, i: (b, 0, 0)),
            pl.BlockSpec((1, D_MODEL), lambda b, i: (0, 0)),
            pl.BlockSpec((D_MODEL, 2 * D_MEM), lambda b, i: (0, 0)),
        ],
        out_specs=[rows(D_MEM), rows(LANES)],
        out_shape=[jax.ShapeDtypeStruct((batch * seq, D_MEM), BF16),
                   jax.ShapeDtypeStruct((batch * seq, LANES), F32)],
        scratch_shapes=[pltpu.VMEM((MEM_LEN, 2 * D_MEM), BF16)],
        compiler_params=_cparams(("parallel", "arbitrary")),
        name="mem_attn",
    )(proj, mem, mem_norm_w, w_kv)


OUT_TM = 512


def _silu(g):
    h = g * 0.5
    return h * jnp.tanh(h) + h


OUT_GW = 256


def _out_kernel(n1_ref, n2_ref, n3_ref, s1_ref, s2_ref, s3_ref, og_ref, nm_ref, dm_ref, ga_ref, gg_ref, gm_ref,
                sm_ref, ex_ref, exm_ref, w_ref, x_ref, nw_ref, out_ref, mix_ref):
    keep = sm_ref[0:1, :]
    fill = sm_ref[1:2, :]
    stats = [s_ref[...] for s_ref in (s1_ref, s2_ref, s3_ref)]
    m1, m2, m3 = [st * keep for st in stats]
    d1, d2, d3 = [pltpu.roll(st, LANES - STAT_DEN_SHIFT, 1) * keep + fill for st in stats]
    mx = jnp.maximum(jnp.maximum(m1, m2), m3)
    t1, t2, t3 = jnp.exp2(m1 - mx), jnp.exp2(m2 - mx), jnp.exp2(m3 - mx)
    inv = 1.0 / (t1 * d1 + t2 * d2 + t3 * d3)
    wts = [(t * inv).astype(BF16) for t in (t1, t2, t3)]
    inv_m = (1.0 / dm_ref[...]).astype(BF16)

    def gate_chunk(lo):
        cols = slice(lo, lo + OUT_GW)
        if lo < D_DIL:
            a = jnp.zeros((OUT_TM, OUT_GW), F32)
            for wt, n_ref in zip(wts, (n1_ref, n2_ref, n3_ref)):
                a = a + jnp.dot(wt, ex_ref[:, cols], preferred_element_type=F32) * n_ref[:, cols].astype(F32)
            g = ga_ref[:, cols]
        elif lo < D_DIL + D_GLA_V:
            lc = slice(lo - D_DIL, lo - D_DIL + OUT_GW)
            a, g = og_ref[:, lc], gg_ref[:, lc]
        else:
            a = jnp.dot(inv_m, exm_ref[...], preferred_element_type=F32) * nm_ref[...].astype(F32)
            g = gm_ref[...]
        mix_ref[:, cols] = a.astype(BF16) * _silu(g)

    for lo in range(0, D_MIX, OUT_GW):
        gate_chunk(lo)
    y = jnp.dot(mix_ref[...], w_ref[...], preferred_element_type=F32)
    ms = jnp.mean(y * y, axis=-1, keepdims=True)
    out_ref[...] = x_ref[...] + y * lax.rsqrt(ms + EPS) * nw_ref[...]


def _out_proj(pats, o_gla, mem_pair, proj, stat_mask, expand, expand_mem, w_out, x2d, norm_w):
    m = x2d.shape[0]
    rows = lambda width, col=0: pl.BlockSpec((OUT_TM, width), lambda s: (s, col // width))
    const = lambda shape: pl.BlockSpec(shape, lambda s: (0, 0), pipeline_mode=pl.Buffered(1))
    nums, stats = zip(*pats)
    return pl.pallas_call(
        _out_kernel,
        grid=(m // OUT_TM,),
        in_specs=[
            *([rows(D_DIL)] * 3), *([rows(LANES)] * 3),
            rows(D_GLA_V), rows(D_MEM), rows(LANES),
            rows(D_DIL, COL_GATE_A), rows(D_GLA_V, COL_GATE_G), rows(D_MEM, COL_GATE_M),
            const((8, LANES)), const((LANES, D_DIL)), const((LANES, D_MEM)), const((D_MIX, D_MODEL)),
            rows(D_MODEL), const((1, D_MODEL)),
        ],
        out_specs=rows(D_MODEL),
        out_shape=jax.ShapeDtypeStruct((m, D_MODEL), x2d.dtype),
        scratch_shapes=[pltpu.VMEM((OUT_TM, D_MIX), BF16)],
        compiler_params=_cparams(("parallel",)),
        name="out_proj",
    )(*nums, *stats, o_gla, *mem_pair, proj, proj, proj, stat_mask, expand, expand_mem, w_out, x2d, norm_w)


D_IN_PROJ = 3 * D_DIL + 2 * D_GLA_K + D_GLA_V + GLA_GATE_RANK + D_MEM + D_MIX
SRC_GLR = 3 * D_DIL + 2 * D_GLA_K + D_GLA_V
SRC_MQ = SRC_GLR + GLA_GATE_RANK
SRC_GATE = SRC_MQ + D_MEM
SCALE_ATTN_Q = HEAD_DIM ** -0.5 * LOG2E
SCALE_GLA_Q = GLA_DK ** -0.5
QKV_SEGMENTS = (
    (0, D_DIL, 0, SCALE_ATTN_Q),
    (D_DIL, 2 * D_DIL, D_DIL, 1.0),
)
REST_SEGMENTS = (
    (QKV_W, D_GLA_K, COL_GQ, SCALE_GLA_Q),
    (QKV_W + D_GLA_K, D_GLA_K + D_GLA_V, COL_GK, 1.0),
    (SRC_GATE + D_DIL, D_GLA_V, COL_GATE_G, 1.0),
    (SRC_MQ, D_MEM, COL_MQ, SCALE_ATTN_Q),
    (SRC_GATE + D_DIL + D_GLA_V, D_MEM, COL_GATE_M, 1.0),
    (SRC_GATE, D_DIL, COL_GATE_A, 1.0),
)
WPREP_TK = 512


def _wprep_kernel(wt_ref, qkv_ref, rest_ref):
    for o_ref, segments in ((qkv_ref, QKV_SEGMENTS), (rest_ref, REST_SEGMENTS)):
        for src, width, dst, scale in segments:
            for lo in range(0, width, LANES):
                w = wt_ref[src + lo:src + lo + LANES, :].T
                if scale != 1.0:
                    w = w * scale
                o_ref[:, dst + lo:dst + lo + LANES] = w.astype(o_ref.dtype)
    glr = wt_ref[SRC_GLR:SRC_GLR + LANES, :].T
    keep = lax.broadcasted_iota(jnp.int32, glr.shape, 1) < GLA_GATE_RANK
    rest_ref[:, COL_GLR:COL_GLR + LANES] = jnp.where(keep, glr, 0.0).astype(rest_ref.dtype)
    rest_ref[:, COL_GLR + LANES:COL_GLR + GLR_PAD] = jnp.zeros((WPREP_TK, GLR_PAD - LANES), rest_ref.dtype)


def _regroup_in_weights(w_in):
    return pl.pallas_call(
        _wprep_kernel,
        grid=(D_MODEL // WPREP_TK,),
        in_specs=[pl.BlockSpec((D_IN_PROJ, WPREP_TK), lambda i: (0, i))],
        out_specs=[pl.BlockSpec((WPREP_TK, QKV_W), lambda i: (i, 0)),
                   pl.BlockSpec((WPREP_TK, REST_W), lambda i: (i, 0))],
        out_shape=[jax.ShapeDtypeStruct((D_MODEL, QKV_W), BF16), jax.ShapeDtypeStruct((D_MODEL, REST_W), BF16)],
        compiler_params=_cparams(("parallel",)),
        name="weight_regroup",
    )(w_in.T)


def kernel(x, mem, norm_pre_w, w_in, rel_bias, w_gla_gate2, b_gla_gate, gla_norm_w, mem_norm_w, w_mem_kv,
           w_out, norm_post_w):
    batch, seq, _ = x.shape
    depth = w_in.shape[0]
    expand_np = np.zeros((LANES, D_DIL), np.float32)
    for h in range(N_HEADS_DIL):
        expand_np[_stat_lane(h), h * HEAD_DIM:(h + 1) * HEAD_DIM] = 1.0
    expand = jnp.asarray(expand_np, BF16)
    expand_mem = jnp.asarray(expand_np[:, :D_MEM], BF16)
    stat_mask_np = np.zeros((8, LANES), np.float32)
    stat_mask_np[0, [_stat_lane(h) for h in range(N_HEADS_DIL)]] = 1.0
    stat_mask_np[1] = 1.0 - stat_mask_np[0]
    stat_mask = jnp.asarray(stat_mask_np)
    rb_t = jnp.zeros((16, LANES), F32).at[:N_HEADS_DIL, :REL_BUCKETS].set(rel_bias.astype(F32).T)
    for l in range(depth):
        x2d = x.reshape(batch * seq, D_MODEL)
        w_qkv, w_rest = _regroup_in_weights(w_in[l])
        h, qkv1, qkv4, qkv16 = _qkv_proj(x2d, norm_pre_w[l][None], w_qkv, batch, seq)
        rest = _rest_proj(h, w_rest)

        srcs = (qkv1.reshape(batch, 1, seq, QKV_W), qkv4, qkv16)
        pats = [_dilated(src, rb_t, batch, seq, window, dil)
                for (window, dil), src in zip(DIL_PATTERNS, srcs)]

        o_gla, w_out_bf = _gla(rest, w_gla_gate2[l], b_gla_gate[l][None], gla_norm_w[l][None], w_out[l],
                               batch, seq)

        mem_pair = _mem_attn(rest, mem, mem_norm_w[l][None], w_mem_kv[l], batch, seq)

        out = _out_proj(pats, o_gla, mem_pair, rest, stat_mask, expand, expand_mem, w_out_bf, x2d,
                        norm_post_w[l][None])
        x = out.reshape(batch, seq, D_MODEL)
    return x
```

```python
import functools
import math

import numpy as np
import jax
import jax.numpy as jnp
from jax import lax
from jax.experimental import pallas as pl
from jax.experimental.pallas import tpu as pltpu

F32 = jnp.float32
BF16 = jnp.bfloat16

D_MODEL = 2048
HEAD_DIM = 64
N_HEADS_DIL = 12
DIL_PATTERNS = ((128, 1), (512, 4), (2048, 16))
DIL_BLOCK = 128
N_HEADS_GLA = 4
GLA_DK = 128
GLA_DV = 256
GLA_GATE_RANK = 16
GLA_TAU = 16.0
GLA_CHUNK = 64
N_HEADS_MEM = 4
MEM_LEN = 256
REL_BUCKETS = 32
REL_MAX_DIST = 2048
EPS = 1e-6
NEG_INF = -1e30

D_DIL = N_HEADS_DIL * HEAD_DIM
D_GLA_K = N_HEADS_GLA * GLA_DK
D_GLA_V = N_HEADS_GLA * GLA_DV
D_MEM = N_HEADS_MEM * HEAD_DIM
D_MIX = D_DIL + D_GLA_V + D_MEM

LANES = 128
F32_SUBLANES = 8
GLR_PAD = 256

QKV_W = 3 * D_DIL
COL_GQ = 0
COL_GK = COL_GQ + D_GLA_K
COL_GV = COL_GK + D_GLA_K
COL_GATE_G = COL_GV + D_GLA_V
COL_MQ = COL_GATE_G + D_GLA_V
COL_GATE_M = COL_MQ + D_MEM
COL_GLR = COL_GATE_M + D_MEM
COL_GATE_A = COL_GLR + GLR_PAD
REST_W = COL_GATE_A + D_DIL

VMEM_LIMIT = 56 * 1024 * 1024


def _cparams(sem):
    return pltpu.CompilerParams(dimension_semantics=sem, vmem_limit_bytes=VMEM_LIMIT)


def _split_hi_lo(w):
    hi = w.astype(BF16)
    return hi, (w - hi.astype(F32)).astype(BF16)


QKV_TM = 512
N_SLABS = D_DIL // LANES
REGROUP_DILS = (4, 16)


def _qkv_proj_kernel(x_ref, nw_ref, w_ref, h_ref, o_ref, x4_ref, x16_ref, slab_ref):
    x = x_ref[...]
    ms = jnp.mean(x * x, axis=-1, keepdims=True)
    h_ref[...] = (x * lax.rsqrt(ms + EPS) * nw_ref[...]).astype(BF16)
    for t in range(QKV_W // D_DIL):
        cols = slice(t * D_DIL, (t + 1) * D_DIL)
        res = jnp.dot(h_ref[...], w_ref[:, cols], preferred_element_type=F32)
        o_ref[:, cols] = res.astype(o_ref.dtype)
        for s in range(N_SLABS):
            slab_ref[t, s] = res[:, s * LANES:(s + 1) * LANES]
        for out_ref, dil in zip((x4_ref, x16_ref), REGROUP_DILS):
            for r in range(dil):
                for s in range(N_SLABS):
                    lo = t * D_DIL + s * LANES
                    out_ref[0, r, :, lo:lo + LANES] = (
                        slab_ref[t, s, pl.ds(r, QKV_TM // dil, stride=dil), :].astype(out_ref.dtype))


def _qkv_proj(x2d, norm_w, w_qkv, batch, seq):
    m = x2d.shape[0]
    tiles_per_seq = seq // QKV_TM
    const = lambda shape: pl.BlockSpec(shape, lambda i: (0, 0), pipeline_mode=pl.Buffered(1))
    grouped = lambda dil: pl.BlockSpec((1, dil, QKV_TM // dil, QKV_W),
                                       lambda i: (i // tiles_per_seq, 0, i % tiles_per_seq, 0))
    return pl.pallas_call(
        _qkv_proj_kernel,
        grid=(m // QKV_TM,),
        in_specs=[
            pl.BlockSpec((QKV_TM, D_MODEL), lambda i: (i, 0)),
            const((1, D_MODEL)), const((D_MODEL, QKV_W)),
        ],
        out_specs=[pl.BlockSpec((QKV_TM, D_MODEL), lambda i: (i, 0)),
                   pl.BlockSpec((QKV_TM, QKV_W), lambda i: (i, 0))] + [grouped(d) for d in REGROUP_DILS],
        out_shape=[jax.ShapeDtypeStruct((m, D_MODEL), BF16), jax.ShapeDtypeStruct((m, QKV_W), BF16)]
        + [jax.ShapeDtypeStruct((batch, d, seq // d, QKV_W), BF16) for d in REGROUP_DILS],
        scratch_shapes=[pltpu.VMEM((QKV_W // D_DIL, N_SLABS, QKV_TM, LANES), F32)],
        compiler_params=_cparams(("parallel",)),
        name="qkv_proj",
    )(x2d, norm_w, w_qkv)


REST_TM = 512


def _rest_proj_kernel(h_ref, w_ref, o_ref):
    o_ref[...] = jnp.dot(h_ref[...], w_ref[...], preferred_element_type=F32).astype(o_ref.dtype)


def _rest_proj(h, w_rest):
    m = h.shape[0]
    const = lambda shape: pl.BlockSpec(shape, lambda i: (0, 0), pipeline_mode=pl.Buffered(1))
    return pl.pallas_call(
        _rest_proj_kernel,
        grid=(m // REST_TM,),
        in_specs=[pl.BlockSpec((REST_TM, D_MODEL), lambda i: (i, 0)), const((D_MODEL, REST_W))],
        out_specs=pl.BlockSpec((REST_TM, REST_W), lambda i: (i, 0)),
        out_shape=jax.ShapeDtypeStruct((m, REST_W), BF16),
        compiler_params=_cparams(("parallel",)),
        name="rest_proj",
    )(h, w_rest)


LOG2E = math.log2(math.e)


def _pair_attention(q_pair, k_pair, v_pair, bias_e, bias_o):
    rows = q_pair.shape[0]
    lo_q = lax.broadcasted_iota(jnp.int32, q_pair.shape, 1) < HEAD_DIM
    zero = jnp.zeros_like(q_pair)
    q_both = jnp.concatenate([jnp.where(lo_q, q_pair, zero), jnp.where(lo_q, zero, q_pair)], axis=0)
    s = lax.dot_general(q_both, k_pair, (((1,), (1,)), ((), ())), preferred_element_type=F32)
    s_e, s_o = s[:rows], s[rows:]
    if bias_e is not None:
        s_e = s_e + bias_e
        s_o = s_o + bias_o
    mx_e = jnp.max(s_e, axis=-1, keepdims=True)
    mx_o = jnp.max(s_o, axis=-1, keepdims=True)
    p_e = jnp.exp2(s_e - mx_e).astype(BF16)
    p_o = jnp.exp2(s_o - mx_o).astype(BF16)
    lo_v = lax.broadcasted_iota(jnp.int32, v_pair.shape, 1) < HEAD_DIM
    one = jnp.ones_like(v_pair)
    r_e = jnp.dot(p_e, jnp.where(lo_v, v_pair, one), preferred_element_type=F32)
    r_o = jnp.dot(p_o, jnp.where(lo_v, one, v_pair), preferred_element_type=F32)
    return r_e, r_o, mx_e, mx_o


def _stat_lane(head):
    return head + HEAD_DIM if head % 2 == 0 else head


STAT_DEN_SHIFT = 16


DIL_UNITS = {1: 8, 4: 4, 16: 8}


def _dil_init_bias(rb_ref, sel_ref, mrow_ref, bias_ref):
    rb_hi, rb_lo = _split_hi_lo(rb_ref[...])
    f = (jnp.dot(rb_hi, sel_ref[...], preferred_element_type=F32)
         + jnp.dot(rb_lo, sel_ref[...], preferred_element_type=F32)) * LOG2E + mrow_ref[...]
    col = lax.broadcasted_iota(jnp.int32, (DIL_BLOCK, 2 * DIL_BLOCK), 1)
    for h in range(N_HEADS_DIL):
        row = jnp.broadcast_to(f[h:h + 1, :], (DIL_BLOCK, 2 * DIL_BLOCK))
        tab = pltpu.roll(row, 0, 1, stride=1, stride_axis=0)
        bias_ref[0, h] = tab
        bias_ref[1, h] = jnp.where(col >= DIL_BLOCK, tab, NEG_INF)


def _dil_unit(q, kp, kc, vp, vc, first, bias_ref, put_num):
    lane = lax.broadcasted_iota(jnp.int32, (DIL_BLOCK, LANES), 1)
    lo = lane < HEAD_DIM
    st_c = jnp.zeros((DIL_BLOCK, LANES), F32)
    for j in range(N_HEADS_DIL // 2):
        sl = slice(j * LANES, (j + 1) * LANES)
        if kp is None:
            k_pair, v_pair = kc(sl), vc(sl)
            bias_e, bias_o = bias_ref[0, 2 * j, :, DIL_BLOCK:], bias_ref[0, 2 * j + 1, :, DIL_BLOCK:]
        else:
            k_pair = jnp.concatenate([kp(sl), kc(sl)], axis=0)
            v_pair = jnp.concatenate([vp(sl), vc(sl)], axis=0)
            bias_e, bias_o = bias_ref[first, 2 * j], bias_ref[first, 2 * j + 1]
        r_e, r_o, mx_e, mx_o = _pair_attention(q(sl), k_pair, v_pair, bias_e, bias_o)
        put_num(j, jnp.where(lo, r_e, r_o))
        l_e, l_o = _stat_lane(2 * j), _stat_lane(2 * j + 1)
        st_c = jnp.where(lane == l_e, mx_e, jnp.where(lane == l_e + STAT_DEN_SHIFT, r_e, st_c))
        st_c = jnp.where(lane == l_o, mx_o, jnp.where(lane == l_o + STAT_DEN_SHIFT, r_o, st_c))
    return st_c


def _dil1_kernel(q_ref, kp_ref, kc_ref, vp_ref, vc_ref, rb_ref, sel_ref, mrow_ref,
                 o_ref, st_ref, bias_ref):
    @pl.when((pl.program_id(0) == 0) & (pl.program_id(1) == 0))
    def _():
        _dil_init_bias(rb_ref, sel_ref, mrow_ref, bias_ref)

    for u in range(DIL_UNITS[1]):
        rows = slice(u * DIL_BLOCK, (u + 1) * DIL_BLOCK)
        prev = slice((u - 1) * DIL_BLOCK, u * DIL_BLOCK)
        if u == 0:
            first = (pl.program_id(1) == 0).astype(jnp.int32)
            kp, vp = (lambda sl: kp_ref[:, sl]), (lambda sl: vp_ref[:, sl])
        else:
            first = 0
            kp, vp = (lambda sl, p=prev: kc_ref[p, sl]), (lambda sl, p=prev: vc_ref[p, sl])

        def put_num(j, num, rows=rows):
            o_ref[rows, j * LANES:(j + 1) * LANES] = num.astype(o_ref.dtype)

        st_ref[rows, :] = _dil_unit(lambda sl, r=rows: q_ref[r, sl], kp, lambda sl, r=rows: kc_ref[r, sl],
                                    vp, lambda sl, r=rows: vc_ref[r, sl], first, bias_ref, put_num)


DIL_LBLOCKS = {4: 2, 16: 2}


def _dil_kernel(*refs, dil, whole_seq):
    if whole_seq:
        q_ref, kc_ref, vc_ref, *rest = refs
        kp_ref = vp_ref = None
    else:
        q_ref, kp_ref, kc_ref, vp_ref, vc_ref, *rest = refs
    rb_ref, sel_ref, mrow_ref, o_ref, st_ref, bias_ref, acc_ref = rest
    b, lb, g = pl.program_id(0), pl.program_id(1), pl.program_id(2)
    nr, nl = DIL_UNITS[dil], DIL_LBLOCKS[dil]

    @pl.when((b == 0) & (lb == 0) & (g == 0))
    def _():
        _dil_init_bias(rb_ref, sel_ref, mrow_ref, bias_ref)

    for ll in range(nl):
        sub = slice(ll * DIL_BLOCK, (ll + 1) * DIL_BLOCK)
        prev = slice((ll - 1) * DIL_BLOCK, ll * DIL_BLOCK)
        first = (lb == 0).astype(jnp.int32) if ll == 0 else 0
        for u in range(nr):
            rows = pl.ds(ll * DIL_BLOCK * dil + g * nr + u, DIL_BLOCK, stride=dil)
            if ll == 0 and whole_seq:
                kp = vp = None
            elif ll == 0:
                kp, vp = (lambda sl, u=u: kp_ref[u, :, sl]), (lambda sl, u=u: vp_ref[u, :, sl])
            else:
                kp = lambda sl, u=u, p=prev: kc_ref[u, p, sl]
                vp = lambda sl, u=u, p=prev: vc_ref[u, p, sl]

            def put_num(j, num, rows=rows):
                acc_ref[j, rows, :] = num

            st_ref[rows, :] = _dil_unit(lambda sl, u=u, r=sub: q_ref[u, r, sl], kp,
                                        lambda sl, u=u, r=sub: kc_ref[u, r, sl], vp,
                                        lambda sl, u=u, r=sub: vc_ref[u, r, sl], first, bias_ref, put_num)

    @pl.when(g == dil // nr - 1)
    def _():
        for j in range(N_HEADS_DIL // 2):
            o_ref[:, j * LANES:(j + 1) * LANES] = acc_ref[j].astype(o_ref.dtype)


def _t5_bucket_np(dist):
    max_exact = REL_BUCKETS // 2
    n = np.maximum(dist, 1).astype(np.float32)
    large = max_exact + (np.log(n / np.float32(max_exact)) / np.float32(math.log(REL_MAX_DIST / max_exact))
                         * np.float32(REL_BUCKETS - max_exact)).astype(np.int32)
    large = np.minimum(large, REL_BUCKETS - 1)
    return np.where(dist < max_exact, dist, large)


def _bias_selector(window, dil):
    back = window // dil
    x = np.arange(2 * DIL_BLOCK)
    rel = DIL_BLOCK - x
    valid = (rel >= 0) & (rel <= back)
    bucket = _t5_bucket_np(np.clip(rel, 0, back) * dil)
    sel = np.zeros((LANES, 2 * DIL_BLOCK), np.float32)
    sel[bucket[valid], x[valid]] = 1.0
    mrow = np.where(valid, 0.0, NEG_INF).astype(np.float32)[None]
    return jnp.asarray(sel, BF16), jnp.asarray(mrow)


def _dilated(src, rb_t, batch, seq, window, dil):
    sub_len = seq // dil
    nb = sub_len // DIL_BLOCK
    sel, mrow = _bias_selector(window, dil)
    nu = DIL_UNITS[dil]
    consts = (rb_t, sel, mrow)
    const = lambda a: pl.BlockSpec(a.shape, lambda *_: (0, 0))
    stat_shape = jax.ShapeDtypeStruct((batch * seq, LANES), F32)
    out_shape = [jax.ShapeDtypeStruct((batch * seq, D_DIL), BF16), stat_shape]
    bias_scratch = pltpu.VMEM((2, N_HEADS_DIL, DIL_BLOCK, 2 * DIL_BLOCK), F32)

    if dil == 1:
        rows = nu * DIL_BLOCK
        steps = nb // nu
        cur = lambda col: pl.BlockSpec((None, None, rows, D_DIL), lambda b, l: (b, 0, l, col))
        prev = lambda col: pl.BlockSpec((None, None, DIL_BLOCK, D_DIL),
                                        lambda b, l: (b, 0, jnp.maximum(nu * l - 1, 0), col))
        out = lambda width: pl.BlockSpec((rows, width), lambda b, l: (b * steps + l, 0))
        return pl.pallas_call(
            _dil1_kernel,
            grid=(batch, steps),
            in_specs=[cur(0), prev(1), cur(1), prev(2), cur(2)] + [const(a) for a in consts],
            out_specs=[out(D_DIL), out(LANES)],
            out_shape=out_shape,
            scratch_shapes=[bias_scratch],
            compiler_params=_cparams(("arbitrary", "arbitrary")),
            name="dilated_d1",
        )(src, src, src, src, src, *consts)

    nl = DIL_LBLOCKS[dil]
    steps = nb // nl
    rows = nl * DIL_BLOCK * dil
    cur = lambda col: pl.BlockSpec((None, nu, nl * DIL_BLOCK, D_DIL), lambda b, l, g: (b, g, l, col))
    prev = lambda col: pl.BlockSpec((None, nu, DIL_BLOCK, D_DIL),
                                    lambda b, l, g: (b, g, jnp.maximum(nl * l - 1, 0), col))
    out = lambda width: pl.BlockSpec((rows, width), lambda b, l, g: (b * steps + l, 0))
    whole_seq = steps == 1
    qkv_specs = [cur(0), cur(1), cur(2)] if whole_seq else [cur(0), prev(1), cur(1), prev(2), cur(2)]
    return pl.pallas_call(
        functools.partial(_dil_kernel, dil=dil, whole_seq=whole_seq),
        grid=(batch, steps, dil // nu),
        in_specs=qkv_specs + [const(a) for a in consts],
        out_specs=[out(D_DIL), out(LANES)],
        out_shape=out_shape,
        scratch_shapes=[bias_scratch, pltpu.VMEM((N_HEADS_DIL // 2, rows, LANES), F32)],
        compiler_params=_cparams(("arbitrary", "arbitrary", "arbitrary")),
        name=f"dilated_d{dil}",
    )(*([src] * len(qkv_specs)), *consts)


GLA_TM = 256
GLA_STEP_TILES = 2


GLA_NC = GLA_TM // GLA_CHUNK


def _gla_tile(bb, rows, w2, q_ref, k_ref, v_ref, g_ref, bg_ref, nw_ref, tri_ref, o_ref, st_ref):
    g = g_ref[bb, rows, :LANES]
    z = jnp.dot(g, w2, preferred_element_type=F32) + bg_ref[...]
    la = (jnp.minimum(z, 0.0) * (LOG2E / GLA_TAU)
          - jnp.log2(1.0 + jnp.exp2(jnp.abs(z) * -LOG2E)) * (1.0 / GLA_TAU))
    la_h = la.astype(BF16)
    la_l = (la - la_h.astype(F32)).astype(BF16)

    tri = tri_ref[...]
    b = jnp.dot(tri, la_h, preferred_element_type=F32) + jnp.dot(tri, la_l, preferred_element_type=F32)
    b_last = jnp.concatenate(
        [jnp.broadcast_to(b[(ci + 1) * GLA_CHUNK - 1:(ci + 1) * GLA_CHUNK, :], (GLA_CHUNK, D_GLA_K))
         for ci in range(GLA_NC)], axis=0)

    q = q_ref[bb, rows].astype(F32)
    k = k_ref[bb, rows].astype(F32)
    q_t = (q * jnp.exp2(b)).astype(BF16)
    k_t = (k * jnp.exp2(-b)).astype(BF16)
    k_s = k * jnp.exp2(b_last - b)

    causal = tri > 0
    t_chunk = lax.broadcasted_iota(jnp.int32, (GLA_DK, GLA_TM), 1) // GLA_CHUNK
    nw = nw_ref[...]
    for h in range(N_HEADS_GLA):
        cs = slice(h * GLA_DK, (h + 1) * GLA_DK)
        vs = slice(h * GLA_DV, (h + 1) * GLA_DV)
        v_h = v_ref[bb, rows, vs]
        a = lax.dot_general(q_t[:, cs], k_t[:, cs], (((1,), (1,)), ((), ())), preferred_element_type=F32)
        o = jnp.dot(jnp.where(causal, a.astype(BF16), jnp.zeros_like(tri)), v_h, preferred_element_type=F32)
        k_s_t = k_s[:, cs].T.astype(BF16)
        zero = jnp.zeros_like(k_s_t)
        kv_all = jnp.dot(jnp.concatenate([jnp.where(t_chunk == ci, k_s_t, zero) for ci in range(GLA_NC)], axis=0),
                         v_h, preferred_element_type=F32)
        decay_t = jnp.exp2(b_last[:, cs].T)
        st = st_ref[bb, h]
        outs = []
        for ci in range(GLA_NC):
            rs = slice(ci * GLA_CHUNK, (ci + 1) * GLA_CHUNK)
            o_c = o[rs] + jnp.dot(q_t[rs, cs], st.astype(BF16), preferred_element_type=F32)
            decay = jnp.broadcast_to(decay_t[:, ci * GLA_CHUNK:ci * GLA_CHUNK + 1], (GLA_DK, GLA_DV))
            st = decay * st + kv_all[ci * GLA_DK:(ci + 1) * GLA_DK]
            ms = jnp.mean(o_c * o_c, axis=-1, keepdims=True)
            outs.append((o_c * lax.rsqrt(ms + EPS) * nw).astype(o_ref.dtype))
        st_ref[bb, h] = st
        o_ref[bb, rows, vs] = jnp.concatenate(outs, axis=0)


def _gla_kernel(q_ref, k_ref, v_ref, g_ref, w2_ref, bg_ref, nw_ref, tri_ref, wout_ref,
                o_ref, wout_bf_ref, st_ref):

    @pl.when(pl.program_id(0) == 0)
    def _():
        st_ref[...] = jnp.zeros_like(st_ref)

    wout_bf_ref[...] = wout_ref[...].astype(wout_bf_ref.dtype)

    w2 = jnp.concatenate([w2_ref[...].astype(BF16), jnp.zeros((LANES - GLA_GATE_RANK, D_GLA_K), BF16)], axis=0)
    for tt in range(GLA_STEP_TILES):
        rows = slice(tt * GLA_TM, (tt + 1) * GLA_TM)
        for bb in range(st_ref.shape[0]):
            _gla_tile(bb, rows, w2, q_ref, k_ref, v_ref, g_ref, bg_ref, nw_ref, tri_ref, o_ref, st_ref)


def _gla(proj, w2, b_gate, norm_w, w_out, batch, seq):
    step_rows = GLA_TM * GLA_STEP_TILES
    steps = seq // step_rows
    wout_rows = w_out.shape[0] // steps
    pos = np.arange(GLA_TM)
    tri = jnp.asarray((pos[:, None] // GLA_CHUNK == pos[None, :] // GLA_CHUNK) & (pos[None, :] <= pos[:, None]),
                      BF16)
    proj3 = proj.reshape(batch, seq, REST_W)
    rows = lambda width, col: pl.BlockSpec((batch, step_rows, width), lambda i: (0, i, col // width))
    const = lambda shape: pl.BlockSpec(shape, lambda i: (0, 0))
    wout_spec = pl.BlockSpec((wout_rows, w_out.shape[1]), lambda i: (i, 0))
    out, w_out_bf = pl.pallas_call(
        _gla_kernel,
        grid=(steps,),
        in_specs=[
            rows(D_GLA_K, COL_GQ), rows(D_GLA_K, COL_GK), rows(D_GLA_V, COL_GV), rows(GLR_PAD, COL_GLR),
            const((GLA_GATE_RANK, D_GLA_K)), const((1, D_GLA_K)), const((1, GLA_DV)),
            const((GLA_TM, GLA_TM)), wout_spec,
        ],
        out_specs=[rows(D_GLA_V, 0), wout_spec],
        out_shape=[jax.ShapeDtypeStruct((batch, seq, D_GLA_V), BF16), jax.ShapeDtypeStruct(w_out.shape, BF16)],
        scratch_shapes=[pltpu.VMEM((batch, N_HEADS_GLA, GLA_DK, GLA_DV), F32)],
        compiler_params=_cparams(("arbitrary",)),
        name="gla",
    )(proj3, proj3, proj3, proj3, w2, b_gate, norm_w, tri, w_out)
    return out.reshape(batch * seq, D_GLA_V), w_out_bf


MEM_TM = 2048
MEM_UNIT = 128


def _mem_kernel(q_ref, mem_ref, mnw_ref, wkv_ref, o_ref, den_ref, kv_ref):
    @pl.when(pl.program_id(1) == 0)
    def _():
        m = mem_ref[0]
        ms = jnp.mean(m * m, axis=-1, keepdims=True)
        mn = (m * lax.rsqrt(ms + EPS) * mnw_ref[...]).astype(BF16)
        kv_ref[...] = jnp.dot(mn, wkv_ref[...].astype(BF16), preferred_element_type=F32).astype(BF16)

    lane = lax.broadcasted_iota(jnp.int32, (MEM_UNIT, LANES), 1)
    lo = lane < HEAD_DIM
    for u in range(MEM_TM // MEM_UNIT):
        rows = slice(u * MEM_UNIT, (u + 1) * MEM_UNIT)
        den_c = jnp.ones((MEM_UNIT, LANES), F32)
        for j in range(N_HEADS_MEM // 2):
            sl = slice(j * LANES, (j + 1) * LANES)
            r_e, r_o, _, _ = _pair_attention(q_ref[rows, sl], kv_ref[:, sl],
                                             kv_ref[:, D_MEM + j * LANES:D_MEM + (j + 1) * LANES], None, None)
            o_ref[rows, sl] = jnp.where(lo, r_e, r_o).astype(o_ref.dtype)
            den_c = jnp.where(lane == _stat_lane(2 * j), r_e, jnp.where(lane == _stat_lane(2 * j + 1), r_o, den_c))
        den_ref[rows, :] = den_c


def _mem_attn(proj, mem, mem_norm_w, w_kv, batch, seq):
    steps = seq // MEM_TM
    rows = lambda width, col=0: pl.BlockSpec((MEM_TM, width), lambda b, i: (b * steps + i, col // width))
    return pl.pallas_call(
        _mem_kernel,
        grid=(batch, steps),
        in_specs=[
            rows(D_MEM, COL_MQ),
            pl.BlockSpec((1, MEM_LEN, D_MODEL), lambda b, i: (b, 0, 0)),
            pl.BlockSpec((1, D_MODEL), lambda b, i: (0, 0)),
            pl.BlockSpec((D_MODEL, 2 * D_MEM), lambda b, i: (0, 0)),
        ],
        out_specs=[rows(D_MEM), rows(LANES)],
        out_shape=[jax.ShapeDtypeStruct((batch * seq, D_MEM), BF16),
                   jax.ShapeDtypeStruct((batch * seq, LANES), F32)],
        scratch_shapes=[pltpu.VMEM((MEM_LEN, 2 * D_MEM), BF16)],
        compiler_params=_cparams(("parallel", "arbitrary")),
        name="mem_attn",
    )(proj, mem, mem_norm_w, w_kv)


OUT_TM = 512


def _silu(g):
    h = g * 0.5
    return h * jnp.tanh(h) + h


OUT_GW = 256


def _out_kernel(n1_ref, n2_ref, n3_ref, s1_ref, s2_ref, s3_ref, og_ref, nm_ref, dm_ref, ga_ref, gg_ref, gm_ref,
                sm_ref, ex_ref, exm_ref, w_ref, x_ref, nw_ref, out_ref, mix_ref):
    keep = sm_ref[0:1, :]
    fill = sm_ref[1:2, :]
    stats = [s_ref[...] for s_ref in (s1_ref, s2_ref, s3_ref)]
    m1, m2, m3 = [st * keep for st in stats]
    d1, d2, d3 = [pltpu.roll(st, LANES - STAT_DEN_SHIFT, 1) * keep + fill for st in stats]
    mx = jnp.maximum(jnp.maximum(m1, m2), m3)
    t1, t2, t3 = jnp.exp2(m1 - mx), jnp.exp2(m2 - mx), jnp.exp2(m3 - mx)
    inv = 1.0 / (t1 * d1 + t2 * d2 + t3 * d3)
    wts = [(t * inv).astype(BF16) for t in (t1, t2, t3)]
    inv_m = (1.0 / dm_ref[...]).astype(BF16)

    def gate_chunk(lo):
        cols = slice(lo, lo + OUT_GW)
        if lo < D_DIL:
            a = jnp.zeros((OUT_TM, OUT_GW), F32)
            for wt, n_ref in zip(wts, (n1_ref, n2_ref, n3_ref)):
                a = a + jnp.dot(wt, ex_ref[:, cols], preferred_element_type=F32) * n_ref[:, cols].astype(F32)
            g = ga_ref[:, cols]
        elif lo < D_DIL + D_GLA_V:
            lc = slice(lo - D_DIL, lo - D_DIL + OUT_GW)
            a, g = og_ref[:, lc], gg_ref[:, lc]
        else:
            a = jnp.dot(inv_m, exm_ref[...], preferred_element_type=F32) * nm_ref[...].astype(F32)
            g = gm_ref[...]
        mix_ref[:, cols] = a.astype(BF16) * _silu(g)

    for lo in range(0, D_MIX, OUT_GW):
        gate_chunk(lo)
    y = jnp.dot(mix_ref[...], w_ref[...], preferred_element_type=F32)
    ms = jnp.mean(y * y, axis=-1, keepdims=True)
    out_ref[...] = x_ref[...] + y * lax.rsqrt(ms + EPS) * nw_ref[...]


def _out_proj(pats, o_gla, mem_pair, proj, stat_mask, expand, expand_mem, w_out, x2d, norm_w):
    m = x2d.shape[0]
    rows = lambda width, col=0: pl.BlockSpec((OUT_TM, width), lambda s: (s, col // width))
    const = lambda shape: pl.BlockSpec(shape, lambda s: (0, 0), pipeline_mode=pl.Buffered(1))
    nums, stats = zip(*pats)
    return pl.pallas_call(
        _out_kernel,
        grid=(m // OUT_TM,),
        in_specs=[
            *([rows(D_DIL)] * 3), *([rows(LANES)] * 3),
            rows(D_GLA_V), rows(D_MEM), rows(LANES),
            rows(D_DIL, COL_GATE_A), rows(D_GLA_V, COL_GATE_G), rows(D_MEM, COL_GATE_M),
            const((F32_SUBLANES, LANES)), const((LANES, D_DIL)), const((LANES, D_MEM)), const((D_MIX, D_MODEL)),
            rows(D_MODEL), const((1, D_MODEL)),
        ],
        out_specs=rows(D_MODEL),
        out_shape=jax.ShapeDtypeStruct((m, D_MODEL), x2d.dtype),
        scratch_shapes=[pltpu.VMEM((OUT_TM, D_MIX), BF16)],
        compiler_params=_cparams(("parallel",)),
        name="out_proj",
    )(*nums, *stats, o_gla, *mem_pair, proj, proj, proj, stat_mask, expand, expand_mem, w_out, x2d, norm_w)


D_IN_PROJ = 3 * D_DIL + 2 * D_GLA_K + D_GLA_V + GLA_GATE_RANK + D_MEM + D_MIX
SRC_GLR = 3 * D_DIL + 2 * D_GLA_K + D_GLA_V
SRC_MQ = SRC_GLR + GLA_GATE_RANK
SRC_GATE = SRC_MQ + D_MEM
SCALE_ATTN_Q = HEAD_DIM ** -0.5 * LOG2E
SCALE_GLA_Q = GLA_DK ** -0.5
QKV_SEGMENTS = (
    (0, D_DIL, 0, SCALE_ATTN_Q),
    (D_DIL, 2 * D_DIL, D_DIL, 1.0),
)
REST_SEGMENTS = (
    (QKV_W, D_GLA_K, COL_GQ, SCALE_GLA_Q),
    (QKV_W + D_GLA_K, D_GLA_K + D_GLA_V, COL_GK, 1.0),
    (SRC_GATE + D_DIL, D_GLA_V, COL_GATE_G, 1.0),
    (SRC_MQ, D_MEM, COL_MQ, SCALE_ATTN_Q),
    (SRC_GATE + D_DIL + D_GLA_V, D_MEM, COL_GATE_M, 1.0),
    (SRC_GATE, D_DIL, COL_GATE_A, 1.0),
)
WPREP_TK = 512


def _wprep_kernel(wt_ref, qkv_ref, rest_ref):
    for o_ref, segments in ((qkv_ref, QKV_SEGMENTS), (rest_ref, REST_SEGMENTS)):
        for src, width, dst, scale in segments:
            for lo in range(0, width, LANES):
                w = wt_ref[src + lo:src + lo + LANES, :].T
                if scale != 1.0:
                    w = w * scale
                o_ref[:, dst + lo:dst + lo + LANES] = w.astype(o_ref.dtype)
    glr = wt_ref[SRC_GLR:SRC_GLR + LANES, :].T
    keep = lax.broadcasted_iota(jnp.int32, glr.shape, 1) < GLA_GATE_RANK
    rest_ref[:, COL_GLR:COL_GLR + LANES] = jnp.where(keep, glr, 0.0).astype(rest_ref.dtype)
    rest_ref[:, COL_GLR + LANES:COL_GLR + GLR_PAD] = jnp.zeros((WPREP_TK, GLR_PAD - LANES), rest_ref.dtype)


def _regroup_in_weights(w_in):
    return pl.pallas_call(
        _wprep_kernel,
        grid=(D_MODEL // WPREP_TK,),
        in_specs=[pl.BlockSpec((D_IN_PROJ, WPREP_TK), lambda i: (0, i))],
        out_specs=[pl.BlockSpec((WPREP_TK, QKV_W), lambda i: (i, 0)),
                   pl.BlockSpec((WPREP_TK, REST_W), lambda i: (i, 0))],
        out_shape=[jax.ShapeDtypeStruct((D_MODEL, QKV_W), BF16), jax.ShapeDtypeStruct((D_MODEL, REST_W), BF16)],
        compiler_params=_cparams(("parallel",)),
        name="weight_regroup",
    )(w_in.T)


def kernel(x, mem, norm_pre_w, w_in, rel_bias, w_gla_gate2, b_gla_gate, gla_norm_w, mem_norm_w, w_mem_kv,
           w_out, norm_post_w):
    batch, seq, _ = x.shape
    depth = w_in.shape[0]
    expand_np = np.zeros((LANES, D_DIL), np.float32)
    for h in range(N_HEADS_DIL):
        expand_np[_stat_lane(h), h * HEAD_DIM:(h + 1) * HEAD_DIM] = 1.0
    expand = jnp.asarray(expand_np, BF16)
    expand_mem = jnp.asarray(expand_np[:, :D_MEM], BF16)
    stat_mask_np = np.zeros((F32_SUBLANES, LANES), np.float32)
    stat_mask_np[0, [_stat_lane(h) for h in range(N_HEADS_DIL)]] = 1.0
    stat_mask_np[1] = 1.0 - stat_mask_np[0]
    stat_mask = jnp.asarray(stat_mask_np)
    rb_t = jnp.zeros((2 * F32_SUBLANES, LANES), F32).at[:N_HEADS_DIL, :REL_BUCKETS].set(rel_bias.astype(F32).T)
    for l in range(depth):
        x2d = x.reshape(batch * seq, D_MODEL)
        w_qkv, w_rest = _regroup_in_weights(w_in[l])
        h, qkv1, qkv4, qkv16 = _qkv_proj(x2d, norm_pre_w[l][None], w_qkv, batch, seq)
        rest = _rest_proj(h, w_rest)

        srcs = (qkv1.reshape(batch, 1, seq, QKV_W), qkv4, qkv16)
        pats = [_dilated(src, rb_t, batch, seq, window, dil)
                for (window, dil), src in zip(DIL_PATTERNS, srcs)]

        o_gla, w_out_bf = _gla(rest, w_gla_gate2[l], b_gla_gate[l][None], gla_norm_w[l][None], w_out[l],
                               batch, seq)

        mem_pair = _mem_attn(rest, mem, mem_norm_w[l][None], w_mem_kv[l], batch, seq)

        out = _out_proj(pats, o_gla, mem_pair, rest, stat_mask, expand, expand_mem, w_out_bf, x2d,
                        norm_post_w[l][None])
        x = out.reshape(batch, seq, D_MODEL)
    return x
```

```python
import functools
import math

import numpy as np
import jax
import jax.numpy as jnp
from jax import lax
from jax.experimental import pallas as pl
from jax.experimental.pallas import tpu as pltpu

F32 = jnp.float32
BF16 = jnp.bfloat16

D_MODEL = 2048
HEAD_DIM = 64
N_HEADS_DIL = 12
DIL_PATTERNS = ((128, 1), (512, 4), (2048, 16))
DIL_BLOCK = 128
N_HEADS_GLA = 4
GLA_DK = 128
GLA_DV = 256
GLA_GATE_RANK = 16
GLA_TAU = 16.0
GLA_CHUNK = 64
N_HEADS_MEM = 4
MEM_LEN = 256
REL_BUCKETS = 32
REL_MAX_DIST = 2048
EPS = 1e-6
NEG_INF = -1e30

D_DIL = N_HEADS_DIL * HEAD_DIM
D_GLA_K = N_HEADS_GLA * GLA_DK
D_GLA_V = N_HEADS_GLA * GLA_DV
D_MEM = N_HEADS_MEM * HEAD_DIM
D_MIX = D_DIL + D_GLA_V + D_MEM

LANES = 128
F32_SUBLANES = 8
GLR_PAD = 256

QKV_W = 3 * D_DIL
COL_GQ = 0
COL_GK = COL_GQ + D_GLA_K
COL_GV = COL_GK + D_GLA_K
COL_GATE_G = COL_GV + D_GLA_V
COL_MQ = COL_GATE_G + D_GLA_V
COL_GATE_M = COL_MQ + D_MEM
COL_GLR = COL_GATE_M + D_MEM
COL_GATE_A = COL_GLR + GLR_PAD
REST_W = COL_GATE_A + D_DIL

VMEM_LIMIT = 56 * 1024 * 1024


def _cparams(sem):
    return pltpu.CompilerParams(dimension_semantics=sem, vmem_limit_bytes=VMEM_LIMIT)


def _split_hi_lo(w):
    hi = w.astype(BF16)
    return hi, (w - hi.astype(F32)).astype(BF16)


QKV_TM = 512
N_SLABS = D_DIL // LANES
REGROUP_DILS = (4, 16)


def _qkv_proj_kernel(x_ref, nw_ref, w_ref, h_ref, o_ref, x4_ref, x16_ref, slab_ref):
    x = x_ref[...]
    ms = jnp.mean(x * x, axis=-1, keepdims=True)
    h_ref[...] = (x * lax.rsqrt(ms + EPS) * nw_ref[...]).astype(BF16)
    for t in range(QKV_W // D_DIL):
        cols = slice(t * D_DIL, (t + 1) * D_DIL)
        res = jnp.dot(h_ref[...], w_ref[:, cols], preferred_element_type=F32)
        o_ref[:, cols] = res.astype(o_ref.dtype)
        for s in range(N_SLABS):
            slab_ref[t, s] = res[:, s * LANES:(s + 1) * LANES]
        for out_ref, dil in zip((x4_ref, x16_ref), REGROUP_DILS):
            for r in range(dil):
                for s in range(N_SLABS):
                    lo = t * D_DIL + s * LANES
                    out_ref[0, r, :, lo:lo + LANES] = (
                        slab_ref[t, s, pl.ds(r, QKV_TM // dil, stride=dil), :].astype(out_ref.dtype))


def _qkv_proj(x2d, norm_w, w_qkv, batch, seq):
    m = x2d.shape[0]
    tiles_per_seq = seq // QKV_TM
    const = lambda shape: pl.BlockSpec(shape, lambda i: (0, 0), pipeline_mode=pl.Buffered(1))
    grouped = lambda dil: pl.BlockSpec((1, dil, QKV_TM // dil, QKV_W),
                                       lambda i: (i // tiles_per_seq, 0, i % tiles_per_seq, 0))
    return pl.pallas_call(
        _qkv_proj_kernel,
        grid=(m // QKV_TM,),
        in_specs=[
            pl.BlockSpec((QKV_TM, D_MODEL), lambda i: (i, 0)),
            const((1, D_MODEL)), const((D_MODEL, QKV_W)),
        ],
        out_specs=[pl.BlockSpec((QKV_TM, D_MODEL), lambda i: (i, 0)),
                   pl.BlockSpec((QKV_TM, QKV_W), lambda i: (i, 0))] + [grouped(d) for d in REGROUP_DILS],
        out_shape=[jax.ShapeDtypeStruct((m, D_MODEL), BF16), jax.ShapeDtypeStruct((m, QKV_W), BF16)]
        + [jax.ShapeDtypeStruct((batch, d, seq // d, QKV_W), BF16) for d in REGROUP_DILS],
        scratch_shapes=[pltpu.VMEM((QKV_W // D_DIL, N_SLABS, QKV_TM, LANES), F32)],
        compiler_params=_cparams(("parallel",)),
        name="qkv_proj",
    )(x2d, norm_w, w_qkv)


REST_TM = 512


def _rest_proj_kernel(h_ref, w_ref, o_ref):
    o_ref[...] = jnp.dot(h_ref[...], w_ref[...], preferred_element_type=F32).astype(o_ref.dtype)


def _rest_proj(h, w_rest):
    m = h.shape[0]
    const = lambda shape: pl.BlockSpec(shape, lambda i: (0, 0), pipeline_mode=pl.Buffered(1))
    return pl.pallas_call(
        _rest_proj_kernel,
        grid=(m // REST_TM,),
        in_specs=[pl.BlockSpec((REST_TM, D_MODEL), lambda i: (i, 0)), const((D_MODEL, REST_W))],
        out_specs=pl.BlockSpec((REST_TM, REST_W), lambda i: (i, 0)),
        out_shape=jax.ShapeDtypeStruct((m, REST_W), BF16),
        compiler_params=_cparams(("parallel",)),
        name="rest_proj",
    )(h, w_rest)


LOG2E = math.log2(math.e)


def _pair_attention(q_pair, k_pair, v_pair, bias_e, bias_o):
    rows = q_pair.shape[0]
    lo_q = lax.broadcasted_iota(jnp.int32, q_pair.shape, 1) < HEAD_DIM
    zero = jnp.zeros_like(q_pair)
    q_both = jnp.concatenate([jnp.where(lo_q, q_pair, zero), jnp.where(lo_q, zero, q_pair)], axis=0)
    s = lax.dot_general(q_both, k_pair, (((1,), (1,)), ((), ())), preferred_element_type=F32)
    s_e, s_o = s[:rows], s[rows:]
    if bias_e is not None:
        s_e = s_e + bias_e
        s_o = s_o + bias_o
    mx_e = jnp.max(s_e, axis=-1, keepdims=True)
    mx_o = jnp.max(s_o, axis=-1, keepdims=True)
    p_e = jnp.exp2(s_e - mx_e).astype(BF16)
    p_o = jnp.exp2(s_o - mx_o).astype(BF16)
    lo_v = lax.broadcasted_iota(jnp.int32, v_pair.shape, 1) < HEAD_DIM
    one = jnp.ones_like(v_pair)
    r_e = jnp.dot(p_e, jnp.where(lo_v, v_pair, one), preferred_element_type=F32)
    r_o = jnp.dot(p_o, jnp.where(lo_v, one, v_pair), preferred_element_type=F32)
    return r_e, r_o, mx_e, mx_o


def _stat_lane(head):
    return head + HEAD_DIM if head % 2 == 0 else head


STAT_DEN_SHIFT = 16


DIL_UNITS = {1: 8, 4: 4, 16: 8}


def _dil_init_bias(rb_ref, sel_ref, mrow_ref, bias_ref):
    rb_hi, rb_lo = _split_hi_lo(rb_ref[...])
    f = (jnp.dot(rb_hi, sel_ref[...], preferred_element_type=F32)
         + jnp.dot(rb_lo, sel_ref[...], preferred_element_type=F32)) * LOG2E + mrow_ref[...]
    col = lax.broadcasted_iota(jnp.int32, (DIL_BLOCK, 2 * DIL_BLOCK), 1)
    for h in range(N_HEADS_DIL):
        row = jnp.broadcast_to(f[h:h + 1, :], (DIL_BLOCK, 2 * DIL_BLOCK))
        tab = pltpu.roll(row, 0, 1, stride=1, stride_axis=0)
        bias_ref[0, h] = tab
        bias_ref[1, h] = jnp.where(col >= DIL_BLOCK, tab, NEG_INF)


def _dil_unit(q, kp, kc, vp, vc, first, bias_ref, put_num):
    lane = lax.broadcasted_iota(jnp.int32, (DIL_BLOCK, LANES), 1)
    lo = lane < HEAD_DIM
    st_c = jnp.zeros((DIL_BLOCK, LANES), F32)
    for j in range(N_HEADS_DIL // 2):
        sl = slice(j * LANES, (j + 1) * LANES)
        if kp is None:
            k_pair, v_pair = kc(sl), vc(sl)
            bias_e, bias_o = bias_ref[0, 2 * j, :, DIL_BLOCK:], bias_ref[0, 2 * j + 1, :, DIL_BLOCK:]
        else:
            k_pair = jnp.concatenate([kp(sl), kc(sl)], axis=0)
            v_pair = jnp.concatenate([vp(sl), vc(sl)], axis=0)
            bias_e, bias_o = bias_ref[first, 2 * j], bias_ref[first, 2 * j + 1]
        r_e, r_o, mx_e, mx_o = _pair_attention(q(sl), k_pair, v_pair, bias_e, bias_o)
        put_num(j, jnp.where(lo, r_e, r_o))
        l_e, l_o = _stat_lane(2 * j), _stat_lane(2 * j + 1)
        st_c = jnp.where(lane == l_e, mx_e, jnp.where(lane == l_e + STAT_DEN_SHIFT, r_e, st_c))
        st_c = jnp.where(lane == l_o, mx_o, jnp.where(lane == l_o + STAT_DEN_SHIFT, r_o, st_c))
    return st_c


def _mem_kv(mem_ref, mnw_ref, wkv_ref, kv_ref):
    m = mem_ref[0]
    ms = jnp.mean(m * m, axis=-1, keepdims=True)
    mn = (m * lax.rsqrt(ms + EPS) * mnw_ref[...]).astype(BF16)
    kv_ref[...] = jnp.dot(mn, wkv_ref[...].astype(BF16), preferred_element_type=F32).astype(BF16)


def _mem_unit(q, kv_ref, put_num):
    lane = lax.broadcasted_iota(jnp.int32, (DIL_BLOCK, LANES), 1)
    lo = lane < HEAD_DIM
    den_c = jnp.ones((DIL_BLOCK, LANES), F32)
    for j in range(N_HEADS_MEM // 2):
        sl = slice(j * LANES, (j + 1) * LANES)
        r_e, r_o, _, _ = _pair_attention(q(sl), kv_ref[:, sl], kv_ref[:, D_MEM + j * LANES:D_MEM + (j + 1) * LANES],
                                         None, None)
        put_num(j, jnp.where(lo, r_e, r_o))
        den_c = jnp.where(lane == _stat_lane(2 * j), r_e, jnp.where(lane == _stat_lane(2 * j + 1), r_o, den_c))
    return den_c


def _dil1_kernel(q_ref, kp_ref, kc_ref, vp_ref, vc_ref, rb_ref, sel_ref, mrow_ref, mq_ref, mem_ref, mnw_ref, wkv_ref,
                 o_ref, st_ref, om_ref, dm_ref, bias_ref, kv_ref):
    @pl.when((pl.program_id(0) == 0) & (pl.program_id(1) == 0))
    def _():
        _dil_init_bias(rb_ref, sel_ref, mrow_ref, bias_ref)

    @pl.when(pl.program_id(1) == 0)
    def _():
        _mem_kv(mem_ref, mnw_ref, wkv_ref, kv_ref)

    for u in range(DIL_UNITS[1]):
        rows = slice(u * DIL_BLOCK, (u + 1) * DIL_BLOCK)
        prev = slice((u - 1) * DIL_BLOCK, u * DIL_BLOCK)
        if u == 0:
            first = (pl.program_id(1) == 0).astype(jnp.int32)
            kp, vp = (lambda sl: kp_ref[:, sl]), (lambda sl: vp_ref[:, sl])
        else:
            first = 0
            kp, vp = (lambda sl, p=prev: kc_ref[p, sl]), (lambda sl, p=prev: vc_ref[p, sl])

        def put_num(j, num, rows=rows):
            o_ref[rows, j * LANES:(j + 1) * LANES] = num.astype(o_ref.dtype)

        st_ref[rows, :] = _dil_unit(lambda sl, r=rows: q_ref[r, sl], kp, lambda sl, r=rows: kc_ref[r, sl],
                                    vp, lambda sl, r=rows: vc_ref[r, sl], first, bias_ref, put_num)

        def put_mem(j, num, rows=rows):
            om_ref[rows, j * LANES:(j + 1) * LANES] = num.astype(om_ref.dtype)

        dm_ref[rows, :] = _mem_unit(lambda sl, r=rows: mq_ref[r, sl], kv_ref, put_mem)


DIL_LBLOCKS = {4: 2, 16: 2}


def _dil_kernel(*refs, dil, whole_seq):
    if whole_seq:
        q_ref, kc_ref, vc_ref, *rest = refs
        kp_ref = vp_ref = None
    else:
        q_ref, kp_ref, kc_ref, vp_ref, vc_ref, *rest = refs
    rb_ref, sel_ref, mrow_ref, o_ref, st_ref, bias_ref, acc_ref = rest
    b, lb, g = pl.program_id(0), pl.program_id(1), pl.program_id(2)
    nr, nl = DIL_UNITS[dil], DIL_LBLOCKS[dil]

    @pl.when((b == 0) & (lb == 0) & (g == 0))
    def _():
        _dil_init_bias(rb_ref, sel_ref, mrow_ref, bias_ref)

    for ll in range(nl):
        sub = slice(ll * DIL_BLOCK, (ll + 1) * DIL_BLOCK)
        prev = slice((ll - 1) * DIL_BLOCK, ll * DIL_BLOCK)
        first = (lb == 0).astype(jnp.int32) if ll == 0 else 0
        for u in range(nr):
            rows = pl.ds(ll * DIL_BLOCK * dil + g * nr + u, DIL_BLOCK, stride=dil)
            if ll == 0 and whole_seq:
                kp = vp = None
            elif ll == 0:
                kp, vp = (lambda sl, u=u: kp_ref[u, :, sl]), (lambda sl, u=u: vp_ref[u, :, sl])
            else:
                kp = lambda sl, u=u, p=prev: kc_ref[u, p, sl]
                vp = lambda sl, u=u, p=prev: vc_ref[u, p, sl]

            def put_num(j, num, rows=rows):
                acc_ref[j, rows, :] = num

            st_ref[rows, :] = _dil_unit(lambda sl, u=u, r=sub: q_ref[u, r, sl], kp,
                                        lambda sl, u=u, r=sub: kc_ref[u, r, sl], vp,
                                        lambda sl, u=u, r=sub: vc_ref[u, r, sl], first, bias_ref, put_num)

    @pl.when(g == dil // nr - 1)
    def _():
        for j in range(N_HEADS_DIL // 2):
            o_ref[:, j * LANES:(j + 1) * LANES] = acc_ref[j].astype(o_ref.dtype)


def _t5_bucket_np(dist):
    max_exact = REL_BUCKETS // 2
    n = np.maximum(dist, 1).astype(np.float32)
    large = max_exact + (np.log(n / np.float32(max_exact)) / np.float32(math.log(REL_MAX_DIST / max_exact))
                         * np.float32(REL_BUCKETS - max_exact)).astype(np.int32)
    large = np.minimum(large, REL_BUCKETS - 1)
    return np.where(dist < max_exact, dist, large)


def _bias_selector(window, dil):
    back = window // dil
    x = np.arange(2 * DIL_BLOCK)
    rel = DIL_BLOCK - x
    valid = (rel >= 0) & (rel <= back)
    bucket = _t5_bucket_np(np.clip(rel, 0, back) * dil)
    sel = np.zeros((LANES, 2 * DIL_BLOCK), np.float32)
    sel[bucket[valid], x[valid]] = 1.0
    mrow = np.where(valid, 0.0, NEG_INF).astype(np.float32)[None]
    return jnp.asarray(sel, BF16), jnp.asarray(mrow)


def _dilated(src, rb_t, batch, seq, window, dil, mem_args=None):
    sub_len = seq // dil
    nb = sub_len // DIL_BLOCK
    sel, mrow = _bias_selector(window, dil)
    nu = DIL_UNITS[dil]
    consts = (rb_t, sel, mrow)
    const = lambda a: pl.BlockSpec(a.shape, lambda *_: (0, 0))
    stat_shape = jax.ShapeDtypeStruct((batch * seq, LANES), F32)
    out_shape = [jax.ShapeDtypeStruct((batch * seq, D_DIL), BF16), stat_shape]
    bias_scratch = pltpu.VMEM((2, N_HEADS_DIL, DIL_BLOCK, 2 * DIL_BLOCK), F32)

    if dil == 1:
        rest, mem, mem_norm_w, w_kv = mem_args
        rows = nu * DIL_BLOCK
        steps = nb // nu
        cur = lambda col: pl.BlockSpec((None, None, rows, D_DIL), lambda b, l: (b, 0, l, col))
        prev = lambda col: pl.BlockSpec((None, None, DIL_BLOCK, D_DIL),
                                        lambda b, l: (b, 0, jnp.maximum(nu * l - 1, 0), col))
        out = lambda width, col=0: pl.BlockSpec((rows, width), lambda b, l: (b * steps + l, col // width))
        mem_specs = [out(D_MEM, COL_MQ), pl.BlockSpec((1, MEM_LEN, D_MODEL), lambda b, l: (b, 0, 0)),
                     const(mem_norm_w), const(w_kv)]
        num, stats, mem_num, mem_den = pl.pallas_call(
            _dil1_kernel,
            grid=(batch, steps),
            in_specs=[cur(0), prev(1), cur(1), prev(2), cur(2)] + [const(a) for a in consts] + mem_specs,
            out_specs=[out(D_DIL), out(LANES), out(D_MEM), out(LANES)],
            out_shape=out_shape + [jax.ShapeDtypeStruct((batch * seq, D_MEM), BF16), stat_shape],
            scratch_shapes=[bias_scratch, pltpu.VMEM((MEM_LEN, 2 * D_MEM), BF16)],
            compiler_params=_cparams(("arbitrary", "arbitrary")),
            name="dilated_d1",
        )(src, src, src, src, src, *consts, rest, mem, mem_norm_w, w_kv)
        return (num, stats), (mem_num, mem_den)

    nl = DIL_LBLOCKS[dil]
    steps = nb // nl
    rows = nl * DIL_BLOCK * dil
    cur = lambda col: pl.BlockSpec((None, nu, nl * DIL_BLOCK, D_DIL), lambda b, l, g: (b, g, l, col))
    prev = lambda col: pl.BlockSpec((None, nu, DIL_BLOCK, D_DIL),
                                    lambda b, l, g: (b, g, jnp.maximum(nl * l - 1, 0), col))
    out = lambda width: pl.BlockSpec((rows, width), lambda b, l, g: (b * steps + l, 0))
    whole_seq = steps == 1
    qkv_specs = [cur(0), cur(1), cur(2)] if whole_seq else [cur(0), prev(1), cur(1), prev(2), cur(2)]
    return pl.pallas_call(
        functools.partial(_dil_kernel, dil=dil, whole_seq=whole_seq),
        grid=(batch, steps, dil // nu),
        in_specs=qkv_specs + [const(a) for a in consts],
        out_specs=[out(D_DIL), out(LANES)],
        out_shape=out_shape,
        scratch_shapes=[bias_scratch, pltpu.VMEM((N_HEADS_DIL // 2, rows, LANES), F32)],
        compiler_params=_cparams(("arbitrary", "arbitrary", "arbitrary")),
        name=f"dilated_d{dil}",
    )(*([src] * len(qkv_specs)), *consts)


GLA_TM = 256
GLA_STEP_TILES = 2


GLA_NC = GLA_TM // GLA_CHUNK


def _gla_tile(bb, rows, w2, q_ref, k_ref, v_ref, g_ref, bg_ref, nw_ref, tri_ref, o_ref, st_ref):
    g = g_ref[bb, rows, :LANES]
    z = jnp.dot(g, w2, preferred_element_type=F32) + bg_ref[...]
    la = (jnp.minimum(z, 0.0) * (LOG2E / GLA_TAU)
          - jnp.log2(1.0 + jnp.exp2(jnp.abs(z) * -LOG2E)) * (1.0 / GLA_TAU))
    la_h = la.astype(BF16)
    la_l = (la - la_h.astype(F32)).astype(BF16)

    tri = tri_ref[...]
    b = jnp.dot(tri, la_h, preferred_element_type=F32) + jnp.dot(tri, la_l, preferred_element_type=F32)
    b_last = jnp.concatenate(
        [jnp.broadcast_to(b[(ci + 1) * GLA_CHUNK - 1:(ci + 1) * GLA_CHUNK, :], (GLA_CHUNK, D_GLA_K))
         for ci in range(GLA_NC)], axis=0)

    q = q_ref[bb, rows].astype(F32)
    k = k_ref[bb, rows].astype(F32)
    q_t = (q * jnp.exp2(b)).astype(BF16)
    k_t = (k * jnp.exp2(-b)).astype(BF16)
    k_s = k * jnp.exp2(b_last - b)

    causal = tri > 0
    t_chunk = lax.broadcasted_iota(jnp.int32, (GLA_DK, GLA_TM), 1) // GLA_CHUNK
    nw = nw_ref[...]
    for h in range(N_HEADS_GLA):
        cs = slice(h * GLA_DK, (h + 1) * GLA_DK)
        vs = slice(h * GLA_DV, (h + 1) * GLA_DV)
        v_h = v_ref[bb, rows, vs]
        a = lax.dot_general(q_t[:, cs], k_t[:, cs], (((1,), (1,)), ((), ())), preferred_element_type=F32)
        o = jnp.dot(jnp.where(causal, a.astype(BF16), jnp.zeros_like(tri)), v_h, preferred_element_type=F32)
        k_s_t = k_s[:, cs].T.astype(BF16)
        zero = jnp.zeros_like(k_s_t)
        kv_all = jnp.dot(jnp.concatenate([jnp.where(t_chunk == ci, k_s_t, zero) for ci in range(GLA_NC)], axis=0),
                         v_h, preferred_element_type=F32)
        decay_t = jnp.exp2(b_last[:, cs].T)
        st = st_ref[bb, h]
        outs = []
        for ci in range(GLA_NC):
            rs = slice(ci * GLA_CHUNK, (ci + 1) * GLA_CHUNK)
            o_c = o[rs] + jnp.dot(q_t[rs, cs], st.astype(BF16), preferred_element_type=F32)
            decay = jnp.broadcast_to(decay_t[:, ci * GLA_CHUNK:ci * GLA_CHUNK + 1], (GLA_DK, GLA_DV))
            st = decay * st + kv_all[ci * GLA_DK:(ci + 1) * GLA_DK]
            ms = jnp.mean(o_c * o_c, axis=-1, keepdims=True)
            outs.append((o_c * lax.rsqrt(ms + EPS) * nw).astype(o_ref.dtype))
        st_ref[bb, h] = st
        o_ref[bb, rows, vs] = jnp.concatenate(outs, axis=0)


def _gla_kernel(q_ref, k_ref, v_ref, g_ref, w2_ref, bg_ref, nw_ref, tri_ref, wout_ref,
                o_ref, wout_bf_ref, st_ref):

    @pl.when(pl.program_id(0) == 0)
    def _():
        st_ref[...] = jnp.zeros_like(st_ref)

    wout_bf_ref[...] = wout_ref[...].astype(wout_bf_ref.dtype)

    w2 = jnp.concatenate([w2_ref[...].astype(BF16), jnp.zeros((LANES - GLA_GATE_RANK, D_GLA_K), BF16)], axis=0)
    for tt in range(GLA_STEP_TILES):
        rows = slice(tt * GLA_TM, (tt + 1) * GLA_TM)
        for bb in range(st_ref.shape[0]):
            _gla_tile(bb, rows, w2, q_ref, k_ref, v_ref, g_ref, bg_ref, nw_ref, tri_ref, o_ref, st_ref)


def _gla(proj, w2, b_gate, norm_w, w_out, batch, seq):
    step_rows = GLA_TM * GLA_STEP_TILES
    steps = seq // step_rows
    wout_rows = w_out.shape[0] // steps
    pos = np.arange(GLA_TM)
    tri = jnp.asarray((pos[:, None] // GLA_CHUNK == pos[None, :] // GLA_CHUNK) & (pos[None, :] <= pos[:, None]),
                      BF16)
    proj3 = proj.reshape(batch, seq, REST_W)
    rows = lambda width, col: pl.BlockSpec((batch, step_rows, width), lambda i: (0, i, col // width))
    const = lambda shape: pl.BlockSpec(shape, lambda i: (0, 0))
    wout_spec = pl.BlockSpec((wout_rows, w_out.shape[1]), lambda i: (i, 0))
    out, w_out_bf = pl.pallas_call(
        _gla_kernel,
        grid=(steps,),
        in_specs=[
            rows(D_GLA_K, COL_GQ), rows(D_GLA_K, COL_GK), rows(D_GLA_V, COL_GV), rows(GLR_PAD, COL_GLR),
            const((GLA_GATE_RANK, D_GLA_K)), const((1, D_GLA_K)), const((1, GLA_DV)),
            const((GLA_TM, GLA_TM)), wout_spec,
        ],
        out_specs=[rows(D_GLA_V, 0), wout_spec],
        out_shape=[jax.ShapeDtypeStruct((batch, seq, D_GLA_V), BF16), jax.ShapeDtypeStruct(w_out.shape, BF16)],
        scratch_shapes=[pltpu.VMEM((batch, N_HEADS_GLA, GLA_DK, GLA_DV), F32)],
        compiler_params=_cparams(("arbitrary",)),
        name="gla",
    )(proj3, proj3, proj3, proj3, w2, b_gate, norm_w, tri, w_out)
    return out.reshape(batch * seq, D_GLA_V), w_out_bf


OUT_TM = 512


def _silu(g):
    h = g * 0.5
    return h * jnp.tanh(h) + h


OUT_GW = 256


def _out_kernel(n1_ref, n2_ref, n3_ref, s1_ref, s2_ref, s3_ref, og_ref, nm_ref, dm_ref, ga_ref, gg_ref, gm_ref,
                sm_ref, ex_ref, exm_ref, w_ref, x_ref, nw_ref, out_ref, mix_ref):
    keep = sm_ref[0:1, :]
    fill = sm_ref[1:2, :]
    stats = [s_ref[...] for s_ref in (s1_ref, s2_ref, s3_ref)]
    m1, m2, m3 = [st * keep for st in stats]
    d1, d2, d3 = [pltpu.roll(st, LANES - STAT_DEN_SHIFT, 1) * keep + fill for st in stats]
    mx = jnp.maximum(jnp.maximum(m1, m2), m3)
    t1, t2, t3 = jnp.exp2(m1 - mx), jnp.exp2(m2 - mx), jnp.exp2(m3 - mx)
    inv = 1.0 / (t1 * d1 + t2 * d2 + t3 * d3)
    wts = [(t * inv).astype(BF16) for t in (t1, t2, t3)]
    inv_m = (1.0 / dm_ref[...]).astype(BF16)

    def gate_chunk(lo):
        cols = slice(lo, lo + OUT_GW)
        if lo < D_DIL:
            a = jnp.zeros((OUT_TM, OUT_GW), F32)
            for wt, n_ref in zip(wts, (n1_ref, n2_ref, n3_ref)):
                a = a + jnp.dot(wt, ex_ref[:, cols], preferred_element_type=F32) * n_ref[:, cols].astype(F32)
            g = ga_ref[:, cols]
        elif lo < D_DIL + D_GLA_V:
            lc = slice(lo - D_DIL, lo - D_DIL + OUT_GW)
            a, g = og_ref[:, lc], gg_ref[:, lc]
        else:
            a = jnp.dot(inv_m, exm_ref[...], preferred_element_type=F32) * nm_ref[...].astype(F32)
            g = gm_ref[...]
        mix_ref[:, cols] = a.astype(BF16) * _silu(g)

    for lo in range(0, D_MIX, OUT_GW):
        gate_chunk(lo)
    y = jnp.dot(mix_ref[...], w_ref[...], preferred_element_type=F32)
    ms = jnp.mean(y * y, axis=-1, keepdims=True)
    out_ref[...] = x_ref[...] + y * lax.rsqrt(ms + EPS) * nw_ref[...]


def _out_proj(pats, o_gla, mem_pair, proj, stat_mask, expand, expand_mem, w_out, x2d, norm_w):
    m = x2d.shape[0]
    rows = lambda width, col=0: pl.BlockSpec((OUT_TM, width), lambda s: (s, col // width))
    const = lambda shape: pl.BlockSpec(shape, lambda s: (0, 0), pipeline_mode=pl.Buffered(1))
    nums, stats = zip(*pats)
    return pl.pallas_call(
        _out_kernel,
        grid=(m // OUT_TM,),
        in_specs=[
            *([rows(D_DIL)] * 3), *([rows(LANES)] * 3),
            rows(D_GLA_V), rows(D_MEM), rows(LANES),
            rows(D_DIL, COL_GATE_A), rows(D_GLA_V, COL_GATE_G), rows(D_MEM, COL_GATE_M),
            const((F32_SUBLANES, LANES)), const((LANES, D_DIL)), const((LANES, D_MEM)), const((D_MIX, D_MODEL)),
            rows(D_MODEL), const((1, D_MODEL)),
        ],
        out_specs=rows(D_MODEL),
        out_shape=jax.ShapeDtypeStruct((m, D_MODEL), x2d.dtype),
        scratch_shapes=[pltpu.VMEM((OUT_TM, D_MIX), BF16)],
        compiler_params=_cparams(("parallel",)),
        name="out_proj",
    )(*nums, *stats, o_gla, *mem_pair, proj, proj, proj, stat_mask, expand, expand_mem, w_out, x2d, norm_w)


D_IN_PROJ = 3 * D_DIL + 2 * D_GLA_K + D_GLA_V + GLA_GATE_RANK + D_MEM + D_MIX
SRC_GLR = 3 * D_DIL + 2 * D_GLA_K + D_GLA_V
SRC_MQ = SRC_GLR + GLA_GATE_RANK
SRC_GATE = SRC_MQ + D_MEM
SCALE_ATTN_Q = HEAD_DIM ** -0.5 * LOG2E
SCALE_GLA_Q = GLA_DK ** -0.5
QKV_SEGMENTS = (
    (0, D_DIL, 0, SCALE_ATTN_Q),
    (D_DIL, 2 * D_DIL, D_DIL, 1.0),
)
REST_SEGMENTS = (
    (QKV_W, D_GLA_K, COL_GQ, SCALE_GLA_Q),
    (QKV_W + D_GLA_K, D_GLA_K + D_GLA_V, COL_GK, 1.0),
    (SRC_GATE + D_DIL, D_GLA_V, COL_GATE_G, 1.0),
    (SRC_MQ, D_MEM, COL_MQ, SCALE_ATTN_Q),
    (SRC_GATE + D_DIL + D_GLA_V, D_MEM, COL_GATE_M, 1.0),
    (SRC_GATE, D_DIL, COL_GATE_A, 1.0),
)
WPREP_TK = 512


def _wprep_kernel(wt_ref, qkv_ref, rest_ref):
    for o_ref, segments in ((qkv_ref, QKV_SEGMENTS), (rest_ref, REST_SEGMENTS)):
        for src, width, dst, scale in segments:
            for lo in range(0, width, LANES):
                w = wt_ref[src + lo:src + lo + LANES, :].T
                if scale != 1.0:
                    w = w * scale
                o_ref[:, dst + lo:dst + lo + LANES] = w.astype(o_ref.dtype)
    glr = wt_ref[SRC_GLR:SRC_GLR + LANES, :].T
    keep = lax.broadcasted_iota(jnp.int32, glr.shape, 1) < GLA_GATE_RANK
    rest_ref[:, COL_GLR:COL_GLR + LANES] = jnp.where(keep, glr, 0.0).astype(rest_ref.dtype)
    rest_ref[:, COL_GLR + LANES:COL_GLR + GLR_PAD] = jnp.zeros((WPREP_TK, GLR_PAD - LANES), rest_ref.dtype)


def _regroup_in_weights(w_in):
    return pl.pallas_call(
        _wprep_kernel,
        grid=(D_MODEL // WPREP_TK,),
        in_specs=[pl.BlockSpec((D_IN_PROJ, WPREP_TK), lambda i: (0, i))],
        out_specs=[pl.BlockSpec((WPREP_TK, QKV_W), lambda i: (i, 0)),
                   pl.BlockSpec((WPREP_TK, REST_W), lambda i: (i, 0))],
        out_shape=[jax.ShapeDtypeStruct((D_MODEL, QKV_W), BF16), jax.ShapeDtypeStruct((D_MODEL, REST_W), BF16)],
        compiler_params=_cparams(("parallel",)),
        name="weight_regroup",
    )(w_in.T)


def kernel(x, mem, norm_pre_w, w_in, rel_bias, w_gla_gate2, b_gla_gate, gla_norm_w, mem_norm_w, w_mem_kv,
           w_out, norm_post_w):
    batch, seq, _ = x.shape
    depth = w_in.shape[0]
    expand_np = np.zeros((LANES, D_DIL), np.float32)
    for h in range(N_HEADS_DIL):
        expand_np[_stat_lane(h), h * HEAD_DIM:(h + 1) * HEAD_DIM] = 1.0
    expand = jnp.asarray(expand_np, BF16)
    expand_mem = jnp.asarray(expand_np[:, :D_MEM], BF16)
    stat_mask_np = np.zeros((F32_SUBLANES, LANES), np.float32)
    stat_mask_np[0, [_stat_lane(h) for h in range(N_HEADS_DIL)]] = 1.0
    stat_mask_np[1] = 1.0 - stat_mask_np[0]
    stat_mask = jnp.asarray(stat_mask_np)
    rb_t = jnp.zeros((2 * F32_SUBLANES, LANES), F32).at[:N_HEADS_DIL, :REL_BUCKETS].set(rel_bias.astype(F32).T)
    for l in range(depth):
        x2d = x.reshape(batch * seq, D_MODEL)
        w_qkv, w_rest = _regroup_in_weights(w_in[l])
        h, qkv1, qkv4, qkv16 = _qkv_proj(x2d, norm_pre_w[l][None], w_qkv, batch, seq)
        rest = _rest_proj(h, w_rest)

        srcs = (qkv1.reshape(batch, 1, seq, QKV_W), qkv4, qkv16)
        mem_args = (rest, mem, mem_norm_w[l][None], w_mem_kv[l])
        pats = [_dilated(src, rb_t, batch, seq, window, dil, mem_args if dil == 1 else None)
                for (window, dil), src in zip(DIL_PATTERNS, srcs)]
        pats[0], mem_pair = pats[0]

        o_gla, w_out_bf = _gla(rest, w_gla_gate2[l], b_gla_gate[l][None], gla_norm_w[l][None], w_out[l],
                               batch, seq)

        out = _out_proj(pats, o_gla, mem_pair, rest, stat_mask, expand, expand_mem, w_out_bf, x2d,
                        norm_post_w[l][None])
        x = out.reshape(batch, seq, D_MODEL)
    return x
```

```python
import functools
import math

import numpy as np
import jax
import jax.numpy as jnp
from jax import lax
from jax.experimental import pallas as pl
from jax.experimental.pallas import tpu as pltpu

F32 = jnp.float32
BF16 = jnp.bfloat16

D_MODEL = 2048
HEAD_DIM = 64
N_HEADS_DIL = 12
DIL_PATTERNS = ((128, 1), (512, 4), (2048, 16))
DIL_BLOCK = 128
N_HEADS_GLA = 4
GLA_DK = 128
GLA_DV = 256
GLA_GATE_RANK = 16
GLA_TAU = 16.0
GLA_CHUNK = 64
N_HEADS_MEM = 4
MEM_LEN = 256
REL_BUCKETS = 32
REL_MAX_DIST = 2048
EPS = 1e-6
NEG_INF = -1e30

D_DIL = N_HEADS_DIL * HEAD_DIM
D_GLA_K = N_HEADS_GLA * GLA_DK
D_GLA_V = N_HEADS_GLA * GLA_DV
D_MEM = N_HEADS_MEM * HEAD_DIM
D_MIX = D_DIL + D_GLA_V + D_MEM

LANES = 128
F32_SUBLANES = 8
GLR_PAD = 256

QKV_W = 3 * D_DIL
COL_GQ = 0
COL_GK = COL_GQ + D_GLA_K
COL_GV = COL_GK + D_GLA_K
COL_GATE_G = COL_GV + D_GLA_V
COL_MQ = COL_GATE_G + D_GLA_V
COL_GATE_M = COL_MQ + D_MEM
COL_GLR = COL_GATE_M + D_MEM
COL_GATE_A = COL_GLR + GLR_PAD
REST_W = COL_GATE_A + D_DIL

VMEM_LIMIT = 56 * 1024 * 1024


def _cparams(sem):
    return pltpu.CompilerParams(dimension_semantics=sem, vmem_limit_bytes=VMEM_LIMIT)


def _split_hi_lo(w):
    hi = w.astype(BF16)
    return hi, (w - hi.astype(F32)).astype(BF16)


QKV_TM = 512
N_SLABS = D_DIL // LANES
REGROUP_DILS = (4, 16)


def _qkv_proj_kernel(x_ref, nw_ref, w_ref, h_ref, o_ref, x4_ref, x16_ref, slab_ref):
    x = x_ref[...]
    ms = jnp.mean(x * x, axis=-1, keepdims=True)
    h_ref[...] = (x * lax.rsqrt(ms + EPS) * nw_ref[...]).astype(BF16)
    for t in range(QKV_W // D_DIL):
        cols = slice(t * D_DIL, (t + 1) * D_DIL)
        res = jnp.dot(h_ref[...], w_ref[:, cols], preferred_element_type=F32)
        o_ref[:, cols] = res.astype(o_ref.dtype)
        for s in range(N_SLABS):
            slab_ref[t, s] = res[:, s * LANES:(s + 1) * LANES]
        for out_ref, dil in zip((x4_ref, x16_ref), REGROUP_DILS):
            for r in range(dil):
                for s in range(N_SLABS):
                    lo = t * D_DIL + s * LANES
                    out_ref[0, r, :, lo:lo + LANES] = (
                        slab_ref[t, s, pl.ds(r, QKV_TM // dil, stride=dil), :].astype(out_ref.dtype))


def _qkv_proj(x2d, norm_w, w_qkv, batch, seq):
    m = x2d.shape[0]
    tiles_per_seq = seq // QKV_TM
    const = lambda shape: pl.BlockSpec(shape, lambda i: (0, 0), pipeline_mode=pl.Buffered(1))
    grouped = lambda dil: pl.BlockSpec((1, dil, QKV_TM // dil, QKV_W),
                                       lambda i: (i // tiles_per_seq, 0, i % tiles_per_seq, 0))
    return pl.pallas_call(
        _qkv_proj_kernel,
        grid=(m // QKV_TM,),
        in_specs=[
            pl.BlockSpec((QKV_TM, D_MODEL), lambda i: (i, 0)),
            const((1, D_MODEL)), const((D_MODEL, QKV_W)),
        ],
        out_specs=[pl.BlockSpec((QKV_TM, D_MODEL), lambda i: (i, 0)),
                   pl.BlockSpec((QKV_TM, QKV_W), lambda i: (i, 0))] + [grouped(d) for d in REGROUP_DILS],
        out_shape=[jax.ShapeDtypeStruct((m, D_MODEL), BF16), jax.ShapeDtypeStruct((m, QKV_W), BF16)]
        + [jax.ShapeDtypeStruct((batch, d, seq // d, QKV_W), BF16) for d in REGROUP_DILS],
        scratch_shapes=[pltpu.VMEM((QKV_W // D_DIL, N_SLABS, QKV_TM, LANES), F32)],
        compiler_params=_cparams(("parallel",)),
        name="qkv_proj",
    )(x2d, norm_w, w_qkv)


REST_TM = 512


def _rest_proj_kernel(h_ref, w_ref, o_ref):
    o_ref[...] = jnp.dot(h_ref[...], w_ref[...], preferred_element_type=F32).astype(o_ref.dtype)


def _rest_proj(h, w_rest):
    m = h.shape[0]
    const = lambda shape: pl.BlockSpec(shape, lambda i: (0, 0), pipeline_mode=pl.Buffered(1))
    return pl.pallas_call(
        _rest_proj_kernel,
        grid=(m // REST_TM,),
        in_specs=[pl.BlockSpec((REST_TM, D_MODEL), lambda i: (i, 0)), const((D_MODEL, REST_W))],
        out_specs=pl.BlockSpec((REST_TM, REST_W), lambda i: (i, 0)),
        out_shape=jax.ShapeDtypeStruct((m, REST_W), BF16),
        compiler_params=_cparams(("parallel",)),
        name="rest_proj",
    )(h, w_rest)


LOG2E = math.log2(math.e)


def _pair_attention(q_pair, k_pair, v_pair, bias_e, bias_o):
    rows = q_pair.shape[0]
    lo_q = lax.broadcasted_iota(jnp.int32, q_pair.shape, 1) < HEAD_DIM
    zero = jnp.zeros_like(q_pair)
    q_both = jnp.concatenate([jnp.where(lo_q, q_pair, zero), jnp.where(lo_q, zero, q_pair)], axis=0)
    s = lax.dot_general(q_both, k_pair, (((1,), (1,)), ((), ())), preferred_element_type=F32)
    s_e, s_o = s[:rows], s[rows:]
    if bias_e is not None:
        s_e = s_e + bias_e
        s_o = s_o + bias_o
    mx_e = jnp.max(s_e, axis=-1, keepdims=True)
    mx_o = jnp.max(s_o, axis=-1, keepdims=True)
    p_e = jnp.exp2(s_e - mx_e).astype(BF16)
    p_o = jnp.exp2(s_o - mx_o).astype(BF16)
    lo_v = lax.broadcasted_iota(jnp.int32, v_pair.shape, 1) < HEAD_DIM
    one = jnp.ones_like(v_pair)
    r_e = jnp.dot(p_e, jnp.where(lo_v, v_pair, one), preferred_element_type=F32)
    r_o = jnp.dot(p_o, jnp.where(lo_v, one, v_pair), preferred_element_type=F32)
    return r_e, r_o, mx_e, mx_o


def _stat_lane(head):
    return head + HEAD_DIM if head % 2 == 0 else head


STAT_DEN_SHIFT = 16


DIL_UNITS = {1: 8, 4: 4, 16: 8}


def _dil_init_bias(rb_ref, sel_ref, mrow_ref, bias_ref):
    rb_hi, rb_lo = _split_hi_lo(rb_ref[...])
    f = (jnp.dot(rb_hi, sel_ref[...], preferred_element_type=F32)
         + jnp.dot(rb_lo, sel_ref[...], preferred_element_type=F32)) * LOG2E + mrow_ref[...]
    col = lax.broadcasted_iota(jnp.int32, (DIL_BLOCK, 2 * DIL_BLOCK), 1)
    for h in range(N_HEADS_DIL):
        row = jnp.broadcast_to(f[h:h + 1, :], (DIL_BLOCK, 2 * DIL_BLOCK))
        tab = pltpu.roll(row, 0, 1, stride=1, stride_axis=0)
        bias_ref[0, h] = tab
        bias_ref[1, h] = jnp.where(col >= DIL_BLOCK, tab, NEG_INF)


def _dil_unit(q, kp, kc, vp, vc, first, bias_ref, put_num):
    lane = lax.broadcasted_iota(jnp.int32, (DIL_BLOCK, LANES), 1)
    lo = lane < HEAD_DIM
    st_c = jnp.zeros((DIL_BLOCK, LANES), F32)
    for j in range(N_HEADS_DIL // 2):
        sl = slice(j * LANES, (j + 1) * LANES)
        if kp is None:
            k_pair, v_pair = kc(sl), vc(sl)
            bias_e, bias_o = bias_ref[0, 2 * j, :, DIL_BLOCK:], bias_ref[0, 2 * j + 1, :, DIL_BLOCK:]
        else:
            k_pair = jnp.concatenate([kp(sl), kc(sl)], axis=0)
            v_pair = jnp.concatenate([vp(sl), vc(sl)], axis=0)
            bias_e, bias_o = bias_ref[first, 2 * j], bias_ref[first, 2 * j + 1]
        r_e, r_o, mx_e, mx_o = _pair_attention(q(sl), k_pair, v_pair, bias_e, bias_o)
        put_num(j, jnp.where(lo, r_e, r_o))
        l_e, l_o = _stat_lane(2 * j), _stat_lane(2 * j + 1)
        st_c = jnp.where(lane == l_e, mx_e, jnp.where(lane == l_e + STAT_DEN_SHIFT, r_e, st_c))
        st_c = jnp.where(lane == l_o, mx_o, jnp.where(lane == l_o + STAT_DEN_SHIFT, r_o, st_c))
    return st_c


def _mem_kv(mem_ref, mnw_ref, wkv_ref, kv_ref):
    m = mem_ref[0]
    ms = jnp.mean(m * m, axis=-1, keepdims=True)
    mn = (m * lax.rsqrt(ms + EPS) * mnw_ref[...]).astype(BF16)
    kv_ref[...] = jnp.dot(mn, wkv_ref[...].astype(BF16), preferred_element_type=F32).astype(BF16)


def _mem_unit(q, kv_ref, put_num):
    lane = lax.broadcasted_iota(jnp.int32, (DIL_BLOCK, LANES), 1)
    lo = lane < HEAD_DIM
    den_c = jnp.ones((DIL_BLOCK, LANES), F32)
    for j in range(N_HEADS_MEM // 2):
        sl = slice(j * LANES, (j + 1) * LANES)
        r_e, r_o, _, _ = _pair_attention(q(sl), kv_ref[:, sl], kv_ref[:, D_MEM + j * LANES:D_MEM + (j + 1) * LANES],
                                         None, None)
        put_num(j, jnp.where(lo, r_e, r_o))
        den_c = jnp.where(lane == _stat_lane(2 * j), r_e, jnp.where(lane == _stat_lane(2 * j + 1), r_o, den_c))
    return den_c


def _dil1_kernel(q_ref, kp_ref, kc_ref, vp_ref, vc_ref, rb_ref, sel_ref, mrow_ref, mq_ref, mem_ref, mnw_ref, wkv_ref,
                 o_ref, st_ref, om_ref, dm_ref, bias_ref, kv_ref):
    @pl.when((pl.program_id(0) == 0) & (pl.program_id(1) == 0))
    def _():
        _dil_init_bias(rb_ref, sel_ref, mrow_ref, bias_ref)

    @pl.when(pl.program_id(1) == 0)
    def _():
        _mem_kv(mem_ref, mnw_ref, wkv_ref, kv_ref)

    for u in range(DIL_UNITS[1]):
        rows = slice(u * DIL_BLOCK, (u + 1) * DIL_BLOCK)
        prev = slice((u - 1) * DIL_BLOCK, u * DIL_BLOCK)
        if u == 0:
            first = (pl.program_id(1) == 0).astype(jnp.int32)
            kp, vp = (lambda sl: kp_ref[:, sl]), (lambda sl: vp_ref[:, sl])
        else:
            first = 0
            kp, vp = (lambda sl, p=prev: kc_ref[p, sl]), (lambda sl, p=prev: vc_ref[p, sl])

        def put_num(j, num, rows=rows):
            o_ref[rows, j * LANES:(j + 1) * LANES] = num.astype(o_ref.dtype)

        st_ref[rows, :] = _dil_unit(lambda sl, r=rows: q_ref[r, sl], kp, lambda sl, r=rows: kc_ref[r, sl],
                                    vp, lambda sl, r=rows: vc_ref[r, sl], first, bias_ref, put_num)

        def put_mem(j, num, rows=rows):
            om_ref[rows, j * LANES:(j + 1) * LANES] = num.astype(om_ref.dtype)

        dm_ref[rows, :] = _mem_unit(lambda sl, r=rows: mq_ref[r, sl], kv_ref, put_mem)


DIL_LBLOCKS = {4: 2, 16: 2}


def _dil_kernel(*refs, dil, whole_seq):
    if whole_seq:
        q_ref, kc_ref, vc_ref, *rest = refs
        kp_ref = vp_ref = None
    else:
        q_ref, kp_ref, kc_ref, vp_ref, vc_ref, *rest = refs
    rb_ref, sel_ref, mrow_ref, o_ref, st_ref, bias_ref, acc_ref = rest
    b, lb, g = pl.program_id(0), pl.program_id(1), pl.program_id(2)
    nr, nl = DIL_UNITS[dil], DIL_LBLOCKS[dil]

    @pl.when((b == 0) & (lb == 0) & (g == 0))
    def _():
        _dil_init_bias(rb_ref, sel_ref, mrow_ref, bias_ref)

    for ll in range(nl):
        sub = slice(ll * DIL_BLOCK, (ll + 1) * DIL_BLOCK)
        prev = slice((ll - 1) * DIL_BLOCK, ll * DIL_BLOCK)
        first = (lb == 0).astype(jnp.int32) if ll == 0 else 0
        for u in range(nr):
            rows = pl.ds(ll * DIL_BLOCK * dil + g * nr + u, DIL_BLOCK, stride=dil)
            if ll == 0 and whole_seq:
                kp = vp = None
            elif ll == 0:
                kp, vp = (lambda sl, u=u: kp_ref[u, :, sl]), (lambda sl, u=u: vp_ref[u, :, sl])
            else:
                kp = lambda sl, u=u, p=prev: kc_ref[u, p, sl]
                vp = lambda sl, u=u, p=prev: vc_ref[u, p, sl]

            def put_num(j, num, rows=rows):
                acc_ref[j, rows, :] = num

            st_ref[rows, :] = _dil_unit(lambda sl, u=u, r=sub: q_ref[u, r, sl], kp,
                                        lambda sl, u=u, r=sub: kc_ref[u, r, sl], vp,
                                        lambda sl, u=u, r=sub: vc_ref[u, r, sl], first, bias_ref, put_num)

    @pl.when(g == dil // nr - 1)
    def _():
        for j in range(N_HEADS_DIL // 2):
            o_ref[:, j * LANES:(j + 1) * LANES] = acc_ref[j].astype(o_ref.dtype)


def _t5_bucket_np(dist):
    max_exact = REL_BUCKETS // 2
    n = np.maximum(dist, 1).astype(np.float32)
    large = max_exact + (np.log(n / np.float32(max_exact)) / np.float32(math.log(REL_MAX_DIST / max_exact))
                         * np.float32(REL_BUCKETS - max_exact)).astype(np.int32)
    large = np.minimum(large, REL_BUCKETS - 1)
    return np.where(dist < max_exact, dist, large)


def _bias_selector(window, dil):
    back = window // dil
    x = np.arange(2 * DIL_BLOCK)
    rel = DIL_BLOCK - x
    valid = (rel >= 0) & (rel <= back)
    bucket = _t5_bucket_np(np.clip(rel, 0, back) * dil)
    sel = np.zeros((LANES, 2 * DIL_BLOCK), np.float32)
    sel[bucket[valid], x[valid]] = 1.0
    mrow = np.where(valid, 0.0, NEG_INF).astype(np.float32)[None]
    return jnp.asarray(sel, BF16), jnp.asarray(mrow)


def _dilated(src, rb_t, batch, seq, window, dil, mem_args=None):
    sub_len = seq // dil
    nb = sub_len // DIL_BLOCK
    sel, mrow = _bias_selector(window, dil)
    nu = DIL_UNITS[dil]
    consts = (rb_t, sel, mrow)
    const = lambda a: pl.BlockSpec(a.shape, lambda *_: (0, 0))
    stat_shape = jax.ShapeDtypeStruct((batch * seq, LANES), F32)
    out_shape = [jax.ShapeDtypeStruct((batch * seq, D_DIL), BF16), stat_shape]
    bias_scratch = pltpu.VMEM((2, N_HEADS_DIL, DIL_BLOCK, 2 * DIL_BLOCK), F32)

    if dil == 1:
        rest, mem, mem_norm_w, w_kv = mem_args
        rows = nu * DIL_BLOCK
        steps = nb // nu
        cur = lambda col: pl.BlockSpec((None, None, rows, D_DIL), lambda b, l: (b, 0, l, col))
        prev = lambda col: pl.BlockSpec((None, None, DIL_BLOCK, D_DIL),
                                        lambda b, l: (b, 0, jnp.maximum(nu * l - 1, 0), col))
        out = lambda width, col=0: pl.BlockSpec((rows, width), lambda b, l: (b * steps + l, col // width))
        mem_specs = [out(D_MEM, COL_MQ), pl.BlockSpec((1, MEM_LEN, D_MODEL), lambda b, l: (b, 0, 0)),
                     const(mem_norm_w), const(w_kv)]
        num, stats, mem_num, mem_den = pl.pallas_call(
            _dil1_kernel,
            grid=(batch, steps),
            in_specs=[cur(0), prev(1), cur(1), prev(2), cur(2)] + [const(a) for a in consts] + mem_specs,
            out_specs=[out(D_DIL), out(LANES), out(D_MEM), out(LANES)],
            out_shape=out_shape + [jax.ShapeDtypeStruct((batch * seq, D_MEM), BF16), stat_shape],
            scratch_shapes=[bias_scratch, pltpu.VMEM((MEM_LEN, 2 * D_MEM), BF16)],
            compiler_params=_cparams(("arbitrary", "arbitrary")),
            name="dilated_d1",
        )(src, src, src, src, src, *consts, rest, mem, mem_norm_w, w_kv)
        return (num, stats), (mem_num, mem_den)

    nl = DIL_LBLOCKS[dil]
    steps = nb // nl
    rows = nl * DIL_BLOCK * dil
    cur = lambda col: pl.BlockSpec((None, nu, nl * DIL_BLOCK, D_DIL), lambda b, l, g: (b, g, l, col))
    prev = lambda col: pl.BlockSpec((None, nu, DIL_BLOCK, D_DIL),
                                    lambda b, l, g: (b, g, jnp.maximum(nl * l - 1, 0), col))
    out = lambda width: pl.BlockSpec((rows, width), lambda b, l, g: (b * steps + l, 0))
    whole_seq = steps == 1
    qkv_specs = [cur(0), cur(1), cur(2)] if whole_seq else [cur(0), prev(1), cur(1), prev(2), cur(2)]
    return pl.pallas_call(
        functools.partial(_dil_kernel, dil=dil, whole_seq=whole_seq),
        grid=(batch, steps, dil // nu),
        in_specs=qkv_specs + [const(a) for a in consts],
        out_specs=[out(D_DIL), out(LANES)],
        out_shape=out_shape,
        scratch_shapes=[bias_scratch, pltpu.VMEM((N_HEADS_DIL // 2, rows, LANES), F32)],
        compiler_params=_cparams(("arbitrary", "arbitrary", "arbitrary")),
        name=f"dilated_d{dil}",
    )(*([src] * len(qkv_specs)), *consts)


GLA_TM = 256
GLA_STEP_TILES = 2


GLA_NC = GLA_TM // GLA_CHUNK


def _gla_tile(bb, rows, w2, q_ref, k_ref, v_ref, g_ref, bg_ref, nw_ref, tri_ref, o_ref, st_ref):
    g = g_ref[bb, rows, :LANES]
    z = jnp.dot(g, w2, preferred_element_type=F32) + bg_ref[...]
    la = (jnp.minimum(z, 0.0) * (LOG2E / GLA_TAU)
          - jnp.log2(1.0 + jnp.exp2(jnp.abs(z) * -LOG2E)) * (1.0 / GLA_TAU))
    la_h = la.astype(BF16)
    la_l = (la - la_h.astype(F32)).astype(BF16)

    tri = tri_ref[...]
    b = jnp.dot(tri, la_h, preferred_element_type=F32) + jnp.dot(tri, la_l, preferred_element_type=F32)
    b_last = jnp.concatenate(
        [jnp.broadcast_to(b[(ci + 1) * GLA_CHUNK - 1:(ci + 1) * GLA_CHUNK, :], (GLA_CHUNK, D_GLA_K))
         for ci in range(GLA_NC)], axis=0)

    q = q_ref[bb, rows].astype(F32)
    k = k_ref[bb, rows].astype(F32)
    q_t = (q * jnp.exp2(b)).astype(BF16)
    k_t = (k * jnp.exp2(-b)).astype(BF16)
    k_s = k * jnp.exp2(b_last - b)

    causal = tri > 0
    t_chunk = lax.broadcasted_iota(jnp.int32, (GLA_DK, GLA_TM), 1) // GLA_CHUNK
    nw = nw_ref[...]
    for h in range(N_HEADS_GLA):
        cs = slice(h * GLA_DK, (h + 1) * GLA_DK)
        vs = slice(h * GLA_DV, (h + 1) * GLA_DV)
        v_h = v_ref[bb, rows, vs]
        a = lax.dot_general(q_t[:, cs], k_t[:, cs], (((1,), (1,)), ((), ())), preferred_element_type=F32)
        o = jnp.dot(jnp.where(causal, a.astype(BF16), jnp.zeros_like(tri)), v_h, preferred_element_type=F32)
        k_s_t = k_s[:, cs].T.astype(BF16)
        zero = jnp.zeros_like(k_s_t)
        kv_all = jnp.dot(jnp.concatenate([jnp.where(t_chunk == ci, k_s_t, zero) for ci in range(GLA_NC)], axis=0),
                         v_h, preferred_element_type=F32)
        decay_t = jnp.exp2(b_last[:, cs].T)
        st = st_ref[bb, h]
        outs = []
        for ci in range(GLA_NC):
            rs = slice(ci * GLA_CHUNK, (ci + 1) * GLA_CHUNK)
            o_c = o[rs] + jnp.dot(q_t[rs, cs], st.astype(BF16), preferred_element_type=F32)
            decay = jnp.broadcast_to(decay_t[:, ci * GLA_CHUNK:ci * GLA_CHUNK + 1], (GLA_DK, GLA_DV))
            st = decay * st + kv_all[ci * GLA_DK:(ci + 1) * GLA_DK]
            ms = jnp.mean(o_c * o_c, axis=-1, keepdims=True)
            outs.append((o_c * lax.rsqrt(ms + EPS) * nw).astype(o_ref.dtype))
        st_ref[bb, h] = st
        o_ref[bb, rows, vs] = jnp.concatenate(outs, axis=0)


def _gla_kernel(q_ref, k_ref, v_ref, g_ref, w2_ref, bg_ref, nw_ref, tri_ref, wout_ref,
                o_ref, wout_bf_ref, st_ref):

    @pl.when(pl.program_id(0) == 0)
    def _():
        st_ref[...] = jnp.zeros_like(st_ref)

    wout_bf_ref[...] = wout_ref[...].astype(wout_bf_ref.dtype)

    w2 = jnp.concatenate([w2_ref[...].astype(BF16), jnp.zeros((LANES - GLA_GATE_RANK, D_GLA_K), BF16)], axis=0)
    for tt in range(GLA_STEP_TILES):
        rows = slice(tt * GLA_TM, (tt + 1) * GLA_TM)
        for bb in range(st_ref.shape[0]):
            _gla_tile(bb, rows, w2, q_ref, k_ref, v_ref, g_ref, bg_ref, nw_ref, tri_ref, o_ref, st_ref)


def _gla(proj, w2, b_gate, norm_w, w_out, batch, seq):
    step_rows = GLA_TM * GLA_STEP_TILES
    steps = seq // step_rows
    wout_rows = w_out.shape[0] // steps
    pos = np.arange(GLA_TM)
    tri = jnp.asarray((pos[:, None] // GLA_CHUNK == pos[None, :] // GLA_CHUNK) & (pos[None, :] <= pos[:, None]),
                      BF16)
    proj3 = proj.reshape(batch, seq, REST_W)
    rows = lambda width, col: pl.BlockSpec((batch, step_rows, width), lambda i: (0, i, col // width))
    const = lambda shape: pl.BlockSpec(shape, lambda i: (0, 0))
    wout_spec = pl.BlockSpec((wout_rows, w_out.shape[1]), lambda i: (i, 0))
    out, w_out_bf = pl.pallas_call(
        _gla_kernel,
        grid=(steps,),
        in_specs=[
            rows(D_GLA_K, COL_GQ), rows(D_GLA_K, COL_GK), rows(D_GLA_V, COL_GV), rows(GLR_PAD, COL_GLR),
            const((GLA_GATE_RANK, D_GLA_K)), const((1, D_GLA_K)), const((1, GLA_DV)),
            const((GLA_TM, GLA_TM)), wout_spec,
        ],
        out_specs=[rows(D_GLA_V, 0), wout_spec],
        out_shape=[jax.ShapeDtypeStruct((batch, seq, D_GLA_V), BF16), jax.ShapeDtypeStruct(w_out.shape, BF16)],
        scratch_shapes=[pltpu.VMEM((batch, N_HEADS_GLA, GLA_DK, GLA_DV), F32)],
        compiler_params=_cparams(("arbitrary",)),
        name="gla",
    )(proj3, proj3, proj3, proj3, w2, b_gate, norm_w, tri, w_out)
    return out.reshape(batch * seq, D_GLA_V), w_out_bf


OUT_TM = 512


def _silu(g):
    h = g * 0.5
    return h * jnp.tanh(h) + h


OUT_GW = 256


def _out_kernel(n1_ref, n2_ref, n3_ref, s1_ref, s2_ref, s3_ref, og_ref, nm_ref, dm_ref, ga_ref, gg_ref, gm_ref,
                sm_ref, ex_ref, exm_ref, w_ref, x_ref, nw_ref, out_ref, mix_ref):
    keep = sm_ref[0:1, :]
    fill = sm_ref[1:2, :]
    stats = [s_ref[...] for s_ref in (s1_ref, s2_ref, s3_ref)]
    m1, m2, m3 = [st * keep for st in stats]
    d1, d2, d3 = [pltpu.roll(st, LANES - STAT_DEN_SHIFT, 1) * keep + fill for st in stats]
    mx = jnp.maximum(jnp.maximum(m1, m2), m3)
    t1, t2, t3 = jnp.exp2(m1 - mx), jnp.exp2(m2 - mx), jnp.exp2(m3 - mx)
    inv = 1.0 / (t1 * d1 + t2 * d2 + t3 * d3)
    wts = [(t * inv).astype(BF16) for t in (t1, t2, t3)]
    inv_m = (1.0 / dm_ref[...]).astype(BF16)

    def gate_chunk(lo):
        cols = slice(lo, lo + OUT_GW)
        if lo < D_DIL:
            a = jnp.zeros((OUT_TM, OUT_GW), F32)
            for wt, n_ref in zip(wts, (n1_ref, n2_ref, n3_ref)):
                a = a + jnp.dot(wt, ex_ref[:, cols], preferred_element_type=F32) * n_ref[:, cols].astype(F32)
            g = ga_ref[:, cols]
        elif lo < D_DIL + D_GLA_V:
            lc = slice(lo - D_DIL, lo - D_DIL + OUT_GW)
            a, g = og_ref[:, lc], gg_ref[:, lc]
        else:
            a = jnp.dot(inv_m, exm_ref[...], preferred_element_type=F32) * nm_ref[...].astype(F32)
            g = gm_ref[...]
        mix_ref[:, cols] = a.astype(BF16) * _silu(g)

    for lo in range(0, D_MIX, OUT_GW):
        gate_chunk(lo)
    y = jnp.dot(mix_ref[...], w_ref[...], preferred_element_type=F32)
    ms = jnp.mean(y * y, axis=-1, keepdims=True)
    out_ref[...] = x_ref[...] + y * lax.rsqrt(ms + EPS) * nw_ref[...]


def _out_proj(pats, o_gla, mem_pair, proj, stat_mask, expand, expand_mem, w_out, x2d, norm_w):
    m = x2d.shape[0]
    rows = lambda width, col=0: pl.BlockSpec((OUT_TM, width), lambda s: (s, col // width))
    const = lambda shape: pl.BlockSpec(shape, lambda s: (0, 0), pipeline_mode=pl.Buffered(1))
    nums, stats = zip(*pats)
    return pl.pallas_call(
        _out_kernel,
        grid=(m // OUT_TM,),
        in_specs=[
            *([rows(D_DIL)] * 3), *([rows(LANES)] * 3),
            rows(D_GLA_V), rows(D_MEM), rows(LANES),
            rows(D_DIL, COL_GATE_A), rows(D_GLA_V, COL_GATE_G), rows(D_MEM, COL_GATE_M),
            const((F32_SUBLANES, LANES)), const((LANES, D_DIL)), const((LANES, D_MEM)), const((D_MIX, D_MODEL)),
            rows(D_MODEL), const((1, D_MODEL)),
        ],
        out_specs=rows(D_MODEL),
        out_shape=jax.ShapeDtypeStruct((m, D_MODEL), x2d.dtype),
        scratch_shapes=[pltpu.VMEM((OUT_TM, D_MIX), BF16)],
        compiler_params=_cparams(("parallel",)),
        name="out_proj",
    )(*nums, *stats, o_gla, *mem_pair, proj, proj, proj, stat_mask, expand, expand_mem, w_out, x2d, norm_w)


D_IN_PROJ = 3 * D_DIL + 2 * D_GLA_K + D_GLA_V + GLA_GATE_RANK + D_MEM + D_MIX
SRC_GLR = 3 * D_DIL + 2 * D_GLA_K + D_GLA_V
SRC_MQ = SRC_GLR + GLA_GATE_RANK
SRC_GATE = SRC_MQ + D_MEM
SCALE_ATTN_Q = HEAD_DIM ** -0.5 * LOG2E
SCALE_GLA_Q = GLA_DK ** -0.5
QKV_SEGMENTS = (
    (0, D_DIL, 0, SCALE_ATTN_Q),
    (D_DIL, 2 * D_DIL, D_DIL, 1.0),
)
REST_SEGMENTS = (
    (QKV_W, D_GLA_K, COL_GQ, SCALE_GLA_Q),
    (QKV_W + D_GLA_K, D_GLA_K + D_GLA_V, COL_GK, 1.0),
    (SRC_GATE + D_DIL, D_GLA_V, COL_GATE_G, 1.0),
    (SRC_MQ, D_MEM, COL_MQ, SCALE_ATTN_Q),
    (SRC_GATE + D_DIL + D_GLA_V, D_MEM, COL_GATE_M, 1.0),
    (SRC_GATE, D_DIL, COL_GATE_A, 1.0),
)
WPREP_TK = 256


def _wprep_kernel(wt_ref, qkv_ref, rest_ref):
    for o_ref, segments in ((qkv_ref, QKV_SEGMENTS), (rest_ref, REST_SEGMENTS)):
        for src, width, dst, scale in segments:
            for lo in range(0, width, LANES):
                w = wt_ref[src + lo:src + lo + LANES, :].T
                if scale != 1.0:
                    w = w * scale
                o_ref[:, dst + lo:dst + lo + LANES] = w.astype(o_ref.dtype)
    glr = wt_ref[SRC_GLR:SRC_GLR + LANES, :].T
    keep = lax.broadcasted_iota(jnp.int32, glr.shape, 1) < GLA_GATE_RANK
    rest_ref[:, COL_GLR:COL_GLR + LANES] = jnp.where(keep, glr, 0.0).astype(rest_ref.dtype)
    rest_ref[:, COL_GLR + LANES:COL_GLR + GLR_PAD] = jnp.zeros((WPREP_TK, GLR_PAD - LANES), rest_ref.dtype)


def _regroup_in_weights(w_in):
    return pl.pallas_call(
        _wprep_kernel,
        grid=(D_MODEL // WPREP_TK,),
        in_specs=[pl.BlockSpec((D_IN_PROJ, WPREP_TK), lambda i: (0, i))],
        out_specs=[pl.BlockSpec((WPREP_TK, QKV_W), lambda i: (i, 0)),
                   pl.BlockSpec((WPREP_TK, REST_W), lambda i: (i, 0))],
        out_shape=[jax.ShapeDtypeStruct((D_MODEL, QKV_W), BF16), jax.ShapeDtypeStruct((D_MODEL, REST_W), BF16)],
        compiler_params=_cparams(("parallel",)),
        name="weight_regroup",
    )(w_in.T)


def kernel(x, mem, norm_pre_w, w_in, rel_bias, w_gla_gate2, b_gla_gate, gla_norm_w, mem_norm_w, w_mem_kv,
           w_out, norm_post_w):
    batch, seq, _ = x.shape
    depth = w_in.shape[0]
    expand_np = np.zeros((LANES, D_DIL), np.float32)
    for h in range(N_HEADS_DIL):
        expand_np[_stat_lane(h), h * HEAD_DIM:(h + 1) * HEAD_DIM] = 1.0
    expand = jnp.asarray(expand_np, BF16)
    expand_mem = jnp.asarray(expand_np[:, :D_MEM], BF16)
    stat_mask_np = np.zeros((F32_SUBLANES, LANES), np.float32)
    stat_mask_np[0, [_stat_lane(h) for h in range(N_HEADS_DIL)]] = 1.0
    stat_mask_np[1] = 1.0 - stat_mask_np[0]
    stat_mask = jnp.asarray(stat_mask_np)
    rb_t = jnp.zeros((2 * F32_SUBLANES, LANES), F32).at[:N_HEADS_DIL, :REL_BUCKETS].set(rel_bias.astype(F32).T)
    for l in range(depth):
        x2d = x.reshape(batch * seq, D_MODEL)
        w_qkv, w_rest = _regroup_in_weights(w_in[l])
        h, qkv1, qkv4, qkv16 = _qkv_proj(x2d, norm_pre_w[l][None], w_qkv, batch, seq)
        rest = _rest_proj(h, w_rest)

        srcs = (qkv1.reshape(batch, 1, seq, QKV_W), qkv4, qkv16)
        mem_args = (rest, mem, mem_norm_w[l][None], w_mem_kv[l])
        pats = [_dilated(src, rb_t, batch, seq, window, dil, mem_args if dil == 1 else None)
                for (window, dil), src in zip(DIL_PATTERNS, srcs)]
        pats[0], mem_pair = pats[0]

        o_gla, w_out_bf = _gla(rest, w_gla_gate2[l], b_gla_gate[l][None], gla_norm_w[l][None], w_out[l],
                               batch, seq)

        out = _out_proj(pats, o_gla, mem_pair, rest, stat_mask, expand, expand_mem, w_out_bf, x2d,
                        norm_post_w[l][None])
        x = out.reshape(batch, seq, D_MODEL)
    return x
```

```python
import functools
import math

import numpy as np
import jax
import jax.numpy as jnp
from jax import lax
from jax.experimental import pallas as pl
from jax.experimental.pallas import tpu as pltpu

F32 = jnp.float32
BF16 = jnp.bfloat16

D_MODEL = 2048
HEAD_DIM = 64
N_HEADS_DIL = 12
DIL_PATTERNS = ((128, 1), (512, 4), (2048, 16))
DIL_BLOCK = 128
N_HEADS_GLA = 4
GLA_DK = 128
GLA_DV = 256
GLA_GATE_RANK = 16
GLA_TAU = 16.0
GLA_CHUNK = 64
N_HEADS_MEM = 4
MEM_LEN = 256
REL_BUCKETS = 32
REL_MAX_DIST = 2048
EPS = 1e-6
NEG_INF = -1e30

D_DIL = N_HEADS_DIL * HEAD_DIM
D_GLA_K = N_HEADS_GLA * GLA_DK
D_GLA_V = N_HEADS_GLA * GLA_DV
D_MEM = N_HEADS_MEM * HEAD_DIM
D_MIX = D_DIL + D_GLA_V + D_MEM

LANES = 128
F32_SUBLANES = 8
GLR_PAD = 256

QKV_W = 3 * D_DIL
COL_GQ = 0
COL_GK = COL_GQ + D_GLA_K
COL_GV = COL_GK + D_GLA_K
COL_GATE_G = COL_GV + D_GLA_V
COL_MQ = COL_GATE_G + D_GLA_V
COL_GATE_M = COL_MQ + D_MEM
COL_GLR = COL_GATE_M + D_MEM
COL_GATE_A = COL_GLR + GLR_PAD
REST_W = COL_GATE_A + D_DIL

VMEM_LIMIT = 56 * 1024 * 1024


def _cparams(sem):
    return pltpu.CompilerParams(dimension_semantics=sem, vmem_limit_bytes=VMEM_LIMIT)


def _split_hi_lo(w):
    hi = w.astype(BF16)
    return hi, (w - hi.astype(F32)).astype(BF16)


QKV_TM = 512
N_SLABS = D_DIL // LANES
REGROUP_DILS = (4, 16)


def _qkv_proj_kernel(x_ref, nw_ref, w_ref, h_ref, o_ref, x4_ref, x16_ref, slab_ref):
    x = x_ref[...]
    ms = jnp.mean(x * x, axis=-1, keepdims=True)
    h_ref[...] = (x * lax.rsqrt(ms + EPS) * nw_ref[...]).astype(BF16)
    for t in range(QKV_W // D_DIL):
        cols = slice(t * D_DIL, (t + 1) * D_DIL)
        res = jnp.dot(h_ref[...], w_ref[:, cols], preferred_element_type=F32)
        o_ref[:, cols] = res.astype(o_ref.dtype)
        for s in range(N_SLABS):
            slab_ref[t, s] = res[:, s * LANES:(s + 1) * LANES]
        for out_ref, dil in zip((x4_ref, x16_ref), REGROUP_DILS):
            for r in range(dil):
                for s in range(N_SLABS):
                    lo = t * D_DIL + s * LANES
                    out_ref[0, r, :, lo:lo + LANES] = (
                        slab_ref[t, s, pl.ds(r, QKV_TM // dil, stride=dil), :].astype(out_ref.dtype))


def _qkv_proj(x2d, norm_w, w_qkv, batch, seq):
    m = x2d.shape[0]
    tiles_per_seq = seq // QKV_TM
    const = lambda shape: pl.BlockSpec(shape, lambda i: (0, 0), pipeline_mode=pl.Buffered(1))
    grouped = lambda dil: pl.BlockSpec((1, dil, QKV_TM // dil, QKV_W),
                                       lambda i: (i // tiles_per_seq, 0, i % tiles_per_seq, 0))
    return pl.pallas_call(
        _qkv_proj_kernel,
        grid=(m // QKV_TM,),
        in_specs=[
            pl.BlockSpec((QKV_TM, D_MODEL), lambda i: (i, 0)),
            const((1, D_MODEL)), const((D_MODEL, QKV_W)),
        ],
        out_specs=[pl.BlockSpec((QKV_TM, D_MODEL), lambda i: (i, 0)),
                   pl.BlockSpec((QKV_TM, QKV_W), lambda i: (i, 0))] + [grouped(d) for d in REGROUP_DILS],
        out_shape=[jax.ShapeDtypeStruct((m, D_MODEL), BF16), jax.ShapeDtypeStruct((m, QKV_W), BF16)]
        + [jax.ShapeDtypeStruct((batch, d, seq // d, QKV_W), BF16) for d in REGROUP_DILS],
        scratch_shapes=[pltpu.VMEM((QKV_W // D_DIL, N_SLABS, QKV_TM, LANES), F32)],
        compiler_params=_cparams(("parallel",)),
        name="qkv_proj",
    )(x2d, norm_w, w_qkv)


REST_TM = 512


def _rest_proj_kernel(h_ref, w_ref, o_ref):
    o_ref[...] = jnp.dot(h_ref[...], w_ref[...], preferred_element_type=F32).astype(o_ref.dtype)


def _rest_proj(h, w_rest):
    m = h.shape[0]
    const = lambda shape: pl.BlockSpec(shape, lambda i: (0, 0), pipeline_mode=pl.Buffered(1))
    return pl.pallas_call(
        _rest_proj_kernel,
        grid=(m // REST_TM,),
        in_specs=[pl.BlockSpec((REST_TM, D_MODEL), lambda i: (i, 0)), const((D_MODEL, REST_W))],
        out_specs=pl.BlockSpec((REST_TM, REST_W), lambda i: (i, 0)),
        out_shape=jax.ShapeDtypeStruct((m, REST_W), BF16),
        compiler_params=_cparams(("parallel",)),
        name="rest_proj",
    )(h, w_rest)


LOG2E = math.log2(math.e)


def _pair_attention(q_pair, k_pair, v_pair, bias_e, bias_o):
    rows = q_pair.shape[0]
    lo_q = lax.broadcasted_iota(jnp.int32, q_pair.shape, 1) < HEAD_DIM
    zero = jnp.zeros_like(q_pair)
    q_both = jnp.concatenate([jnp.where(lo_q, q_pair, zero), jnp.where(lo_q, zero, q_pair)], axis=0)
    s = lax.dot_general(q_both, k_pair, (((1,), (1,)), ((), ())), preferred_element_type=F32)
    s_e, s_o = s[:rows], s[rows:]
    if bias_e is not None:
        s_e = s_e + bias_e
        s_o = s_o + bias_o
    mx_e = jnp.max(s_e, axis=-1, keepdims=True)
    mx_o = jnp.max(s_o, axis=-1, keepdims=True)
    p_e = jnp.exp2(s_e - mx_e).astype(BF16)
    p_o = jnp.exp2(s_o - mx_o).astype(BF16)
    lo_v = lax.broadcasted_iota(jnp.int32, v_pair.shape, 1) < HEAD_DIM
    one = jnp.ones_like(v_pair)
    r_e = jnp.dot(p_e, jnp.where(lo_v, v_pair, one), preferred_element_type=F32)
    r_o = jnp.dot(p_o, jnp.where(lo_v, one, v_pair), preferred_element_type=F32)
    return r_e, r_o, mx_e, mx_o


def _stat_lane(head):
    return head + HEAD_DIM if head % 2 == 0 else head


STAT_DEN_SHIFT = 16


DIL_UNITS = {1: 8, 4: 4, 16: 8}


def _dil_init_bias(rb_ref, sel_ref, mrow_ref, bias_ref):
    rb_hi, rb_lo = _split_hi_lo(rb_ref[...])
    f = (jnp.dot(rb_hi, sel_ref[...], preferred_element_type=F32)
         + jnp.dot(rb_lo, sel_ref[...], preferred_element_type=F32)) * LOG2E + mrow_ref[...]
    col = lax.broadcasted_iota(jnp.int32, (DIL_BLOCK, 2 * DIL_BLOCK), 1)
    for h in range(N_HEADS_DIL):
        row = jnp.broadcast_to(f[h:h + 1, :], (DIL_BLOCK, 2 * DIL_BLOCK))
        tab = pltpu.roll(row, 0, 1, stride=1, stride_axis=0)
        bias_ref[0, h] = tab
        bias_ref[1, h] = jnp.where(col >= DIL_BLOCK, tab, NEG_INF)


def _dil_unit(q, kp, kc, vp, vc, first, bias_ref, put_num):
    lane = lax.broadcasted_iota(jnp.int32, (DIL_BLOCK, LANES), 1)
    lo = lane < HEAD_DIM
    st_c = jnp.zeros((DIL_BLOCK, LANES), F32)
    for j in range(N_HEADS_DIL // 2):
        sl = slice(j * LANES, (j + 1) * LANES)
        if kp is None:
            k_pair, v_pair = kc(sl), vc(sl)
            bias_e, bias_o = bias_ref[0, 2 * j, :, DIL_BLOCK:], bias_ref[0, 2 * j + 1, :, DIL_BLOCK:]
        else:
            k_pair = jnp.concatenate([kp(sl), kc(sl)], axis=0)
            v_pair = jnp.concatenate([vp(sl), vc(sl)], axis=0)
            bias_e, bias_o = bias_ref[first, 2 * j], bias_ref[first, 2 * j + 1]
        r_e, r_o, mx_e, mx_o = _pair_attention(q(sl), k_pair, v_pair, bias_e, bias_o)
        put_num(j, jnp.where(lo, r_e, r_o))
        l_e, l_o = _stat_lane(2 * j), _stat_lane(2 * j + 1)
        st_c = jnp.where(lane == l_e, mx_e, jnp.where(lane == l_e + STAT_DEN_SHIFT, r_e, st_c))
        st_c = jnp.where(lane == l_o, mx_o, jnp.where(lane == l_o + STAT_DEN_SHIFT, r_o, st_c))
    return st_c


def _mem_kv(mem_ref, mnw_ref, wkv_ref, kv_ref):
    m = mem_ref[0]
    ms = jnp.mean(m * m, axis=-1, keepdims=True)
    mn = (m * lax.rsqrt(ms + EPS) * mnw_ref[...]).astype(BF16)
    kv_ref[...] = jnp.dot(mn, wkv_ref[...].astype(BF16), preferred_element_type=F32).astype(BF16)


def _mem_unit(q, kv_ref, put_num):
    lane = lax.broadcasted_iota(jnp.int32, (DIL_BLOCK, LANES), 1)
    lo = lane < HEAD_DIM
    den_c = jnp.ones((DIL_BLOCK, LANES), F32)
    for j in range(N_HEADS_MEM // 2):
        sl = slice(j * LANES, (j + 1) * LANES)
        r_e, r_o, _, _ = _pair_attention(q(sl), kv_ref[:, sl], kv_ref[:, D_MEM + j * LANES:D_MEM + (j + 1) * LANES],
                                         None, None)
        put_num(j, jnp.where(lo, r_e, r_o))
        den_c = jnp.where(lane == _stat_lane(2 * j), r_e, jnp.where(lane == _stat_lane(2 * j + 1), r_o, den_c))
    return den_c


def _dil1_kernel(q_ref, kp_ref, kc_ref, vp_ref, vc_ref, rb_ref, sel_ref, mrow_ref, mq_ref, mem_ref, mnw_ref, wkv_ref,
                 o_ref, st_ref, om_ref, dm_ref, bias_ref, kv_ref):
    @pl.when((pl.program_id(0) == 0) & (pl.program_id(1) == 0))
    def _():
        _dil_init_bias(rb_ref, sel_ref, mrow_ref, bias_ref)

    @pl.when(pl.program_id(1) == 0)
    def _():
        _mem_kv(mem_ref, mnw_ref, wkv_ref, kv_ref)

    for u in range(DIL_UNITS[1]):
        rows = slice(u * DIL_BLOCK, (u + 1) * DIL_BLOCK)
        prev = slice((u - 1) * DIL_BLOCK, u * DIL_BLOCK)
        if u == 0:
            first = (pl.program_id(1) == 0).astype(jnp.int32)
            kp, vp = (lambda sl: kp_ref[:, sl]), (lambda sl: vp_ref[:, sl])
        else:
            first = 0
            kp, vp = (lambda sl, p=prev: kc_ref[p, sl]), (lambda sl, p=prev: vc_ref[p, sl])

        def put_num(j, num, rows=rows):
            o_ref[rows, j * LANES:(j + 1) * LANES] = num.astype(o_ref.dtype)

        st_ref[rows, :] = _dil_unit(lambda sl, r=rows: q_ref[r, sl], kp, lambda sl, r=rows: kc_ref[r, sl],
                                    vp, lambda sl, r=rows: vc_ref[r, sl], first, bias_ref, put_num)

        def put_mem(j, num, rows=rows):
            om_ref[rows, j * LANES:(j + 1) * LANES] = num.astype(om_ref.dtype)

        dm_ref[rows, :] = _mem_unit(lambda sl, r=rows: mq_ref[r, sl], kv_ref, put_mem)


DIL_LBLOCKS = {4: 2, 16: 2}


def _dil_kernel(*refs, dil, whole_seq):
    if whole_seq:
        q_ref, kc_ref, vc_ref, *rest = refs
        kp_ref = vp_ref = None
    else:
        q_ref, kp_ref, kc_ref, vp_ref, vc_ref, *rest = refs
    rb_ref, sel_ref, mrow_ref, o_ref, st_ref, bias_ref, acc_ref = rest
    b, lb, g = pl.program_id(0), pl.program_id(1), pl.program_id(2)
    nr, nl = DIL_UNITS[dil], DIL_LBLOCKS[dil]
    gather = whole_seq and dil == 2 * F32_SUBLANES

    @pl.when((b == 0) & (lb == 0) & (g == 0))
    def _():
        _dil_init_bias(rb_ref, sel_ref, mrow_ref, bias_ref)

    for ll in range(nl):
        sub = slice(ll * DIL_BLOCK, (ll + 1) * DIL_BLOCK)
        prev = slice((ll - 1) * DIL_BLOCK, ll * DIL_BLOCK)
        first = (lb == 0).astype(jnp.int32) if ll == 0 else 0
        for u in range(nr):
            rows = pl.ds(ll * DIL_BLOCK * dil + g * nr + u, DIL_BLOCK, stride=dil)
            if ll == 0 and whole_seq:
                kp = vp = None
            elif ll == 0:
                kp, vp = (lambda sl, u=u: kp_ref[u, :, sl]), (lambda sl, u=u: vp_ref[u, :, sl])
            else:
                kp = lambda sl, u=u, p=prev: kc_ref[u, p, sl]
                vp = lambda sl, u=u, p=prev: vc_ref[u, p, sl]

            if gather:
                def put_num(j, num, ll=ll, u=u):
                    for c in range(DIL_BLOCK // dil):
                        at = pl.multiple_of((ll * (DIL_BLOCK // dil) + c) * dil * dil + (g * nr + u) * dil, dil)
                        acc_ref[j, pl.ds(at, dil), :] = num[c * dil:(c + 1) * dil]
            else:
                def put_num(j, num, rows=rows):
                    acc_ref[j, rows, :] = num

            st_ref[rows, :] = _dil_unit(lambda sl, u=u, r=sub: q_ref[u, r, sl], kp,
                                        lambda sl, u=u, r=sub: kc_ref[u, r, sl], vp,
                                        lambda sl, u=u, r=sub: vc_ref[u, r, sl], first, bias_ref, put_num)

    @pl.when(g == dil // nr - 1)
    def _():
        if gather:
            def body(mm, carry):
                for i in range(dil):
                    for j in range(N_HEADS_DIL // 2):
                        o_ref[pl.ds(pl.multiple_of((mm * dil + i) * dil, dil), dil), j * LANES:(j + 1) * LANES] = (
                            acc_ref[j, pl.ds(mm * dil * dil + i, dil, stride=dil), :].astype(o_ref.dtype))
                return carry
            lax.fori_loop(0, nl * DIL_BLOCK // dil, body, 0)
        else:
            for j in range(N_HEADS_DIL // 2):
                o_ref[:, j * LANES:(j + 1) * LANES] = acc_ref[j].astype(o_ref.dtype)


def _t5_bucket_np(dist):
    max_exact = REL_BUCKETS // 2
    n = np.maximum(dist, 1).astype(np.float32)
    large = max_exact + (np.log(n / np.float32(max_exact)) / np.float32(math.log(REL_MAX_DIST / max_exact))
                         * np.float32(REL_BUCKETS - max_exact)).astype(np.int32)
    large = np.minimum(large, REL_BUCKETS - 1)
    return np.where(dist < max_exact, dist, large)


def _bias_selector(window, dil):
    back = window // dil
    x = np.arange(2 * DIL_BLOCK)
    rel = DIL_BLOCK - x
    valid = (rel >= 0) & (rel <= back)
    bucket = _t5_bucket_np(np.clip(rel, 0, back) * dil)
    sel = np.zeros((LANES, 2 * DIL_BLOCK), np.float32)
    sel[bucket[valid], x[valid]] = 1.0
    mrow = np.where(valid, 0.0, NEG_INF).astype(np.float32)[None]
    return jnp.asarray(sel, BF16), jnp.asarray(mrow)


def _dilated(src, rb_t, batch, seq, window, dil, mem_args=None):
    sub_len = seq // dil
    nb = sub_len // DIL_BLOCK
    sel, mrow = _bias_selector(window, dil)
    nu = DIL_UNITS[dil]
    consts = (rb_t, sel, mrow)
    const = lambda a: pl.BlockSpec(a.shape, lambda *_: (0, 0))
    stat_shape = jax.ShapeDtypeStruct((batch * seq, LANES), F32)
    out_shape = [jax.ShapeDtypeStruct((batch * seq, D_DIL), BF16), stat_shape]
    bias_scratch = pltpu.VMEM((2, N_HEADS_DIL, DIL_BLOCK, 2 * DIL_BLOCK), F32)

    if dil == 1:
        rest, mem, mem_norm_w, w_kv = mem_args
        rows = nu * DIL_BLOCK
        steps = nb // nu
        cur = lambda col: pl.BlockSpec((None, None, rows, D_DIL), lambda b, l: (b, 0, l, col))
        prev = lambda col: pl.BlockSpec((None, None, DIL_BLOCK, D_DIL),
                                        lambda b, l: (b, 0, jnp.maximum(nu * l - 1, 0), col))
        out = lambda width, col=0: pl.BlockSpec((rows, width), lambda b, l: (b * steps + l, col // width))
        mem_specs = [out(D_MEM, COL_MQ), pl.BlockSpec((1, MEM_LEN, D_MODEL), lambda b, l: (b, 0, 0)),
                     const(mem_norm_w), const(w_kv)]
        num, stats, mem_num, mem_den = pl.pallas_call(
            _dil1_kernel,
            grid=(batch, steps),
            in_specs=[cur(0), prev(1), cur(1), prev(2), cur(2)] + [const(a) for a in consts] + mem_specs,
            out_specs=[out(D_DIL), out(LANES), out(D_MEM), out(LANES)],
            out_shape=out_shape + [jax.ShapeDtypeStruct((batch * seq, D_MEM), BF16), stat_shape],
            scratch_shapes=[bias_scratch, pltpu.VMEM((MEM_LEN, 2 * D_MEM), BF16)],
            compiler_params=_cparams(("arbitrary", "arbitrary")),
            name="dilated_d1",
        )(src, src, src, src, src, *consts, rest, mem, mem_norm_w, w_kv)
        return (num, stats), (mem_num, mem_den)

    nl = DIL_LBLOCKS[dil]
    steps = nb // nl
    rows = nl * DIL_BLOCK * dil
    cur = lambda col: pl.BlockSpec((None, nu, nl * DIL_BLOCK, D_DIL), lambda b, l, g: (b, g, l, col))
    prev = lambda col: pl.BlockSpec((None, nu, DIL_BLOCK, D_DIL),
                                    lambda b, l, g: (b, g, jnp.maximum(nl * l - 1, 0), col))
    out = lambda width: pl.BlockSpec((rows, width), lambda b, l, g: (b * steps + l, 0))
    whole_seq = steps == 1
    qkv_specs = [cur(0), cur(1), cur(2)] if whole_seq else [cur(0), prev(1), cur(1), prev(2), cur(2)]
    return pl.pallas_call(
        functools.partial(_dil_kernel, dil=dil, whole_seq=whole_seq),
        grid=(batch, steps, dil // nu),
        in_specs=qkv_specs + [const(a) for a in consts],
        out_specs=[out(D_DIL), out(LANES)],
        out_shape=out_shape,
        scratch_shapes=[bias_scratch, pltpu.VMEM((N_HEADS_DIL // 2, rows, LANES), F32)],
        compiler_params=_cparams(("arbitrary", "arbitrary", "arbitrary")),
        name=f"dilated_d{dil}",
    )(*([src] * len(qkv_specs)), *consts)


GLA_TM = 256
GLA_STEP_TILES = 2


GLA_NC = GLA_TM // GLA_CHUNK


def _gla_tile(bb, rows, w2, q_ref, k_ref, v_ref, g_ref, bg_ref, nw_ref, tri_ref, o_ref, st_ref):
    g = g_ref[bb, rows, :LANES]
    z = jnp.dot(g, w2, preferred_element_type=F32) + bg_ref[...]
    la = (jnp.minimum(z, 0.0) * (LOG2E / GLA_TAU)
          - jnp.log2(1.0 + jnp.exp2(jnp.abs(z) * -LOG2E)) * (1.0 / GLA_TAU))
    la_h = la.astype(BF16)
    la_l = (la - la_h.astype(F32)).astype(BF16)

    tri = tri_ref[...]
    b = jnp.dot(tri, la_h, preferred_element_type=F32) + jnp.dot(tri, la_l, preferred_element_type=F32)
    b_last = jnp.concatenate(
        [jnp.broadcast_to(b[(ci + 1) * GLA_CHUNK - 1:(ci + 1) * GLA_CHUNK, :], (GLA_CHUNK, D_GLA_K))
         for ci in range(GLA_NC)], axis=0)

    q = q_ref[bb, rows].astype(F32)
    k = k_ref[bb, rows].astype(F32)
    q_t = (q * jnp.exp2(b)).astype(BF16)
    k_t = (k * jnp.exp2(-b)).astype(BF16)
    k_s = k * jnp.exp2(b_last - b)

    causal = tri > 0
    t_chunk = lax.broadcasted_iota(jnp.int32, (GLA_DK, GLA_TM), 1) // GLA_CHUNK
    nw = nw_ref[...]
    for h in range(N_HEADS_GLA):
        cs = slice(h * GLA_DK, (h + 1) * GLA_DK)
        vs = slice(h * GLA_DV, (h + 1) * GLA_DV)
        v_h = v_ref[bb, rows, vs]
        a = lax.dot_general(q_t[:, cs], k_t[:, cs], (((1,), (1,)), ((), ())), preferred_element_type=F32)
        o = jnp.dot(jnp.where(causal, a.astype(BF16), jnp.zeros_like(tri)), v_h, preferred_element_type=F32)
        k_s_t = k_s[:, cs].T.astype(BF16)
        zero = jnp.zeros_like(k_s_t)
        kv_all = jnp.dot(jnp.concatenate([jnp.where(t_chunk == ci, k_s_t, zero) for ci in range(GLA_NC)], axis=0),
                         v_h, preferred_element_type=F32)
        decay_t = jnp.exp2(b_last[:, cs].T)
        st = st_ref[bb, h]
        outs = []
        for ci in range(GLA_NC):
            rs = slice(ci * GLA_CHUNK, (ci + 1) * GLA_CHUNK)
            o_c = o[rs] + jnp.dot(q_t[rs, cs], st.astype(BF16), preferred_element_type=F32)
            decay = jnp.broadcast_to(decay_t[:, ci * GLA_CHUNK:ci * GLA_CHUNK + 1], (GLA_DK, GLA_DV))
            st = decay * st + kv_all[ci * GLA_DK:(ci + 1) * GLA_DK]
            ms = jnp.mean(o_c * o_c, axis=-1, keepdims=True)
            outs.append((o_c * lax.rsqrt(ms + EPS) * nw).astype(o_ref.dtype))
        st_ref[bb, h] = st
        o_ref[bb, rows, vs] = jnp.concatenate(outs, axis=0)


def _gla_kernel(q_ref, k_ref, v_ref, g_ref, w2_ref, bg_ref, nw_ref, tri_ref, wout_ref,
                o_ref, wout_bf_ref, st_ref):

    @pl.when(pl.program_id(0) == 0)
    def _():
        st_ref[...] = jnp.zeros_like(st_ref)

    wout_bf_ref[...] = wout_ref[...].astype(wout_bf_ref.dtype)

    w2 = jnp.concatenate([w2_ref[...].astype(BF16), jnp.zeros((LANES - GLA_GATE_RANK, D_GLA_K), BF16)], axis=0)
    for tt in range(GLA_STEP_TILES):
        rows = slice(tt * GLA_TM, (tt + 1) * GLA_TM)
        for bb in range(st_ref.shape[0]):
            _gla_tile(bb, rows, w2, q_ref, k_ref, v_ref, g_ref, bg_ref, nw_ref, tri_ref, o_ref, st_ref)


def _gla(proj, w2, b_gate, norm_w, w_out, batch, seq):
    step_rows = GLA_TM * GLA_STEP_TILES
    steps = seq // step_rows
    wout_rows = w_out.shape[0] // steps
    pos = np.arange(GLA_TM)
    tri = jnp.asarray((pos[:, None] // GLA_CHUNK == pos[None, :] // GLA_CHUNK) & (pos[None, :] <= pos[:, None]),
                      BF16)
    proj3 = proj.reshape(batch, seq, REST_W)
    rows = lambda width, col: pl.BlockSpec((batch, step_rows, width), lambda i: (0, i, col // width))
    const = lambda shape: pl.BlockSpec(shape, lambda i: (0, 0))
    wout_spec = pl.BlockSpec((wout_rows, w_out.shape[1]), lambda i: (i, 0))
    out, w_out_bf = pl.pallas_call(
        _gla_kernel,
        grid=(steps,),
        in_specs=[
            rows(D_GLA_K, COL_GQ), rows(D_GLA_K, COL_GK), rows(D_GLA_V, COL_GV), rows(GLR_PAD, COL_GLR),
            const((GLA_GATE_RANK, D_GLA_K)), const((1, D_GLA_K)), const((1, GLA_DV)),
            const((GLA_TM, GLA_TM)), wout_spec,
        ],
        out_specs=[rows(D_GLA_V, 0), wout_spec],
        out_shape=[jax.ShapeDtypeStruct((batch, seq, D_GLA_V), BF16), jax.ShapeDtypeStruct(w_out.shape, BF16)],
        scratch_shapes=[pltpu.VMEM((batch, N_HEADS_GLA, GLA_DK, GLA_DV), F32)],
        compiler_params=_cparams(("arbitrary",)),
        name="gla",
    )(proj3, proj3, proj3, proj3, w2, b_gate, norm_w, tri, w_out)
    return out.reshape(batch * seq, D_GLA_V), w_out_bf


OUT_TM = 512


def _silu(g):
    h = g * 0.5
    return h * jnp.tanh(h) + h


OUT_GW = 256


def _out_kernel(n1_ref, n2_ref, n3_ref, s1_ref, s2_ref, s3_ref, og_ref, nm_ref, dm_ref, ga_ref, gg_ref, gm_ref,
                sm_ref, ex_ref, exm_ref, w_ref, x_ref, nw_ref, out_ref, mix_ref):
    keep = sm_ref[0:1, :]
    fill = sm_ref[1:2, :]
    stats = [s_ref[...] for s_ref in (s1_ref, s2_ref, s3_ref)]
    m1, m2, m3 = [st * keep for st in stats]
    d1, d2, d3 = [pltpu.roll(st, LANES - STAT_DEN_SHIFT, 1) * keep + fill for st in stats]
    mx = jnp.maximum(jnp.maximum(m1, m2), m3)
    t1, t2, t3 = jnp.exp2(m1 - mx), jnp.exp2(m2 - mx), jnp.exp2(m3 - mx)
    inv = 1.0 / (t1 * d1 + t2 * d2 + t3 * d3)
    wts = [(t * inv).astype(BF16) for t in (t1, t2, t3)]
    inv_m = (1.0 / dm_ref[...]).astype(BF16)

    def gate_chunk(lo):
        cols = slice(lo, lo + OUT_GW)
        if lo < D_DIL:
            a = jnp.zeros((OUT_TM, OUT_GW), F32)
            for wt, n_ref in zip(wts, (n1_ref, n2_ref, n3_ref)):
                a = a + jnp.dot(wt, ex_ref[:, cols], preferred_element_type=F32) * n_ref[:, cols].astype(F32)
            g = ga_ref[:, cols]
        elif lo < D_DIL + D_GLA_V:
            lc = slice(lo - D_DIL, lo - D_DIL + OUT_GW)
            a, g = og_ref[:, lc], gg_ref[:, lc]
        else:
            a = jnp.dot(inv_m, exm_ref[...], preferred_element_type=F32) * nm_ref[...].astype(F32)
            g = gm_ref[...]
        mix_ref[:, cols] = a.astype(BF16) * _silu(g)

    for lo in range(0, D_MIX, OUT_GW):
        gate_chunk(lo)
    y = jnp.dot(mix_ref[...], w_ref[...], preferred_element_type=F32)
    ms = jnp.mean(y * y, axis=-1, keepdims=True)
    out_ref[...] = x_ref[...] + y * lax.rsqrt(ms + EPS) * nw_ref[...]


def _out_proj(pats, o_gla, mem_pair, proj, stat_mask, expand, expand_mem, w_out, x2d, norm_w):
    m = x2d.shape[0]
    rows = lambda width, col=0: pl.BlockSpec((OUT_TM, width), lambda s: (s, col // width))
    const = lambda shape: pl.BlockSpec(shape, lambda s: (0, 0), pipeline_mode=pl.Buffered(1))
    nums, stats = zip(*pats)
    return pl.pallas_call(
        _out_kernel,
        grid=(m // OUT_TM,),
        in_specs=[
            *([rows(D_DIL)] * 3), *([rows(LANES)] * 3),
            rows(D_GLA_V), rows(D_MEM), rows(LANES),
            rows(D_DIL, COL_GATE_A), rows(D_GLA_V, COL_GATE_G), rows(D_MEM, COL_GATE_M),
            const((F32_SUBLANES, LANES)), const((LANES, D_DIL)), const((LANES, D_MEM)), const((D_MIX, D_MODEL)),
            rows(D_MODEL), const((1, D_MODEL)),
        ],
        out_specs=rows(D_MODEL),
        out_shape=jax.ShapeDtypeStruct((m, D_MODEL), x2d.dtype),
        scratch_shapes=[pltpu.VMEM((OUT_TM, D_MIX), BF16)],
        compiler_params=_cparams(("parallel",)),
        name="out_proj",
    )(*nums, *stats, o_gla, *mem_pair, proj, proj, proj, stat_mask, expand, expand_mem, w_out, x2d, norm_w)


D_IN_PROJ = 3 * D_DIL + 2 * D_GLA_K + D_GLA_V + GLA_GATE_RANK + D_MEM + D_MIX
SRC_GLR = 3 * D_DIL + 2 * D_GLA_K + D_GLA_V
SRC_MQ = SRC_GLR + GLA_GATE_RANK
SRC_GATE = SRC_MQ + D_MEM
SCALE_ATTN_Q = HEAD_DIM ** -0.5 * LOG2E
SCALE_GLA_Q = GLA_DK ** -0.5
QKV_SEGMENTS = (
    (0, D_DIL, 0, SCALE_ATTN_Q),
    (D_DIL, 2 * D_DIL, D_DIL, 1.0),
)
REST_SEGMENTS = (
    (QKV_W, D_GLA_K, COL_GQ, SCALE_GLA_Q),
    (QKV_W + D_GLA_K, D_GLA_K + D_GLA_V, COL_GK, 1.0),
    (SRC_GATE + D_DIL, D_GLA_V, COL_GATE_G, 1.0),
    (SRC_MQ, D_MEM, COL_MQ, SCALE_ATTN_Q),
    (SRC_GATE + D_DIL + D_GLA_V, D_MEM, COL_GATE_M, 1.0),
    (SRC_GATE, D_DIL, COL_GATE_A, 1.0),
)
WPREP_TK = 512


def _wprep_kernel(wt_ref, qkv_ref, rest_ref):
    for o_ref, segments in ((qkv_ref, QKV_SEGMENTS), (rest_ref, REST_SEGMENTS)):
        for src, width, dst, scale in segments:
            for lo in range(0, width, LANES):
                w = wt_ref[src + lo:src + lo + LANES, :].T
                if scale != 1.0:
                    w = w * scale
                o_ref[:, dst + lo:dst + lo + LANES] = w.astype(o_ref.dtype)
    glr = wt_ref[SRC_GLR:SRC_GLR + LANES, :].T
    keep = lax.broadcasted_iota(jnp.int32, glr.shape, 1) < GLA_GATE_RANK
    rest_ref[:, COL_GLR:COL_GLR + LANES] = jnp.where(keep, glr, 0.0).astype(rest_ref.dtype)
    rest_ref[:, COL_GLR + LANES:COL_GLR + GLR_PAD] = jnp.zeros((WPREP_TK, GLR_PAD - LANES), rest_ref.dtype)


def _regroup_in_weights(w_in):
    return pl.pallas_call(
        _wprep_kernel,
        grid=(D_MODEL // WPREP_TK,),
        in_specs=[pl.BlockSpec((D_IN_PROJ, WPREP_TK), lambda i: (0, i))],
        out_specs=[pl.BlockSpec((WPREP_TK, QKV_W), lambda i: (i, 0)),
                   pl.BlockSpec((WPREP_TK, REST_W), lambda i: (i, 0))],
        out_shape=[jax.ShapeDtypeStruct((D_MODEL, QKV_W), BF16), jax.ShapeDtypeStruct((D_MODEL, REST_W), BF16)],
        compiler_params=_cparams(("parallel",)),
        name="weight_regroup",
    )(w_in.T)


def kernel(x, mem, norm_pre_w, w_in, rel_bias, w_gla_gate2, b_gla_gate, gla_norm_w, mem_norm_w, w_mem_kv,
           w_out, norm_post_w):
    batch, seq, _ = x.shape
    depth = w_in.shape[0]
    expand_np = np.zeros((LANES, D_DIL), np.float32)
    for h in range(N_HEADS_DIL):
        expand_np[_stat_lane(h), h * HEAD_DIM:(h + 1) * HEAD_DIM] = 1.0
    expand = jnp.asarray(expand_np, BF16)
    expand_mem = jnp.asarray(expand_np[:, :D_MEM], BF16)
    stat_mask_np = np.zeros((F32_SUBLANES, LANES), np.float32)
    stat_mask_np[0, [_stat_lane(h) for h in range(N_HEADS_DIL)]] = 1.0
    stat_mask_np[1] = 1.0 - stat_mask_np[0]
    stat_mask = jnp.asarray(stat_mask_np)
    rb_t = jnp.zeros((2 * F32_SUBLANES, LANES), F32).at[:N_HEADS_DIL, :REL_BUCKETS].set(rel_bias.astype(F32).T)
    for l in range(depth):
        x2d = x.reshape(batch * seq, D_MODEL)
        w_qkv, w_rest = _regroup_in_weights(w_in[l])
        h, qkv1, qkv4, qkv16 = _qkv_proj(x2d, norm_pre_w[l][None], w_qkv, batch, seq)
        rest = _rest_proj(h, w_rest)

        srcs = (qkv1.reshape(batch, 1, seq, QKV_W), qkv4, qkv16)
        mem_args = (rest, mem, mem_norm_w[l][None], w_mem_kv[l])
        pats = [_dilated(src, rb_t, batch, seq, window, dil, mem_args if dil == 1 else None)
                for (window, dil), src in zip(DIL_PATTERNS, srcs)]
        pats[0], mem_pair = pats[0]

        o_gla, w_out_bf = _gla(rest, w_gla_gate2[l], b_gla_gate[l][None], gla_norm_w[l][None], w_out[l],
                               batch, seq)

        out = _out_proj(pats, o_gla, mem_pair, rest, stat_mask, expand, expand_mem, w_out_bf, x2d,
                        norm_post_w[l][None])
        x = out.reshape(batch, seq, D_MODEL)
    return x
```

```python
import functools
import math

import numpy as np
import jax
import jax.numpy as jnp
from jax import lax
from jax.experimental import pallas as pl
from jax.experimental.pallas import tpu as pltpu

F32 = jnp.float32
BF16 = jnp.bfloat16

D_MODEL = 2048
HEAD_DIM = 64
N_HEADS_DIL = 12
DIL_PATTERNS = ((128, 1), (512, 4), (2048, 16))
DIL_BLOCK = 128
N_HEADS_GLA = 4
GLA_DK = 128
GLA_DV = 256
GLA_GATE_RANK = 16
GLA_TAU = 16.0
GLA_CHUNK = 64
N_HEADS_MEM = 4
MEM_LEN = 256
REL_BUCKETS = 32
REL_MAX_DIST = 2048
EPS = 1e-6
NEG_INF = -1e30

D_DIL = N_HEADS_DIL * HEAD_DIM
D_GLA_K = N_HEADS_GLA * GLA_DK
D_GLA_V = N_HEADS_GLA * GLA_DV
D_MEM = N_HEADS_MEM * HEAD_DIM
D_MIX = D_DIL + D_GLA_V + D_MEM

LANES = 128
F32_SUBLANES = 8
GLR_PAD = 256

QKV_W = 3 * D_DIL
COL_GQ = 0
COL_GK = COL_GQ + D_GLA_K
COL_GV = COL_GK + D_GLA_K
COL_GATE_G = COL_GV + D_GLA_V
COL_MQ = COL_GATE_G + D_GLA_V
COL_GATE_M = COL_MQ + D_MEM
COL_GLR = COL_GATE_M + D_MEM
COL_GATE_A = COL_GLR + GLR_PAD
REST_W = COL_GATE_A + D_DIL

VMEM_LIMIT = 56 * 1024 * 1024


def _cparams(sem):
    return pltpu.CompilerParams(dimension_semantics=sem, vmem_limit_bytes=VMEM_LIMIT)


def _split_hi_lo(w):
    hi = w.astype(BF16)
    return hi, (w - hi.astype(F32)).astype(BF16)


QKV_TM = 512
N_SLABS = D_DIL // LANES
REGROUP_DILS = (4, 16)


def _qkv_proj_kernel(x_ref, nw_ref, w_ref, h_ref, o_ref, x4_ref, x16_ref, slab_ref, sub_ref):
    x = x_ref[...]
    ms = jnp.mean(x * x, axis=-1, keepdims=True)
    h_ref[...] = (x * lax.rsqrt(ms + EPS) * nw_ref[...]).astype(BF16)
    for t in range(QKV_W // D_DIL):
        cols = slice(t * D_DIL, (t + 1) * D_DIL)
        res = jnp.dot(h_ref[...], w_ref[:, cols], preferred_element_type=F32)
        o_ref[:, cols] = res.astype(o_ref.dtype)
        for s in range(N_SLABS):
            slab_ref[t, s] = res[:, s * LANES:(s + 1) * LANES]
        for s in range(N_SLABS):
            lo = t * D_DIL + s * LANES
            for r in range(4):
                sub = slab_ref[t, s, pl.ds(r, QKV_TM // 4, stride=4), :]
                x4_ref[0, r, :, lo:lo + LANES] = sub.astype(x4_ref.dtype)
                sub_ref[t, s, r] = sub
            for r in range(4):
                for q in range(4):
                    x16_ref[0, 4 * q + r, :, lo:lo + LANES] = (
                        sub_ref[t, s, r, pl.ds(q, QKV_TM // 16, stride=4), :].astype(x16_ref.dtype))


def _qkv_proj(x2d, norm_w, w_qkv, batch, seq):
    m = x2d.shape[0]
    tiles_per_seq = seq // QKV_TM
    const = lambda shape: pl.BlockSpec(shape, lambda i: (0, 0), pipeline_mode=pl.Buffered(1))
    grouped = lambda dil: pl.BlockSpec((1, dil, QKV_TM // dil, QKV_W),
                                       lambda i: (i // tiles_per_seq, 0, i % tiles_per_seq, 0))
    return pl.pallas_call(
        _qkv_proj_kernel,
        grid=(m // QKV_TM,),
        in_specs=[
            pl.BlockSpec((QKV_TM, D_MODEL), lambda i: (i, 0)),
            const((1, D_MODEL)), const((D_MODEL, QKV_W)),
        ],
        out_specs=[pl.BlockSpec((QKV_TM, D_MODEL), lambda i: (i, 0)),
                   pl.BlockSpec((QKV_TM, QKV_W), lambda i: (i, 0))] + [grouped(d) for d in REGROUP_DILS],
        out_shape=[jax.ShapeDtypeStruct((m, D_MODEL), BF16), jax.ShapeDtypeStruct((m, QKV_W), BF16)]
        + [jax.ShapeDtypeStruct((batch, d, seq // d, QKV_W), BF16) for d in REGROUP_DILS],
        scratch_shapes=[pltpu.VMEM((QKV_W // D_DIL, N_SLABS, QKV_TM, LANES), F32),
                        pltpu.VMEM((QKV_W // D_DIL, N_SLABS, 4, QKV_TM // 4, LANES), F32)],
        compiler_params=_cparams(("parallel",)),
        name="qkv_proj",
    )(x2d, norm_w, w_qkv)


REST_TM = 512


def _rest_proj_kernel(h_ref, w_ref, o_ref):
    o_ref[...] = jnp.dot(h_ref[...], w_ref[...], preferred_element_type=F32).astype(o_ref.dtype)


def _rest_proj(h, w_rest):
    m = h.shape[0]
    const = lambda shape: pl.BlockSpec(shape, lambda i: (0, 0), pipeline_mode=pl.Buffered(1))
    return pl.pallas_call(
        _rest_proj_kernel,
        grid=(m // REST_TM,),
        in_specs=[pl.BlockSpec((REST_TM, D_MODEL), lambda i: (i, 0)), const((D_MODEL, REST_W))],
        out_specs=pl.BlockSpec((REST_TM, REST_W), lambda i: (i, 0)),
        out_shape=jax.ShapeDtypeStruct((m, REST_W), BF16),
        compiler_params=_cparams(("parallel",)),
        name="rest_proj",
    )(h, w_rest)


LOG2E = math.log2(math.e)


def _pair_attention(q_pair, k_pair, v_pair, bias_e, bias_o):
    rows = q_pair.shape[0]
    lo_q = lax.broadcasted_iota(jnp.int32, q_pair.shape, 1) < HEAD_DIM
    zero = jnp.zeros_like(q_pair)
    q_both = jnp.concatenate([jnp.where(lo_q, q_pair, zero), jnp.where(lo_q, zero, q_pair)], axis=0)
    s = lax.dot_general(q_both, k_pair, (((1,), (1,)), ((), ())), preferred_element_type=F32)
    s_e, s_o = s[:rows], s[rows:]
    if bias_e is not None:
        s_e = s_e + bias_e
        s_o = s_o + bias_o
    mx_e = jnp.max(s_e, axis=-1, keepdims=True)
    mx_o = jnp.max(s_o, axis=-1, keepdims=True)
    p_e = jnp.exp2(s_e - mx_e).astype(BF16)
    p_o = jnp.exp2(s_o - mx_o).astype(BF16)
    lo_v = lax.broadcasted_iota(jnp.int32, v_pair.shape, 1) < HEAD_DIM
    one = jnp.ones_like(v_pair)
    r_e = jnp.dot(p_e, jnp.where(lo_v, v_pair, one), preferred_element_type=F32)
    r_o = jnp.dot(p_o, jnp.where(lo_v, one, v_pair), preferred_element_type=F32)
    return r_e, r_o, mx_e, mx_o


def _stat_lane(head):
    return head + HEAD_DIM if head % 2 == 0 else head


STAT_DEN_SHIFT = 16


DIL_UNITS = {1: 8, 4: 4, 16: 8}


def _dil_init_bias(rb_ref, sel_ref, mrow_ref, bias_ref):
    rb_hi, rb_lo = _split_hi_lo(rb_ref[...])
    f = (jnp.dot(rb_hi, sel_ref[...], preferred_element_type=F32)
         + jnp.dot(rb_lo, sel_ref[...], preferred_element_type=F32)) * LOG2E + mrow_ref[...]
    col = lax.broadcasted_iota(jnp.int32, (DIL_BLOCK, 2 * DIL_BLOCK), 1)
    for h in range(N_HEADS_DIL):
        row = jnp.broadcast_to(f[h:h + 1, :], (DIL_BLOCK, 2 * DIL_BLOCK))
        tab = pltpu.roll(row, 0, 1, stride=1, stride_axis=0)
        bias_ref[0, h] = tab
        bias_ref[1, h] = jnp.where(col >= DIL_BLOCK, tab, NEG_INF)


def _dil_unit(q, kp, kc, vp, vc, first, bias_ref, put_num):
    lane = lax.broadcasted_iota(jnp.int32, (DIL_BLOCK, LANES), 1)
    lo = lane < HEAD_DIM
    st_c = jnp.zeros((DIL_BLOCK, LANES), F32)
    for j in range(N_HEADS_DIL // 2):
        sl = slice(j * LANES, (j + 1) * LANES)
        if kp is None:
            k_pair, v_pair = kc(sl), vc(sl)
            bias_e, bias_o = bias_ref[0, 2 * j, :, DIL_BLOCK:], bias_ref[0, 2 * j + 1, :, DIL_BLOCK:]
        else:
            k_pair = jnp.concatenate([kp(sl), kc(sl)], axis=0)
            v_pair = jnp.concatenate([vp(sl), vc(sl)], axis=0)
            bias_e, bias_o = bias_ref[first, 2 * j], bias_ref[first, 2 * j + 1]
        r_e, r_o, mx_e, mx_o = _pair_attention(q(sl), k_pair, v_pair, bias_e, bias_o)
        put_num(j, jnp.where(lo, r_e, r_o))
        l_e, l_o = _stat_lane(2 * j), _stat_lane(2 * j + 1)
        st_c = jnp.where(lane == l_e, mx_e, jnp.where(lane == l_e + STAT_DEN_SHIFT, r_e, st_c))
        st_c = jnp.where(lane == l_o, mx_o, jnp.where(lane == l_o + STAT_DEN_SHIFT, r_o, st_c))
    return st_c


def _mem_kv(mem_ref, mnw_ref, wkv_ref, kv_ref):
    m = mem_ref[0]
    ms = jnp.mean(m * m, axis=-1, keepdims=True)
    mn = (m * lax.rsqrt(ms + EPS) * mnw_ref[...]).astype(BF16)
    kv_ref[...] = jnp.dot(mn, wkv_ref[...].astype(BF16), preferred_element_type=F32).astype(BF16)


def _mem_unit(q, kv_ref, put_num):
    lane = lax.broadcasted_iota(jnp.int32, (DIL_BLOCK, LANES), 1)
    lo = lane < HEAD_DIM
    den_c = jnp.ones((DIL_BLOCK, LANES), F32)
    for j in range(N_HEADS_MEM // 2):
        sl = slice(j * LANES, (j + 1) * LANES)
        r_e, r_o, _, _ = _pair_attention(q(sl), kv_ref[:, sl], kv_ref[:, D_MEM + j * LANES:D_MEM + (j + 1) * LANES],
                                         None, None)
        put_num(j, jnp.where(lo, r_e, r_o))
        den_c = jnp.where(lane == _stat_lane(2 * j), r_e, jnp.where(lane == _stat_lane(2 * j + 1), r_o, den_c))
    return den_c


def _dil1_kernel(q_ref, kp_ref, kc_ref, vp_ref, vc_ref, rb_ref, sel_ref, mrow_ref, mq_ref, mem_ref, mnw_ref, wkv_ref,
                 o_ref, st_ref, om_ref, dm_ref, bias_ref, kv_ref):
    @pl.when((pl.program_id(0) == 0) & (pl.program_id(1) == 0))
    def _():
        _dil_init_bias(rb_ref, sel_ref, mrow_ref, bias_ref)

    @pl.when(pl.program_id(1) == 0)
    def _():
        _mem_kv(mem_ref, mnw_ref, wkv_ref, kv_ref)

    for u in range(DIL_UNITS[1]):
        rows = slice(u * DIL_BLOCK, (u + 1) * DIL_BLOCK)
        prev = slice((u - 1) * DIL_BLOCK, u * DIL_BLOCK)
        if u == 0:
            first = (pl.program_id(1) == 0).astype(jnp.int32)
            kp, vp = (lambda sl: kp_ref[:, sl]), (lambda sl: vp_ref[:, sl])
        else:
            first = 0
            kp, vp = (lambda sl, p=prev: kc_ref[p, sl]), (lambda sl, p=prev: vc_ref[p, sl])

        def put_num(j, num, rows=rows):
            o_ref[rows, j * LANES:(j + 1) * LANES] = num.astype(o_ref.dtype)

        st_ref[rows, :] = _dil_unit(lambda sl, r=rows: q_ref[r, sl], kp, lambda sl, r=rows: kc_ref[r, sl],
                                    vp, lambda sl, r=rows: vc_ref[r, sl], first, bias_ref, put_num)

        def put_mem(j, num, rows=rows):
            om_ref[rows, j * LANES:(j + 1) * LANES] = num.astype(om_ref.dtype)

        dm_ref[rows, :] = _mem_unit(lambda sl, r=rows: mq_ref[r, sl], kv_ref, put_mem)


DIL_LBLOCKS = {4: 2, 16: 2}


def _dil_kernel(*refs, dil, whole_seq):
    if whole_seq:
        q_ref, kc_ref, vc_ref, *rest = refs
        kp_ref = vp_ref = None
    else:
        q_ref, kp_ref, kc_ref, vp_ref, vc_ref, *rest = refs
    rb_ref, sel_ref, mrow_ref, o_ref, st_ref, bias_ref, acc_ref = rest
    b, lb, g = pl.program_id(0), pl.program_id(1), pl.program_id(2)
    nr, nl = DIL_UNITS[dil], DIL_LBLOCKS[dil]
    gather = whole_seq and dil == 2 * F32_SUBLANES

    @pl.when((b == 0) & (lb == 0) & (g == 0))
    def _():
        _dil_init_bias(rb_ref, sel_ref, mrow_ref, bias_ref)

    for ll in range(nl):
        sub = slice(ll * DIL_BLOCK, (ll + 1) * DIL_BLOCK)
        prev = slice((ll - 1) * DIL_BLOCK, ll * DIL_BLOCK)
        first = (lb == 0).astype(jnp.int32) if ll == 0 else 0
        for u in range(nr):
            rows = pl.ds(ll * DIL_BLOCK * dil + g * nr + u, DIL_BLOCK, stride=dil)
            if ll == 0 and whole_seq:
                kp = vp = None
            elif ll == 0:
                kp, vp = (lambda sl, u=u: kp_ref[u, :, sl]), (lambda sl, u=u: vp_ref[u, :, sl])
            else:
                kp = lambda sl, u=u, p=prev: kc_ref[u, p, sl]
                vp = lambda sl, u=u, p=prev: vc_ref[u, p, sl]

            if gather:
                def put_num(j, num, ll=ll, u=u):
                    for c in range(DIL_BLOCK // dil):
                        at = pl.multiple_of((ll * (DIL_BLOCK // dil) + c) * dil * dil + (g * nr + u) * dil, dil)
                        acc_ref[j, pl.ds(at, dil), :] = num[c * dil:(c + 1) * dil]
            else:
                def put_num(j, num, rows=rows):
                    acc_ref[j, rows, :] = num

            st_ref[rows, :] = _dil_unit(lambda sl, u=u, r=sub: q_ref[u, r, sl], kp,
                                        lambda sl, u=u, r=sub: kc_ref[u, r, sl], vp,
                                        lambda sl, u=u, r=sub: vc_ref[u, r, sl], first, bias_ref, put_num)

    @pl.when(g == dil // nr - 1)
    def _():
        if gather:
            def body(mm, carry):
                for i in range(dil):
                    for j in range(N_HEADS_DIL // 2):
                        o_ref[pl.ds(pl.multiple_of((mm * dil + i) * dil, dil), dil), j * LANES:(j + 1) * LANES] = (
                            acc_ref[j, pl.ds(mm * dil * dil + i, dil, stride=dil), :].astype(o_ref.dtype))
                return carry
            lax.fori_loop(0, nl * DIL_BLOCK // dil, body, 0)
        else:
            for j in range(N_HEADS_DIL // 2):
                o_ref[:, j * LANES:(j + 1) * LANES] = acc_ref[j].astype(o_ref.dtype)


def _t5_bucket_np(dist):
    max_exact = REL_BUCKETS // 2
    n = np.maximum(dist, 1).astype(np.float32)
    large = max_exact + (np.log(n / np.float32(max_exact)) / np.float32(math.log(REL_MAX_DIST / max_exact))
                         * np.float32(REL_BUCKETS - max_exact)).astype(np.int32)
    large = np.minimum(large, REL_BUCKETS - 1)
    return np.where(dist < max_exact, dist, large)


def _bias_selector(window, dil):
    back = window // dil
    x = np.arange(2 * DIL_BLOCK)
    rel = DIL_BLOCK - x
    valid = (rel >= 0) & (rel <= back)
    bucket = _t5_bucket_np(np.clip(rel, 0, back) * dil)
    sel = np.zeros((LANES, 2 * DIL_BLOCK), np.float32)
    sel[bucket[valid], x[valid]] = 1.0
    mrow = np.where(valid, 0.0, NEG_INF).astype(np.float32)[None]
    return jnp.asarray(sel, BF16), jnp.asarray(mrow)


def _dilated(src, rb_t, batch, seq, window, dil, mem_args=None):
    sub_len = seq // dil
    nb = sub_len // DIL_BLOCK
    sel, mrow = _bias_selector(window, dil)
    nu = DIL_UNITS[dil]
    consts = (rb_t, sel, mrow)
    const = lambda a: pl.BlockSpec(a.shape, lambda *_: (0, 0))
    stat_shape = jax.ShapeDtypeStruct((batch * seq, LANES), F32)
    out_shape = [jax.ShapeDtypeStruct((batch * seq, D_DIL), BF16), stat_shape]
    bias_scratch = pltpu.VMEM((2, N_HEADS_DIL, DIL_BLOCK, 2 * DIL_BLOCK), F32)

    if dil == 1:
        rest, mem, mem_norm_w, w_kv = mem_args
        rows = nu * DIL_BLOCK
        steps = nb // nu
        cur = lambda col: pl.BlockSpec((None, None, rows, D_DIL), lambda b, l: (b, 0, l, col))
        prev = lambda col: pl.BlockSpec((None, None, DIL_BLOCK, D_DIL),
                                        lambda b, l: (b, 0, jnp.maximum(nu * l - 1, 0), col))
        out = lambda width, col=0: pl.BlockSpec((rows, width), lambda b, l: (b * steps + l, col // width))
        mem_specs = [out(D_MEM, COL_MQ), pl.BlockSpec((1, MEM_LEN, D_MODEL), lambda b, l: (b, 0, 0)),
                     const(mem_norm_w), const(w_kv)]
        num, stats, mem_num, mem_den = pl.pallas_call(
            _dil1_kernel,
            grid=(batch, steps),
            in_specs=[cur(0), prev(1), cur(1), prev(2), cur(2)] + [const(a) for a in consts] + mem_specs,
            out_specs=[out(D_DIL), out(LANES), out(D_MEM), out(LANES)],
            out_shape=out_shape + [jax.ShapeDtypeStruct((batch * seq, D_MEM), BF16), stat_shape],
            scratch_shapes=[bias_scratch, pltpu.VMEM((MEM_LEN, 2 * D_MEM), BF16)],
            compiler_params=_cparams(("arbitrary", "arbitrary")),
            name="dilated_d1",
        )(src, src, src, src, src, *consts, rest, mem, mem_norm_w, w_kv)
        return (num, stats), (mem_num, mem_den)

    nl = DIL_LBLOCKS[dil]
    steps = nb // nl
    rows = nl * DIL_BLOCK * dil
    cur = lambda col: pl.BlockSpec((None, nu, nl * DIL_BLOCK, D_DIL), lambda b, l, g: (b, g, l, col))
    prev = lambda col: pl.BlockSpec((None, nu, DIL_BLOCK, D_DIL),
                                    lambda b, l, g: (b, g, jnp.maximum(nl * l - 1, 0), col))
    out = lambda width: pl.BlockSpec((rows, width), lambda b, l, g: (b * steps + l, 0))
    whole_seq = steps == 1
    qkv_specs = [cur(0), cur(1), cur(2)] if whole_seq else [cur(0), prev(1), cur(1), prev(2), cur(2)]
    return pl.pallas_call(
        functools.partial(_dil_kernel, dil=dil, whole_seq=whole_seq),
        grid=(batch, steps, dil // nu),
        in_specs=qkv_specs + [const(a) for a in consts],
        out_specs=[out(D_DIL), out(LANES)],
        out_shape=out_shape,
        scratch_shapes=[bias_scratch, pltpu.VMEM((N_HEADS_DIL // 2, rows, LANES), F32)],
        compiler_params=_cparams(("arbitrary", "arbitrary", "arbitrary")),
        name=f"dilated_d{dil}",
    )(*([src] * len(qkv_specs)), *consts)


GLA_TM = 256
GLA_STEP_TILES = 2


GLA_NC = GLA_TM // GLA_CHUNK


def _gla_tile(bb, rows, w2, q_ref, k_ref, v_ref, g_ref, bg_ref, nw_ref, tri_ref, o_ref, st_ref):
    g = g_ref[bb, rows, :LANES]
    z = jnp.dot(g, w2, preferred_element_type=F32) + bg_ref[...]
    la = (jnp.minimum(z, 0.0) * (LOG2E / GLA_TAU)
          - jnp.log2(1.0 + jnp.exp2(jnp.abs(z) * -LOG2E)) * (1.0 / GLA_TAU))
    la_h = la.astype(BF16)
    la_l = (la - la_h.astype(F32)).astype(BF16)

    tri = tri_ref[...]
    b = jnp.dot(tri, la_h, preferred_element_type=F32) + jnp.dot(tri, la_l, preferred_element_type=F32)
    b_last = jnp.concatenate(
        [jnp.broadcast_to(b[(ci + 1) * GLA_CHUNK - 1:(ci + 1) * GLA_CHUNK, :], (GLA_CHUNK, D_GLA_K))
         for ci in range(GLA_NC)], axis=0)

    q = q_ref[bb, rows].astype(F32)
    k = k_ref[bb, rows].astype(F32)
    q_t = (q * jnp.exp2(b)).astype(BF16)
    k_t = (k * jnp.exp2(-b)).astype(BF16)
    k_s = k * jnp.exp2(b_last - b)

    causal = tri > 0
    t_chunk = lax.broadcasted_iota(jnp.int32, (GLA_DK, GLA_TM), 1) // GLA_CHUNK
    nw = nw_ref[...]
    for h in range(N_HEADS_GLA):
        cs = slice(h * GLA_DK, (h + 1) * GLA_DK)
        vs = slice(h * GLA_DV, (h + 1) * GLA_DV)
        v_h = v_ref[bb, rows, vs]
        a = lax.dot_general(q_t[:, cs], k_t[:, cs], (((1,), (1,)), ((), ())), preferred_element_type=F32)
        o = jnp.dot(jnp.where(causal, a.astype(BF16), jnp.zeros_like(tri)), v_h, preferred_element_type=F32)
        k_s_t = k_s[:, cs].T.astype(BF16)
        zero = jnp.zeros_like(k_s_t)
        kv_all = jnp.dot(jnp.concatenate([jnp.where(t_chunk == ci, k_s_t, zero) for ci in range(GLA_NC)], axis=0),
                         v_h, preferred_element_type=F32)
        decay_t = jnp.exp2(b_last[:, cs].T)
        st = st_ref[bb, h]
        outs = []
        for ci in range(GLA_NC):
            rs = slice(ci * GLA_CHUNK, (ci + 1) * GLA_CHUNK)
            o_c = o[rs] + jnp.dot(q_t[rs, cs], st.astype(BF16), preferred_element_type=F32)
            decay = jnp.broadcast_to(decay_t[:, ci * GLA_CHUNK:ci * GLA_CHUNK + 1], (GLA_DK, GLA_DV))
            st = decay * st + kv_all[ci * GLA_DK:(ci + 1) * GLA_DK]
            ms = jnp.mean(o_c * o_c, axis=-1, keepdims=True)
            outs.append((o_c * lax.rsqrt(ms + EPS) * nw).astype(o_ref.dtype))
        st_ref[bb, h] = st
        o_ref[bb, rows, vs] = jnp.concatenate(outs, axis=0)


def _gla_kernel(q_ref, k_ref, v_ref, g_ref, w2_ref, bg_ref, nw_ref, tri_ref, wout_ref,
                o_ref, wout_bf_ref, st_ref):

    @pl.when(pl.program_id(0) == 0)
    def _():
        st_ref[...] = jnp.zeros_like(st_ref)

    wout_bf_ref[...] = wout_ref[...].astype(wout_bf_ref.dtype)

    w2 = jnp.concatenate([w2_ref[...].astype(BF16), jnp.zeros((LANES - GLA_GATE_RANK, D_GLA_K), BF16)], axis=0)
    for tt in range(GLA_STEP_TILES):
        rows = slice(tt * GLA_TM, (tt + 1) * GLA_TM)
        for bb in range(st_ref.shape[0]):
            _gla_tile(bb, rows, w2, q_ref, k_ref, v_ref, g_ref, bg_ref, nw_ref, tri_ref, o_ref, st_ref)


def _gla(proj, w2, b_gate, norm_w, w_out, batch, seq):
    step_rows = GLA_TM * GLA_STEP_TILES
    steps = seq // step_rows
    wout_rows = w_out.shape[0] // steps
    pos = np.arange(GLA_TM)
    tri = jnp.asarray((pos[:, None] // GLA_CHUNK == pos[None, :] // GLA_CHUNK) & (pos[None, :] <= pos[:, None]),
                      BF16)
    proj3 = proj.reshape(batch, seq, REST_W)
    rows = lambda width, col: pl.BlockSpec((batch, step_rows, width), lambda i: (0, i, col // width))
    const = lambda shape: pl.BlockSpec(shape, lambda i: (0, 0))
    wout_spec = pl.BlockSpec((wout_rows, w_out.shape[1]), lambda i: (i, 0))
    out, w_out_bf = pl.pallas_call(
        _gla_kernel,
        grid=(steps,),
        in_specs=[
            rows(D_GLA_K, COL_GQ), rows(D_GLA_K, COL_GK), rows(D_GLA_V, COL_GV), rows(GLR_PAD, COL_GLR),
            const((GLA_GATE_RANK, D_GLA_K)), const((1, D_GLA_K)), const((1, GLA_DV)),
            const((GLA_TM, GLA_TM)), wout_spec,
        ],
        out_specs=[rows(D_GLA_V, 0), wout_spec],
        out_shape=[jax.ShapeDtypeStruct((batch, seq, D_GLA_V), BF16), jax.ShapeDtypeStruct(w_out.shape, BF16)],
        scratch_shapes=[pltpu.VMEM((batch, N_HEADS_GLA, GLA_DK, GLA_DV), F32)],
        compiler_params=_cparams(("arbitrary",)),
        name="gla",
    )(proj3, proj3, proj3, proj3, w2, b_gate, norm_w, tri, w_out)
    return out.reshape(batch * seq, D_GLA_V), w_out_bf


OUT_TM = 512


def _silu(g):
    h = g * 0.5
    return h * jnp.tanh(h) + h


OUT_GW = 256


def _out_kernel(n1_ref, n2_ref, n3_ref, s1_ref, s2_ref, s3_ref, og_ref, nm_ref, dm_ref, ga_ref, gg_ref, gm_ref,
                sm_ref, ex_ref, exm_ref, w_ref, x_ref, nw_ref, out_ref, mix_ref):
    keep = sm_ref[0:1, :]
    fill = sm_ref[1:2, :]
    stats = [s_ref[...] for s_ref in (s1_ref, s2_ref, s3_ref)]
    m1, m2, m3 = [st * keep for st in stats]
    d1, d2, d3 = [pltpu.roll(st, LANES - STAT_DEN_SHIFT, 1) * keep + fill for st in stats]
    mx = jnp.maximum(jnp.maximum(m1, m2), m3)
    t1, t2, t3 = jnp.exp2(m1 - mx), jnp.exp2(m2 - mx), jnp.exp2(m3 - mx)
    inv = 1.0 / (t1 * d1 + t2 * d2 + t3 * d3)
    wts = [(t * inv).astype(BF16) for t in (t1, t2, t3)]
    inv_m = (1.0 / dm_ref[...]).astype(BF16)

    def gate_chunk(lo):
        cols = slice(lo, lo + OUT_GW)
        if lo < D_DIL:
            a = jnp.zeros((OUT_TM, OUT_GW), F32)
            for wt, n_ref in zip(wts, (n1_ref, n2_ref, n3_ref)):
                a = a + jnp.dot(wt, ex_ref[:, cols], preferred_element_type=F32) * n_ref[:, cols].astype(F32)
            g = ga_ref[:, cols]
        elif lo < D_DIL + D_GLA_V:
            lc = slice(lo - D_DIL, lo - D_DIL + OUT_GW)
            a, g = og_ref[:, lc], gg_ref[:, lc]
        else:
            a = jnp.dot(inv_m, exm_ref[...], preferred_element_type=F32) * nm_ref[...].astype(F32)
            g = gm_ref[...]
        mix_ref[:, cols] = a.astype(BF16) * _silu(g)

    for lo in range(0, D_MIX, OUT_GW):
        gate_chunk(lo)
    y = jnp.dot(mix_ref[...], w_ref[...], preferred_element_type=F32)
    ms = jnp.mean(y * y, axis=-1, keepdims=True)
    out_ref[...] = x_ref[...] + y * lax.rsqrt(ms + EPS) * nw_ref[...]


def _out_proj(pats, o_gla, mem_pair, proj, stat_mask, expand, expand_mem, w_out, x2d, norm_w):
    m = x2d.shape[0]
    rows = lambda width, col=0: pl.BlockSpec((OUT_TM, width), lambda s: (s, col // width))
    const = lambda shape: pl.BlockSpec(shape, lambda s: (0, 0), pipeline_mode=pl.Buffered(1))
    nums, stats = zip(*pats)
    return pl.pallas_call(
        _out_kernel,
        grid=(m // OUT_TM,),
        in_specs=[
            *([rows(D_DIL)] * 3), *([rows(LANES)] * 3),
            rows(D_GLA_V), rows(D_MEM), rows(LANES),
            rows(D_DIL, COL_GATE_A), rows(D_GLA_V, COL_GATE_G), rows(D_MEM, COL_GATE_M),
            const((F32_SUBLANES, LANES)), const((LANES, D_DIL)), const((LANES, D_MEM)), const((D_MIX, D_MODEL)),
            rows(D_MODEL), const((1, D_MODEL)),
        ],
        out_specs=rows(D_MODEL),
        out_shape=jax.ShapeDtypeStruct((m, D_MODEL), x2d.dtype),
        scratch_shapes=[pltpu.VMEM((OUT_TM, D_MIX), BF16)],
        compiler_params=_cparams(("parallel",)),
        name="out_proj",
    )(*nums, *stats, o_gla, *mem_pair, proj, proj, proj, stat_mask, expand, expand_mem, w_out, x2d, norm_w)


D_IN_PROJ = 3 * D_DIL + 2 * D_GLA_K + D_GLA_V + GLA_GATE_RANK + D_MEM + D_MIX
SRC_GLR = 3 * D_DIL + 2 * D_GLA_K + D_GLA_V
SRC_MQ = SRC_GLR + GLA_GATE_RANK
SRC_GATE = SRC_MQ + D_MEM
SCALE_ATTN_Q = HEAD_DIM ** -0.5 * LOG2E
SCALE_GLA_Q = GLA_DK ** -0.5
QKV_SEGMENTS = (
    (0, D_DIL, 0, SCALE_ATTN_Q),
    (D_DIL, 2 * D_DIL, D_DIL, 1.0),
)
REST_SEGMENTS = (
    (QKV_W, D_GLA_K, COL_GQ, SCALE_GLA_Q),
    (QKV_W + D_GLA_K, D_GLA_K + D_GLA_V, COL_GK, 1.0),
    (SRC_GATE + D_DIL, D_GLA_V, COL_GATE_G, 1.0),
    (SRC_MQ, D_MEM, COL_MQ, SCALE_ATTN_Q),
    (SRC_GATE + D_DIL + D_GLA_V, D_MEM, COL_GATE_M, 1.0),
    (SRC_GATE, D_DIL, COL_GATE_A, 1.0),
)
WPREP_TK = 512


def _wprep_kernel(wt_ref, qkv_ref, rest_ref):
    for o_ref, segments in ((qkv_ref, QKV_SEGMENTS), (rest_ref, REST_SEGMENTS)):
        for src, width, dst, scale in segments:
            for lo in range(0, width, LANES):
                w = wt_ref[src + lo:src + lo + LANES, :].T
                if scale != 1.0:
                    w = w * scale
                o_ref[:, dst + lo:dst + lo + LANES] = w.astype(o_ref.dtype)
    glr = wt_ref[SRC_GLR:SRC_GLR + LANES, :].T
    keep = lax.broadcasted_iota(jnp.int32, glr.shape, 1) < GLA_GATE_RANK
    rest_ref[:, COL_GLR:COL_GLR + LANES] = jnp.where(keep, glr, 0.0).astype(rest_ref.dtype)
    rest_ref[:, COL_GLR + LANES:COL_GLR + GLR_PAD] = jnp.zeros((WPREP_TK, GLR_PAD - LANES), rest_ref.dtype)


def _regroup_in_weights(w_in):
    return pl.pallas_call(
        _wprep_kernel,
        grid=(D_MODEL // WPREP_TK,),
        in_specs=[pl.BlockSpec((D_IN_PROJ, WPREP_TK), lambda i: (0, i))],
        out_specs=[pl.BlockSpec((WPREP_TK, QKV_W), lambda i: (i, 0)),
                   pl.BlockSpec((WPREP_TK, REST_W), lambda i: (i, 0))],
        out_shape=[jax.ShapeDtypeStruct((D_MODEL, QKV_W), BF16), jax.ShapeDtypeStruct((D_MODEL, REST_W), BF16)],
        compiler_params=_cparams(("parallel",)),
        name="weight_regroup",
    )(w_in.T)


def kernel(x, mem, norm_pre_w, w_in, rel_bias, w_gla_gate2, b_gla_gate, gla_norm_w, mem_norm_w, w_mem_kv,
           w_out, norm_post_w):
    batch, seq, _ = x.shape
    depth = w_in.shape[0]
    expand_np = np.zeros((LANES, D_DIL), np.float32)
    for h in range(N_HEADS_DIL):
        expand_np[_stat_lane(h), h * HEAD_DIM:(h + 1) * HEAD_DIM] = 1.0
    expand = jnp.asarray(expand_np, BF16)
    expand_mem = jnp.asarray(expand_np[:, :D_MEM], BF16)
    stat_mask_np = np.zeros((F32_SUBLANES, LANES), np.float32)
    stat_mask_np[0, [_stat_lane(h) for h in range(N_HEADS_DIL)]] = 1.0
    stat_mask_np[1] = 1.0 - stat_mask_np[0]
    stat_mask = jnp.asarray(stat_mask_np)
    rb_t = jnp.zeros((2 * F32_SUBLANES, LANES), F32).at[:N_HEADS_DIL, :REL_BUCKETS].set(rel_bias.astype(F32).T)
    for l in range(depth):
        x2d = x.reshape(batch * seq, D_MODEL)
        w_qkv, w_rest = _regroup_in_weights(w_in[l])
        h, qkv1, qkv4, qkv16 = _qkv_proj(x2d, norm_pre_w[l][None], w_qkv, batch, seq)
        rest = _rest_proj(h, w_rest)

        srcs = (qkv1.reshape(batch, 1, seq, QKV_W), qkv4, qkv16)
        mem_args = (rest, mem, mem_norm_w[l][None], w_mem_kv[l])
        pats = [_dilated(src, rb_t, batch, seq, window, dil, mem_args if dil == 1 else None)
                for (window, dil), src in zip(DIL_PATTERNS, srcs)]
        pats[0], mem_pair = pats[0]

        o_gla, w_out_bf = _gla(rest, w_gla_gate2[l], b_gla_gate[l][None], gla_norm_w[l][None], w_out[l],
                               batch, seq)

        out = _out_proj(pats, o_gla, mem_pair, rest, stat_mask, expand, expand_mem, w_out_bf, x2d,
                        norm_post_w[l][None])
        x = out.reshape(batch, seq, D_MODEL)
    return x
```

```python
import functools
import math

import numpy as np
import jax
import jax.numpy as jnp
from jax import lax
from jax.experimental import pallas as pl
from jax.experimental.pallas import tpu as pltpu

F32 = jnp.float32
BF16 = jnp.bfloat16

D_MODEL = 2048
HEAD_DIM = 64
N_HEADS_DIL = 12
DIL_PATTERNS = ((128, 1), (512, 4), (2048, 16))
DIL_BLOCK = 128
N_HEADS_GLA = 4
GLA_DK = 128
GLA_DV = 256
GLA_GATE_RANK = 16
GLA_TAU = 16.0
GLA_CHUNK = 64
N_HEADS_MEM = 4
MEM_LEN = 256
REL_BUCKETS = 32
REL_MAX_DIST = 2048
EPS = 1e-6
NEG_INF = -1e30

D_DIL = N_HEADS_DIL * HEAD_DIM
D_GLA_K = N_HEADS_GLA * GLA_DK
D_GLA_V = N_HEADS_GLA * GLA_DV
D_MEM = N_HEADS_MEM * HEAD_DIM
D_MIX = D_DIL + D_GLA_V + D_MEM

LANES = 128
F32_SUBLANES = 8
GLR_PAD = 256

QKV_W = 3 * D_DIL
COL_GQ = 0
COL_GK = COL_GQ + D_GLA_K
COL_GV = COL_GK + D_GLA_K
COL_GATE_G = COL_GV + D_GLA_V
COL_MQ = COL_GATE_G + D_GLA_V
COL_GATE_M = COL_MQ + D_MEM
COL_GLR = COL_GATE_M + D_MEM
COL_GATE_A = COL_GLR + GLR_PAD
REST_W = COL_GATE_A + D_DIL

VMEM_LIMIT = 56 * 1024 * 1024


def _cparams(sem):
    return pltpu.CompilerParams(dimension_semantics=sem, vmem_limit_bytes=VMEM_LIMIT)


def _split_hi_lo(w):
    hi = w.astype(BF16)
    return hi, (w - hi.astype(F32)).astype(BF16)


QKV_TM = 512
N_SLABS = D_DIL // LANES
REGROUP_DILS = (4, 16)


def _qkv_proj_kernel(x_ref, nw_ref, w_ref, h_ref, o_ref, x4_ref, x16_ref, slab_ref, sub_ref):
    x = x_ref[...]
    ms = jnp.mean(x * x, axis=-1, keepdims=True)
    h_ref[...] = (x * lax.rsqrt(ms + EPS) * nw_ref[...]).astype(BF16)
    for t in range(QKV_W // D_DIL):
        cols = slice(t * D_DIL, (t + 1) * D_DIL)
        res = jnp.dot(h_ref[...], w_ref[:, cols], preferred_element_type=F32)
        o_ref[:, cols] = res.astype(o_ref.dtype)
        for s in range(N_SLABS):
            slab_ref[t, s] = res[:, s * LANES:(s + 1) * LANES]
        for s in range(N_SLABS):
            lo = t * D_DIL + s * LANES
            for r in range(4):
                sub = slab_ref[t, s, pl.ds(r, QKV_TM // 4, stride=4), :]
                x4_ref[0, r, :, lo:lo + LANES] = sub.astype(x4_ref.dtype)
                sub_ref[t, s, r] = sub
            for r in range(4):
                for q in range(4):
                    x16_ref[0, 4 * q + r, :, lo:lo + LANES] = (
                        sub_ref[t, s, r, pl.ds(q, QKV_TM // 16, stride=4), :].astype(x16_ref.dtype))


def _qkv_proj(x2d, norm_w, w_qkv, batch, seq):
    m = x2d.shape[0]
    tiles_per_seq = seq // QKV_TM
    const = lambda shape: pl.BlockSpec(shape, lambda i: (0, 0), pipeline_mode=pl.Buffered(1))
    grouped = lambda dil: pl.BlockSpec((1, dil, QKV_TM // dil, QKV_W),
                                       lambda i: (i // tiles_per_seq, 0, i % tiles_per_seq, 0))
    return pl.pallas_call(
        _qkv_proj_kernel,
        grid=(m // QKV_TM,),
        in_specs=[
            pl.BlockSpec((QKV_TM, D_MODEL), lambda i: (i, 0)),
            const((1, D_MODEL)), const((D_MODEL, QKV_W)),
        ],
        out_specs=[pl.BlockSpec((QKV_TM, D_MODEL), lambda i: (i, 0)),
                   pl.BlockSpec((QKV_TM, QKV_W), lambda i: (i, 0))] + [grouped(d) for d in REGROUP_DILS],
        out_shape=[jax.ShapeDtypeStruct((m, D_MODEL), BF16), jax.ShapeDtypeStruct((m, QKV_W), BF16)]
        + [jax.ShapeDtypeStruct((batch, d, seq // d, QKV_W), BF16) for d in REGROUP_DILS],
        scratch_shapes=[pltpu.VMEM((QKV_W // D_DIL, N_SLABS, QKV_TM, LANES), F32),
                        pltpu.VMEM((QKV_W // D_DIL, N_SLABS, 4, QKV_TM // 4, LANES), F32)],
        compiler_params=_cparams(("parallel",)),
        name="qkv_proj",
    )(x2d, norm_w, w_qkv)


REST_TM = 512


def _rest_proj_kernel(h_ref, w_ref, o_ref):
    o_ref[...] = jnp.dot(h_ref[...], w_ref[...], preferred_element_type=F32).astype(o_ref.dtype)


def _rest_proj(h, w_rest):
    m = h.shape[0]
    const = lambda shape: pl.BlockSpec(shape, lambda i: (0, 0), pipeline_mode=pl.Buffered(1))
    return pl.pallas_call(
        _rest_proj_kernel,
        grid=(m // REST_TM,),
        in_specs=[pl.BlockSpec((REST_TM, D_MODEL), lambda i: (i, 0)), const((D_MODEL, REST_W))],
        out_specs=pl.BlockSpec((REST_TM, REST_W), lambda i: (i, 0)),
        out_shape=jax.ShapeDtypeStruct((m, REST_W), BF16),
        compiler_params=_cparams(("parallel",)),
        name="rest_proj",
    )(h, w_rest)


LOG2E = math.log2(math.e)


def _pair_attention(q_pair, k_pair, v_pair, bias_e, bias_o):
    rows = q_pair.shape[0]
    lo_q = lax.broadcasted_iota(jnp.int32, q_pair.shape, 1) < HEAD_DIM
    zero = jnp.zeros_like(q_pair)
    q_both = jnp.concatenate([jnp.where(lo_q, q_pair, zero), jnp.where(lo_q, zero, q_pair)], axis=0)
    s = lax.dot_general(q_both, k_pair, (((1,), (1,)), ((), ())), preferred_element_type=F32)
    s_e, s_o = s[:rows], s[rows:]
    if bias_e is not None:
        s_e = s_e + bias_e
        s_o = s_o + bias_o
    mx_e = jnp.max(s_e, axis=-1, keepdims=True)
    mx_o = jnp.max(s_o, axis=-1, keepdims=True)
    p_e = jnp.exp2(s_e - mx_e).astype(BF16)
    p_o = jnp.exp2(s_o - mx_o).astype(BF16)
    lo_v = lax.broadcasted_iota(jnp.int32, v_pair.shape, 1) < HEAD_DIM
    one = jnp.ones_like(v_pair)
    r_e = jnp.dot(p_e, jnp.where(lo_v, v_pair, one), preferred_element_type=F32)
    r_o = jnp.dot(p_o, jnp.where(lo_v, one, v_pair), preferred_element_type=F32)
    return r_e, r_o, mx_e, mx_o


def _stat_lane(head):
    return head + HEAD_DIM if head % 2 == 0 else head


STAT_DEN_SHIFT = 16


DIL_UNITS = {1: 8, 4: 4, 16: 8}


def _dil_init_bias(rb_ref, sel_ref, mrow_ref, bias_ref):
    rb_hi, rb_lo = _split_hi_lo(rb_ref[...])
    f = (jnp.dot(rb_hi, sel_ref[...], preferred_element_type=F32)
         + jnp.dot(rb_lo, sel_ref[...], preferred_element_type=F32)) * LOG2E + mrow_ref[...]
    col = lax.broadcasted_iota(jnp.int32, (DIL_BLOCK, 2 * DIL_BLOCK), 1)
    for h in range(N_HEADS_DIL):
        row = jnp.broadcast_to(f[h:h + 1, :], (DIL_BLOCK, 2 * DIL_BLOCK))
        tab = pltpu.roll(row, 0, 1, stride=1, stride_axis=0)
        bias_ref[0, h] = tab
        bias_ref[1, h] = jnp.where(col >= DIL_BLOCK, tab, NEG_INF)


def _dil_unit(q, kp, kc, vp, vc, first, bias_ref, put_num):
    lane = lax.broadcasted_iota(jnp.int32, (DIL_BLOCK, LANES), 1)
    lo = lane < HEAD_DIM
    st_c = jnp.zeros((DIL_BLOCK, LANES), F32)
    for j in range(N_HEADS_DIL // 2):
        sl = slice(j * LANES, (j + 1) * LANES)
        if kp is None:
            k_pair, v_pair = kc(sl), vc(sl)
            bias_e, bias_o = bias_ref[0, 2 * j, :, DIL_BLOCK:], bias_ref[0, 2 * j + 1, :, DIL_BLOCK:]
        else:
            k_pair = jnp.concatenate([kp(sl), kc(sl)], axis=0)
            v_pair = jnp.concatenate([vp(sl), vc(sl)], axis=0)
            bias_e, bias_o = bias_ref[first, 2 * j], bias_ref[first, 2 * j + 1]
        r_e, r_o, mx_e, mx_o = _pair_attention(q(sl), k_pair, v_pair, bias_e, bias_o)
        put_num(j, jnp.where(lo, r_e, r_o))
        l_e, l_o = _stat_lane(2 * j), _stat_lane(2 * j + 1)
        st_c = jnp.where(lane == l_e, mx_e, jnp.where(lane == l_e + STAT_DEN_SHIFT, r_e, st_c))
        st_c = jnp.where(lane == l_o, mx_o, jnp.where(lane == l_o + STAT_DEN_SHIFT, r_o, st_c))
    return st_c


def _mem_kv(mem_ref, mnw_ref, wkv_ref, kv_ref):
    m = mem_ref[0]
    ms = jnp.mean(m * m, axis=-1, keepdims=True)
    mn = (m * lax.rsqrt(ms + EPS) * mnw_ref[...]).astype(BF16)
    kv_ref[...] = jnp.dot(mn, wkv_ref[...].astype(BF16), preferred_element_type=F32).astype(BF16)


def _mem_unit(q, kv_ref, put_num):
    lane = lax.broadcasted_iota(jnp.int32, (DIL_BLOCK, LANES), 1)
    lo = lane < HEAD_DIM
    den_c = jnp.ones((DIL_BLOCK, LANES), F32)
    for j in range(N_HEADS_MEM // 2):
        sl = slice(j * LANES, (j + 1) * LANES)
        r_e, r_o, _, _ = _pair_attention(q(sl), kv_ref[:, sl], kv_ref[:, D_MEM + j * LANES:D_MEM + (j + 1) * LANES],
                                         None, None)
        put_num(j, jnp.where(lo, r_e, r_o))
        den_c = jnp.where(lane == _stat_lane(2 * j), r_e, jnp.where(lane == _stat_lane(2 * j + 1), r_o, den_c))
    return den_c


def _dil1_kernel(q_ref, kp_ref, kc_ref, vp_ref, vc_ref, rb_ref, sel_ref, mrow_ref, mq_ref, mem_ref, mnw_ref, wkv_ref,
                 o_ref, st_ref, om_ref, dm_ref, bias_ref, kv_ref):
    @pl.when((pl.program_id(0) == 0) & (pl.program_id(1) == 0))
    def _():
        _dil_init_bias(rb_ref, sel_ref, mrow_ref, bias_ref)

    @pl.when(pl.program_id(1) == 0)
    def _():
        _mem_kv(mem_ref, mnw_ref, wkv_ref, kv_ref)

    for u in range(DIL_UNITS[1]):
        rows = slice(u * DIL_BLOCK, (u + 1) * DIL_BLOCK)
        prev = slice((u - 1) * DIL_BLOCK, u * DIL_BLOCK)
        if u == 0:
            first = (pl.program_id(1) == 0).astype(jnp.int32)
            kp, vp = (lambda sl: kp_ref[:, sl]), (lambda sl: vp_ref[:, sl])
        else:
            first = 0
            kp, vp = (lambda sl, p=prev: kc_ref[p, sl]), (lambda sl, p=prev: vc_ref[p, sl])

        def put_num(j, num, rows=rows):
            o_ref[rows, j * LANES:(j + 1) * LANES] = num.astype(o_ref.dtype)

        st_ref[rows, :] = _dil_unit(lambda sl, r=rows: q_ref[r, sl], kp, lambda sl, r=rows: kc_ref[r, sl],
                                    vp, lambda sl, r=rows: vc_ref[r, sl], first, bias_ref, put_num)

        def put_mem(j, num, rows=rows):
            om_ref[rows, j * LANES:(j + 1) * LANES] = num.astype(om_ref.dtype)

        dm_ref[rows, :] = _mem_unit(lambda sl, r=rows: mq_ref[r, sl], kv_ref, put_mem)


DIL_LBLOCKS = {4: 2, 16: 2}


def _dil_kernel(*refs, dil, whole_seq):
    if whole_seq:
        q_ref, kc_ref, vc_ref, *rest = refs
        kp_ref = vp_ref = None
    else:
        q_ref, kp_ref, kc_ref, vp_ref, vc_ref, *rest = refs
    rb_ref, sel_ref, mrow_ref, o_ref, st_ref, bias_ref, acc_ref, tmp_ref = rest
    b, lb, g = pl.program_id(0), pl.program_id(1), pl.program_id(2)
    nr, nl = DIL_UNITS[dil], DIL_LBLOCKS[dil]
    gather = whole_seq and dil == 2 * F32_SUBLANES

    @pl.when((b == 0) & (lb == 0) & (g == 0))
    def _():
        _dil_init_bias(rb_ref, sel_ref, mrow_ref, bias_ref)

    for ll in range(nl):
        sub = slice(ll * DIL_BLOCK, (ll + 1) * DIL_BLOCK)
        prev = slice((ll - 1) * DIL_BLOCK, ll * DIL_BLOCK)
        first = (lb == 0).astype(jnp.int32) if ll == 0 else 0
        for u in range(nr):
            rows = pl.ds(ll * DIL_BLOCK * dil + g * nr + u, DIL_BLOCK, stride=dil)
            if ll == 0 and whole_seq:
                kp = vp = None
            elif ll == 0:
                kp, vp = (lambda sl, u=u: kp_ref[u, :, sl]), (lambda sl, u=u: vp_ref[u, :, sl])
            else:
                kp = lambda sl, u=u, p=prev: kc_ref[u, p, sl]
                vp = lambda sl, u=u, p=prev: vc_ref[u, p, sl]

            if gather:
                def put_num(j, num, ll=ll, u=u):
                    for c in range(DIL_BLOCK // dil):
                        at = pl.multiple_of((ll * (DIL_BLOCK // dil) + c) * dil * dil + (g * nr + u) * dil, dil)
                        acc_ref[j, pl.ds(at, dil), :] = num[c * dil:(c + 1) * dil]
            else:
                def put_num(j, num, rows=rows):
                    acc_ref[j, rows, :] = num

            st_ref[rows, :] = _dil_unit(lambda sl, u=u, r=sub: q_ref[u, r, sl], kp,
                                        lambda sl, u=u, r=sub: kc_ref[u, r, sl], vp,
                                        lambda sl, u=u, r=sub: vc_ref[u, r, sl], first, bias_ref, put_num)

    @pl.when(g == dil // nr - 1)
    def _():
        if gather:
            def body(mm, carry):
                for j in range(N_HEADS_DIL // 2):
                    for i0 in range(4):
                        tmp_ref[j, i0] = acc_ref[j, pl.ds(mm * dil * dil + i0, dil * 4, stride=4), :]
                    for i in range(dil):
                        o_ref[pl.ds(pl.multiple_of((mm * dil + i) * dil, dil), dil), j * LANES:(j + 1) * LANES] = (
                            tmp_ref[j, i % 4, pl.ds(i // 4, dil, stride=4), :].astype(o_ref.dtype))
                return carry
            lax.fori_loop(0, nl * DIL_BLOCK // dil, body, 0)
        else:
            for j in range(N_HEADS_DIL // 2):
                o_ref[:, j * LANES:(j + 1) * LANES] = acc_ref[j].astype(o_ref.dtype)


def _t5_bucket_np(dist):
    max_exact = REL_BUCKETS // 2
    n = np.maximum(dist, 1).astype(np.float32)
    large = max_exact + (np.log(n / np.float32(max_exact)) / np.float32(math.log(REL_MAX_DIST / max_exact))
                         * np.float32(REL_BUCKETS - max_exact)).astype(np.int32)
    large = np.minimum(large, REL_BUCKETS - 1)
    return np.where(dist < max_exact, dist, large)


def _bias_selector(window, dil):
    back = window // dil
    x = np.arange(2 * DIL_BLOCK)
    rel = DIL_BLOCK - x
    valid = (rel >= 0) & (rel <= back)
    bucket = _t5_bucket_np(np.clip(rel, 0, back) * dil)
    sel = np.zeros((LANES, 2 * DIL_BLOCK), np.float32)
    sel[bucket[valid], x[valid]] = 1.0
    mrow = np.where(valid, 0.0, NEG_INF).astype(np.float32)[None]
    return jnp.asarray(sel, BF16), jnp.asarray(mrow)


def _dilated(src, rb_t, batch, seq, window, dil, mem_args=None):
    sub_len = seq // dil
    nb = sub_len // DIL_BLOCK
    sel, mrow = _bias_selector(window, dil)
    nu = DIL_UNITS[dil]
    consts = (rb_t, sel, mrow)
    const = lambda a: pl.BlockSpec(a.shape, lambda *_: (0, 0))
    stat_shape = jax.ShapeDtypeStruct((batch * seq, LANES), F32)
    out_shape = [jax.ShapeDtypeStruct((batch * seq, D_DIL), BF16), stat_shape]
    bias_scratch = pltpu.VMEM((2, N_HEADS_DIL, DIL_BLOCK, 2 * DIL_BLOCK), F32)

    if dil == 1:
        rest, mem, mem_norm_w, w_kv = mem_args
        rows = nu * DIL_BLOCK
        steps = nb // nu
        cur = lambda col: pl.BlockSpec((None, None, rows, D_DIL), lambda b, l: (b, 0, l, col))
        prev = lambda col: pl.BlockSpec((None, None, DIL_BLOCK, D_DIL),
                                        lambda b, l: (b, 0, jnp.maximum(nu * l - 1, 0), col))
        out = lambda width, col=0: pl.BlockSpec((rows, width), lambda b, l: (b * steps + l, col // width))
        mem_specs = [out(D_MEM, COL_MQ), pl.BlockSpec((1, MEM_LEN, D_MODEL), lambda b, l: (b, 0, 0)),
                     const(mem_norm_w), const(w_kv)]
        num, stats, mem_num, mem_den = pl.pallas_call(
            _dil1_kernel,
            grid=(batch, steps),
            in_specs=[cur(0), prev(1), cur(1), prev(2), cur(2)] + [const(a) for a in consts] + mem_specs,
            out_specs=[out(D_DIL), out(LANES), out(D_MEM), out(LANES)],
            out_shape=out_shape + [jax.ShapeDtypeStruct((batch * seq, D_MEM), BF16), stat_shape],
            scratch_shapes=[bias_scratch, pltpu.VMEM((MEM_LEN, 2 * D_MEM), BF16)],
            compiler_params=_cparams(("arbitrary", "arbitrary")),
            name="dilated_d1",
        )(src, src, src, src, src, *consts, rest, mem, mem_norm_w, w_kv)
        return (num, stats), (mem_num, mem_den)

    nl = DIL_LBLOCKS[dil]
    steps = nb // nl
    rows = nl * DIL_BLOCK * dil
    cur = lambda col: pl.BlockSpec((None, nu, nl * DIL_BLOCK, D_DIL), lambda b, l, g: (b, g, l, col))
    prev = lambda col: pl.BlockSpec((None, nu, DIL_BLOCK, D_DIL),
                                    lambda b, l, g: (b, g, jnp.maximum(nl * l - 1, 0), col))
    out = lambda width: pl.BlockSpec((rows, width), lambda b, l, g: (b * steps + l, 0))
    whole_seq = steps == 1
    qkv_specs = [cur(0), cur(1), cur(2)] if whole_seq else [cur(0), prev(1), cur(1), prev(2), cur(2)]
    return pl.pallas_call(
        functools.partial(_dil_kernel, dil=dil, whole_seq=whole_seq),
        grid=(batch, steps, dil // nu),
        in_specs=qkv_specs + [const(a) for a in consts],
        out_specs=[out(D_DIL), out(LANES)],
        out_shape=out_shape,
        scratch_shapes=[bias_scratch, pltpu.VMEM((N_HEADS_DIL // 2, rows, LANES), F32),
                        pltpu.VMEM((N_HEADS_DIL // 2, 4, 4 * dil, LANES), F32)],
        compiler_params=_cparams(("arbitrary", "arbitrary", "arbitrary")),
        name=f"dilated_d{dil}",
    )(*([src] * len(qkv_specs)), *consts)


GLA_TM = 256
GLA_STEP_TILES = 2


GLA_NC = GLA_TM // GLA_CHUNK


def _gla_tile(bb, rows, w2, q_ref, k_ref, v_ref, g_ref, bg_ref, nw_ref, tri_ref, o_ref, st_ref):
    g = g_ref[bb, rows, :LANES]
    z = jnp.dot(g, w2, preferred_element_type=F32) + bg_ref[...]
    la = (jnp.minimum(z, 0.0) * (LOG2E / GLA_TAU)
          - jnp.log2(1.0 + jnp.exp2(jnp.abs(z) * -LOG2E)) * (1.0 / GLA_TAU))
    la_h = la.astype(BF16)
    la_l = (la - la_h.astype(F32)).astype(BF16)

    tri = tri_ref[...]
    b = jnp.dot(tri, la_h, preferred_element_type=F32) + jnp.dot(tri, la_l, preferred_element_type=F32)
    b_last = jnp.concatenate(
        [jnp.broadcast_to(b[(ci + 1) * GLA_CHUNK - 1:(ci + 1) * GLA_CHUNK, :], (GLA_CHUNK, D_GLA_K))
         for ci in range(GLA_NC)], axis=0)

    q = q_ref[bb, rows].astype(F32)
    k = k_ref[bb, rows].astype(F32)
    q_t = (q * jnp.exp2(b)).astype(BF16)
    k_t = (k * jnp.exp2(-b)).astype(BF16)
    k_s = k * jnp.exp2(b_last - b)

    causal = tri > 0
    t_chunk = lax.broadcasted_iota(jnp.int32, (GLA_DK, GLA_TM), 1) // GLA_CHUNK
    nw = nw_ref[...]
    for h in range(N_HEADS_GLA):
        cs = slice(h * GLA_DK, (h + 1) * GLA_DK)
        vs = slice(h * GLA_DV, (h + 1) * GLA_DV)
        v_h = v_ref[bb, rows, vs]
        a = lax.dot_general(q_t[:, cs], k_t[:, cs], (((1,), (1,)), ((), ())), preferred_element_type=F32)
        o = jnp.dot(jnp.where(causal, a.astype(BF16), jnp.zeros_like(tri)), v_h, preferred_element_type=F32)
        k_s_t = k_s[:, cs].T.astype(BF16)
        zero = jnp.zeros_like(k_s_t)
        kv_all = jnp.dot(jnp.concatenate([jnp.where(t_chunk == ci, k_s_t, zero) for ci in range(GLA_NC)], axis=0),
                         v_h, preferred_element_type=F32)
        decay_t = jnp.exp2(b_last[:, cs].T)
        st = st_ref[bb, h]
        outs = []
        for ci in range(GLA_NC):
            rs = slice(ci * GLA_CHUNK, (ci + 1) * GLA_CHUNK)
            o_c = o[rs] + jnp.dot(q_t[rs, cs], st.astype(BF16), preferred_element_type=F32)
            decay = jnp.broadcast_to(decay_t[:, ci * GLA_CHUNK:ci * GLA_CHUNK + 1], (GLA_DK, GLA_DV))
            st = decay * st + kv_all[ci * GLA_DK:(ci + 1) * GLA_DK]
            ms = jnp.mean(o_c * o_c, axis=-1, keepdims=True)
            outs.append((o_c * lax.rsqrt(ms + EPS) * nw).astype(o_ref.dtype))
        st_ref[bb, h] = st
        o_ref[bb, rows, vs] = jnp.concatenate(outs, axis=0)


def _gla_kernel(q_ref, k_ref, v_ref, g_ref, w2_ref, bg_ref, nw_ref, tri_ref, wout_ref,
                o_ref, wout_bf_ref, st_ref):

    @pl.when(pl.program_id(0) == 0)
    def _():
        st_ref[...] = jnp.zeros_like(st_ref)

    wout_bf_ref[...] = wout_ref[...].astype(wout_bf_ref.dtype)

    w2 = jnp.concatenate([w2_ref[...].astype(BF16), jnp.zeros((LANES - GLA_GATE_RANK, D_GLA_K), BF16)], axis=0)
    for tt in range(GLA_STEP_TILES):
        rows = slice(tt * GLA_TM, (tt + 1) * GLA_TM)
        for bb in range(st_ref.shape[0]):
            _gla_tile(bb, rows, w2, q_ref, k_ref, v_ref, g_ref, bg_ref, nw_ref, tri_ref, o_ref, st_ref)


def _gla(proj, w2, b_gate, norm_w, w_out, batch, seq):
    step_rows = GLA_TM * GLA_STEP_TILES
    steps = seq // step_rows
    wout_rows = w_out.shape[0] // steps
    pos = np.arange(GLA_TM)
    tri = jnp.asarray((pos[:, None] // GLA_CHUNK == pos[None, :] // GLA_CHUNK) & (pos[None, :] <= pos[:, None]),
                      BF16)
    proj3 = proj.reshape(batch, seq, REST_W)
    rows = lambda width, col: pl.BlockSpec((batch, step_rows, width), lambda i: (0, i, col // width))
    const = lambda shape: pl.BlockSpec(shape, lambda i: (0, 0))
    wout_spec = pl.BlockSpec((wout_rows, w_out.shape[1]), lambda i: (i, 0))
    out, w_out_bf = pl.pallas_call(
        _gla_kernel,
        grid=(steps,),
        in_specs=[
            rows(D_GLA_K, COL_GQ), rows(D_GLA_K, COL_GK), rows(D_GLA_V, COL_GV), rows(GLR_PAD, COL_GLR),
            const((GLA_GATE_RANK, D_GLA_K)), const((1, D_GLA_K)), const((1, GLA_DV)),
            const((GLA_TM, GLA_TM)), wout_spec,
        ],
        out_specs=[rows(D_GLA_V, 0), wout_spec],
        out_shape=[jax.ShapeDtypeStruct((batch, seq, D_GLA_V), BF16), jax.ShapeDtypeStruct(w_out.shape, BF16)],
        scratch_shapes=[pltpu.VMEM((batch, N_HEADS_GLA, GLA_DK, GLA_DV), F32)],
        compiler_params=_cparams(("arbitrary",)),
        name="gla",
    )(proj3, proj3, proj3, proj3, w2, b_gate, norm_w, tri, w_out)
    return out.reshape(batch * seq, D_GLA_V), w_out_bf


OUT_TM = 512


def _silu(g):
    h = g * 0.5
    return h * jnp.tanh(h) + h


OUT_GW = 256


def _out_kernel(n1_ref, n2_ref, n3_ref, s1_ref, s2_ref, s3_ref, og_ref, nm_ref, dm_ref, ga_ref, gg_ref, gm_ref,
                sm_ref, ex_ref, exm_ref, w_ref, x_ref, nw_ref, out_ref, mix_ref):
    keep = sm_ref[0:1, :]
    fill = sm_ref[1:2, :]
    stats = [s_ref[...] for s_ref in (s1_ref, s2_ref, s3_ref)]
    m1, m2, m3 = [st * keep for st in stats]
    d1, d2, d3 = [pltpu.roll(st, LANES - STAT_DEN_SHIFT, 1) * keep + fill for st in stats]
    mx = jnp.maximum(jnp.maximum(m1, m2), m3)
    t1, t2, t3 = jnp.exp2(m1 - mx), jnp.exp2(m2 - mx), jnp.exp2(m3 - mx)
    inv = 1.0 / (t1 * d1 + t2 * d2 + t3 * d3)
    wts = [(t * inv).astype(BF16) for t in (t1, t2, t3)]
    inv_m = (1.0 / dm_ref[...]).astype(BF16)

    def gate_chunk(lo):
        cols = slice(lo, lo + OUT_GW)
        if lo < D_DIL:
            a = jnp.zeros((OUT_TM, OUT_GW), F32)
            for wt, n_ref in zip(wts, (n1_ref, n2_ref, n3_ref)):
                a = a + jnp.dot(wt, ex_ref[:, cols], preferred_element_type=F32) * n_ref[:, cols].astype(F32)
            g = ga_ref[:, cols]
        elif lo < D_DIL + D_GLA_V:
            lc = slice(lo - D_DIL, lo - D_DIL + OUT_GW)
            a, g = og_ref[:, lc], gg_ref[:, lc]
        else:
            a = jnp.dot(inv_m, exm_ref[...], preferred_element_type=F32) * nm_ref[...].astype(F32)
            g = gm_ref[...]
        mix_ref[:, cols] = a.astype(BF16) * _silu(g)

    for lo in range(0, D_MIX, OUT_GW):
        gate_chunk(lo)
    y = jnp.dot(mix_ref[...], w_ref[...], preferred_element_type=F32)
    ms = jnp.mean(y * y, axis=-1, keepdims=True)
    out_ref[...] = x_ref[...] + y * lax.rsqrt(ms + EPS) * nw_ref[...]


def _out_proj(pats, o_gla, mem_pair, proj, stat_mask, expand, expand_mem, w_out, x2d, norm_w):
    m = x2d.shape[0]
    rows = lambda width, col=0: pl.BlockSpec((OUT_TM, width), lambda s: (s, col // width))
    const = lambda shape: pl.BlockSpec(shape, lambda s: (0, 0), pipeline_mode=pl.Buffered(1))
    nums, stats = zip(*pats)
    return pl.pallas_call(
        _out_kernel,
        grid=(m // OUT_TM,),
        in_specs=[
            *([rows(D_DIL)] * 3), *([rows(LANES)] * 3),
            rows(D_GLA_V), rows(D_MEM), rows(LANES),
            rows(D_DIL, COL_GATE_A), rows(D_GLA_V, COL_GATE_G), rows(D_MEM, COL_GATE_M),
            const((F32_SUBLANES, LANES)), const((LANES, D_DIL)), const((LANES, D_MEM)), const((D_MIX, D_MODEL)),
            rows(D_MODEL), const((1, D_MODEL)),
        ],
        out_specs=rows(D_MODEL),
        out_shape=jax.ShapeDtypeStruct((m, D_MODEL), x2d.dtype),
        scratch_shapes=[pltpu.VMEM((OUT_TM, D_MIX), BF16)],
        compiler_params=_cparams(("parallel",)),
        name="out_proj",
    )(*nums, *stats, o_gla, *mem_pair, proj, proj, proj, stat_mask, expand, expand_mem, w_out, x2d, norm_w)


D_IN_PROJ = 3 * D_DIL + 2 * D_GLA_K + D_GLA_V + GLA_GATE_RANK + D_MEM + D_MIX
SRC_GLR = 3 * D_DIL + 2 * D_GLA_K + D_GLA_V
SRC_MQ = SRC_GLR + GLA_GATE_RANK
SRC_GATE = SRC_MQ + D_MEM
SCALE_ATTN_Q = HEAD_DIM ** -0.5 * LOG2E
SCALE_GLA_Q = GLA_DK ** -0.5
QKV_SEGMENTS = (
    (0, D_DIL, 0, SCALE_ATTN_Q),
    (D_DIL, 2 * D_DIL, D_DIL, 1.0),
)
REST_SEGMENTS = (
    (QKV_W, D_GLA_K, COL_GQ, SCALE_GLA_Q),
    (QKV_W + D_GLA_K, D_GLA_K + D_GLA_V, COL_GK, 1.0),
    (SRC_GATE + D_DIL, D_GLA_V, COL_GATE_G, 1.0),
    (SRC_MQ, D_MEM, COL_MQ, SCALE_ATTN_Q),
    (SRC_GATE + D_DIL + D_GLA_V, D_MEM, COL_GATE_M, 1.0),
    (SRC_GATE, D_DIL, COL_GATE_A, 1.0),
)
WPREP_TK = 512


def _wprep_kernel(wt_ref, qkv_ref, rest_ref):
    for o_ref, segments in ((qkv_ref, QKV_SEGMENTS), (rest_ref, REST_SEGMENTS)):
        for src, width, dst, scale in segments:
            for lo in range(0, width, LANES):
                w = wt_ref[src + lo:src + lo + LANES, :].T
                if scale != 1.0:
                    w = w * scale
                o_ref[:, dst + lo:dst + lo + LANES] = w.astype(o_ref.dtype)
    glr = wt_ref[SRC_GLR:SRC_GLR + LANES, :].T
    keep = lax.broadcasted_iota(jnp.int32, glr.shape, 1) < GLA_GATE_RANK
    rest_ref[:, COL_GLR:COL_GLR + LANES] = jnp.where(keep, glr, 0.0).astype(rest_ref.dtype)
    rest_ref[:, COL_GLR + LANES:COL_GLR + GLR_PAD] = jnp.zeros((WPREP_TK, GLR_PAD - LANES), rest_ref.dtype)


def _regroup_in_weights(w_in):
    return pl.pallas_call(
        _wprep_kernel,
        grid=(D_MODEL // WPREP_TK,),
        in_specs=[pl.BlockSpec((D_IN_PROJ, WPREP_TK), lambda i: (0, i))],
        out_specs=[pl.BlockSpec((WPREP_TK, QKV_W), lambda i: (i, 0)),
                   pl.BlockSpec((WPREP_TK, REST_W), lambda i: (i, 0))],
        out_shape=[jax.ShapeDtypeStruct((D_MODEL, QKV_W), BF16), jax.ShapeDtypeStruct((D_MODEL, REST_W), BF16)],
        compiler_params=_cparams(("parallel",)),
        name="weight_regroup",
    )(w_in.T)


def kernel(x, mem, norm_pre_w, w_in, rel_bias, w_gla_gate2, b_gla_gate, gla_norm_w, mem_norm_w, w_mem_kv,
           w_out, norm_post_w):
    batch, seq, _ = x.shape
    depth = w_in.shape[0]
    expand_np = np.zeros((LANES, D_DIL), np.float32)
    for h in range(N_HEADS_DIL):
        expand_np[_stat_lane(h), h * HEAD_DIM:(h + 1) * HEAD_DIM] = 1.0
    expand = jnp.asarray(expand_np, BF16)
    expand_mem = jnp.asarray(expand_np[:, :D_MEM], BF16)
    stat_mask_np = np.zeros((F32_SUBLANES, LANES), np.float32)
    stat_mask_np[0, [_stat_lane(h) for h in range(N_HEADS_DIL)]] = 1.0
    stat_mask_np[1] = 1.0 - stat_mask_np[0]
    stat_mask = jnp.asarray(stat_mask_np)
    rb_t = jnp.zeros((2 * F32_SUBLANES, LANES), F32).at[:N_HEADS_DIL, :REL_BUCKETS].set(rel_bias.astype(F32).T)
    for l in range(depth):
        x2d = x.reshape(batch * seq, D_MODEL)
        w_qkv, w_rest = _regroup_in_weights(w_in[l])
        h, qkv1, qkv4, qkv16 = _qkv_proj(x2d, norm_pre_w[l][None], w_qkv, batch, seq)
        rest = _rest_proj(h, w_rest)

        srcs = (qkv1.reshape(batch, 1, seq, QKV_W), qkv4, qkv16)
        mem_args = (rest, mem, mem_norm_w[l][None], w_mem_kv[l])
        pats = [_dilated(src, rb_t, batch, seq, window, dil, mem_args if dil == 1 else None)
                for (window, dil), src in zip(DIL_PATTERNS, srcs)]
        pats[0], mem_pair = pats[0]

        o_gla, w_out_bf = _gla(rest, w_gla_gate2[l], b_gla_gate[l][None], gla_norm_w[l][None], w_out[l],
                               batch, seq)

        out = _out_proj(pats, o_gla, mem_pair, rest, stat_mask, expand, expand_mem, w_out_bf, x2d,
                        norm_post_w[l][None])
        x = out.reshape(batch, seq, D_MODEL)
    return x
```

```python
import functools
import math

import numpy as np
import jax
import jax.numpy as jnp
from jax import lax
from jax.experimental import pallas as pl
from jax.experimental.pallas import tpu as pltpu

F32 = jnp.float32
BF16 = jnp.bfloat16

D_MODEL = 2048
HEAD_DIM = 64
N_HEADS_DIL = 12
DIL_PATTERNS = ((128, 1), (512, 4), (2048, 16))
DIL_BLOCK = 128
N_HEADS_GLA = 4
GLA_DK = 128
GLA_DV = 256
GLA_GATE_RANK = 16
GLA_TAU = 16.0
GLA_CHUNK = 64
N_HEADS_MEM = 4
MEM_LEN = 256
REL_BUCKETS = 32
REL_MAX_DIST = 2048
EPS = 1e-6
NEG_INF = -1e30

D_DIL = N_HEADS_DIL * HEAD_DIM
D_GLA_K = N_HEADS_GLA * GLA_DK
D_GLA_V = N_HEADS_GLA * GLA_DV
D_MEM = N_HEADS_MEM * HEAD_DIM
D_MIX = D_DIL + D_GLA_V + D_MEM

LANES = 128
F32_SUBLANES = 8
GLR_PAD = 256

QKV_W = 3 * D_DIL
COL_GQ = 0
COL_GK = COL_GQ + D_GLA_K
COL_GV = COL_GK + D_GLA_K
COL_GATE_G = COL_GV + D_GLA_V
COL_MQ = COL_GATE_G + D_GLA_V
COL_GATE_M = COL_MQ + D_MEM
COL_GLR = COL_GATE_M + D_MEM
COL_GATE_A = COL_GLR + GLR_PAD
REST_W = COL_GATE_A + D_DIL

VMEM_LIMIT = 56 * 1024 * 1024


def _cparams(sem):
    return pltpu.CompilerParams(dimension_semantics=sem, vmem_limit_bytes=VMEM_LIMIT)


def _split_hi_lo(w):
    hi = w.astype(BF16)
    return hi, (w - hi.astype(F32)).astype(BF16)


QKV_TM = 512
N_SLABS = D_DIL // LANES
REGROUP_DILS = (4, 16)


def _qkv_proj_kernel(x_ref, nw_ref, w_ref, h_ref, o_ref, x4_ref, x16_ref, slab_ref, sub_ref):
    x = x_ref[...]
    ms = jnp.mean(x * x, axis=-1, keepdims=True)
    h_ref[...] = (x * lax.rsqrt(ms + EPS) * nw_ref[...]).astype(BF16)
    for t in range(QKV_W // D_DIL):
        cols = slice(t * D_DIL, (t + 1) * D_DIL)
        res = jnp.dot(h_ref[...], w_ref[:, cols], preferred_element_type=F32)
        o_ref[:, cols] = res.astype(o_ref.dtype)
        for s in range(N_SLABS):
            slab_ref[t, s] = res[:, s * LANES:(s + 1) * LANES]
        for s in range(N_SLABS):
            lo = t * D_DIL + s * LANES
            for r in range(4):
                sub = slab_ref[t, s, pl.ds(r, QKV_TM // 4, stride=4), :]
                x4_ref[0, r, :, lo:lo + LANES] = sub.astype(x4_ref.dtype)
                sub_ref[t, s, r] = sub
            for r in range(4):
                for q in range(4):
                    x16_ref[0, 4 * q + r, :, lo:lo + LANES] = (
                        sub_ref[t, s, r, pl.ds(q, QKV_TM // 16, stride=4), :].astype(x16_ref.dtype))


def _qkv_proj(x2d, norm_w, w_qkv, batch, seq):
    m = x2d.shape[0]
    tiles_per_seq = seq // QKV_TM
    const = lambda shape: pl.BlockSpec(shape, lambda i: (0, 0), pipeline_mode=pl.Buffered(1))
    grouped = lambda dil: pl.BlockSpec((1, dil, QKV_TM // dil, QKV_W),
                                       lambda i: (i // tiles_per_seq, 0, i % tiles_per_seq, 0))
    return pl.pallas_call(
        _qkv_proj_kernel,
        grid=(m // QKV_TM,),
        in_specs=[
            pl.BlockSpec((QKV_TM, D_MODEL), lambda i: (i, 0)),
            const((1, D_MODEL)), const((D_MODEL, QKV_W)),
        ],
        out_specs=[pl.BlockSpec((QKV_TM, D_MODEL), lambda i: (i, 0)),
                   pl.BlockSpec((QKV_TM, QKV_W), lambda i: (i, 0))] + [grouped(d) for d in REGROUP_DILS],
        out_shape=[jax.ShapeDtypeStruct((m, D_MODEL), BF16), jax.ShapeDtypeStruct((m, QKV_W), BF16)]
        + [jax.ShapeDtypeStruct((batch, d, seq // d, QKV_W), BF16) for d in REGROUP_DILS],
        scratch_shapes=[pltpu.VMEM((QKV_W // D_DIL, N_SLABS, QKV_TM, LANES), F32),
                        pltpu.VMEM((QKV_W // D_DIL, N_SLABS, 4, QKV_TM // 4, LANES), F32)],
        compiler_params=_cparams(("parallel",)),
        name="qkv_proj",
    )(x2d, norm_w, w_qkv)


REST_TM = 512


def _rest_proj_kernel(h_ref, w_ref, o_ref):
    o_ref[...] = jnp.dot(h_ref[...], w_ref[...], preferred_element_type=F32).astype(o_ref.dtype)


def _rest_proj(h, w_rest):
    m = h.shape[0]
    const = lambda shape: pl.BlockSpec(shape, lambda i: (0, 0), pipeline_mode=pl.Buffered(1))
    return pl.pallas_call(
        _rest_proj_kernel,
        grid=(m // REST_TM,),
        in_specs=[pl.BlockSpec((REST_TM, D_MODEL), lambda i: (i, 0)), const((D_MODEL, REST_W))],
        out_specs=pl.BlockSpec((REST_TM, REST_W), lambda i: (i, 0)),
        out_shape=jax.ShapeDtypeStruct((m, REST_W), BF16),
        compiler_params=_cparams(("parallel",)),
        name="rest_proj",
    )(h, w_rest)


LOG2E = math.log2(math.e)


def _pair_attention(q_pair, k_pair, v_pair, bias_e, bias_o):
    rows = q_pair.shape[0]
    lo_q = lax.broadcasted_iota(jnp.int32, q_pair.shape, 1) < HEAD_DIM
    zero = jnp.zeros_like(q_pair)
    q_both = jnp.concatenate([jnp.where(lo_q, q_pair, zero), jnp.where(lo_q, zero, q_pair)], axis=0)
    s = lax.dot_general(q_both, k_pair, (((1,), (1,)), ((), ())), preferred_element_type=F32)
    s_e, s_o = s[:rows], s[rows:]
    if bias_e is not None:
        s_e = s_e + bias_e
        s_o = s_o + bias_o
    mx_e = jnp.max(s_e, axis=-1, keepdims=True)
    mx_o = jnp.max(s_o, axis=-1, keepdims=True)
    p_e = jnp.exp2(s_e - mx_e).astype(BF16)
    p_o = jnp.exp2(s_o - mx_o).astype(BF16)
    lo_v = lax.broadcasted_iota(jnp.int32, v_pair.shape, 1) < HEAD_DIM
    one = jnp.ones_like(v_pair)
    r_e = jnp.dot(p_e, jnp.where(lo_v, v_pair, one), preferred_element_type=F32)
    r_o = jnp.dot(p_o, jnp.where(lo_v, one, v_pair), preferred_element_type=F32)
    return r_e, r_o, mx_e, mx_o


def _stat_lane(head):
    return head + HEAD_DIM if head % 2 == 0 else head


STAT_DEN_SHIFT = 16


DIL_UNITS = {1: 8, 4: 4, 16: 8}


def _dil_init_bias(rb_ref, sel_ref, mrow_ref, bias_ref):
    rb_hi, rb_lo = _split_hi_lo(rb_ref[...])
    f = (jnp.dot(rb_hi, sel_ref[...], preferred_element_type=F32)
         + jnp.dot(rb_lo, sel_ref[...], preferred_element_type=F32)) * LOG2E + mrow_ref[...]
    col = lax.broadcasted_iota(jnp.int32, (DIL_BLOCK, 2 * DIL_BLOCK), 1)
    for h in range(N_HEADS_DIL):
        row = jnp.broadcast_to(f[h:h + 1, :], (DIL_BLOCK, 2 * DIL_BLOCK))
        tab = pltpu.roll(row, 0, 1, stride=1, stride_axis=0)
        bias_ref[0, h] = tab
        bias_ref[1, h] = jnp.where(col >= DIL_BLOCK, tab, NEG_INF)


def _dil_unit(q, kp, kc, vp, vc, first, bias_ref, put_num):
    lane = lax.broadcasted_iota(jnp.int32, (DIL_BLOCK, LANES), 1)
    lo = lane < HEAD_DIM
    st_c = jnp.zeros((DIL_BLOCK, LANES), F32)
    for j in range(N_HEADS_DIL // 2):
        sl = slice(j * LANES, (j + 1) * LANES)
        if kp is None:
            k_pair, v_pair = kc(sl), vc(sl)
            bias_e, bias_o = bias_ref[0, 2 * j, :, DIL_BLOCK:], bias_ref[0, 2 * j + 1, :, DIL_BLOCK:]
        else:
            k_pair = jnp.concatenate([kp(sl), kc(sl)], axis=0)
            v_pair = jnp.concatenate([vp(sl), vc(sl)], axis=0)
            bias_e, bias_o = bias_ref[first, 2 * j], bias_ref[first, 2 * j + 1]
        r_e, r_o, mx_e, mx_o = _pair_attention(q(sl), k_pair, v_pair, bias_e, bias_o)
        put_num(j, jnp.where(lo, r_e, r_o))
        l_e, l_o = _stat_lane(2 * j), _stat_lane(2 * j + 1)
        st_c = jnp.where(lane == l_e, mx_e, jnp.where(lane == l_e + STAT_DEN_SHIFT, r_e, st_c))
        st_c = jnp.where(lane == l_o, mx_o, jnp.where(lane == l_o + STAT_DEN_SHIFT, r_o, st_c))
    return st_c


def _mem_kv(mem_ref, mnw_ref, wkv_ref, kv_ref):
    m = mem_ref[0]
    ms = jnp.mean(m * m, axis=-1, keepdims=True)
    mn = (m * lax.rsqrt(ms + EPS) * mnw_ref[...]).astype(BF16)
    kv_ref[...] = jnp.dot(mn, wkv_ref[...].astype(BF16), preferred_element_type=F32).astype(BF16)


def _mem_unit(q, kv_ref, put_num):
    lane = lax.broadcasted_iota(jnp.int32, (DIL_BLOCK, LANES), 1)
    lo = lane < HEAD_DIM
    den_c = jnp.ones((DIL_BLOCK, LANES), F32)
    for j in range(N_HEADS_MEM // 2):
        sl = slice(j * LANES, (j + 1) * LANES)
        r_e, r_o, _, _ = _pair_attention(q(sl), kv_ref[:, sl], kv_ref[:, D_MEM + j * LANES:D_MEM + (j + 1) * LANES],
                                         None, None)
        put_num(j, jnp.where(lo, r_e, r_o))
        den_c = jnp.where(lane == _stat_lane(2 * j), r_e, jnp.where(lane == _stat_lane(2 * j + 1), r_o, den_c))
    return den_c


def _dil1_kernel(q_ref, kp_ref, kc_ref, vp_ref, vc_ref, rb_ref, sel_ref, mrow_ref, mq_ref, mem_ref, mnw_ref, wkv_ref,
                 o_ref, st_ref, om_ref, dm_ref, bias_ref, kv_ref):
    @pl.when((pl.program_id(0) == 0) & (pl.program_id(1) == 0))
    def _():
        _dil_init_bias(rb_ref, sel_ref, mrow_ref, bias_ref)

    @pl.when(pl.program_id(1) == 0)
    def _():
        _mem_kv(mem_ref, mnw_ref, wkv_ref, kv_ref)

    for u in range(DIL_UNITS[1]):
        rows = slice(u * DIL_BLOCK, (u + 1) * DIL_BLOCK)
        prev = slice((u - 1) * DIL_BLOCK, u * DIL_BLOCK)
        if u == 0:
            first = (pl.program_id(1) == 0).astype(jnp.int32)
            kp, vp = (lambda sl: kp_ref[:, sl]), (lambda sl: vp_ref[:, sl])
        else:
            first = 0
            kp, vp = (lambda sl, p=prev: kc_ref[p, sl]), (lambda sl, p=prev: vc_ref[p, sl])

        def put_num(j, num, rows=rows):
            o_ref[rows, j * LANES:(j + 1) * LANES] = num.astype(o_ref.dtype)

        st_ref[rows, :] = _dil_unit(lambda sl, r=rows: q_ref[r, sl], kp, lambda sl, r=rows: kc_ref[r, sl],
                                    vp, lambda sl, r=rows: vc_ref[r, sl], first, bias_ref, put_num)

        def put_mem(j, num, rows=rows):
            om_ref[rows, j * LANES:(j + 1) * LANES] = num.astype(om_ref.dtype)

        dm_ref[rows, :] = _mem_unit(lambda sl, r=rows: mq_ref[r, sl], kv_ref, put_mem)


DIL_LBLOCKS = {4: 2, 16: 2}


def _dil_kernel(*refs, dil, whole_seq):
    if whole_seq:
        q_ref, kc_ref, vc_ref, *rest = refs
        kp_ref = vp_ref = None
    else:
        q_ref, kp_ref, kc_ref, vp_ref, vc_ref, *rest = refs
    rb_ref, sel_ref, mrow_ref, o_ref, st_ref, bias_ref, acc_ref, tmp_ref = rest
    b, lb, g = pl.program_id(0), pl.program_id(1), pl.program_id(2)
    nr, nl = DIL_UNITS[dil], DIL_LBLOCKS[dil]
    gather = whole_seq and dil == 2 * F32_SUBLANES
    n_pairs = N_HEADS_DIL // 2

    @pl.when((b == 0) & (lb == 0) & (g == 0))
    def _():
        _dil_init_bias(rb_ref, sel_ref, mrow_ref, bias_ref)

    for ll in range(nl):
        sub = slice(ll * DIL_BLOCK, (ll + 1) * DIL_BLOCK)
        prev = slice((ll - 1) * DIL_BLOCK, ll * DIL_BLOCK)
        first = (lb == 0).astype(jnp.int32) if ll == 0 else 0
        for u in range(nr):
            rows = pl.ds(ll * DIL_BLOCK * dil + g * nr + u, DIL_BLOCK, stride=dil)
            if ll == 0 and whole_seq:
                kp = vp = None
            elif ll == 0:
                kp, vp = (lambda sl, u=u: kp_ref[u, :, sl]), (lambda sl, u=u: vp_ref[u, :, sl])
            else:
                kp = lambda sl, u=u, p=prev: kc_ref[u, p, sl]
                vp = lambda sl, u=u, p=prev: vc_ref[u, p, sl]

            if gather:
                def put_num(j, num, ll=ll, u=u):
                    for c in range(DIL_BLOCK // dil):
                        at = pl.multiple_of((ll * (DIL_BLOCK // dil) + c) * dil * dil + (g * nr + u) * dil, dil)
                        acc_ref[j, pl.ds(at, dil), :] = num[c * dil:(c + 1) * dil]
            else:
                def put_num(j, num, rows=rows):
                    acc_ref[j, rows, :] = num

            stats = _dil_unit(lambda sl, u=u, r=sub: q_ref[u, r, sl], kp,
                              lambda sl, u=u, r=sub: kc_ref[u, r, sl], vp,
                              lambda sl, u=u, r=sub: vc_ref[u, r, sl], first, bias_ref, put_num)
            if gather:
                put_num(n_pairs, stats)
            else:
                st_ref[rows, :] = stats

    @pl.when(g == dil // nr - 1)
    def _():
        if gather:
            def body(mm, carry):
                for j in range(n_pairs + 1):
                    for i0 in range(4):
                        tmp_ref[j, i0] = acc_ref[j, pl.ds(mm * dil * dil + i0, dil * 4, stride=4), :]
                    for i in range(dil):
                        group = pl.ds(pl.multiple_of((mm * dil + i) * dil, dil), dil)
                        val = tmp_ref[j, i % 4, pl.ds(i // 4, dil, stride=4), :]
                        if j < n_pairs:
                            o_ref[group, j * LANES:(j + 1) * LANES] = val.astype(o_ref.dtype)
                        else:
                            st_ref[group, :] = val
                return carry
            lax.fori_loop(0, nl * DIL_BLOCK // dil, body, 0)
        else:
            for j in range(N_HEADS_DIL // 2):
                o_ref[:, j * LANES:(j + 1) * LANES] = acc_ref[j].astype(o_ref.dtype)


def _t5_bucket_np(dist):
    max_exact = REL_BUCKETS // 2
    n = np.maximum(dist, 1).astype(np.float32)
    large = max_exact + (np.log(n / np.float32(max_exact)) / np.float32(math.log(REL_MAX_DIST / max_exact))
                         * np.float32(REL_BUCKETS - max_exact)).astype(np.int32)
    large = np.minimum(large, REL_BUCKETS - 1)
    return np.where(dist < max_exact, dist, large)


def _bias_selector(window, dil):
    back = window // dil
    x = np.arange(2 * DIL_BLOCK)
    rel = DIL_BLOCK - x
    valid = (rel >= 0) & (rel <= back)
    bucket = _t5_bucket_np(np.clip(rel, 0, back) * dil)
    sel = np.zeros((LANES, 2 * DIL_BLOCK), np.float32)
    sel[bucket[valid], x[valid]] = 1.0
    mrow = np.where(valid, 0.0, NEG_INF).astype(np.float32)[None]
    return jnp.asarray(sel, BF16), jnp.asarray(mrow)


def _dilated(src, rb_t, batch, seq, window, dil, mem_args=None):
    sub_len = seq // dil
    nb = sub_len // DIL_BLOCK
    sel, mrow = _bias_selector(window, dil)
    nu = DIL_UNITS[dil]
    consts = (rb_t, sel, mrow)
    const = lambda a: pl.BlockSpec(a.shape, lambda *_: (0, 0))
    stat_shape = jax.ShapeDtypeStruct((batch * seq, LANES), F32)
    out_shape = [jax.ShapeDtypeStruct((batch * seq, D_DIL), BF16), stat_shape]
    bias_scratch = pltpu.VMEM((2, N_HEADS_DIL, DIL_BLOCK, 2 * DIL_BLOCK), F32)

    if dil == 1:
        rest, mem, mem_norm_w, w_kv = mem_args
        rows = nu * DIL_BLOCK
        steps = nb // nu
        cur = lambda col: pl.BlockSpec((None, None, rows, D_DIL), lambda b, l: (b, 0, l, col))
        prev = lambda col: pl.BlockSpec((None, None, DIL_BLOCK, D_DIL),
                                        lambda b, l: (b, 0, jnp.maximum(nu * l - 1, 0), col))
        out = lambda width, col=0: pl.BlockSpec((rows, width), lambda b, l: (b * steps + l, col // width))
        mem_specs = [out(D_MEM, COL_MQ), pl.BlockSpec((1, MEM_LEN, D_MODEL), lambda b, l: (b, 0, 0)),
                     const(mem_norm_w), const(w_kv)]
        num, stats, mem_num, mem_den = pl.pallas_call(
            _dil1_kernel,
            grid=(batch, steps),
            in_specs=[cur(0), prev(1), cur(1), prev(2), cur(2)] + [const(a) for a in consts] + mem_specs,
            out_specs=[out(D_DIL), out(LANES), out(D_MEM), out(LANES)],
            out_shape=out_shape + [jax.ShapeDtypeStruct((batch * seq, D_MEM), BF16), stat_shape],
            scratch_shapes=[bias_scratch, pltpu.VMEM((MEM_LEN, 2 * D_MEM), BF16)],
            compiler_params=_cparams(("arbitrary", "arbitrary")),
            name="dilated_d1",
        )(src, src, src, src, src, *consts, rest, mem, mem_norm_w, w_kv)
        return (num, stats), (mem_num, mem_den)

    nl = DIL_LBLOCKS[dil]
    steps = nb // nl
    rows = nl * DIL_BLOCK * dil
    cur = lambda col: pl.BlockSpec((None, nu, nl * DIL_BLOCK, D_DIL), lambda b, l, g: (b, g, l, col))
    prev = lambda col: pl.BlockSpec((None, nu, DIL_BLOCK, D_DIL),
                                    lambda b, l, g: (b, g, jnp.maximum(nl * l - 1, 0), col))
    out = lambda width: pl.BlockSpec((rows, width), lambda b, l, g: (b * steps + l, 0))
    whole_seq = steps == 1
    qkv_specs = [cur(0), cur(1), cur(2)] if whole_seq else [cur(0), prev(1), cur(1), prev(2), cur(2)]
    return pl.pallas_call(
        functools.partial(_dil_kernel, dil=dil, whole_seq=whole_seq),
        grid=(batch, steps, dil // nu),
        in_specs=qkv_specs + [const(a) for a in consts],
        out_specs=[out(D_DIL), out(LANES)],
        out_shape=out_shape,
        scratch_shapes=[bias_scratch, pltpu.VMEM((N_HEADS_DIL // 2 + 1, rows, LANES), F32),
                        pltpu.VMEM((N_HEADS_DIL // 2 + 1, 4, 4 * dil, LANES), F32)],
        compiler_params=_cparams(("arbitrary", "arbitrary", "arbitrary")),
        name=f"dilated_d{dil}",
    )(*([src] * len(qkv_specs)), *consts)


GLA_TM = 256
GLA_STEP_TILES = 2


GLA_NC = GLA_TM // GLA_CHUNK


def _gla_tile(bb, rows, w2, q_ref, k_ref, v_ref, g_ref, bg_ref, nw_ref, tri_ref, o_ref, st_ref):
    g = g_ref[bb, rows, :LANES]
    z = jnp.dot(g, w2, preferred_element_type=F32) + bg_ref[...]
    la = (jnp.minimum(z, 0.0) * (LOG2E / GLA_TAU)
          - jnp.log2(1.0 + jnp.exp2(jnp.abs(z) * -LOG2E)) * (1.0 / GLA_TAU))
    la_h = la.astype(BF16)
    la_l = (la - la_h.astype(F32)).astype(BF16)

    tri = tri_ref[...]
    b = jnp.dot(tri, la_h, preferred_element_type=F32) + jnp.dot(tri, la_l, preferred_element_type=F32)
    b_last = jnp.concatenate(
        [jnp.broadcast_to(b[(ci + 1) * GLA_CHUNK - 1:(ci + 1) * GLA_CHUNK, :], (GLA_CHUNK, D_GLA_K))
         for ci in range(GLA_NC)], axis=0)

    q = q_ref[bb, rows].astype(F32)
    k = k_ref[bb, rows].astype(F32)
    q_t = (q * jnp.exp2(b)).astype(BF16)
    k_t = (k * jnp.exp2(-b)).astype(BF16)
    k_s = k * jnp.exp2(b_last - b)

    causal = tri > 0
    t_chunk = lax.broadcasted_iota(jnp.int32, (GLA_DK, GLA_TM), 1) // GLA_CHUNK
    nw = nw_ref[...]
    for h in range(N_HEADS_GLA):
        cs = slice(h * GLA_DK, (h + 1) * GLA_DK)
        vs = slice(h * GLA_DV, (h + 1) * GLA_DV)
        v_h = v_ref[bb, rows, vs]
        a = lax.dot_general(q_t[:, cs], k_t[:, cs], (((1,), (1,)), ((), ())), preferred_element_type=F32)
        o = jnp.dot(jnp.where(causal, a.astype(BF16), jnp.zeros_like(tri)), v_h, preferred_element_type=F32)
        k_s_t = k_s[:, cs].T.astype(BF16)
        zero = jnp.zeros_like(k_s_t)
        kv_all = jnp.dot(jnp.concatenate([jnp.where(t_chunk == ci, k_s_t, zero) for ci in range(GLA_NC)], axis=0),
                         v_h, preferred_element_type=F32)
        decay_t = jnp.exp2(b_last[:, cs].T)
        st = st_ref[bb, h]
        outs = []
        for ci in range(GLA_NC):
            rs = slice(ci * GLA_CHUNK, (ci + 1) * GLA_CHUNK)
            o_c = o[rs] + jnp.dot(q_t[rs, cs], st.astype(BF16), preferred_element_type=F32)
            decay = jnp.broadcast_to(decay_t[:, ci * GLA_CHUNK:ci * GLA_CHUNK + 1], (GLA_DK, GLA_DV))
            st = decay * st + kv_all[ci * GLA_DK:(ci + 1) * GLA_DK]
            ms = jnp.mean(o_c * o_c, axis=-1, keepdims=True)
            outs.append((o_c * lax.rsqrt(ms + EPS) * nw).astype(o_ref.dtype))
        st_ref[bb, h] = st
        o_ref[bb, rows, vs] = jnp.concatenate(outs, axis=0)


def _gla_kernel(q_ref, k_ref, v_ref, g_ref, w2_ref, bg_ref, nw_ref, tri_ref, wout_ref,
                o_ref, wout_bf_ref, st_ref):

    @pl.when(pl.program_id(0) == 0)
    def _():
        st_ref[...] = jnp.zeros_like(st_ref)

    wout_bf_ref[...] = wout_ref[...].astype(wout_bf_ref.dtype)

    w2 = jnp.concatenate([w2_ref[...].astype(BF16), jnp.zeros((LANES - GLA_GATE_RANK, D_GLA_K), BF16)], axis=0)
    for tt in range(GLA_STEP_TILES):
        rows = slice(tt * GLA_TM, (tt + 1) * GLA_TM)
        for bb in range(st_ref.shape[0]):
            _gla_tile(bb, rows, w2, q_ref, k_ref, v_ref, g_ref, bg_ref, nw_ref, tri_ref, o_ref, st_ref)


def _gla(proj, w2, b_gate, norm_w, w_out, batch, seq):
    step_rows = GLA_TM * GLA_STEP_TILES
    steps = seq // step_rows
    wout_rows = w_out.shape[0] // steps
    pos = np.arange(GLA_TM)
    tri = jnp.asarray((pos[:, None] // GLA_CHUNK == pos[None, :] // GLA_CHUNK) & (pos[None, :] <= pos[:, None]),
                      BF16)
    proj3 = proj.reshape(batch, seq, REST_W)
    rows = lambda width, col: pl.BlockSpec((batch, step_rows, width), lambda i: (0, i, col // width))
    const = lambda shape: pl.BlockSpec(shape, lambda i: (0, 0))
    wout_spec = pl.BlockSpec((wout_rows, w_out.shape[1]), lambda i: (i, 0))
    out, w_out_bf = pl.pallas_call(
        _gla_kernel,
        grid=(steps,),
        in_specs=[
            rows(D_GLA_K, COL_GQ), rows(D_GLA_K, COL_GK), rows(D_GLA_V, COL_GV), rows(GLR_PAD, COL_GLR),
            const((GLA_GATE_RANK, D_GLA_K)), const((1, D_GLA_K)), const((1, GLA_DV)),
            const((GLA_TM, GLA_TM)), wout_spec,
        ],
        out_specs=[rows(D_GLA_V, 0), wout_spec],
        out_shape=[jax.ShapeDtypeStruct((batch, seq, D_GLA_V), BF16), jax.ShapeDtypeStruct(w_out.shape, BF16)],
        scratch_shapes=[pltpu.VMEM((batch, N_HEADS_GLA, GLA_DK, GLA_DV), F32)],
        compiler_params=_cparams(("arbitrary",)),
        name="gla",
    )(proj3, proj3, proj3, proj3, w2, b_gate, norm_w, tri, w_out)
    return out.reshape(batch * seq, D_GLA_V), w_out_bf


OUT_TM = 512


def _silu(g):
    h = g * 0.5
    return h * jnp.tanh(h) + h


OUT_GW = 256


def _out_kernel(n1_ref, n2_ref, n3_ref, s1_ref, s2_ref, s3_ref, og_ref, nm_ref, dm_ref, ga_ref, gg_ref, gm_ref,
                sm_ref, ex_ref, exm_ref, w_ref, x_ref, nw_ref, out_ref, mix_ref):
    keep = sm_ref[0:1, :]
    fill = sm_ref[1:2, :]
    stats = [s_ref[...] for s_ref in (s1_ref, s2_ref, s3_ref)]
    m1, m2, m3 = [st * keep for st in stats]
    d1, d2, d3 = [pltpu.roll(st, LANES - STAT_DEN_SHIFT, 1) * keep + fill for st in stats]
    mx = jnp.maximum(jnp.maximum(m1, m2), m3)
    t1, t2, t3 = jnp.exp2(m1 - mx), jnp.exp2(m2 - mx), jnp.exp2(m3 - mx)
    inv = 1.0 / (t1 * d1 + t2 * d2 + t3 * d3)
    wts = [(t * inv).astype(BF16) for t in (t1, t2, t3)]
    inv_m = (1.0 / dm_ref[...]).astype(BF16)

    def gate_chunk(lo):
        cols = slice(lo, lo + OUT_GW)
        if lo < D_DIL:
            a = jnp.zeros((OUT_TM, OUT_GW), F32)
            for wt, n_ref in zip(wts, (n1_ref, n2_ref, n3_ref)):
                a = a + jnp.dot(wt, ex_ref[:, cols], preferred_element_type=F32) * n_ref[:, cols].astype(F32)
            g = ga_ref[:, cols]
        elif lo < D_DIL + D_GLA_V:
            lc = slice(lo - D_DIL, lo - D_DIL + OUT_GW)
            a, g = og_ref[:, lc], gg_ref[:, lc]
        else:
            a = jnp.dot(inv_m, exm_ref[...], preferred_element_type=F32) * nm_ref[...].astype(F32)
            g = gm_ref[...]
        mix_ref[:, cols] = a.astype(BF16) * _silu(g)

    for lo in range(0, D_MIX, OUT_GW):
        gate_chunk(lo)
    y = jnp.dot(mix_ref[...], w_ref[...], preferred_element_type=F32)
    ms = jnp.mean(y * y, axis=-1, keepdims=True)
    out_ref[...] = x_ref[...] + y * lax.rsqrt(ms + EPS) * nw_ref[...]


def _out_proj(pats, o_gla, mem_pair, proj, stat_mask, expand, expand_mem, w_out, x2d, norm_w):
    m = x2d.shape[0]
    rows = lambda width, col=0: pl.BlockSpec((OUT_TM, width), lambda s: (s, col // width))
    const = lambda shape: pl.BlockSpec(shape, lambda s: (0, 0), pipeline_mode=pl.Buffered(1))
    nums, stats = zip(*pats)
    return pl.pallas_call(
        _out_kernel,
        grid=(m // OUT_TM,),
        in_specs=[
            *([rows(D_DIL)] * 3), *([rows(LANES)] * 3),
            rows(D_GLA_V), rows(D_MEM), rows(LANES),
            rows(D_DIL, COL_GATE_A), rows(D_GLA_V, COL_GATE_G), rows(D_MEM, COL_GATE_M),
            const((F32_SUBLANES, LANES)), const((LANES, D_DIL)), const((LANES, D_MEM)), const((D_MIX, D_MODEL)),
            rows(D_MODEL), const((1, D_MODEL)),
        ],
        out_specs=rows(D_MODEL),
        out_shape=jax.ShapeDtypeStruct((m, D_MODEL), x2d.dtype),
        scratch_shapes=[pltpu.VMEM((OUT_TM, D_MIX), BF16)],
        compiler_params=_cparams(("parallel",)),
        name="out_proj",
    )(*nums, *stats, o_gla, *mem_pair, proj, proj, proj, stat_mask, expand, expand_mem, w_out, x2d, norm_w)


D_IN_PROJ = 3 * D_DIL + 2 * D_GLA_K + D_GLA_V + GLA_GATE_RANK + D_MEM + D_MIX
SRC_GLR = 3 * D_DIL + 2 * D_GLA_K + D_GLA_V
SRC_MQ = SRC_GLR + GLA_GATE_RANK
SRC_GATE = SRC_MQ + D_MEM
SCALE_ATTN_Q = HEAD_DIM ** -0.5 * LOG2E
SCALE_GLA_Q = GLA_DK ** -0.5
QKV_SEGMENTS = (
    (0, D_DIL, 0, SCALE_ATTN_Q),
    (D_DIL, 2 * D_DIL, D_DIL, 1.0),
)
REST_SEGMENTS = (
    (QKV_W, D_GLA_K, COL_GQ, SCALE_GLA_Q),
    (QKV_W + D_GLA_K, D_GLA_K + D_GLA_V, COL_GK, 1.0),
    (SRC_GATE + D_DIL, D_GLA_V, COL_GATE_G, 1.0),
    (SRC_MQ, D_MEM, COL_MQ, SCALE_ATTN_Q),
    (SRC_GATE + D_DIL + D_GLA_V, D_MEM, COL_GATE_M, 1.0),
    (SRC_GATE, D_DIL, COL_GATE_A, 1.0),
)
WPREP_TK = 512


def _wprep_kernel(wt_ref, qkv_ref, rest_ref):
    for o_ref, segments in ((qkv_ref, QKV_SEGMENTS), (rest_ref, REST_SEGMENTS)):
        for src, width, dst, scale in segments:
            for lo in range(0, width, LANES):
                w = wt_ref[src + lo:src + lo + LANES, :].T
                if scale != 1.0:
                    w = w * scale
                o_ref[:, dst + lo:dst + lo + LANES] = w.astype(o_ref.dtype)
    glr = wt_ref[SRC_GLR:SRC_GLR + LANES, :].T
    keep = lax.broadcasted_iota(jnp.int32, glr.shape, 1) < GLA_GATE_RANK
    rest_ref[:, COL_GLR:COL_GLR + LANES] = jnp.where(keep, glr, 0.0).astype(rest_ref.dtype)
    rest_ref[:, COL_GLR + LANES:COL_GLR + GLR_PAD] = jnp.zeros((WPREP_TK, GLR_PAD - LANES), rest_ref.dtype)


def _regroup_in_weights(w_in):
    return pl.pallas_call(
        _wprep_kernel,
        grid=(D_MODEL // WPREP_TK,),
        in_specs=[pl.BlockSpec((D_IN_PROJ, WPREP_TK), lambda i: (0, i))],
        out_specs=[pl.BlockSpec((WPREP_TK, QKV_W), lambda i: (i, 0)),
                   pl.BlockSpec((WPREP_TK, REST_W), lambda i: (i, 0))],
        out_shape=[jax.ShapeDtypeStruct((D_MODEL, QKV_W), BF16), jax.ShapeDtypeStruct((D_MODEL, REST_W), BF16)],
        compiler_params=_cparams(("parallel",)),
        name="weight_regroup",
    )(w_in.T)


def kernel(x, mem, norm_pre_w, w_in, rel_bias, w_gla_gate2, b_gla_gate, gla_norm_w, mem_norm_w, w_mem_kv,
           w_out, norm_post_w):
    batch, seq, _ = x.shape
    depth = w_in.shape[0]
    expand_np = np.zeros((LANES, D_DIL), np.float32)
    for h in range(N_HEADS_DIL):
        expand_np[_stat_lane(h), h * HEAD_DIM:(h + 1) * HEAD_DIM] = 1.0
    expand = jnp.asarray(expand_np, BF16)
    expand_mem = jnp.asarray(expand_np[:, :D_MEM], BF16)
    stat_mask_np = np.zeros((F32_SUBLANES, LANES), np.float32)
    stat_mask_np[0, [_stat_lane(h) for h in range(N_HEADS_DIL)]] = 1.0
    stat_mask_np[1] = 1.0 - stat_mask_np[0]
    stat_mask = jnp.asarray(stat_mask_np)
    rb_t = jnp.zeros((2 * F32_SUBLANES, LANES), F32).at[:N_HEADS_DIL, :REL_BUCKETS].set(rel_bias.astype(F32).T)
    for l in range(depth):
        x2d = x.reshape(batch * seq, D_MODEL)
        w_qkv, w_rest = _regroup_in_weights(w_in[l])
        h, qkv1, qkv4, qkv16 = _qkv_proj(x2d, norm_pre_w[l][None], w_qkv, batch, seq)
        rest = _rest_proj(h, w_rest)

        srcs = (qkv1.reshape(batch, 1, seq, QKV_W), qkv4, qkv16)
        mem_args = (rest, mem, mem_norm_w[l][None], w_mem_kv[l])
        pats = [_dilated(src, rb_t, batch, seq, window, dil, mem_args if dil == 1 else None)
                for (window, dil), src in zip(DIL_PATTERNS, srcs)]
        pats[0], mem_pair = pats[0]

        o_gla, w_out_bf = _gla(rest, w_gla_gate2[l], b_gla_gate[l][None], gla_norm_w[l][None], w_out[l],
                               batch, seq)

        out = _out_proj(pats, o_gla, mem_pair, rest, stat_mask, expand, expand_mem, w_out_bf, x2d,
                        norm_post_w[l][None])
        x = out.reshape(batch, seq, D_MODEL)
    return x
```

```python
import functools
import math

import numpy as np
import jax
import jax.numpy as jnp
from jax import lax
from jax.experimental import pallas as pl
from jax.experimental.pallas import tpu as pltpu

F32 = jnp.float32
BF16 = jnp.bfloat16

D_MODEL = 2048
HEAD_DIM = 64
N_HEADS_DIL = 12
DIL_PATTERNS = ((128, 1), (512, 4), (2048, 16))
DIL_BLOCK = 128
N_HEADS_GLA = 4
GLA_DK = 128
GLA_DV = 256
GLA_GATE_RANK = 16
GLA_TAU = 16.0
GLA_CHUNK = 64
N_HEADS_MEM = 4
MEM_LEN = 256
REL_BUCKETS = 32
REL_MAX_DIST = 2048
EPS = 1e-6
NEG_INF = -1e30

D_DIL = N_HEADS_DIL * HEAD_DIM
D_GLA_K = N_HEADS_GLA * GLA_DK
D_GLA_V = N_HEADS_GLA * GLA_DV
D_MEM = N_HEADS_MEM * HEAD_DIM
D_MIX = D_DIL + D_GLA_V + D_MEM

LANES = 128
F32_SUBLANES = 8
GLR_PAD = 256

QKV_W = 3 * D_DIL
COL_GQ = 0
COL_GK = COL_GQ + D_GLA_K
COL_GV = COL_GK + D_GLA_K
COL_GATE_G = COL_GV + D_GLA_V
COL_MQ = COL_GATE_G + D_GLA_V
COL_GATE_M = COL_MQ + D_MEM
COL_GLR = COL_GATE_M + D_MEM
COL_GATE_A = COL_GLR + GLR_PAD
REST_W = COL_GATE_A + D_DIL

VMEM_LIMIT = 56 * 1024 * 1024


def _cparams(sem):
    return pltpu.CompilerParams(dimension_semantics=sem, vmem_limit_bytes=VMEM_LIMIT)


def _split_hi_lo(w):
    hi = w.astype(BF16)
    return hi, (w - hi.astype(F32)).astype(BF16)


QKV_TM = 512
N_SLABS = D_DIL // LANES
REGROUP_DILS = (4, 16)


QKV_X_SLOTS = 3


def _qkv_proj_kernel(x_hbm, nw_ref, w_ref, h_ref, o_ref, x4_ref, x16_ref, slab_ref, sub_ref, xbuf_ref, sem):
    i, n = pl.program_id(0), pl.num_programs(0)

    def fetch(step):
        slot = step % QKV_X_SLOTS
        return pltpu.make_async_copy(x_hbm.at[pl.ds(pl.multiple_of(step * QKV_TM, QKV_TM), QKV_TM), :],
                                     xbuf_ref.at[slot], sem.at[slot])

    @pl.when(i == 0)
    def _():
        for step in range(QKV_X_SLOTS - 1):
            fetch(step).start()

    @pl.when(i + QKV_X_SLOTS - 1 < n)
    def _():
        fetch(i + QKV_X_SLOTS - 1).start()

    fetch(i).wait()
    x = xbuf_ref[i % QKV_X_SLOTS]
    ms = jnp.mean(x * x, axis=-1, keepdims=True)
    h_ref[...] = (x * lax.rsqrt(ms + EPS) * nw_ref[...]).astype(BF16)
    for t in range(QKV_W // D_DIL):
        cols = slice(t * D_DIL, (t + 1) * D_DIL)
        res = jnp.dot(h_ref[...], w_ref[:, cols], preferred_element_type=F32)
        o_ref[:, cols] = res.astype(o_ref.dtype)
        for s in range(N_SLABS):
            slab_ref[t, s] = res[:, s * LANES:(s + 1) * LANES]
        for s in range(N_SLABS):
            lo = t * D_DIL + s * LANES
            for r in range(4):
                sub = slab_ref[t, s, pl.ds(r, QKV_TM // 4, stride=4), :]
                x4_ref[0, r, :, lo:lo + LANES] = sub.astype(x4_ref.dtype)
                sub_ref[t, s, r] = sub
            for r in range(4):
                for q in range(4):
                    x16_ref[0, 4 * q + r, :, lo:lo + LANES] = (
                        sub_ref[t, s, r, pl.ds(q, QKV_TM // 16, stride=4), :].astype(x16_ref.dtype))


def _qkv_proj(x2d, norm_w, w_qkv, batch, seq):
    m = x2d.shape[0]
    tiles_per_seq = seq // QKV_TM
    assert m // QKV_TM >= QKV_X_SLOTS - 1
    const = lambda shape: pl.BlockSpec(shape, lambda i: (0, 0), pipeline_mode=pl.Buffered(1))
    grouped = lambda dil: pl.BlockSpec((1, dil, QKV_TM // dil, QKV_W),
                                       lambda i: (i // tiles_per_seq, 0, i % tiles_per_seq, 0))
    return pl.pallas_call(
        _qkv_proj_kernel,
        grid=(m // QKV_TM,),
        in_specs=[
            pl.BlockSpec(memory_space=pl.ANY),
            const((1, D_MODEL)), const((D_MODEL, QKV_W)),
        ],
        out_specs=[pl.BlockSpec((QKV_TM, D_MODEL), lambda i: (i, 0)),
                   pl.BlockSpec((QKV_TM, QKV_W), lambda i: (i, 0))] + [grouped(d) for d in REGROUP_DILS],
        out_shape=[jax.ShapeDtypeStruct((m, D_MODEL), BF16), jax.ShapeDtypeStruct((m, QKV_W), BF16)]
        + [jax.ShapeDtypeStruct((batch, d, seq // d, QKV_W), BF16) for d in REGROUP_DILS],
        scratch_shapes=[pltpu.VMEM((QKV_W // D_DIL, N_SLABS, QKV_TM, LANES), F32),
                        pltpu.VMEM((QKV_W // D_DIL, N_SLABS, 4, QKV_TM // 4, LANES), F32),
                        pltpu.VMEM((QKV_X_SLOTS, QKV_TM, D_MODEL), F32),
                        pltpu.SemaphoreType.DMA((QKV_X_SLOTS,))],
        compiler_params=_cparams(("arbitrary",)),
        name="qkv_proj",
    )(x2d, norm_w, w_qkv)


REST_TM = 512


def _rest_proj_kernel(h_ref, w_ref, o_ref):
    o_ref[...] = jnp.dot(h_ref[...], w_ref[...], preferred_element_type=F32).astype(o_ref.dtype)


def _rest_proj(h, w_rest):
    m = h.shape[0]
    const = lambda shape: pl.BlockSpec(shape, lambda i: (0, 0), pipeline_mode=pl.Buffered(1))
    return pl.pallas_call(
        _rest_proj_kernel,
        grid=(m // REST_TM,),
        in_specs=[pl.BlockSpec((REST_TM, D_MODEL), lambda i: (i, 0)), const((D_MODEL, REST_W))],
        out_specs=pl.BlockSpec((REST_TM, REST_W), lambda i: (i, 0)),
        out_shape=jax.ShapeDtypeStruct((m, REST_W), BF16),
        compiler_params=_cparams(("parallel",)),
        name="rest_proj",
    )(h, w_rest)


LOG2E = math.log2(math.e)


def _pair_attention(q_pair, k_pair, v_pair, bias_e, bias_o):
    rows = q_pair.shape[0]
    lo_q = lax.broadcasted_iota(jnp.int32, q_pair.shape, 1) < HEAD_DIM
    zero = jnp.zeros_like(q_pair)
    q_both = jnp.concatenate([jnp.where(lo_q, q_pair, zero), jnp.where(lo_q, zero, q_pair)], axis=0)
    s = lax.dot_general(q_both, k_pair, (((1,), (1,)), ((), ())), preferred_element_type=F32)
    s_e, s_o = s[:rows], s[rows:]
    if bias_e is not None:
        s_e = s_e + bias_e
        s_o = s_o + bias_o
    mx_e = jnp.max(s_e, axis=-1, keepdims=True)
    mx_o = jnp.max(s_o, axis=-1, keepdims=True)
    p_e = jnp.exp2(s_e - mx_e).astype(BF16)
    p_o = jnp.exp2(s_o - mx_o).astype(BF16)
    lo_v = lax.broadcasted_iota(jnp.int32, v_pair.shape, 1) < HEAD_DIM
    one = jnp.ones_like(v_pair)
    r_e = jnp.dot(p_e, jnp.where(lo_v, v_pair, one), preferred_element_type=F32)
    r_o = jnp.dot(p_o, jnp.where(lo_v, one, v_pair), preferred_element_type=F32)
    return r_e, r_o, mx_e, mx_o


def _stat_lane(head):
    return head + HEAD_DIM if head % 2 == 0 else head


STAT_DEN_SHIFT = 16


DIL_UNITS = {1: 8, 4: 4, 16: 8}


def _dil_init_bias(rb_ref, sel_ref, mrow_ref, bias_ref):
    rb_hi, rb_lo = _split_hi_lo(rb_ref[...])
    f = (jnp.dot(rb_hi, sel_ref[...], preferred_element_type=F32)
         + jnp.dot(rb_lo, sel_ref[...], preferred_element_type=F32)) * LOG2E + mrow_ref[...]
    col = lax.broadcasted_iota(jnp.int32, (DIL_BLOCK, 2 * DIL_BLOCK), 1)
    for h in range(N_HEADS_DIL):
        row = jnp.broadcast_to(f[h:h + 1, :], (DIL_BLOCK, 2 * DIL_BLOCK))
        tab = pltpu.roll(row, 0, 1, stride=1, stride_axis=0)
        bias_ref[0, h] = tab
        bias_ref[1, h] = jnp.where(col >= DIL_BLOCK, tab, NEG_INF)


def _dil_unit(q, kp, kc, vp, vc, first, bias_ref, put_num):
    lane = lax.broadcasted_iota(jnp.int32, (DIL_BLOCK, LANES), 1)
    lo = lane < HEAD_DIM
    st_c = jnp.zeros((DIL_BLOCK, LANES), F32)
    for j in range(N_HEADS_DIL // 2):
        sl = slice(j * LANES, (j + 1) * LANES)
        if kp is None:
            k_pair, v_pair = kc(sl), vc(sl)
            bias_e, bias_o = bias_ref[0, 2 * j, :, DIL_BLOCK:], bias_ref[0, 2 * j + 1, :, DIL_BLOCK:]
        else:
            k_pair = jnp.concatenate([kp(sl), kc(sl)], axis=0)
            v_pair = jnp.concatenate([vp(sl), vc(sl)], axis=0)
            bias_e, bias_o = bias_ref[first, 2 * j], bias_ref[first, 2 * j + 1]
        r_e, r_o, mx_e, mx_o = _pair_attention(q(sl), k_pair, v_pair, bias_e, bias_o)
        put_num(j, jnp.where(lo, r_e, r_o))
        l_e, l_o = _stat_lane(2 * j), _stat_lane(2 * j + 1)
        st_c = jnp.where(lane == l_e, mx_e, jnp.where(lane == l_e + STAT_DEN_SHIFT, r_e, st_c))
        st_c = jnp.where(lane == l_o, mx_o, jnp.where(lane == l_o + STAT_DEN_SHIFT, r_o, st_c))
    return st_c


def _mem_kv(mem_ref, mnw_ref, wkv_ref, kv_ref):
    m = mem_ref[0]
    ms = jnp.mean(m * m, axis=-1, keepdims=True)
    mn = (m * lax.rsqrt(ms + EPS) * mnw_ref[...]).astype(BF16)
    kv_ref[...] = jnp.dot(mn, wkv_ref[...].astype(BF16), preferred_element_type=F32).astype(BF16)


def _mem_unit(q, kv_ref, put_num):
    lane = lax.broadcasted_iota(jnp.int32, (DIL_BLOCK, LANES), 1)
    lo = lane < HEAD_DIM
    den_c = jnp.ones((DIL_BLOCK, LANES), F32)
    for j in range(N_HEADS_MEM // 2):
        sl = slice(j * LANES, (j + 1) * LANES)
        r_e, r_o, _, _ = _pair_attention(q(sl), kv_ref[:, sl], kv_ref[:, D_MEM + j * LANES:D_MEM + (j + 1) * LANES],
                                         None, None)
        put_num(j, jnp.where(lo, r_e, r_o))
        den_c = jnp.where(lane == _stat_lane(2 * j), r_e, jnp.where(lane == _stat_lane(2 * j + 1), r_o, den_c))
    return den_c


def _dil1_kernel(q_ref, kp_ref, kc_ref, vp_ref, vc_ref, rb_ref, sel_ref, mrow_ref, mq_ref, mem_ref, mnw_ref, wkv_ref,
                 o_ref, st_ref, om_ref, dm_ref, bias_ref, kv_ref):
    @pl.when((pl.program_id(0) == 0) & (pl.program_id(1) == 0))
    def _():
        _dil_init_bias(rb_ref, sel_ref, mrow_ref, bias_ref)

    @pl.when(pl.program_id(1) == 0)
    def _():
        _mem_kv(mem_ref, mnw_ref, wkv_ref, kv_ref)

    for u in range(DIL_UNITS[1]):
        rows = slice(u * DIL_BLOCK, (u + 1) * DIL_BLOCK)
        prev = slice((u - 1) * DIL_BLOCK, u * DIL_BLOCK)
        if u == 0:
            first = (pl.program_id(1) == 0).astype(jnp.int32)
            kp, vp = (lambda sl: kp_ref[:, sl]), (lambda sl: vp_ref[:, sl])
        else:
            first = 0
            kp, vp = (lambda sl, p=prev: kc_ref[p, sl]), (lambda sl, p=prev: vc_ref[p, sl])

        def put_num(j, num, rows=rows):
            o_ref[rows, j * LANES:(j + 1) * LANES] = num.astype(o_ref.dtype)

        st_ref[rows, :] = _dil_unit(lambda sl, r=rows: q_ref[r, sl], kp, lambda sl, r=rows: kc_ref[r, sl],
                                    vp, lambda sl, r=rows: vc_ref[r, sl], first, bias_ref, put_num)

        def put_mem(j, num, rows=rows):
            om_ref[rows, j * LANES:(j + 1) * LANES] = num.astype(om_ref.dtype)

        dm_ref[rows, :] = _mem_unit(lambda sl, r=rows: mq_ref[r, sl], kv_ref, put_mem)


DIL_LBLOCKS = {4: 2, 16: 2}


def _dil_kernel(*refs, dil, whole_seq):
    if whole_seq:
        q_ref, kc_ref, vc_ref, *rest = refs
        kp_ref = vp_ref = None
    else:
        q_ref, kp_ref, kc_ref, vp_ref, vc_ref, *rest = refs
    rb_ref, sel_ref, mrow_ref, o_ref, st_ref, bias_ref, acc_ref, tmp_ref = rest
    b, lb, g = pl.program_id(0), pl.program_id(1), pl.program_id(2)
    nr, nl = DIL_UNITS[dil], DIL_LBLOCKS[dil]
    gather = whole_seq and dil == 2 * F32_SUBLANES
    n_pairs = N_HEADS_DIL // 2

    @pl.when((b == 0) & (lb == 0) & (g == 0))
    def _():
        _dil_init_bias(rb_ref, sel_ref, mrow_ref, bias_ref)

    for ll in range(nl):
        sub = slice(ll * DIL_BLOCK, (ll + 1) * DIL_BLOCK)
        prev = slice((ll - 1) * DIL_BLOCK, ll * DIL_BLOCK)
        first = (lb == 0).astype(jnp.int32) if ll == 0 else 0
        for u in range(nr):
            rows = pl.ds(ll * DIL_BLOCK * dil + g * nr + u, DIL_BLOCK, stride=dil)
            if ll == 0 and whole_seq:
                kp = vp = None
            elif ll == 0:
                kp, vp = (lambda sl, u=u: kp_ref[u, :, sl]), (lambda sl, u=u: vp_ref[u, :, sl])
            else:
                kp = lambda sl, u=u, p=prev: kc_ref[u, p, sl]
                vp = lambda sl, u=u, p=prev: vc_ref[u, p, sl]

            if gather:
                def put_num(j, num, ll=ll, u=u):
                    for c in range(DIL_BLOCK // dil):
                        at = pl.multiple_of((ll * (DIL_BLOCK // dil) + c) * dil * dil + (g * nr + u) * dil, dil)
                        acc_ref[j, pl.ds(at, dil), :] = num[c * dil:(c + 1) * dil]
            else:
                def put_num(j, num, rows=rows):
                    acc_ref[j, rows, :] = num

            stats = _dil_unit(lambda sl, u=u, r=sub: q_ref[u, r, sl], kp,
                              lambda sl, u=u, r=sub: kc_ref[u, r, sl], vp,
                              lambda sl, u=u, r=sub: vc_ref[u, r, sl], first, bias_ref, put_num)
            if gather:
                put_num(n_pairs, stats)
            else:
                st_ref[rows, :] = stats

    @pl.when(g == dil // nr - 1)
    def _():
        if gather:
            def body(mm, carry):
                for j in range(n_pairs + 1):
                    for i0 in range(4):
                        tmp_ref[j, i0] = acc_ref[j, pl.ds(mm * dil * dil + i0, dil * 4, stride=4), :]
                    for i in range(dil):
                        group = pl.ds(pl.multiple_of((mm * dil + i) * dil, dil), dil)
                        val = tmp_ref[j, i % 4, pl.ds(i // 4, dil, stride=4), :]
                        if j < n_pairs:
                            o_ref[group, j * LANES:(j + 1) * LANES] = val.astype(o_ref.dtype)
                        else:
                            st_ref[group, :] = val
                return carry
            lax.fori_loop(0, nl * DIL_BLOCK // dil, body, 0)
        else:
            for j in range(N_HEADS_DIL // 2):
                o_ref[:, j * LANES:(j + 1) * LANES] = acc_ref[j].astype(o_ref.dtype)


def _t5_bucket_np(dist):
    max_exact = REL_BUCKETS // 2
    n = np.maximum(dist, 1).astype(np.float32)
    large = max_exact + (np.log(n / np.float32(max_exact)) / np.float32(math.log(REL_MAX_DIST / max_exact))
                         * np.float32(REL_BUCKETS - max_exact)).astype(np.int32)
    large = np.minimum(large, REL_BUCKETS - 1)
    return np.where(dist < max_exact, dist, large)


def _bias_selector(window, dil):
    back = window // dil
    x = np.arange(2 * DIL_BLOCK)
    rel = DIL_BLOCK - x
    valid = (rel >= 0) & (rel <= back)
    bucket = _t5_bucket_np(np.clip(rel, 0, back) * dil)
    sel = np.zeros((LANES, 2 * DIL_BLOCK), np.float32)
    sel[bucket[valid], x[valid]] = 1.0
    mrow = np.where(valid, 0.0, NEG_INF).astype(np.float32)[None]
    return jnp.asarray(sel, BF16), jnp.asarray(mrow)


def _dilated(src, rb_t, batch, seq, window, dil, mem_args=None):
    sub_len = seq // dil
    nb = sub_len // DIL_BLOCK
    sel, mrow = _bias_selector(window, dil)
    nu = DIL_UNITS[dil]
    consts = (rb_t, sel, mrow)
    const = lambda a: pl.BlockSpec(a.shape, lambda *_: (0, 0))
    stat_shape = jax.ShapeDtypeStruct((batch * seq, LANES), F32)
    out_shape = [jax.ShapeDtypeStruct((batch * seq, D_DIL), BF16), stat_shape]
    bias_scratch = pltpu.VMEM((2, N_HEADS_DIL, DIL_BLOCK, 2 * DIL_BLOCK), F32)

    if dil == 1:
        rest, mem, mem_norm_w, w_kv = mem_args
        rows = nu * DIL_BLOCK
        steps = nb // nu
        cur = lambda col: pl.BlockSpec((None, None, rows, D_DIL), lambda b, l: (b, 0, l, col))
        prev = lambda col: pl.BlockSpec((None, None, DIL_BLOCK, D_DIL),
                                        lambda b, l: (b, 0, jnp.maximum(nu * l - 1, 0), col))
        out = lambda width, col=0: pl.BlockSpec((rows, width), lambda b, l: (b * steps + l, col // width))
        mem_specs = [out(D_MEM, COL_MQ), pl.BlockSpec((1, MEM_LEN, D_MODEL), lambda b, l: (b, 0, 0)),
                     const(mem_norm_w), const(w_kv)]
        num, stats, mem_num, mem_den = pl.pallas_call(
            _dil1_kernel,
            grid=(batch, steps),
            in_specs=[cur(0), prev(1), cur(1), prev(2), cur(2)] + [const(a) for a in consts] + mem_specs,
            out_specs=[out(D_DIL), out(LANES), out(D_MEM), out(LANES)],
            out_shape=out_shape + [jax.ShapeDtypeStruct((batch * seq, D_MEM), BF16), stat_shape],
            scratch_shapes=[bias_scratch, pltpu.VMEM((MEM_LEN, 2 * D_MEM), BF16)],
            compiler_params=_cparams(("arbitrary", "arbitrary")),
            name="dilated_d1",
        )(src, src, src, src, src, *consts, rest, mem, mem_norm_w, w_kv)
        return (num, stats), (mem_num, mem_den)

    nl = DIL_LBLOCKS[dil]
    steps = nb // nl
    rows = nl * DIL_BLOCK * dil
    cur = lambda col: pl.BlockSpec((None, nu, nl * DIL_BLOCK, D_DIL), lambda b, l, g: (b, g, l, col))
    prev = lambda col: pl.BlockSpec((None, nu, DIL_BLOCK, D_DIL),
                                    lambda b, l, g: (b, g, jnp.maximum(nl * l - 1, 0), col))
    out = lambda width: pl.BlockSpec((rows, width), lambda b, l, g: (b * steps + l, 0))
    whole_seq = steps == 1
    qkv_specs = [cur(0), cur(1), cur(2)] if whole_seq else [cur(0), prev(1), cur(1), prev(2), cur(2)]
    return pl.pallas_call(
        functools.partial(_dil_kernel, dil=dil, whole_seq=whole_seq),
        grid=(batch, steps, dil // nu),
        in_specs=qkv_specs + [const(a) for a in consts],
        out_specs=[out(D_DIL), out(LANES)],
        out_shape=out_shape,
        scratch_shapes=[bias_scratch, pltpu.VMEM((N_HEADS_DIL // 2 + 1, rows, LANES), F32),
                        pltpu.VMEM((N_HEADS_DIL // 2 + 1, 4, 4 * dil, LANES), F32)],
        compiler_params=_cparams(("arbitrary", "arbitrary", "arbitrary")),
        name=f"dilated_d{dil}",
    )(*([src] * len(qkv_specs)), *consts)


GLA_TM = 256
GLA_STEP_TILES = 2


GLA_NC = GLA_TM // GLA_CHUNK


def _gla_tile(bb, rows, w2, q_ref, k_ref, v_ref, g_ref, bg_ref, nw_ref, tri_ref, o_ref, st_ref):
    g = g_ref[bb, rows, :LANES]
    z = jnp.dot(g, w2, preferred_element_type=F32) + bg_ref[...]
    la = (jnp.minimum(z, 0.0) * (LOG2E / GLA_TAU)
          - jnp.log2(1.0 + jnp.exp2(jnp.abs(z) * -LOG2E)) * (1.0 / GLA_TAU))
    la_h = la.astype(BF16)
    la_l = (la - la_h.astype(F32)).astype(BF16)

    tri = tri_ref[...]
    b = jnp.dot(tri, la_h, preferred_element_type=F32) + jnp.dot(tri, la_l, preferred_element_type=F32)
    b_last = jnp.concatenate(
        [jnp.broadcast_to(b[(ci + 1) * GLA_CHUNK - 1:(ci + 1) * GLA_CHUNK, :], (GLA_CHUNK, D_GLA_K))
         for ci in range(GLA_NC)], axis=0)

    q = q_ref[bb, rows].astype(F32)
    k = k_ref[bb, rows].astype(F32)
    q_t = (q * jnp.exp2(b)).astype(BF16)
    k_t = (k * jnp.exp2(-b)).astype(BF16)
    k_s = k * jnp.exp2(b_last - b)

    causal = tri > 0
    t_chunk = lax.broadcasted_iota(jnp.int32, (GLA_DK, GLA_TM), 1) // GLA_CHUNK
    nw = nw_ref[...]
    for h in range(N_HEADS_GLA):
        cs = slice(h * GLA_DK, (h + 1) * GLA_DK)
        vs = slice(h * GLA_DV, (h + 1) * GLA_DV)
        v_h = v_ref[bb, rows, vs]
        a = lax.dot_general(q_t[:, cs], k_t[:, cs], (((1,), (1,)), ((), ())), preferred_element_type=F32)
        o = jnp.dot(jnp.where(causal, a.astype(BF16), jnp.zeros_like(tri)), v_h, preferred_element_type=F32)
        k_s_t = k_s[:, cs].T.astype(BF16)
        zero = jnp.zeros_like(k_s_t)
        kv_all = jnp.dot(jnp.concatenate([jnp.where(t_chunk == ci, k_s_t, zero) for ci in range(GLA_NC)], axis=0),
                         v_h, preferred_element_type=F32)
        decay_t = jnp.exp2(b_last[:, cs].T)
        st = st_ref[bb, h]
        outs = []
        for ci in range(GLA_NC):
            rs = slice(ci * GLA_CHUNK, (ci + 1) * GLA_CHUNK)
            o_c = o[rs] + jnp.dot(q_t[rs, cs], st.astype(BF16), preferred_element_type=F32)
            decay = jnp.broadcast_to(decay_t[:, ci * GLA_CHUNK:ci * GLA_CHUNK + 1], (GLA_DK, GLA_DV))
            st = decay * st + kv_all[ci * GLA_DK:(ci + 1) * GLA_DK]
            ms = jnp.mean(o_c * o_c, axis=-1, keepdims=True)
            outs.append((o_c * lax.rsqrt(ms + EPS) * nw).astype(o_ref.dtype))
        st_ref[bb, h] = st
        o_ref[bb, rows, vs] = jnp.concatenate(outs, axis=0)


def _gla_kernel(q_ref, k_ref, v_ref, g_ref, w2_ref, bg_ref, nw_ref, tri_ref, wout_ref,
                o_ref, wout_bf_ref, st_ref):

    @pl.when(pl.program_id(0) == 0)
    def _():
        st_ref[...] = jnp.zeros_like(st_ref)

    wout_bf_ref[...] = wout_ref[...].astype(wout_bf_ref.dtype)

    w2 = jnp.concatenate([w2_ref[...].astype(BF16), jnp.zeros((LANES - GLA_GATE_RANK, D_GLA_K), BF16)], axis=0)
    for tt in range(GLA_STEP_TILES):
        rows = slice(tt * GLA_TM, (tt + 1) * GLA_TM)
        for bb in range(st_ref.shape[0]):
            _gla_tile(bb, rows, w2, q_ref, k_ref, v_ref, g_ref, bg_ref, nw_ref, tri_ref, o_ref, st_ref)


def _gla(proj, w2, b_gate, norm_w, w_out, batch, seq):
    step_rows = GLA_TM * GLA_STEP_TILES
    steps = seq // step_rows
    wout_rows = w_out.shape[0] // steps
    pos = np.arange(GLA_TM)
    tri = jnp.asarray((pos[:, None] // GLA_CHUNK == pos[None, :] // GLA_CHUNK) & (pos[None, :] <= pos[:, None]),
                      BF16)
    proj3 = proj.reshape(batch, seq, REST_W)
    rows = lambda width, col: pl.BlockSpec((batch, step_rows, width), lambda i: (0, i, col // width))
    const = lambda shape: pl.BlockSpec(shape, lambda i: (0, 0))
    wout_spec = pl.BlockSpec((wout_rows, w_out.shape[1]), lambda i: (i, 0))
    out, w_out_bf = pl.pallas_call(
        _gla_kernel,
        grid=(steps,),
        in_specs=[
            rows(D_GLA_K, COL_GQ), rows(D_GLA_K, COL_GK), rows(D_GLA_V, COL_GV), rows(GLR_PAD, COL_GLR),
            const((GLA_GATE_RANK, D_GLA_K)), const((1, D_GLA_K)), const((1, GLA_DV)),
            const((GLA_TM, GLA_TM)), wout_spec,
        ],
        out_specs=[rows(D_GLA_V, 0), wout_spec],
        out_shape=[jax.ShapeDtypeStruct((batch, seq, D_GLA_V), BF16), jax.ShapeDtypeStruct(w_out.shape, BF16)],
        scratch_shapes=[pltpu.VMEM((batch, N_HEADS_GLA, GLA_DK, GLA_DV), F32)],
        compiler_params=_cparams(("arbitrary",)),
        name="gla",
    )(proj3, proj3, proj3, proj3, w2, b_gate, norm_w, tri, w_out)
    return out.reshape(batch * seq, D_GLA_V), w_out_bf


OUT_TM = 512


def _silu(g):
    h = g * 0.5
    return h * jnp.tanh(h) + h


OUT_GW = 256


def _out_kernel(n1_ref, n2_ref, n3_ref, s1_ref, s2_ref, s3_ref, og_ref, nm_ref, dm_ref, ga_ref, gg_ref, gm_ref,
                sm_ref, ex_ref, exm_ref, w_ref, x_ref, nw_ref, out_ref, mix_ref):
    keep = sm_ref[0:1, :]
    fill = sm_ref[1:2, :]
    stats = [s_ref[...] for s_ref in (s1_ref, s2_ref, s3_ref)]
    m1, m2, m3 = [st * keep for st in stats]
    d1, d2, d3 = [pltpu.roll(st, LANES - STAT_DEN_SHIFT, 1) * keep + fill for st in stats]
    mx = jnp.maximum(jnp.maximum(m1, m2), m3)
    t1, t2, t3 = jnp.exp2(m1 - mx), jnp.exp2(m2 - mx), jnp.exp2(m3 - mx)
    inv = 1.0 / (t1 * d1 + t2 * d2 + t3 * d3)
    wts = [(t * inv).astype(BF16) for t in (t1, t2, t3)]
    inv_m = (1.0 / dm_ref[...]).astype(BF16)

    def gate_chunk(lo):
        cols = slice(lo, lo + OUT_GW)
        if lo < D_DIL:
            a = jnp.zeros((OUT_TM, OUT_GW), F32)
            for wt, n_ref in zip(wts, (n1_ref, n2_ref, n3_ref)):
                a = a + jnp.dot(wt, ex_ref[:, cols], preferred_element_type=F32) * n_ref[:, cols].astype(F32)
            g = ga_ref[:, cols]
        elif lo < D_DIL + D_GLA_V:
            lc = slice(lo - D_DIL, lo - D_DIL + OUT_GW)
            a, g = og_ref[:, lc], gg_ref[:, lc]
        else:
            a = jnp.dot(inv_m, exm_ref[...], preferred_element_type=F32) * nm_ref[...].astype(F32)
            g = gm_ref[...]
        mix_ref[:, cols] = a.astype(BF16) * _silu(g)

    for lo in range(0, D_MIX, OUT_GW):
        gate_chunk(lo)
    y = jnp.dot(mix_ref[...], w_ref[...], preferred_element_type=F32)
    ms = jnp.mean(y * y, axis=-1, keepdims=True)
    out_ref[...] = x_ref[...] + y * lax.rsqrt(ms + EPS) * nw_ref[...]


def _out_proj(pats, o_gla, mem_pair, proj, stat_mask, expand, expand_mem, w_out, x2d, norm_w):
    m = x2d.shape[0]
    rows = lambda width, col=0: pl.BlockSpec((OUT_TM, width), lambda s: (s, col // width))
    const = lambda shape: pl.BlockSpec(shape, lambda s: (0, 0), pipeline_mode=pl.Buffered(1))
    nums, stats = zip(*pats)
    return pl.pallas_call(
        _out_kernel,
        grid=(m // OUT_TM,),
        in_specs=[
            *([rows(D_DIL)] * 3), *([rows(LANES)] * 3),
            rows(D_GLA_V), rows(D_MEM), rows(LANES),
            rows(D_DIL, COL_GATE_A), rows(D_GLA_V, COL_GATE_G), rows(D_MEM, COL_GATE_M),
            const((F32_SUBLANES, LANES)), const((LANES, D_DIL)), const((LANES, D_MEM)), const((D_MIX, D_MODEL)),
            rows(D_MODEL), const((1, D_MODEL)),
        ],
        out_specs=rows(D_MODEL),
        out_shape=jax.ShapeDtypeStruct((m, D_MODEL), x2d.dtype),
        scratch_shapes=[pltpu.VMEM((OUT_TM, D_MIX), BF16)],
        compiler_params=_cparams(("parallel",)),
        name="out_proj",
    )(*nums, *stats, o_gla, *mem_pair, proj, proj, proj, stat_mask, expand, expand_mem, w_out, x2d, norm_w)


D_IN_PROJ = 3 * D_DIL + 2 * D_GLA_K + D_GLA_V + GLA_GATE_RANK + D_MEM + D_MIX
SRC_GLR = 3 * D_DIL + 2 * D_GLA_K + D_GLA_V
SRC_MQ = SRC_GLR + GLA_GATE_RANK
SRC_GATE = SRC_MQ + D_MEM
SCALE_ATTN_Q = HEAD_DIM ** -0.5 * LOG2E
SCALE_GLA_Q = GLA_DK ** -0.5
QKV_SEGMENTS = (
    (0, D_DIL, 0, SCALE_ATTN_Q),
    (D_DIL, 2 * D_DIL, D_DIL, 1.0),
)
REST_SEGMENTS = (
    (QKV_W, D_GLA_K, COL_GQ, SCALE_GLA_Q),
    (QKV_W + D_GLA_K, D_GLA_K + D_GLA_V, COL_GK, 1.0),
    (SRC_GATE + D_DIL, D_GLA_V, COL_GATE_G, 1.0),
    (SRC_MQ, D_MEM, COL_MQ, SCALE_ATTN_Q),
    (SRC_GATE + D_DIL + D_GLA_V, D_MEM, COL_GATE_M, 1.0),
    (SRC_GATE, D_DIL, COL_GATE_A, 1.0),
)
WPREP_TK = 512


def _wprep_kernel(wt_ref, qkv_ref, rest_ref):
    for o_ref, segments in ((qkv_ref, QKV_SEGMENTS), (rest_ref, REST_SEGMENTS)):
        for src, width, dst, scale in segments:
            for lo in range(0, width, LANES):
                w = wt_ref[src + lo:src + lo + LANES, :].T
                if scale != 1.0:
                    w = w * scale
                o_ref[:, dst + lo:dst + lo + LANES] = w.astype(o_ref.dtype)
    glr = wt_ref[SRC_GLR:SRC_GLR + LANES, :].T
    keep = lax.broadcasted_iota(jnp.int32, glr.shape, 1) < GLA_GATE_RANK
    rest_ref[:, COL_GLR:COL_GLR + LANES] = jnp.where(keep, glr, 0.0).astype(rest_ref.dtype)
    rest_ref[:, COL_GLR + LANES:COL_GLR + GLR_PAD] = jnp.zeros((WPREP_TK, GLR_PAD - LANES), rest_ref.dtype)


def _regroup_in_weights(w_in):
    return pl.pallas_call(
        _wprep_kernel,
        grid=(D_MODEL // WPREP_TK,),
        in_specs=[pl.BlockSpec((D_IN_PROJ, WPREP_TK), lambda i: (0, i))],
        out_specs=[pl.BlockSpec((WPREP_TK, QKV_W), lambda i: (i, 0)),
                   pl.BlockSpec((WPREP_TK, REST_W), lambda i: (i, 0))],
        out_shape=[jax.ShapeDtypeStruct((D_MODEL, QKV_W), BF16), jax.ShapeDtypeStruct((D_MODEL, REST_W), BF16)],
        compiler_params=_cparams(("parallel",)),
        name="weight_regroup",
    )(w_in.T)


def kernel(x, mem, norm_pre_w, w_in, rel_bias, w_gla_gate2, b_gla_gate, gla_norm_w, mem_norm_w, w_mem_kv,
           w_out, norm_post_w):
    batch, seq, _ = x.shape
    depth = w_in.shape[0]
    expand_np = np.zeros((LANES, D_DIL), np.float32)
    for h in range(N_HEADS_DIL):
        expand_np[_stat_lane(h), h * HEAD_DIM:(h + 1) * HEAD_DIM] = 1.0
    expand = jnp.asarray(expand_np, BF16)
    expand_mem = jnp.asarray(expand_np[:, :D_MEM], BF16)
    stat_mask_np = np.zeros((F32_SUBLANES, LANES), np.float32)
    stat_mask_np[0, [_stat_lane(h) for h in range(N_HEADS_DIL)]] = 1.0
    stat_mask_np[1] = 1.0 - stat_mask_np[0]
    stat_mask = jnp.asarray(stat_mask_np)
    rb_t = jnp.zeros((2 * F32_SUBLANES, LANES), F32).at[:N_HEADS_DIL, :REL_BUCKETS].set(rel_bias.astype(F32).T)
    for l in range(depth):
        x2d = x.reshape(batch * seq, D_MODEL)
        w_qkv, w_rest = _regroup_in_weights(w_in[l])
        h, qkv1, qkv4, qkv16 = _qkv_proj(x2d, norm_pre_w[l][None], w_qkv, batch, seq)
        rest = _rest_proj(h, w_rest)

        srcs = (qkv1.reshape(batch, 1, seq, QKV_W), qkv4, qkv16)
        mem_args = (rest, mem, mem_norm_w[l][None], w_mem_kv[l])
        pats = [_dilated(src, rb_t, batch, seq, window, dil, mem_args if dil == 1 else None)
                for (window, dil), src in zip(DIL_PATTERNS, srcs)]
        pats[0], mem_pair = pats[0]

        o_gla, w_out_bf = _gla(rest, w_gla_gate2[l], b_gla_gate[l][None], gla_norm_w[l][None], w_out[l],
                               batch, seq)

        out = _out_proj(pats, o_gla, mem_pair, rest, stat_mask, expand, expand_mem, w_out_bf, x2d,
                        norm_post_w[l][None])
        x = out.reshape(batch, seq, D_MODEL)
    return x
```

```python
import functools
import math

import numpy as np
import jax
import jax.numpy as jnp
from jax import lax
from jax.experimental import pallas as pl
from jax.experimental.pallas import tpu as pltpu

F32 = jnp.float32
BF16 = jnp.bfloat16

D_MODEL = 2048
HEAD_DIM = 64
N_HEADS_DIL = 12
DIL_PATTERNS = ((128, 1), (512, 4), (2048, 16))
DIL_BLOCK = 128
N_HEADS_GLA = 4
GLA_DK = 128
GLA_DV = 256
GLA_GATE_RANK = 16
GLA_TAU = 16.0
GLA_CHUNK = 64
N_HEADS_MEM = 4
MEM_LEN = 256
REL_BUCKETS = 32
REL_MAX_DIST = 2048
EPS = 1e-6
NEG_INF = -1e30

D_DIL = N_HEADS_DIL * HEAD_DIM
D_GLA_K = N_HEADS_GLA * GLA_DK
D_GLA_V = N_HEADS_GLA * GLA_DV
D_MEM = N_HEADS_MEM * HEAD_DIM
D_MIX = D_DIL + D_GLA_V + D_MEM

LANES = 128
F32_SUBLANES = 8
GLR_PAD = 256

QKV_W = 3 * D_DIL
COL_GQ = 0
COL_GK = COL_GQ + D_GLA_K
COL_GV = COL_GK + D_GLA_K
COL_GATE_G = COL_GV + D_GLA_V
COL_MQ = COL_GATE_G + D_GLA_V
COL_GATE_M = COL_MQ + D_MEM
COL_GLR = COL_GATE_M + D_MEM
COL_GATE_A = COL_GLR + GLR_PAD
REST_W = COL_GATE_A + D_DIL

VMEM_LIMIT = 56 * 1024 * 1024


def _cparams(sem):
    return pltpu.CompilerParams(dimension_semantics=sem, vmem_limit_bytes=VMEM_LIMIT)


def _split_hi_lo(w):
    hi = w.astype(BF16)
    return hi, (w - hi.astype(F32)).astype(BF16)


QKV_TM = 512
N_SLABS = D_DIL // LANES
REGROUP_DILS = (4, 16)


def _qkv_proj_kernel(x_ref, nw_ref, w_ref, h_ref, o_ref, x4_ref, x16_ref, slab_ref, sub_ref):
    x = x_ref[...]
    ms = jnp.mean(x * x, axis=-1, keepdims=True)
    h_ref[...] = (x * lax.rsqrt(ms + EPS) * nw_ref[...]).astype(BF16)
    for t in range(QKV_W // D_DIL):
        cols = slice(t * D_DIL, (t + 1) * D_DIL)
        res = jnp.dot(h_ref[...], w_ref[:, cols], preferred_element_type=F32)
        o_ref[:, cols] = res.astype(o_ref.dtype)
        for s in range(N_SLABS):
            slab_ref[t, s] = res[:, s * LANES:(s + 1) * LANES]
        for s in range(N_SLABS):
            lo = t * D_DIL + s * LANES
            for r in range(4):
                sub = slab_ref[t, s, pl.ds(r, QKV_TM // 4, stride=4), :]
                x4_ref[0, r, :, lo:lo + LANES] = sub.astype(x4_ref.dtype)
                sub_ref[t, s, r] = sub
            for r in range(4):
                for q in range(4):
                    x16_ref[0, 4 * q + r, :, lo:lo + LANES] = (
                        sub_ref[t, s, r, pl.ds(q, QKV_TM // 16, stride=4), :].astype(x16_ref.dtype))


def _qkv_proj(x2d, norm_w, w_qkv, batch, seq):
    m = x2d.shape[0]
    tiles_per_seq = seq // QKV_TM
    const = lambda shape: pl.BlockSpec(shape, lambda i: (0, 0), pipeline_mode=pl.Buffered(1))
    grouped = lambda dil: pl.BlockSpec((1, dil, QKV_TM // dil, QKV_W),
                                       lambda i: (i // tiles_per_seq, 0, i % tiles_per_seq, 0))
    return pl.pallas_call(
        _qkv_proj_kernel,
        grid=(m // QKV_TM,),
        in_specs=[
            pl.BlockSpec((QKV_TM, D_MODEL), lambda i: (i, 0)),
            const((1, D_MODEL)), const((D_MODEL, QKV_W)),
        ],
        out_specs=[pl.BlockSpec((QKV_TM, D_MODEL), lambda i: (i, 0)),
                   pl.BlockSpec((QKV_TM, QKV_W), lambda i: (i, 0))] + [grouped(d) for d in REGROUP_DILS],
        out_shape=[jax.ShapeDtypeStruct((m, D_MODEL), BF16), jax.ShapeDtypeStruct((m, QKV_W), BF16)]
        + [jax.ShapeDtypeStruct((batch, d, seq // d, QKV_W), BF16) for d in REGROUP_DILS],
        scratch_shapes=[pltpu.VMEM((QKV_W // D_DIL, N_SLABS, QKV_TM, LANES), F32),
                        pltpu.VMEM((QKV_W // D_DIL, N_SLABS, 4, QKV_TM // 4, LANES), F32)],
        compiler_params=_cparams(("parallel",)),
        name="qkv_proj",
    )(x2d, norm_w, w_qkv)


REST_TM = 512


def _rest_proj_kernel(h_ref, w_ref, o_ref):
    o_ref[...] = jnp.dot(h_ref[...], w_ref[...], preferred_element_type=F32).astype(o_ref.dtype)


def _rest_proj(h, w_rest):
    m = h.shape[0]
    const = lambda shape: pl.BlockSpec(shape, lambda i: (0, 0), pipeline_mode=pl.Buffered(1))
    return pl.pallas_call(
        _rest_proj_kernel,
        grid=(m // REST_TM,),
        in_specs=[pl.BlockSpec((REST_TM, D_MODEL), lambda i: (i, 0)), const((D_MODEL, REST_W))],
        out_specs=pl.BlockSpec((REST_TM, REST_W), lambda i: (i, 0)),
        out_shape=jax.ShapeDtypeStruct((m, REST_W), BF16),
        compiler_params=_cparams(("parallel",)),
        name="rest_proj",
    )(h, w_rest)


LOG2E = math.log2(math.e)


def _pair_attention(q_pair, k_pair, v_pair, bias_e, bias_o):
    rows = q_pair.shape[0]
    lo_q = lax.broadcasted_iota(jnp.int32, q_pair.shape, 1) < HEAD_DIM
    zero = jnp.zeros_like(q_pair)
    q_both = jnp.concatenate([jnp.where(lo_q, q_pair, zero), jnp.where(lo_q, zero, q_pair)], axis=0)
    s = lax.dot_general(q_both, k_pair, (((1,), (1,)), ((), ())), preferred_element_type=F32)
    s_e, s_o = s[:rows], s[rows:]
    if bias_e is not None:
        s_e = s_e + bias_e
        s_o = s_o + bias_o
    mx_e = jnp.max(s_e, axis=-1, keepdims=True)
    mx_o = jnp.max(s_o, axis=-1, keepdims=True)
    p_e = jnp.exp2(s_e - mx_e).astype(BF16)
    p_o = jnp.exp2(s_o - mx_o).astype(BF16)
    lo_v = lax.broadcasted_iota(jnp.int32, v_pair.shape, 1) < HEAD_DIM
    one = jnp.ones_like(v_pair)
    r_e = jnp.dot(p_e, jnp.where(lo_v, v_pair, one), preferred_element_type=F32)
    r_o = jnp.dot(p_o, jnp.where(lo_v, one, v_pair), preferred_element_type=F32)
    return r_e, r_o, mx_e, mx_o


def _stat_lane(head):
    return head + HEAD_DIM if head % 2 == 0 else head


STAT_DEN_SHIFT = 16


DIL_UNITS = {1: 8, 4: 4, 16: 8}


def _dil_init_bias(rb_ref, sel_ref, mrow_ref, bias_ref):
    rb_hi, rb_lo = _split_hi_lo(rb_ref[...])
    f = (jnp.dot(rb_hi, sel_ref[...], preferred_element_type=F32)
         + jnp.dot(rb_lo, sel_ref[...], preferred_element_type=F32)) * LOG2E + mrow_ref[...]
    col = lax.broadcasted_iota(jnp.int32, (DIL_BLOCK, 2 * DIL_BLOCK), 1)
    for h in range(N_HEADS_DIL):
        row = jnp.broadcast_to(f[h:h + 1, :], (DIL_BLOCK, 2 * DIL_BLOCK))
        tab = pltpu.roll(row, 0, 1, stride=1, stride_axis=0)
        bias_ref[0, h] = tab
        bias_ref[1, h] = jnp.where(col >= DIL_BLOCK, tab, NEG_INF)


def _dil_unit(q, kp, kc, vp, vc, first, bias_ref, put_num):
    lane = lax.broadcasted_iota(jnp.int32, (DIL_BLOCK, LANES), 1)
    lo = lane < HEAD_DIM
    st_c = jnp.zeros((DIL_BLOCK, LANES), F32)
    for j in range(N_HEADS_DIL // 2):
        sl = slice(j * LANES, (j + 1) * LANES)
        if kp is None:
            k_pair, v_pair = kc(sl), vc(sl)
            bias_e, bias_o = bias_ref[0, 2 * j, :, DIL_BLOCK:], bias_ref[0, 2 * j + 1, :, DIL_BLOCK:]
        else:
            k_pair = jnp.concatenate([kp(sl), kc(sl)], axis=0)
            v_pair = jnp.concatenate([vp(sl), vc(sl)], axis=0)
            bias_e, bias_o = bias_ref[first, 2 * j], bias_ref[first, 2 * j + 1]
        r_e, r_o, mx_e, mx_o = _pair_attention(q(sl), k_pair, v_pair, bias_e, bias_o)
        put_num(j, jnp.where(lo, r_e, r_o))
        l_e, l_o = _stat_lane(2 * j), _stat_lane(2 * j + 1)
        st_c = jnp.where(lane == l_e, mx_e, jnp.where(lane == l_e + STAT_DEN_SHIFT, r_e, st_c))
        st_c = jnp.where(lane == l_o, mx_o, jnp.where(lane == l_o + STAT_DEN_SHIFT, r_o, st_c))
    return st_c


def _mem_kv(mem_ref, mnw_ref, wkv_ref, kv_ref):
    m = mem_ref[0]
    ms = jnp.mean(m * m, axis=-1, keepdims=True)
    mn = (m * lax.rsqrt(ms + EPS) * mnw_ref[...]).astype(BF16)
    kv_ref[...] = jnp.dot(mn, wkv_ref[...].astype(BF16), preferred_element_type=F32).astype(BF16)


def _mem_unit(q, kv_ref, put_num):
    lane = lax.broadcasted_iota(jnp.int32, (DIL_BLOCK, LANES), 1)
    lo = lane < HEAD_DIM
    den_c = jnp.ones((DIL_BLOCK, LANES), F32)
    for j in range(N_HEADS_MEM // 2):
        sl = slice(j * LANES, (j + 1) * LANES)
        r_e, r_o, _, _ = _pair_attention(q(sl), kv_ref[:, sl], kv_ref[:, D_MEM + j * LANES:D_MEM + (j + 1) * LANES],
                                         None, None)
        put_num(j, jnp.where(lo, r_e, r_o))
        den_c = jnp.where(lane == _stat_lane(2 * j), r_e, jnp.where(lane == _stat_lane(2 * j + 1), r_o, den_c))
    return den_c


def _dil1_kernel(q_ref, kp_ref, kc_ref, vp_ref, vc_ref, rb_ref, sel_ref, mrow_ref, mq_ref, mem_ref, mnw_ref, wkv_ref,
                 o_ref, st_ref, om_ref, dm_ref, bias_ref, kv_ref):
    @pl.when((pl.program_id(0) == 0) & (pl.program_id(1) == 0))
    def _():
        _dil_init_bias(rb_ref, sel_ref, mrow_ref, bias_ref)

    @pl.when(pl.program_id(1) == 0)
    def _():
        _mem_kv(mem_ref, mnw_ref, wkv_ref, kv_ref)

    for u in range(DIL_UNITS[1]):
        rows = slice(u * DIL_BLOCK, (u + 1) * DIL_BLOCK)
        prev = slice((u - 1) * DIL_BLOCK, u * DIL_BLOCK)
        if u == 0:
            first = (pl.program_id(1) == 0).astype(jnp.int32)
            kp, vp = (lambda sl: kp_ref[:, sl]), (lambda sl: vp_ref[:, sl])
        else:
            first = 0
            kp, vp = (lambda sl, p=prev: kc_ref[p, sl]), (lambda sl, p=prev: vc_ref[p, sl])

        def put_num(j, num, rows=rows):
            o_ref[rows, j * LANES:(j + 1) * LANES] = num.astype(o_ref.dtype)

        st_ref[rows, :] = _dil_unit(lambda sl, r=rows: q_ref[r, sl], kp, lambda sl, r=rows: kc_ref[r, sl],
                                    vp, lambda sl, r=rows: vc_ref[r, sl], first, bias_ref, put_num)

        def put_mem(j, num, rows=rows):
            om_ref[rows, j * LANES:(j + 1) * LANES] = num.astype(om_ref.dtype)

        dm_ref[rows, :] = _mem_unit(lambda sl, r=rows: mq_ref[r, sl], kv_ref, put_mem)


DIL_LBLOCKS = {4: 2, 16: 2}


def _dil_kernel(*refs, dil, whole_seq):
    if whole_seq:
        q_ref, kc_ref, vc_ref, *rest = refs
        kp_ref = vp_ref = None
    else:
        q_ref, kp_ref, kc_ref, vp_ref, vc_ref, *rest = refs
    rb_ref, sel_ref, mrow_ref, o_ref, st_ref, bias_ref, acc_ref, tmp_ref = rest
    b, lb, g = pl.program_id(0), pl.program_id(1), pl.program_id(2)
    nr, nl = DIL_UNITS[dil], DIL_LBLOCKS[dil]
    gather = whole_seq and dil == 2 * F32_SUBLANES
    n_pairs = N_HEADS_DIL // 2

    @pl.when((b == 0) & (lb == 0) & (g == 0))
    def _():
        _dil_init_bias(rb_ref, sel_ref, mrow_ref, bias_ref)

    for ll in range(nl):
        sub = slice(ll * DIL_BLOCK, (ll + 1) * DIL_BLOCK)
        prev = slice((ll - 1) * DIL_BLOCK, ll * DIL_BLOCK)
        first = (lb == 0).astype(jnp.int32) if ll == 0 else 0
        for u in range(nr):
            rows = pl.ds(ll * DIL_BLOCK * dil + g * nr + u, DIL_BLOCK, stride=dil)
            if ll == 0 and whole_seq:
                kp = vp = None
            elif ll == 0:
                kp, vp = (lambda sl, u=u: kp_ref[u, :, sl]), (lambda sl, u=u: vp_ref[u, :, sl])
            else:
                kp = lambda sl, u=u, p=prev: kc_ref[u, p, sl]
                vp = lambda sl, u=u, p=prev: vc_ref[u, p, sl]

            if gather:
                def put_num(j, num, ll=ll, u=u):
                    for c in range(DIL_BLOCK // dil):
                        at = pl.multiple_of((ll * (DIL_BLOCK // dil) + c) * dil * dil + (g * nr + u) * dil, dil)
                        acc_ref[j, pl.ds(at, dil), :] = num[c * dil:(c + 1) * dil]
            else:
                def put_num(j, num, rows=rows):
                    acc_ref[j, rows, :] = num

            stats = _dil_unit(lambda sl, u=u, r=sub: q_ref[u, r, sl], kp,
                              lambda sl, u=u, r=sub: kc_ref[u, r, sl], vp,
                              lambda sl, u=u, r=sub: vc_ref[u, r, sl], first, bias_ref, put_num)
            if gather:
                put_num(n_pairs, stats)
            else:
                st_ref[rows, :] = stats

    @pl.when(g == dil // nr - 1)
    def _():
        if gather:
            def body(mm, carry):
                for j in range(n_pairs + 1):
                    for i0 in range(4):
                        tmp_ref[j, i0] = acc_ref[j, pl.ds(mm * dil * dil + i0, dil * 4, stride=4), :]
                    for i in range(dil):
                        group = pl.ds(pl.multiple_of((mm * dil + i) * dil, dil), dil)
                        val = tmp_ref[j, i % 4, pl.ds(i // 4, dil, stride=4), :]
                        if j < n_pairs:
                            o_ref[group, j * LANES:(j + 1) * LANES] = val.astype(o_ref.dtype)
                        else:
                            st_ref[group, :] = val
                return carry
            lax.fori_loop(0, nl * DIL_BLOCK // dil, body, 0)
        else:
            for j in range(N_HEADS_DIL // 2):
                o_ref[:, j * LANES:(j + 1) * LANES] = acc_ref[j].astype(o_ref.dtype)


def _t5_bucket_np(dist):
    max_exact = REL_BUCKETS // 2
    n = np.maximum(dist, 1).astype(np.float32)
    large = max_exact + (np.log(n / np.float32(max_exact)) / np.float32(math.log(REL_MAX_DIST / max_exact))
                         * np.float32(REL_BUCKETS - max_exact)).astype(np.int32)
    large = np.minimum(large, REL_BUCKETS - 1)
    return np.where(dist < max_exact, dist, large)


def _bias_selector(window, dil):
    back = window // dil
    x = np.arange(2 * DIL_BLOCK)
    rel = DIL_BLOCK - x
    valid = (rel >= 0) & (rel <= back)
    bucket = _t5_bucket_np(np.clip(rel, 0, back) * dil)
    sel = np.zeros((LANES, 2 * DIL_BLOCK), np.float32)
    sel[bucket[valid], x[valid]] = 1.0
    mrow = np.where(valid, 0.0, NEG_INF).astype(np.float32)[None]
    return jnp.asarray(sel, BF16), jnp.asarray(mrow)


def _dilated(src, rb_t, batch, seq, window, dil, mem_args=None):
    sub_len = seq // dil
    nb = sub_len // DIL_BLOCK
    sel, mrow = _bias_selector(window, dil)
    nu = DIL_UNITS[dil]
    consts = (rb_t, sel, mrow)
    const = lambda a: pl.BlockSpec(a.shape, lambda *_: (0, 0))
    stat_shape = jax.ShapeDtypeStruct((batch * seq, LANES), F32)
    out_shape = [jax.ShapeDtypeStruct((batch * seq, D_DIL), BF16), stat_shape]
    bias_scratch = pltpu.VMEM((2, N_HEADS_DIL, DIL_BLOCK, 2 * DIL_BLOCK), F32)

    if dil == 1:
        rest, mem, mem_norm_w, w_kv = mem_args
        rows = nu * DIL_BLOCK
        steps = nb // nu
        cur = lambda col: pl.BlockSpec((None, None, rows, D_DIL), lambda b, l: (b, 0, l, col))
        prev = lambda col: pl.BlockSpec((None, None, DIL_BLOCK, D_DIL),
                                        lambda b, l: (b, 0, jnp.maximum(nu * l - 1, 0), col))
        out = lambda width, col=0: pl.BlockSpec((rows, width), lambda b, l: (b * steps + l, col // width))
        mem_specs = [out(D_MEM, COL_MQ), pl.BlockSpec((1, MEM_LEN, D_MODEL), lambda b, l: (b, 0, 0)),
                     const(mem_norm_w), const(w_kv)]
        num, stats, mem_num, mem_den = pl.pallas_call(
            _dil1_kernel,
            grid=(batch, steps),
            in_specs=[cur(0), prev(1), cur(1), prev(2), cur(2)] + [const(a) for a in consts] + mem_specs,
            out_specs=[out(D_DIL), out(LANES), out(D_MEM), out(LANES)],
            out_shape=out_shape + [jax.ShapeDtypeStruct((batch * seq, D_MEM), BF16), stat_shape],
            scratch_shapes=[bias_scratch, pltpu.VMEM((MEM_LEN, 2 * D_MEM), BF16)],
            compiler_params=_cparams(("arbitrary", "arbitrary")),
            name="dilated_d1",
        )(src, src, src, src, src, *consts, rest, mem, mem_norm_w, w_kv)
        return (num, stats), (mem_num, mem_den)

    nl = DIL_LBLOCKS[dil]
    steps = nb // nl
    rows = nl * DIL_BLOCK * dil
    cur = lambda col: pl.BlockSpec((None, nu, nl * DIL_BLOCK, D_DIL), lambda b, l, g: (b, g, l, col))
    prev = lambda col: pl.BlockSpec((None, nu, DIL_BLOCK, D_DIL),
                                    lambda b, l, g: (b, g, jnp.maximum(nl * l - 1, 0), col))
    out = lambda width: pl.BlockSpec((rows, width), lambda b, l, g: (b * steps + l, 0))
    whole_seq = steps == 1
    qkv_specs = [cur(0), cur(1), cur(2)] if whole_seq else [cur(0), prev(1), cur(1), prev(2), cur(2)]
    return pl.pallas_call(
        functools.partial(_dil_kernel, dil=dil, whole_seq=whole_seq),
        grid=(batch, steps, dil // nu),
        in_specs=qkv_specs + [const(a) for a in consts],
        out_specs=[out(D_DIL), out(LANES)],
        out_shape=out_shape,
        scratch_shapes=[bias_scratch, pltpu.VMEM((N_HEADS_DIL // 2 + 1, rows, LANES), F32),
                        pltpu.VMEM((N_HEADS_DIL // 2 + 1, 4, 4 * dil, LANES), F32)],
        compiler_params=_cparams(("arbitrary", "arbitrary", "arbitrary")),
        name=f"dilated_d{dil}",
    )(*([src] * len(qkv_specs)), *consts)


GLA_TM = 256
GLA_STEP_TILES = 2


GLA_NC = GLA_TM // GLA_CHUNK


def _gla_tile(bb, rows, w2, q_ref, k_ref, v_ref, g_ref, bg_ref, nw_ref, tri_ref, o_ref, st_ref):
    g = g_ref[bb, rows, :LANES]
    z = jnp.dot(g, w2, preferred_element_type=F32) + bg_ref[...]
    la = (jnp.minimum(z, 0.0) * (LOG2E / GLA_TAU)
          - jnp.log2(1.0 + jnp.exp2(jnp.abs(z) * -LOG2E)) * (1.0 / GLA_TAU))
    la_h = la.astype(BF16)
    la_l = (la - la_h.astype(F32)).astype(BF16)

    tri = tri_ref[...]
    b = jnp.dot(tri, la_h, preferred_element_type=F32) + jnp.dot(tri, la_l, preferred_element_type=F32)
    b_last = jnp.concatenate(
        [jnp.broadcast_to(b[(ci + 1) * GLA_CHUNK - 1:(ci + 1) * GLA_CHUNK, :], (GLA_CHUNK, D_GLA_K))
         for ci in range(GLA_NC)], axis=0)

    causal = tri > 0
    t_chunk = lax.broadcasted_iota(jnp.int32, (GLA_DK, GLA_TM), 1) // GLA_CHUNK
    nw = nw_ref[...]
    for h in range(N_HEADS_GLA):
        cs = slice(h * GLA_DK, (h + 1) * GLA_DK)
        vs = slice(h * GLA_DV, (h + 1) * GLA_DV)
        v_h = v_ref[bb, rows, vs]
        b_h, b_last_h = b[:, cs], b_last[:, cs]
        q = q_ref[bb, rows, cs].astype(F32)
        k = k_ref[bb, rows, cs].astype(F32)
        q_t = (q * jnp.exp2(b_h)).astype(BF16)
        k_t = (k * jnp.exp2(-b_h)).astype(BF16)
        k_s = k * jnp.exp2(b_last_h - b_h)
        a = lax.dot_general(q_t, k_t, (((1,), (1,)), ((), ())), preferred_element_type=F32)
        o = jnp.dot(jnp.where(causal, a.astype(BF16), jnp.zeros_like(tri)), v_h, preferred_element_type=F32)
        k_s_t = k_s.T.astype(BF16)
        zero = jnp.zeros_like(k_s_t)
        kv_all = jnp.dot(jnp.concatenate([jnp.where(t_chunk == ci, k_s_t, zero) for ci in range(GLA_NC)], axis=0),
                         v_h, preferred_element_type=F32)
        decay_t = jnp.exp2(b_last_h.T)
        st = st_ref[bb, h]
        outs = []
        for ci in range(GLA_NC):
            rs = slice(ci * GLA_CHUNK, (ci + 1) * GLA_CHUNK)
            o_c = o[rs] + jnp.dot(q_t[rs], st.astype(BF16), preferred_element_type=F32)
            decay = jnp.broadcast_to(decay_t[:, ci * GLA_CHUNK:ci * GLA_CHUNK + 1], (GLA_DK, GLA_DV))
            st = decay * st + kv_all[ci * GLA_DK:(ci + 1) * GLA_DK]
            ms = jnp.mean(o_c * o_c, axis=-1, keepdims=True)
            outs.append((o_c * lax.rsqrt(ms + EPS) * nw).astype(o_ref.dtype))
        st_ref[bb, h] = st
        o_ref[bb, rows, vs] = jnp.concatenate(outs, axis=0)


def _gla_kernel(q_ref, k_ref, v_ref, g_ref, w2_ref, bg_ref, nw_ref, tri_ref, wout_ref,
                o_ref, wout_bf_ref, st_ref):

    @pl.when(pl.program_id(0) == 0)
    def _():
        st_ref[...] = jnp.zeros_like(st_ref)

    wout_bf_ref[...] = wout_ref[...].astype(wout_bf_ref.dtype)

    w2 = jnp.concatenate([w2_ref[...].astype(BF16), jnp.zeros((LANES - GLA_GATE_RANK, D_GLA_K), BF16)], axis=0)
    for tt in range(GLA_STEP_TILES):
        rows = slice(tt * GLA_TM, (tt + 1) * GLA_TM)
        for bb in range(st_ref.shape[0]):
            _gla_tile(bb, rows, w2, q_ref, k_ref, v_ref, g_ref, bg_ref, nw_ref, tri_ref, o_ref, st_ref)


def _gla(proj, w2, b_gate, norm_w, w_out, batch, seq):
    step_rows = GLA_TM * GLA_STEP_TILES
    steps = seq // step_rows
    wout_rows = w_out.shape[0] // steps
    pos = np.arange(GLA_TM)
    tri = jnp.asarray((pos[:, None] // GLA_CHUNK == pos[None, :] // GLA_CHUNK) & (pos[None, :] <= pos[:, None]),
                      BF16)
    proj3 = proj.reshape(batch, seq, REST_W)
    rows = lambda width, col: pl.BlockSpec((batch, step_rows, width), lambda i: (0, i, col // width))
    const = lambda shape: pl.BlockSpec(shape, lambda i: (0, 0))
    wout_spec = pl.BlockSpec((wout_rows, w_out.shape[1]), lambda i: (i, 0))
    out, w_out_bf = pl.pallas_call(
        _gla_kernel,
        grid=(steps,),
        in_specs=[
            rows(D_GLA_K, COL_GQ), rows(D_GLA_K, COL_GK), rows(D_GLA_V, COL_GV), rows(GLR_PAD, COL_GLR),
            const((GLA_GATE_RANK, D_GLA_K)), const((1, D_GLA_K)), const((1, GLA_DV)),
            const((GLA_TM, GLA_TM)), wout_spec,
        ],
        out_specs=[rows(D_GLA_V, 0), wout_spec],
        out_shape=[jax.ShapeDtypeStruct((batch, seq, D_GLA_V), BF16), jax.ShapeDtypeStruct(w_out.shape, BF16)],
        scratch_shapes=[pltpu.VMEM((batch, N_HEADS_GLA, GLA_DK, GLA_DV), F32)],
        compiler_params=_cparams(("arbitrary",)),
        name="gla",
    )(proj3, proj3, proj3, proj3, w2, b_gate, norm_w, tri, w_out)
    return out.reshape(batch * seq, D_GLA_V), w_out_bf


OUT_TM = 512


def _silu(g):
    h = g * 0.5
    return h * jnp.tanh(h) + h


OUT_GW = 256


def _out_kernel(n1_ref, n2_ref, n3_ref, s1_ref, s2_ref, s3_ref, og_ref, nm_ref, dm_ref, ga_ref, gg_ref, gm_ref,
                sm_ref, ex_ref, exm_ref, w_ref, x_ref, nw_ref, out_ref, mix_ref):
    keep = sm_ref[0:1, :]
    fill = sm_ref[1:2, :]
    stats = [s_ref[...] for s_ref in (s1_ref, s2_ref, s3_ref)]
    m1, m2, m3 = [st * keep for st in stats]
    d1, d2, d3 = [pltpu.roll(st, LANES - STAT_DEN_SHIFT, 1) * keep + fill for st in stats]
    mx = jnp.maximum(jnp.maximum(m1, m2), m3)
    t1, t2, t3 = jnp.exp2(m1 - mx), jnp.exp2(m2 - mx), jnp.exp2(m3 - mx)
    inv = 1.0 / (t1 * d1 + t2 * d2 + t3 * d3)
    wts = [(t * inv).astype(BF16) for t in (t1, t2, t3)]
    inv_m = (1.0 / dm_ref[...]).astype(BF16)

    def gate_chunk(lo):
        cols = slice(lo, lo + OUT_GW)
        if lo < D_DIL:
            a = jnp.zeros((OUT_TM, OUT_GW), F32)
            for wt, n_ref in zip(wts, (n1_ref, n2_ref, n3_ref)):
                a = a + jnp.dot(wt, ex_ref[:, cols], preferred_element_type=F32) * n_ref[:, cols].astype(F32)
            g = ga_ref[:, cols]
        elif lo < D_DIL + D_GLA_V:
            lc = slice(lo - D_DIL, lo - D_DIL + OUT_GW)
            a, g = og_ref[:, lc], gg_ref[:, lc]
        else:
            a = jnp.dot(inv_m, exm_ref[...], preferred_element_type=F32) * nm_ref[...].astype(F32)
            g = gm_ref[...]
        mix_ref[:, cols] = a.astype(BF16) * _silu(g)

    for lo in range(0, D_MIX, OUT_GW):
        gate_chunk(lo)
    y = jnp.dot(mix_ref[...], w_ref[...], preferred_element_type=F32)
    ms = jnp.mean(y * y, axis=-1, keepdims=True)
    out_ref[...] = x_ref[...] + y * lax.rsqrt(ms + EPS) * nw_ref[...]


def _out_proj(pats, o_gla, mem_pair, proj, stat_mask, expand, expand_mem, w_out, x2d, norm_w):
    m = x2d.shape[0]
    rows = lambda width, col=0: pl.BlockSpec((OUT_TM, width), lambda s: (s, col // width))
    const = lambda shape: pl.BlockSpec(shape, lambda s: (0, 0), pipeline_mode=pl.Buffered(1))
    nums, stats = zip(*pats)
    return pl.pallas_call(
        _out_kernel,
        grid=(m // OUT_TM,),
        in_specs=[
            *([rows(D_DIL)] * 3), *([rows(LANES)] * 3),
            rows(D_GLA_V), rows(D_MEM), rows(LANES),
            rows(D_DIL, COL_GATE_A), rows(D_GLA_V, COL_GATE_G), rows(D_MEM, COL_GATE_M),
            const((F32_SUBLANES, LANES)), const((LANES, D_DIL)), const((LANES, D_MEM)), const((D_MIX, D_MODEL)),
            rows(D_MODEL), const((1, D_MODEL)),
        ],
        out_specs=rows(D_MODEL),
        out_shape=jax.ShapeDtypeStruct((m, D_MODEL), x2d.dtype),
        scratch_shapes=[pltpu.VMEM((OUT_TM, D_MIX), BF16)],
        compiler_params=_cparams(("parallel",)),
        name="out_proj",
    )(*nums, *stats, o_gla, *mem_pair, proj, proj, proj, stat_mask, expand, expand_mem, w_out, x2d, norm_w)


D_IN_PROJ = 3 * D_DIL + 2 * D_GLA_K + D_GLA_V + GLA_GATE_RANK + D_MEM + D_MIX
SRC_GLR = 3 * D_DIL + 2 * D_GLA_K + D_GLA_V
SRC_MQ = SRC_GLR + GLA_GATE_RANK
SRC_GATE = SRC_MQ + D_MEM
SCALE_ATTN_Q = HEAD_DIM ** -0.5 * LOG2E
SCALE_GLA_Q = GLA_DK ** -0.5
QKV_SEGMENTS = (
    (0, D_DIL, 0, SCALE_ATTN_Q),
    (D_DIL, 2 * D_DIL, D_DIL, 1.0),
)
REST_SEGMENTS = (
    (QKV_W, D_GLA_K, COL_GQ, SCALE_GLA_Q),
    (QKV_W + D_GLA_K, D_GLA_K + D_GLA_V, COL_GK, 1.0),
    (SRC_GATE + D_DIL, D_GLA_V, COL_GATE_G, 1.0),
    (SRC_MQ, D_MEM, COL_MQ, SCALE_ATTN_Q),
    (SRC_GATE + D_DIL + D_GLA_V, D_MEM, COL_GATE_M, 1.0),
    (SRC_GATE, D_DIL, COL_GATE_A, 1.0),
)
WPREP_TK = 512


def _wprep_kernel(wt_ref, qkv_ref, rest_ref):
    for o_ref, segments in ((qkv_ref, QKV_SEGMENTS), (rest_ref, REST_SEGMENTS)):
        for src, width, dst, scale in segments:
            for lo in range(0, width, LANES):
                w = wt_ref[src + lo:src + lo + LANES, :].T
                if scale != 1.0:
                    w = w * scale
                o_ref[:, dst + lo:dst + lo + LANES] = w.astype(o_ref.dtype)
    glr = wt_ref[SRC_GLR:SRC_GLR + LANES, :].T
    keep = lax.broadcasted_iota(jnp.int32, glr.shape, 1) < GLA_GATE_RANK
    rest_ref[:, COL_GLR:COL_GLR + LANES] = jnp.where(keep, glr, 0.0).astype(rest_ref.dtype)
    rest_ref[:, COL_GLR + LANES:COL_GLR + GLR_PAD] = jnp.zeros((WPREP_TK, GLR_PAD - LANES), rest_ref.dtype)


def _regroup_in_weights(w_in):
    return pl.pallas_call(
        _wprep_kernel,
        grid=(D_MODEL // WPREP_TK,),
        in_specs=[pl.BlockSpec((D_IN_PROJ, WPREP_TK), lambda i: (0, i))],
        out_specs=[pl.BlockSpec((WPREP_TK, QKV_W), lambda i: (i, 0)),
                   pl.BlockSpec((WPREP_TK, REST_W), lambda i: (i, 0))],
        out_shape=[jax.ShapeDtypeStruct((D_MODEL, QKV_W), BF16), jax.ShapeDtypeStruct((D_MODEL, REST_W), BF16)],
        compiler_params=_cparams(("parallel",)),
        name="weight_regroup",
    )(w_in.T)


def kernel(x, mem, norm_pre_w, w_in, rel_bias, w_gla_gate2, b_gla_gate, gla_norm_w, mem_norm_w, w_mem_kv,
           w_out, norm_post_w):
    batch, seq, _ = x.shape
    depth = w_in.shape[0]
    expand_np = np.zeros((LANES, D_DIL), np.float32)
    for h in range(N_HEADS_DIL):
        expand_np[_stat_lane(h), h * HEAD_DIM:(h + 1) * HEAD_DIM] = 1.0
    expand = jnp.asarray(expand_np, BF16)
    expand_mem = jnp.asarray(expand_np[:, :D_MEM], BF16)
    stat_mask_np = np.zeros((F32_SUBLANES, LANES), np.float32)
    stat_mask_np[0, [_stat_lane(h) for h in range(N_HEADS_DIL)]] = 1.0
    stat_mask_np[1] = 1.0 - stat_mask_np[0]
    stat_mask = jnp.asarray(stat_mask_np)
    rb_t = jnp.zeros((2 * F32_SUBLANES, LANES), F32).at[:N_HEADS_DIL, :REL_BUCKETS].set(rel_bias.astype(F32).T)
    for l in range(depth):
        x2d = x.reshape(batch * seq, D_MODEL)
        w_qkv, w_rest = _regroup_in_weights(w_in[l])
        h, qkv1, qkv4, qkv16 = _qkv_proj(x2d, norm_pre_w[l][None], w_qkv, batch, seq)
        rest = _rest_proj(h, w_rest)

        srcs = (qkv1.reshape(batch, 1, seq, QKV_W), qkv4, qkv16)
        mem_args = (rest, mem, mem_norm_w[l][None], w_mem_kv[l])
        pats = [_dilated(src, rb_t, batch, seq, window, dil, mem_args if dil == 1 else None)
                for (window, dil), src in zip(DIL_PATTERNS, srcs)]
        pats[0], mem_pair = pats[0]

        o_gla, w_out_bf = _gla(rest, w_gla_gate2[l], b_gla_gate[l][None], gla_norm_w[l][None], w_out[l],
                               batch, seq)

        out = _out_proj(pats, o_gla, mem_pair, rest, stat_mask, expand, expand_mem, w_out_bf, x2d,
                        norm_post_w[l][None])
        x = out.reshape(batch, seq, D_MODEL)
    return x
```

```python
import functools
import math

import numpy as np
import jax
import jax.numpy as jnp
from jax import lax
from jax.experimental import pallas as pl
from jax.experimental.pallas import tpu as pltpu

F32 = jnp.float32
BF16 = jnp.bfloat16

D_MODEL = 2048
HEAD_DIM = 64
N_HEADS_DIL = 12
DIL_PATTERNS = ((128, 1), (512, 4), (2048, 16))
DIL_BLOCK = 128
N_HEADS_GLA = 4
GLA_DK = 128
GLA_DV = 256
GLA_GATE_RANK = 16
GLA_TAU = 16.0
GLA_CHUNK = 64
N_HEADS_MEM = 4
MEM_LEN = 256
REL_BUCKETS = 32
REL_MAX_DIST = 2048
EPS = 1e-6
NEG_INF = -1e30

D_DIL = N_HEADS_DIL * HEAD_DIM
D_GLA_K = N_HEADS_GLA * GLA_DK
D_GLA_V = N_HEADS_GLA * GLA_DV
D_MEM = N_HEADS_MEM * HEAD_DIM
D_MIX = D_DIL + D_GLA_V + D_MEM

LANES = 128
F32_SUBLANES = 8
GLR_PAD = 256

QKV_W = 3 * D_DIL
COL_GQ = 0
COL_GK = COL_GQ + D_GLA_K
COL_GV = COL_GK + D_GLA_K
COL_GATE_G = COL_GV + D_GLA_V
COL_MQ = COL_GATE_G + D_GLA_V
COL_GATE_M = COL_MQ + D_MEM
COL_GLR = COL_GATE_M + D_MEM
COL_GATE_A = COL_GLR + GLR_PAD
REST_W = COL_GATE_A + D_DIL

VMEM_LIMIT = 56 * 1024 * 1024


def _cparams(sem):
    return pltpu.CompilerParams(dimension_semantics=sem, vmem_limit_bytes=VMEM_LIMIT)


def _split_hi_lo(w):
    hi = w.astype(BF16)
    return hi, (w - hi.astype(F32)).astype(BF16)


QKV_TM = 512
N_SLABS = D_DIL // LANES
REGROUP_DILS = (4, 16)


def _qkv_proj_kernel(x_ref, nw_ref, w_ref, h_ref, o_ref, x4_ref, x16_ref, slab_ref, sub_ref):
    x = x_ref[...]
    ms = jnp.mean(x * x, axis=-1, keepdims=True)
    h_ref[...] = (x * lax.rsqrt(ms + EPS) * nw_ref[...]).astype(BF16)
    for t in range(QKV_W // D_DIL):
        cols = slice(t * D_DIL, (t + 1) * D_DIL)
        res = jnp.dot(h_ref[...], w_ref[:, cols], preferred_element_type=F32)
        o_ref[:, cols] = res.astype(o_ref.dtype)
        for s in range(N_SLABS):
            slab_ref[t, s] = res[:, s * LANES:(s + 1) * LANES]
        for s in range(N_SLABS):
            lo = t * D_DIL + s * LANES
            for r in range(4):
                sub = slab_ref[t, s, pl.ds(r, QKV_TM // 4, stride=4), :]
                x4_ref[0, r, :, lo:lo + LANES] = sub.astype(x4_ref.dtype)
                sub_ref[t, s, r] = sub
            for r in range(4):
                for q in range(4):
                    x16_ref[0, 4 * q + r, :, lo:lo + LANES] = (
                        sub_ref[t, s, r, pl.ds(q, QKV_TM // 16, stride=4), :].astype(x16_ref.dtype))


def _qkv_proj(x2d, norm_w, w_qkv, batch, seq):
    m = x2d.shape[0]
    tiles_per_seq = seq // QKV_TM
    const = lambda shape: pl.BlockSpec(shape, lambda i: (0, 0), pipeline_mode=pl.Buffered(1))
    grouped = lambda dil: pl.BlockSpec((1, dil, QKV_TM // dil, QKV_W),
                                       lambda i: (i // tiles_per_seq, 0, i % tiles_per_seq, 0))
    return pl.pallas_call(
        _qkv_proj_kernel,
        grid=(m // QKV_TM,),
        in_specs=[
            pl.BlockSpec((QKV_TM, D_MODEL), lambda i: (i, 0)),
            const((1, D_MODEL)), const((D_MODEL, QKV_W)),
        ],
        out_specs=[pl.BlockSpec((QKV_TM, D_MODEL), lambda i: (i, 0)),
                   pl.BlockSpec((QKV_TM, QKV_W), lambda i: (i, 0))] + [grouped(d) for d in REGROUP_DILS],
        out_shape=[jax.ShapeDtypeStruct((m, D_MODEL), BF16), jax.ShapeDtypeStruct((m, QKV_W), BF16)]
        + [jax.ShapeDtypeStruct((batch, d, seq // d, QKV_W), BF16) for d in REGROUP_DILS],
        scratch_shapes=[pltpu.VMEM((QKV_W // D_DIL, N_SLABS, QKV_TM, LANES), F32),
                        pltpu.VMEM((QKV_W // D_DIL, N_SLABS, 4, QKV_TM // 4, LANES), F32)],
        compiler_params=_cparams(("parallel",)),
        name="qkv_proj",
    )(x2d, norm_w, w_qkv)


REST_TM = 512


def _rest_proj_kernel(h_ref, w_ref, o_ref):
    o_ref[...] = jnp.dot(h_ref[...], w_ref[...], preferred_element_type=F32).astype(o_ref.dtype)


def _rest_proj(h, w_rest):
    m = h.shape[0]
    const = lambda shape: pl.BlockSpec(shape, lambda i: (0, 0), pipeline_mode=pl.Buffered(1))
    return pl.pallas_call(
        _rest_proj_kernel,
        grid=(m // REST_TM,),
        in_specs=[pl.BlockSpec((REST_TM, D_MODEL), lambda i: (i, 0)), const((D_MODEL, REST_W))],
        out_specs=pl.BlockSpec((REST_TM, REST_W), lambda i: (i, 0)),
        out_shape=jax.ShapeDtypeStruct((m, REST_W), BF16),
        compiler_params=_cparams(("parallel",)),
        name="rest_proj",
    )(h, w_rest)


LOG2E = math.log2(math.e)


def _pair_attention(q_pair, k_pair, v_pair, bias_e, bias_o):
    rows = q_pair.shape[0]
    lo_q = lax.broadcasted_iota(jnp.int32, q_pair.shape, 1) < HEAD_DIM
    zero = jnp.zeros_like(q_pair)
    q_both = jnp.concatenate([jnp.where(lo_q, q_pair, zero), jnp.where(lo_q, zero, q_pair)], axis=0)
    s = lax.dot_general(q_both, k_pair, (((1,), (1,)), ((), ())), preferred_element_type=F32)
    s_e, s_o = s[:rows], s[rows:]
    if bias_e is not None:
        s_e = s_e + bias_e
        s_o = s_o + bias_o
    mx_e = jnp.max(s_e, axis=-1, keepdims=True)
    mx_o = jnp.max(s_o, axis=-1, keepdims=True)
    p_e = jnp.exp2(s_e - mx_e).astype(BF16)
    p_o = jnp.exp2(s_o - mx_o).astype(BF16)
    lo_v = lax.broadcasted_iota(jnp.int32, v_pair.shape, 1) < HEAD_DIM
    one = jnp.ones_like(v_pair)
    r_e = jnp.dot(p_e, jnp.where(lo_v, v_pair, one), preferred_element_type=F32)
    r_o = jnp.dot(p_o, jnp.where(lo_v, one, v_pair), preferred_element_type=F32)
    return r_e, r_o, mx_e, mx_o


def _stat_lane(head):
    return head + HEAD_DIM if head % 2 == 0 else head


STAT_DEN_SHIFT = 16


DIL_UNITS = {1: 8, 4: 4, 16: 8}


def _dil_init_bias(rb_ref, sel_ref, mrow_ref, bias_ref):
    rb_hi, rb_lo = _split_hi_lo(rb_ref[...])
    f = (jnp.dot(rb_hi, sel_ref[...], preferred_element_type=F32)
         + jnp.dot(rb_lo, sel_ref[...], preferred_element_type=F32)) * LOG2E + mrow_ref[...]
    col = lax.broadcasted_iota(jnp.int32, (DIL_BLOCK, 2 * DIL_BLOCK), 1)
    for h in range(N_HEADS_DIL):
        row = jnp.broadcast_to(f[h:h + 1, :], (DIL_BLOCK, 2 * DIL_BLOCK))
        tab = pltpu.roll(row, 0, 1, stride=1, stride_axis=0)
        bias_ref[0, h] = tab
        bias_ref[1, h] = jnp.where(col >= DIL_BLOCK, tab, NEG_INF)


def _dil_unit(q, kp, kc, vp, vc, first, bias_ref, put_num):
    lane = lax.broadcasted_iota(jnp.int32, (DIL_BLOCK, LANES), 1)
    lo = lane < HEAD_DIM
    st_c = jnp.zeros((DIL_BLOCK, LANES), F32)
    for j in range(N_HEADS_DIL // 2):
        sl = slice(j * LANES, (j + 1) * LANES)
        if kp is None:
            k_pair, v_pair = kc(sl), vc(sl)
            bias_e, bias_o = bias_ref[0, 2 * j, :, DIL_BLOCK:], bias_ref[0, 2 * j + 1, :, DIL_BLOCK:]
        else:
            k_pair = jnp.concatenate([kp(sl), kc(sl)], axis=0)
            v_pair = jnp.concatenate([vp(sl), vc(sl)], axis=0)
            bias_e, bias_o = bias_ref[first, 2 * j], bias_ref[first, 2 * j + 1]
        r_e, r_o, mx_e, mx_o = _pair_attention(q(sl), k_pair, v_pair, bias_e, bias_o)
        put_num(j, jnp.where(lo, r_e, r_o))
        l_e, l_o = _stat_lane(2 * j), _stat_lane(2 * j + 1)
        st_c = jnp.where(lane == l_e, mx_e, jnp.where(lane == l_e + STAT_DEN_SHIFT, r_e, st_c))
        st_c = jnp.where(lane == l_o, mx_o, jnp.where(lane == l_o + STAT_DEN_SHIFT, r_o, st_c))
    return st_c


def _mem_kv(mem_ref, mnw_ref, wkv_ref, kv_ref):
    m = mem_ref[0]
    ms = jnp.mean(m * m, axis=-1, keepdims=True)
    mn = (m * lax.rsqrt(ms + EPS) * mnw_ref[...]).astype(BF16)
    kv_ref[...] = jnp.dot(mn, wkv_ref[...].astype(BF16), preferred_element_type=F32).astype(BF16)


def _mem_unit(q, kv_ref, put_num):
    lane = lax.broadcasted_iota(jnp.int32, (DIL_BLOCK, LANES), 1)
    lo = lane < HEAD_DIM
    den_c = jnp.ones((DIL_BLOCK, LANES), F32)
    for j in range(N_HEADS_MEM // 2):
        sl = slice(j * LANES, (j + 1) * LANES)
        r_e, r_o, _, _ = _pair_attention(q(sl), kv_ref[:, sl], kv_ref[:, D_MEM + j * LANES:D_MEM + (j + 1) * LANES],
                                         None, None)
        put_num(j, jnp.where(lo, r_e, r_o))
        den_c = jnp.where(lane == _stat_lane(2 * j), r_e, jnp.where(lane == _stat_lane(2 * j + 1), r_o, den_c))
    return den_c


def _dil1_kernel(q_ref, kp_ref, kc_ref, vp_ref, vc_ref, rb_ref, sel_ref, mrow_ref, mq_ref, mem_ref, mnw_ref, wkv_ref,
                 o_ref, st_ref, om_ref, dm_ref, bias_ref, kv_ref):
    @pl.when((pl.program_id(0) == 0) & (pl.program_id(1) == 0))
    def _():
        _dil_init_bias(rb_ref, sel_ref, mrow_ref, bias_ref)

    @pl.when(pl.program_id(1) == 0)
    def _():
        _mem_kv(mem_ref, mnw_ref, wkv_ref, kv_ref)

    for u in range(DIL_UNITS[1]):
        rows = slice(u * DIL_BLOCK, (u + 1) * DIL_BLOCK)
        prev = slice((u - 1) * DIL_BLOCK, u * DIL_BLOCK)
        if u == 0:
            first = (pl.program_id(1) == 0).astype(jnp.int32)
            kp, vp = (lambda sl: kp_ref[:, sl]), (lambda sl: vp_ref[:, sl])
        else:
            first = 0
            kp, vp = (lambda sl, p=prev: kc_ref[p, sl]), (lambda sl, p=prev: vc_ref[p, sl])

        def put_num(j, num, rows=rows):
            o_ref[rows, j * LANES:(j + 1) * LANES] = num.astype(o_ref.dtype)

        st_ref[rows, :] = _dil_unit(lambda sl, r=rows: q_ref[r, sl], kp, lambda sl, r=rows: kc_ref[r, sl],
                                    vp, lambda sl, r=rows: vc_ref[r, sl], first, bias_ref, put_num)

        def put_mem(j, num, rows=rows):
            om_ref[rows, j * LANES:(j + 1) * LANES] = num.astype(om_ref.dtype)

        dm_ref[rows, :] = _mem_unit(lambda sl, r=rows: mq_ref[r, sl], kv_ref, put_mem)


DIL_LBLOCKS = {4: 2, 16: 2}


def _dil_kernel(*refs, dil, whole_seq):
    if whole_seq:
        q_ref, kc_ref, vc_ref, *rest = refs
        kp_ref = vp_ref = None
    else:
        q_ref, kp_ref, kc_ref, vp_ref, vc_ref, *rest = refs
    rb_ref, sel_ref, mrow_ref, o_ref, st_ref, bias_ref, acc_ref, tmp_ref = rest
    b, lb, g = pl.program_id(0), pl.program_id(1), pl.program_id(2)
    nr, nl = DIL_UNITS[dil], DIL_LBLOCKS[dil]
    gather = whole_seq and dil == 2 * F32_SUBLANES
    n_pairs = N_HEADS_DIL // 2

    @pl.when((b == 0) & (lb == 0) & (g == 0))
    def _():
        _dil_init_bias(rb_ref, sel_ref, mrow_ref, bias_ref)

    for ll in range(nl):
        sub = slice(ll * DIL_BLOCK, (ll + 1) * DIL_BLOCK)
        prev = slice((ll - 1) * DIL_BLOCK, ll * DIL_BLOCK)
        first = (lb == 0).astype(jnp.int32) if ll == 0 else 0
        for u in range(nr):
            rows = pl.ds(ll * DIL_BLOCK * dil + g * nr + u, DIL_BLOCK, stride=dil)
            if ll == 0 and whole_seq:
                kp = vp = None
            elif ll == 0:
                kp, vp = (lambda sl, u=u: kp_ref[u, :, sl]), (lambda sl, u=u: vp_ref[u, :, sl])
            else:
                kp = lambda sl, u=u, p=prev: kc_ref[u, p, sl]
                vp = lambda sl, u=u, p=prev: vc_ref[u, p, sl]

            if gather:
                def put_num(j, num, ll=ll, u=u):
                    for c in range(DIL_BLOCK // dil):
                        at = pl.multiple_of((ll * (DIL_BLOCK // dil) + c) * dil * dil + (g * nr + u) * dil, dil)
                        acc_ref[j, pl.ds(at, dil), :] = num[c * dil:(c + 1) * dil]
            else:
                def put_num(j, num, rows=rows):
                    acc_ref[j, rows, :] = num

            stats = _dil_unit(lambda sl, u=u, r=sub: q_ref[u, r, sl], kp,
                              lambda sl, u=u, r=sub: kc_ref[u, r, sl], vp,
                              lambda sl, u=u, r=sub: vc_ref[u, r, sl], first, bias_ref, put_num)
            if gather:
                put_num(n_pairs, stats)
            else:
                st_ref[rows, :] = stats

    @pl.when(g == dil // nr - 1)
    def _():
        if gather:
            def body(mm, carry):
                for j in range(n_pairs + 1):
                    for i0 in range(4):
                        tmp_ref[j, i0] = acc_ref[j, pl.ds(mm * dil * dil + i0, dil * 4, stride=4), :]
                    for i in range(dil):
                        group = pl.ds(pl.multiple_of((mm * dil + i) * dil, dil), dil)
                        val = tmp_ref[j, i % 4, pl.ds(i // 4, dil, stride=4), :]
                        if j < n_pairs:
                            o_ref[group, j * LANES:(j + 1) * LANES] = val.astype(o_ref.dtype)
                        else:
                            st_ref[group, :] = val
                return carry
            lax.fori_loop(0, nl * DIL_BLOCK // dil, body, 0)
        else:
            for j in range(N_HEADS_DIL // 2):
                o_ref[:, j * LANES:(j + 1) * LANES] = acc_ref[j].astype(o_ref.dtype)


def _t5_bucket_np(dist):
    max_exact = REL_BUCKETS // 2
    n = np.maximum(dist, 1).astype(np.float32)
    large = max_exact + (np.log(n / np.float32(max_exact)) / np.float32(math.log(REL_MAX_DIST / max_exact))
                         * np.float32(REL_BUCKETS - max_exact)).astype(np.int32)
    large = np.minimum(large, REL_BUCKETS - 1)
    return np.where(dist < max_exact, dist, large)


def _bias_selector(window, dil):
    back = window // dil
    x = np.arange(2 * DIL_BLOCK)
    rel = DIL_BLOCK - x
    valid = (rel >= 0) & (rel <= back)
    bucket = _t5_bucket_np(np.clip(rel, 0, back) * dil)
    sel = np.zeros((LANES, 2 * DIL_BLOCK), np.float32)
    sel[bucket[valid], x[valid]] = 1.0
    mrow = np.where(valid, 0.0, NEG_INF).astype(np.float32)[None]
    return jnp.asarray(sel, BF16), jnp.asarray(mrow)


def _dilated(src, rb_t, batch, seq, window, dil, mem_args=None):
    sub_len = seq // dil
    nb = sub_len // DIL_BLOCK
    sel, mrow = _bias_selector(window, dil)
    nu = DIL_UNITS[dil]
    consts = (rb_t, sel, mrow)
    const = lambda a: pl.BlockSpec(a.shape, lambda *_: (0, 0))
    stat_shape = jax.ShapeDtypeStruct((batch * seq, LANES), F32)
    out_shape = [jax.ShapeDtypeStruct((batch * seq, D_DIL), BF16), stat_shape]
    bias_scratch = pltpu.VMEM((2, N_HEADS_DIL, DIL_BLOCK, 2 * DIL_BLOCK), F32)

    if dil == 1:
        rest, mem, mem_norm_w, w_kv = mem_args
        rows = nu * DIL_BLOCK
        steps = nb // nu
        cur = lambda col: pl.BlockSpec((None, None, rows, D_DIL), lambda b, l: (b, 0, l, col))
        prev = lambda col: pl.BlockSpec((None, None, DIL_BLOCK, D_DIL),
                                        lambda b, l: (b, 0, jnp.maximum(nu * l - 1, 0), col))
        out = lambda width, col=0: pl.BlockSpec((rows, width), lambda b, l: (b * steps + l, col // width))
        mem_specs = [out(D_MEM, COL_MQ), pl.BlockSpec((1, MEM_LEN, D_MODEL), lambda b, l: (b, 0, 0)),
                     const(mem_norm_w), const(w_kv)]
        num, stats, mem_num, mem_den = pl.pallas_call(
            _dil1_kernel,
            grid=(batch, steps),
            in_specs=[cur(0), prev(1), cur(1), prev(2), cur(2)] + [const(a) for a in consts] + mem_specs,
            out_specs=[out(D_DIL), out(LANES), out(D_MEM), out(LANES)],
            out_shape=out_shape + [jax.ShapeDtypeStruct((batch * seq, D_MEM), BF16), stat_shape],
            scratch_shapes=[bias_scratch, pltpu.VMEM((MEM_LEN, 2 * D_MEM), BF16)],
            compiler_params=_cparams(("arbitrary", "arbitrary")),
            name="dilated_d1",
        )(src, src, src, src, src, *consts, rest, mem, mem_norm_w, w_kv)
        return (num, stats), (mem_num, mem_den)

    nl = DIL_LBLOCKS[dil]
    steps = nb // nl
    rows = nl * DIL_BLOCK * dil
    cur = lambda col: pl.BlockSpec((None, nu, nl * DIL_BLOCK, D_DIL), lambda b, l, g: (b, g, l, col))
    prev = lambda col: pl.BlockSpec((None, nu, DIL_BLOCK, D_DIL),
                                    lambda b, l, g: (b, g, jnp.maximum(nl * l - 1, 0), col))
    out = lambda width: pl.BlockSpec((rows, width), lambda b, l, g: (b * steps + l, 0))
    whole_seq = steps == 1
    qkv_specs = [cur(0), cur(1), cur(2)] if whole_seq else [cur(0), prev(1), cur(1), prev(2), cur(2)]
    return pl.pallas_call(
        functools.partial(_dil_kernel, dil=dil, whole_seq=whole_seq),
        grid=(batch, steps, dil // nu),
        in_specs=qkv_specs + [const(a) for a in consts],
        out_specs=[out(D_DIL), out(LANES)],
        out_shape=out_shape,
        scratch_shapes=[bias_scratch, pltpu.VMEM((N_HEADS_DIL // 2 + 1, rows, LANES), F32),
                        pltpu.VMEM((N_HEADS_DIL // 2 + 1, 4, 4 * dil, LANES), F32)],
        compiler_params=_cparams(("arbitrary", "arbitrary", "arbitrary")),
        name=f"dilated_d{dil}",
    )(*([src] * len(qkv_specs)), *consts)


GLA_TM = 256
GLA_STEP_TILES = 2


GLA_NC = GLA_TM // GLA_CHUNK


def _gla_tile(bb, rows, w2, q_ref, k_ref, v_ref, g_ref, bg_ref, tri_ref, o_ref, st_ref):
    g = g_ref[bb, rows, :LANES]
    z = jnp.dot(g, w2, preferred_element_type=F32) + bg_ref[...]
    la = (jnp.minimum(z, 0.0) * (LOG2E / GLA_TAU)
          - jnp.log2(1.0 + jnp.exp2(jnp.abs(z) * -LOG2E)) * (1.0 / GLA_TAU))
    la_h = la.astype(BF16)
    la_l = (la - la_h.astype(F32)).astype(BF16)

    tri = tri_ref[...]
    b = jnp.dot(tri, la_h, preferred_element_type=F32) + jnp.dot(tri, la_l, preferred_element_type=F32)
    b_last = jnp.concatenate(
        [jnp.broadcast_to(b[(ci + 1) * GLA_CHUNK - 1:(ci + 1) * GLA_CHUNK, :], (GLA_CHUNK, D_GLA_K))
         for ci in range(GLA_NC)], axis=0)

    causal = tri > 0
    t_chunk = lax.broadcasted_iota(jnp.int32, (GLA_DK, GLA_TM), 1) // GLA_CHUNK
    for h in range(N_HEADS_GLA):
        cs = slice(h * GLA_DK, (h + 1) * GLA_DK)
        vs = slice(h * GLA_DV, (h + 1) * GLA_DV)
        v_h = v_ref[bb, rows, vs]
        b_h, b_last_h = b[:, cs], b_last[:, cs]
        q = q_ref[bb, rows, cs].astype(F32)
        k = k_ref[bb, rows, cs].astype(F32)
        q_t = (q * jnp.exp2(b_h)).astype(BF16)
        k_t = (k * jnp.exp2(-b_h)).astype(BF16)
        k_s = k * jnp.exp2(b_last_h - b_h)
        a = lax.dot_general(q_t, k_t, (((1,), (1,)), ((), ())), preferred_element_type=F32)
        o = jnp.dot(jnp.where(causal, a.astype(BF16), jnp.zeros_like(tri)), v_h, preferred_element_type=F32)
        k_s_t = k_s.T.astype(BF16)
        zero = jnp.zeros_like(k_s_t)
        kv_all = jnp.dot(jnp.concatenate([jnp.where(t_chunk == ci, k_s_t, zero) for ci in range(GLA_NC)], axis=0),
                         v_h, preferred_element_type=F32)
        decay_t = jnp.exp2(b_last_h.T)
        st = st_ref[bb, h]
        outs = []
        for ci in range(GLA_NC):
            rs = slice(ci * GLA_CHUNK, (ci + 1) * GLA_CHUNK)
            o_c = o[rs] + jnp.dot(q_t[rs], st.astype(BF16), preferred_element_type=F32)
            decay = jnp.broadcast_to(decay_t[:, ci * GLA_CHUNK:ci * GLA_CHUNK + 1], (GLA_DK, GLA_DV))
            st = decay * st + kv_all[ci * GLA_DK:(ci + 1) * GLA_DK]
            ms = jnp.mean(o_c * o_c, axis=-1, keepdims=True)
            outs.append((o_c * lax.rsqrt(ms + EPS)).astype(o_ref.dtype))
        st_ref[bb, h] = st
        o_ref[bb, rows, vs] = jnp.concatenate(outs, axis=0)


def _gla_kernel(q_ref, k_ref, v_ref, g_ref, w2_ref, bg_ref, nw_ref, tri_ref, wout_ref,
                o_ref, wout_bf_ref, st_ref):

    @pl.when(pl.program_id(0) == 0)
    def _():
        st_ref[...] = jnp.zeros_like(st_ref)

    i = pl.program_id(0)
    first_gla = D_DIL // GLA_DV
    is_gla = (i >= first_gla) & (i < first_gla + N_HEADS_GLA)
    wout_bf_ref[...] = (wout_ref[...] * jnp.where(is_gla, nw_ref[...], 1.0)).astype(wout_bf_ref.dtype)

    w2 = jnp.concatenate([w2_ref[...].astype(BF16), jnp.zeros((LANES - GLA_GATE_RANK, D_GLA_K), BF16)], axis=0)
    for tt in range(GLA_STEP_TILES):
        rows = slice(tt * GLA_TM, (tt + 1) * GLA_TM)
        for bb in range(st_ref.shape[0]):
            _gla_tile(bb, rows, w2, q_ref, k_ref, v_ref, g_ref, bg_ref, tri_ref, o_ref, st_ref)


def _gla(proj, w2, b_gate, norm_w, w_out, batch, seq):
    step_rows = GLA_TM * GLA_STEP_TILES
    steps = seq // step_rows
    wout_rows = w_out.shape[0] // steps
    assert wout_rows == GLA_DV and D_DIL % GLA_DV == 0
    pos = np.arange(GLA_TM)
    tri = jnp.asarray((pos[:, None] // GLA_CHUNK == pos[None, :] // GLA_CHUNK) & (pos[None, :] <= pos[:, None]),
                      BF16)
    proj3 = proj.reshape(batch, seq, REST_W)
    rows = lambda width, col: pl.BlockSpec((batch, step_rows, width), lambda i: (0, i, col // width))
    const = lambda shape: pl.BlockSpec(shape, lambda i: (0, 0))
    wout_spec = pl.BlockSpec((wout_rows, w_out.shape[1]), lambda i: (i, 0))
    out, w_out_bf = pl.pallas_call(
        _gla_kernel,
        grid=(steps,),
        in_specs=[
            rows(D_GLA_K, COL_GQ), rows(D_GLA_K, COL_GK), rows(D_GLA_V, COL_GV), rows(GLR_PAD, COL_GLR),
            const((GLA_GATE_RANK, D_GLA_K)), const((1, D_GLA_K)), const((GLA_DV, 1)),
            const((GLA_TM, GLA_TM)), wout_spec,
        ],
        out_specs=[rows(D_GLA_V, 0), wout_spec],
        out_shape=[jax.ShapeDtypeStruct((batch, seq, D_GLA_V), BF16), jax.ShapeDtypeStruct(w_out.shape, BF16)],
        scratch_shapes=[pltpu.VMEM((batch, N_HEADS_GLA, GLA_DK, GLA_DV), F32)],
        compiler_params=_cparams(("arbitrary",)),
        name="gla",
    )(proj3, proj3, proj3, proj3, w2, b_gate, norm_w.reshape(GLA_DV, 1), tri, w_out)
    return out.reshape(batch * seq, D_GLA_V), w_out_bf


OUT_TM = 512


def _silu(g):
    h = g * 0.5
    return h * jnp.tanh(h) + h


OUT_GW = 256


def _out_kernel(n1_ref, n2_ref, n3_ref, s1_ref, s2_ref, s3_ref, og_ref, nm_ref, dm_ref, ga_ref, gg_ref, gm_ref,
                sm_ref, ex_ref, exm_ref, w_ref, x_ref, nw_ref, out_ref, mix_ref):
    keep = sm_ref[0:1, :]
    fill = sm_ref[1:2, :]
    stats = [s_ref[...] for s_ref in (s1_ref, s2_ref, s3_ref)]
    m1, m2, m3 = [st * keep for st in stats]
    d1, d2, d3 = [pltpu.roll(st, LANES - STAT_DEN_SHIFT, 1) * keep + fill for st in stats]
    mx = jnp.maximum(jnp.maximum(m1, m2), m3)
    t1, t2, t3 = jnp.exp2(m1 - mx), jnp.exp2(m2 - mx), jnp.exp2(m3 - mx)
    inv = 1.0 / (t1 * d1 + t2 * d2 + t3 * d3)
    wts = [(t * inv).astype(BF16) for t in (t1, t2, t3)]
    inv_m = (1.0 / dm_ref[...]).astype(BF16)

    def gate_chunk(lo):
        cols = slice(lo, lo + OUT_GW)
        if lo < D_DIL:
            a = jnp.zeros((OUT_TM, OUT_GW), F32)
            for wt, n_ref in zip(wts, (n1_ref, n2_ref, n3_ref)):
                a = a + jnp.dot(wt, ex_ref[:, cols], preferred_element_type=F32) * n_ref[:, cols].astype(F32)
            g = ga_ref[:, cols]
        elif lo < D_DIL + D_GLA_V:
            lc = slice(lo - D_DIL, lo - D_DIL + OUT_GW)
            a, g = og_ref[:, lc], gg_ref[:, lc]
        else:
            a = jnp.dot(inv_m, exm_ref[...], preferred_element_type=F32) * nm_ref[...].astype(F32)
            g = gm_ref[...]
        mix_ref[:, cols] = a.astype(BF16) * _silu(g)

    for lo in range(0, D_MIX, OUT_GW):
        gate_chunk(lo)
    y = jnp.dot(mix_ref[...], w_ref[...], preferred_element_type=F32)
    ms = jnp.mean(y * y, axis=-1, keepdims=True)
    out_ref[...] = x_ref[...] + y * lax.rsqrt(ms + EPS) * nw_ref[...]


def _out_proj(pats, o_gla, mem_pair, proj, stat_mask, expand, expand_mem, w_out, x2d, norm_w):
    m = x2d.shape[0]
    rows = lambda width, col=0: pl.BlockSpec((OUT_TM, width), lambda s: (s, col // width))
    const = lambda shape: pl.BlockSpec(shape, lambda s: (0, 0), pipeline_mode=pl.Buffered(1))
    nums, stats = zip(*pats)
    return pl.pallas_call(
        _out_kernel,
        grid=(m // OUT_TM,),
        in_specs=[
            *([rows(D_DIL)] * 3), *([rows(LANES)] * 3),
            rows(D_GLA_V), rows(D_MEM), rows(LANES),
            rows(D_DIL, COL_GATE_A), rows(D_GLA_V, COL_GATE_G), rows(D_MEM, COL_GATE_M),
            const((F32_SUBLANES, LANES)), const((LANES, D_DIL)), const((LANES, D_MEM)), const((D_MIX, D_MODEL)),
            rows(D_MODEL), const((1, D_MODEL)),
        ],
        out_specs=rows(D_MODEL),
        out_shape=jax.ShapeDtypeStruct((m, D_MODEL), x2d.dtype),
        scratch_shapes=[pltpu.VMEM((OUT_TM, D_MIX), BF16)],
        compiler_params=_cparams(("parallel",)),
        name="out_proj",
    )(*nums, *stats, o_gla, *mem_pair, proj, proj, proj, stat_mask, expand, expand_mem, w_out, x2d, norm_w)


D_IN_PROJ = 3 * D_DIL + 2 * D_GLA_K + D_GLA_V + GLA_GATE_RANK + D_MEM + D_MIX
SRC_GLR = 3 * D_DIL + 2 * D_GLA_K + D_GLA_V
SRC_MQ = SRC_GLR + GLA_GATE_RANK
SRC_GATE = SRC_MQ + D_MEM
SCALE_ATTN_Q = HEAD_DIM ** -0.5 * LOG2E
SCALE_GLA_Q = GLA_DK ** -0.5
QKV_SEGMENTS = (
    (0, D_DIL, 0, SCALE_ATTN_Q),
    (D_DIL, 2 * D_DIL, D_DIL, 1.0),
)
REST_SEGMENTS = (
    (QKV_W, D_GLA_K, COL_GQ, SCALE_GLA_Q),
    (QKV_W + D_GLA_K, D_GLA_K + D_GLA_V, COL_GK, 1.0),
    (SRC_GATE + D_DIL, D_GLA_V, COL_GATE_G, 1.0),
    (SRC_MQ, D_MEM, COL_MQ, SCALE_ATTN_Q),
    (SRC_GATE + D_DIL + D_GLA_V, D_MEM, COL_GATE_M, 1.0),
    (SRC_GATE, D_DIL, COL_GATE_A, 1.0),
)
WPREP_TK = 512


def _wprep_kernel(wt_ref, qkv_ref, rest_ref):
    for o_ref, segments in ((qkv_ref, QKV_SEGMENTS), (rest_ref, REST_SEGMENTS)):
        for src, width, dst, scale in segments:
            for lo in range(0, width, LANES):
                w = wt_ref[src + lo:src + lo + LANES, :].T
                if scale != 1.0:
                    w = w * scale
                o_ref[:, dst + lo:dst + lo + LANES] = w.astype(o_ref.dtype)
    glr = wt_ref[SRC_GLR:SRC_GLR + LANES, :].T
    keep = lax.broadcasted_iota(jnp.int32, glr.shape, 1) < GLA_GATE_RANK
    rest_ref[:, COL_GLR:COL_GLR + LANES] = jnp.where(keep, glr, 0.0).astype(rest_ref.dtype)
    rest_ref[:, COL_GLR + LANES:COL_GLR + GLR_PAD] = jnp.zeros((WPREP_TK, GLR_PAD - LANES), rest_ref.dtype)


def _regroup_in_weights(w_in):
    return pl.pallas_call(
        _wprep_kernel,
        grid=(D_MODEL // WPREP_TK,),
        in_specs=[pl.BlockSpec((D_IN_PROJ, WPREP_TK), lambda i: (0, i))],
        out_specs=[pl.BlockSpec((WPREP_TK, QKV_W), lambda i: (i, 0)),
                   pl.BlockSpec((WPREP_TK, REST_W), lambda i: (i, 0))],
        out_shape=[jax.ShapeDtypeStruct((D_MODEL, QKV_W), BF16), jax.ShapeDtypeStruct((D_MODEL, REST_W), BF16)],
        compiler_params=_cparams(("parallel",)),
        name="weight_regroup",
    )(w_in.T)


def kernel(x, mem, norm_pre_w, w_in, rel_bias, w_gla_gate2, b_gla_gate, gla_norm_w, mem_norm_w, w_mem_kv,
           w_out, norm_post_w):
    batch, seq, _ = x.shape
    depth = w_in.shape[0]
    expand_np = np.zeros((LANES, D_DIL), np.float32)
    for h in range(N_HEADS_DIL):
        expand_np[_stat_lane(h), h * HEAD_DIM:(h + 1) * HEAD_DIM] = 1.0
    expand = jnp.asarray(expand_np, BF16)
    expand_mem = jnp.asarray(expand_np[:, :D_MEM], BF16)
    stat_mask_np = np.zeros((F32_SUBLANES, LANES), np.float32)
    stat_mask_np[0, [_stat_lane(h) for h in range(N_HEADS_DIL)]] = 1.0
    stat_mask_np[1] = 1.0 - stat_mask_np[0]
    stat_mask = jnp.asarray(stat_mask_np)
    rb_t = jnp.zeros((2 * F32_SUBLANES, LANES), F32).at[:N_HEADS_DIL, :REL_BUCKETS].set(rel_bias.astype(F32).T)
    for l in range(depth):
        x2d = x.reshape(batch * seq, D_MODEL)
        w_qkv, w_rest = _regroup_in_weights(w_in[l])
        h, qkv1, qkv4, qkv16 = _qkv_proj(x2d, norm_pre_w[l][None], w_qkv, batch, seq)
        rest = _rest_proj(h, w_rest)

        srcs = (qkv1.reshape(batch, 1, seq, QKV_W), qkv4, qkv16)
        mem_args = (rest, mem, mem_norm_w[l][None], w_mem_kv[l])
        pats = [_dilated(src, rb_t, batch, seq, window, dil, mem_args if dil == 1 else None)
                for (window, dil), src in zip(DIL_PATTERNS, srcs)]
        pats[0], mem_pair = pats[0]

        o_gla, w_out_bf = _gla(rest, w_gla_gate2[l], b_gla_gate[l][None], gla_norm_w[l][None], w_out[l],
                               batch, seq)

        out = _out_proj(pats, o_gla, mem_pair, rest, stat_mask, expand, expand_mem, w_out_bf, x2d,
                        norm_post_w[l][None])
        x = out.reshape(batch, seq, D_MODEL)
    return x
```
